```python
import math, functools
import jax, jax.numpy as jnp
from jax import lax
import numpy as np

D_MODEL = 1024
BATCH = 2
SEQ = 8192
DEPTH = 1
DEC_BATCH = 128
DEC_SEQ = 8
PAST_LEN = 16384
PAGE_SIZE = 128

M_HEADS = 4
M_DK = 64
M_DV = 128
M_CHUNK = 64
M_QK_W = M_HEADS * M_DK
M_V_W = M_HEADS * M_DV
FORGET_BIAS_MEAN = 3.0
A_HEADS = 8
A_KV_HEADS = 2
A_HEAD_DIM = 64
A_GROUPS = A_HEADS // A_KV_HEADS
A_Q_W = A_HEADS * A_HEAD_DIM
A_KV_W = A_KV_HEADS * A_HEAD_DIM
WINDOW = 128
D_FF = -(-8 * D_MODEL // (3 * 256)) * 256
EPS = 1e-6
IN_SIZES = (M_QK_W, M_QK_W, M_V_W, M_V_W, M_HEADS, M_HEADS, A_Q_W, A_KV_W, A_KV_W, D_MODEL, D_MODEL)
D_IN = sum(IN_SIZES)

kernel_name = "hybrid_mlstm_swa_sink_decoder_step"


def rms_norm(x, w):
    xf = x.astype(jnp.float32)
    y = xf * lax.rsqrt(jnp.mean(xf * xf, axis=-1, keepdims=True) + EPS)
    return (y * w.astype(jnp.float32)).astype(x.dtype)


def mlstm_chunkwise(q, k, v, i_pre, f_pre, C0, n0, m0):
    f32 = jnp.float32
    B, T = q.shape[0], q.shape[1]
    L = math.gcd(T, M_CHUNK)
    nc = T // L

    def chunks(a):
        a = a.astype(f32).reshape((B, nc, L) + a.shape[2:])
        return jnp.moveaxis(jnp.moveaxis(a, 1, 0), 3, 2)

    qc = chunks(q)
    kc = chunks(k) * (M_DK ** -0.5)
    vc = chunks(v)
    ic = chunks(i_pre)
    bc = jnp.cumsum(jax.nn.log_sigmoid(chunks(f_pre)), axis=-1)
    causal = jnp.tril(jnp.ones((L, L), dtype=bool))

    def step(carry, xs):
        C, n, m = carry
        q_, k_, v_, i_, b_ = xs
        logw = jnp.where(causal, b_[..., :, None] - b_[..., None, :] + i_[..., None, :], -jnp.inf)
        inter = b_ + m[..., None]
        m_t = jnp.maximum(inter, logw.max(axis=-1))
        w_inter = jnp.exp(inter - m_t)
        w = jnp.exp(logw - m_t[..., None])
        s = jnp.einsum('bhtd,bhsd->bhts', q_, k_) * w
        num = w_inter[..., None] * jnp.einsum('bhtd,bhde->bhte', q_, C) + jnp.einsum('bhts,bhse->bhte', s, v_)
        den = w_inter * jnp.einsum('bhtd,bhd->bht', q_, n) + s.sum(axis=-1)
        h = num / jnp.maximum(jnp.abs(den), jnp.exp(-m_t))[..., None]
        w_last = w[..., -1, :]
        g = w_inter[..., -1]
        C = g[..., None, None] * C + jnp.einsum('bhs,bhsd,bhse->bhde', w_last, k_, v_)
        n = g[..., None] * n + jnp.einsum('bhs,bhsd->bhd', w_last, k_)
        return (C, n, m_t[..., -1]), h

    (C, n, m), h = lax.scan(step, (C0.astype(f32), n0.astype(f32), m0.astype(f32)), (qc, kc, vc, ic, bc))
    h = jnp.moveaxis(jnp.moveaxis(h, 2, 3), 0, 1).reshape(B, T, M_HEADS, M_DV)
    return h, C, n, m


def sink_softmax(scores, mask, sinks):
    s = jnp.where(mask, scores, -jnp.inf)
    sk = sinks.astype(jnp.float32).reshape(A_KV_HEADS, A_GROUPS)[:, :, None, None]
    mx = jnp.maximum(s.max(axis=-1, keepdims=True), sk)
    p = jnp.exp(s - mx)
    return p / (p.sum(axis=-1, keepdims=True) + jnp.exp(sk - mx))


def swa_prompt(q, k, v, sinks):
    f32 = jnp.float32
    B, T = q.shape[0], q.shape[1]
    nb = T // WINDOW
    qb = q.astype(f32).reshape(B, nb, WINDOW, A_KV_HEADS, A_GROUPS, A_HEAD_DIM)
    kb = k.astype(f32).reshape(B, nb, WINDOW, A_KV_HEADS, A_HEAD_DIM)
    vb = v.astype(f32).reshape(B, nb, WINDOW, A_KV_HEADS, A_HEAD_DIM)

    def with_prev(a):
        prev = jnp.pad(a, ((0, 0), (1, 0), (0, 0), (0, 0), (0, 0)))[:, :-1]
        return jnp.concatenate([prev, a], axis=2)

    kk, vv = with_prev(kb), with_prev(vb)
    scores = jnp.einsum('bnqkgd,bnskd->bnkgqs', qb, kk) * (A_HEAD_DIM ** -0.5)
    dist = jnp.arange(WINDOW)[:, None] + WINDOW - jnp.arange(2 * WINDOW)[None, :]
    local = (dist >= 0) & (dist < WINDOW)
    valid = (jnp.arange(nb)[:, None, None] > 0) | (jnp.arange(2 * WINDOW)[None, None, :] >= WINDOW)
    mask = local[None] & valid
    p = sink_softmax(scores, mask[None, :, None, None], sinks)
    out = jnp.einsum('bnkgqs,bnskd->bnqkgd', p, vv).reshape(B, T, A_Q_W)
    wb = min(WINDOW, T)
    return out.astype(q.dtype), k[:, T - wb:], v[:, T - wb:]


def swa_sample(q, k, v, sinks, k_buf, v_buf):
    f32 = jnp.float32
    B, T = q.shape[0], q.shape[1]
    wb = k_buf.shape[1]
    kc = jnp.concatenate([k_buf.astype(k.dtype), k], axis=1)
    vc = jnp.concatenate([v_buf.astype(v.dtype), v], axis=1)
    qg = q.astype(f32).reshape(B, T, A_KV_HEADS, A_GROUPS, A_HEAD_DIM)
    scores = jnp.einsum('btkgd,bskd->bkgts', qg, kc.astype(f32)) * (A_HEAD_DIM ** -0.5)
    dist = jnp.arange(T)[:, None] + wb - jnp.arange(wb + T)[None, :]
    mask = (dist >= 0) & (dist < WINDOW)
    p = sink_softmax(scores, mask, sinks)
    out = jnp.einsum('bkgts,bskd->btkgd', p, vc.astype(f32)).reshape(B, T, A_Q_W)
    return out.astype(q.dtype), kc[:, T:], vc[:, T:]


def layer(x, C0, n0, m0, attn, norm_mix_w, w_in, i_bias, f_bias, m_norm_w, q_norm_w, k_norm_w, sinks,
          w_branch_a, w_branch_b, w_out, norm_ffn_w, w_gate, w_up, w_down):
    B, T = x.shape[0], x.shape[1]
    hn = rms_norm(x, norm_mix_w)
    proj = hn @ w_in
    idx = np.cumsum(IN_SIZES)[:-1].tolist()
    q_m, k_m, v_m, o_m, i_m, f_m, q_a, k_a, v_a, g_a, g_b = jnp.split(proj, idx, axis=-1)

    def heads(a, h):
        return a.reshape(B, T, h, -1)

    h_m, C, n, m = mlstm_chunkwise(heads(q_m, M_HEADS), heads(k_m, M_HEADS), heads(v_m, M_HEADS),
                                   i_m + i_bias, f_m + f_bias, C0, n0, m0)
    h_m = rms_norm(h_m, m_norm_w).reshape(B, T, M_V_W)
    h_m = (h_m * jax.nn.sigmoid(o_m.astype(jnp.float32))).astype(x.dtype)
    qa = rms_norm(heads(q_a, A_HEADS), q_norm_w)
    ka = rms_norm(heads(k_a, A_KV_HEADS), k_norm_w)
    h_a, k_win, v_win = attn(qa, ka, heads(v_a, A_KV_HEADS), sinks)
    mix = jax.nn.sigmoid(g_a) * (h_m @ w_branch_a) + jax.nn.sigmoid(g_b) * (h_a @ w_branch_b)
    x = x + mix @ w_out
    hf = rms_norm(x, norm_ffn_w)
    x = x + (jax.nn.silu(hf @ w_gate) * (hf @ w_up)) @ w_down
    return x, C, n, m, k_win, v_win


def setup_inputs(seed: int = 0) -> dict:
    key = jax.random.key(seed)
    ks = jax.random.split(key, 24)
    f32 = jnp.float32
    wb = min(WINDOW, PAST_LEN)
    nrm = lambda k, shape, scale: jax.random.normal(k, shape, f32) * scale
    return {
        'x_prompt': nrm(ks[0], (BATCH, SEQ, D_MODEL), 1.0),
        'x_sample': nrm(ks[1], (DEC_BATCH, DEC_SEQ, D_MODEL), 1.0),
        'state_mlstm_C': nrm(ks[2], (DEPTH, DEC_BATCH, M_HEADS, M_DK, M_DV), 0.5),
        'state_mlstm_n': nrm(ks[3], (DEPTH, DEC_BATCH, M_HEADS, M_DK), 0.5),
        'state_mlstm_m': nrm(ks[4], (DEPTH, DEC_BATCH, M_HEADS), 1.0),
        'cache_swa_k': nrm(ks[5], (DEPTH, DEC_BATCH, wb, A_KV_HEADS, A_HEAD_DIM), 1.0),
        'cache_swa_v': nrm(ks[6], (DEPTH, DEC_BATCH, wb, A_KV_HEADS, A_HEAD_DIM), 1.0),
        'norm_mix_w': 1.0 + nrm(ks[7], (DEPTH, D_MODEL), 0.02),
        'w_in': nrm(ks[8], (DEPTH, D_MODEL, D_IN), D_MODEL ** -0.5),
        'mlstm_i_bias': nrm(ks[9], (DEPTH, M_HEADS), 0.1),
        'mlstm_f_bias': FORGET_BIAS_MEAN + nrm(ks[10], (DEPTH, M_HEADS), 0.5),
        'mlstm_norm_w': 1.0 + nrm(ks[11], (DEPTH, M_HEADS, M_DV), 0.02),
        'q_norm_w': 1.0 + nrm(ks[12], (DEPTH, A_HEAD_DIM), 0.02),
        'k_norm_w': 1.0 + nrm(ks[13], (DEPTH, A_HEAD_DIM), 0.02),
        'attn_sinks': nrm(ks[14], (DEPTH, A_HEADS), 0.5),
        'w_branch_a': nrm(ks[15], (DEPTH, M_V_W, D_MODEL), M_V_W ** -0.5),
        'w_branch_b': nrm(ks[16], (DEPTH, A_Q_W, D_MODEL), A_Q_W ** -0.5),
        'w_out': nrm(ks[17], (DEPTH, D_MODEL, D_MODEL), D_MODEL ** -0.5),
        'norm_ffn_w': 1.0 + nrm(ks[18], (DEPTH, D_MODEL), 0.02),
        'w_gate': nrm(ks[19], (DEPTH, D_MODEL, D_FF), D_MODEL ** -0.5),
        'w_up': nrm(ks[20], (DEPTH, D_MODEL, D_FF), D_MODEL ** -0.5),
        'w_down': nrm(ks[21], (DEPTH, D_FF, D_MODEL), D_FF ** -0.5),
    }


def reference(x_prompt, x_sample, state_mlstm_C, state_mlstm_n, state_mlstm_m, cache_swa_k, cache_swa_v,
              norm_mix_w, w_in, mlstm_i_bias, mlstm_f_bias, mlstm_norm_w, q_norm_w, k_norm_w, attn_sinks,
              w_branch_a, w_branch_b, w_out, norm_ffn_w, w_gate, w_up, w_down):
    f32 = jnp.float32
    bp = x_prompt.shape[0]
    yp, ys = x_prompt, x_sample
    Cp_l, np_l, mp_l, kp_l, vp_l = [], [], [], [], []
    Cs_l, ns_l, ms_l, ks_l, vs_l = [], [], [], [], []
    for l in range(DEPTH):
        w = (norm_mix_w[l], w_in[l], mlstm_i_bias[l], mlstm_f_bias[l], mlstm_norm_w[l], q_norm_w[l],
             k_norm_w[l], attn_sinks[l], w_branch_a[l], w_branch_b[l], w_out[l], norm_ffn_w[l],
             w_gate[l], w_up[l], w_down[l])
        C0 = jnp.zeros((bp, M_HEADS, M_DK, M_DV), f32)
        n0 = jnp.zeros((bp, M_HEADS, M_DK), f32)
        m0 = jnp.zeros((bp, M_HEADS), f32)
        yp, Cp, np_, mp, kp, vp = layer(yp, C0, n0, m0, swa_prompt, *w)
        attn_s = functools.partial(swa_sample, k_buf=cache_swa_k[l], v_buf=cache_swa_v[l])
        ys, Cs, ns, ms, kS, vS = layer(ys, state_mlstm_C[l], state_mlstm_n[l], state_mlstm_m[l], attn_s, *w)
        Cp_l.append(Cp); np_l.append(np_); mp_l.append(mp); kp_l.append(kp); vp_l.append(vp)
        Cs_l.append(Cs); ns_l.append(ns); ms_l.append(ms); ks_l.append(kS); vs_l.append(vS)
    return (yp, ys,
            jnp.stack(Cp_l), jnp.stack(np_l), jnp.stack(mp_l), jnp.stack(kp_l), jnp.stack(vp_l),
            jnp.stack(Cs_l), jnp.stack(ns_l), jnp.stack(ms_l), jnp.stack(ks_l), jnp.stack(vs_l))
```

```python
import functools

import jax
import jax.numpy as jnp
from jax import lax
from jax.experimental import pallas as pl
from jax.experimental.pallas import tpu as pltpu

F32 = jnp.float32
BF16 = jnp.bfloat16

D_MODEL = 1024
M_HEADS = 4
M_DK = 64
M_DV = 128
M_CHUNK = 64
M_QK_W = M_HEADS * M_DK
M_V_W = M_HEADS * M_DV
A_HEADS = 8
A_KV_HEADS = 2
A_HEAD_DIM = 64
A_GROUPS = A_HEADS // A_KV_HEADS
A_Q_W = A_HEADS * A_HEAD_DIM
A_KV_W = A_KV_HEADS * A_HEAD_DIM
WINDOW = 128
D_FF = 2816
EPS = 1e-6

LANES = 128
GROUP = 128
S_W = 2 * M_DV
VMEM_LIMIT = 56 * 1024 * 1024

NT_DIMS = (((1,), (1,)), ((), ()))
TN_DIMS = (((0,), (0,)), ((), ()))


def _dot(a, b):
    return jnp.dot(a, b, preferred_element_type=F32)


def _dot_nt(a, b):
    return lax.dot_general(a, b, NT_DIMS, preferred_element_type=F32)


def _dot_tn(a, b):
    return lax.dot_general(a, b, TN_DIMS, preferred_element_type=F32)


def _const_spec(shape):
    nd = len(shape)
    return pl.BlockSpec(shape, lambda *_: (0,) * nd, pipeline_mode=pl.Buffered(1))


def _params(sem):
    return pltpu.CompilerParams(dimension_semantics=sem, vmem_limit_bytes=VMEM_LIMIT)


def _proj_kernel(x_ref, nw_ref, w1_ref, wif_ref, w2_ref,
                 qkv_ref, o_ref, g_ref, qa_ref, kva_ref, gab_ref):
    x = x_ref[...]
    ms = jnp.mean(x * x, axis=-1, keepdims=True)
    hn = ((x * lax.rsqrt(ms + EPS)) * nw_ref[...]).astype(BF16)
    qkv_ref[...] = _dot(hn, w1_ref[:, 0:2 * M_QK_W + M_V_W]).astype(qkv_ref.dtype)
    o_ref[...] = _dot(hn, w1_ref[:, 2 * M_QK_W + M_V_W:]).astype(o_ref.dtype)
    g_ref[...] = _dot(hn, wif_ref[...])
    qa_ref[...] = _dot(hn, w2_ref[:, 0:A_Q_W]).astype(qa_ref.dtype)
    kva_ref[...] = _dot(hn, w2_ref[:, A_Q_W:A_Q_W + 2 * A_KV_W])
    gab_ref[...] = _dot(hn, w2_ref[:, A_Q_W + 2 * A_KV_W:]).astype(gab_ref.dtype)


def _proj(x2d, nw, w1, wif, w2, qkv_dtype, tm):
    n = x2d.shape[0]
    row = lambda w: pl.BlockSpec((tm, w), lambda i: (i, 0))
    w_qkv = 2 * M_QK_W + M_V_W
    return pl.pallas_call(
        _proj_kernel,
        grid=(n // tm,),
        in_specs=[row(D_MODEL), _const_spec(nw.shape), _const_spec(w1.shape),
                  _const_spec(wif.shape), _const_spec(w2.shape)],
        out_specs=[row(w_qkv), row(M_V_W), row(LANES), row(A_Q_W), row(2 * A_KV_W), row(2 * D_MODEL)],
        out_shape=[jax.ShapeDtypeStruct((n, w_qkv), qkv_dtype),
                   jax.ShapeDtypeStruct((n, M_V_W), BF16),
                   jax.ShapeDtypeStruct((n, LANES), F32),
                   jax.ShapeDtypeStruct((n, A_Q_W), BF16),
                   jax.ShapeDtypeStruct((n, 2 * A_KV_W), F32),
                   jax.ShapeDtypeStruct((n, 2 * D_MODEL), BF16)],
        compiler_params=_params(("arbitrary",)),
        name="proj",
    )(x2d, nw, w1, wif, w2)


def _split3(x):
    hi = x.astype(BF16)
    r1 = x - hi.astype(F32)
    mid = r1.astype(BF16)
    lo = (r1 - mid.astype(F32)).astype(BF16)
    return hi, mid, lo


def _log_sigmoid(x):
    return jnp.minimum(x, 0.0) - jnp.log1p(jnp.exp(-jnp.abs(x)))


def _gate_prep(gb, chunk_shift):
    r = lax.broadcasted_iota(jnp.int32, (GROUP, GROUP), 0)
    c = lax.broadcasted_iota(jnp.int32, (GROUP, GROUP), 1)
    same = (r >> chunk_shift) == (c >> chunk_shift)
    causal = same & (c <= r)
    causal_t = same & (r <= c)
    lf = _log_sigmoid(gb)
    cm = jnp.where(causal, 1.0, 0.0).astype(BF16)
    b_col = sum(_dot(cm, p) for p in _split3(lf))
    gt8 = gb.T[0:8]
    lft8 = _log_sigmoid(gt8)
    cm_t = jnp.where(causal_t, 1.0, 0.0).astype(BF16)
    bt8 = sum(_dot(p, cm_t) for p in _split3(lft8))
    return same, causal, b_col, gt8, bt8


def _head_maxes(h, gb, b_col, gt8, bt8, same, causal):
    a_row = gt8[h:h + 1] - bt8[M_HEADS + h:M_HEADS + h + 1]
    bc = b_col[:, M_HEADS + h:M_HEADS + h + 1]
    a_col = gb[:, h:h + 1] - bc
    arow = jnp.broadcast_to(a_row, (GROUP, GROUP))
    run_max = jnp.max(jnp.where(causal, arow, -jnp.inf), axis=1, keepdims=True)
    chunk_max = jnp.max(jnp.where(same, arow, -jnp.inf), axis=1, keepdims=True)
    return arow, a_col, bc, run_max, chunk_max


def _head_weights(arow, a_col, bc, run_max, chunk_max, m_prev, causal):
    big_m = jnp.maximum(m_prev, run_max)
    m_last = jnp.maximum(m_prev, chunk_max)
    w = jnp.where(causal, jnp.exp(arow - big_m), 0.0)
    w_inter = jnp.exp(m_prev - big_m)
    g_vec = jnp.exp(m_prev - m_last)
    w_last = jnp.exp(a_col - m_last)
    m_t = bc + big_m
    return w, w_inter, g_vec, w_last, m_t


def _head_masks():
    lane_head = lax.broadcasted_iota(jnp.int32, (1, M_QK_W), 1) >> 6
    return [lane_head == h for h in range(M_HEADS)]


def _v_aug(v_all, h):
    ones_col = jnp.where(lax.broadcasted_iota(jnp.int32, (v_all.shape[0], M_DV), 1) == 0, 1.0, 0.0)
    return jnp.concatenate([v_all[:, h * M_DV:(h + 1) * M_DV], ones_col.astype(v_all.dtype)], axis=1)


def _head_out(out_aug, m_t, nw_h, o_h):
    num = out_aug[:, :M_DV]
    den = out_aug[:, M_DV:M_DV + 1]
    hh = num / jnp.maximum(jnp.abs(den), jnp.exp(-m_t))
    ms = jnp.mean(hh * hh, axis=-1, keepdims=True)
    hn = (hh * lax.rsqrt(ms + EPS)) * nw_h
    return hn * jax.nn.sigmoid(o_h.astype(F32))


def _mlstm_prompt_kernel(qkv_ref, g_ref, o_ref, bias_ref, nw_ref, s0_ref, m0_ref,
                         h_ref, s_ref, m_ref):
    step = pl.program_id(0)

    @pl.when(step == 0)
    def _():
        s_ref[...] = s0_ref[...]
        m_ref[...] = m0_ref[...]

    nb = qkv_ref.shape[0]
    hm = _head_masks()
    rows = lax.broadcasted_iota(jnp.int32, (GROUP, 1), 0)
    lane = lax.broadcasted_iota(jnp.int32, (8, LANES), 1)
    n_chunks = GROUP // M_CHUNK
    for b in range(nb):
        q_all = qkv_ref[b, :, 0:M_QK_W]
        k_all = qkv_ref[b, :, M_QK_W:2 * M_QK_W]
        v_all = qkv_ref[b, :, 2 * M_QK_W:]
        gb = g_ref[b] + bias_ref[...]
        same, causal, b_col, gt8, bt8 = _gate_prep(gb, 6)
        qf = q_all.astype(F32)
        ks = k_all.astype(F32) * (M_DK ** -0.5)
        q_stack = jnp.concatenate([jnp.where(hm[h], q_all, jnp.zeros_like(q_all)) for h in range(M_HEADS)], axis=0)
        sc = _dot_nt(q_stack, ks.astype(BF16))
        m_all = m_ref[b]
        per_head = []
        m_new_mat = jnp.zeros((8, LANES), F32)
        for h in range(M_HEADS):
            arow, a_col, bc, run_max, chunk_max = _head_maxes(h, gb, b_col, gt8, bt8, same, causal)
            m_c = m_all[0:1, h:h + 1]
            m_prev = jnp.broadcast_to(m_c, (GROUP, 1))
            for ci in range(1, n_chunks):
                last = ci * M_CHUNK - 1
                m_c = bc[last:last + 1] + jnp.maximum(m_c, chunk_max[last:last + 1])
                m_prev = jnp.where(rows >= ci * M_CHUNK, m_c, m_prev)
            w, w_inter, g_vec, w_last, m_t = _head_weights(arow, a_col, bc, run_max, chunk_max, m_prev, causal)
            va = _v_aug(v_all, h)
            s_h = (sc[h * GROUP:(h + 1) * GROUP] * w).astype(BF16)
            intra = _dot(s_h, va)
            per_head.append((w_inter, g_vec, w_last, m_t, va, intra))
            m_new_mat = jnp.where(lane == h, m_t[GROUP - 1:GROUP], m_new_mat)
        m_ref[b] = m_new_mat

        outs = [[] for _ in range(M_HEADS)]
        for ci in range(n_chunks):
            rs = slice(ci * M_CHUNK, (ci + 1) * M_CHUNK)
            s_old = s_ref[b]
            qw = jnp.concatenate(
                [(jnp.where(hm[h], qf[rs], 0.0) * per_head[h][0][rs]).astype(BF16) for h in range(M_HEADS)], axis=0)
            inter = _dot(qw, s_old.astype(BF16))
            d_s = sum(_dot_tn((jnp.where(hm[h], ks[rs], 0.0) * per_head[h][2][rs]).astype(BF16), per_head[h][4][rs])
                      for h in range(M_HEADS))
            for h in range(M_HEADS):
                hs = slice(h * M_DK, (h + 1) * M_DK)
                g = per_head[h][1][ci * M_CHUNK:ci * M_CHUNK + 1]
                s_ref[b, hs, :] = g * s_old[hs] + d_s[hs]
                outs[h].append(inter[h * M_CHUNK:(h + 1) * M_CHUNK] + per_head[h][5][rs])
        for h in range(M_HEADS):
            out_aug = jnp.concatenate(outs[h], axis=0)
            vs = slice(h * M_DV, (h + 1) * M_DV)
            h_ref[b, :, vs] = _head_out(out_aug, per_head[h][3], nw_ref[:, vs], o_ref[b, :, vs]).astype(h_ref.dtype)


def _mlstm_prompt(qkv, gates, o_m, bias, nw, s0, m0):
    nb, t = qkv.shape[0], qkv.shape[1]
    blk = lambda w: pl.BlockSpec((nb, GROUP, w), lambda i: (0, i, 0))
    full = lambda a: pl.BlockSpec(a.shape, lambda i: (0,) * a.ndim)
    return pl.pallas_call(
        _mlstm_prompt_kernel,
        grid=(t // GROUP,),
        in_specs=[blk(qkv.shape[2]), blk(LANES), blk(M_V_W), full(bias), full(nw), full(s0), full(m0)],
        out_specs=[blk(M_V_W), full(s0), full(m0)],
        out_shape=[jax.ShapeDtypeStruct((nb, t, M_V_W), BF16),
                   jax.ShapeDtypeStruct(s0.shape, F32),
                   jax.ShapeDtypeStruct(m0.shape, F32)],
        compiler_params=_params(("arbitrary",)),
        name="mlstm_prompt",
    )(qkv, gates, o_m, bias, nw, s0, m0)


def _mlstm_sample_kernel(seq_len, qkv_ref, g_ref, o_ref, mrep_ref, bias_ref, nw_ref, s0_ref,
                         h_ref, s_ref, mt_ref):
    n_seq = GROUP // seq_len
    shift = seq_len.bit_length() - 1
    hm = _head_masks()
    lane = lax.broadcasted_iota(jnp.int32, (GROUP, LANES), 1)
    q_all = qkv_ref[:, 0:M_QK_W]
    ks = qkv_ref[:, M_QK_W:2 * M_QK_W] * (M_DK ** -0.5)
    v_all = qkv_ref[:, 2 * M_QK_W:]
    gb = g_ref[...] + bias_ref[...]
    same, causal, b_col, gt8, bt8 = _gate_prep(gb, shift)
    q_bf = q_all.astype(BF16)
    q_stack = jnp.concatenate([jnp.where(hm[h], q_bf, jnp.zeros_like(q_bf)) for h in range(M_HEADS)], axis=0)
    sc = _dot_nt(q_stack, ks.astype(BF16))
    m_rep = mrep_ref[...]
    qw, kw, va, gv, intra, m_ts = [], [], [], [], [], []
    mt_mat = jnp.zeros((GROUP, LANES), F32)
    for h in range(M_HEADS):
        arow, a_col, bc, run_max, chunk_max = _head_maxes(h, gb, b_col, gt8, bt8, same, causal)
        w, w_inter, g_vec, w_last, m_t = _head_weights(arow, a_col, bc, run_max, chunk_max,
                                                       m_rep[:, h:h + 1], causal)
        va_h = _v_aug(v_all, h)
        s_h = (sc[h * GROUP:(h + 1) * GROUP] * w).astype(BF16)
        intra.append(_dot(s_h, va_h.astype(BF16)))
        qw.append(jnp.where(hm[h], q_all, 0.0) * w_inter)
        kw.append(jnp.where(hm[h], ks, 0.0) * w_last)
        va.append(va_h)
        gv.append(g_vec)
        m_ts.append(m_t)
        mt_mat = jnp.where(lane == h, m_t, mt_mat)
    mt_ref[...] = mt_mat

    zpad = jnp.zeros((16 - seq_len, M_QK_W), F32) if seq_len < 16 else None
    pad = (lambda x: jnp.concatenate([x, zpad], axis=0)) if zpad is not None else (lambda x: x)
    inter = [[] for _ in range(M_HEADS)]
    for s in range(n_seq):
        rs = slice(s * seq_len, (s + 1) * seq_len)
        s_old = s0_ref[s]
        qw_s = jnp.concatenate([pad(qw[h][rs]) for h in range(M_HEADS)], axis=0).astype(BF16)
        inter_s = _dot(qw_s, s_old.astype(BF16))
        rows_h = qw_s.shape[0] // M_HEADS
        d_s = sum(_dot_tn(pad(kw[h][rs]).astype(BF16), pad(va[h][rs]).astype(BF16)) for h in range(M_HEADS))
        for h in range(M_HEADS):
            hs = slice(h * M_DK, (h + 1) * M_DK)
            g = gv[h][s * seq_len:s * seq_len + 1]
            s_ref[s, hs, :] = g * s_old[hs] + d_s[hs]
            inter[h].append(inter_s[h * rows_h:h * rows_h + seq_len])
    for h in range(M_HEADS):
        out_aug = jnp.concatenate(inter[h], axis=0) + intra[h]
        vs = slice(h * M_DV, (h + 1) * M_DV)
        h_ref[:, vs] = _head_out(out_aug, m_ts[h], nw_ref[:, vs], o_ref[:, vs]).astype(h_ref.dtype)


def _mlstm_sample(qkv, gates, o_m, m_rep, bias, nw, s0, seq_len):
    n = qkv.shape[0]
    n_seq = GROUP // seq_len
    row = lambda w: pl.BlockSpec((GROUP, w), lambda i: (i, 0))
    full = lambda a: pl.BlockSpec(a.shape, lambda i: (0,) * a.ndim)
    st = pl.BlockSpec((n_seq,) + s0.shape[1:], lambda i: (i, 0, 0))
    return pl.pallas_call(
        functools.partial(_mlstm_sample_kernel, seq_len),
        grid=(n // GROUP,),
        in_specs=[row(qkv.shape[1]), row(LANES), row(M_V_W), row(LANES), full(bias), full(nw), st],
        out_specs=[row(M_V_W), st, row(LANES)],
        out_shape=[jax.ShapeDtypeStruct((n, M_V_W), BF16),
                   jax.ShapeDtypeStruct(s0.shape, F32),
                   jax.ShapeDtypeStruct((n, LANES), F32)],
        compiler_params=_params(("arbitrary",)),
        name="mlstm_sample",
    )(qkv, gates, o_m, m_rep, bias, nw, s0)


def _head_rms(x, w):
    ms = jnp.mean(x * x, axis=-1, keepdims=True)
    return (x * lax.rsqrt(ms + EPS)) * w


def _swa_prompt_kernel(nqb, sink_ref, q_ref, kv_ref, kvp_ref, qnw_ref, knw_ref,
                       h_ref, kwin_ref, vwin_ref):
    j = pl.program_id(1)
    hd = A_HEAD_DIM
    qi = lax.broadcasted_iota(jnp.int32, (A_GROUPS * WINDOW, 2 * WINDOW), 0) & (WINDOW - 1)
    si = lax.broadcasted_iota(jnp.int32, (A_GROUPS * WINDOW, 2 * WINDOW), 1)
    local = ((si < WINDOW) & (si > qi)) | ((si >= WINDOW) & (si - WINDOW <= qi))
    row_grp = lax.broadcasted_iota(jnp.int32, (A_GROUPS * WINDOW, 1), 0) >> 7
    kv_blocks = [kvp_ref[0]] + [kv_ref[0, i * WINDOW:(i + 1) * WINDOW, :] for i in range(nqb)]
    for kvh in range(A_KV_HEADS):
        kn = [_head_rms(blk[:, kvh * hd:(kvh + 1) * hd], knw_ref[...]) for blk in kv_blocks]
        vb = [blk[:, A_KV_W + kvh * hd:A_KV_W + (kvh + 1) * hd] for blk in kv_blocks]
        sk = jnp.zeros((A_GROUPS * WINDOW, 1), F32)
        for g in range(A_GROUPS):
            sk = jnp.where(row_grp == g, sink_ref[kvh * A_GROUPS + g], sk)
        for qb in range(nqb):
            rs = slice(qb * WINDOW, (qb + 1) * WINDOW)
            q4 = jnp.concatenate(
                [_head_rms(q_ref[0, rs, (kvh * A_GROUPS + g) * hd:(kvh * A_GROUPS + g + 1) * hd].astype(F32),
                           qnw_ref[...]) for g in range(A_GROUPS)], axis=0).astype(BF16)
            kk = jnp.concatenate([kn[qb], kn[qb + 1]], axis=0).astype(BF16)
            vv = jnp.concatenate([vb[qb], vb[qb + 1]], axis=0).astype(BF16)
            s = _dot_nt(q4, kk) * (hd ** -0.5)
            mask = local if qb > 0 else (local & ((j > 0) | (si >= WINDOW)))
            s = jnp.where(mask, s, -jnp.inf)
            mx = jnp.maximum(jnp.max(s, axis=-1, keepdims=True), sk)
            p = jnp.exp(s - mx)
            den = jnp.sum(p, axis=-1, keepdims=True) + jnp.exp(sk - mx)
            o = _dot(p.astype(BF16), vv) / den
            for g in range(A_GROUPS):
                head = kvh * A_GROUPS + g
                h_ref[0, rs, head * hd:(head + 1) * hd] = o[g * WINDOW:(g + 1) * WINDOW].astype(h_ref.dtype)
        kwin_ref[0, :, kvh * hd:(kvh + 1) * hd] = kn[nqb]
    vwin_ref[0] = kv_blocks[nqb][:, A_KV_W:]


def _swa_prompt(q_a, kv_a, sinks, qnw, knw, nqb):
    nb, t = q_a.shape[0], q_a.shape[1]
    tb = nqb * WINDOW
    return pl.pallas_call(
        functools.partial(_swa_prompt_kernel, nqb),
        grid=(nb, t // tb),
        in_specs=[pl.BlockSpec(memory_space=pltpu.SMEM),
                  pl.BlockSpec((1, tb, A_Q_W), lambda b, j: (b, j, 0)),
                  pl.BlockSpec((1, tb, 2 * A_KV_W), lambda b, j: (b, j, 0)),
                  pl.BlockSpec((1, WINDOW, 2 * A_KV_W), lambda b, j: (b, jnp.maximum(j * nqb - 1, 0), 0)),
                  pl.BlockSpec(qnw.shape, lambda b, j: (0, 0)),
                  pl.BlockSpec(knw.shape, lambda b, j: (0, 0))],
        out_specs=[pl.BlockSpec((1, tb, A_Q_W), lambda b, j: (b, j, 0)),
                   pl.BlockSpec((1, WINDOW, A_KV_W), lambda b, j: (b, 0, 0)),
                   pl.BlockSpec((1, WINDOW, A_KV_W), lambda b, j: (b, 0, 0))],
        out_shape=[jax.ShapeDtypeStruct((nb, t, A_Q_W), BF16),
                   jax.ShapeDtypeStruct((nb, WINDOW, A_KV_W), F32),
                   jax.ShapeDtypeStruct((nb, WINDOW, A_KV_W), F32)],
        compiler_params=_params(("arbitrary", "arbitrary")),
        name="swa_prompt",
    )(sinks, q_a, kv_a, kv_a, qnw, knw)


def _swa_sample_kernel(seq_len, n_seq, sink_ref, q_ref, kv_ref, ck_ref, cv_ref, qnw_ref, knw_ref,
                       h_ref, kwin_ref, vwin_ref):
    hd = A_HEAD_DIM
    wb = ck_ref.shape[1]
    nrow = A_GROUPS * seq_len
    pad_rows = 16
    ti = lax.broadcasted_iota(jnp.int32, (nrow, wb), 0) & (seq_len - 1)
    si = lax.broadcasted_iota(jnp.int32, (nrow, wb), 1)
    mask_c = (ti + wb - si) < WINDOW
    ti2 = lax.broadcasted_iota(jnp.int32, (nrow, pad_rows), 0) & (seq_len - 1)
    si2 = lax.broadcasted_iota(jnp.int32, (nrow, pad_rows), 1)
    mask_n = (si2 <= ti2) & (si2 < seq_len)
    row_grp = lax.broadcasted_iota(jnp.int32, (nrow, 1), 0) >> (seq_len.bit_length() - 1)
    zpad = jnp.zeros((pad_rows - seq_len, hd), F32)
    for s in range(n_seq):
        rs = slice(s * seq_len, (s + 1) * seq_len)
        for kvh in range(A_KV_HEADS):
            ks = slice(kvh * hd, (kvh + 1) * hd)
            k_new = _head_rms(kv_ref[rs, ks], knw_ref[...])
            v_new = kv_ref[rs, A_KV_W + kvh * hd:A_KV_W + (kvh + 1) * hd]
            kwin_ref[s, wb - seq_len:wb, ks] = k_new
            vwin_ref[s, wb - seq_len:wb, ks] = v_new
            sk = jnp.zeros((nrow, 1), F32)
            for g in range(A_GROUPS):
                sk = jnp.where(row_grp == g, sink_ref[kvh * A_GROUPS + g], sk)
            q4 = jnp.concatenate(
                [_head_rms(q_ref[rs, (kvh * A_GROUPS + g) * hd:(kvh * A_GROUPS + g + 1) * hd].astype(F32),
                           qnw_ref[...]) for g in range(A_GROUPS)], axis=0).astype(BF16)
            k_c = ck_ref[s, :, ks].astype(BF16)
            v_c = cv_ref[s, :, ks].astype(BF16)
            k_n = jnp.concatenate([k_new, zpad], axis=0).astype(BF16)
            v_n = jnp.concatenate([v_new, zpad], axis=0).astype(BF16)
            s1 = jnp.where(mask_c, _dot_nt(q4, k_c) * (hd ** -0.5), -jnp.inf)
            s2 = jnp.where(mask_n, _dot_nt(q4, k_n) * (hd ** -0.5), -jnp.inf)
            mx = jnp.maximum(jnp.maximum(jnp.max(s1, axis=-1, keepdims=True),
                                         jnp.max(s2, axis=-1, keepdims=True)), sk)
            p1 = jnp.exp(s1 - mx)
            p2 = jnp.exp(s2 - mx)
            den = jnp.sum(p1, axis=-1, keepdims=True) + jnp.sum(p2, axis=-1, keepdims=True) + jnp.exp(sk - mx)
            o = (_dot(p1.astype(BF16), v_c) + _dot(p2.astype(BF16), v_n)) / den
            for g in range(A_GROUPS):
                head = kvh * A_GROUPS + g
                h_ref[rs, head * hd:(head + 1) * hd] = o[g * seq_len:(g + 1) * seq_len].astype(h_ref.dtype)
        kwin_ref[s, 0:wb - seq_len, :] = ck_ref[s, seq_len:wb, :]
        vwin_ref[s, 0:wb - seq_len, :] = cv_ref[s, seq_len:wb, :]


def _swa_sample(q_a, kv_a, cache_k, cache_v, sinks, qnw, knw, seq_len, n_seq):
    n = q_a.shape[0]
    nb, wb = cache_k.shape[0], cache_k.shape[1]
    rows = n_seq * seq_len
    row = lambda w: pl.BlockSpec((rows, w), lambda i: (i, 0))
    cache = pl.BlockSpec((n_seq, wb, A_KV_W), lambda i: (i, 0, 0))
    return pl.pallas_call(
        functools.partial(_swa_sample_kernel, seq_len, n_seq),
        grid=(nb // n_seq,),
        in_specs=[pl.BlockSpec(memory_space=pltpu.SMEM), row(A_Q_W), row(2 * A_KV_W), cache, cache,
                  pl.BlockSpec(qnw.shape, lambda i: (0, 0)), pl.BlockSpec(knw.shape, lambda i: (0, 0))],
        out_specs=[row(A_Q_W), cache, cache],
        out_shape=[jax.ShapeDtypeStruct((n, A_Q_W), BF16),
                   jax.ShapeDtypeStruct(cache_k.shape, F32),
                   jax.ShapeDtypeStruct(cache_v.shape, F32)],
        compiler_params=_params(("arbitrary",)),
        name="swa_sample",
    )(sinks, q_a, kv_a, cache_k, cache_v, qnw, knw)


def _merge_ffn_kernel(x_ref, hm_ref, ha_ref, gab_ref, wa_ref, wb_ref, wo_ref, nw_ref, wg_ref, wu_ref, wd_ref,
                      y_ref):
    ga = jax.nn.sigmoid(gab_ref[:, 0:D_MODEL].astype(F32))
    gb = jax.nn.sigmoid(gab_ref[:, D_MODEL:].astype(F32))
    mix = ga * _dot(hm_ref[...], wa_ref[...]) + gb * _dot(ha_ref[...], wb_ref[...])
    x1 = x_ref[...] + _dot(mix.astype(BF16), wo_ref[...])
    ms = jnp.mean(x1 * x1, axis=-1, keepdims=True)
    hf = ((x1 * lax.rsqrt(ms + EPS)) * nw_ref[...]).astype(BF16)
    gate = _dot(hf, wg_ref[...])
    up = _dot(hf, wu_ref[...])
    act = (jax.nn.silu(gate) * up).astype(BF16)
    y_ref[...] = x1 + _dot(act, wd_ref[...])


def _merge_ffn(x2d, h_m, h_a, g_ab, wa, wb, wo, nw, wg, wu, wd, tm):
    n = x2d.shape[0]
    row = lambda w: pl.BlockSpec((tm, w), lambda i: (i, 0))
    return pl.pallas_call(
        _merge_ffn_kernel,
        grid=(n // tm,),
        in_specs=[row(D_MODEL), row(M_V_W), row(A_Q_W), row(2 * D_MODEL)]
                 + [_const_spec(w.shape) for w in (wa, wb, wo, nw, wg, wu, wd)],
        out_specs=row(D_MODEL),
        out_shape=jax.ShapeDtypeStruct((n, D_MODEL), F32),
        compiler_params=_params(("arbitrary",)),
        name="merge_ffn",
    )(x2d, h_m, h_a, g_ab, wa, wb, wo, nw, wg, wu, wd)


def _state_in(c, n):
    nb = c.shape[0]
    c2 = c.reshape(nb, M_HEADS * M_DK, M_DV)
    n2 = n.reshape(nb, M_HEADS * M_DK, 1)
    return jnp.concatenate([c2, n2, jnp.zeros((nb, M_HEADS * M_DK, M_DV - 1), F32)], axis=-1)


def _state_out(s):
    nb = s.shape[0]
    c = s[:, :, :M_DV].reshape(nb, M_HEADS, M_DK, M_DV)
    n = s[:, :, M_DV].reshape(nb, M_HEADS, M_DK)
    return c, n


def kernel(x_prompt, x_sample, state_mlstm_C, state_mlstm_n, state_mlstm_m, cache_swa_k, cache_swa_v,
           norm_mix_w, w_in, mlstm_i_bias, mlstm_f_bias, mlstm_norm_w, q_norm_w, k_norm_w, attn_sinks,
           w_branch_a, w_branch_b, w_out, norm_ffn_w, w_gate, w_up, w_down):
    depth = w_in.shape[0]
    assert depth == 1, "single trunk layer"
    l = 0
    bp, tp = x_prompt.shape[0], x_prompt.shape[1]
    bs, ts = x_sample.shape[0], x_sample.shape[1]
    assert tp % 512 == 0 and (bs * ts) % GROUP == 0 and GROUP % ts == 0 and ts & (ts - 1) == 0

    n1 = 2 * M_QK_W + 2 * M_V_W
    w1 = w_in[l][:, :n1].astype(BF16)
    wif = jnp.pad(w_in[l][:, n1:n1 + 2 * M_HEADS], ((0, 0), (0, LANES - 2 * M_HEADS))).astype(BF16)
    w2 = w_in[l][:, n1 + 2 * M_HEADS:].astype(BF16)
    nw_mix = norm_mix_w[l].reshape(1, D_MODEL)
    nw_ffn = norm_ffn_w[l].reshape(1, D_MODEL)
    bias = jnp.pad(jnp.concatenate([mlstm_i_bias[l], mlstm_f_bias[l]]), (0, LANES - 2 * M_HEADS)).reshape(1, LANES)
    nw_m = mlstm_norm_w[l].reshape(1, M_V_W)
    qnw = q_norm_w[l].reshape(1, A_HEAD_DIM)
    knw = k_norm_w[l].reshape(1, A_HEAD_DIM)
    sinks = attn_sinks[l]
    wa, wb, wo = w_branch_a[l].astype(BF16), w_branch_b[l].astype(BF16), w_out[l].astype(BF16)
    wg, wu, wd = w_gate[l].astype(BF16), w_up[l].astype(BF16), w_down[l].astype(BF16)

    xp = x_prompt.reshape(bp * tp, D_MODEL)
    qkv, o_m, gates, q_a, kv_a, g_ab = _proj(xp, nw_mix, w1, wif, w2, BF16, 512)
    s0 = jnp.zeros((bp, M_HEADS * M_DK, S_W), F32)
    m0 = jnp.zeros((bp, 8, LANES), F32)
    h_m, s_p, m_p = _mlstm_prompt(qkv.reshape(bp, tp, -1), gates.reshape(bp, tp, LANES),
                                  o_m.reshape(bp, tp, M_V_W), bias, nw_m, s0, m0)
    h_a, kwin_p, vwin_p = _swa_prompt(q_a.reshape(bp, tp, A_Q_W), kv_a.reshape(bp, tp, 2 * A_KV_W),
                                      sinks, qnw, knw, 4)
    yp = _merge_ffn(xp, h_m.reshape(bp * tp, M_V_W), h_a.reshape(bp * tp, A_Q_W), g_ab,
                    wa, wb, wo, nw_ffn, wg, wu, wd, 512).reshape(bp, tp, D_MODEL)
    c_p, n_p = _state_out(s_p)
    m_pr = m_p[:, 0, :M_HEADS]

    xs = x_sample.reshape(bs * ts, D_MODEL)
    tms = 512 if (bs * ts) % 512 == 0 else GROUP
    qkv, o_m, gates, q_a, kv_a, g_ab = _proj(xs, nw_mix, w1, wif, w2, F32, tms)
    s0 = _state_in(state_mlstm_C[l], state_mlstm_n[l])
    m_rep = jnp.pad(jnp.repeat(state_mlstm_m[l], ts, axis=0), ((0, 0), (0, LANES - M_HEADS)))
    h_m, s_s, mt_s = _mlstm_sample(qkv, gates, o_m, m_rep, bias, nw_m, s0, ts)
    wbuf = cache_swa_k.shape[2]
    h_a, kwin_s, vwin_s = _swa_sample(q_a, kv_a, cache_swa_k[l].reshape(bs, wbuf, A_KV_W),
                                      cache_swa_v[l].reshape(bs, wbuf, A_KV_W), sinks, qnw, knw, ts, 8)
    ys = _merge_ffn(xs, h_m, h_a, g_ab, wa, wb, wo, nw_ffn, wg, wu, wd, tms).reshape(bs, ts, D_MODEL)
    c_s, n_s = _state_out(s_s)
    m_s = mt_s.reshape(bs, ts, LANES)[:, ts - 1, :M_HEADS]

    kv5 = lambda a: a.reshape(a.shape[0], a.shape[1], A_KV_HEADS, A_HEAD_DIM)[None]
    return (yp, ys,
            c_p[None], n_p[None], m_pr[None], kv5(kwin_p), kv5(vwin_p),
            c_s[None], n_s[None], m_s[None], kv5(kwin_s), kv5(vwin_s))
```

```python
import functools

import jax
import jax.numpy as jnp
from jax import lax
from jax.experimental import pallas as pl
from jax.experimental.pallas import tpu as pltpu

F32 = jnp.float32
BF16 = jnp.bfloat16

D_MODEL = 1024
M_HEADS = 4
M_DK = 64
M_DV = 128
M_CHUNK = 64
M_QK_W = M_HEADS * M_DK
M_V_W = M_HEADS * M_DV
A_HEADS = 8
A_KV_HEADS = 2
A_HEAD_DIM = 64
A_GROUPS = A_HEADS // A_KV_HEADS
A_Q_W = A_HEADS * A_HEAD_DIM
A_KV_W = A_KV_HEADS * A_HEAD_DIM
WINDOW = 128
D_FF = 2816
EPS = 1e-6

LANES = 128
GROUP = 128
S_W = 2 * M_DV
VMEM_LIMIT = 56 * 1024 * 1024

NT_DIMS = (((1,), (1,)), ((), ()))
TN_DIMS = (((0,), (0,)), ((), ()))


def _dot(a, b):
    return jnp.dot(a, b, preferred_element_type=F32)


def _dot_nt(a, b):
    return lax.dot_general(a, b, NT_DIMS, preferred_element_type=F32)


def _dot_tn(a, b):
    return lax.dot_general(a, b, TN_DIMS, preferred_element_type=F32)


def _const_spec(shape):
    nd = len(shape)
    return pl.BlockSpec(shape, lambda *_: (0,) * nd, pipeline_mode=pl.Buffered(1))


def _params(sem):
    return pltpu.CompilerParams(dimension_semantics=sem, vmem_limit_bytes=VMEM_LIMIT)


def _proj_kernel(x_ref, nw_ref, w1_ref, wif_ref, w2_ref,
                 qkv_ref, o_ref, g_ref, qa_ref, kva_ref, gab_ref):
    x = x_ref[...]
    ms = jnp.mean(x * x, axis=-1, keepdims=True)
    hn = ((x * lax.rsqrt(ms + EPS)) * nw_ref[...]).astype(BF16)
    qkv_ref[...] = _dot(hn, w1_ref[:, 0:2 * M_QK_W + M_V_W]).astype(qkv_ref.dtype)
    o_ref[...] = _dot(hn, w1_ref[:, 2 * M_QK_W + M_V_W:]).astype(o_ref.dtype)
    g_ref[...] = _dot(hn, wif_ref[...])
    qa_ref[...] = _dot(hn, w2_ref[:, 0:A_Q_W]).astype(qa_ref.dtype)
    kva_ref[...] = _dot(hn, w2_ref[:, A_Q_W:A_Q_W + 2 * A_KV_W])
    gab_ref[...] = _dot(hn, w2_ref[:, A_Q_W + 2 * A_KV_W:]).astype(gab_ref.dtype)


def _proj(x2d, nw, w1, wif, w2, qkv_dtype, tm):
    n = x2d.shape[0]
    row = lambda w: pl.BlockSpec((tm, w), lambda i: (i, 0))
    w_qkv = 2 * M_QK_W + M_V_W
    return pl.pallas_call(
        _proj_kernel,
        grid=(n // tm,),
        in_specs=[row(D_MODEL), _const_spec(nw.shape), _const_spec(w1.shape),
                  _const_spec(wif.shape), _const_spec(w2.shape)],
        out_specs=[row(w_qkv), row(M_V_W), row(LANES), row(A_Q_W), row(2 * A_KV_W), row(2 * D_MODEL)],
        out_shape=[jax.ShapeDtypeStruct((n, w_qkv), qkv_dtype),
                   jax.ShapeDtypeStruct((n, M_V_W), BF16),
                   jax.ShapeDtypeStruct((n, LANES), F32),
                   jax.ShapeDtypeStruct((n, A_Q_W), BF16),
                   jax.ShapeDtypeStruct((n, 2 * A_KV_W), F32),
                   jax.ShapeDtypeStruct((n, 2 * D_MODEL), BF16)],
        compiler_params=_params(("arbitrary",)),
        name="proj",
    )(x2d, nw, w1, wif, w2)


def _proj_prompt_kernel(x_ref, nw_ref, w1_ref, wif_ref, w2_ref, wqt_ref, bd_ref, qcol_ref, krow_ref,
                        qkv_ref, o_ref, g_ref, qat_ref, kva_ref, gab_ref):
    x = x_ref[...]
    ms = jnp.mean(x * x, axis=-1, keepdims=True)
    hn = ((x * lax.rsqrt(ms + EPS)) * nw_ref[...]).astype(BF16)
    qkv_ref[...] = _dot(hn, w1_ref[:, 0:2 * M_QK_W + M_V_W]).astype(qkv_ref.dtype)
    o_ref[...] = _dot(hn, w1_ref[:, 2 * M_QK_W + M_V_W:]).astype(o_ref.dtype)
    g_ref[...] = _dot(hn, wif_ref[...])
    gab_ref[...] = _dot(hn, w2_ref[:, 2 * A_KV_W:]).astype(gab_ref.dtype)
    kv = _dot(hn, w2_ref[:, 0:2 * A_KV_W])
    k = kv[:, 0:A_KV_W]
    ksq = k * k
    hi = ksq.astype(BF16)
    lo = (ksq - hi.astype(F32)).astype(BF16)
    ssq = _dot(hi, bd_ref[...]) + _dot(lo, bd_ref[...])
    kva_ref[:, 0:A_KV_W] = (k * lax.rsqrt(ssq * (1.0 / A_HEAD_DIM) + EPS)) * krow_ref[...]
    kva_ref[:, A_KV_W:] = kv[:, A_KV_W:]
    qt = _dot_nt(wqt_ref[...], hn)
    n_blk = qat_ref.shape[0]
    for h in range(A_HEADS):
        hs = slice(h * A_HEAD_DIM, (h + 1) * A_HEAD_DIM)
        blk = qt[hs]
        ssq_q = jnp.sum(blk * blk, axis=0, keepdims=True)
        qn = (blk * lax.rsqrt(ssq_q * (1.0 / A_HEAD_DIM) + EPS)) * qcol_ref[hs]
        for c in range(n_blk):
            qat_ref[c, hs, :] = qn[:, c * LANES:(c + 1) * LANES].astype(qat_ref.dtype)


def _proj_prompt(x2d, nw, w1, wif, w2, wqt, bd, qcol, krow, tm):
    n = x2d.shape[0]
    row = lambda w: pl.BlockSpec((tm, w), lambda i: (i, 0))
    w_qkv = 2 * M_QK_W + M_V_W
    n_blk = tm // LANES
    return pl.pallas_call(
        _proj_prompt_kernel,
        grid=(n // tm,),
        in_specs=[row(D_MODEL)] + [_const_spec(a.shape) for a in (nw, w1, wif, w2, wqt, bd, qcol, krow)],
        out_specs=[row(w_qkv), row(M_V_W), row(LANES),
                   pl.BlockSpec((n_blk, A_Q_W, LANES), lambda i: (i, 0, 0)),
                   row(2 * A_KV_W), row(2 * D_MODEL)],
        out_shape=[jax.ShapeDtypeStruct((n, w_qkv), BF16),
                   jax.ShapeDtypeStruct((n, M_V_W), BF16),
                   jax.ShapeDtypeStruct((n, LANES), F32),
                   jax.ShapeDtypeStruct((n // LANES, A_Q_W, LANES), BF16),
                   jax.ShapeDtypeStruct((n, 2 * A_KV_W), F32),
                   jax.ShapeDtypeStruct((n, 2 * D_MODEL), BF16)],
        compiler_params=_params(("arbitrary",)),
        name="proj_prompt",
    )(x2d, nw, w1, wif, w2, wqt, bd, qcol, krow)


def _split3(x):
    hi = x.astype(BF16)
    r1 = x - hi.astype(F32)
    mid = r1.astype(BF16)
    lo = (r1 - mid.astype(F32)).astype(BF16)
    return hi, mid, lo


def _log_sigmoid(x):
    return jnp.minimum(x, 0.0) - jnp.log1p(jnp.exp(-jnp.abs(x)))


def _gate_prep(gb, chunk_shift):
    r = lax.broadcasted_iota(jnp.int32, (GROUP, GROUP), 0)
    c = lax.broadcasted_iota(jnp.int32, (GROUP, GROUP), 1)
    same = (r >> chunk_shift) == (c >> chunk_shift)
    causal = same & (c <= r)
    causal_t = same & (r <= c)
    lf = _log_sigmoid(gb)
    cm = jnp.where(causal, 1.0, 0.0).astype(BF16)
    b_col = sum(_dot(cm, p) for p in _split3(lf))
    gt8 = gb.T[0:8]
    lft8 = _log_sigmoid(gt8)
    cm_t = jnp.where(causal_t, 1.0, 0.0).astype(BF16)
    bt8 = sum(_dot(p, cm_t) for p in _split3(lft8))
    return same, causal, b_col, gt8, bt8


def _head_maxes(h, gb, b_col, gt8, bt8, same, causal):
    a_row = gt8[h:h + 1] - bt8[M_HEADS + h:M_HEADS + h + 1]
    bc = b_col[:, M_HEADS + h:M_HEADS + h + 1]
    a_col = gb[:, h:h + 1] - bc
    arow = jnp.broadcast_to(a_row, (GROUP, GROUP))
    run_max = jnp.max(jnp.where(causal, arow, -jnp.inf), axis=1, keepdims=True)
    chunk_max = jnp.max(jnp.where(same, arow, -jnp.inf), axis=1, keepdims=True)
    return arow, a_col, bc, run_max, chunk_max


def _head_weights(arow, a_col, bc, run_max, chunk_max, m_prev, causal):
    big_m = jnp.maximum(m_prev, run_max)
    m_last = jnp.maximum(m_prev, chunk_max)
    w = jnp.where(causal, jnp.exp(arow - big_m), 0.0)
    w_inter = jnp.exp(m_prev - big_m)
    g_vec = jnp.exp(m_prev - m_last)
    w_last = jnp.exp(a_col - m_last)
    m_t = bc + big_m
    return w, w_inter, g_vec, w_last, m_t


def _head_masks():
    lane_head = lax.broadcasted_iota(jnp.int32, (1, M_QK_W), 1) >> 6
    return [lane_head == h for h in range(M_HEADS)]


def _v_aug(v_all, h):
    ones_col = jnp.where(lax.broadcasted_iota(jnp.int32, (v_all.shape[0], M_DV), 1) == 0, 1.0, 0.0)
    return jnp.concatenate([v_all[:, h * M_DV:(h + 1) * M_DV], ones_col.astype(v_all.dtype)], axis=1)


def _head_out(out_aug, m_t, nw_h, o_h):
    num = out_aug[:, :M_DV]
    den = out_aug[:, M_DV:M_DV + 1]
    hh = num / jnp.maximum(jnp.abs(den), jnp.exp(-m_t))
    ms = jnp.mean(hh * hh, axis=-1, keepdims=True)
    hn = (hh * lax.rsqrt(ms + EPS)) * nw_h
    return hn * jax.nn.sigmoid(o_h.astype(F32))


def _mlstm_prompt_kernel(qkv_ref, g_ref, o_ref, bias_ref, nw_ref, s0_ref, m0_ref,
                         h_ref, s_ref, m_ref):
    step = pl.program_id(0)

    @pl.when(step == 0)
    def _():
        s_ref[...] = s0_ref[...]
        m_ref[...] = m0_ref[...]

    nb = qkv_ref.shape[0]
    hm = _head_masks()
    rows = lax.broadcasted_iota(jnp.int32, (GROUP, 1), 0)
    lane = lax.broadcasted_iota(jnp.int32, (8, LANES), 1)
    n_chunks = GROUP // M_CHUNK
    for b in range(nb):
        q_all = qkv_ref[b, :, 0:M_QK_W]
        k_all = qkv_ref[b, :, M_QK_W:2 * M_QK_W]
        v_all = qkv_ref[b, :, 2 * M_QK_W:]
        gb = g_ref[b] + bias_ref[...]
        same, causal, b_col, gt8, bt8 = _gate_prep(gb, 6)
        qf = q_all.astype(F32)
        ks = k_all.astype(F32) * (M_DK ** -0.5)
        q_stack = jnp.concatenate([jnp.where(hm[h], q_all, jnp.zeros_like(q_all)) for h in range(M_HEADS)], axis=0)
        sc = _dot_nt(q_stack, ks.astype(BF16))
        m_all = m_ref[b]
        per_head = []
        m_new_mat = jnp.zeros((8, LANES), F32)
        for h in range(M_HEADS):
            arow, a_col, bc, run_max, chunk_max = _head_maxes(h, gb, b_col, gt8, bt8, same, causal)
            m_c = m_all[0:1, h:h + 1]
            m_prev = jnp.broadcast_to(m_c, (GROUP, 1))
            for ci in range(1, n_chunks):
                last = ci * M_CHUNK - 1
                m_c = bc[last:last + 1] + jnp.maximum(m_c, chunk_max[last:last + 1])
                m_prev = jnp.where(rows >= ci * M_CHUNK, m_c, m_prev)
            w, w_inter, g_vec, w_last, m_t = _head_weights(arow, a_col, bc, run_max, chunk_max, m_prev, causal)
            va = _v_aug(v_all, h)
            s_h = (sc[h * GROUP:(h + 1) * GROUP] * w).astype(BF16)
            intra = _dot(s_h, va)
            per_head.append((w_inter, g_vec, w_last, m_t, va, intra))
            m_new_mat = jnp.where(lane == h, m_t[GROUP - 1:GROUP], m_new_mat)
        m_ref[b] = m_new_mat

        outs = [[] for _ in range(M_HEADS)]
        for ci in range(n_chunks):
            rs = slice(ci * M_CHUNK, (ci + 1) * M_CHUNK)
            s_old = s_ref[b]
            qw = jnp.concatenate(
                [(jnp.where(hm[h], qf[rs], 0.0) * per_head[h][0][rs]).astype(BF16) for h in range(M_HEADS)], axis=0)
            inter = _dot(qw, s_old.astype(BF16))
            d_s = sum(_dot_tn((jnp.where(hm[h], ks[rs], 0.0) * per_head[h][2][rs]).astype(BF16), per_head[h][4][rs])
                      for h in range(M_HEADS))
            for h in range(M_HEADS):
                hs = slice(h * M_DK, (h + 1) * M_DK)
                g = per_head[h][1][ci * M_CHUNK:ci * M_CHUNK + 1]
                s_ref[b, hs, :] = g * s_old[hs] + d_s[hs]
                outs[h].append(inter[h * M_CHUNK:(h + 1) * M_CHUNK] + per_head[h][5][rs])
        for h in range(M_HEADS):
            out_aug = jnp.concatenate(outs[h], axis=0)
            vs = slice(h * M_DV, (h + 1) * M_DV)
            h_ref[b, :, vs] = _head_out(out_aug, per_head[h][3], nw_ref[:, vs], o_ref[b, :, vs]).astype(h_ref.dtype)


def _mlstm_prompt(qkv, gates, o_m, bias, nw, s0, m0):
    nb, t = qkv.shape[0], qkv.shape[1]
    blk = lambda w: pl.BlockSpec((nb, GROUP, w), lambda i: (0, i, 0))
    full = lambda a: pl.BlockSpec(a.shape, lambda i: (0,) * a.ndim)
    return pl.pallas_call(
        _mlstm_prompt_kernel,
        grid=(t // GROUP,),
        in_specs=[blk(qkv.shape[2]), blk(LANES), blk(M_V_W), full(bias), full(nw), full(s0), full(m0)],
        out_specs=[blk(M_V_W), full(s0), full(m0)],
        out_shape=[jax.ShapeDtypeStruct((nb, t, M_V_W), BF16),
                   jax.ShapeDtypeStruct(s0.shape, F32),
                   jax.ShapeDtypeStruct(m0.shape, F32)],
        compiler_params=_params(("arbitrary",)),
        name="mlstm_prompt",
    )(qkv, gates, o_m, bias, nw, s0, m0)


def _mlstm_sample_kernel(seq_len, qkv_ref, g_ref, o_ref, mrep_ref, bias_ref, nw_ref, s0_ref,
                         h_ref, s_ref, mt_ref):
    n_seq = GROUP // seq_len
    shift = seq_len.bit_length() - 1
    hm = _head_masks()
    lane = lax.broadcasted_iota(jnp.int32, (GROUP, LANES), 1)
    q_all = qkv_ref[:, 0:M_QK_W]
    ks = qkv_ref[:, M_QK_W:2 * M_QK_W] * (M_DK ** -0.5)
    v_all = qkv_ref[:, 2 * M_QK_W:]
    gb = g_ref[...] + bias_ref[...]
    same, causal, b_col, gt8, bt8 = _gate_prep(gb, shift)
    q_bf = q_all.astype(BF16)
    q_stack = jnp.concatenate([jnp.where(hm[h], q_bf, jnp.zeros_like(q_bf)) for h in range(M_HEADS)], axis=0)
    sc = _dot_nt(q_stack, ks.astype(BF16))
    m_rep = mrep_ref[...]
    qw, kw, va, gv, intra, m_ts = [], [], [], [], [], []
    mt_mat = jnp.zeros((GROUP, LANES), F32)
    for h in range(M_HEADS):
        arow, a_col, bc, run_max, chunk_max = _head_maxes(h, gb, b_col, gt8, bt8, same, causal)
        w, w_inter, g_vec, w_last, m_t = _head_weights(arow, a_col, bc, run_max, chunk_max,
                                                       m_rep[:, h:h + 1], causal)
        va_h = _v_aug(v_all, h)
        s_h = (sc[h * GROUP:(h + 1) * GROUP] * w).astype(BF16)
        intra.append(_dot(s_h, va_h.astype(BF16)))
        qw.append(jnp.where(hm[h], q_all, 0.0) * w_inter)
        kw.append(jnp.where(hm[h], ks, 0.0) * w_last)
        va.append(va_h)
        gv.append(g_vec)
        m_ts.append(m_t)
        mt_mat = jnp.where(lane == h, m_t, mt_mat)
    mt_ref[...] = mt_mat

    zpad = jnp.zeros((16 - seq_len, M_QK_W), F32) if seq_len < 16 else None
    pad = (lambda x: jnp.concatenate([x, zpad], axis=0)) if zpad is not None else (lambda x: x)
    inter = [[] for _ in range(M_HEADS)]
    for s in range(n_seq):
        rs = slice(s * seq_len, (s + 1) * seq_len)
        s_old = s0_ref[s]
        qw_s = jnp.concatenate([pad(qw[h][rs]) for h in range(M_HEADS)], axis=0).astype(BF16)
        inter_s = _dot(qw_s, s_old.astype(BF16))
        rows_h = qw_s.shape[0] // M_HEADS
        d_s = sum(_dot_tn(pad(kw[h][rs]).astype(BF16), pad(va[h][rs]).astype(BF16)) for h in range(M_HEADS))
        for h in range(M_HEADS):
            hs = slice(h * M_DK, (h + 1) * M_DK)
            g = gv[h][s * seq_len:s * seq_len + 1]
            s_ref[s, hs, :] = g * s_old[hs] + d_s[hs]
            inter[h].append(inter_s[h * rows_h:h * rows_h + seq_len])
    for h in range(M_HEADS):
        out_aug = jnp.concatenate(inter[h], axis=0) + intra[h]
        vs = slice(h * M_DV, (h + 1) * M_DV)
        h_ref[:, vs] = _head_out(out_aug, m_ts[h], nw_ref[:, vs], o_ref[:, vs]).astype(h_ref.dtype)


def _mlstm_sample(qkv, gates, o_m, m_rep, bias, nw, s0, seq_len):
    n = qkv.shape[0]
    n_seq = GROUP // seq_len
    row = lambda w: pl.BlockSpec((GROUP, w), lambda i: (i, 0))
    full = lambda a: pl.BlockSpec(a.shape, lambda i: (0,) * a.ndim)
    st = pl.BlockSpec((n_seq,) + s0.shape[1:], lambda i: (i, 0, 0))
    return pl.pallas_call(
        functools.partial(_mlstm_sample_kernel, seq_len),
        grid=(n // GROUP,),
        in_specs=[row(qkv.shape[1]), row(LANES), row(M_V_W), row(LANES), full(bias), full(nw), st],
        out_specs=[row(M_V_W), st, row(LANES)],
        out_shape=[jax.ShapeDtypeStruct((n, M_V_W), BF16),
                   jax.ShapeDtypeStruct(s0.shape, F32),
                   jax.ShapeDtypeStruct((n, LANES), F32)],
        compiler_params=_params(("arbitrary",)),
        name="mlstm_sample",
    )(qkv, gates, o_m, m_rep, bias, nw, s0)


def _head_rms(x, w):
    ms = jnp.mean(x * x, axis=-1, keepdims=True)
    return (x * lax.rsqrt(ms + EPS)) * w


def _swa_prompt_kernel(nqb, sink_ref, q_ref, kv_ref, kvp_ref, qnw_ref, knw_ref,
                       h_ref, kwin_ref, vwin_ref):
    j = pl.program_id(1)
    hd = A_HEAD_DIM
    qi = lax.broadcasted_iota(jnp.int32, (A_GROUPS * WINDOW, 2 * WINDOW), 0) & (WINDOW - 1)
    si = lax.broadcasted_iota(jnp.int32, (A_GROUPS * WINDOW, 2 * WINDOW), 1)
    local = ((si < WINDOW) & (si > qi)) | ((si >= WINDOW) & (si - WINDOW <= qi))
    row_grp = lax.broadcasted_iota(jnp.int32, (A_GROUPS * WINDOW, 1), 0) >> 7
    kv_blocks = [kvp_ref[0]] + [kv_ref[0, i * WINDOW:(i + 1) * WINDOW, :] for i in range(nqb)]
    for kvh in range(A_KV_HEADS):
        kn = [_head_rms(blk[:, kvh * hd:(kvh + 1) * hd], knw_ref[...]) for blk in kv_blocks]
        vb = [blk[:, A_KV_W + kvh * hd:A_KV_W + (kvh + 1) * hd] for blk in kv_blocks]
        sk = jnp.zeros((A_GROUPS * WINDOW, 1), F32)
        for g in range(A_GROUPS):
            sk = jnp.where(row_grp == g, sink_ref[kvh * A_GROUPS + g], sk)
        for qb in range(nqb):
            rs = slice(qb * WINDOW, (qb + 1) * WINDOW)
            q4 = jnp.concatenate(
                [_head_rms(q_ref[0, rs, (kvh * A_GROUPS + g) * hd:(kvh * A_GROUPS + g + 1) * hd].astype(F32),
                           qnw_ref[...]) for g in range(A_GROUPS)], axis=0).astype(BF16)
            kk = jnp.concatenate([kn[qb], kn[qb + 1]], axis=0).astype(BF16)
            vv = jnp.concatenate([vb[qb], vb[qb + 1]], axis=0).astype(BF16)
            s = _dot_nt(q4, kk) * (hd ** -0.5)
            mask = local if qb > 0 else (local & ((j > 0) | (si >= WINDOW)))
            s = jnp.where(mask, s, -jnp.inf)
            mx = jnp.maximum(jnp.max(s, axis=-1, keepdims=True), sk)
            p = jnp.exp(s - mx)
            den = jnp.sum(p, axis=-1, keepdims=True) + jnp.exp(sk - mx)
            o = _dot(p.astype(BF16), vv) / den
            for g in range(A_GROUPS):
                head = kvh * A_GROUPS + g
                h_ref[0, rs, head * hd:(head + 1) * hd] = o[g * WINDOW:(g + 1) * WINDOW].astype(h_ref.dtype)
        kwin_ref[0, :, kvh * hd:(kvh + 1) * hd] = kn[nqb]
    vwin_ref[0] = kv_blocks[nqb][:, A_KV_W:]


def _swa_prompt(q_a, kv_a, sinks, qnw, knw, nqb):
    nb, t = q_a.shape[0], q_a.shape[1]
    tb = nqb * WINDOW
    return pl.pallas_call(
        functools.partial(_swa_prompt_kernel, nqb),
        grid=(nb, t // tb),
        in_specs=[pl.BlockSpec(memory_space=pltpu.SMEM),
                  pl.BlockSpec((1, tb, A_Q_W), lambda b, j: (b, j, 0)),
                  pl.BlockSpec((1, tb, 2 * A_KV_W), lambda b, j: (b, j, 0)),
                  pl.BlockSpec((1, WINDOW, 2 * A_KV_W), lambda b, j: (b, jnp.maximum(j * nqb - 1, 0), 0)),
                  pl.BlockSpec(qnw.shape, lambda b, j: (0, 0)),
                  pl.BlockSpec(knw.shape, lambda b, j: (0, 0))],
        out_specs=[pl.BlockSpec((1, tb, A_Q_W), lambda b, j: (b, j, 0)),
                   pl.BlockSpec((1, WINDOW, A_KV_W), lambda b, j: (b, 0, 0)),
                   pl.BlockSpec((1, WINDOW, A_KV_W), lambda b, j: (b, 0, 0))],
        out_shape=[jax.ShapeDtypeStruct((nb, t, A_Q_W), BF16),
                   jax.ShapeDtypeStruct((nb, WINDOW, A_KV_W), F32),
                   jax.ShapeDtypeStruct((nb, WINDOW, A_KV_W), F32)],
        compiler_params=_params(("arbitrary", "arbitrary")),
        name="swa_prompt",
    )(sinks, q_a, kv_a, kv_a, qnw, knw)


def _swa_prompt_t_kernel(nqb, sink_ref, qt_ref, kv_ref, kvp_ref, h_ref, kwin_ref, vwin_ref):
    j = pl.program_id(1)
    hd = A_HEAD_DIM
    nq = A_GROUPS * WINDOW
    si = lax.broadcasted_iota(jnp.int32, (2 * WINDOW, nq), 0)
    qi = lax.broadcasted_iota(jnp.int32, (2 * WINDOW, nq), 1) & (WINDOW - 1)
    local = ((si < WINDOW) & (si > qi)) | ((si >= WINDOW) & (si - WINDOW <= qi))
    first = local & ((j > 0) | (si >= WINDOW))
    lane_grp = lax.broadcasted_iota(jnp.int32, (1, nq), 1) >> (WINDOW.bit_length() - 1)
    kv_blocks = [kvp_ref[0]] + [kv_ref[0, i * WINDOW:(i + 1) * WINDOW, :] for i in range(nqb)]
    k_bf = [blk[:, 0:A_KV_W].astype(BF16) for blk in kv_blocks]
    vt_bf = [blk[:, A_KV_W:].T.astype(BF16) for blk in kv_blocks]
    zeros = jnp.zeros((hd, nq), BF16)
    sinks = []
    for kvh in range(A_KV_HEADS):
        sk = jnp.zeros((1, nq), F32)
        for g in range(A_GROUPS):
            sk = jnp.where(lane_grp == g, sink_ref[kvh * A_GROUPS + g], sk)
        sinks.append(sk)
    for qb in range(nqb):
        kk = jnp.concatenate([k_bf[qb], k_bf[qb + 1]], axis=0)
        vt = jnp.concatenate([vt_bf[qb], vt_bf[qb + 1]], axis=1)
        mask = local if qb > 0 else first
        pieces = []
        for kvh in range(A_KV_HEADS):
            q4t = jnp.concatenate(
                [qt_ref[qb, (kvh * A_GROUPS + g) * hd:(kvh * A_GROUPS + g + 1) * hd, :] for g in range(A_GROUPS)],
                axis=1)
            wq = jnp.concatenate([q4t, zeros] if kvh == 0 else [zeros, q4t], axis=0)
            s = jnp.where(mask, _dot(kk, wq), -jnp.inf)
            sk = sinks[kvh]
            mx = jnp.maximum(jnp.max(s, axis=0, keepdims=True), sk)
            p = jnp.exp(s - mx)
            den = jnp.sum(p, axis=0, keepdims=True) + jnp.exp(sk - mx)
            ot = _dot(vt, p.astype(BF16))[kvh * hd:(kvh + 1) * hd] / den
            pieces += [ot[:, g * WINDOW:(g + 1) * WINDOW] for g in range(A_GROUPS)]
        h_t = jnp.concatenate(pieces, axis=0)
        h_ref[0, qb * WINDOW:(qb + 1) * WINDOW, :] = h_t.T.astype(h_ref.dtype)
    kwin_ref[0] = kv_blocks[nqb][:, 0:A_KV_W]
    vwin_ref[0] = kv_blocks[nqb][:, A_KV_W:]


def _swa_prompt_t(qat, kv_a, sinks, nqb):
    nb, t = kv_a.shape[0], kv_a.shape[1]
    tb = nqb * WINDOW
    nj = t // tb
    return pl.pallas_call(
        functools.partial(_swa_prompt_t_kernel, nqb),
        grid=(nb, nj),
        in_specs=[pl.BlockSpec(memory_space=pltpu.SMEM),
                  pl.BlockSpec((nqb, A_Q_W, WINDOW), lambda b, j: (b * nj + j, 0, 0)),
                  pl.BlockSpec((1, tb, 2 * A_KV_W), lambda b, j: (b, j, 0)),
                  pl.BlockSpec((1, WINDOW, 2 * A_KV_W), lambda b, j: (b, jnp.maximum(j * nqb - 1, 0), 0))],
        out_specs=[pl.BlockSpec((1, tb, A_Q_W), lambda b, j: (b, j, 0)),
                   pl.BlockSpec((1, WINDOW, A_KV_W), lambda b, j: (b, 0, 0)),
                   pl.BlockSpec((1, WINDOW, A_KV_W), lambda b, j: (b, 0, 0))],
        out_shape=[jax.ShapeDtypeStruct((nb, t, A_Q_W), BF16),
                   jax.ShapeDtypeStruct((nb, WINDOW, A_KV_W), F32),
                   jax.ShapeDtypeStruct((nb, WINDOW, A_KV_W), F32)],
        compiler_params=_params(("arbitrary", "arbitrary")),
        name="swa_prompt",
    )(sinks, qat, kv_a, kv_a)


def _swa_sample_kernel(seq_len, n_seq, sink_ref, q_ref, kv_ref, ck_ref, cv_ref, qnw_ref, knw_ref,
                       h_ref, kwin_ref, vwin_ref):
    hd = A_HEAD_DIM
    wb = ck_ref.shape[1]
    nrow = A_GROUPS * seq_len
    pad_rows = 16
    ti = lax.broadcasted_iota(jnp.int32, (nrow, wb), 0) & (seq_len - 1)
    si = lax.broadcasted_iota(jnp.int32, (nrow, wb), 1)
    mask_c = (ti + wb - si) < WINDOW
    ti2 = lax.broadcasted_iota(jnp.int32, (nrow, pad_rows), 0) & (seq_len - 1)
    si2 = lax.broadcasted_iota(jnp.int32, (nrow, pad_rows), 1)
    mask_n = (si2 <= ti2) & (si2 < seq_len)
    row_grp = lax.broadcasted_iota(jnp.int32, (nrow, 1), 0) >> (seq_len.bit_length() - 1)
    zpad = jnp.zeros((pad_rows - seq_len, hd), F32)
    for s in range(n_seq):
        rs = slice(s * seq_len, (s + 1) * seq_len)
        for kvh in range(A_KV_HEADS):
            ks = slice(kvh * hd, (kvh + 1) * hd)
            k_new = _head_rms(kv_ref[rs, ks], knw_ref[...])
            v_new = kv_ref[rs, A_KV_W + kvh * hd:A_KV_W + (kvh + 1) * hd]
            kwin_ref[s, wb - seq_len:wb, ks] = k_new
            vwin_ref[s, wb - seq_len:wb, ks] = v_new
            sk = jnp.zeros((nrow, 1), F32)
            for g in range(A_GROUPS):
                sk = jnp.where(row_grp == g, sink_ref[kvh * A_GROUPS + g], sk)
            q4 = jnp.concatenate(
                [_head_rms(q_ref[rs, (kvh * A_GROUPS + g) * hd:(kvh * A_GROUPS + g + 1) * hd].astype(F32),
                           qnw_ref[...]) for g in range(A_GROUPS)], axis=0).astype(BF16)
            k_c = ck_ref[s, :, ks].astype(BF16)
            v_c = cv_ref[s, :, ks].astype(BF16)
            k_n = jnp.concatenate([k_new, zpad], axis=0).astype(BF16)
            v_n = jnp.concatenate([v_new, zpad], axis=0).astype(BF16)
            s1 = jnp.where(mask_c, _dot_nt(q4, k_c) * (hd ** -0.5), -jnp.inf)
            s2 = jnp.where(mask_n, _dot_nt(q4, k_n) * (hd ** -0.5), -jnp.inf)
            mx = jnp.maximum(jnp.maximum(jnp.max(s1, axis=-1, keepdims=True),
                                         jnp.max(s2, axis=-1, keepdims=True)), sk)
            p1 = jnp.exp(s1 - mx)
            p2 = jnp.exp(s2 - mx)
            den = jnp.sum(p1, axis=-1, keepdims=True) + jnp.sum(p2, axis=-1, keepdims=True) + jnp.exp(sk - mx)
            o = (_dot(p1.astype(BF16), v_c) + _dot(p2.astype(BF16), v_n)) / den
            for g in range(A_GROUPS):
                head = kvh * A_GROUPS + g
                h_ref[rs, head * hd:(head + 1) * hd] = o[g * seq_len:(g + 1) * seq_len].astype(h_ref.dtype)
        kwin_ref[s, 0:wb - seq_len, :] = ck_ref[s, seq_len:wb, :]
        vwin_ref[s, 0:wb - seq_len, :] = cv_ref[s, seq_len:wb, :]


def _swa_sample(q_a, kv_a, cache_k, cache_v, sinks, qnw, knw, seq_len, n_seq):
    n = q_a.shape[0]
    nb, wb = cache_k.shape[0], cache_k.shape[1]
    rows = n_seq * seq_len
    row = lambda w: pl.BlockSpec((rows, w), lambda i: (i, 0))
    cache = pl.BlockSpec((n_seq, wb, A_KV_W), lambda i: (i, 0, 0))
    return pl.pallas_call(
        functools.partial(_swa_sample_kernel, seq_len, n_seq),
        grid=(nb // n_seq,),
        in_specs=[pl.BlockSpec(memory_space=pltpu.SMEM), row(A_Q_W), row(2 * A_KV_W), cache, cache,
                  pl.BlockSpec(qnw.shape, lambda i: (0, 0)), pl.BlockSpec(knw.shape, lambda i: (0, 0))],
        out_specs=[row(A_Q_W), cache, cache],
        out_shape=[jax.ShapeDtypeStruct((n, A_Q_W), BF16),
                   jax.ShapeDtypeStruct(cache_k.shape, F32),
                   jax.ShapeDtypeStruct(cache_v.shape, F32)],
        compiler_params=_params(("arbitrary",)),
        name="swa_sample",
    )(sinks, q_a, kv_a, cache_k, cache_v, qnw, knw)


def _merge_ffn_kernel(x_ref, hm_ref, ha_ref, gab_ref, wa_ref, wb_ref, wo_ref, nw_ref, wg_ref, wu_ref, wd_ref,
                      y_ref):
    ga = jax.nn.sigmoid(gab_ref[:, 0:D_MODEL].astype(F32))
    gb = jax.nn.sigmoid(gab_ref[:, D_MODEL:].astype(F32))
    mix = ga * _dot(hm_ref[...], wa_ref[...]) + gb * _dot(ha_ref[...], wb_ref[...])
    x1 = x_ref[...] + _dot(mix.astype(BF16), wo_ref[...])
    ms = jnp.mean(x1 * x1, axis=-1, keepdims=True)
    hf = ((x1 * lax.rsqrt(ms + EPS)) * nw_ref[...]).astype(BF16)
    gate = _dot(hf, wg_ref[...])
    up = _dot(hf, wu_ref[...])
    act = (jax.nn.silu(gate) * up).astype(BF16)
    y_ref[...] = x1 + _dot(act, wd_ref[...])


def _merge_ffn(x2d, h_m, h_a, g_ab, wa, wb, wo, nw, wg, wu, wd, tm):
    n = x2d.shape[0]
    row = lambda w: pl.BlockSpec((tm, w), lambda i: (i, 0))
    return pl.pallas_call(
        _merge_ffn_kernel,
        grid=(n // tm,),
        in_specs=[row(D_MODEL), row(M_V_W), row(A_Q_W), row(2 * D_MODEL)]
                 + [_const_spec(w.shape) for w in (wa, wb, wo, nw, wg, wu, wd)],
        out_specs=row(D_MODEL),
        out_shape=jax.ShapeDtypeStruct((n, D_MODEL), F32),
        compiler_params=_params(("arbitrary",)),
        name="merge_ffn",
    )(x2d, h_m, h_a, g_ab, wa, wb, wo, nw, wg, wu, wd)


def _state_in(c, n):
    nb = c.shape[0]
    c2 = c.reshape(nb, M_HEADS * M_DK, M_DV)
    n2 = n.reshape(nb, M_HEADS * M_DK, 1)
    return jnp.concatenate([c2, n2, jnp.zeros((nb, M_HEADS * M_DK, M_DV - 1), F32)], axis=-1)


def _state_out(s):
    nb = s.shape[0]
    c = s[:, :, :M_DV].reshape(nb, M_HEADS, M_DK, M_DV)
    n = s[:, :, M_DV].reshape(nb, M_HEADS, M_DK)
    return c, n


def kernel(x_prompt, x_sample, state_mlstm_C, state_mlstm_n, state_mlstm_m, cache_swa_k, cache_swa_v,
           norm_mix_w, w_in, mlstm_i_bias, mlstm_f_bias, mlstm_norm_w, q_norm_w, k_norm_w, attn_sinks,
           w_branch_a, w_branch_b, w_out, norm_ffn_w, w_gate, w_up, w_down):
    depth = w_in.shape[0]
    assert depth == 1, "single trunk layer"
    l = 0
    bp, tp = x_prompt.shape[0], x_prompt.shape[1]
    bs, ts = x_sample.shape[0], x_sample.shape[1]
    assert tp % 512 == 0 and (bs * ts) % GROUP == 0 and GROUP % ts == 0 and ts & (ts - 1) == 0

    n1 = 2 * M_QK_W + 2 * M_V_W
    w1 = w_in[l][:, :n1].astype(BF16)
    wif = jnp.pad(w_in[l][:, n1:n1 + 2 * M_HEADS], ((0, 0), (0, LANES - 2 * M_HEADS))).astype(BF16)
    w2 = w_in[l][:, n1 + 2 * M_HEADS:].astype(BF16)
    nw_mix = norm_mix_w[l].reshape(1, D_MODEL)
    nw_ffn = norm_ffn_w[l].reshape(1, D_MODEL)
    bias = jnp.pad(jnp.concatenate([mlstm_i_bias[l], mlstm_f_bias[l]]), (0, LANES - 2 * M_HEADS)).reshape(1, LANES)
    nw_m = mlstm_norm_w[l].reshape(1, M_V_W)
    qnw = q_norm_w[l].reshape(1, A_HEAD_DIM)
    knw = k_norm_w[l].reshape(1, A_HEAD_DIM)
    sinks = attn_sinks[l]
    wa, wb, wo = w_branch_a[l].astype(BF16), w_branch_b[l].astype(BF16), w_out[l].astype(BF16)
    wg, wu, wd = w_gate[l].astype(BF16), w_up[l].astype(BF16), w_down[l].astype(BF16)

    xp = x_prompt.reshape(bp * tp, D_MODEL)
    wqt = w_in[l][:, n1 + 2 * M_HEADS:n1 + 2 * M_HEADS + A_Q_W].T.astype(BF16)
    head_of = jnp.arange(A_KV_W) // A_HEAD_DIM
    bd = (head_of[:, None] == head_of[None, :]).astype(BF16)
    qcol = (jnp.tile(q_norm_w[l], A_HEADS) * (A_HEAD_DIM ** -0.5)).reshape(A_Q_W, 1)
    krow = jnp.tile(k_norm_w[l], A_KV_HEADS).reshape(1, A_KV_W)
    qkv, o_m, gates, qat, kv_a, g_ab = _proj_prompt(xp, nw_mix, w1, wif, w2[:, A_Q_W:], wqt, bd, qcol, krow, 512)
    s0 = jnp.zeros((bp, M_HEADS * M_DK, S_W), F32)
    m0 = jnp.zeros((bp, 8, LANES), F32)
    h_m, s_p, m_p = _mlstm_prompt(qkv.reshape(bp, tp, -1), gates.reshape(bp, tp, LANES),
                                  o_m.reshape(bp, tp, M_V_W), bias, nw_m, s0, m0)
    h_a, kwin_p, vwin_p = _swa_prompt_t(qat, kv_a.reshape(bp, tp, 2 * A_KV_W), sinks, 4)
    yp = _merge_ffn(xp, h_m.reshape(bp * tp, M_V_W), h_a.reshape(bp * tp, A_Q_W), g_ab,
                    wa, wb, wo, nw_ffn, wg, wu, wd, 512).reshape(bp, tp, D_MODEL)
    c_p, n_p = _state_out(s_p)
    m_pr = m_p[:, 0, :M_HEADS]

    xs = x_sample.reshape(bs * ts, D_MODEL)
    tms = 512 if (bs * ts) % 512 == 0 else GROUP
    qkv, o_m, gates, q_a, kv_a, g_ab = _proj(xs, nw_mix, w1, wif, w2, F32, tms)
    s0 = _state_in(state_mlstm_C[l], state_mlstm_n[l])
    m_rep = jnp.pad(jnp.repeat(state_mlstm_m[l], ts, axis=0), ((0, 0), (0, LANES - M_HEADS)))
    h_m, s_s, mt_s = _mlstm_sample(qkv, gates, o_m, m_rep, bias, nw_m, s0, ts)
    wbuf = cache_swa_k.shape[2]
    h_a, kwin_s, vwin_s = _swa_sample(q_a, kv_a, cache_swa_k[l].reshape(bs, wbuf, A_KV_W),
                                      cache_swa_v[l].reshape(bs, wbuf, A_KV_W), sinks, qnw, knw, ts, 8)
    ys = _merge_ffn(xs, h_m, h_a, g_ab, wa, wb, wo, nw_ffn, wg, wu, wd, tms).reshape(bs, ts, D_MODEL)
    c_s, n_s = _state_out(s_s)
    m_s = mt_s.reshape(bs, ts, LANES)[:, ts - 1, :M_HEADS]

    kv5 = lambda a: a.reshape(a.shape[0], a.shape[1], A_KV_HEADS, A_HEAD_DIM)[None]
    return (yp, ys,
            c_p[None], n_p[None], m_pr[None], kv5(kwin_p), kv5(vwin_p),
            c_s[None], n_s[None], m_s[None], kv5(kwin_s), kv5(vwin_s))
```

```python
import functools

import jax
import jax.numpy as jnp
from jax import lax
from jax.experimental import pallas as pl
from jax.experimental.pallas import tpu as pltpu

F32 = jnp.float32
BF16 = jnp.bfloat16

D_MODEL = 1024
M_HEADS = 4
M_DK = 64
M_DV = 128
M_CHUNK = 64
M_QK_W = M_HEADS * M_DK
M_V_W = M_HEADS * M_DV
A_HEADS = 8
A_KV_HEADS = 2
A_HEAD_DIM = 64
A_GROUPS = A_HEADS // A_KV_HEADS
A_Q_W = A_HEADS * A_HEAD_DIM
A_KV_W = A_KV_HEADS * A_HEAD_DIM
WINDOW = 128
D_FF = 2816
EPS = 1e-6

LANES = 128
SUBLANES = 8
BF16_ROWS = 16
GROUP = 128
S_W = 2 * M_DV
S_ROWS = M_DV + BF16_ROWS
VMEM_LIMIT = 56 * 1024 * 1024
DK_SHIFT = M_DK.bit_length() - 1
WINDOW_SHIFT = WINDOW.bit_length() - 1

NT_DIMS = (((1,), (1,)), ((), ()))
TN_DIMS = (((0,), (0,)), ((), ()))


def _dot(a, b):
    return jnp.dot(a, b, preferred_element_type=F32)


def _dot_nt(a, b):
    return lax.dot_general(a, b, NT_DIMS, preferred_element_type=F32)


def _dot_tn(a, b):
    return lax.dot_general(a, b, TN_DIMS, preferred_element_type=F32)


def _const_spec(shape):
    nd = len(shape)
    return pl.BlockSpec(shape, lambda *_: (0,) * nd, pipeline_mode=pl.Buffered(1))


def _params(sem):
    return pltpu.CompilerParams(dimension_semantics=sem, vmem_limit_bytes=VMEM_LIMIT)


def _rms_rows(x, nw):
    ms = jnp.mean(x * x, axis=-1, keepdims=True)
    return (x * lax.rsqrt(ms + EPS)) * nw


def _proj_kernel(x_ref, nw_ref, w1_ref, wif_ref, w2_ref,
                 qkv_ref, o_ref, g_ref, qa_ref, kva_ref, gab_ref):
    hn = _rms_rows(x_ref[...], nw_ref[...]).astype(BF16)
    qkv_ref[...] = _dot(hn, w1_ref[:, 0:2 * M_QK_W + M_V_W]).astype(qkv_ref.dtype)
    o_ref[...] = _dot(hn, w1_ref[:, 2 * M_QK_W + M_V_W:]).astype(o_ref.dtype)
    g_ref[...] = _dot(hn, wif_ref[...])
    qa_ref[...] = _dot(hn, w2_ref[:, 0:A_Q_W]).astype(qa_ref.dtype)
    kva_ref[...] = _dot(hn, w2_ref[:, A_Q_W:A_Q_W + 2 * A_KV_W])
    gab_ref[...] = _dot(hn, w2_ref[:, A_Q_W + 2 * A_KV_W:]).astype(gab_ref.dtype)


def _proj(x2d, nw, w1, wif, w2, qkv_dtype, tm):
    n = x2d.shape[0]
    row = lambda w: pl.BlockSpec((tm, w), lambda i: (i, 0))
    w_qkv = 2 * M_QK_W + M_V_W
    return pl.pallas_call(
        _proj_kernel,
        grid=(n // tm,),
        in_specs=[row(D_MODEL), _const_spec(nw.shape), _const_spec(w1.shape),
                  _const_spec(wif.shape), _const_spec(w2.shape)],
        out_specs=[row(w_qkv), row(M_V_W), row(LANES), row(A_Q_W), row(2 * A_KV_W), row(2 * D_MODEL)],
        out_shape=[jax.ShapeDtypeStruct((n, w_qkv), qkv_dtype),
                   jax.ShapeDtypeStruct((n, M_V_W), BF16),
                   jax.ShapeDtypeStruct((n, LANES), F32),
                   jax.ShapeDtypeStruct((n, A_Q_W), BF16),
                   jax.ShapeDtypeStruct((n, 2 * A_KV_W), F32),
                   jax.ShapeDtypeStruct((n, 2 * D_MODEL), BF16)],
        compiler_params=_params(("arbitrary",)),
        name="proj",
    )(x2d, nw, w1, wif, w2)


def _proj_prompt_kernel(x_ref, nw_ref, wmt_ref, wgt_ref, wqt_ref, wn_ref, bd_ref, qcol_ref, krow_ref,
                        qvot_ref, gt_ref, qat_ref, km_ref, kva_ref, gab_ref):
    hn = _rms_rows(x_ref[...], nw_ref[...]).astype(BF16)
    n_blk = qat_ref.shape[0]

    def put(ref, rows, val):
        for c in range(n_blk):
            ref[c, rows, :] = val[:, c * LANES:(c + 1) * LANES].astype(ref.dtype)

    put(qvot_ref, slice(None), _dot_nt(wmt_ref[...], hn))
    put(gt_ref, slice(None), _dot_nt(wgt_ref[...], hn)[0:2 * M_HEADS])
    qt = _dot_nt(wqt_ref[...], hn)
    for h in range(A_HEADS):
        hs = slice(h * A_HEAD_DIM, (h + 1) * A_HEAD_DIM)
        blk = qt[hs]
        ssq_q = jnp.sum(blk * blk, axis=0, keepdims=True)
        put(qat_ref, hs, (blk * lax.rsqrt(ssq_q * (1.0 / A_HEAD_DIM) + EPS)) * qcol_ref[hs])
    km_ref[...] = _dot(hn, wn_ref[:, 0:M_QK_W]).astype(km_ref.dtype)
    gab_ref[...] = _dot(hn, wn_ref[:, M_QK_W + 2 * A_KV_W:]).astype(gab_ref.dtype)
    kv = _dot(hn, wn_ref[:, M_QK_W:M_QK_W + 2 * A_KV_W])
    k = kv[:, 0:A_KV_W]
    ksq = k * k
    hi = ksq.astype(BF16)
    lo = (ksq - hi.astype(F32)).astype(BF16)
    ssq = _dot(hi, bd_ref[...]) + _dot(lo, bd_ref[...])
    kva_ref[:, 0:A_KV_W] = (k * lax.rsqrt(ssq * (1.0 / A_HEAD_DIM) + EPS)) * krow_ref[...]
    kva_ref[:, A_KV_W:] = kv[:, A_KV_W:]


def _proj_prompt(x2d, nw, wmt, wgt, wqt, wn, bd, qcol, krow, tm):
    n = x2d.shape[0]
    row = lambda w: pl.BlockSpec((tm, w), lambda i: (i, 0))
    n_blk = tm // LANES
    slab = lambda r: pl.BlockSpec((n_blk, r, LANES), lambda i: (i, 0, 0))
    w_qvo = M_QK_W + 2 * M_V_W
    return pl.pallas_call(
        _proj_prompt_kernel,
        grid=(n // tm,),
        in_specs=[row(D_MODEL)] + [_const_spec(a.shape) for a in (nw, wmt, wgt, wqt, wn, bd, qcol, krow)],
        out_specs=[slab(w_qvo), slab(2 * M_HEADS), slab(A_Q_W), row(M_QK_W), row(2 * A_KV_W), row(2 * D_MODEL)],
        out_shape=[jax.ShapeDtypeStruct((n // LANES, w_qvo, LANES), BF16),
                   jax.ShapeDtypeStruct((n // LANES, 2 * M_HEADS, LANES), F32),
                   jax.ShapeDtypeStruct((n // LANES, A_Q_W, LANES), BF16),
                   jax.ShapeDtypeStruct((n, M_QK_W), BF16),
                   jax.ShapeDtypeStruct((n, 2 * A_KV_W), F32),
                   jax.ShapeDtypeStruct((n, 2 * D_MODEL), BF16)],
        compiler_params=_params(("arbitrary",)),
        name="proj_prompt",
    )(x2d, nw, wmt, wgt, wqt, wn, bd, qcol, krow)


def _split3(x):
    hi = x.astype(BF16)
    r1 = x - hi.astype(F32)
    mid = r1.astype(BF16)
    lo = (r1 - mid.astype(F32)).astype(BF16)
    return hi, mid, lo


def _log_sigmoid(x):
    return jnp.minimum(x, 0.0) - jnp.log1p(jnp.exp(-jnp.abs(x)))


def _chunk_masks(chunk_shift):
    s = lax.broadcasted_iota(jnp.int32, (GROUP, GROUP), 0)
    t = lax.broadcasted_iota(jnp.int32, (GROUP, GROUP), 1)
    same = (s >> chunk_shift) == (t >> chunk_shift)
    return same, same & (s <= t)


def _mlstm_prompt_kernel(qvot_ref, gt_ref, k_ref, bias_ref, nw_ref, s0_ref, m0_ref,
                         h_ref, st_ref, m_ref):
    step = pl.program_id(0)

    @pl.when(step == 0)
    def _():
        st_ref[...] = s0_ref[...]
        m_ref[...] = m0_ref[...]

    nb = k_ref.shape[0]
    n_chunks = GROUP // M_CHUNK
    same, causal = _chunk_masks(M_CHUNK.bit_length() - 1)
    cm_bf = jnp.where(causal, 1.0, 0.0).astype(BF16)
    lane = lax.broadcasted_iota(jnp.int32, (1, GROUP), 1)
    lane_head = lax.broadcasted_iota(jnp.int32, (1, M_QK_W), 1) >> DK_SHIFT
    row_head = lax.broadcasted_iota(jnp.int32, (M_QK_W, 1), 0) >> DK_SHIFT
    ones_rows = jnp.where(lax.broadcasted_iota(jnp.int32, (BF16_ROWS, GROUP), 0) == 0, 1.0, 0.0).astype(BF16)
    pad_rows = jnp.zeros((GROUP - SUBLANES, GROUP), F32)
    for b in range(nb):
        qt = qvot_ref[b, 0, 0:M_QK_W, :]
        ks = k_ref[b] * (M_DK ** -0.5)
        gt = gt_ref[b, 0] + bias_ref[...]
        lf = _log_sigmoid(gt)
        bt = sum(_dot(p, cm_bf) for p in _split3(lf))
        b4 = bt[M_HEADS:2 * M_HEADS]
        a4 = gt[0:M_HEADS] - b4
        a8 = jnp.concatenate([a4, a4], axis=0)
        a_cols = jnp.concatenate([a8, pad_rows], axis=0).T
        run_rows, chunk_rows, at_mats = [], [], []
        for h in range(M_HEADS):
            at = jnp.broadcast_to(a_cols[:, h:h + 1], (GROUP, GROUP))
            at_mats.append(at)
            run_rows.append(jnp.max(jnp.where(causal, at, -jnp.inf), axis=0, keepdims=True))
            chunk_rows.append(jnp.max(jnp.where(same, at, -jnp.inf), axis=0, keepdims=True))
        run4 = jnp.concatenate(run_rows, axis=0)
        chunk4 = jnp.concatenate(chunk_rows, axis=0)
        m_c = m_ref[b][0:M_HEADS, 0:1]
        m_prev = jnp.broadcast_to(m_c, (M_HEADS, GROUP))
        for ci in range(1, n_chunks):
            last = ci * M_CHUNK - 1
            m_c = b4[:, last:last + 1] + jnp.maximum(m_c, chunk4[:, last:last + 1])
            m_prev = jnp.where(lane >= ci * M_CHUNK, m_c, m_prev)
        big_m = jnp.maximum(m_prev, run4)
        m_last = jnp.maximum(m_prev, chunk4)
        w_inter = jnp.exp(m_prev - big_m)
        g_vec = jnp.exp(m_prev - m_last)
        w_last = jnp.exp(a4 - m_last)
        m_t = b4 + big_m
        e_neg_m = jnp.exp(-m_t)
        m_ref[b, 0:M_HEADS, :] = jnp.broadcast_to(m_t[:, GROUP - 1:GROUP], (M_HEADS, LANES))

        k_heads = [jnp.where(lane_head == h, ks, jnp.zeros_like(ks)) for h in range(M_HEADS)]
        q_heads = [jnp.where(row_head == h, qt, jnp.zeros_like(qt)) for h in range(M_HEADS)]
        sc_t = _dot(jnp.concatenate(k_heads, axis=0), qt)
        qw = jnp.concatenate(q_heads, axis=1)
        vta, intra = [], []
        for h in range(M_HEADS):
            w_t = jnp.where(causal, jnp.exp(at_mats[h] - big_m[h:h + 1]), 0.0)
            s_t = (sc_t[h * GROUP:(h + 1) * GROUP] * w_t).astype(BF16)
            vta_h = jnp.concatenate([qvot_ref[b, 0, M_QK_W + h * M_DV:M_QK_W + (h + 1) * M_DV, :], ones_rows], axis=0)
            vta.append(vta_h)
            intra.append(_dot(vta_h, s_t))

        inter = None
        for ci in range(n_chunks):
            st_old = st_ref[b]
            inter_c = _dot(st_old.astype(BF16), qw)
            inter = inter_c if inter is None else jnp.where(
                jnp.concatenate([lane] * M_HEADS, axis=1) >= ci * M_CHUNK, inter_c, inter)
            in_chunk = (lane >= ci * M_CHUNK) & (lane < (ci + 1) * M_CHUNK)
            d_st = sum(_dot((vta[h].astype(F32) * jnp.where(in_chunk, w_last[h:h + 1], 0.0)).astype(BF16), k_heads[h])
                       for h in range(M_HEADS))
            g_row = jnp.zeros((1, M_QK_W), F32)
            for h in range(M_HEADS):
                g_row = jnp.where(lane_head == h, g_vec[h:h + 1, ci * M_CHUNK:ci * M_CHUNK + 1], g_row)
            st_ref[b] = g_row * st_old + d_st

        outs = []
        for h in range(M_HEADS):
            out_t = inter[:, h * GROUP:(h + 1) * GROUP] * w_inter[h:h + 1] + intra[h]
            hh = out_t[0:M_DV] / jnp.maximum(jnp.abs(out_t[M_DV:M_DV + 1]), e_neg_m[h:h + 1])
            ms = jnp.mean(hh * hh, axis=0, keepdims=True)
            hn = (hh * lax.rsqrt(ms + EPS)) * nw_ref[h * M_DV:(h + 1) * M_DV]
            o_t = qvot_ref[b, 0, M_QK_W + M_V_W + h * M_DV:M_QK_W + M_V_W + (h + 1) * M_DV, :]
            outs.append(hn * jax.nn.sigmoid(o_t.astype(F32)))
        h_ref[b] = jnp.concatenate(outs, axis=0).T.astype(h_ref.dtype)


def _mlstm_prompt(qvot, gt, k_m, bias_col, nw_col, s0, m0):
    nb, nblk = qvot.shape[0], qvot.shape[1]
    full = lambda a: pl.BlockSpec(a.shape, lambda i: (0,) * a.ndim)
    slab = lambda a: pl.BlockSpec((nb, 1) + a.shape[2:], lambda i: (0, i, 0, 0))
    return pl.pallas_call(
        _mlstm_prompt_kernel,
        grid=(nblk,),
        in_specs=[slab(qvot), slab(gt), pl.BlockSpec((nb, GROUP, M_QK_W), lambda i: (0, i, 0)),
                  full(bias_col), full(nw_col), full(s0), full(m0)],
        out_specs=[pl.BlockSpec((nb, GROUP, M_V_W), lambda i: (0, i, 0)), full(s0), full(m0)],
        out_shape=[jax.ShapeDtypeStruct((nb, nblk * GROUP, M_V_W), BF16),
                   jax.ShapeDtypeStruct(s0.shape, F32),
                   jax.ShapeDtypeStruct(m0.shape, F32)],
        compiler_params=_params(("arbitrary",)),
        name="mlstm_prompt",
    )(qvot, gt, k_m, bias_col, nw_col, s0, m0)


def _gate_prep(gb, chunk_shift):
    same, causal_t = _chunk_masks(chunk_shift)
    r = lax.broadcasted_iota(jnp.int32, (GROUP, GROUP), 0)
    c = lax.broadcasted_iota(jnp.int32, (GROUP, GROUP), 1)
    causal = same & (c <= r)
    lf = _log_sigmoid(gb)
    cm = jnp.where(causal, 1.0, 0.0).astype(BF16)
    b_col = sum(_dot(cm, p) for p in _split3(lf))
    gt8 = gb.T[0:SUBLANES]
    lft8 = _log_sigmoid(gt8)
    cm_t = jnp.where(causal_t, 1.0, 0.0).astype(BF16)
    bt8 = sum(_dot(p, cm_t) for p in _split3(lft8))
    return same, causal, b_col, gt8, bt8


def _head_maxes(h, gb, b_col, gt8, bt8, same, causal):
    a_row = gt8[h:h + 1] - bt8[M_HEADS + h:M_HEADS + h + 1]
    bc = b_col[:, M_HEADS + h:M_HEADS + h + 1]
    a_col = gb[:, h:h + 1] - bc
    arow = jnp.broadcast_to(a_row, (GROUP, GROUP))
    run_max = jnp.max(jnp.where(causal, arow, -jnp.inf), axis=1, keepdims=True)
    chunk_max = jnp.max(jnp.where(same, arow, -jnp.inf), axis=1, keepdims=True)
    return arow, a_col, bc, run_max, chunk_max


def _head_weights(arow, a_col, bc, run_max, chunk_max, m_prev, causal):
    big_m = jnp.maximum(m_prev, run_max)
    m_last = jnp.maximum(m_prev, chunk_max)
    w = jnp.where(causal, jnp.exp(arow - big_m), 0.0)
    w_inter = jnp.exp(m_prev - big_m)
    g_vec = jnp.exp(m_prev - m_last)
    w_last = jnp.exp(a_col - m_last)
    m_t = bc + big_m
    return w, w_inter, g_vec, w_last, m_t


def _head_masks():
    lane_head = lax.broadcasted_iota(jnp.int32, (1, M_QK_W), 1) >> DK_SHIFT
    return [lane_head == h for h in range(M_HEADS)]


def _v_aug(v_all, h):
    ones_col = jnp.where(lax.broadcasted_iota(jnp.int32, (v_all.shape[0], M_DV), 1) == 0, 1.0, 0.0)
    return jnp.concatenate([v_all[:, h * M_DV:(h + 1) * M_DV], ones_col.astype(v_all.dtype)], axis=1)


def _head_out(out_aug, m_t, nw_h, o_h):
    num = out_aug[:, :M_DV]
    den = out_aug[:, M_DV:M_DV + 1]
    hh = num / jnp.maximum(jnp.abs(den), jnp.exp(-m_t))
    return _rms_rows(hh, nw_h) * jax.nn.sigmoid(o_h.astype(F32))


def _mlstm_sample_kernel(seq_len, qkv_ref, g_ref, o_ref, mrep_ref, bias_ref, nw_ref, s0_ref,
                         h_ref, s_ref, mt_ref):
    n_seq = GROUP // seq_len
    shift = seq_len.bit_length() - 1
    hm = _head_masks()
    lane = lax.broadcasted_iota(jnp.int32, (GROUP, LANES), 1)
    q_all = qkv_ref[:, 0:M_QK_W]
    ks = qkv_ref[:, M_QK_W:2 * M_QK_W] * (M_DK ** -0.5)
    v_all = qkv_ref[:, 2 * M_QK_W:]
    gb = g_ref[...] + bias_ref[...]
    same, causal, b_col, gt8, bt8 = _gate_prep(gb, shift)
    q_bf = q_all.astype(BF16)
    q_stack = jnp.concatenate([jnp.where(hm[h], q_bf, jnp.zeros_like(q_bf)) for h in range(M_HEADS)], axis=0)
    sc = _dot_nt(q_stack, ks.astype(BF16))
    m_rep = mrep_ref[...]
    qw, kw, va, gv, intra, m_ts = [], [], [], [], [], []
    mt_mat = jnp.zeros((GROUP, LANES), F32)
    for h in range(M_HEADS):
        arow, a_col, bc, run_max, chunk_max = _head_maxes(h, gb, b_col, gt8, bt8, same, causal)
        w, w_inter, g_vec, w_last, m_t = _head_weights(arow, a_col, bc, run_max, chunk_max,
                                                       m_rep[:, h:h + 1], causal)
        va_h = _v_aug(v_all, h)
        s_h = (sc[h * GROUP:(h + 1) * GROUP] * w).astype(BF16)
        intra.append(_dot(s_h, va_h.astype(BF16)))
        qw.append(jnp.where(hm[h], q_all, 0.0) * w_inter)
        kw.append(jnp.where(hm[h], ks, 0.0) * w_last)
        va.append(va_h)
        gv.append(g_vec)
        m_ts.append(m_t)
        mt_mat = jnp.where(lane == h, m_t, mt_mat)
    mt_ref[...] = mt_mat

    zpad = jnp.zeros((BF16_ROWS - seq_len, M_QK_W), F32) if seq_len < BF16_ROWS else None
    pad = (lambda x: jnp.concatenate([x, zpad], axis=0)) if zpad is not None else (lambda x: x)
    inter = [[] for _ in range(M_HEADS)]
    for s in range(n_seq):
        rs = slice(s * seq_len, (s + 1) * seq_len)
        s_old = s0_ref[s]
        qw_s = jnp.concatenate([pad(qw[h][rs]) for h in range(M_HEADS)], axis=0).astype(BF16)
        inter_s = _dot(qw_s, s_old.astype(BF16))
        rows_h = qw_s.shape[0] // M_HEADS
        d_s = sum(_dot_tn(pad(kw[h][rs]).astype(BF16), pad(va[h][rs]).astype(BF16)) for h in range(M_HEADS))
        for h in range(M_HEADS):
            hs = slice(h * M_DK, (h + 1) * M_DK)
            g = gv[h][s * seq_len:s * seq_len + 1]
            s_ref[s, hs, :] = g * s_old[hs] + d_s[hs]
            inter[h].append(inter_s[h * rows_h:h * rows_h + seq_len])
    for h in range(M_HEADS):
        out_aug = jnp.concatenate(inter[h], axis=0) + intra[h]
        vs = slice(h * M_DV, (h + 1) * M_DV)
        h_ref[:, vs] = _head_out(out_aug, m_ts[h], nw_ref[:, vs], o_ref[:, vs]).astype(h_ref.dtype)


def _mlstm_sample(qkv, gates, o_m, m_rep, bias, nw, s0, seq_len):
    n = qkv.shape[0]
    n_seq = GROUP // seq_len
    row = lambda w: pl.BlockSpec((GROUP, w), lambda i: (i, 0))
    full = lambda a: pl.BlockSpec(a.shape, lambda i: (0,) * a.ndim)
    st = pl.BlockSpec((n_seq,) + s0.shape[1:], lambda i: (i, 0, 0))
    return pl.pallas_call(
        functools.partial(_mlstm_sample_kernel, seq_len),
        grid=(n // GROUP,),
        in_specs=[row(qkv.shape[1]), row(LANES), row(M_V_W), row(LANES), full(bias), full(nw), st],
        out_specs=[row(M_V_W), st, row(LANES)],
        out_shape=[jax.ShapeDtypeStruct((n, M_V_W), BF16),
                   jax.ShapeDtypeStruct(s0.shape, F32),
                   jax.ShapeDtypeStruct((n, LANES), F32)],
        compiler_params=_params(("arbitrary",)),
        name="mlstm_sample",
    )(qkv, gates, o_m, m_rep, bias, nw, s0)


def _swa_prompt_kernel(nqb, sink_ref, qt_ref, kv_ref, kvp_ref, h_ref, kwin_ref, vwin_ref):
    j = pl.program_id(1)
    hd = A_HEAD_DIM
    nq = A_GROUPS * WINDOW
    si = lax.broadcasted_iota(jnp.int32, (2 * WINDOW, nq), 0)
    qi = lax.broadcasted_iota(jnp.int32, (2 * WINDOW, nq), 1) & (WINDOW - 1)
    local = ((si < WINDOW) & (si > qi)) | ((si >= WINDOW) & (si - WINDOW <= qi))
    first = local & ((j > 0) | (si >= WINDOW))
    lane_grp = lax.broadcasted_iota(jnp.int32, (1, nq), 1) >> WINDOW_SHIFT
    kv_blocks = [kvp_ref[0]] + [kv_ref[0, i * WINDOW:(i + 1) * WINDOW, :] for i in range(nqb)]
    k_bf = [blk[:, 0:A_KV_W].astype(BF16) for blk in kv_blocks]
    vt_bf = [blk[:, A_KV_W:].T.astype(BF16) for blk in kv_blocks]
    zeros = jnp.zeros((hd, nq), BF16)
    sinks = []
    for kvh in range(A_KV_HEADS):
        sk = jnp.zeros((1, nq), F32)
        for g in range(A_GROUPS):
            sk = jnp.where(lane_grp == g, sink_ref[kvh * A_GROUPS + g], sk)
        sinks.append(sk)
    for qb in range(nqb):
        kk = jnp.concatenate([k_bf[qb], k_bf[qb + 1]], axis=0)
        vt = jnp.concatenate([vt_bf[qb], vt_bf[qb + 1]], axis=1)
        mask = local if qb > 0 else first
        pieces = []
        for kvh in range(A_KV_HEADS):
            q4t = jnp.concatenate(
                [qt_ref[qb, (kvh * A_GROUPS + g) * hd:(kvh * A_GROUPS + g + 1) * hd, :] for g in range(A_GROUPS)],
                axis=1)
            wq = jnp.concatenate([q4t, zeros] if kvh == 0 else [zeros, q4t], axis=0)
            s = jnp.where(mask, _dot(kk, wq), -jnp.inf)
            sk = sinks[kvh]
            mx = jnp.maximum(jnp.max(s, axis=0, keepdims=True), sk)
            p = jnp.exp(s - mx)
            den = jnp.sum(p, axis=0, keepdims=True) + jnp.exp(sk - mx)
            ot = _dot(vt, p.astype(BF16))[kvh * hd:(kvh + 1) * hd] / den
            pieces += [ot[:, g * WINDOW:(g + 1) * WINDOW] for g in range(A_GROUPS)]
        h_t = jnp.concatenate(pieces, axis=0)
        h_ref[0, qb * WINDOW:(qb + 1) * WINDOW, :] = h_t.T.astype(h_ref.dtype)
    kwin_ref[0] = kv_blocks[nqb][:, 0:A_KV_W]
    vwin_ref[0] = kv_blocks[nqb][:, A_KV_W:]


def _swa_prompt(qat, kv_a, sinks, nqb):
    nb, t = kv_a.shape[0], kv_a.shape[1]
    tb = nqb * WINDOW
    nj = t // tb
    return pl.pallas_call(
        functools.partial(_swa_prompt_kernel, nqb),
        grid=(nb, nj),
        in_specs=[pl.BlockSpec(memory_space=pltpu.SMEM),
                  pl.BlockSpec((nqb, A_Q_W, WINDOW), lambda b, j: (b * nj + j, 0, 0)),
                  pl.BlockSpec((1, tb, 2 * A_KV_W), lambda b, j: (b, j, 0)),
                  pl.BlockSpec((1, WINDOW, 2 * A_KV_W), lambda b, j: (b, jnp.maximum(j * nqb - 1, 0), 0))],
        out_specs=[pl.BlockSpec((1, tb, A_Q_W), lambda b, j: (b, j, 0)),
                   pl.BlockSpec((1, WINDOW, A_KV_W), lambda b, j: (b, 0, 0)),
                   pl.BlockSpec((1, WINDOW, A_KV_W), lambda b, j: (b, 0, 0))],
        out_shape=[jax.ShapeDtypeStruct((nb, t, A_Q_W), BF16),
                   jax.ShapeDtypeStruct((nb, WINDOW, A_KV_W), F32),
                   jax.ShapeDtypeStruct((nb, WINDOW, A_KV_W), F32)],
        compiler_params=_params(("arbitrary", "arbitrary")),
        name="swa_prompt",
    )(sinks, qat, kv_a, kv_a)


def _swa_sample_kernel(seq_len, n_seq, sink_ref, q_ref, kv_ref, ck_ref, cv_ref, qnw_ref, knw_ref,
                       h_ref, kwin_ref, vwin_ref):
    hd = A_HEAD_DIM
    wb = ck_ref.shape[1]
    nrow = A_GROUPS * seq_len
    ti = lax.broadcasted_iota(jnp.int32, (nrow, wb), 0) & (seq_len - 1)
    si = lax.broadcasted_iota(jnp.int32, (nrow, wb), 1)
    mask_c = (ti + wb - si) < WINDOW
    ti2 = lax.broadcasted_iota(jnp.int32, (nrow, BF16_ROWS), 0) & (seq_len - 1)
    si2 = lax.broadcasted_iota(jnp.int32, (nrow, BF16_ROWS), 1)
    mask_n = (si2 <= ti2) & (si2 < seq_len)
    row_grp = lax.broadcasted_iota(jnp.int32, (nrow, 1), 0) >> (seq_len.bit_length() - 1)
    zpad = jnp.zeros((BF16_ROWS - seq_len, hd), F32)
    for s in range(n_seq):
        rs = slice(s * seq_len, (s + 1) * seq_len)
        for kvh in range(A_KV_HEADS):
            ks = slice(kvh * hd, (kvh + 1) * hd)
            k_new = _rms_rows(kv_ref[rs, ks], knw_ref[...])
            v_new = kv_ref[rs, A_KV_W + kvh * hd:A_KV_W + (kvh + 1) * hd]
            kwin_ref[s, wb - seq_len:wb, ks] = k_new
            vwin_ref[s, wb - seq_len:wb, ks] = v_new
            sk = jnp.zeros((nrow, 1), F32)
            for g in range(A_GROUPS):
                sk = jnp.where(row_grp == g, sink_ref[kvh * A_GROUPS + g], sk)
            q4 = jnp.concatenate(
                [_rms_rows(q_ref[rs, (kvh * A_GROUPS + g) * hd:(kvh * A_GROUPS + g + 1) * hd].astype(F32),
                           qnw_ref[...]) for g in range(A_GROUPS)], axis=0).astype(BF16)
            k_c = ck_ref[s, :, ks].astype(BF16)
            v_c = cv_ref[s, :, ks].astype(BF16)
            k_n = jnp.concatenate([k_new, zpad], axis=0).astype(BF16)
            v_n = jnp.concatenate([v_new, zpad], axis=0).astype(BF16)
            s1 = jnp.where(mask_c, _dot_nt(q4, k_c) * (hd ** -0.5), -jnp.inf)
            s2 = jnp.where(mask_n, _dot_nt(q4, k_n) * (hd ** -0.5), -jnp.inf)
            mx = jnp.maximum(jnp.maximum(jnp.max(s1, axis=-1, keepdims=True),
                                         jnp.max(s2, axis=-1, keepdims=True)), sk)
            p1 = jnp.exp(s1 - mx)
            p2 = jnp.exp(s2 - mx)
            den = jnp.sum(p1, axis=-1, keepdims=True) + jnp.sum(p2, axis=-1, keepdims=True) + jnp.exp(sk - mx)
            o = (_dot(p1.astype(BF16), v_c) + _dot(p2.astype(BF16), v_n)) / den
            for g in range(A_GROUPS):
                head = kvh * A_GROUPS + g
                h_ref[rs, head * hd:(head + 1) * hd] = o[g * seq_len:(g + 1) * seq_len].astype(h_ref.dtype)
        kwin_ref[s, 0:wb - seq_len, :] = ck_ref[s, seq_len:wb, :]
        vwin_ref[s, 0:wb - seq_len, :] = cv_ref[s, seq_len:wb, :]


def _swa_sample(q_a, kv_a, cache_k, cache_v, sinks, qnw, knw, seq_len, n_seq):
    n = q_a.shape[0]
    nb, wb = cache_k.shape[0], cache_k.shape[1]
    rows = n_seq * seq_len
    row = lambda w: pl.BlockSpec((rows, w), lambda i: (i, 0))
    cache = pl.BlockSpec((n_seq, wb, A_KV_W), lambda i: (i, 0, 0))
    return pl.pallas_call(
        functools.partial(_swa_sample_kernel, seq_len, n_seq),
        grid=(nb // n_seq,),
        in_specs=[pl.BlockSpec(memory_space=pltpu.SMEM), row(A_Q_W), row(2 * A_KV_W), cache, cache,
                  pl.BlockSpec(qnw.shape, lambda i: (0, 0)), pl.BlockSpec(knw.shape, lambda i: (0, 0))],
        out_specs=[row(A_Q_W), cache, cache],
        out_shape=[jax.ShapeDtypeStruct((n, A_Q_W), BF16),
                   jax.ShapeDtypeStruct(cache_k.shape, F32),
                   jax.ShapeDtypeStruct(cache_v.shape, F32)],
        compiler_params=_params(("arbitrary",)),
        name="swa_sample",
    )(sinks, q_a, kv_a, cache_k, cache_v, qnw, knw)


def _merge_ffn_kernel(x_ref, hm_ref, ha_ref, gab_ref, wa_ref, wb_ref, wo_ref, nw_ref, wg_ref, wu_ref, wd_ref,
                      y_ref):
    ga = jax.nn.sigmoid(gab_ref[:, 0:D_MODEL].astype(F32))
    gb = jax.nn.sigmoid(gab_ref[:, D_MODEL:].astype(F32))
    mix = ga * _dot(hm_ref[...], wa_ref[...]) + gb * _dot(ha_ref[...], wb_ref[...])
    x1 = x_ref[...] + _dot(mix.astype(BF16), wo_ref[...])
    hf = _rms_rows(x1, nw_ref[...]).astype(BF16)
    gate = _dot(hf, wg_ref[...])
    up = _dot(hf, wu_ref[...])
    act = (jax.nn.silu(gate) * up).astype(BF16)
    y_ref[...] = x1 + _dot(act, wd_ref[...])


def _merge_ffn(x2d, h_m, h_a, g_ab, wa, wb, wo, nw, wg, wu, wd, tm):
    n = x2d.shape[0]
    row = lambda w: pl.BlockSpec((tm, w), lambda i: (i, 0))
    return pl.pallas_call(
        _merge_ffn_kernel,
        grid=(n // tm,),
        in_specs=[row(D_MODEL), row(M_V_W), row(A_Q_W), row(2 * D_MODEL)]
                 + [_const_spec(w.shape) for w in (wa, wb, wo, nw, wg, wu, wd)],
        out_specs=row(D_MODEL),
        out_shape=jax.ShapeDtypeStruct((n, D_MODEL), F32),
        compiler_params=_params(("arbitrary",)),
        name="merge_ffn",
    )(x2d, h_m, h_a, g_ab, wa, wb, wo, nw, wg, wu, wd)


def _state_in(c, n):
    nb = c.shape[0]
    c2 = c.reshape(nb, M_HEADS * M_DK, M_DV)
    n2 = n.reshape(nb, M_HEADS * M_DK, 1)
    return jnp.concatenate([c2, n2, jnp.zeros((nb, M_HEADS * M_DK, M_DV - 1), F32)], axis=-1)


def _state_out(s):
    nb = s.shape[0]
    c = s[:, :, :M_DV].reshape(nb, M_HEADS, M_DK, M_DV)
    n = s[:, :, M_DV].reshape(nb, M_HEADS, M_DK)
    return c, n


def kernel(x_prompt, x_sample, state_mlstm_C, state_mlstm_n, state_mlstm_m, cache_swa_k, cache_swa_v,
           norm_mix_w, w_in, mlstm_i_bias, mlstm_f_bias, mlstm_norm_w, q_norm_w, k_norm_w, attn_sinks,
           w_branch_a, w_branch_b, w_out, norm_ffn_w, w_gate, w_up, w_down):
    depth = w_in.shape[0]
    assert depth == 1, "single trunk layer"
    l = 0
    bp, tp = x_prompt.shape[0], x_prompt.shape[1]
    bs, ts = x_sample.shape[0], x_sample.shape[1]
    assert tp % 512 == 0 and (bs * ts) % GROUP == 0 and GROUP % ts == 0 and ts & (ts - 1) == 0

    w = w_in[l]
    c_qm, c_km, c_vm, c_om = 0, M_QK_W, 2 * M_QK_W, 2 * M_QK_W + M_V_W
    c_g = 2 * M_QK_W + 2 * M_V_W
    c_qa = c_g + 2 * M_HEADS
    c_ka = c_qa + A_Q_W
    nw_mix = norm_mix_w[l].reshape(1, D_MODEL)
    nw_ffn = norm_ffn_w[l].reshape(1, D_MODEL)
    gate_bias = jnp.concatenate([mlstm_i_bias[l], mlstm_f_bias[l]])
    sinks = attn_sinks[l]
    wa, wb, wo = w_branch_a[l].astype(BF16), w_branch_b[l].astype(BF16), w_out[l].astype(BF16)
    wg, wu, wd = w_gate[l].astype(BF16), w_up[l].astype(BF16), w_down[l].astype(BF16)

    wmt = jnp.concatenate([w[:, c_qm:c_km], w[:, c_vm:c_g]], axis=1).T.astype(BF16)
    wgt = jnp.pad(w[:, c_g:c_qa], ((0, 0), (0, BF16_ROWS - 2 * M_HEADS))).T.astype(BF16)
    wqt = w[:, c_qa:c_ka].T.astype(BF16)
    wn = jnp.concatenate([w[:, c_km:c_vm], w[:, c_ka:]], axis=1).astype(BF16)
    head_of = jnp.arange(A_KV_W) // A_HEAD_DIM
    bd = (head_of[:, None] == head_of[None, :]).astype(BF16)
    qcol = (jnp.tile(q_norm_w[l], A_HEADS) * (A_HEAD_DIM ** -0.5)).reshape(A_Q_W, 1)
    krow = jnp.tile(k_norm_w[l], A_KV_HEADS).reshape(1, A_KV_W)
    xp = x_prompt.reshape(bp * tp, D_MODEL)
    qvot, gt, qat, k_m, kv_a, g_ab = _proj_prompt(xp, nw_mix, wmt, wgt, wqt, wn, bd, qcol, krow, 512)
    nblk = tp // GROUP
    s0 = jnp.zeros((bp, S_ROWS, M_QK_W), F32)
    m0 = jnp.zeros((bp, SUBLANES, LANES), F32)
    h_m, st_p, m_p = _mlstm_prompt(qvot.reshape(bp, nblk, -1, LANES), gt.reshape(bp, nblk, -1, LANES),
                                   k_m.reshape(bp, tp, M_QK_W), gate_bias.reshape(2 * M_HEADS, 1),
                                   mlstm_norm_w[l].reshape(M_V_W, 1), s0, m0)
    h_a, kwin_p, vwin_p = _swa_prompt(qat, kv_a.reshape(bp, tp, 2 * A_KV_W), sinks, 4)
    yp = _merge_ffn(xp, h_m.reshape(bp * tp, M_V_W), h_a.reshape(bp * tp, A_Q_W), g_ab,
                    wa, wb, wo, nw_ffn, wg, wu, wd, 512).reshape(bp, tp, D_MODEL)
    c_p = jnp.swapaxes(st_p[:, :M_DV, :], 1, 2).reshape(bp, M_HEADS, M_DK, M_DV)
    n_p = st_p[:, M_DV, :].reshape(bp, M_HEADS, M_DK)
    m_pr = m_p[:, :M_HEADS, 0]

    w1 = w[:, :c_g].astype(BF16)
    wif = jnp.pad(w[:, c_g:c_qa], ((0, 0), (0, LANES - 2 * M_HEADS))).astype(BF16)
    w2 = w[:, c_qa:].astype(BF16)
    bias = jnp.pad(gate_bias, (0, LANES - 2 * M_HEADS)).reshape(1, LANES)
    nw_m = mlstm_norm_w[l].reshape(1, M_V_W)
    qnw = q_norm_w[l].reshape(1, A_HEAD_DIM)
    knw = k_norm_w[l].reshape(1, A_HEAD_DIM)
    xs = x_sample.reshape(bs * ts, D_MODEL)
    tms = 512 if (bs * ts) % 512 == 0 else GROUP
    qkv, o_m, gates, q_a, kv_a, g_ab = _proj(xs, nw_mix, w1, wif, w2, F32, tms)
    s0 = _state_in(state_mlstm_C[l], state_mlstm_n[l])
    m_rep = jnp.pad(jnp.repeat(state_mlstm_m[l], ts, axis=0), ((0, 0), (0, LANES - M_HEADS)))
    h_m, s_s, mt_s = _mlstm_sample(qkv, gates, o_m, m_rep, bias, nw_m, s0, ts)
    wbuf = cache_swa_k.shape[2]
    h_a, kwin_s, vwin_s = _swa_sample(q_a, kv_a, cache_swa_k[l].reshape(bs, wbuf, A_KV_W),
                                      cache_swa_v[l].reshape(bs, wbuf, A_KV_W), sinks, qnw, knw, ts, 8)
    ys = _merge_ffn(xs, h_m, h_a, g_ab, wa, wb, wo, nw_ffn, wg, wu, wd, tms).reshape(bs, ts, D_MODEL)
    c_s, n_s = _state_out(s_s)
    m_s = mt_s.reshape(bs, ts, LANES)[:, ts - 1, :M_HEADS]

    kv5 = lambda a: a.reshape(a.shape[0], a.shape[1], A_KV_HEADS, A_HEAD_DIM)[None]
    return (yp, ys,
            c_p[None], n_p[None], m_pr[None], kv5(kwin_p), kv5(vwin_p),
            c_s[None], n_s[None], m_s[None], kv5(kwin_s), kv5(vwin_s))
```

```python
import functools

import jax
import jax.numpy as jnp
from jax import lax
from jax.experimental import pallas as pl
from jax.experimental.pallas import tpu as pltpu

F32 = jnp.float32
BF16 = jnp.bfloat16

D_MODEL = 1024
M_HEADS = 4
M_DK = 64
M_DV = 128
M_CHUNK = 64
M_QK_W = M_HEADS * M_DK
M_V_W = M_HEADS * M_DV
A_HEADS = 8
A_KV_HEADS = 2
A_HEAD_DIM = 64
A_GROUPS = A_HEADS // A_KV_HEADS
A_Q_W = A_HEADS * A_HEAD_DIM
A_KV_W = A_KV_HEADS * A_HEAD_DIM
WINDOW = 128
D_FF = 2816
EPS = 1e-6

LANES = 128
SUBLANES = 8
BF16_ROWS = 16
GROUP = 128
S_ROWS = M_DV + BF16_ROWS
VMEM_LIMIT = 56 * 1024 * 1024
DK_SHIFT = M_DK.bit_length() - 1
WINDOW_SHIFT = WINDOW.bit_length() - 1
HALF = LANES // 2
assert A_HEAD_DIM == HALF and A_KV_W == LANES, "attention head pairs share one lane-width"

NT_DIMS = (((1,), (1,)), ((), ()))


def _dot(a, b):
    return jnp.dot(a, b, preferred_element_type=F32)


def _dot_nt(a, b):
    return lax.dot_general(a, b, NT_DIMS, preferred_element_type=F32)


def _const_spec(shape):
    nd = len(shape)
    return pl.BlockSpec(shape, lambda *_: (0,) * nd, pipeline_mode=pl.Buffered(1))


def _params(sem):
    return pltpu.CompilerParams(dimension_semantics=sem, vmem_limit_bytes=VMEM_LIMIT)


def _rms_rows(x, nw):
    ms = jnp.mean(x * x, axis=-1, keepdims=True)
    return (x * lax.rsqrt(ms + EPS)) * nw


def _proj_kernel(x_ref, nw_ref, wmt_ref, wgt_ref, wqt_ref, wn_ref, bd_ref, qcol_ref, krow_ref,
                 qvot_ref, gt_ref, qat_ref, km_ref, kva_ref, gab_ref):
    hn = _rms_rows(x_ref[...], nw_ref[...]).astype(BF16)
    n_blk = qat_ref.shape[0]

    def put(ref, rows, val):
        for c in range(n_blk):
            ref[c, rows, :] = val[:, c * LANES:(c + 1) * LANES].astype(ref.dtype)

    put(qvot_ref, slice(None), _dot_nt(wmt_ref[...], hn))
    put(gt_ref, slice(None), _dot_nt(wgt_ref[...], hn)[0:2 * M_HEADS])
    qt = _dot_nt(wqt_ref[...], hn)
    for h in range(A_HEADS):
        hs = slice(h * A_HEAD_DIM, (h + 1) * A_HEAD_DIM)
        blk = qt[hs]
        ssq_q = jnp.sum(blk * blk, axis=0, keepdims=True)
        put(qat_ref, hs, (blk * lax.rsqrt(ssq_q * (1.0 / A_HEAD_DIM) + EPS)) * qcol_ref[hs])
    km_ref[...] = _dot(hn, wn_ref[:, 0:M_QK_W]).astype(km_ref.dtype)
    gab_ref[...] = _dot(hn, wn_ref[:, M_QK_W + 2 * A_KV_W:]).astype(gab_ref.dtype)
    kv = _dot(hn, wn_ref[:, M_QK_W:M_QK_W + 2 * A_KV_W])
    k = kv[:, 0:A_KV_W]
    ksq = k * k
    hi = ksq.astype(BF16)
    lo = (ksq - hi.astype(F32)).astype(BF16)
    ssq = _dot(hi, bd_ref[...]) + _dot(lo, bd_ref[...])
    kva_ref[:, 0:A_KV_W] = (k * lax.rsqrt(ssq * (1.0 / A_HEAD_DIM) + EPS)) * krow_ref[...]
    kva_ref[:, A_KV_W:] = kv[:, A_KV_W:]


def _proj(x2d, nw, wmt, wgt, wqt, wn, bd, qcol, krow, tm):
    n = x2d.shape[0]
    row = lambda w: pl.BlockSpec((tm, w), lambda i: (i, 0))
    n_blk = tm // LANES
    slab = lambda r: pl.BlockSpec((n_blk, r, LANES), lambda i: (i, 0, 0))
    w_qvo = M_QK_W + 2 * M_V_W
    return pl.pallas_call(
        _proj_kernel,
        grid=(n // tm,),
        in_specs=[row(D_MODEL)] + [_const_spec(a.shape) for a in (nw, wmt, wgt, wqt, wn, bd, qcol, krow)],
        out_specs=[slab(w_qvo), slab(2 * M_HEADS), slab(A_Q_W), row(M_QK_W), row(2 * A_KV_W), row(2 * D_MODEL)],
        out_shape=[jax.ShapeDtypeStruct((n // LANES, w_qvo, LANES), BF16),
                   jax.ShapeDtypeStruct((n // LANES, 2 * M_HEADS, LANES), F32),
                   jax.ShapeDtypeStruct((n // LANES, A_Q_W, LANES), BF16),
                   jax.ShapeDtypeStruct((n, M_QK_W), BF16),
                   jax.ShapeDtypeStruct((n, 2 * A_KV_W), F32),
                   jax.ShapeDtypeStruct((n, 2 * D_MODEL), BF16)],
        compiler_params=_params(("arbitrary",)),
        name="proj",
    )(x2d, nw, wmt, wgt, wqt, wn, bd, qcol, krow)


def _split3(x):
    hi = x.astype(BF16)
    r1 = x - hi.astype(F32)
    mid = r1.astype(BF16)
    lo = (r1 - mid.astype(F32)).astype(BF16)
    return hi, mid, lo


def _log_sigmoid(x):
    return jnp.minimum(x, 0.0) - jnp.log1p(jnp.exp(-jnp.abs(x)))


def _chunk_masks(chunk_shift):
    s = lax.broadcasted_iota(jnp.int32, (GROUP, GROUP), 0)
    t = lax.broadcasted_iota(jnp.int32, (GROUP, GROUP), 1)
    same = (s >> chunk_shift) == (t >> chunk_shift)
    return same, same & (s <= t)


def _group_gates(gt, same, causal):
    cm_bf = jnp.where(causal, 1.0, 0.0).astype(BF16)
    lf = _log_sigmoid(gt)
    bt = sum(_dot(p, cm_bf) for p in _split3(lf))
    b4 = bt[M_HEADS:2 * M_HEADS]
    a4 = gt[0:M_HEADS] - b4
    a8 = jnp.concatenate([a4, a4], axis=0)
    a_cols = jnp.concatenate([a8, jnp.zeros((GROUP - SUBLANES, GROUP), F32)], axis=0).T
    at_mats, run_rows, chunk_rows = [], [], []
    for h in range(M_HEADS):
        at = jnp.broadcast_to(a_cols[:, h:h + 1], (GROUP, GROUP))
        at_mats.append(at)
        run_rows.append(jnp.max(jnp.where(causal, at, -jnp.inf), axis=0, keepdims=True))
        chunk_rows.append(jnp.max(jnp.where(same, at, -jnp.inf), axis=0, keepdims=True))
    return b4, a4, at_mats, jnp.concatenate(run_rows, axis=0), jnp.concatenate(chunk_rows, axis=0)


def _group_weights(m_prev, b4, a4, run4, chunk4):
    big_m = jnp.maximum(m_prev, run4)
    m_last = jnp.maximum(m_prev, chunk4)
    w_inter = jnp.exp(m_prev - big_m)
    g_vec = jnp.exp(m_prev - m_last)
    w_last = jnp.exp(a4 - m_last)
    m_t = b4 + big_m
    return big_m, w_inter, g_vec, w_last, m_t, jnp.exp(-m_t)


def _group_intra(qvot, ks, at_mats, big_m, causal):
    lane_head = lax.broadcasted_iota(jnp.int32, (1, M_QK_W), 1) >> DK_SHIFT
    row_head = lax.broadcasted_iota(jnp.int32, (M_QK_W, 1), 0) >> DK_SHIFT
    ones_rows = jnp.where(lax.broadcasted_iota(jnp.int32, (BF16_ROWS, GROUP), 0) == 0, 1.0, 0.0).astype(BF16)
    qt = qvot(0, M_QK_W)
    k_heads = [jnp.where(lane_head == h, ks, jnp.zeros_like(ks)) for h in range(M_HEADS)]
    qw = jnp.concatenate([jnp.where(row_head == h, qt, jnp.zeros_like(qt)) for h in range(M_HEADS)], axis=1)
    sc_t = _dot(jnp.concatenate(k_heads, axis=0), qt)
    vta, intra = [], []
    for h in range(M_HEADS):
        w_t = jnp.where(causal, jnp.exp(at_mats[h] - big_m[h:h + 1]), 0.0)
        s_t = (sc_t[h * GROUP:(h + 1) * GROUP] * w_t).astype(BF16)
        vta_h = jnp.concatenate([qvot(M_QK_W + h * M_DV, M_QK_W + (h + 1) * M_DV), ones_rows], axis=0)
        vta.append(vta_h)
        intra.append(_dot(vta_h, s_t))
    return k_heads, qw, vta, intra


def _group_out(qvot, inter, intra, w_inter, e_neg_m, nw_ref):
    outs = []
    for h in range(M_HEADS):
        out_t = inter[:, h * GROUP:(h + 1) * GROUP] * w_inter[h:h + 1] + intra[h]
        hh = out_t[0:M_DV] / jnp.maximum(jnp.abs(out_t[M_DV:M_DV + 1]), e_neg_m[h:h + 1])
        ms = jnp.mean(hh * hh, axis=0, keepdims=True)
        hn = (hh * lax.rsqrt(ms + EPS)) * nw_ref[h * M_DV:(h + 1) * M_DV]
        o_t = qvot(M_QK_W + M_V_W + h * M_DV, M_QK_W + M_V_W + (h + 1) * M_DV)
        outs.append(hn * jax.nn.sigmoid(o_t.astype(F32)))
    return jnp.concatenate(outs, axis=0).T


def _decay_row(g_vec, lane0):
    lane_head = lax.broadcasted_iota(jnp.int32, (1, M_QK_W), 1) >> DK_SHIFT
    g_row = jnp.zeros((1, M_QK_W), F32)
    for h in range(M_HEADS):
        g_row = jnp.where(lane_head == h, g_vec[h:h + 1, lane0:lane0 + 1], g_row)
    return g_row


def _mlstm_prompt_kernel(qvot_ref, gt_ref, k_ref, bias_ref, nw_ref, s0_ref, m0_ref,
                         h_ref, st_ref, m_ref):
    step = pl.program_id(0)

    @pl.when(step == 0)
    def _():
        st_ref[...] = s0_ref[...]
        m_ref[...] = m0_ref[...]

    nb = k_ref.shape[0]
    n_chunks = GROUP // M_CHUNK
    same, causal = _chunk_masks(M_CHUNK.bit_length() - 1)
    lane = lax.broadcasted_iota(jnp.int32, (1, GROUP), 1)
    lane_x = jnp.concatenate([lane] * M_HEADS, axis=1)
    for b in range(nb):
        qvot = lambda r0, r1: qvot_ref[b, 0, r0:r1, :]
        ks = k_ref[b] * (M_DK ** -0.5)
        b4, a4, at_mats, run4, chunk4 = _group_gates(gt_ref[b, 0] + bias_ref[...], same, causal)
        m_c = m_ref[b][0:M_HEADS, 0:1]
        m_prev = jnp.broadcast_to(m_c, (M_HEADS, GROUP))
        for ci in range(1, n_chunks):
            last = ci * M_CHUNK - 1
            m_c = b4[:, last:last + 1] + jnp.maximum(m_c, chunk4[:, last:last + 1])
            m_prev = jnp.where(lane >= ci * M_CHUNK, m_c, m_prev)
        big_m, w_inter, g_vec, w_last, m_t, e_neg_m = _group_weights(m_prev, b4, a4, run4, chunk4)
        m_ref[b, 0:M_HEADS, :] = jnp.broadcast_to(m_t[:, GROUP - 1:GROUP], (M_HEADS, LANES))
        k_heads, qw, vta, intra = _group_intra(qvot, ks, at_mats, big_m, causal)
        inter = None
        for ci in range(n_chunks):
            st_old = st_ref[b]
            inter_c = _dot(st_old.astype(BF16), qw)
            inter = inter_c if inter is None else jnp.where(lane_x >= ci * M_CHUNK, inter_c, inter)
            in_chunk = (lane >= ci * M_CHUNK) & (lane < (ci + 1) * M_CHUNK)
            d_st = sum(_dot((vta[h].astype(F32) * jnp.where(in_chunk, w_last[h:h + 1], 0.0)).astype(BF16), k_heads[h])
                       for h in range(M_HEADS))
            st_ref[b] = _decay_row(g_vec, ci * M_CHUNK) * st_old + d_st
        h_ref[b] = _group_out(qvot, inter, intra, w_inter, e_neg_m, nw_ref).astype(h_ref.dtype)


def _mlstm_prompt(qvot, gt, k_m, bias_col, nw_col, s0, m0):
    nb, nblk = qvot.shape[0], qvot.shape[1]
    full = lambda a: pl.BlockSpec(a.shape, lambda i: (0,) * a.ndim)
    slab = lambda a: pl.BlockSpec((nb, 1) + a.shape[2:], lambda i: (0, i, 0, 0))
    return pl.pallas_call(
        _mlstm_prompt_kernel,
        grid=(nblk,),
        in_specs=[slab(qvot), slab(gt), pl.BlockSpec((nb, GROUP, M_QK_W), lambda i: (0, i, 0)),
                  full(bias_col), full(nw_col), full(s0), full(m0)],
        out_specs=[pl.BlockSpec((nb, GROUP, M_V_W), lambda i: (0, i, 0)), full(s0), full(m0)],
        out_shape=[jax.ShapeDtypeStruct((nb, nblk * GROUP, M_V_W), BF16),
                   jax.ShapeDtypeStruct(s0.shape, F32),
                   jax.ShapeDtypeStruct(m0.shape, F32)],
        compiler_params=_params(("arbitrary",)),
        name="mlstm_prompt",
    )(qvot, gt, k_m, bias_col, nw_col, s0, m0)


def _mlstm_sample_kernel(seq_len, qvot_ref, gt_ref, k_ref, mrow_ref, bias_ref, nw_ref, c_ref, n_ref,
                         h_ref, c_out_ref, n_out_ref, mt_ref):
    n_seq = GROUP // seq_len
    shift = seq_len.bit_length() - 1
    same, causal = _chunk_masks(shift)
    lane_seq = lax.broadcasted_iota(jnp.int32, (1, GROUP), 1) >> shift
    lane_seq_x = jnp.concatenate([lane_seq] * M_HEADS, axis=1)
    row0 = lax.broadcasted_iota(jnp.int32, (BF16_ROWS, M_QK_W), 0) == 0
    qvot = lambda r0, r1: qvot_ref[0, r0:r1, :]
    ks = k_ref[...] * (M_DK ** -0.5)
    b4, a4, at_mats, run4, chunk4 = _group_gates(gt_ref[0] + bias_ref[...], same, causal)
    big_m, w_inter, g_vec, w_last, m_t, e_neg_m = _group_weights(mrow_ref[0, 0:M_HEADS, :], b4, a4, run4, chunk4)
    mt_ref[0] = jnp.concatenate([m_t, m_t], axis=0)
    k_heads, qw, vta, intra = _group_intra(qvot, ks, at_mats, big_m, causal)

    st_old, inter = [], None
    for s in range(n_seq):
        n_rows = jnp.where(row0, jnp.broadcast_to(n_ref[s:s + 1, :], (BF16_ROWS, M_QK_W)), 0.0)
        st_s = jnp.concatenate([c_ref[s].T, n_rows], axis=0)
        st_old.append(st_s)
        inter_s = _dot(st_s.astype(BF16), qw)
        inter = inter_s if inter is None else jnp.where(lane_seq_x == s, inter_s, inter)
    d_st = None
    for h in range(M_HEADS):
        vf = vta[h].astype(F32)
        tall = jnp.concatenate(
            [(vf * jnp.where(lane_seq == s, w_last[h:h + 1], 0.0)).astype(BF16) for s in range(n_seq)], axis=0)
        part = _dot(tall, k_heads[h])
        d_st = part if d_st is None else d_st + part
    for s in range(n_seq):
        st_new = _decay_row(g_vec, s * seq_len) * st_old[s] + d_st[s * S_ROWS:(s + 1) * S_ROWS]
        c_out_ref[s] = st_new[0:M_DV].T
        n_out_ref[s:s + 1, :] = st_new[M_DV:M_DV + 1]
    h_ref[...] = _group_out(qvot, inter, intra, w_inter, e_neg_m, nw_ref).astype(h_ref.dtype)


def _mlstm_sample(qvot, gt, k_m, mrow, bias_col, nw_col, c, n, seq_len):
    ngrp = qvot.shape[0]
    n_seq = GROUP // seq_len
    full = lambda a: pl.BlockSpec(a.shape, lambda i: (0,) * a.ndim)
    slab = lambda a: pl.BlockSpec((1,) + a.shape[1:], lambda i: (i, 0, 0))
    row = lambda w: pl.BlockSpec((GROUP, w), lambda i: (i, 0))
    c_spec = pl.BlockSpec((n_seq,) + c.shape[1:], lambda i: (i, 0, 0))
    n_spec = pl.BlockSpec((n_seq, n.shape[1]), lambda i: (i, 0))
    return pl.pallas_call(
        functools.partial(_mlstm_sample_kernel, seq_len),
        grid=(ngrp,),
        in_specs=[slab(qvot), slab(gt), row(M_QK_W), slab(mrow), full(bias_col), full(nw_col), c_spec, n_spec],
        out_specs=[row(M_V_W), c_spec, n_spec, slab(mrow)],
        out_shape=[jax.ShapeDtypeStruct((ngrp * GROUP, M_V_W), BF16),
                   jax.ShapeDtypeStruct(c.shape, F32),
                   jax.ShapeDtypeStruct(n.shape, F32),
                   jax.ShapeDtypeStruct(mrow.shape, F32)],
        compiler_params=_params(("arbitrary",)),
        name="mlstm_sample",
    )(qvot, gt, k_m, mrow, bias_col, nw_col, c, n)


def _swa_prompt_kernel(nqb, sink_ref, qt_ref, kv_ref, kvp_ref, h_ref, kwin_ref, vwin_ref):
    j = pl.program_id(1)
    hd = A_HEAD_DIM
    nq = A_GROUPS * WINDOW
    si = lax.broadcasted_iota(jnp.int32, (2 * WINDOW, nq), 0)
    qi = lax.broadcasted_iota(jnp.int32, (2 * WINDOW, nq), 1) & (WINDOW - 1)
    local = ((si < WINDOW) & (si > qi)) | ((si >= WINDOW) & (si - WINDOW <= qi))
    first = local & ((j > 0) | (si >= WINDOW))
    lane_grp = lax.broadcasted_iota(jnp.int32, (1, nq), 1) >> WINDOW_SHIFT
    kv_blocks = [kvp_ref[0]] + [kv_ref[0, i * WINDOW:(i + 1) * WINDOW, :] for i in range(nqb)]
    k_bf = [blk[:, 0:A_KV_W].astype(BF16) for blk in kv_blocks]
    vt_bf = [blk[:, A_KV_W:].T.astype(BF16) for blk in kv_blocks]
    zeros = jnp.zeros((hd, nq), BF16)
    sinks = []
    for kvh in range(A_KV_HEADS):
        sk = jnp.zeros((1, nq), F32)
        for g in range(A_GROUPS):
            sk = jnp.where(lane_grp == g, sink_ref[kvh * A_GROUPS + g], sk)
        sinks.append(sk)
    for qb in range(nqb):
        kk = jnp.concatenate([k_bf[qb], k_bf[qb + 1]], axis=0)
        vt = jnp.concatenate([vt_bf[qb], vt_bf[qb + 1]], axis=1)
        mask = local if qb > 0 else first
        pieces = []
        for kvh in range(A_KV_HEADS):
            q4t = jnp.concatenate(
                [qt_ref[qb, (kvh * A_GROUPS + g) * hd:(kvh * A_GROUPS + g + 1) * hd, :] for g in range(A_GROUPS)],
                axis=1)
            wq = jnp.concatenate([q4t, zeros] if kvh == 0 else [zeros, q4t], axis=0)
            s = jnp.where(mask, _dot(kk, wq), -jnp.inf)
            sk = sinks[kvh]
            mx = jnp.maximum(jnp.max(s, axis=0, keepdims=True), sk)
            p = jnp.exp(s - mx)
            den = jnp.sum(p, axis=0, keepdims=True) + jnp.exp(sk - mx)
            ot = _dot(vt, p.astype(BF16))[kvh * hd:(kvh + 1) * hd] / den
            pieces += [ot[:, g * WINDOW:(g + 1) * WINDOW] for g in range(A_GROUPS)]
        h_t = jnp.concatenate(pieces, axis=0)
        h_ref[0, qb * WINDOW:(qb + 1) * WINDOW, :] = h_t.T.astype(h_ref.dtype)
    kwin_ref[0] = kv_blocks[nqb][:, 0:A_KV_W]
    vwin_ref[0] = kv_blocks[nqb][:, A_KV_W:]


def _swa_prompt(qat, kv_a, sinks, nqb):
    nb, t = kv_a.shape[0], kv_a.shape[1]
    tb = nqb * WINDOW
    nj = t // tb
    return pl.pallas_call(
        functools.partial(_swa_prompt_kernel, nqb),
        grid=(nb, nj),
        in_specs=[pl.BlockSpec(memory_space=pltpu.SMEM),
                  pl.BlockSpec((nqb, A_Q_W, WINDOW), lambda b, j: (b * nj + j, 0, 0)),
                  pl.BlockSpec((1, tb, 2 * A_KV_W), lambda b, j: (b, j, 0)),
                  pl.BlockSpec((1, WINDOW, 2 * A_KV_W), lambda b, j: (b, jnp.maximum(j * nqb - 1, 0), 0))],
        out_specs=[pl.BlockSpec((1, tb, A_Q_W), lambda b, j: (b, j, 0)),
                   pl.BlockSpec((1, WINDOW, A_KV_W), lambda b, j: (b, 0, 0)),
                   pl.BlockSpec((1, WINDOW, A_KV_W), lambda b, j: (b, 0, 0))],
        out_shape=[jax.ShapeDtypeStruct((nb, t, A_Q_W), BF16),
                   jax.ShapeDtypeStruct((nb, WINDOW, A_KV_W), F32),
                   jax.ShapeDtypeStruct((nb, WINDOW, A_KV_W), F32)],
        compiler_params=_params(("arbitrary", "arbitrary")),
        name="swa_prompt",
    )(sinks, qat, kv_a, kv_a)


def _swa_sample_kernel(seq_len, sink_ref, qt_ref, kv_ref, ck_ref, cv_ref, h_ref, kwin_ref, vwin_ref):
    n_seq = GROUP // seq_len
    wb = ck_ref.shape[1]
    n_keys = wb + BF16_ROWS
    lane = lax.broadcasted_iota(jnp.int32, (1, LANES), 1)
    lo_half = lane < HALF
    q_rows = qt_ref[0].astype(F32).T
    k_new = kv_ref[:, 0:A_KV_W].reshape(n_seq, seq_len, A_KV_W)
    v_new = kv_ref[:, A_KV_W:].reshape(n_seq, seq_len, A_KV_W)
    kwin_ref[:, 0:wb - seq_len, :] = ck_ref[:, seq_len:wb, :]
    vwin_ref[:, 0:wb - seq_len, :] = cv_ref[:, seq_len:wb, :]
    kwin_ref[:, wb - seq_len:wb, :] = k_new
    vwin_ref[:, wb - seq_len:wb, :] = v_new

    def to_kv_half(x, head):
        kvh = head // A_GROUPS
        if head % 2 != kvh:
            x = pltpu.roll(x, HALF, axis=1)
        return jnp.where(lo_half if kvh == 0 else ~lo_half, x, 0.0)

    lhs = jnp.concatenate(
        [to_kv_half(q_rows[:, (h // 2) * LANES:(h // 2 + 1) * LANES], h).reshape(n_seq, seq_len, LANES)
         for h in range(A_HEADS)], axis=1).astype(BF16)
    zpad = jnp.zeros((n_seq, BF16_ROWS - seq_len, A_KV_W), F32)
    k_all = jnp.concatenate([ck_ref[...], k_new, zpad], axis=1).astype(BF16)
    v_all = jnp.concatenate([cv_ref[...], v_new, zpad], axis=1).astype(BF16)
    s = jnp.einsum('sqf,skf->sqk', lhs, k_all, preferred_element_type=F32)
    nrow = A_HEADS * seq_len
    ti = lax.broadcasted_iota(jnp.int32, (nrow, n_keys), 0) & (seq_len - 1)
    ki = lax.broadcasted_iota(jnp.int32, (nrow, n_keys), 1)
    mask = ((ki < wb) & (ti + wb - ki < WINDOW)) | ((ki >= wb) & (ki - wb <= ti))
    row_head = lax.broadcasted_iota(jnp.int32, (nrow, 1), 0) >> (seq_len.bit_length() - 1)
    sk = jnp.zeros((nrow, 1), F32)
    for h in range(A_HEADS):
        sk = jnp.where(row_head == h, sink_ref[h], sk)
    s = jnp.where(mask, s, -jnp.inf)
    mx = jnp.maximum(jnp.max(s, axis=-1, keepdims=True), sk)
    p = jnp.exp(s - mx)
    den = jnp.sum(p, axis=-1, keepdims=True) + jnp.exp(sk - mx)
    o = jnp.einsum('sqk,skf->sqf', p.astype(BF16), v_all, preferred_element_type=F32) / den

    def from_kv_half(head):
        x = o[:, head * seq_len:(head + 1) * seq_len, :].reshape(GROUP, LANES)
        return pltpu.roll(x, HALF, axis=1) if head % 2 != head // A_GROUPS else x

    for c in range(A_HEADS // 2):
        h_ref[:, c * LANES:(c + 1) * LANES] = jnp.where(lo_half, from_kv_half(2 * c),
                                                        from_kv_half(2 * c + 1)).astype(h_ref.dtype)


def _swa_sample(qat, kv_a, cache_k, cache_v, sinks, seq_len):
    ngrp = qat.shape[0]
    n_seq = GROUP // seq_len
    wb = cache_k.shape[1]
    row = lambda w: pl.BlockSpec((GROUP, w), lambda i: (i, 0))
    cache = pl.BlockSpec((n_seq, wb, A_KV_W), lambda i: (i, 0, 0))
    return pl.pallas_call(
        functools.partial(_swa_sample_kernel, seq_len),
        grid=(ngrp,),
        in_specs=[pl.BlockSpec(memory_space=pltpu.SMEM), pl.BlockSpec((1, A_Q_W, LANES), lambda i: (i, 0, 0)),
                  row(2 * A_KV_W), cache, cache],
        out_specs=[row(A_Q_W), cache, cache],
        out_shape=[jax.ShapeDtypeStruct((ngrp * GROUP, A_Q_W), BF16),
                   jax.ShapeDtypeStruct(cache_k.shape, F32),
                   jax.ShapeDtypeStruct(cache_v.shape, F32)],
        compiler_params=_params(("arbitrary",)),
        name="swa_sample",
    )(sinks, qat, kv_a, cache_k, cache_v)


def _merge_ffn_kernel(x_ref, hm_ref, ha_ref, gab_ref, wa_ref, wb_ref, wo_ref, nw_ref, wg_ref, wu_ref, wd_ref,
                      y_ref):
    ga = jax.nn.sigmoid(gab_ref[:, 0:D_MODEL].astype(F32))
    gb = jax.nn.sigmoid(gab_ref[:, D_MODEL:].astype(F32))
    mix = ga * _dot(hm_ref[...], wa_ref[...]) + gb * _dot(ha_ref[...], wb_ref[...])
    x1 = x_ref[...] + _dot(mix.astype(BF16), wo_ref[...])
    hf = _rms_rows(x1, nw_ref[...]).astype(BF16)
    gate = _dot(hf, wg_ref[...])
    up = _dot(hf, wu_ref[...])
    act = (jax.nn.silu(gate) * up).astype(BF16)
    y_ref[...] = x1 + _dot(act, wd_ref[...])


def _merge_ffn(x2d, h_m, h_a, g_ab, wa, wb, wo, nw, wg, wu, wd, tm):
    n = x2d.shape[0]
    row = lambda w: pl.BlockSpec((tm, w), lambda i: (i, 0))
    return pl.pallas_call(
        _merge_ffn_kernel,
        grid=(n // tm,),
        in_specs=[row(D_MODEL), row(M_V_W), row(A_Q_W), row(2 * D_MODEL)]
                 + [_const_spec(w.shape) for w in (wa, wb, wo, nw, wg, wu, wd)],
        out_specs=row(D_MODEL),
        out_shape=jax.ShapeDtypeStruct((n, D_MODEL), F32),
        compiler_params=_params(("arbitrary",)),
        name="merge_ffn",
    )(x2d, h_m, h_a, g_ab, wa, wb, wo, nw, wg, wu, wd)


def kernel(x_prompt, x_sample, state_mlstm_C, state_mlstm_n, state_mlstm_m, cache_swa_k, cache_swa_v,
           norm_mix_w, w_in, mlstm_i_bias, mlstm_f_bias, mlstm_norm_w, q_norm_w, k_norm_w, attn_sinks,
           w_branch_a, w_branch_b, w_out, norm_ffn_w, w_gate, w_up, w_down):
    depth = w_in.shape[0]
    assert depth == 1, "single trunk layer"
    l = 0
    bp, tp = x_prompt.shape[0], x_prompt.shape[1]
    bs, ts = x_sample.shape[0], x_sample.shape[1]
    assert tp % 512 == 0 and (bs * ts) % GROUP == 0 and GROUP % ts == 0 and ts & (ts - 1) == 0
    assert ts <= SUBLANES, "sample chunk must fit one sublane tile"

    w = w_in[l]
    c_km, c_vm = M_QK_W, 2 * M_QK_W
    c_g = 2 * M_QK_W + 2 * M_V_W
    c_qa = c_g + 2 * M_HEADS
    c_ka = c_qa + A_Q_W
    wmt = jnp.concatenate([w[:, 0:c_km], w[:, c_vm:c_g]], axis=1).T.astype(BF16)
    wgt = jnp.pad(w[:, c_g:c_qa], ((0, 0), (0, BF16_ROWS - 2 * M_HEADS))).T.astype(BF16)
    wqt = w[:, c_qa:c_ka].T.astype(BF16)
    wn = jnp.concatenate([w[:, c_km:c_vm], w[:, c_ka:]], axis=1).astype(BF16)
    head_of = jnp.arange(A_KV_W) // A_HEAD_DIM
    bd = (head_of[:, None] == head_of[None, :]).astype(BF16)
    qcol = (jnp.tile(q_norm_w[l], A_HEADS) * (A_HEAD_DIM ** -0.5)).reshape(A_Q_W, 1)
    krow = jnp.tile(k_norm_w[l], A_KV_HEADS).reshape(1, A_KV_W)
    nw_mix = norm_mix_w[l].reshape(1, D_MODEL)
    nw_ffn = norm_ffn_w[l].reshape(1, D_MODEL)
    bias_col = jnp.concatenate([mlstm_i_bias[l], mlstm_f_bias[l]]).reshape(2 * M_HEADS, 1)
    nw_col = mlstm_norm_w[l].reshape(M_V_W, 1)
    sinks = attn_sinks[l]
    wa, wb, wo = w_branch_a[l].astype(BF16), w_branch_b[l].astype(BF16), w_out[l].astype(BF16)
    wg, wu, wd = w_gate[l].astype(BF16), w_up[l].astype(BF16), w_down[l].astype(BF16)
    proj = lambda x2d, tm: _proj(x2d, nw_mix, wmt, wgt, wqt, wn, bd, qcol, krow, tm)
    merge = lambda x2d, h_m, h_a, g_ab, tm: _merge_ffn(x2d, h_m, h_a, g_ab, wa, wb, wo, nw_ffn, wg, wu, wd, tm)

    xp = x_prompt.reshape(bp * tp, D_MODEL)
    qvot, gt, qat, k_m, kv_a, g_ab = proj(xp, 512)
    nblk = tp // GROUP
    s0 = jnp.zeros((bp, S_ROWS, M_QK_W), F32)
    m0 = jnp.zeros((bp, SUBLANES, LANES), F32)
    h_m, st_p, m_p = _mlstm_prompt(qvot.reshape(bp, nblk, -1, LANES), gt.reshape(bp, nblk, -1, LANES),
                                   k_m.reshape(bp, tp, M_QK_W), bias_col, nw_col, s0, m0)
    h_a, kwin_p, vwin_p = _swa_prompt(qat, kv_a.reshape(bp, tp, 2 * A_KV_W), sinks, 4)
    yp = merge(xp, h_m.reshape(bp * tp, M_V_W), h_a.reshape(bp * tp, A_Q_W), g_ab, 512).reshape(bp, tp, D_MODEL)
    c_p = jnp.swapaxes(st_p[:, :M_DV, :], 1, 2).reshape(bp, M_HEADS, M_DK, M_DV)
    n_p = st_p[:, M_DV, :].reshape(bp, M_HEADS, M_DK)
    m_pr = m_p[:, :M_HEADS, 0]

    ns = bs * ts
    xs = x_sample.reshape(ns, D_MODEL)
    tms = 512 if ns % 512 == 0 else GROUP
    qvot, gt, qat, k_m, kv_a, g_ab = proj(xs, tms)
    ngrp = ns // GROUP
    m_lanes = jnp.repeat(state_mlstm_m[l], ts, axis=0).reshape(ngrp, GROUP, M_HEADS)
    mrow = jnp.pad(jnp.swapaxes(m_lanes, 1, 2), ((0, 0), (0, SUBLANES - M_HEADS), (0, 0)))
    h_m, c_s, n_s, mt_s = _mlstm_sample(qvot, gt, k_m, mrow, bias_col, nw_col,
                                        state_mlstm_C[l].reshape(bs, M_QK_W, M_DV),
                                        state_mlstm_n[l].reshape(bs, M_QK_W), ts)
    wbuf = cache_swa_k.shape[2]
    h_a, kwin_s, vwin_s = _swa_sample(qat, kv_a, cache_swa_k[l].reshape(bs, wbuf, A_KV_W),
                                      cache_swa_v[l].reshape(bs, wbuf, A_KV_W), sinks, ts)
    ys = merge(xs, h_m, h_a, g_ab, tms).reshape(bs, ts, D_MODEL)
    m_s = jnp.swapaxes(mt_s[:, :M_HEADS, :], 1, 2).reshape(bs, ts, M_HEADS)[:, ts - 1, :]

    kv5 = lambda a: a.reshape(a.shape[0], a.shape[1], A_KV_HEADS, A_HEAD_DIM)[None]
    return (yp, ys,
            c_p[None], n_p[None], m_pr[None], kv5(kwin_p), kv5(vwin_p),
            c_s.reshape(bs, M_HEADS, M_DK, M_DV)[None], n_s.reshape(bs, M_HEADS, M_DK)[None], m_s[None],
            kv5(kwin_s), kv5(vwin_s))
```

```python
import functools

import jax
import jax.numpy as jnp
from jax import lax
from jax.experimental import pallas as pl
from jax.experimental.pallas import tpu as pltpu

F32 = jnp.float32
BF16 = jnp.bfloat16

D_MODEL = 1024
M_HEADS = 4
M_DK = 64
M_DV = 128
M_CHUNK = 64
M_QK_W = M_HEADS * M_DK
M_V_W = M_HEADS * M_DV
A_HEADS = 8
A_KV_HEADS = 2
A_HEAD_DIM = 64
A_GROUPS = A_HEADS // A_KV_HEADS
A_Q_W = A_HEADS * A_HEAD_DIM
A_KV_W = A_KV_HEADS * A_HEAD_DIM
WINDOW = 128
D_FF = 2816
EPS = 1e-6

LANES = 128
SUBLANES = 8
BF16_ROWS = 16
GROUP = 128
S_ROWS = M_DV + BF16_ROWS
VMEM_LIMIT = 56 * 1024 * 1024
DK_SHIFT = M_DK.bit_length() - 1
WINDOW_SHIFT = WINDOW.bit_length() - 1
HALF = LANES // 2
assert A_HEAD_DIM == HALF and A_KV_W == LANES, "attention head pairs share one lane-width"

NT_DIMS = (((1,), (1,)), ((), ()))


def _dot(a, b):
    return jnp.dot(a, b, preferred_element_type=F32)


def _dot_nt(a, b):
    return lax.dot_general(a, b, NT_DIMS, preferred_element_type=F32)


def _const_spec(shape):
    nd = len(shape)
    return pl.BlockSpec(shape, lambda *_: (0,) * nd, pipeline_mode=pl.Buffered(1))


def _params(sem):
    return pltpu.CompilerParams(dimension_semantics=sem, vmem_limit_bytes=VMEM_LIMIT)


def _rms_rows(x, nw):
    ms = jnp.mean(x * x, axis=-1, keepdims=True)
    return (x * lax.rsqrt(ms + EPS)) * nw


def _proj_kernel(x_ref, nw_ref, wmt_ref, wgt_ref, wqt_ref, wn_ref, bd_ref, qcol_ref, krow_ref,
                 qvot_ref, gt_ref, qat_ref, km_ref, kva_ref, gab_ref):
    hn = _rms_rows(x_ref[...], nw_ref[...]).astype(BF16)
    n_blk = qat_ref.shape[0]

    def put(ref, rows, val):
        for c in range(n_blk):
            ref[c, rows, :] = val[:, c * LANES:(c + 1) * LANES].astype(ref.dtype)

    put(qvot_ref, slice(None), _dot_nt(wmt_ref[...], hn))
    put(gt_ref, slice(None), _dot_nt(wgt_ref[...], hn)[0:2 * M_HEADS])
    qt = _dot_nt(wqt_ref[...], hn)
    for h in range(A_HEADS):
        hs = slice(h * A_HEAD_DIM, (h + 1) * A_HEAD_DIM)
        blk = qt[hs]
        ssq_q = jnp.sum(blk * blk, axis=0, keepdims=True)
        put(qat_ref, hs, (blk * lax.rsqrt(ssq_q * (1.0 / A_HEAD_DIM) + EPS)) * qcol_ref[hs])
    km_ref[...] = _dot(hn, wn_ref[:, 0:M_QK_W]).astype(km_ref.dtype)
    gab_ref[...] = _dot(hn, wn_ref[:, M_QK_W + 2 * A_KV_W:]).astype(gab_ref.dtype)
    kv = _dot(hn, wn_ref[:, M_QK_W:M_QK_W + 2 * A_KV_W])
    k = kv[:, 0:A_KV_W]
    ksq = k * k
    hi = ksq.astype(BF16)
    lo = (ksq - hi.astype(F32)).astype(BF16)
    ssq = _dot(hi, bd_ref[...]) + _dot(lo, bd_ref[...])
    kva_ref[:, 0:A_KV_W] = (k * lax.rsqrt(ssq * (1.0 / A_HEAD_DIM) + EPS)) * krow_ref[...]
    kva_ref[:, A_KV_W:] = kv[:, A_KV_W:]


def _proj(x2d, nw, wmt, wgt, wqt, wn, bd, qcol, krow, tm):
    n = x2d.shape[0]
    row = lambda w: pl.BlockSpec((tm, w), lambda i: (i, 0))
    n_blk = tm // LANES
    slab = lambda r: pl.BlockSpec((n_blk, r, LANES), lambda i: (i, 0, 0))
    w_qvo = M_QK_W + 2 * M_V_W
    return pl.pallas_call(
        _proj_kernel,
        grid=(n // tm,),
        in_specs=[row(D_MODEL)] + [_const_spec(a.shape) for a in (nw, wmt, wgt, wqt, wn, bd, qcol, krow)],
        out_specs=[slab(w_qvo), slab(2 * M_HEADS), slab(A_Q_W), row(M_QK_W), row(2 * A_KV_W), row(2 * D_MODEL)],
        out_shape=[jax.ShapeDtypeStruct((n // LANES, w_qvo, LANES), BF16),
                   jax.ShapeDtypeStruct((n // LANES, 2 * M_HEADS, LANES), F32),
                   jax.ShapeDtypeStruct((n // LANES, A_Q_W, LANES), BF16),
                   jax.ShapeDtypeStruct((n, M_QK_W), BF16),
                   jax.ShapeDtypeStruct((n, 2 * A_KV_W), F32),
                   jax.ShapeDtypeStruct((n, 2 * D_MODEL), BF16)],
        compiler_params=_params(("arbitrary",)),
        name="proj",
    )(x2d, nw, wmt, wgt, wqt, wn, bd, qcol, krow)


def _split3_rows(x):
    hi = x.astype(BF16).astype(F32)
    r1 = x - hi
    mid = r1.astype(BF16).astype(F32)
    lo = r1 - mid
    return jnp.concatenate([hi, mid, lo], axis=0).astype(BF16)


def _log_sigmoid(x):
    return jnp.minimum(x, 0.0) - jnp.log1p(jnp.exp(-jnp.abs(x)))


def _chunk_masks(chunk_shift):
    s = lax.broadcasted_iota(jnp.int32, (GROUP, GROUP), 0)
    t = lax.broadcasted_iota(jnp.int32, (GROUP, GROUP), 1)
    same = (s >> chunk_shift) == (t >> chunk_shift)
    return same, same & (s <= t)


def _group_gates(gt, same, causal):
    cm_bf = jnp.where(causal, 1.0, 0.0).astype(BF16)
    lf = _log_sigmoid(gt)
    nr = gt.shape[0]
    bt3 = _dot(_split3_rows(lf), cm_bf)
    bt = (bt3[0:nr] + bt3[nr:2 * nr]) + bt3[2 * nr:3 * nr]
    b4 = bt[M_HEADS:2 * M_HEADS]
    a4 = gt[0:M_HEADS] - b4
    a8 = jnp.concatenate([a4, a4], axis=0)
    a_cols = jnp.concatenate([a8, jnp.zeros((GROUP - SUBLANES, GROUP), F32)], axis=0).T
    at_mats, run_rows, chunk_rows = [], [], []
    for h in range(M_HEADS):
        at = jnp.broadcast_to(a_cols[:, h:h + 1], (GROUP, GROUP))
        at_mats.append(at)
        run_rows.append(jnp.max(jnp.where(causal, at, -jnp.inf), axis=0, keepdims=True))
        chunk_rows.append(jnp.max(jnp.where(same, at, -jnp.inf), axis=0, keepdims=True))
    return b4, a4, at_mats, jnp.concatenate(run_rows, axis=0), jnp.concatenate(chunk_rows, axis=0)


def _group_weights(m_prev, b4, a4, run4, chunk4):
    big_m = jnp.maximum(m_prev, run4)
    m_last = jnp.maximum(m_prev, chunk4)
    w_inter = jnp.exp(m_prev - big_m)
    g_vec = jnp.exp(m_prev - m_last)
    w_last = jnp.exp(a4 - m_last)
    m_t = b4 + big_m
    return big_m, w_inter, g_vec, w_last, m_t, jnp.exp(-m_t)


def _group_intra(qvot, ks, at_mats, big_m, causal):
    lane_head = lax.broadcasted_iota(jnp.int32, (1, M_QK_W), 1) >> DK_SHIFT
    row_head = lax.broadcasted_iota(jnp.int32, (M_QK_W, 1), 0) >> DK_SHIFT
    ones_rows = jnp.where(lax.broadcasted_iota(jnp.int32, (BF16_ROWS, GROUP), 0) == 0, 1.0, 0.0).astype(BF16)
    qt = qvot(0, M_QK_W)
    k_stack = jnp.concatenate([jnp.where(lane_head == h, ks, jnp.zeros_like(ks)) for h in range(M_HEADS)], axis=0)
    qw = jnp.concatenate([jnp.where(row_head == h, qt, jnp.zeros_like(qt)) for h in range(M_HEADS)], axis=1)
    sc_t = _dot(k_stack, qt)
    zero_blk = jnp.zeros((GROUP, GROUP), BF16)
    vta, s_rows = [], []
    for h in range(M_HEADS):
        w_t = jnp.where(causal, jnp.exp(at_mats[h] - big_m[h:h + 1]), 0.0)
        s_t = (sc_t[h * GROUP:(h + 1) * GROUP] * w_t).astype(BF16)
        s_rows.append(jnp.concatenate([s_t if j == h else zero_blk for j in range(M_HEADS)], axis=1))
        vta.append(jnp.concatenate([qvot(M_QK_W + h * M_DV, M_QK_W + (h + 1) * M_DV), ones_rows], axis=0))
    intra = _dot(jnp.concatenate(vta, axis=1), jnp.concatenate(s_rows, axis=0))
    return k_stack, qw, vta, intra


def _weighted_values(vta, w_rows):
    return jnp.concatenate([(vta[h].astype(F32) * w_rows[h:h + 1]).astype(BF16) for h in range(M_HEADS)], axis=1)


def _lanes_x(rows):
    return jnp.concatenate([rows[h:h + 1] for h in range(M_HEADS)], axis=1)


def _group_out(qvot, inter, intra, w_inter, e_neg_m, nw_ref):
    outs = []
    out_all = inter * _lanes_x(w_inter) + intra
    for h in range(M_HEADS):
        out_t = out_all[:, h * GROUP:(h + 1) * GROUP]
        hh = out_t[0:M_DV] / jnp.maximum(jnp.abs(out_t[M_DV:M_DV + 1]), e_neg_m[h:h + 1])
        ms = jnp.mean(hh * hh, axis=0, keepdims=True)
        hn = (hh * lax.rsqrt(ms + EPS)) * nw_ref[h * M_DV:(h + 1) * M_DV]
        o_t = qvot(M_QK_W + M_V_W + h * M_DV, M_QK_W + M_V_W + (h + 1) * M_DV)
        outs.append(hn * jax.nn.sigmoid(o_t.astype(F32)))
    return jnp.concatenate(outs, axis=0).T


def _decay_row(g_vec, lane0):
    lane_head = lax.broadcasted_iota(jnp.int32, (1, M_QK_W), 1) >> DK_SHIFT
    g_row = jnp.zeros((1, M_QK_W), F32)
    for h in range(M_HEADS):
        g_row = jnp.where(lane_head == h, g_vec[h:h + 1, lane0:lane0 + 1], g_row)
    return g_row


def _mlstm_prompt_kernel(qvot_ref, gt_ref, k_ref, bias_ref, nw_ref, s0_ref, m0_ref,
                         h_ref, st_ref, m_ref):
    step = pl.program_id(0)

    @pl.when(step == 0)
    def _():
        st_ref[...] = s0_ref[...]
        m_ref[...] = m0_ref[...]

    nb = k_ref.shape[0]
    same, causal = _chunk_masks(GROUP.bit_length() - 1)
    for b in range(nb):
        qvot = lambda r0, r1: qvot_ref[b, 0, r0:r1, :]
        ks = k_ref[b] * (M_DK ** -0.5)
        b4, a4, at_mats, run4, chunk4 = _group_gates(gt_ref[b, 0] + bias_ref[...], same, causal)
        m_prev = jnp.broadcast_to(m_ref[b][0:M_HEADS, 0:1], (M_HEADS, GROUP))
        big_m, w_inter, g_vec, w_last, m_t, e_neg_m = _group_weights(m_prev, b4, a4, run4, chunk4)
        m_ref[b, 0:M_HEADS, :] = jnp.broadcast_to(m_t[:, GROUP - 1:GROUP], (M_HEADS, LANES))
        k_stack, qw, vta, intra = _group_intra(qvot, ks, at_mats, big_m, causal)
        st_old = st_ref[b]
        inter = _dot(st_old.astype(BF16), qw)
        st_ref[b] = _decay_row(g_vec, 0) * st_old + _dot(_weighted_values(vta, w_last), k_stack)
        h_ref[b] = _group_out(qvot, inter, intra, w_inter, e_neg_m, nw_ref).astype(h_ref.dtype)


def _mlstm_prompt(qvot, gt, k_m, bias_col, nw_col, s0, m0):
    nb, nblk = qvot.shape[0], qvot.shape[1]
    full = lambda a: pl.BlockSpec(a.shape, lambda i: (0,) * a.ndim)
    slab = lambda a: pl.BlockSpec((nb, 1) + a.shape[2:], lambda i: (0, i, 0, 0))
    return pl.pallas_call(
        _mlstm_prompt_kernel,
        grid=(nblk,),
        in_specs=[slab(qvot), slab(gt), pl.BlockSpec((nb, GROUP, M_QK_W), lambda i: (0, i, 0)),
                  full(bias_col), full(nw_col), full(s0), full(m0)],
        out_specs=[pl.BlockSpec((nb, GROUP, M_V_W), lambda i: (0, i, 0)), full(s0), full(m0)],
        out_shape=[jax.ShapeDtypeStruct((nb, nblk * GROUP, M_V_W), BF16),
                   jax.ShapeDtypeStruct(s0.shape, F32),
                   jax.ShapeDtypeStruct(m0.shape, F32)],
        compiler_params=_params(("arbitrary",)),
        name="mlstm_prompt",
    )(qvot, gt, k_m, bias_col, nw_col, s0, m0)


def _mlstm_sample_kernel(seq_len, qvot_ref, gt_ref, k_ref, mrow_ref, bias_ref, nw_ref, c_ref, n_ref,
                         h_ref, c_out_ref, n_out_ref, mt_ref):
    n_seq = GROUP // seq_len
    shift = seq_len.bit_length() - 1
    same, causal = _chunk_masks(shift)
    lane_seq = lax.broadcasted_iota(jnp.int32, (1, GROUP), 1) >> shift
    lane_seq_x = jnp.concatenate([lane_seq] * M_HEADS, axis=1)
    row0 = lax.broadcasted_iota(jnp.int32, (BF16_ROWS, M_QK_W), 0) == 0
    qvot = lambda r0, r1: qvot_ref[0, r0:r1, :]
    ks = k_ref[...] * (M_DK ** -0.5)
    b4, a4, at_mats, run4, chunk4 = _group_gates(gt_ref[0] + bias_ref[...], same, causal)
    big_m, w_inter, g_vec, w_last, m_t, e_neg_m = _group_weights(mrow_ref[0, 0:M_HEADS, :], b4, a4, run4, chunk4)
    mt_ref[0] = jnp.concatenate([m_t, m_t], axis=0)
    k_stack, qw, vta, intra = _group_intra(qvot, ks, at_mats, big_m, causal)

    st_old = []
    for s in range(n_seq):
        n_rows = jnp.where(row0, jnp.broadcast_to(n_ref[s:s + 1, :], (BF16_ROWS, M_QK_W)), 0.0)
        st_old.append(jnp.concatenate([c_ref[s].T, n_rows], axis=0))
    inter_all = _dot(jnp.concatenate(st_old, axis=0).astype(BF16), qw)
    inter = inter_all[0:S_ROWS]
    for s in range(1, n_seq):
        inter = jnp.where(lane_seq_x == s, inter_all[s * S_ROWS:(s + 1) * S_ROWS], inter)
    tall = jnp.concatenate([_weighted_values(vta, jnp.where(lane_seq == s, w_last, 0.0)) for s in range(n_seq)],
                           axis=0)
    d_st = _dot(tall, k_stack)
    for s in range(n_seq):
        st_new = _decay_row(g_vec, s * seq_len) * st_old[s] + d_st[s * S_ROWS:(s + 1) * S_ROWS]
        c_out_ref[s] = st_new[0:M_DV].T
        n_out_ref[s:s + 1, :] = st_new[M_DV:M_DV + 1]
    h_ref[...] = _group_out(qvot, inter, intra, w_inter, e_neg_m, nw_ref).astype(h_ref.dtype)


def _mlstm_sample(qvot, gt, k_m, mrow, bias_col, nw_col, c, n, seq_len):
    ngrp = qvot.shape[0]
    n_seq = GROUP // seq_len
    full = lambda a: pl.BlockSpec(a.shape, lambda i: (0,) * a.ndim)
    slab = lambda a: pl.BlockSpec((1,) + a.shape[1:], lambda i: (i, 0, 0))
    row = lambda w: pl.BlockSpec((GROUP, w), lambda i: (i, 0))
    c_spec = pl.BlockSpec((n_seq,) + c.shape[1:], lambda i: (i, 0, 0))
    n_spec = pl.BlockSpec((n_seq, n.shape[1]), lambda i: (i, 0))
    return pl.pallas_call(
        functools.partial(_mlstm_sample_kernel, seq_len),
        grid=(ngrp,),
        in_specs=[slab(qvot), slab(gt), row(M_QK_W), slab(mrow), full(bias_col), full(nw_col), c_spec, n_spec],
        out_specs=[row(M_V_W), c_spec, n_spec, slab(mrow)],
        out_shape=[jax.ShapeDtypeStruct((ngrp * GROUP, M_V_W), BF16),
                   jax.ShapeDtypeStruct(c.shape, F32),
                   jax.ShapeDtypeStruct(n.shape, F32),
                   jax.ShapeDtypeStruct(mrow.shape, F32)],
        compiler_params=_params(("arbitrary",)),
        name="mlstm_sample",
    )(qvot, gt, k_m, mrow, bias_col, nw_col, c, n)


def _swa_prompt_kernel(nqb, sink_ref, qt_ref, kv_ref, kvp_ref, h_ref, kwin_ref, vwin_ref):
    j = pl.program_id(1)
    hd = A_HEAD_DIM
    nq = A_GROUPS * WINDOW
    si = lax.broadcasted_iota(jnp.int32, (2 * WINDOW, nq), 0)
    qi = lax.broadcasted_iota(jnp.int32, (2 * WINDOW, nq), 1) & (WINDOW - 1)
    local = ((si < WINDOW) & (si > qi)) | ((si >= WINDOW) & (si - WINDOW <= qi))
    first = local & ((j > 0) | (si >= WINDOW))
    lane_grp = lax.broadcasted_iota(jnp.int32, (1, nq), 1) >> WINDOW_SHIFT
    kv_blocks = [kvp_ref[0]] + [kv_ref[0, i * WINDOW:(i + 1) * WINDOW, :] for i in range(nqb)]
    k_bf = [blk[:, 0:A_KV_W].astype(BF16) for blk in kv_blocks]
    vt_bf = [blk[:, A_KV_W:].T.astype(BF16) for blk in kv_blocks]
    zeros = jnp.zeros((hd, nq), BF16)
    sinks = []
    for kvh in range(A_KV_HEADS):
        sk = jnp.zeros((1, nq), F32)
        for g in range(A_GROUPS):
            sk = jnp.where(lane_grp == g, sink_ref[kvh * A_GROUPS + g], sk)
        sinks.append(sk)
    for qb in range(nqb):
        kk = jnp.concatenate([k_bf[qb], k_bf[qb + 1]], axis=0)
        vt = jnp.concatenate([vt_bf[qb], vt_bf[qb + 1]], axis=1)
        mask = local if qb > 0 else first
        pieces = []
        for kvh in range(A_KV_HEADS):
            q4t = jnp.concatenate(
                [qt_ref[qb, (kvh * A_GROUPS + g) * hd:(kvh * A_GROUPS + g + 1) * hd, :] for g in range(A_GROUPS)],
                axis=1)
            wq = jnp.concatenate([q4t, zeros] if kvh == 0 else [zeros, q4t], axis=0)
            s = jnp.where(mask, _dot(kk, wq), -jnp.inf)
            sk = sinks[kvh]
            mx = jnp.maximum(jnp.max(s, axis=0, keepdims=True), sk)
            p = jnp.exp(s - mx)
            den = jnp.sum(p, axis=0, keepdims=True) + jnp.exp(sk - mx)
            ot = _dot(vt, p.astype(BF16))[kvh * hd:(kvh + 1) * hd] / den
            pieces += [ot[:, g * WINDOW:(g + 1) * WINDOW] for g in range(A_GROUPS)]
        h_t = jnp.concatenate(pieces, axis=0)
        h_ref[0, qb * WINDOW:(qb + 1) * WINDOW, :] = h_t.T.astype(h_ref.dtype)
    kwin_ref[0] = kv_blocks[nqb][:, 0:A_KV_W]
    vwin_ref[0] = kv_blocks[nqb][:, A_KV_W:]


def _swa_prompt(qat, kv_a, sinks, nqb):
    nb, t = kv_a.shape[0], kv_a.shape[1]
    tb = nqb * WINDOW
    nj = t // tb
    return pl.pallas_call(
        functools.partial(_swa_prompt_kernel, nqb),
        grid=(nb, nj),
        in_specs=[pl.BlockSpec(memory_space=pltpu.SMEM),
                  pl.BlockSpec((nqb, A_Q_W, WINDOW), lambda b, j: (b * nj + j, 0, 0)),
                  pl.BlockSpec((1, tb, 2 * A_KV_W), lambda b, j: (b, j, 0)),
                  pl.BlockSpec((1, WINDOW, 2 * A_KV_W), lambda b, j: (b, jnp.maximum(j * nqb - 1, 0), 0))],
        out_specs=[pl.BlockSpec((1, tb, A_Q_W), lambda b, j: (b, j, 0)),
                   pl.BlockSpec((1, WINDOW, A_KV_W), lambda b, j: (b, 0, 0)),
                   pl.BlockSpec((1, WINDOW, A_KV_W), lambda b, j: (b, 0, 0))],
        out_shape=[jax.ShapeDtypeStruct((nb, t, A_Q_W), BF16),
                   jax.ShapeDtypeStruct((nb, WINDOW, A_KV_W), F32),
                   jax.ShapeDtypeStruct((nb, WINDOW, A_KV_W), F32)],
        compiler_params=_params(("arbitrary", "arbitrary")),
        name="swa_prompt",
    )(sinks, qat, kv_a, kv_a)


def _swa_sample_kernel(seq_len, sink_ref, qt_ref, kv_ref, ck_ref, cv_ref, h_ref, kwin_ref, vwin_ref):
    n_seq = GROUP // seq_len
    wb = ck_ref.shape[1]
    n_keys = wb + BF16_ROWS
    lane = lax.broadcasted_iota(jnp.int32, (1, LANES), 1)
    lo_half = lane < HALF
    q_rows = qt_ref[0].astype(F32).T
    k_new = kv_ref[:, 0:A_KV_W].reshape(n_seq, seq_len, A_KV_W)
    v_new = kv_ref[:, A_KV_W:].reshape(n_seq, seq_len, A_KV_W)
    kwin_ref[:, 0:wb - seq_len, :] = ck_ref[:, seq_len:wb, :]
    vwin_ref[:, 0:wb - seq_len, :] = cv_ref[:, seq_len:wb, :]
    kwin_ref[:, wb - seq_len:wb, :] = k_new
    vwin_ref[:, wb - seq_len:wb, :] = v_new

    def to_kv_half(x, head):
        kvh = head // A_GROUPS
        if head % 2 != kvh:
            x = pltpu.roll(x, HALF, axis=1)
        return jnp.where(lo_half if kvh == 0 else ~lo_half, x, 0.0)

    lhs = jnp.concatenate(
        [to_kv_half(q_rows[:, (h // 2) * LANES:(h // 2 + 1) * LANES], h).reshape(n_seq, seq_len, LANES)
         for h in range(A_HEADS)], axis=1).astype(BF16)
    zpad = jnp.zeros((n_seq, BF16_ROWS - seq_len, A_KV_W), F32)
    k_all = jnp.concatenate([ck_ref[...], k_new, zpad], axis=1).astype(BF16)
    v_all = jnp.concatenate([cv_ref[...], v_new, zpad], axis=1).astype(BF16)
    s = jnp.einsum('sqf,skf->sqk', lhs, k_all, preferred_element_type=F32)
    nrow = A_HEADS * seq_len
    ti = lax.broadcasted_iota(jnp.int32, (nrow, n_keys), 0) & (seq_len - 1)
    ki = lax.broadcasted_iota(jnp.int32, (nrow, n_keys), 1)
    mask = ((ki < wb) & (ti + wb - ki < WINDOW)) | ((ki >= wb) & (ki - wb <= ti))
    row_head = lax.broadcasted_iota(jnp.int32, (nrow, 1), 0) >> (seq_len.bit_length() - 1)
    sk = jnp.zeros((nrow, 1), F32)
    for h in range(A_HEADS):
        sk = jnp.where(row_head == h, sink_ref[h], sk)
    s = jnp.where(mask, s, -jnp.inf)
    mx = jnp.maximum(jnp.max(s, axis=-1, keepdims=True), sk)
    p = jnp.exp(s - mx)
    den = jnp.sum(p, axis=-1, keepdims=True) + jnp.exp(sk - mx)
    o = jnp.einsum('sqk,skf->sqf', p.astype(BF16), v_all, preferred_element_type=F32) / den

    def from_kv_half(head):
        x = o[:, head * seq_len:(head + 1) * seq_len, :].reshape(GROUP, LANES)
        return pltpu.roll(x, HALF, axis=1) if head % 2 != head // A_GROUPS else x

    for c in range(A_HEADS // 2):
        h_ref[:, c * LANES:(c + 1) * LANES] = jnp.where(lo_half, from_kv_half(2 * c),
                                                        from_kv_half(2 * c + 1)).astype(h_ref.dtype)


def _swa_sample(qat, kv_a, cache_k, cache_v, sinks, seq_len):
    ngrp = qat.shape[0]
    n_seq = GROUP // seq_len
    wb = cache_k.shape[1]
    row = lambda w: pl.BlockSpec((GROUP, w), lambda i: (i, 0))
    cache = pl.BlockSpec((n_seq, wb, A_KV_W), lambda i: (i, 0, 0))
    return pl.pallas_call(
        functools.partial(_swa_sample_kernel, seq_len),
        grid=(ngrp,),
        in_specs=[pl.BlockSpec(memory_space=pltpu.SMEM), pl.BlockSpec((1, A_Q_W, LANES), lambda i: (i, 0, 0)),
                  row(2 * A_KV_W), cache, cache],
        out_specs=[row(A_Q_W), cache, cache],
        out_shape=[jax.ShapeDtypeStruct((ngrp * GROUP, A_Q_W), BF16),
                   jax.ShapeDtypeStruct(cache_k.shape, F32),
                   jax.ShapeDtypeStruct(cache_v.shape, F32)],
        compiler_params=_params(("arbitrary",)),
        name="swa_sample",
    )(sinks, qat, kv_a, cache_k, cache_v)


def _merge_ffn_kernel(x_ref, hm_ref, ha_ref, gab_ref, wa_ref, wb_ref, wo_ref, nw_ref, wg_ref, wu_ref, wd_ref,
                      y_ref):
    ga = jax.nn.sigmoid(gab_ref[:, 0:D_MODEL].astype(F32))
    gb = jax.nn.sigmoid(gab_ref[:, D_MODEL:].astype(F32))
    mix = ga * _dot(hm_ref[...], wa_ref[...]) + gb * _dot(ha_ref[...], wb_ref[...])
    x1 = x_ref[...] + _dot(mix.astype(BF16), wo_ref[...])
    hf = _rms_rows(x1, nw_ref[...]).astype(BF16)
    gate = _dot(hf, wg_ref[...])
    up = _dot(hf, wu_ref[...])
    act = (jax.nn.silu(gate) * up).astype(BF16)
    y_ref[...] = x1 + _dot(act, wd_ref[...])


def _merge_ffn(x2d, h_m, h_a, g_ab, wa, wb, wo, nw, wg, wu, wd, tm):
    n = x2d.shape[0]
    row = lambda w: pl.BlockSpec((tm, w), lambda i: (i, 0))
    return pl.pallas_call(
        _merge_ffn_kernel,
        grid=(n // tm,),
        in_specs=[row(D_MODEL), row(M_V_W), row(A_Q_W), row(2 * D_MODEL)]
                 + [_const_spec(w.shape) for w in (wa, wb, wo, nw, wg, wu, wd)],
        out_specs=row(D_MODEL),
        out_shape=jax.ShapeDtypeStruct((n, D_MODEL), F32),
        compiler_params=_params(("arbitrary",)),
        name="merge_ffn",
    )(x2d, h_m, h_a, g_ab, wa, wb, wo, nw, wg, wu, wd)


def kernel(x_prompt, x_sample, state_mlstm_C, state_mlstm_n, state_mlstm_m, cache_swa_k, cache_swa_v,
           norm_mix_w, w_in, mlstm_i_bias, mlstm_f_bias, mlstm_norm_w, q_norm_w, k_norm_w, attn_sinks,
           w_branch_a, w_branch_b, w_out, norm_ffn_w, w_gate, w_up, w_down):
    depth = w_in.shape[0]
    assert depth == 1, "single trunk layer"
    l = 0
    bp, tp = x_prompt.shape[0], x_prompt.shape[1]
    bs, ts = x_sample.shape[0], x_sample.shape[1]
    assert tp % 512 == 0 and (bs * ts) % GROUP == 0 and GROUP % ts == 0 and ts & (ts - 1) == 0
    assert ts <= SUBLANES, "sample chunk must fit one sublane tile"

    w = w_in[l]
    c_km, c_vm = M_QK_W, 2 * M_QK_W
    c_g = 2 * M_QK_W + 2 * M_V_W
    c_qa = c_g + 2 * M_HEADS
    c_ka = c_qa + A_Q_W
    wmt = jnp.concatenate([w[:, 0:c_km], w[:, c_vm:c_g]], axis=1).T.astype(BF16)
    wgt = jnp.pad(w[:, c_g:c_qa], ((0, 0), (0, BF16_ROWS - 2 * M_HEADS))).T.astype(BF16)
    wqt = w[:, c_qa:c_ka].T.astype(BF16)
    wn = jnp.concatenate([w[:, c_km:c_vm], w[:, c_ka:]], axis=1).astype(BF16)
    head_of = jnp.arange(A_KV_W) // A_HEAD_DIM
    bd = (head_of[:, None] == head_of[None, :]).astype(BF16)
    qcol = (jnp.tile(q_norm_w[l], A_HEADS) * (A_HEAD_DIM ** -0.5)).reshape(A_Q_W, 1)
    krow = jnp.tile(k_norm_w[l], A_KV_HEADS).reshape(1, A_KV_W)
    nw_mix = norm_mix_w[l].reshape(1, D_MODEL)
    nw_ffn = norm_ffn_w[l].reshape(1, D_MODEL)
    bias_col = jnp.concatenate([mlstm_i_bias[l], mlstm_f_bias[l]]).reshape(2 * M_HEADS, 1)
    nw_col = mlstm_norm_w[l].reshape(M_V_W, 1)
    sinks = attn_sinks[l]
    wa, wb, wo = w_branch_a[l].astype(BF16), w_branch_b[l].astype(BF16), w_out[l].astype(BF16)
    wg, wu, wd = w_gate[l].astype(BF16), w_up[l].astype(BF16), w_down[l].astype(BF16)
    proj = lambda x2d, tm: _proj(x2d, nw_mix, wmt, wgt, wqt, wn, bd, qcol, krow, tm)
    merge = lambda x2d, h_m, h_a, g_ab, tm: _merge_ffn(x2d, h_m, h_a, g_ab, wa, wb, wo, nw_ffn, wg, wu, wd, tm)

    xp = x_prompt.reshape(bp * tp, D_MODEL)
    qvot, gt, qat, k_m, kv_a, g_ab = proj(xp, 512)
    nblk = tp // GROUP
    s0 = jnp.zeros((bp, S_ROWS, M_QK_W), F32)
    m0 = jnp.zeros((bp, SUBLANES, LANES), F32)
    h_m, st_p, m_p = _mlstm_prompt(qvot.reshape(bp, nblk, -1, LANES), gt.reshape(bp, nblk, -1, LANES),
                                   k_m.reshape(bp, tp, M_QK_W), bias_col, nw_col, s0, m0)
    h_a, kwin_p, vwin_p = _swa_prompt(qat, kv_a.reshape(bp, tp, 2 * A_KV_W), sinks, 4)
    yp = merge(xp, h_m.reshape(bp * tp, M_V_W), h_a.reshape(bp * tp, A_Q_W), g_ab, 512).reshape(bp, tp, D_MODEL)
    c_p = jnp.swapaxes(st_p[:, :M_DV, :], 1, 2).reshape(bp, M_HEADS, M_DK, M_DV)
    n_p = st_p[:, M_DV, :].reshape(bp, M_HEADS, M_DK)
    m_pr = m_p[:, :M_HEADS, 0]

    ns = bs * ts
    xs = x_sample.reshape(ns, D_MODEL)
    tms = 512 if ns % 512 == 0 else GROUP
    qvot, gt, qat, k_m, kv_a, g_ab = proj(xs, tms)
    ngrp = ns // GROUP
    m_lanes = jnp.repeat(state_mlstm_m[l], ts, axis=0).reshape(ngrp, GROUP, M_HEADS)
    mrow = jnp.pad(jnp.swapaxes(m_lanes, 1, 2), ((0, 0), (0, SUBLANES - M_HEADS), (0, 0)))
    h_m, c_s, n_s, mt_s = _mlstm_sample(qvot, gt, k_m, mrow, bias_col, nw_col,
                                        state_mlstm_C[l].reshape(bs, M_QK_W, M_DV),
                                        state_mlstm_n[l].reshape(bs, M_QK_W), ts)
    wbuf = cache_swa_k.shape[2]
    h_a, kwin_s, vwin_s = _swa_sample(qat, kv_a, cache_swa_k[l].reshape(bs, wbuf, A_KV_W),
                                      cache_swa_v[l].reshape(bs, wbuf, A_KV_W), sinks, ts)
    ys = merge(xs, h_m, h_a, g_ab, tms).reshape(bs, ts, D_MODEL)
    m_s = jnp.swapaxes(mt_s[:, :M_HEADS, :], 1, 2).reshape(bs, ts, M_HEADS)[:, ts - 1, :]

    kv5 = lambda a: a.reshape(a.shape[0], a.shape[1], A_KV_HEADS, A_HEAD_DIM)[None]
    return (yp, ys,
            c_p[None], n_p[None], m_pr[None], kv5(kwin_p), kv5(vwin_p),
            c_s.reshape(bs, M_HEADS, M_DK, M_DV)[None], n_s.reshape(bs, M_HEADS, M_DK)[None], m_s[None],
            kv5(kwin_s), kv5(vwin_s))
```

```python
import functools

import jax
import jax.numpy as jnp
from jax import lax
from jax.experimental import pallas as pl
from jax.experimental.pallas import tpu as pltpu

F32 = jnp.float32
BF16 = jnp.bfloat16

D_MODEL = 1024
M_HEADS = 4
M_DK = 64
M_DV = 128
M_CHUNK = 64
M_QK_W = M_HEADS * M_DK
M_V_W = M_HEADS * M_DV
A_HEADS = 8
A_KV_HEADS = 2
A_HEAD_DIM = 64
A_GROUPS = A_HEADS // A_KV_HEADS
A_Q_W = A_HEADS * A_HEAD_DIM
A_KV_W = A_KV_HEADS * A_HEAD_DIM
WINDOW = 128
D_FF = 2816
EPS = 1e-6

LANES = 128
SUBLANES = 8
BF16_ROWS = 16
GROUP = 128
S_ROWS = M_DV + BF16_ROWS
VMEM_LIMIT = 56 * 1024 * 1024
DK_SHIFT = M_DK.bit_length() - 1
WINDOW_SHIFT = WINDOW.bit_length() - 1
HALF = LANES // 2
assert A_HEAD_DIM == HALF and A_KV_W == LANES, "attention head pairs share one lane-width"

NT_DIMS = (((1,), (1,)), ((), ()))


def _dot(a, b):
    return jnp.dot(a, b, preferred_element_type=F32)


def _dot_nt(a, b):
    return lax.dot_general(a, b, NT_DIMS, preferred_element_type=F32)


def _const_spec(shape):
    nd = len(shape)
    return pl.BlockSpec(shape, lambda *_: (0,) * nd, pipeline_mode=pl.Buffered(1))


def _params(sem):
    return pltpu.CompilerParams(dimension_semantics=sem, vmem_limit_bytes=VMEM_LIMIT)


def _rms_rows(x, nw):
    ms = jnp.mean(x * x, axis=-1, keepdims=True)
    return (x * lax.rsqrt(ms + EPS)) * nw


def _proj_kernel(x_ref, nw_ref, wmt_ref, wqgt_ref, wn_ref, bd_ref, qcol_ref, krow_ref,
                 qvot_ref, gt_ref, qat_ref, km_ref, kva_ref, gab_ref):
    hn = _rms_rows(x_ref[...], nw_ref[...]).astype(BF16)
    n_blk = qat_ref.shape[0]

    def put(ref, rows, val):
        for c in range(n_blk):
            ref[c, rows, :] = val[:, c * LANES:(c + 1) * LANES].astype(ref.dtype)

    put(qvot_ref, slice(None), _dot_nt(wmt_ref[...], hn))
    qt = _dot_nt(wqgt_ref[...], hn)
    put(gt_ref, slice(None), qt[A_Q_W:A_Q_W + 2 * M_HEADS])
    for h in range(A_HEADS):
        hs = slice(h * A_HEAD_DIM, (h + 1) * A_HEAD_DIM)
        blk = qt[hs]
        ssq_q = jnp.sum(blk * blk, axis=0, keepdims=True)
        put(qat_ref, hs, (blk * lax.rsqrt(ssq_q * (1.0 / A_HEAD_DIM) + EPS)) * qcol_ref[hs])
    km_ref[...] = _dot(hn, wn_ref[:, 0:M_QK_W]).astype(km_ref.dtype)
    gab_ref[...] = _dot(hn, wn_ref[:, M_QK_W + 2 * A_KV_W:]).astype(gab_ref.dtype)
    kv = _dot(hn, wn_ref[:, M_QK_W:M_QK_W + 2 * A_KV_W])
    k = kv[:, 0:A_KV_W]
    ksq = k * k
    hi = ksq.astype(BF16)
    lo = (ksq - hi.astype(F32)).astype(BF16)
    ssq = _dot(hi, bd_ref[...]) + _dot(lo, bd_ref[...])
    kva_ref[:, 0:A_KV_W] = (k * lax.rsqrt(ssq * (1.0 / A_HEAD_DIM) + EPS)) * krow_ref[...]
    kva_ref[:, A_KV_W:] = kv[:, A_KV_W:]


def _proj(x2d, nw, wmt, wqgt, wn, bd, qcol, krow, tm):
    n = x2d.shape[0]
    row = lambda w: pl.BlockSpec((tm, w), lambda i: (i, 0))
    n_blk = tm // LANES
    slab = lambda r: pl.BlockSpec((n_blk, r, LANES), lambda i: (i, 0, 0))
    w_qvo = M_QK_W + 2 * M_V_W
    return pl.pallas_call(
        _proj_kernel,
        grid=(n // tm,),
        in_specs=[row(D_MODEL)] + [_const_spec(a.shape) for a in (nw, wmt, wqgt, wn, bd, qcol, krow)],
        out_specs=[slab(w_qvo), slab(2 * M_HEADS), slab(A_Q_W), row(M_QK_W), row(2 * A_KV_W), row(2 * D_MODEL)],
        out_shape=[jax.ShapeDtypeStruct((n // LANES, w_qvo, LANES), BF16),
                   jax.ShapeDtypeStruct((n // LANES, 2 * M_HEADS, LANES), F32),
                   jax.ShapeDtypeStruct((n // LANES, A_Q_W, LANES), BF16),
                   jax.ShapeDtypeStruct((n, M_QK_W), BF16),
                   jax.ShapeDtypeStruct((n, 2 * A_KV_W), F32),
                   jax.ShapeDtypeStruct((n, 2 * D_MODEL), BF16)],
        compiler_params=_params(("arbitrary",)),
        name="proj",
    )(x2d, nw, wmt, wqgt, wn, bd, qcol, krow)


def _split3_rows(x):
    hi = x.astype(BF16).astype(F32)
    r1 = x - hi
    mid = r1.astype(BF16).astype(F32)
    lo = r1 - mid
    return jnp.concatenate([hi, mid, lo], axis=0).astype(BF16)


def _log_sigmoid(x):
    return jnp.minimum(x, 0.0) - jnp.log1p(jnp.exp(-jnp.abs(x)))


def _chunk_masks(chunk_shift):
    s = lax.broadcasted_iota(jnp.int32, (GROUP, GROUP), 0)
    t = lax.broadcasted_iota(jnp.int32, (GROUP, GROUP), 1)
    same = (s >> chunk_shift) == (t >> chunk_shift)
    return same, same & (s <= t)


def _group_gates(gt, same, causal):
    cm_bf = jnp.where(causal, 1.0, 0.0).astype(BF16)
    lf = _log_sigmoid(gt)
    nr = gt.shape[0]
    bt3 = _dot(_split3_rows(lf), cm_bf)
    bt = (bt3[0:nr] + bt3[nr:2 * nr]) + bt3[2 * nr:3 * nr]
    b4 = bt[M_HEADS:2 * M_HEADS]
    a4 = gt[0:M_HEADS] - b4
    a8 = jnp.concatenate([a4, a4], axis=0)
    a_cols = jnp.concatenate([a8, jnp.zeros((GROUP - SUBLANES, GROUP), F32)], axis=0).T
    at_mats, run_rows, chunk_rows = [], [], []
    for h in range(M_HEADS):
        at = jnp.broadcast_to(a_cols[:, h:h + 1], (GROUP, GROUP))
        at_mats.append(at)
        run_rows.append(jnp.max(jnp.where(causal, at, -jnp.inf), axis=0, keepdims=True))
        chunk_rows.append(jnp.max(jnp.where(same, at, -jnp.inf), axis=0, keepdims=True))
    return b4, a4, at_mats, jnp.concatenate(run_rows, axis=0), jnp.concatenate(chunk_rows, axis=0)


def _group_weights(m_prev, b4, a4, run4, chunk4):
    big_m = jnp.maximum(m_prev, run4)
    m_last = jnp.maximum(m_prev, chunk4)
    w_inter = jnp.exp(m_prev - big_m)
    g_vec = jnp.exp(m_prev - m_last)
    w_last = jnp.exp(a4 - m_last)
    m_t = b4 + big_m
    return big_m, w_inter, g_vec, w_last, m_t, jnp.exp(-m_t)


def _group_intra(qvot, ks, at_mats, big_m, causal):
    lane_head = lax.broadcasted_iota(jnp.int32, (1, M_QK_W), 1) >> DK_SHIFT
    row_head = lax.broadcasted_iota(jnp.int32, (M_QK_W, 1), 0) >> DK_SHIFT
    ones_rows = jnp.where(lax.broadcasted_iota(jnp.int32, (BF16_ROWS, GROUP), 0) == 0, 1.0, 0.0).astype(BF16)
    qt = qvot(0, M_QK_W)
    k_stack = jnp.concatenate([jnp.where(lane_head == h, ks, jnp.zeros_like(ks)) for h in range(M_HEADS)], axis=0)
    qw = jnp.concatenate([jnp.where(row_head == h, qt, jnp.zeros_like(qt)) for h in range(M_HEADS)], axis=1)
    sc_t = _dot(k_stack, qt)
    zero_blk = jnp.zeros((GROUP, GROUP), BF16)
    vta, s_rows = [], []
    for h in range(M_HEADS):
        w_t = jnp.where(causal, jnp.exp(at_mats[h] - big_m[h:h + 1]), 0.0)
        s_t = (sc_t[h * GROUP:(h + 1) * GROUP] * w_t).astype(BF16)
        s_rows.append(jnp.concatenate([s_t if j == h else zero_blk for j in range(M_HEADS)], axis=1))
        vta.append(jnp.concatenate([qvot(M_QK_W + h * M_DV, M_QK_W + (h + 1) * M_DV), ones_rows], axis=0))
    intra = _dot(jnp.concatenate(vta, axis=1), jnp.concatenate(s_rows, axis=0))
    return k_stack, qw, vta, intra


def _weighted_values(vta, w_rows):
    return jnp.concatenate([(vta[h].astype(F32) * w_rows[h:h + 1]).astype(BF16) for h in range(M_HEADS)], axis=1)


def _lanes_x(rows):
    return jnp.concatenate([rows[h:h + 1] for h in range(M_HEADS)], axis=1)


def _group_out(qvot, inter, intra, w_inter, e_neg_m, nw_ref):
    outs = []
    out_all = inter * _lanes_x(w_inter) + intra
    for h in range(M_HEADS):
        out_t = out_all[:, h * GROUP:(h + 1) * GROUP]
        hh = out_t[0:M_DV] / jnp.maximum(jnp.abs(out_t[M_DV:M_DV + 1]), e_neg_m[h:h + 1])
        ms = jnp.mean(hh * hh, axis=0, keepdims=True)
        hn = (hh * lax.rsqrt(ms + EPS)) * nw_ref[h * M_DV:(h + 1) * M_DV]
        o_t = qvot(M_QK_W + M_V_W + h * M_DV, M_QK_W + M_V_W + (h + 1) * M_DV)
        outs.append(hn * jax.nn.sigmoid(o_t.astype(F32)))
    return jnp.concatenate(outs, axis=0).T


def _decay_row(g_vec, lane0):
    lane_head = lax.broadcasted_iota(jnp.int32, (1, M_QK_W), 1) >> DK_SHIFT
    g_row = jnp.zeros((1, M_QK_W), F32)
    for h in range(M_HEADS):
        g_row = jnp.where(lane_head == h, g_vec[h:h + 1, lane0:lane0 + 1], g_row)
    return g_row


def _mlstm_sample_kernel(seq_len, qvot_ref, gt_ref, k_ref, mrow_ref, bias_ref, nw_ref, c_ref, n_ref,
                         h_ref, c_out_ref, n_out_ref, mt_ref):
    n_seq = GROUP // seq_len
    shift = seq_len.bit_length() - 1
    same, causal = _chunk_masks(shift)
    lane_seq = lax.broadcasted_iota(jnp.int32, (1, GROUP), 1) >> shift
    lane_seq_x = jnp.concatenate([lane_seq] * M_HEADS, axis=1)
    row0 = lax.broadcasted_iota(jnp.int32, (BF16_ROWS, M_QK_W), 0) == 0
    qvot = lambda r0, r1: qvot_ref[0, r0:r1, :]
    ks = k_ref[...] * (M_DK ** -0.5)
    b4, a4, at_mats, run4, chunk4 = _group_gates(gt_ref[0] + bias_ref[...], same, causal)
    big_m, w_inter, g_vec, w_last, m_t, e_neg_m = _group_weights(mrow_ref[0, 0:M_HEADS, :], b4, a4, run4, chunk4)
    mt_ref[0] = jnp.concatenate([m_t, m_t], axis=0)
    k_stack, qw, vta, intra = _group_intra(qvot, ks, at_mats, big_m, causal)

    st_old = []
    for s in range(n_seq):
        n_rows = jnp.where(row0, jnp.broadcast_to(n_ref[s:s + 1, :], (BF16_ROWS, M_QK_W)), 0.0)
        st_old.append(jnp.concatenate([c_ref[s].T, n_rows], axis=0))
    inter_all = _dot(jnp.concatenate(st_old, axis=0).astype(BF16), qw)
    inter = inter_all[0:S_ROWS]
    for s in range(1, n_seq):
        inter = jnp.where(lane_seq_x == s, inter_all[s * S_ROWS:(s + 1) * S_ROWS], inter)
    tall = jnp.concatenate([_weighted_values(vta, jnp.where(lane_seq == s, w_last, 0.0)) for s in range(n_seq)],
                           axis=0)
    d_st = _dot(tall, k_stack)
    for s in range(n_seq):
        st_new = _decay_row(g_vec, s * seq_len) * st_old[s] + d_st[s * S_ROWS:(s + 1) * S_ROWS]
        c_out_ref[s] = st_new[0:M_DV].T
        n_out_ref[s:s + 1, :] = st_new[M_DV:M_DV + 1]
    h_ref[...] = _group_out(qvot, inter, intra, w_inter, e_neg_m, nw_ref).astype(h_ref.dtype)


def _mlstm_sample(qvot, gt, k_m, mrow, bias_col, nw_col, c, n, seq_len):
    ngrp = qvot.shape[0]
    n_seq = GROUP // seq_len
    full = lambda a: pl.BlockSpec(a.shape, lambda i: (0,) * a.ndim)
    slab = lambda a: pl.BlockSpec((1,) + a.shape[1:], lambda i: (i, 0, 0))
    row = lambda w: pl.BlockSpec((GROUP, w), lambda i: (i, 0))
    c_spec = pl.BlockSpec((n_seq,) + c.shape[1:], lambda i: (i, 0, 0))
    n_spec = pl.BlockSpec((n_seq, n.shape[1]), lambda i: (i, 0))
    return pl.pallas_call(
        functools.partial(_mlstm_sample_kernel, seq_len),
        grid=(ngrp,),
        in_specs=[slab(qvot), slab(gt), row(M_QK_W), slab(mrow), full(bias_col), full(nw_col), c_spec, n_spec],
        out_specs=[row(M_V_W), c_spec, n_spec, slab(mrow)],
        out_shape=[jax.ShapeDtypeStruct((ngrp * GROUP, M_V_W), BF16),
                   jax.ShapeDtypeStruct(c.shape, F32),
                   jax.ShapeDtypeStruct(n.shape, F32),
                   jax.ShapeDtypeStruct(mrow.shape, F32)],
        compiler_params=_params(("arbitrary",)),
        name="mlstm_sample",
    )(qvot, gt, k_m, mrow, bias_col, nw_col, c, n)


def _swa_prompt_body(nqb, has_prev, sink_ref, qt_ref, kv_ref, kvp_ref, h_ref, kwin_ref, vwin_ref):
    hd = A_HEAD_DIM
    nq = A_GROUPS * WINDOW
    si = lax.broadcasted_iota(jnp.int32, (2 * WINDOW, nq), 0)
    qi = lax.broadcasted_iota(jnp.int32, (2 * WINDOW, nq), 1) & (WINDOW - 1)
    local = ((si < WINDOW) & (si > qi)) | ((si >= WINDOW) & (si - WINDOW <= qi))
    first = local & (has_prev | (si >= WINDOW))
    lane_grp = lax.broadcasted_iota(jnp.int32, (1, nq), 1) >> WINDOW_SHIFT
    kv_blocks = [kvp_ref[...]] + [kv_ref[i * WINDOW:(i + 1) * WINDOW, :] for i in range(nqb)]
    k_bf = [blk[:, 0:A_KV_W].astype(BF16) for blk in kv_blocks]
    vt_bf = [blk[:, A_KV_W:].T.astype(BF16) for blk in kv_blocks]
    zeros = jnp.zeros((hd, nq), BF16)
    sinks = []
    for kvh in range(A_KV_HEADS):
        sk = jnp.zeros((1, nq), F32)
        for g in range(A_GROUPS):
            sk = jnp.where(lane_grp == g, sink_ref[kvh * A_GROUPS + g], sk)
        sinks.append(sk)
    for qb in range(nqb):
        kk = jnp.concatenate([k_bf[qb], k_bf[qb + 1]], axis=0)
        vt = jnp.concatenate([vt_bf[qb], vt_bf[qb + 1]], axis=1)
        mask = local if qb > 0 else first
        pieces = []
        for kvh in range(A_KV_HEADS):
            q4t = jnp.concatenate(
                [qt_ref[qb, (kvh * A_GROUPS + g) * hd:(kvh * A_GROUPS + g + 1) * hd, :] for g in range(A_GROUPS)],
                axis=1)
            wq = jnp.concatenate([q4t, zeros] if kvh == 0 else [zeros, q4t], axis=0)
            s = jnp.where(mask, _dot(kk, wq), -jnp.inf)
            sk = sinks[kvh]
            mx = jnp.maximum(jnp.max(s, axis=0, keepdims=True), sk)
            p = jnp.exp(s - mx)
            den = jnp.sum(p, axis=0, keepdims=True) + jnp.exp(sk - mx)
            ot = _dot(vt, p.astype(BF16))[kvh * hd:(kvh + 1) * hd] / den
            pieces += [ot[:, g * WINDOW:(g + 1) * WINDOW] for g in range(A_GROUPS)]
        h_t = jnp.concatenate(pieces, axis=0)
        h_ref[qb * WINDOW:(qb + 1) * WINDOW, :] = h_t.T.astype(h_ref.dtype)
    kwin_ref[...] = kv_blocks[nqb][:, 0:A_KV_W]
    vwin_ref[...] = kv_blocks[nqb][:, A_KV_W:]


def _front_kernel(tiles_per_seq, sink_ref, x_ref, nw_ref, wmt_ref, wqgt_ref, wn_ref, bd_ref, qcol_ref, krow_ref,
                  bias_ref, nwm_ref,
                  gab_ref, hm_ref, ha_ref, kwin_ref, vwin_ref, st_ref, m_ref,
                  q_qvot, q_gt, q_qat, q_km, q_kva, p_qvot, p_gt, p_qat, p_km, p_kva, st_s, m_s, kvp_s):
    k = pl.program_id(0)

    @pl.when(k == 0)
    def _():
        for r in (p_qvot, p_gt, p_qat, p_km, p_kva, st_s, m_s, kvp_s):
            r[...] = jnp.zeros(r.shape, r.dtype)

    _proj_kernel(x_ref, nw_ref, wmt_ref, wqgt_ref, wn_ref, bd_ref, qcol_ref, krow_ref,
                 q_qvot, q_gt, q_qat, q_km, q_kva, gab_ref)

    seq_start = lax.rem(k - 1 + tiles_per_seq, tiles_per_seq) == 0
    n_blk = p_qat.shape[0]
    same, causal = _chunk_masks(GROUP.bit_length() - 1)
    st = jnp.where(seq_start, 0.0, st_s[...])
    m_col = jnp.where(seq_start, 0.0, m_s[0:M_HEADS, 0:1])
    for g in range(n_blk):
        qvot = lambda r0, r1, g=g: p_qvot[g, r0:r1, :]
        ks = p_km[g * GROUP:(g + 1) * GROUP, :] * (M_DK ** -0.5)
        b4, a4, at_mats, run4, chunk4 = _group_gates(p_gt[g] + bias_ref[...], same, causal)
        big_m, w_inter, g_vec, w_last, m_t, e_neg_m = _group_weights(
            jnp.broadcast_to(m_col, (M_HEADS, GROUP)), b4, a4, run4, chunk4)
        m_col = m_t[:, GROUP - 1:GROUP]
        k_stack, qw, vta, intra = _group_intra(qvot, ks, at_mats, big_m, causal)
        inter = _dot(st.astype(BF16), qw)
        st = _decay_row(g_vec, 0) * st + _dot(_weighted_values(vta, w_last), k_stack)
        hm_ref[g * GROUP:(g + 1) * GROUP, :] = _group_out(qvot, inter, intra, w_inter, e_neg_m,
                                                          nwm_ref).astype(hm_ref.dtype)
    st_s[...] = st
    st_ref[0] = st
    m_rows = jnp.broadcast_to(m_col, (M_HEADS, LANES))
    m_rows = jnp.concatenate([m_rows, m_rows], axis=0)
    m_s[...] = m_rows
    m_ref[0] = m_rows

    _swa_prompt_body(n_blk, jnp.logical_not(seq_start), sink_ref, p_qat, p_kva, kvp_s, ha_ref,
                     kwin_ref.at[0], vwin_ref.at[0])
    kvp_s[...] = p_kva[p_kva.shape[0] - WINDOW:, :]

    for p, q in ((p_qvot, q_qvot), (p_gt, q_gt), (p_qat, q_qat), (p_km, q_km), (p_kva, q_kva)):
        p[...] = q[...]


def _front(x2d, nb, sinks, nw, wmt, wqgt, wn, bd, qcol, krow, bias_col, nwm_col, tm):
    n = x2d.shape[0]
    n_tiles = n // tm
    tps = n_tiles // nb
    n_blk = tm // LANES
    w_qvo = M_QK_W + 2 * M_V_W
    cur = lambda w: pl.BlockSpec((tm, w), lambda k: (jnp.minimum(k, n_tiles - 1), 0))
    prev = lambda w: pl.BlockSpec((tm, w), lambda k: (jnp.maximum(k - 1, 0), 0))
    per_seq = lambda r, w: pl.BlockSpec((1, r, w), lambda k: (jnp.maximum(k - 1, 0) // tps, 0, 0))
    proj_scratch = [pltpu.VMEM((n_blk, w_qvo, LANES), BF16), pltpu.VMEM((n_blk, 2 * M_HEADS, LANES), F32),
                    pltpu.VMEM((n_blk, A_Q_W, LANES), BF16), pltpu.VMEM((tm, M_QK_W), BF16),
                    pltpu.VMEM((tm, 2 * A_KV_W), F32)]
    return pl.pallas_call(
        functools.partial(_front_kernel, tps),
        grid=(n_tiles + 1,),
        in_specs=[pl.BlockSpec(memory_space=pltpu.SMEM), cur(D_MODEL)]
                 + [_const_spec(a.shape) for a in (nw, wmt, wqgt, wn, bd, qcol, krow, bias_col, nwm_col)],
        out_specs=[cur(2 * D_MODEL), prev(M_V_W), prev(A_Q_W), per_seq(WINDOW, A_KV_W), per_seq(WINDOW, A_KV_W),
                   per_seq(S_ROWS, M_QK_W), per_seq(SUBLANES, LANES)],
        out_shape=[jax.ShapeDtypeStruct((n, 2 * D_MODEL), BF16),
                   jax.ShapeDtypeStruct((n, M_V_W), BF16),
                   jax.ShapeDtypeStruct((n, A_Q_W), BF16),
                   jax.ShapeDtypeStruct((nb, WINDOW, A_KV_W), F32),
                   jax.ShapeDtypeStruct((nb, WINDOW, A_KV_W), F32),
                   jax.ShapeDtypeStruct((nb, S_ROWS, M_QK_W), F32),
                   jax.ShapeDtypeStruct((nb, SUBLANES, LANES), F32)],
        scratch_shapes=proj_scratch + proj_scratch + [pltpu.VMEM((S_ROWS, M_QK_W), F32),
                                                       pltpu.VMEM((SUBLANES, LANES), F32),
                                                       pltpu.VMEM((WINDOW, 2 * A_KV_W), F32)],
        compiler_params=_params(("arbitrary",)),
        name="front",
    )(sinks, x2d, nw, wmt, wqgt, wn, bd, qcol, krow, bias_col, nwm_col)


def _swa_sample_kernel(seq_len, sink_ref, qt_ref, kv_ref, ck_ref, cv_ref, h_ref, kwin_ref, vwin_ref):
    n_seq = GROUP // seq_len
    wb = ck_ref.shape[1]
    n_keys = wb + BF16_ROWS
    lane = lax.broadcasted_iota(jnp.int32, (1, LANES), 1)
    lo_half = lane < HALF
    q_rows = qt_ref[0].astype(F32).T
    k_new = kv_ref[:, 0:A_KV_W].reshape(n_seq, seq_len, A_KV_W)
    v_new = kv_ref[:, A_KV_W:].reshape(n_seq, seq_len, A_KV_W)
    kwin_ref[:, 0:wb - seq_len, :] = ck_ref[:, seq_len:wb, :]
    vwin_ref[:, 0:wb - seq_len, :] = cv_ref[:, seq_len:wb, :]
    kwin_ref[:, wb - seq_len:wb, :] = k_new
    vwin_ref[:, wb - seq_len:wb, :] = v_new

    def to_kv_half(x, head):
        kvh = head // A_GROUPS
        if head % 2 != kvh:
            x = pltpu.roll(x, HALF, axis=1)
        return jnp.where(lo_half if kvh == 0 else ~lo_half, x, 0.0)

    lhs = jnp.concatenate(
        [to_kv_half(q_rows[:, (h // 2) * LANES:(h // 2 + 1) * LANES], h).reshape(n_seq, seq_len, LANES)
         for h in range(A_HEADS)], axis=1).astype(BF16)
    zpad = jnp.zeros((n_seq, BF16_ROWS - seq_len, A_KV_W), F32)
    k_all = jnp.concatenate([ck_ref[...], k_new, zpad], axis=1).astype(BF16)
    v_all = jnp.concatenate([cv_ref[...], v_new, zpad], axis=1).astype(BF16)
    s = jnp.einsum('sqf,skf->sqk', lhs, k_all, preferred_element_type=F32)
    nrow = A_HEADS * seq_len
    ti = lax.broadcasted_iota(jnp.int32, (nrow, n_keys), 0) & (seq_len - 1)
    ki = lax.broadcasted_iota(jnp.int32, (nrow, n_keys), 1)
    mask = ((ki < wb) & (ti + wb - ki < WINDOW)) | ((ki >= wb) & (ki - wb <= ti))
    row_head = lax.broadcasted_iota(jnp.int32, (nrow, 1), 0) >> (seq_len.bit_length() - 1)
    sk = jnp.zeros((nrow, 1), F32)
    for h in range(A_HEADS):
        sk = jnp.where(row_head == h, sink_ref[h], sk)
    s = jnp.where(mask, s, -jnp.inf)
    mx = jnp.maximum(jnp.max(s, axis=-1, keepdims=True), sk)
    p = jnp.exp(s - mx)
    den = jnp.sum(p, axis=-1, keepdims=True) + jnp.exp(sk - mx)
    o = jnp.einsum('sqk,skf->sqf', p.astype(BF16), v_all, preferred_element_type=F32) / den

    def from_kv_half(head):
        x = o[:, head * seq_len:(head + 1) * seq_len, :].reshape(GROUP, LANES)
        return pltpu.roll(x, HALF, axis=1) if head % 2 != head // A_GROUPS else x

    for c in range(A_HEADS // 2):
        h_ref[:, c * LANES:(c + 1) * LANES] = jnp.where(lo_half, from_kv_half(2 * c),
                                                        from_kv_half(2 * c + 1)).astype(h_ref.dtype)


def _swa_sample(qat, kv_a, cache_k, cache_v, sinks, seq_len):
    ngrp = qat.shape[0]
    n_seq = GROUP // seq_len
    wb = cache_k.shape[1]
    row = lambda w: pl.BlockSpec((GROUP, w), lambda i: (i, 0))
    cache = pl.BlockSpec((n_seq, wb, A_KV_W), lambda i: (i, 0, 0))
    return pl.pallas_call(
        functools.partial(_swa_sample_kernel, seq_len),
        grid=(ngrp,),
        in_specs=[pl.BlockSpec(memory_space=pltpu.SMEM), pl.BlockSpec((1, A_Q_W, LANES), lambda i: (i, 0, 0)),
                  row(2 * A_KV_W), cache, cache],
        out_specs=[row(A_Q_W), cache, cache],
        out_shape=[jax.ShapeDtypeStruct((ngrp * GROUP, A_Q_W), BF16),
                   jax.ShapeDtypeStruct(cache_k.shape, F32),
                   jax.ShapeDtypeStruct(cache_v.shape, F32)],
        compiler_params=_params(("arbitrary",)),
        name="swa_sample",
    )(sinks, qat, kv_a, cache_k, cache_v)


def _merge_ffn_kernel(x_ref, hm_ref, ha_ref, gab_ref, wa_ref, wb_ref, wo_ref, nw_ref, wg_ref, wu_ref, wd_ref,
                      y_ref):
    ga = jax.nn.sigmoid(gab_ref[:, 0:D_MODEL].astype(F32))
    gb = jax.nn.sigmoid(gab_ref[:, D_MODEL:].astype(F32))
    mix = ga * _dot(hm_ref[...], wa_ref[...]) + gb * _dot(ha_ref[...], wb_ref[...])
    x1 = x_ref[...] + _dot(mix.astype(BF16), wo_ref[...])
    hf = _rms_rows(x1, nw_ref[...]).astype(BF16)
    gate = _dot(hf, wg_ref[...])
    up = _dot(hf, wu_ref[...])
    act = (jax.nn.silu(gate) * up).astype(BF16)
    y_ref[...] = x1 + _dot(act, wd_ref[...])


def _merge_ffn(x2d, h_m, h_a, g_ab, wa, wb, wo, nw, wg, wu, wd, tm):
    n = x2d.shape[0]
    row = lambda w: pl.BlockSpec((tm, w), lambda i: (i, 0))
    return pl.pallas_call(
        _merge_ffn_kernel,
        grid=(n // tm,),
        in_specs=[row(D_MODEL), row(M_V_W), row(A_Q_W), row(2 * D_MODEL)]
                 + [_const_spec(w.shape) for w in (wa, wb, wo, nw, wg, wu, wd)],
        out_specs=row(D_MODEL),
        out_shape=jax.ShapeDtypeStruct((n, D_MODEL), F32),
        compiler_params=_params(("arbitrary",)),
        name="merge_ffn",
    )(x2d, h_m, h_a, g_ab, wa, wb, wo, nw, wg, wu, wd)


def kernel(x_prompt, x_sample, state_mlstm_C, state_mlstm_n, state_mlstm_m, cache_swa_k, cache_swa_v,
           norm_mix_w, w_in, mlstm_i_bias, mlstm_f_bias, mlstm_norm_w, q_norm_w, k_norm_w, attn_sinks,
           w_branch_a, w_branch_b, w_out, norm_ffn_w, w_gate, w_up, w_down):
    depth = w_in.shape[0]
    assert depth == 1, "single trunk layer"
    l = 0
    bp, tp = x_prompt.shape[0], x_prompt.shape[1]
    bs, ts = x_sample.shape[0], x_sample.shape[1]
    assert tp % 512 == 0 and (bs * ts) % GROUP == 0 and GROUP % ts == 0 and ts & (ts - 1) == 0
    assert ts <= SUBLANES, "sample chunk must fit one sublane tile"

    w = w_in[l]
    c_km, c_vm = M_QK_W, 2 * M_QK_W
    c_g = 2 * M_QK_W + 2 * M_V_W
    c_qa = c_g + 2 * M_HEADS
    c_ka = c_qa + A_Q_W
    wmt = jnp.concatenate([w[:, 0:c_km], w[:, c_vm:c_g]], axis=1).T.astype(BF16)
    wqgt = jnp.concatenate([w[:, c_qa:c_ka], jnp.pad(w[:, c_g:c_qa], ((0, 0), (0, BF16_ROWS - 2 * M_HEADS)))],
                           axis=1).T.astype(BF16)
    wn = jnp.concatenate([w[:, c_km:c_vm], w[:, c_ka:]], axis=1).astype(BF16)
    head_of = jnp.arange(A_KV_W) // A_HEAD_DIM
    bd = (head_of[:, None] == head_of[None, :]).astype(BF16)
    qcol = (jnp.tile(q_norm_w[l], A_HEADS) * (A_HEAD_DIM ** -0.5)).reshape(A_Q_W, 1)
    krow = jnp.tile(k_norm_w[l], A_KV_HEADS).reshape(1, A_KV_W)
    nw_mix = norm_mix_w[l].reshape(1, D_MODEL)
    nw_ffn = norm_ffn_w[l].reshape(1, D_MODEL)
    bias_col = jnp.concatenate([mlstm_i_bias[l], mlstm_f_bias[l]]).reshape(2 * M_HEADS, 1)
    nw_col = mlstm_norm_w[l].reshape(M_V_W, 1)
    sinks = attn_sinks[l]
    wa, wb, wo = w_branch_a[l].astype(BF16), w_branch_b[l].astype(BF16), w_out[l].astype(BF16)
    wg, wu, wd = w_gate[l].astype(BF16), w_up[l].astype(BF16), w_down[l].astype(BF16)
    proj = lambda x2d, tm: _proj(x2d, nw_mix, wmt, wqgt, wn, bd, qcol, krow, tm)
    merge = lambda x2d, h_m, h_a, g_ab, tm: _merge_ffn(x2d, h_m, h_a, g_ab, wa, wb, wo, nw_ffn, wg, wu, wd, tm)

    xp = x_prompt.reshape(bp * tp, D_MODEL)
    g_ab, h_m, h_a, kwin_p, vwin_p, st_p, m_p = _front(xp, bp, sinks, nw_mix, wmt, wqgt, wn, bd, qcol, krow,
                                                       bias_col, nw_col, 512)
    yp = merge(xp, h_m, h_a, g_ab, 512).reshape(bp, tp, D_MODEL)
    c_p = jnp.swapaxes(st_p[:, :M_DV, :], 1, 2).reshape(bp, M_HEADS, M_DK, M_DV)
    n_p = st_p[:, M_DV, :].reshape(bp, M_HEADS, M_DK)
    m_pr = m_p[:, :M_HEADS, 0]

    ns = bs * ts
    xs = x_sample.reshape(ns, D_MODEL)
    tms = 512 if ns % 512 == 0 else GROUP
    qvot, gt, qat, k_m, kv_a, g_ab = proj(xs, tms)
    ngrp = ns // GROUP
    m_lanes = jnp.repeat(state_mlstm_m[l], ts, axis=0).reshape(ngrp, GROUP, M_HEADS)
    mrow = jnp.pad(jnp.swapaxes(m_lanes, 1, 2), ((0, 0), (0, SUBLANES - M_HEADS), (0, 0)))
    h_m, c_s, n_s, mt_s = _mlstm_sample(qvot, gt, k_m, mrow, bias_col, nw_col,
                                        state_mlstm_C[l].reshape(bs, M_QK_W, M_DV),
                                        state_mlstm_n[l].reshape(bs, M_QK_W), ts)
    wbuf = cache_swa_k.shape[2]
    h_a, kwin_s, vwin_s = _swa_sample(qat, kv_a, cache_swa_k[l].reshape(bs, wbuf, A_KV_W),
                                      cache_swa_v[l].reshape(bs, wbuf, A_KV_W), sinks, ts)
    ys = merge(xs, h_m, h_a, g_ab, tms).reshape(bs, ts, D_MODEL)
    m_s = jnp.swapaxes(mt_s[:, :M_HEADS, :], 1, 2).reshape(bs, ts, M_HEADS)[:, ts - 1, :]

    kv5 = lambda a: a.reshape(a.shape[0], a.shape[1], A_KV_HEADS, A_HEAD_DIM)[None]
    return (yp, ys,
            c_p[None], n_p[None], m_pr[None], kv5(kwin_p), kv5(vwin_p),
            c_s.reshape(bs, M_HEADS, M_DK, M_DV)[None], n_s.reshape(bs, M_HEADS, M_DK)[None], m_s[None],
            kv5(kwin_s), kv5(vwin_s))
```

```python
import functools

import jax
import jax.numpy as jnp
from jax import lax
from jax.experimental import pallas as pl
from jax.experimental.pallas import tpu as pltpu

F32 = jnp.float32
BF16 = jnp.bfloat16

D_MODEL = 1024
M_HEADS = 4
M_DK = 64
M_DV = 128
M_CHUNK = 64
M_QK_W = M_HEADS * M_DK
M_V_W = M_HEADS * M_DV
A_HEADS = 8
A_KV_HEADS = 2
A_HEAD_DIM = 64
A_GROUPS = A_HEADS // A_KV_HEADS
A_Q_W = A_HEADS * A_HEAD_DIM
A_KV_W = A_KV_HEADS * A_HEAD_DIM
WINDOW = 128
D_FF = 2816
EPS = 1e-6

LANES = 128
SUBLANES = 8
BF16_ROWS = 16
GROUP = 128
S_ROWS = M_DV + BF16_ROWS
VMEM_LIMIT = 56 * 1024 * 1024
DK_SHIFT = M_DK.bit_length() - 1
WINDOW_SHIFT = WINDOW.bit_length() - 1
HALF = LANES // 2
assert A_HEAD_DIM == HALF and A_KV_W == LANES, "attention head pairs share one lane-width"

NT_DIMS = (((1,), (1,)), ((), ()))


def _dot(a, b):
    return jnp.dot(a, b, preferred_element_type=F32)


def _dot_nt(a, b):
    return lax.dot_general(a, b, NT_DIMS, preferred_element_type=F32)


def _const_spec(shape):
    nd = len(shape)
    return pl.BlockSpec(shape, lambda *_: (0,) * nd, pipeline_mode=pl.Buffered(1))


def _params(sem):
    return pltpu.CompilerParams(dimension_semantics=sem, vmem_limit_bytes=VMEM_LIMIT)


def _rms_rows(x, nw):
    ms = jnp.mean(x * x, axis=-1, keepdims=True)
    return (x * lax.rsqrt(ms + EPS)) * nw


GAB_CHUNK = 512


def _proj_chunks(x_ref, nw_ref, wmt_ref, wqgt_ref, wn_ref, bd_ref, qcol_ref, krow_ref,
                 qvot_ref, gt_ref, qat_ref, km_ref, kva_ref, gab_ref):
    hn = _rms_rows(x_ref[...], nw_ref[...]).astype(BF16)
    n_blk = qat_ref.shape[0]

    def put(ref, rows, val):
        for c in range(n_blk):
            ref[c, rows, :] = val[:, c * LANES:(c + 1) * LANES].astype(ref.dtype)

    def mlstm_qvo():
        put(qvot_ref, slice(None), _dot_nt(wmt_ref[...], hn))

    def attn_q_and_gates():
        qt = _dot_nt(wqgt_ref[...], hn)
        put(gt_ref, slice(None), qt[A_Q_W:A_Q_W + 2 * M_HEADS])
        for h in range(A_HEADS):
            hs = slice(h * A_HEAD_DIM, (h + 1) * A_HEAD_DIM)
            blk = qt[hs]
            ssq_q = jnp.sum(blk * blk, axis=0, keepdims=True)
            put(qat_ref, hs, (blk * lax.rsqrt(ssq_q * (1.0 / A_HEAD_DIM) + EPS)) * qcol_ref[hs])

    def mlstm_k():
        km_ref[...] = _dot(hn, wn_ref[:, 0:M_QK_W]).astype(km_ref.dtype)

    def branch_gates(c0):
        def run():
            w0 = M_QK_W + 2 * A_KV_W + c0
            gab_ref[:, c0:c0 + GAB_CHUNK] = _dot(hn, wn_ref[:, w0:w0 + GAB_CHUNK]).astype(gab_ref.dtype)
        return run

    def attn_kv():
        kv = _dot(hn, wn_ref[:, M_QK_W:M_QK_W + 2 * A_KV_W])
        k = kv[:, 0:A_KV_W]
        ksq = k * k
        hi = ksq.astype(BF16)
        lo = (ksq - hi.astype(F32)).astype(BF16)
        ssq = _dot(hi, bd_ref[...]) + _dot(lo, bd_ref[...])
        kva_ref[:, 0:A_KV_W] = (k * lax.rsqrt(ssq * (1.0 / A_HEAD_DIM) + EPS)) * krow_ref[...]
        kva_ref[:, A_KV_W:] = kv[:, A_KV_W:]

    return ([mlstm_qvo, attn_q_and_gates, mlstm_k]
            + [branch_gates(c0) for c0 in range(0, gab_ref.shape[1], GAB_CHUNK)] + [attn_kv])


def _proj_kernel(*refs):
    for piece in _proj_chunks(*refs):
        piece()


def _proj(x2d, nw, wmt, wqgt, wn, bd, qcol, krow, tm):
    n = x2d.shape[0]
    row = lambda w: pl.BlockSpec((tm, w), lambda i: (i, 0))
    n_blk = tm // LANES
    slab = lambda r: pl.BlockSpec((n_blk, r, LANES), lambda i: (i, 0, 0))
    w_qvo = M_QK_W + 2 * M_V_W
    return pl.pallas_call(
        _proj_kernel,
        grid=(n // tm,),
        in_specs=[row(D_MODEL)] + [_const_spec(a.shape) for a in (nw, wmt, wqgt, wn, bd, qcol, krow)],
        out_specs=[slab(w_qvo), slab(2 * M_HEADS), slab(A_Q_W), row(M_QK_W), row(2 * A_KV_W), row(2 * D_MODEL)],
        out_shape=[jax.ShapeDtypeStruct((n // LANES, w_qvo, LANES), BF16),
                   jax.ShapeDtypeStruct((n // LANES, 2 * M_HEADS, LANES), F32),
                   jax.ShapeDtypeStruct((n // LANES, A_Q_W, LANES), BF16),
                   jax.ShapeDtypeStruct((n, M_QK_W), BF16),
                   jax.ShapeDtypeStruct((n, 2 * A_KV_W), F32),
                   jax.ShapeDtypeStruct((n, 2 * D_MODEL), BF16)],
        compiler_params=_params(("arbitrary",)),
        name="proj",
    )(x2d, nw, wmt, wqgt, wn, bd, qcol, krow)


def _split3_rows(x):
    hi = x.astype(BF16).astype(F32)
    r1 = x - hi
    mid = r1.astype(BF16).astype(F32)
    lo = r1 - mid
    return jnp.concatenate([hi, mid, lo], axis=0).astype(BF16)


def _log_sigmoid(x):
    return jnp.minimum(x, 0.0) - jnp.log1p(jnp.exp(-jnp.abs(x)))


def _chunk_masks(chunk_shift):
    s = lax.broadcasted_iota(jnp.int32, (GROUP, GROUP), 0)
    t = lax.broadcasted_iota(jnp.int32, (GROUP, GROUP), 1)
    same = (s >> chunk_shift) == (t >> chunk_shift)
    return same, same & (s <= t)


def _group_gates(gt, same, causal):
    cm_bf = jnp.where(causal, 1.0, 0.0).astype(BF16)
    lf = _log_sigmoid(gt)
    nr = gt.shape[0]
    bt3 = _dot(_split3_rows(lf), cm_bf)
    bt = (bt3[0:nr] + bt3[nr:2 * nr]) + bt3[2 * nr:3 * nr]
    b4 = bt[M_HEADS:2 * M_HEADS]
    a4 = gt[0:M_HEADS] - b4
    a8 = jnp.concatenate([a4, a4], axis=0)
    a_cols = jnp.concatenate([a8, jnp.zeros((GROUP - SUBLANES, GROUP), F32)], axis=0).T
    at_mats, run_rows, chunk_rows = [], [], []
    for h in range(M_HEADS):
        at = jnp.broadcast_to(a_cols[:, h:h + 1], (GROUP, GROUP))
        at_mats.append(at)
        run_rows.append(jnp.max(jnp.where(causal, at, -jnp.inf), axis=0, keepdims=True))
        chunk_rows.append(jnp.max(jnp.where(same, at, -jnp.inf), axis=0, keepdims=True))
    return b4, a4, at_mats, jnp.concatenate(run_rows, axis=0), jnp.concatenate(chunk_rows, axis=0)


def _group_weights(m_prev, b4, a4, run4, chunk4):
    big_m = jnp.maximum(m_prev, run4)
    m_last = jnp.maximum(m_prev, chunk4)
    w_inter = jnp.exp(m_prev - big_m)
    g_vec = jnp.exp(m_prev - m_last)
    w_last = jnp.exp(a4 - m_last)
    m_t = b4 + big_m
    return big_m, w_inter, g_vec, w_last, m_t, jnp.exp(-m_t)


def _group_scores(qvot, ks):
    lane_head = lax.broadcasted_iota(jnp.int32, (1, M_QK_W), 1) >> DK_SHIFT
    row_head = lax.broadcasted_iota(jnp.int32, (M_QK_W, 1), 0) >> DK_SHIFT
    qt = qvot(0, M_QK_W)
    k_stack = jnp.concatenate([jnp.where(lane_head == h, ks, jnp.zeros_like(ks)) for h in range(M_HEADS)], axis=0)
    qw = jnp.concatenate([jnp.where(row_head == h, qt, jnp.zeros_like(qt)) for h in range(M_HEADS)], axis=1)
    return k_stack, qw, _dot(k_stack, qt)


def _group_values(qvot, sc_t, at_mats, big_m, causal):
    ones_rows = jnp.where(lax.broadcasted_iota(jnp.int32, (BF16_ROWS, GROUP), 0) == 0, 1.0, 0.0).astype(BF16)
    zero_blk = jnp.zeros((GROUP, GROUP), BF16)
    vta, s_rows = [], []
    for h in range(M_HEADS):
        w_t = jnp.where(causal, jnp.exp(at_mats[h] - big_m[h:h + 1]), 0.0)
        s_t = (sc_t[h * GROUP:(h + 1) * GROUP] * w_t).astype(BF16)
        s_rows.append(jnp.concatenate([s_t if j == h else zero_blk for j in range(M_HEADS)], axis=1))
        vta.append(jnp.concatenate([qvot(M_QK_W + h * M_DV, M_QK_W + (h + 1) * M_DV), ones_rows], axis=0))
    return vta, _dot(jnp.concatenate(vta, axis=1), jnp.concatenate(s_rows, axis=0))


def _group_intra(qvot, ks, at_mats, big_m, causal):
    k_stack, qw, sc_t = _group_scores(qvot, ks)
    vta, intra = _group_values(qvot, sc_t, at_mats, big_m, causal)
    return k_stack, qw, vta, intra


def _weighted_values(vta, w_rows):
    return jnp.concatenate([(vta[h].astype(F32) * w_rows[h:h + 1]).astype(BF16) for h in range(M_HEADS)], axis=1)


def _lanes_x(rows):
    return jnp.concatenate([rows[h:h + 1] for h in range(M_HEADS)], axis=1)


def _group_out(qvot, inter, intra, w_inter, e_neg_m, nw_ref):
    outs = []
    out_all = inter * _lanes_x(w_inter) + intra
    for h in range(M_HEADS):
        out_t = out_all[:, h * GROUP:(h + 1) * GROUP]
        hh = out_t[0:M_DV] / jnp.maximum(jnp.abs(out_t[M_DV:M_DV + 1]), e_neg_m[h:h + 1])
        ms = jnp.mean(hh * hh, axis=0, keepdims=True)
        hn = (hh * lax.rsqrt(ms + EPS)) * nw_ref[h * M_DV:(h + 1) * M_DV]
        o_t = qvot(M_QK_W + M_V_W + h * M_DV, M_QK_W + M_V_W + (h + 1) * M_DV)
        outs.append(hn * jax.nn.sigmoid(o_t.astype(F32)))
    return jnp.concatenate(outs, axis=0).T


def _decay_row(g_vec, lane0):
    lane_head = lax.broadcasted_iota(jnp.int32, (1, M_QK_W), 1) >> DK_SHIFT
    g_row = jnp.zeros((1, M_QK_W), F32)
    for h in range(M_HEADS):
        g_row = jnp.where(lane_head == h, g_vec[h:h + 1, lane0:lane0 + 1], g_row)
    return g_row


def _mlstm_sample_kernel(seq_len, qvot_ref, gt_ref, k_ref, mrow_ref, bias_ref, nw_ref, c_ref, n_ref,
                         h_ref, c_out_ref, n_out_ref, mt_ref):
    n_seq = GROUP // seq_len
    shift = seq_len.bit_length() - 1
    same, causal = _chunk_masks(shift)
    lane_seq = lax.broadcasted_iota(jnp.int32, (1, GROUP), 1) >> shift
    lane_seq_x = jnp.concatenate([lane_seq] * M_HEADS, axis=1)
    row0 = lax.broadcasted_iota(jnp.int32, (BF16_ROWS, M_QK_W), 0) == 0
    qvot = lambda r0, r1: qvot_ref[0, r0:r1, :]
    ks = k_ref[...] * (M_DK ** -0.5)
    b4, a4, at_mats, run4, chunk4 = _group_gates(gt_ref[0] + bias_ref[...], same, causal)
    big_m, w_inter, g_vec, w_last, m_t, e_neg_m = _group_weights(mrow_ref[0, 0:M_HEADS, :], b4, a4, run4, chunk4)
    mt_ref[0] = jnp.concatenate([m_t, m_t], axis=0)
    k_stack, qw, vta, intra = _group_intra(qvot, ks, at_mats, big_m, causal)

    st_old = []
    for s in range(n_seq):
        n_rows = jnp.where(row0, jnp.broadcast_to(n_ref[s:s + 1, :], (BF16_ROWS, M_QK_W)), 0.0)
        st_old.append(jnp.concatenate([c_ref[s].T, n_rows], axis=0))
    inter_all = _dot(jnp.concatenate(st_old, axis=0).astype(BF16), qw)
    inter = inter_all[0:S_ROWS]
    for s in range(1, n_seq):
        inter = jnp.where(lane_seq_x == s, inter_all[s * S_ROWS:(s + 1) * S_ROWS], inter)
    tall = jnp.concatenate([_weighted_values(vta, jnp.where(lane_seq == s, w_last, 0.0)) for s in range(n_seq)],
                           axis=0)
    d_st = _dot(tall, k_stack)
    for s in range(n_seq):
        st_new = _decay_row(g_vec, s * seq_len) * st_old[s] + d_st[s * S_ROWS:(s + 1) * S_ROWS]
        c_out_ref[s] = st_new[0:M_DV].T
        n_out_ref[s:s + 1, :] = st_new[M_DV:M_DV + 1]
    h_ref[...] = _group_out(qvot, inter, intra, w_inter, e_neg_m, nw_ref).astype(h_ref.dtype)


def _mlstm_sample(qvot, gt, k_m, mrow, bias_col, nw_col, c, n, seq_len):
    ngrp = qvot.shape[0]
    n_seq = GROUP // seq_len
    full = lambda a: pl.BlockSpec(a.shape, lambda i: (0,) * a.ndim)
    slab = lambda a: pl.BlockSpec((1,) + a.shape[1:], lambda i: (i, 0, 0))
    row = lambda w: pl.BlockSpec((GROUP, w), lambda i: (i, 0))
    c_spec = pl.BlockSpec((n_seq,) + c.shape[1:], lambda i: (i, 0, 0))
    n_spec = pl.BlockSpec((n_seq, n.shape[1]), lambda i: (i, 0))
    return pl.pallas_call(
        functools.partial(_mlstm_sample_kernel, seq_len),
        grid=(ngrp,),
        in_specs=[slab(qvot), slab(gt), row(M_QK_W), slab(mrow), full(bias_col), full(nw_col), c_spec, n_spec],
        out_specs=[row(M_V_W), c_spec, n_spec, slab(mrow)],
        out_shape=[jax.ShapeDtypeStruct((ngrp * GROUP, M_V_W), BF16),
                   jax.ShapeDtypeStruct(c.shape, F32),
                   jax.ShapeDtypeStruct(n.shape, F32),
                   jax.ShapeDtypeStruct(mrow.shape, F32)],
        compiler_params=_params(("arbitrary",)),
        name="mlstm_sample",
    )(qvot, gt, k_m, mrow, bias_col, nw_col, c, n)


def _swa_scores(nqb, has_prev, sink_ref, qt_ref, kv_ref, kvp_ref):
    hd = A_HEAD_DIM
    nq = A_GROUPS * WINDOW
    si = lax.broadcasted_iota(jnp.int32, (2 * WINDOW, nq), 0)
    qi = lax.broadcasted_iota(jnp.int32, (2 * WINDOW, nq), 1) & (WINDOW - 1)
    local = ((si < WINDOW) & (si > qi)) | ((si >= WINDOW) & (si - WINDOW <= qi))
    first = local & (has_prev | (si >= WINDOW))
    lane_grp = lax.broadcasted_iota(jnp.int32, (1, nq), 1) >> WINDOW_SHIFT
    kv_blocks = [kvp_ref[...]] + [kv_ref[i * WINDOW:(i + 1) * WINDOW, :] for i in range(nqb)]
    k_bf = [blk[:, 0:A_KV_W].astype(BF16) for blk in kv_blocks]
    vt_bf = [blk[:, A_KV_W:].T.astype(BF16) for blk in kv_blocks]
    zeros = jnp.zeros((hd, nq), BF16)
    sinks = []
    for kvh in range(A_KV_HEADS):
        sk = jnp.zeros((1, nq), F32)
        for g in range(A_GROUPS):
            sk = jnp.where(lane_grp == g, sink_ref[kvh * A_GROUPS + g], sk)
        sinks.append(sk)
    units = []
    for qb in range(nqb):
        kk = jnp.concatenate([k_bf[qb], k_bf[qb + 1]], axis=0)
        vt = jnp.concatenate([vt_bf[qb], vt_bf[qb + 1]], axis=1)
        mask = local if qb > 0 else first
        for kvh in range(A_KV_HEADS):
            q4t = jnp.concatenate(
                [qt_ref[qb, (kvh * A_GROUPS + g) * hd:(kvh * A_GROUPS + g + 1) * hd, :] for g in range(A_GROUPS)],
                axis=1)
            wq = jnp.concatenate([q4t, zeros] if kvh == 0 else [zeros, q4t], axis=0)
            units.append((jnp.where(mask, _dot(kk, wq), -jnp.inf), sinks[kvh], vt))
    return units


def _swa_finish(nqb, units, h_ref):
    hd = A_HEAD_DIM
    for qb in range(nqb):
        pieces = []
        for kvh in range(A_KV_HEADS):
            s, sk, vt = units[qb * A_KV_HEADS + kvh]
            mx = jnp.maximum(jnp.max(s, axis=0, keepdims=True), sk)
            p = jnp.exp(s - mx)
            den = jnp.sum(p, axis=0, keepdims=True) + jnp.exp(sk - mx)
            ot = _dot(vt, p.astype(BF16))[kvh * hd:(kvh + 1) * hd] / den
            pieces += [ot[:, g * WINDOW:(g + 1) * WINDOW] for g in range(A_GROUPS)]
        h_t = jnp.concatenate(pieces, axis=0)
        h_ref[qb * WINDOW:(qb + 1) * WINDOW, :] = h_t.T.astype(h_ref.dtype)


def _front_kernel(tiles_per_seq, sink_ref, x_ref, nw_ref, wmt_ref, wqgt_ref, wn_ref, bd_ref, qcol_ref, krow_ref,
                  bias_ref, nwm_ref,
                  gab_ref, hm_ref, ha_ref, kwin_ref, vwin_ref, st_ref, m_ref,
                  q_qvot, q_gt, q_qat, q_km, q_kva, p_qvot, p_gt, p_qat, p_km, p_kva, st_s, m_s, kvp_s):
    k = pl.program_id(0)

    @pl.when(k == 0)
    def _():
        for r in (p_qvot, p_gt, p_qat, p_km, p_kva, st_s, m_s, kvp_s):
            r[...] = jnp.zeros(r.shape, r.dtype)

    pieces = _proj_chunks(x_ref, nw_ref, wmt_ref, wqgt_ref, wn_ref, bd_ref, qcol_ref, krow_ref,
                          q_qvot, q_gt, q_qat, q_km, q_kva, gab_ref)
    n_piece = len(pieces)

    seq_start = lax.rem(k - 1 + tiles_per_seq, tiles_per_seq) == 0
    n_blk = p_qat.shape[0]
    same, causal = _chunk_masks(GROUP.bit_length() - 1)
    st = jnp.where(seq_start, 0.0, st_s[...])
    m_col = jnp.where(seq_start, 0.0, m_s[0:M_HEADS, 0:1])

    grp = []
    for g in range(n_blk):
        qvot = lambda r0, r1, g=g: p_qvot[g, r0:r1, :]
        ks = p_km[g * GROUP:(g + 1) * GROUP, :] * (M_DK ** -0.5)
        b4, a4, at_mats, run4, chunk4 = _group_gates(p_gt[g] + bias_ref[...], same, causal)
        weights = _group_weights(jnp.broadcast_to(m_col, (M_HEADS, GROUP)), b4, a4, run4, chunk4)
        m_col = weights[4][:, GROUP - 1:GROUP]
        grp.append((qvot, at_mats, weights) + _group_scores(qvot, ks))
    units = _swa_scores(n_blk, jnp.logical_not(seq_start), sink_ref, p_qat, p_kva, kvp_s)
    inter = [_dot(st.astype(BF16), grp[0][4])]
    for piece in pieces[0:n_piece // 3]:
        piece()

    vals = []
    for qvot, at_mats, (big_m, _, _, w_last, _, _), k_stack, _, sc_t in grp:
        vta, intra = _group_values(qvot, sc_t, at_mats, big_m, causal)
        vals.append((intra, _dot(_weighted_values(vta, w_last), k_stack)))
    _swa_finish(n_blk, units, ha_ref)
    for piece in pieces[n_piece // 3:2 * n_piece // 3]:
        piece()

    for g in range(n_blk):
        st = _decay_row(grp[g][2][2], 0) * st + vals[g][1]
        if g + 1 < n_blk:
            inter.append(_dot(st.astype(BF16), grp[g + 1][4]))
    for piece in pieces[2 * n_piece // 3:]:
        piece()

    for g in range(n_blk):
        qvot, _, (_, w_inter, _, _, _, e_neg_m) = grp[g][0:3]
        hm_ref[g * GROUP:(g + 1) * GROUP, :] = _group_out(qvot, inter[g], vals[g][0], w_inter, e_neg_m,
                                                          nwm_ref).astype(hm_ref.dtype)
    st_s[...] = st
    st_ref[0] = st
    m_rows = jnp.broadcast_to(m_col, (M_HEADS, LANES))
    m_rows = jnp.concatenate([m_rows, m_rows], axis=0)
    m_s[...] = m_rows
    m_ref[0] = m_rows
    tm = p_kva.shape[0]
    kwin_ref[0] = p_kva[tm - WINDOW:, 0:A_KV_W]
    vwin_ref[0] = p_kva[tm - WINDOW:, A_KV_W:]
    kvp_s[...] = p_kva[tm - WINDOW:, :]

    for p, q in ((p_qvot, q_qvot), (p_gt, q_gt), (p_qat, q_qat), (p_km, q_km), (p_kva, q_kva)):
        p[...] = q[...]


def _front(x2d, nb, sinks, nw, wmt, wqgt, wn, bd, qcol, krow, bias_col, nwm_col, tm):
    n = x2d.shape[0]
    n_tiles = n // tm
    tps = n_tiles // nb
    n_blk = tm // LANES
    w_qvo = M_QK_W + 2 * M_V_W
    cur = lambda w: pl.BlockSpec((tm, w), lambda k: (jnp.minimum(k, n_tiles - 1), 0))
    prev = lambda w: pl.BlockSpec((tm, w), lambda k: (jnp.maximum(k - 1, 0), 0))
    per_seq = lambda r, w: pl.BlockSpec((1, r, w), lambda k: (jnp.maximum(k - 1, 0) // tps, 0, 0))
    proj_scratch = [pltpu.VMEM((n_blk, w_qvo, LANES), BF16), pltpu.VMEM((n_blk, 2 * M_HEADS, LANES), F32),
                    pltpu.VMEM((n_blk, A_Q_W, LANES), BF16), pltpu.VMEM((tm, M_QK_W), BF16),
                    pltpu.VMEM((tm, 2 * A_KV_W), F32)]
    return pl.pallas_call(
        functools.partial(_front_kernel, tps),
        grid=(n_tiles + 1,),
        in_specs=[pl.BlockSpec(memory_space=pltpu.SMEM), cur(D_MODEL)]
                 + [_const_spec(a.shape) for a in (nw, wmt, wqgt, wn, bd, qcol, krow, bias_col, nwm_col)],
        out_specs=[cur(2 * D_MODEL), prev(M_V_W), prev(A_Q_W), per_seq(WINDOW, A_KV_W), per_seq(WINDOW, A_KV_W),
                   per_seq(S_ROWS, M_QK_W), per_seq(SUBLANES, LANES)],
        out_shape=[jax.ShapeDtypeStruct((n, 2 * D_MODEL), BF16),
                   jax.ShapeDtypeStruct((n, M_V_W), BF16),
                   jax.ShapeDtypeStruct((n, A_Q_W), BF16),
                   jax.ShapeDtypeStruct((nb, WINDOW, A_KV_W), F32),
                   jax.ShapeDtypeStruct((nb, WINDOW, A_KV_W), F32),
                   jax.ShapeDtypeStruct((nb, S_ROWS, M_QK_W), F32),
                   jax.ShapeDtypeStruct((nb, SUBLANES, LANES), F32)],
        scratch_shapes=proj_scratch + proj_scratch + [pltpu.VMEM((S_ROWS, M_QK_W), F32),
                                                       pltpu.VMEM((SUBLANES, LANES), F32),
                                                       pltpu.VMEM((WINDOW, 2 * A_KV_W), F32)],
        compiler_params=_params(("arbitrary",)),
        name="front",
    )(sinks, x2d, nw, wmt, wqgt, wn, bd, qcol, krow, bias_col, nwm_col)


def _swa_sample_kernel(seq_len, sink_ref, qt_ref, kv_ref, ck_ref, cv_ref, h_ref, kwin_ref, vwin_ref):
    n_seq = GROUP // seq_len
    wb = ck_ref.shape[1]
    n_keys = wb + BF16_ROWS
    lane = lax.broadcasted_iota(jnp.int32, (1, LANES), 1)
    lo_half = lane < HALF
    q_rows = qt_ref[0].astype(F32).T
    k_new = kv_ref[:, 0:A_KV_W].reshape(n_seq, seq_len, A_KV_W)
    v_new = kv_ref[:, A_KV_W:].reshape(n_seq, seq_len, A_KV_W)
    kwin_ref[:, 0:wb - seq_len, :] = ck_ref[:, seq_len:wb, :]
    vwin_ref[:, 0:wb - seq_len, :] = cv_ref[:, seq_len:wb, :]
    kwin_ref[:, wb - seq_len:wb, :] = k_new
    vwin_ref[:, wb - seq_len:wb, :] = v_new

    def to_kv_half(x, head):
        kvh = head // A_GROUPS
        if head % 2 != kvh:
            x = pltpu.roll(x, HALF, axis=1)
        return jnp.where(lo_half if kvh == 0 else ~lo_half, x, 0.0)

    lhs = jnp.concatenate(
        [to_kv_half(q_rows[:, (h // 2) * LANES:(h // 2 + 1) * LANES], h).reshape(n_seq, seq_len, LANES)
         for h in range(A_HEADS)], axis=1).astype(BF16)
    zpad = jnp.zeros((n_seq, BF16_ROWS - seq_len, A_KV_W), F32)
    k_all = jnp.concatenate([ck_ref[...], k_new, zpad], axis=1).astype(BF16)
    v_all = jnp.concatenate([cv_ref[...], v_new, zpad], axis=1).astype(BF16)
    s = jnp.einsum('sqf,skf->sqk', lhs, k_all, preferred_element_type=F32)
    nrow = A_HEADS * seq_len
    ti = lax.broadcasted_iota(jnp.int32, (nrow, n_keys), 0) & (seq_len - 1)
    ki = lax.broadcasted_iota(jnp.int32, (nrow, n_keys), 1)
    mask = ((ki < wb) & (ti + wb - ki < WINDOW)) | ((ki >= wb) & (ki - wb <= ti))
    row_head = lax.broadcasted_iota(jnp.int32, (nrow, 1), 0) >> (seq_len.bit_length() - 1)
    sk = jnp.zeros((nrow, 1), F32)
    for h in range(A_HEADS):
        sk = jnp.where(row_head == h, sink_ref[h], sk)
    s = jnp.where(mask, s, -jnp.inf)
    mx = jnp.maximum(jnp.max(s, axis=-1, keepdims=True), sk)
    p = jnp.exp(s - mx)
    den = jnp.sum(p, axis=-1, keepdims=True) + jnp.exp(sk - mx)
    o = jnp.einsum('sqk,skf->sqf', p.astype(BF16), v_all, preferred_element_type=F32) / den

    def from_kv_half(head):
        x = o[:, head * seq_len:(head + 1) * seq_len, :].reshape(GROUP, LANES)
        return pltpu.roll(x, HALF, axis=1) if head % 2 != head // A_GROUPS else x

    for c in range(A_HEADS // 2):
        h_ref[:, c * LANES:(c + 1) * LANES] = jnp.where(lo_half, from_kv_half(2 * c),
                                                        from_kv_half(2 * c + 1)).astype(h_ref.dtype)


def _swa_sample(qat, kv_a, cache_k, cache_v, sinks, seq_len):
    ngrp = qat.shape[0]
    n_seq = GROUP // seq_len
    wb = cache_k.shape[1]
    row = lambda w: pl.BlockSpec((GROUP, w), lambda i: (i, 0))
    cache = pl.BlockSpec((n_seq, wb, A_KV_W), lambda i: (i, 0, 0))
    return pl.pallas_call(
        functools.partial(_swa_sample_kernel, seq_len),
        grid=(ngrp,),
        in_specs=[pl.BlockSpec(memory_space=pltpu.SMEM), pl.BlockSpec((1, A_Q_W, LANES), lambda i: (i, 0, 0)),
                  row(2 * A_KV_W), cache, cache],
        out_specs=[row(A_Q_W), cache, cache],
        out_shape=[jax.ShapeDtypeStruct((ngrp * GROUP, A_Q_W), BF16),
                   jax.ShapeDtypeStruct(cache_k.shape, F32),
                   jax.ShapeDtypeStruct(cache_v.shape, F32)],
        compiler_params=_params(("arbitrary",)),
        name="swa_sample",
    )(sinks, qat, kv_a, cache_k, cache_v)


def _merge_ffn_kernel(x_ref, hm_ref, ha_ref, gab_ref, wa_ref, wb_ref, wo_ref, nw_ref, wg_ref, wu_ref, wd_ref,
                      y_ref):
    ga = jax.nn.sigmoid(gab_ref[:, 0:D_MODEL].astype(F32))
    gb = jax.nn.sigmoid(gab_ref[:, D_MODEL:].astype(F32))
    mix = ga * _dot(hm_ref[...], wa_ref[...]) + gb * _dot(ha_ref[...], wb_ref[...])
    x1 = x_ref[...] + _dot(mix.astype(BF16), wo_ref[...])
    hf = _rms_rows(x1, nw_ref[...]).astype(BF16)
    gate = _dot(hf, wg_ref[...])
    up = _dot(hf, wu_ref[...])
    act = (jax.nn.silu(gate) * up).astype(BF16)
    y_ref[...] = x1 + _dot(act, wd_ref[...])


def _merge_ffn(x2d, h_m, h_a, g_ab, wa, wb, wo, nw, wg, wu, wd, tm):
    n = x2d.shape[0]
    row = lambda w: pl.BlockSpec((tm, w), lambda i: (i, 0))
    return pl.pallas_call(
        _merge_ffn_kernel,
        grid=(n // tm,),
        in_specs=[row(D_MODEL), row(M_V_W), row(A_Q_W), row(2 * D_MODEL)]
                 + [_const_spec(w.shape) for w in (wa, wb, wo, nw, wg, wu, wd)],
        out_specs=row(D_MODEL),
        out_shape=jax.ShapeDtypeStruct((n, D_MODEL), F32),
        compiler_params=_params(("arbitrary",)),
        name="merge_ffn",
    )(x2d, h_m, h_a, g_ab, wa, wb, wo, nw, wg, wu, wd)


def kernel(x_prompt, x_sample, state_mlstm_C, state_mlstm_n, state_mlstm_m, cache_swa_k, cache_swa_v,
           norm_mix_w, w_in, mlstm_i_bias, mlstm_f_bias, mlstm_norm_w, q_norm_w, k_norm_w, attn_sinks,
           w_branch_a, w_branch_b, w_out, norm_ffn_w, w_gate, w_up, w_down):
    depth = w_in.shape[0]
    assert depth == 1, "single trunk layer"
    l = 0
    bp, tp = x_prompt.shape[0], x_prompt.shape[1]
    bs, ts = x_sample.shape[0], x_sample.shape[1]
    assert tp % 512 == 0 and (bs * ts) % GROUP == 0 and GROUP % ts == 0 and ts & (ts - 1) == 0
    assert ts <= SUBLANES, "sample chunk must fit one sublane tile"

    w = w_in[l]
    c_km, c_vm = M_QK_W, 2 * M_QK_W
    c_g = 2 * M_QK_W + 2 * M_V_W
    c_qa = c_g + 2 * M_HEADS
    c_ka = c_qa + A_Q_W
    wmt = jnp.concatenate([w[:, 0:c_km], w[:, c_vm:c_g]], axis=1).T.astype(BF16)
    wqgt = jnp.concatenate([w[:, c_qa:c_ka], jnp.pad(w[:, c_g:c_qa], ((0, 0), (0, BF16_ROWS - 2 * M_HEADS)))],
                           axis=1).T.astype(BF16)
    wn = jnp.concatenate([w[:, c_km:c_vm], w[:, c_ka:]], axis=1).astype(BF16)
    head_of = jnp.arange(A_KV_W) // A_HEAD_DIM
    bd = (head_of[:, None] == head_of[None, :]).astype(BF16)
    qcol = (jnp.tile(q_norm_w[l], A_HEADS) * (A_HEAD_DIM ** -0.5)).reshape(A_Q_W, 1)
    krow = jnp.tile(k_norm_w[l], A_KV_HEADS).reshape(1, A_KV_W)
    nw_mix = norm_mix_w[l].reshape(1, D_MODEL)
    nw_ffn = norm_ffn_w[l].reshape(1, D_MODEL)
    bias_col = jnp.concatenate([mlstm_i_bias[l], mlstm_f_bias[l]]).reshape(2 * M_HEADS, 1)
    nw_col = mlstm_norm_w[l].reshape(M_V_W, 1)
    sinks = attn_sinks[l]
    wa, wb, wo = w_branch_a[l].astype(BF16), w_branch_b[l].astype(BF16), w_out[l].astype(BF16)
    wg, wu, wd = w_gate[l].astype(BF16), w_up[l].astype(BF16), w_down[l].astype(BF16)
    proj = lambda x2d, tm: _proj(x2d, nw_mix, wmt, wqgt, wn, bd, qcol, krow, tm)
    merge = lambda x2d, h_m, h_a, g_ab, tm: _merge_ffn(x2d, h_m, h_a, g_ab, wa, wb, wo, nw_ffn, wg, wu, wd, tm)

    xp = x_prompt.reshape(bp * tp, D_MODEL)
    g_ab, h_m, h_a, kwin_p, vwin_p, st_p, m_p = _front(xp, bp, sinks, nw_mix, wmt, wqgt, wn, bd, qcol, krow,
                                                       bias_col, nw_col, 512)
    yp = merge(xp, h_m, h_a, g_ab, 512).reshape(bp, tp, D_MODEL)
    c_p = jnp.swapaxes(st_p[:, :M_DV, :], 1, 2).reshape(bp, M_HEADS, M_DK, M_DV)
    n_p = st_p[:, M_DV, :].reshape(bp, M_HEADS, M_DK)
    m_pr = m_p[:, :M_HEADS, 0]

    ns = bs * ts
    xs = x_sample.reshape(ns, D_MODEL)
    tms = 512 if ns % 512 == 0 else GROUP
    qvot, gt, qat, k_m, kv_a, g_ab = proj(xs, tms)
    ngrp = ns // GROUP
    m_lanes = jnp.repeat(state_mlstm_m[l], ts, axis=0).reshape(ngrp, GROUP, M_HEADS)
    mrow = jnp.pad(jnp.swapaxes(m_lanes, 1, 2), ((0, 0), (0, SUBLANES - M_HEADS), (0, 0)))
    h_m, c_s, n_s, mt_s = _mlstm_sample(qvot, gt, k_m, mrow, bias_col, nw_col,
                                        state_mlstm_C[l].reshape(bs, M_QK_W, M_DV),
                                        state_mlstm_n[l].reshape(bs, M_QK_W), ts)
    wbuf = cache_swa_k.shape[2]
    h_a, kwin_s, vwin_s = _swa_sample(qat, kv_a, cache_swa_k[l].reshape(bs, wbuf, A_KV_W),
                                      cache_swa_v[l].reshape(bs, wbuf, A_KV_W), sinks, ts)
    ys = merge(xs, h_m, h_a, g_ab, tms).reshape(bs, ts, D_MODEL)
    m_s = jnp.swapaxes(mt_s[:, :M_HEADS, :], 1, 2).reshape(bs, ts, M_HEADS)[:, ts - 1, :]

    kv5 = lambda a: a.reshape(a.shape[0], a.shape[1], A_KV_HEADS, A_HEAD_DIM)[None]
    return (yp, ys,
            c_p[None], n_p[None], m_pr[None], kv5(kwin_p), kv5(vwin_p),
            c_s.reshape(bs, M_HEADS, M_DK, M_DV)[None], n_s.reshape(bs, M_HEADS, M_DK)[None], m_s[None],
            kv5(kwin_s), kv5(vwin_s))
```

```python
import functools

import jax
import jax.numpy as jnp
from jax import lax
from jax.experimental import pallas as pl
from jax.experimental.pallas import tpu as pltpu

F32 = jnp.float32
BF16 = jnp.bfloat16

D_MODEL = 1024
M_HEADS = 4
M_DK = 64
M_DV = 128
M_CHUNK = 64
M_QK_W = M_HEADS * M_DK
M_V_W = M_HEADS * M_DV
A_HEADS = 8
A_KV_HEADS = 2
A_HEAD_DIM = 64
A_GROUPS = A_HEADS // A_KV_HEADS
A_Q_W = A_HEADS * A_HEAD_DIM
A_KV_W = A_KV_HEADS * A_HEAD_DIM
WINDOW = 128
D_FF = 2816
EPS = 1e-6

LANES = 128
SUBLANES = 8
BF16_ROWS = 16
GROUP = 128
S_ROWS = M_DV + BF16_ROWS
VMEM_LIMIT = 56 * 1024 * 1024
DK_SHIFT = M_DK.bit_length() - 1
WINDOW_SHIFT = WINDOW.bit_length() - 1
HALF = LANES // 2
assert A_HEAD_DIM == HALF and A_KV_W == LANES, "attention head pairs share one lane-width"

NT_DIMS = (((1,), (1,)), ((), ()))


def _dot(a, b):
    return jnp.dot(a, b, preferred_element_type=F32)


def _dot_nt(a, b):
    return lax.dot_general(a, b, NT_DIMS, preferred_element_type=F32)


def _const_spec(shape):
    nd = len(shape)
    return pl.BlockSpec(shape, lambda *_: (0,) * nd, pipeline_mode=pl.Buffered(1))


def _params(sem):
    return pltpu.CompilerParams(dimension_semantics=sem, vmem_limit_bytes=VMEM_LIMIT)


def _rms_rows(x, nw):
    ms = jnp.mean(x * x, axis=-1, keepdims=True)
    return (x * lax.rsqrt(ms + EPS)) * nw


GAB_CHUNK = 512


def _proj_chunks(x_ref, nw_ref, wmt_ref, wqgt_ref, wn_ref, bd_ref, qcol_ref, krow_ref,
                 qvot_ref, gt_ref, qat_ref, km_ref, kva_ref, gab_ref):
    hn = _rms_rows(x_ref[...], nw_ref[...]).astype(BF16)
    n_blk = qat_ref.shape[0]

    def put(ref, rows, val):
        for c in range(n_blk):
            ref[c, rows, :] = val[:, c * LANES:(c + 1) * LANES].astype(ref.dtype)

    def mlstm_qvo():
        put(qvot_ref, slice(None), _dot_nt(wmt_ref[...], hn))

    def attn_q_and_gates():
        qt = _dot_nt(wqgt_ref[...], hn)
        put(gt_ref, slice(None), qt[A_Q_W:A_Q_W + 2 * M_HEADS])
        for h in range(A_HEADS):
            hs = slice(h * A_HEAD_DIM, (h + 1) * A_HEAD_DIM)
            blk = qt[hs]
            ssq_q = jnp.sum(blk * blk, axis=0, keepdims=True)
            put(qat_ref, hs, (blk * lax.rsqrt(ssq_q * (1.0 / A_HEAD_DIM) + EPS)) * qcol_ref[hs])

    def mlstm_k():
        km_ref[...] = _dot_nt(hn, wn_ref[0:M_QK_W, :]).astype(km_ref.dtype)

    def branch_gates(c0):
        def run():
            w0 = M_QK_W + 2 * A_KV_W + c0
            gab_ref[:, c0:c0 + GAB_CHUNK] = _dot_nt(hn, wn_ref[w0:w0 + GAB_CHUNK, :]).astype(gab_ref.dtype)
        return run

    def attn_kv():
        kv = _dot_nt(hn, wn_ref[M_QK_W:M_QK_W + 2 * A_KV_W, :])
        k = kv[:, 0:A_KV_W]
        ksq = k * k
        hi = ksq.astype(BF16)
        lo = (ksq - hi.astype(F32)).astype(BF16)
        ssq = _dot(hi, bd_ref[...]) + _dot(lo, bd_ref[...])
        kva_ref[:, 0:A_KV_W] = (k * lax.rsqrt(ssq * (1.0 / A_HEAD_DIM) + EPS)) * krow_ref[...]
        kva_ref[:, A_KV_W:] = kv[:, A_KV_W:]

    return ([mlstm_qvo, attn_q_and_gates, mlstm_k]
            + [branch_gates(c0) for c0 in range(0, gab_ref.shape[1], GAB_CHUNK)] + [attn_kv])


def _proj_kernel(*refs):
    for piece in _proj_chunks(*refs):
        piece()


def _proj(x2d, nw, wmt, wqgt, wn, bd, qcol, krow, tm):
    n = x2d.shape[0]
    row = lambda w: pl.BlockSpec((tm, w), lambda i: (i, 0))
    n_blk = tm // LANES
    slab = lambda r: pl.BlockSpec((n_blk, r, LANES), lambda i: (i, 0, 0))
    w_qvo = M_QK_W + 2 * M_V_W
    return pl.pallas_call(
        _proj_kernel,
        grid=(n // tm,),
        in_specs=[row(D_MODEL)] + [_const_spec(a.shape) for a in (nw, wmt, wqgt, wn, bd, qcol, krow)],
        out_specs=[slab(w_qvo), slab(2 * M_HEADS), slab(A_Q_W), row(M_QK_W), row(2 * A_KV_W), row(2 * D_MODEL)],
        out_shape=[jax.ShapeDtypeStruct((n // LANES, w_qvo, LANES), BF16),
                   jax.ShapeDtypeStruct((n // LANES, 2 * M_HEADS, LANES), F32),
                   jax.ShapeDtypeStruct((n // LANES, A_Q_W, LANES), BF16),
                   jax.ShapeDtypeStruct((n, M_QK_W), BF16),
                   jax.ShapeDtypeStruct((n, 2 * A_KV_W), F32),
                   jax.ShapeDtypeStruct((n, 2 * D_MODEL), BF16)],
        compiler_params=_params(("arbitrary",)),
        name="proj",
    )(x2d, nw, wmt, wqgt, wn, bd, qcol, krow)


def _split3_rows(x):
    hi = x.astype(BF16).astype(F32)
    r1 = x - hi
    mid = r1.astype(BF16).astype(F32)
    lo = r1 - mid
    return jnp.concatenate([hi, mid, lo], axis=0).astype(BF16)


def _log_sigmoid(x):
    return jnp.minimum(x, 0.0) - jnp.log1p(jnp.exp(-jnp.abs(x)))


def _chunk_masks(chunk_shift):
    s = lax.broadcasted_iota(jnp.int32, (GROUP, GROUP), 0)
    t = lax.broadcasted_iota(jnp.int32, (GROUP, GROUP), 1)
    same = (s >> chunk_shift) == (t >> chunk_shift)
    return same, same & (s <= t)


def _group_gates(gt, same, causal):
    cm_bf = jnp.where(causal, 1.0, 0.0).astype(BF16)
    lf = _log_sigmoid(gt)
    nr = gt.shape[0]
    bt3 = _dot(_split3_rows(lf), cm_bf)
    bt = (bt3[0:nr] + bt3[nr:2 * nr]) + bt3[2 * nr:3 * nr]
    b4 = bt[M_HEADS:2 * M_HEADS]
    a4 = gt[0:M_HEADS] - b4
    a8 = jnp.concatenate([a4, a4], axis=0)
    a_cols = jnp.concatenate([a8, jnp.zeros((GROUP - SUBLANES, GROUP), F32)], axis=0).T
    at_mats, run_rows, chunk_rows = [], [], []
    for h in range(M_HEADS):
        at = jnp.broadcast_to(a_cols[:, h:h + 1], (GROUP, GROUP))
        at_mats.append(at)
        run_rows.append(jnp.max(jnp.where(causal, at, -jnp.inf), axis=0, keepdims=True))
        chunk_rows.append(jnp.max(jnp.where(same, at, -jnp.inf), axis=0, keepdims=True))
    return b4, a4, at_mats, jnp.concatenate(run_rows, axis=0), jnp.concatenate(chunk_rows, axis=0)


def _group_weights(m_prev, b4, a4, run4, chunk4):
    big_m = jnp.maximum(m_prev, run4)
    m_last = jnp.maximum(m_prev, chunk4)
    w_inter = jnp.exp(m_prev - big_m)
    g_vec = jnp.exp(m_prev - m_last)
    w_last = jnp.exp(a4 - m_last)
    m_t = b4 + big_m
    return big_m, w_inter, g_vec, w_last, m_t, jnp.exp(-m_t)


def _group_scores(qvot, ks):
    lane_head = lax.broadcasted_iota(jnp.int32, (1, M_QK_W), 1) >> DK_SHIFT
    row_head = lax.broadcasted_iota(jnp.int32, (M_QK_W, 1), 0) >> DK_SHIFT
    qt = qvot(0, M_QK_W)
    k_stack = jnp.concatenate([jnp.where(lane_head == h, ks, jnp.zeros_like(ks)) for h in range(M_HEADS)], axis=0)
    qw = jnp.concatenate([jnp.where(row_head == h, qt, jnp.zeros_like(qt)) for h in range(M_HEADS)], axis=1)
    return k_stack, qw, _dot(k_stack, qt)


def _group_values(qvot, sc_t, at_mats, big_m, causal):
    ones_rows = jnp.where(lax.broadcasted_iota(jnp.int32, (BF16_ROWS, GROUP), 0) == 0, 1.0, 0.0).astype(BF16)
    zero_blk = jnp.zeros((GROUP, GROUP), BF16)
    vta, s_rows = [], []
    for h in range(M_HEADS):
        w_t = jnp.where(causal, jnp.exp(at_mats[h] - big_m[h:h + 1]), 0.0)
        s_t = (sc_t[h * GROUP:(h + 1) * GROUP] * w_t).astype(BF16)
        s_rows.append(jnp.concatenate([s_t if j == h else zero_blk for j in range(M_HEADS)], axis=1))
        vta.append(jnp.concatenate([qvot(M_QK_W + h * M_DV, M_QK_W + (h + 1) * M_DV), ones_rows], axis=0))
    return vta, _dot(jnp.concatenate(vta, axis=1), jnp.concatenate(s_rows, axis=0))


def _group_intra(qvot, ks, at_mats, big_m, causal):
    k_stack, qw, sc_t = _group_scores(qvot, ks)
    vta, intra = _group_values(qvot, sc_t, at_mats, big_m, causal)
    return k_stack, qw, vta, intra


def _weighted_values(vta, w_rows):
    return jnp.concatenate([(vta[h].astype(F32) * w_rows[h:h + 1]).astype(BF16) for h in range(M_HEADS)], axis=1)


def _lanes_x(rows):
    return jnp.concatenate([rows[h:h + 1] for h in range(M_HEADS)], axis=1)


def _group_out(qvot, inter, intra, w_inter, e_neg_m, nw_ref):
    outs = []
    out_all = inter * _lanes_x(w_inter) + intra
    for h in range(M_HEADS):
        out_t = out_all[:, h * GROUP:(h + 1) * GROUP]
        hh = out_t[0:M_DV] / jnp.maximum(jnp.abs(out_t[M_DV:M_DV + 1]), e_neg_m[h:h + 1])
        ms = jnp.mean(hh * hh, axis=0, keepdims=True)
        hn = (hh * lax.rsqrt(ms + EPS)) * nw_ref[h * M_DV:(h + 1) * M_DV]
        o_t = qvot(M_QK_W + M_V_W + h * M_DV, M_QK_W + M_V_W + (h + 1) * M_DV)
        outs.append(hn * jax.nn.sigmoid(o_t.astype(F32)))
    return jnp.concatenate(outs, axis=0).T


def _decay_row(g_vec, lane0):
    lane_head = lax.broadcasted_iota(jnp.int32, (1, M_QK_W), 1) >> DK_SHIFT
    g_row = jnp.zeros((1, M_QK_W), F32)
    for h in range(M_HEADS):
        g_row = jnp.where(lane_head == h, g_vec[h:h + 1, lane0:lane0 + 1], g_row)
    return g_row


def _mlstm_sample_kernel(seq_len, qvot_ref, gt_ref, k_ref, mrow_ref, bias_ref, nw_ref, c_ref, n_ref,
                         h_ref, c_out_ref, n_out_ref, mt_ref):
    n_seq = GROUP // seq_len
    shift = seq_len.bit_length() - 1
    same, causal = _chunk_masks(shift)
    lane_seq = lax.broadcasted_iota(jnp.int32, (1, GROUP), 1) >> shift
    lane_seq_x = jnp.concatenate([lane_seq] * M_HEADS, axis=1)
    row0 = lax.broadcasted_iota(jnp.int32, (BF16_ROWS, M_QK_W), 0) == 0
    qvot = lambda r0, r1: qvot_ref[0, r0:r1, :]
    ks = k_ref[...] * (M_DK ** -0.5)
    b4, a4, at_mats, run4, chunk4 = _group_gates(gt_ref[0] + bias_ref[...], same, causal)
    big_m, w_inter, g_vec, w_last, m_t, e_neg_m = _group_weights(mrow_ref[0, 0:M_HEADS, :], b4, a4, run4, chunk4)
    mt_ref[0] = jnp.concatenate([m_t, m_t], axis=0)
    k_stack, qw, vta, intra = _group_intra(qvot, ks, at_mats, big_m, causal)

    st_old = []
    for s in range(n_seq):
        n_rows = jnp.where(row0, jnp.broadcast_to(n_ref[s:s + 1, :], (BF16_ROWS, M_QK_W)), 0.0)
        st_old.append(jnp.concatenate([c_ref[s].T, n_rows], axis=0))
    inter_all = _dot(jnp.concatenate(st_old, axis=0).astype(BF16), qw)
    inter = inter_all[0:S_ROWS]
    for s in range(1, n_seq):
        inter = jnp.where(lane_seq_x == s, inter_all[s * S_ROWS:(s + 1) * S_ROWS], inter)
    tall = jnp.concatenate([_weighted_values(vta, jnp.where(lane_seq == s, w_last, 0.0)) for s in range(n_seq)],
                           axis=0)
    d_st = _dot(tall, k_stack)
    for s in range(n_seq):
        st_new = _decay_row(g_vec, s * seq_len) * st_old[s] + d_st[s * S_ROWS:(s + 1) * S_ROWS]
        c_out_ref[s] = st_new[0:M_DV].T
        n_out_ref[s:s + 1, :] = st_new[M_DV:M_DV + 1]
    h_ref[...] = _group_out(qvot, inter, intra, w_inter, e_neg_m, nw_ref).astype(h_ref.dtype)


def _mlstm_sample(qvot, gt, k_m, mrow, bias_col, nw_col, c, n, seq_len):
    ngrp = qvot.shape[0]
    n_seq = GROUP // seq_len
    full = lambda a: pl.BlockSpec(a.shape, lambda i: (0,) * a.ndim)
    slab = lambda a: pl.BlockSpec((1,) + a.shape[1:], lambda i: (i, 0, 0))
    row = lambda w: pl.BlockSpec((GROUP, w), lambda i: (i, 0))
    c_spec = pl.BlockSpec((n_seq,) + c.shape[1:], lambda i: (i, 0, 0))
    n_spec = pl.BlockSpec((n_seq, n.shape[1]), lambda i: (i, 0))
    return pl.pallas_call(
        functools.partial(_mlstm_sample_kernel, seq_len),
        grid=(ngrp,),
        in_specs=[slab(qvot), slab(gt), row(M_QK_W), slab(mrow), full(bias_col), full(nw_col), c_spec, n_spec],
        out_specs=[row(M_V_W), c_spec, n_spec, slab(mrow)],
        out_shape=[jax.ShapeDtypeStruct((ngrp * GROUP, M_V_W), BF16),
                   jax.ShapeDtypeStruct(c.shape, F32),
                   jax.ShapeDtypeStruct(n.shape, F32),
                   jax.ShapeDtypeStruct(mrow.shape, F32)],
        compiler_params=_params(("arbitrary",)),
        name="mlstm_sample",
    )(qvot, gt, k_m, mrow, bias_col, nw_col, c, n)


def _swa_scores(qbs, has_prev, sink_ref, qt_ref, kv_ref, kvp_ref):
    hd = A_HEAD_DIM
    nq = A_GROUPS * WINDOW
    si = lax.broadcasted_iota(jnp.int32, (2 * WINDOW, nq), 0)
    qi = lax.broadcasted_iota(jnp.int32, (2 * WINDOW, nq), 1) & (WINDOW - 1)
    local = ((si < WINDOW) & (si > qi)) | ((si >= WINDOW) & (si - WINDOW <= qi))
    first = local & (has_prev | (si >= WINDOW))
    lane_grp = lax.broadcasted_iota(jnp.int32, (1, nq), 1) >> WINDOW_SHIFT
    kv_block = lambda i: kvp_ref[...] if i == 0 else kv_ref[(i - 1) * WINDOW:i * WINDOW, :]
    k_bf = lambda i: kv_block(i)[:, 0:A_KV_W].astype(BF16)
    vt_bf = lambda i: kv_block(i)[:, A_KV_W:].T.astype(BF16)
    zeros = jnp.zeros((hd, nq), BF16)
    sinks = []
    for kvh in range(A_KV_HEADS):
        sk = jnp.zeros((1, nq), F32)
        for g in range(A_GROUPS):
            sk = jnp.where(lane_grp == g, sink_ref[kvh * A_GROUPS + g], sk)
        sinks.append(sk)
    units = []
    for qb in qbs:
        kk = jnp.concatenate([k_bf(qb), k_bf(qb + 1)], axis=0)
        vt = jnp.concatenate([vt_bf(qb), vt_bf(qb + 1)], axis=1)
        mask = local if qb > 0 else first
        for kvh in range(A_KV_HEADS):
            q4t = jnp.concatenate(
                [qt_ref[qb, (kvh * A_GROUPS + g) * hd:(kvh * A_GROUPS + g + 1) * hd, :] for g in range(A_GROUPS)],
                axis=1)
            wq = jnp.concatenate([q4t, zeros] if kvh == 0 else [zeros, q4t], axis=0)
            units.append((jnp.where(mask, _dot(kk, wq), -jnp.inf), sinks[kvh], vt))
    return units


def _swa_finish(qbs, units, h_ref):
    hd = A_HEAD_DIM
    for i, qb in enumerate(qbs):
        pieces = []
        for kvh in range(A_KV_HEADS):
            s, sk, vt = units[i * A_KV_HEADS + kvh]
            mx = jnp.maximum(jnp.max(s, axis=0, keepdims=True), sk)
            p = jnp.exp(s - mx)
            den = jnp.sum(p, axis=0, keepdims=True) + jnp.exp(sk - mx)
            ot = _dot(vt, p.astype(BF16))[kvh * hd:(kvh + 1) * hd] / den
            pieces += [ot[:, g * WINDOW:(g + 1) * WINDOW] for g in range(A_GROUPS)]
        h_t = jnp.concatenate(pieces, axis=0)
        h_ref[qb * WINDOW:(qb + 1) * WINDOW, :] = h_t.T.astype(h_ref.dtype)


def _front_kernel(tiles_per_seq, sink_ref, x_ref, nw_ref, wmt_ref, wqgt_ref, wn_ref, bd_ref, qcol_ref, krow_ref,
                  bias_ref, nwm_ref,
                  gab_ref, hm_ref, ha_ref, kwin_ref, vwin_ref, st_ref, m_ref,
                  q_qvot, q_gt, q_qat, q_km, q_kva, p_qvot, p_gt, p_qat, p_km, p_kva, st_s, m_s, kvp_s):
    k = pl.program_id(0)

    @pl.when(k == 0)
    def _():
        for r in (p_qvot, p_gt, p_qat, p_km, p_kva, st_s, m_s, kvp_s):
            r[...] = jnp.zeros(r.shape, r.dtype)

    pieces = _proj_chunks(x_ref, nw_ref, wmt_ref, wqgt_ref, wn_ref, bd_ref, qcol_ref, krow_ref,
                          q_qvot, q_gt, q_qat, q_km, q_kva, gab_ref)
    n_piece = len(pieces)

    seq_start = lax.rem(k - 1 + tiles_per_seq, tiles_per_seq) == 0
    n_blk = p_qat.shape[0]
    same, causal = _chunk_masks(GROUP.bit_length() - 1)
    st = jnp.where(seq_start, 0.0, st_s[...])
    m_col = jnp.where(seq_start, 0.0, m_s[0:M_HEADS, 0:1])

    grp = []
    for g in range(n_blk):
        qvot = lambda r0, r1, g=g: p_qvot[g, r0:r1, :]
        ks = p_km[g * GROUP:(g + 1) * GROUP, :] * (M_DK ** -0.5)
        b4, a4, at_mats, run4, chunk4 = _group_gates(p_gt[g] + bias_ref[...], same, causal)
        weights = _group_weights(jnp.broadcast_to(m_col, (M_HEADS, GROUP)), b4, a4, run4, chunk4)
        m_col = weights[4][:, GROUP - 1:GROUP]
        grp.append((qvot, at_mats, weights) + _group_scores(qvot, ks))
    has_prev = jnp.logical_not(seq_start)
    qb_lo, qb_hi = list(range(0, n_blk // 2)), list(range(n_blk // 2, n_blk))
    units_lo = _swa_scores(qb_lo, has_prev, sink_ref, p_qat, p_kva, kvp_s)
    inter = [_dot(st.astype(BF16), grp[0][4])]
    for piece in pieces[0:1]:
        piece()

    vals = []
    for qvot, at_mats, (big_m, _, _, w_last, _, _), k_stack, _, sc_t in grp:
        vta, intra = _group_values(qvot, sc_t, at_mats, big_m, causal)
        vals.append((intra, _dot(_weighted_values(vta, w_last), k_stack)))
    _swa_finish(qb_lo, units_lo, ha_ref)
    units_hi = _swa_scores(qb_hi, has_prev, sink_ref, p_qat, p_kva, kvp_s)
    for piece in pieces[1:3]:
        piece()

    for g in range(n_blk):
        st = _decay_row(grp[g][2][2], 0) * st + vals[g][1]
        if g + 1 < n_blk:
            inter.append(_dot(st.astype(BF16), grp[g + 1][4]))
    _swa_finish(qb_hi, units_hi, ha_ref)
    for piece in pieces[3:]:
        piece()

    for g in range(n_blk):
        qvot, _, (_, w_inter, _, _, _, e_neg_m) = grp[g][0:3]
        hm_ref[g * GROUP:(g + 1) * GROUP, :] = _group_out(qvot, inter[g], vals[g][0], w_inter, e_neg_m,
                                                          nwm_ref).astype(hm_ref.dtype)
    st_s[...] = st
    st_ref[0] = st
    m_rows = jnp.broadcast_to(m_col, (M_HEADS, LANES))
    m_rows = jnp.concatenate([m_rows, m_rows], axis=0)
    m_s[...] = m_rows
    m_ref[0] = m_rows
    tm = p_kva.shape[0]
    kwin_ref[0] = p_kva[tm - WINDOW:, 0:A_KV_W]
    vwin_ref[0] = p_kva[tm - WINDOW:, A_KV_W:]
    kvp_s[...] = p_kva[tm - WINDOW:, :]

    for p, q in ((p_qvot, q_qvot), (p_gt, q_gt), (p_qat, q_qat), (p_km, q_km), (p_kva, q_kva)):
        p[...] = q[...]


def _front(x2d, nb, sinks, nw, wmt, wqgt, wn, bd, qcol, krow, bias_col, nwm_col, tm):
    n = x2d.shape[0]
    n_tiles = n // tm
    tps = n_tiles // nb
    n_blk = tm // LANES
    w_qvo = M_QK_W + 2 * M_V_W
    cur = lambda w: pl.BlockSpec((tm, w), lambda k: (jnp.minimum(k, n_tiles - 1), 0))
    prev = lambda w: pl.BlockSpec((tm, w), lambda k: (jnp.maximum(k - 1, 0), 0))
    per_seq = lambda r, w: pl.BlockSpec((1, r, w), lambda k: (jnp.maximum(k - 1, 0) // tps, 0, 0))
    proj_scratch = [pltpu.VMEM((n_blk, w_qvo, LANES), BF16), pltpu.VMEM((n_blk, 2 * M_HEADS, LANES), F32),
                    pltpu.VMEM((n_blk, A_Q_W, LANES), BF16), pltpu.VMEM((tm, M_QK_W), BF16),
                    pltpu.VMEM((tm, 2 * A_KV_W), F32)]
    return pl.pallas_call(
        functools.partial(_front_kernel, tps),
        grid=(n_tiles + 1,),
        in_specs=[pl.BlockSpec(memory_space=pltpu.SMEM), cur(D_MODEL)]
                 + [_const_spec(a.shape) for a in (nw, wmt, wqgt, wn, bd, qcol, krow, bias_col, nwm_col)],
        out_specs=[cur(2 * D_MODEL), prev(M_V_W), prev(A_Q_W), per_seq(WINDOW, A_KV_W), per_seq(WINDOW, A_KV_W),
                   per_seq(S_ROWS, M_QK_W), per_seq(SUBLANES, LANES)],
        out_shape=[jax.ShapeDtypeStruct((n, 2 * D_MODEL), BF16),
                   jax.ShapeDtypeStruct((n, M_V_W), BF16),
                   jax.ShapeDtypeStruct((n, A_Q_W), BF16),
                   jax.ShapeDtypeStruct((nb, WINDOW, A_KV_W), F32),
                   jax.ShapeDtypeStruct((nb, WINDOW, A_KV_W), F32),
                   jax.ShapeDtypeStruct((nb, S_ROWS, M_QK_W), F32),
                   jax.ShapeDtypeStruct((nb, SUBLANES, LANES), F32)],
        scratch_shapes=proj_scratch + proj_scratch + [pltpu.VMEM((S_ROWS, M_QK_W), F32),
                                                       pltpu.VMEM((SUBLANES, LANES), F32),
                                                       pltpu.VMEM((WINDOW, 2 * A_KV_W), F32)],
        compiler_params=_params(("arbitrary",)),
        name="front",
    )(sinks, x2d, nw, wmt, wqgt, wn, bd, qcol, krow, bias_col, nwm_col)


def _swa_sample_kernel(seq_len, sink_ref, qt_ref, kv_ref, ck_ref, cv_ref, h_ref, kwin_ref, vwin_ref):
    n_seq = GROUP // seq_len
    wb = ck_ref.shape[1]
    n_keys = wb + BF16_ROWS
    lane = lax.broadcasted_iota(jnp.int32, (1, LANES), 1)
    lo_half = lane < HALF
    q_rows = qt_ref[0].astype(F32).T
    k_new = kv_ref[:, 0:A_KV_W].reshape(n_seq, seq_len, A_KV_W)
    v_new = kv_ref[:, A_KV_W:].reshape(n_seq, seq_len, A_KV_W)
    kwin_ref[:, 0:wb - seq_len, :] = ck_ref[:, seq_len:wb, :]
    vwin_ref[:, 0:wb - seq_len, :] = cv_ref[:, seq_len:wb, :]
    kwin_ref[:, wb - seq_len:wb, :] = k_new
    vwin_ref[:, wb - seq_len:wb, :] = v_new

    def to_kv_half(x, head):
        kvh = head // A_GROUPS
        if head % 2 != kvh:
            x = pltpu.roll(x, HALF, axis=1)
        return jnp.where(lo_half if kvh == 0 else ~lo_half, x, 0.0)

    lhs = jnp.concatenate(
        [to_kv_half(q_rows[:, (h // 2) * LANES:(h // 2 + 1) * LANES], h).reshape(n_seq, seq_len, LANES)
         for h in range(A_HEADS)], axis=1).astype(BF16)
    zpad = jnp.zeros((n_seq, BF16_ROWS - seq_len, A_KV_W), F32)
    k_all = jnp.concatenate([ck_ref[...], k_new, zpad], axis=1).astype(BF16)
    v_all = jnp.concatenate([cv_ref[...], v_new, zpad], axis=1).astype(BF16)
    s = jnp.einsum('sqf,skf->sqk', lhs, k_all, preferred_element_type=F32)
    nrow = A_HEADS * seq_len
    ti = lax.broadcasted_iota(jnp.int32, (nrow, n_keys), 0) & (seq_len - 1)
    ki = lax.broadcasted_iota(jnp.int32, (nrow, n_keys), 1)
    mask = ((ki < wb) & (ti + wb - ki < WINDOW)) | ((ki >= wb) & (ki - wb <= ti))
    row_head = lax.broadcasted_iota(jnp.int32, (nrow, 1), 0) >> (seq_len.bit_length() - 1)
    sk = jnp.zeros((nrow, 1), F32)
    for h in range(A_HEADS):
        sk = jnp.where(row_head == h, sink_ref[h], sk)
    s = jnp.where(mask, s, -jnp.inf)
    mx = jnp.maximum(jnp.max(s, axis=-1, keepdims=True), sk)
    p = jnp.exp(s - mx)
    den = jnp.sum(p, axis=-1, keepdims=True) + jnp.exp(sk - mx)
    o = jnp.einsum('sqk,skf->sqf', p.astype(BF16), v_all, preferred_element_type=F32) / den

    def from_kv_half(head):
        x = o[:, head * seq_len:(head + 1) * seq_len, :].reshape(GROUP, LANES)
        return pltpu.roll(x, HALF, axis=1) if head % 2 != head // A_GROUPS else x

    for c in range(A_HEADS // 2):
        h_ref[:, c * LANES:(c + 1) * LANES] = jnp.where(lo_half, from_kv_half(2 * c),
                                                        from_kv_half(2 * c + 1)).astype(h_ref.dtype)


def _swa_sample(qat, kv_a, cache_k, cache_v, sinks, seq_len):
    ngrp = qat.shape[0]
    n_seq = GROUP // seq_len
    wb = cache_k.shape[1]
    row = lambda w: pl.BlockSpec((GROUP, w), lambda i: (i, 0))
    cache = pl.BlockSpec((n_seq, wb, A_KV_W), lambda i: (i, 0, 0))
    return pl.pallas_call(
        functools.partial(_swa_sample_kernel, seq_len),
        grid=(ngrp,),
        in_specs=[pl.BlockSpec(memory_space=pltpu.SMEM), pl.BlockSpec((1, A_Q_W, LANES), lambda i: (i, 0, 0)),
                  row(2 * A_KV_W), cache, cache],
        out_specs=[row(A_Q_W), cache, cache],
        out_shape=[jax.ShapeDtypeStruct((ngrp * GROUP, A_Q_W), BF16),
                   jax.ShapeDtypeStruct(cache_k.shape, F32),
                   jax.ShapeDtypeStruct(cache_v.shape, F32)],
        compiler_params=_params(("arbitrary",)),
        name="swa_sample",
    )(sinks, qat, kv_a, cache_k, cache_v)


def _merge_ffn_kernel(x_ref, hm_ref, ha_ref, gab_ref, wa_ref, wb_ref, wo_ref, nw_ref, wg_ref, wu_ref, wd_ref,
                      y_ref):
    ga = jax.nn.sigmoid(gab_ref[:, 0:D_MODEL].astype(F32))
    gb = jax.nn.sigmoid(gab_ref[:, D_MODEL:].astype(F32))
    mix = ga * _dot(hm_ref[...], wa_ref[...]) + gb * _dot(ha_ref[...], wb_ref[...])
    x1 = x_ref[...] + _dot(mix.astype(BF16), wo_ref[...])
    hf = _rms_rows(x1, nw_ref[...]).astype(BF16)
    gate = _dot(hf, wg_ref[...])
    up = _dot(hf, wu_ref[...])
    act = (jax.nn.silu(gate) * up).astype(BF16)
    y_ref[...] = x1 + _dot(act, wd_ref[...])


def _merge_ffn(x2d, h_m, h_a, g_ab, wa, wb, wo, nw, wg, wu, wd, tm):
    n = x2d.shape[0]
    row = lambda w: pl.BlockSpec((tm, w), lambda i: (i, 0))
    return pl.pallas_call(
        _merge_ffn_kernel,
        grid=(n // tm,),
        in_specs=[row(D_MODEL), row(M_V_W), row(A_Q_W), row(2 * D_MODEL)]
                 + [_const_spec(w.shape) for w in (wa, wb, wo, nw, wg, wu, wd)],
        out_specs=row(D_MODEL),
        out_shape=jax.ShapeDtypeStruct((n, D_MODEL), F32),
        compiler_params=_params(("arbitrary",)),
        name="merge_ffn",
    )(x2d, h_m, h_a, g_ab, wa, wb, wo, nw, wg, wu, wd)


def kernel(x_prompt, x_sample, state_mlstm_C, state_mlstm_n, state_mlstm_m, cache_swa_k, cache_swa_v,
           norm_mix_w, w_in, mlstm_i_bias, mlstm_f_bias, mlstm_norm_w, q_norm_w, k_norm_w, attn_sinks,
           w_branch_a, w_branch_b, w_out, norm_ffn_w, w_gate, w_up, w_down):
    depth = w_in.shape[0]
    assert depth == 1, "single trunk layer"
    l = 0
    bp, tp = x_prompt.shape[0], x_prompt.shape[1]
    bs, ts = x_sample.shape[0], x_sample.shape[1]
    assert tp % 512 == 0 and (bs * ts) % GROUP == 0 and GROUP % ts == 0 and ts & (ts - 1) == 0
    assert ts <= SUBLANES, "sample chunk must fit one sublane tile"

    wt = jnp.transpose(w_in[l])
    c_km, c_vm = M_QK_W, 2 * M_QK_W
    c_g = 2 * M_QK_W + 2 * M_V_W
    c_qa = c_g + 2 * M_HEADS
    c_ka = c_qa + A_Q_W
    gate_pad = jnp.zeros((BF16_ROWS - 2 * M_HEADS, D_MODEL), F32)
    wmt = jnp.concatenate([wt[0:c_km], wt[c_vm:c_g]], axis=0).astype(BF16)
    wqgt = jnp.concatenate([wt[c_qa:c_ka], wt[c_g:c_qa], gate_pad], axis=0).astype(BF16)
    wn = jnp.concatenate([wt[c_km:c_vm], wt[c_ka:]], axis=0).astype(BF16)
    head_of = jnp.arange(A_KV_W) // A_HEAD_DIM
    bd = (head_of[:, None] == head_of[None, :]).astype(BF16)
    qcol = (jnp.tile(q_norm_w[l], A_HEADS) * (A_HEAD_DIM ** -0.5)).reshape(A_Q_W, 1)
    krow = jnp.tile(k_norm_w[l], A_KV_HEADS).reshape(1, A_KV_W)
    nw_mix = norm_mix_w[l].reshape(1, D_MODEL)
    nw_ffn = norm_ffn_w[l].reshape(1, D_MODEL)
    bias_col = jnp.concatenate([mlstm_i_bias[l], mlstm_f_bias[l]]).reshape(2 * M_HEADS, 1)
    nw_col = mlstm_norm_w[l].reshape(M_V_W, 1)
    sinks = attn_sinks[l]
    wa, wb, wo = w_branch_a[l].astype(BF16), w_branch_b[l].astype(BF16), w_out[l].astype(BF16)
    wg, wu, wd = w_gate[l].astype(BF16), w_up[l].astype(BF16), w_down[l].astype(BF16)
    proj = lambda x2d, tm: _proj(x2d, nw_mix, wmt, wqgt, wn, bd, qcol, krow, tm)
    merge = lambda x2d, h_m, h_a, g_ab, tm: _merge_ffn(x2d, h_m, h_a, g_ab, wa, wb, wo, nw_ffn, wg, wu, wd, tm)

    xp = x_prompt.reshape(bp * tp, D_MODEL)
    g_ab, h_m, h_a, kwin_p, vwin_p, st_p, m_p = _front(xp, bp, sinks, nw_mix, wmt, wqgt, wn, bd, qcol, krow,
                                                       bias_col, nw_col, 512)
    yp = merge(xp, h_m, h_a, g_ab, 512).reshape(bp, tp, D_MODEL)
    c_p = jnp.swapaxes(st_p[:, :M_DV, :], 1, 2).reshape(bp, M_HEADS, M_DK, M_DV)
    n_p = st_p[:, M_DV, :].reshape(bp, M_HEADS, M_DK)
    m_pr = m_p[:, :M_HEADS, 0]

    ns = bs * ts
    xs = x_sample.reshape(ns, D_MODEL)
    tms = 512 if ns % 512 == 0 else GROUP
    qvot, gt, qat, k_m, kv_a, g_ab = proj(xs, tms)
    ngrp = ns // GROUP
    m_lanes = jnp.repeat(state_mlstm_m[l], ts, axis=0).reshape(ngrp, GROUP, M_HEADS)
    mrow = jnp.pad(jnp.swapaxes(m_lanes, 1, 2), ((0, 0), (0, SUBLANES - M_HEADS), (0, 0)))
    h_m, c_s, n_s, mt_s = _mlstm_sample(qvot, gt, k_m, mrow, bias_col, nw_col,
                                        state_mlstm_C[l].reshape(bs, M_QK_W, M_DV),
                                        state_mlstm_n[l].reshape(bs, M_QK_W), ts)
    wbuf = cache_swa_k.shape[2]
    h_a, kwin_s, vwin_s = _swa_sample(qat, kv_a, cache_swa_k[l].reshape(bs, wbuf, A_KV_W),
                                      cache_swa_v[l].reshape(bs, wbuf, A_KV_W), sinks, ts)
    ys = merge(xs, h_m, h_a, g_ab, tms).reshape(bs, ts, D_MODEL)
    m_s = jnp.swapaxes(mt_s[:, :M_HEADS, :], 1, 2).reshape(bs, ts, M_HEADS)[:, ts - 1, :]

    kv5 = lambda a: a.reshape(a.shape[0], a.shape[1], A_KV_HEADS, A_HEAD_DIM)[None]
    return (yp, ys,
            c_p[None], n_p[None], m_pr[None], kv5(kwin_p), kv5(vwin_p),
            c_s.reshape(bs, M_HEADS, M_DK, M_DV)[None], n_s.reshape(bs, M_HEADS, M_DK)[None], m_s[None],
            kv5(kwin_s), kv5(vwin_s))
```

```python
import functools

import jax
import jax.numpy as jnp
from jax import lax
from jax.experimental import pallas as pl
from jax.experimental.pallas import tpu as pltpu

F32 = jnp.float32
BF16 = jnp.bfloat16

D_MODEL = 1024
M_HEADS = 4
M_DK = 64
M_DV = 128
M_CHUNK = 64
M_QK_W = M_HEADS * M_DK
M_V_W = M_HEADS * M_DV
A_HEADS = 8
A_KV_HEADS = 2
A_HEAD_DIM = 64
A_GROUPS = A_HEADS // A_KV_HEADS
A_Q_W = A_HEADS * A_HEAD_DIM
A_KV_W = A_KV_HEADS * A_HEAD_DIM
WINDOW = 128
D_FF = 2816
EPS = 1e-6

LANES = 128
SUBLANES = 8
BF16_ROWS = 16
GROUP = 128
S_ROWS = M_DV + BF16_ROWS
VMEM_LIMIT = 56 * 1024 * 1024
DK_SHIFT = M_DK.bit_length() - 1
WINDOW_SHIFT = WINDOW.bit_length() - 1
HALF = LANES // 2
assert A_HEAD_DIM == HALF and A_KV_W == LANES, "attention head pairs share one lane-width"

NT_DIMS = (((1,), (1,)), ((), ()))


def _dot(a, b):
    return jnp.dot(a, b, preferred_element_type=F32)


def _dot_nt(a, b):
    return lax.dot_general(a, b, NT_DIMS, preferred_element_type=F32)


def _const_spec(shape):
    nd = len(shape)
    return pl.BlockSpec(shape, lambda *_: (0,) * nd, pipeline_mode=pl.Buffered(1))


def _params(sem):
    return pltpu.CompilerParams(dimension_semantics=sem, vmem_limit_bytes=VMEM_LIMIT)


def _rms_rows(x, nw):
    ms = jnp.mean(x * x, axis=-1, keepdims=True)
    return (x * lax.rsqrt(ms + EPS)) * nw


GAB_CHUNK = 512


def _proj_chunks(x_ref, nw_ref, wmt_ref, wqgt_ref, wn_ref, bd_ref, qcol_ref, krow_ref,
                 qvot_ref, gt_ref, qat_ref, km_ref, kva_ref, gab_ref):
    hn = _rms_rows(x_ref[...], nw_ref[...]).astype(BF16)
    n_blk = qat_ref.shape[0]

    def put(ref, rows, val):
        for c in range(n_blk):
            ref[c, rows, :] = val[:, c * LANES:(c + 1) * LANES].astype(ref.dtype)

    def mlstm_qvo():
        put(qvot_ref, slice(None), _dot_nt(wmt_ref[...], hn))

    def attn_q_and_gates():
        qt = _dot_nt(wqgt_ref[...], hn)
        put(gt_ref, slice(None), qt[A_Q_W:A_Q_W + 2 * M_HEADS])
        for h in range(A_HEADS):
            hs = slice(h * A_HEAD_DIM, (h + 1) * A_HEAD_DIM)
            blk = qt[hs]
            ssq_q = jnp.sum(blk * blk, axis=0, keepdims=True)
            put(qat_ref, hs, (blk * lax.rsqrt(ssq_q * (1.0 / A_HEAD_DIM) + EPS)) * qcol_ref[hs])

    def mlstm_k():
        km_ref[...] = _dot_nt(hn, wn_ref[0:M_QK_W, :]).astype(km_ref.dtype)

    def branch_gates(c0):
        def run():
            w0 = M_QK_W + 2 * A_KV_W + c0
            gab_ref[:, c0:c0 + GAB_CHUNK] = _dot_nt(hn, wn_ref[w0:w0 + GAB_CHUNK, :]).astype(gab_ref.dtype)
        return run

    def attn_kv():
        kv = _dot_nt(hn, wn_ref[M_QK_W:M_QK_W + 2 * A_KV_W, :])
        k = kv[:, 0:A_KV_W]
        ksq = k * k
        hi = ksq.astype(BF16)
        lo = (ksq - hi.astype(F32)).astype(BF16)
        ssq = _dot(hi, bd_ref[...]) + _dot(lo, bd_ref[...])
        kva_ref[:, 0:A_KV_W] = (k * lax.rsqrt(ssq * (1.0 / A_HEAD_DIM) + EPS)) * krow_ref[...]
        kva_ref[:, A_KV_W:] = kv[:, A_KV_W:]

    return ([mlstm_qvo, attn_q_and_gates, mlstm_k]
            + [branch_gates(c0) for c0 in range(0, gab_ref.shape[1], GAB_CHUNK)] + [attn_kv])


def _proj_kernel(*refs):
    for piece in _proj_chunks(*refs):
        piece()


def _proj(x2d, nw, wmt, wqgt, wn, bd, qcol, krow, tm):
    n = x2d.shape[0]
    row = lambda w: pl.BlockSpec((tm, w), lambda i: (i, 0))
    n_blk = tm // LANES
    slab = lambda r: pl.BlockSpec((n_blk, r, LANES), lambda i: (i, 0, 0))
    w_qvo = M_QK_W + 2 * M_V_W
    return pl.pallas_call(
        _proj_kernel,
        grid=(n // tm,),
        in_specs=[row(D_MODEL)] + [_const_spec(a.shape) for a in (nw, wmt, wqgt, wn, bd, qcol, krow)],
        out_specs=[slab(w_qvo), slab(2 * M_HEADS), slab(A_Q_W), row(M_QK_W), row(2 * A_KV_W), row(2 * D_MODEL)],
        out_shape=[jax.ShapeDtypeStruct((n // LANES, w_qvo, LANES), BF16),
                   jax.ShapeDtypeStruct((n // LANES, 2 * M_HEADS, LANES), F32),
                   jax.ShapeDtypeStruct((n // LANES, A_Q_W, LANES), BF16),
                   jax.ShapeDtypeStruct((n, M_QK_W), BF16),
                   jax.ShapeDtypeStruct((n, 2 * A_KV_W), F32),
                   jax.ShapeDtypeStruct((n, 2 * D_MODEL), BF16)],
        compiler_params=_params(("arbitrary",)),
        name="proj",
    )(x2d, nw, wmt, wqgt, wn, bd, qcol, krow)


def _split3_rows(x):
    hi = x.astype(BF16).astype(F32)
    r1 = x - hi
    mid = r1.astype(BF16).astype(F32)
    lo = r1 - mid
    return jnp.concatenate([hi, mid, lo], axis=0).astype(BF16)


def _log_sigmoid(x):
    return jnp.minimum(x, 0.0) - jnp.log1p(jnp.exp(-jnp.abs(x)))


def _chunk_masks(chunk_shift):
    s = lax.broadcasted_iota(jnp.int32, (GROUP, GROUP), 0)
    t = lax.broadcasted_iota(jnp.int32, (GROUP, GROUP), 1)
    same = (s >> chunk_shift) == (t >> chunk_shift)
    return same, same & (s <= t)


def _group_gates(gt, same, causal):
    cm_bf = jnp.where(causal, 1.0, 0.0).astype(BF16)
    lf = _log_sigmoid(gt)
    nr = gt.shape[0]
    bt3 = _dot(_split3_rows(lf), cm_bf)
    bt = (bt3[0:nr] + bt3[nr:2 * nr]) + bt3[2 * nr:3 * nr]
    b4 = bt[M_HEADS:2 * M_HEADS]
    a4 = gt[0:M_HEADS] - b4
    a8 = jnp.concatenate([a4, a4], axis=0)
    a_cols = jnp.concatenate([a8, jnp.zeros((GROUP - SUBLANES, GROUP), F32)], axis=0).T
    at_mats, run_rows, chunk_rows = [], [], []
    for h in range(M_HEADS):
        at = jnp.broadcast_to(a_cols[:, h:h + 1], (GROUP, GROUP))
        at_mats.append(at)
        run_rows.append(jnp.max(jnp.where(causal, at, -jnp.inf), axis=0, keepdims=True))
        chunk_rows.append(jnp.max(jnp.where(same, at, -jnp.inf), axis=0, keepdims=True))
    return b4, a4, at_mats, jnp.concatenate(run_rows, axis=0), jnp.concatenate(chunk_rows, axis=0)


def _group_weights(m_prev, b4, a4, run4, chunk4):
    big_m = jnp.maximum(m_prev, run4)
    m_last = jnp.maximum(m_prev, chunk4)
    w_inter = jnp.exp(m_prev - big_m)
    g_vec = jnp.exp(m_prev - m_last)
    w_last = jnp.exp(a4 - m_last)
    m_t = b4 + big_m
    return big_m, w_inter, g_vec, w_last, m_t, jnp.exp(-m_t)


def _group_scores(qvot, ks):
    lane_head = lax.broadcasted_iota(jnp.int32, (1, M_QK_W), 1) >> DK_SHIFT
    row_head = lax.broadcasted_iota(jnp.int32, (M_QK_W, 1), 0) >> DK_SHIFT
    qt = qvot(0, M_QK_W)
    k_stack = jnp.concatenate([jnp.where(lane_head == h, ks, jnp.zeros_like(ks)) for h in range(M_HEADS)], axis=0)
    qw = jnp.concatenate([jnp.where(row_head == h, qt, jnp.zeros_like(qt)) for h in range(M_HEADS)], axis=1)
    return k_stack, qw, _dot(k_stack, qt)


def _group_values(qvot, sc_t, at_mats, big_m, causal):
    ones_rows = jnp.where(lax.broadcasted_iota(jnp.int32, (BF16_ROWS, GROUP), 0) == 0, 1.0, 0.0).astype(BF16)
    zero_blk = jnp.zeros((GROUP, GROUP), BF16)
    vta, s_rows = [], []
    for h in range(M_HEADS):
        w_t = jnp.where(causal, jnp.exp(at_mats[h] - big_m[h:h + 1]), 0.0)
        s_t = (sc_t[h * GROUP:(h + 1) * GROUP] * w_t).astype(BF16)
        s_rows.append(jnp.concatenate([s_t if j == h else zero_blk for j in range(M_HEADS)], axis=1))
        vta.append(jnp.concatenate([qvot(M_QK_W + h * M_DV, M_QK_W + (h + 1) * M_DV), ones_rows], axis=0))
    return vta, _dot(jnp.concatenate(vta, axis=1), jnp.concatenate(s_rows, axis=0))


def _group_intra(qvot, ks, at_mats, big_m, causal):
    k_stack, qw, sc_t = _group_scores(qvot, ks)
    vta, intra = _group_values(qvot, sc_t, at_mats, big_m, causal)
    return k_stack, qw, vta, intra


def _weighted_values(vta, w_rows):
    return jnp.concatenate([(vta[h].astype(F32) * w_rows[h:h + 1]).astype(BF16) for h in range(M_HEADS)], axis=1)


def _lanes_x(rows):
    return jnp.concatenate([rows[h:h + 1] for h in range(M_HEADS)], axis=1)


def _group_out(qvot, inter, intra, w_inter, e_neg_m, nw_ref):
    outs = []
    out_all = inter * _lanes_x(w_inter) + intra
    for h in range(M_HEADS):
        out_t = out_all[:, h * GROUP:(h + 1) * GROUP]
        hh = out_t[0:M_DV] / jnp.maximum(jnp.abs(out_t[M_DV:M_DV + 1]), e_neg_m[h:h + 1])
        ms = jnp.mean(hh * hh, axis=0, keepdims=True)
        hn = (hh * lax.rsqrt(ms + EPS)) * nw_ref[h * M_DV:(h + 1) * M_DV]
        o_t = qvot(M_QK_W + M_V_W + h * M_DV, M_QK_W + M_V_W + (h + 1) * M_DV)
        outs.append(hn * jax.nn.sigmoid(o_t.astype(F32)))
    return jnp.concatenate(outs, axis=0).T


def _decay_row(g_vec, lane0):
    lane_head = lax.broadcasted_iota(jnp.int32, (1, M_QK_W), 1) >> DK_SHIFT
    g_row = jnp.zeros((1, M_QK_W), F32)
    for h in range(M_HEADS):
        g_row = jnp.where(lane_head == h, g_vec[h:h + 1, lane0:lane0 + 1], g_row)
    return g_row


def _mlstm_sample_kernel(seq_len, qvot_ref, gt_ref, k_ref, mrow_ref, bias_ref, nw_ref, c_ref, n_ref,
                         h_ref, c_out_ref, n_out_ref, mt_ref):
    n_seq = GROUP // seq_len
    shift = seq_len.bit_length() - 1
    same, causal = _chunk_masks(shift)
    lane_seq = lax.broadcasted_iota(jnp.int32, (1, GROUP), 1) >> shift
    lane_seq_x = jnp.concatenate([lane_seq] * M_HEADS, axis=1)
    row0 = lax.broadcasted_iota(jnp.int32, (BF16_ROWS, M_QK_W), 0) == 0
    qvot = lambda r0, r1: qvot_ref[0, r0:r1, :]
    ks = k_ref[...] * (M_DK ** -0.5)
    b4, a4, at_mats, run4, chunk4 = _group_gates(gt_ref[0] + bias_ref[...], same, causal)
    big_m, w_inter, g_vec, w_last, m_t, e_neg_m = _group_weights(mrow_ref[0, 0:M_HEADS, :], b4, a4, run4, chunk4)
    mt_ref[0] = jnp.concatenate([m_t, m_t], axis=0)
    k_stack, qw, vta, intra = _group_intra(qvot, ks, at_mats, big_m, causal)

    st_old = []
    for s in range(n_seq):
        n_rows = jnp.where(row0, jnp.broadcast_to(n_ref[s:s + 1, :], (BF16_ROWS, M_QK_W)), 0.0)
        st_old.append(jnp.concatenate([c_ref[s].T, n_rows], axis=0))
    inter_all = _dot(jnp.concatenate(st_old, axis=0).astype(BF16), qw)
    inter = inter_all[0:S_ROWS]
    for s in range(1, n_seq):
        inter = jnp.where(lane_seq_x == s, inter_all[s * S_ROWS:(s + 1) * S_ROWS], inter)
    tall = jnp.concatenate([_weighted_values(vta, jnp.where(lane_seq == s, w_last, 0.0)) for s in range(n_seq)],
                           axis=0)
    d_st = _dot(tall, k_stack)
    for s in range(n_seq):
        st_new = _decay_row(g_vec, s * seq_len) * st_old[s] + d_st[s * S_ROWS:(s + 1) * S_ROWS]
        c_out_ref[s] = st_new[0:M_DV].T
        n_out_ref[s:s + 1, :] = st_new[M_DV:M_DV + 1]
    h_ref[...] = _group_out(qvot, inter, intra, w_inter, e_neg_m, nw_ref).astype(h_ref.dtype)


def _mlstm_sample(qvot, gt, k_m, mrow, bias_col, nw_col, c, n, seq_len):
    ngrp = qvot.shape[0]
    n_seq = GROUP // seq_len
    full = lambda a: pl.BlockSpec(a.shape, lambda i: (0,) * a.ndim)
    slab = lambda a: pl.BlockSpec((1,) + a.shape[1:], lambda i: (i, 0, 0))
    row = lambda w: pl.BlockSpec((GROUP, w), lambda i: (i, 0))
    c_spec = pl.BlockSpec((n_seq,) + c.shape[1:], lambda i: (i, 0, 0))
    n_spec = pl.BlockSpec((n_seq, n.shape[1]), lambda i: (i, 0))
    return pl.pallas_call(
        functools.partial(_mlstm_sample_kernel, seq_len),
        grid=(ngrp,),
        in_specs=[slab(qvot), slab(gt), row(M_QK_W), slab(mrow), full(bias_col), full(nw_col), c_spec, n_spec],
        out_specs=[row(M_V_W), c_spec, n_spec, slab(mrow)],
        out_shape=[jax.ShapeDtypeStruct((ngrp * GROUP, M_V_W), BF16),
                   jax.ShapeDtypeStruct(c.shape, F32),
                   jax.ShapeDtypeStruct(n.shape, F32),
                   jax.ShapeDtypeStruct(mrow.shape, F32)],
        compiler_params=_params(("arbitrary",)),
        name="mlstm_sample",
    )(qvot, gt, k_m, mrow, bias_col, nw_col, c, n)


def _swa_scores(qbs, has_prev, sink_ref, qt_ref, kv_ref, kvp_ref):
    hd = A_HEAD_DIM
    nq = A_GROUPS * WINDOW
    si = lax.broadcasted_iota(jnp.int32, (2 * WINDOW, nq), 0)
    qi = lax.broadcasted_iota(jnp.int32, (2 * WINDOW, nq), 1) & (WINDOW - 1)
    local = ((si < WINDOW) & (si > qi)) | ((si >= WINDOW) & (si - WINDOW <= qi))
    first = local & (has_prev | (si >= WINDOW))
    lane_grp = lax.broadcasted_iota(jnp.int32, (1, nq), 1) >> WINDOW_SHIFT
    kv_block = lambda i: kvp_ref[...] if i == 0 else kv_ref[(i - 1) * WINDOW:i * WINDOW, :]
    k_bf = lambda i: kv_block(i)[:, 0:A_KV_W].astype(BF16)
    vt_bf = lambda i: kv_block(i)[:, A_KV_W:].T.astype(BF16)
    zeros = jnp.zeros((hd, nq), BF16)
    sinks = []
    for kvh in range(A_KV_HEADS):
        sk = jnp.zeros((1, nq), F32)
        for g in range(A_GROUPS):
            sk = jnp.where(lane_grp == g, sink_ref[kvh * A_GROUPS + g], sk)
        sinks.append(sk)
    units = []
    for qb in qbs:
        kk = jnp.concatenate([k_bf(qb), k_bf(qb + 1)], axis=0)
        vt = jnp.concatenate([vt_bf(qb), vt_bf(qb + 1)], axis=1)
        mask = local if qb > 0 else first
        for kvh in range(A_KV_HEADS):
            q4t = jnp.concatenate(
                [qt_ref[qb, (kvh * A_GROUPS + g) * hd:(kvh * A_GROUPS + g + 1) * hd, :] for g in range(A_GROUPS)],
                axis=1)
            wq = jnp.concatenate([q4t, zeros] if kvh == 0 else [zeros, q4t], axis=0)
            units.append((jnp.where(mask, _dot(kk, wq), -jnp.inf), sinks[kvh], vt))
    return units


def _swa_finish(qbs, units, h_ref):
    hd = A_HEAD_DIM
    for i, qb in enumerate(qbs):
        pieces = []
        for kvh in range(A_KV_HEADS):
            s, sk, vt = units[i * A_KV_HEADS + kvh]
            mx = jnp.maximum(jnp.max(s, axis=0, keepdims=True), sk)
            p = jnp.exp(s - mx)
            den = jnp.sum(p, axis=0, keepdims=True) + jnp.exp(sk - mx)
            ot = _dot(vt, p.astype(BF16))[kvh * hd:(kvh + 1) * hd] / den
            pieces += [ot[:, g * WINDOW:(g + 1) * WINDOW] for g in range(A_GROUPS)]
        h_t = jnp.concatenate(pieces, axis=0)
        h_ref[qb * WINDOW:(qb + 1) * WINDOW, :] = h_t.T.astype(h_ref.dtype)


def _front_kernel(tiles_per_seq, n_cast, *refs):
    (sink_ref, x_ref, nw_ref, wmt_ref, wqgt_ref, wn_ref, bd_ref, qcol_ref, krow_ref, bias_ref, nwm_ref) = refs[:11]
    cast_in = refs[11:11 + n_cast]
    (gab_ref, hm_ref, ha_ref, kwin_ref, vwin_ref, st_ref, m_ref) = refs[11 + n_cast:18 + n_cast]
    cast_out = refs[18 + n_cast:18 + 2 * n_cast]
    (q_qvot, q_gt, q_qat, q_km, q_kva, p_qvot, p_gt, p_qat, p_km, p_kva, st_s, m_s, kvp_s) = refs[18 + 2 * n_cast:]
    _front_body(tiles_per_seq, sink_ref, x_ref, nw_ref, wmt_ref, wqgt_ref, wn_ref, bd_ref, qcol_ref, krow_ref,
                bias_ref, nwm_ref, gab_ref, hm_ref, ha_ref, kwin_ref, vwin_ref, st_ref, m_ref,
                q_qvot, q_gt, q_qat, q_km, q_kva, p_qvot, p_gt, p_qat, p_km, p_kva, st_s, m_s, kvp_s)
    for src, dst in zip(cast_in, cast_out):
        dst[...] = src[...].astype(dst.dtype)


def _front_body(tiles_per_seq, sink_ref, x_ref, nw_ref, wmt_ref, wqgt_ref, wn_ref, bd_ref, qcol_ref, krow_ref,
                bias_ref, nwm_ref,
                gab_ref, hm_ref, ha_ref, kwin_ref, vwin_ref, st_ref, m_ref,
                q_qvot, q_gt, q_qat, q_km, q_kva, p_qvot, p_gt, p_qat, p_km, p_kva, st_s, m_s, kvp_s):
    k = pl.program_id(0)

    @pl.when(k == 0)
    def _():
        for r in (p_qvot, p_gt, p_qat, p_km, p_kva, st_s, m_s, kvp_s):
            r[...] = jnp.zeros(r.shape, r.dtype)

    pieces = _proj_chunks(x_ref, nw_ref, wmt_ref, wqgt_ref, wn_ref, bd_ref, qcol_ref, krow_ref,
                          q_qvot, q_gt, q_qat, q_km, q_kva, gab_ref)
    n_piece = len(pieces)

    seq_start = lax.rem(k - 1 + tiles_per_seq, tiles_per_seq) == 0
    n_blk = p_qat.shape[0]
    same, causal = _chunk_masks(GROUP.bit_length() - 1)
    st = jnp.where(seq_start, 0.0, st_s[...])
    m_col = jnp.where(seq_start, 0.0, m_s[0:M_HEADS, 0:1])

    grp = []
    for g in range(n_blk):
        qvot = lambda r0, r1, g=g: p_qvot[g, r0:r1, :]
        ks = p_km[g * GROUP:(g + 1) * GROUP, :] * (M_DK ** -0.5)
        b4, a4, at_mats, run4, chunk4 = _group_gates(p_gt[g] + bias_ref[...], same, causal)
        weights = _group_weights(jnp.broadcast_to(m_col, (M_HEADS, GROUP)), b4, a4, run4, chunk4)
        m_col = weights[4][:, GROUP - 1:GROUP]
        grp.append((qvot, at_mats, weights) + _group_scores(qvot, ks))
    has_prev = jnp.logical_not(seq_start)
    qb_lo, qb_hi = list(range(0, n_blk // 2)), list(range(n_blk // 2, n_blk))
    units_lo = _swa_scores(qb_lo, has_prev, sink_ref, p_qat, p_kva, kvp_s)
    inter = [_dot(st.astype(BF16), grp[0][4])]
    for piece in pieces[0:1]:
        piece()

    vals = []
    for qvot, at_mats, (big_m, _, _, w_last, _, _), k_stack, _, sc_t in grp:
        vta, intra = _group_values(qvot, sc_t, at_mats, big_m, causal)
        vals.append((intra, _dot(_weighted_values(vta, w_last), k_stack)))
    _swa_finish(qb_lo, units_lo, ha_ref)
    units_hi = _swa_scores(qb_hi, has_prev, sink_ref, p_qat, p_kva, kvp_s)
    for piece in pieces[1:3]:
        piece()

    for g in range(n_blk):
        st = _decay_row(grp[g][2][2], 0) * st + vals[g][1]
        if g + 1 < n_blk:
            inter.append(_dot(st.astype(BF16), grp[g + 1][4]))
    _swa_finish(qb_hi, units_hi, ha_ref)
    for piece in pieces[3:]:
        piece()

    for g in range(n_blk):
        qvot, _, (_, w_inter, _, _, _, e_neg_m) = grp[g][0:3]
        hm_ref[g * GROUP:(g + 1) * GROUP, :] = _group_out(qvot, inter[g], vals[g][0], w_inter, e_neg_m,
                                                          nwm_ref).astype(hm_ref.dtype)
    st_s[...] = st
    st_ref[0] = st
    m_rows = jnp.broadcast_to(m_col, (M_HEADS, LANES))
    m_rows = jnp.concatenate([m_rows, m_rows], axis=0)
    m_s[...] = m_rows
    m_ref[0] = m_rows
    tm = p_kva.shape[0]
    kwin_ref[0] = p_kva[tm - WINDOW:, 0:A_KV_W]
    vwin_ref[0] = p_kva[tm - WINDOW:, A_KV_W:]
    kvp_s[...] = p_kva[tm - WINDOW:, :]

    for p, q in ((p_qvot, q_qvot), (p_gt, q_gt), (p_qat, q_qat), (p_km, q_km), (p_kva, q_kva)):
        p[...] = q[...]


def _cast_rows(rows, n_steps):
    rb = BF16_ROWS
    while rows % rb or rows // rb > n_steps:
        rb += BF16_ROWS
    return rb


def _front(x2d, nb, sinks, nw, wmt, wqgt, wn, bd, qcol, krow, bias_col, nwm_col, cast_ws, tm):
    n = x2d.shape[0]
    n_tiles = n // tm
    tps = n_tiles // nb
    n_blk = tm // LANES
    w_qvo = M_QK_W + 2 * M_V_W
    cur = lambda w: pl.BlockSpec((tm, w), lambda k: (jnp.minimum(k, n_tiles - 1), 0))
    prev = lambda w: pl.BlockSpec((tm, w), lambda k: (jnp.maximum(k - 1, 0), 0))
    per_seq = lambda r, w: pl.BlockSpec((1, r, w), lambda k: (jnp.maximum(k - 1, 0) // tps, 0, 0))

    def cast_spec(a):
        rb = _cast_rows(a.shape[0], n_tiles)
        return pl.BlockSpec((rb, a.shape[1]), lambda k: (jnp.minimum(k, a.shape[0] // rb - 1), 0))

    cast_in_specs = [cast_spec(a) for a in cast_ws]
    cast_out_specs = [cast_spec(a) for a in cast_ws]
    proj_scratch = [pltpu.VMEM((n_blk, w_qvo, LANES), BF16), pltpu.VMEM((n_blk, 2 * M_HEADS, LANES), F32),
                    pltpu.VMEM((n_blk, A_Q_W, LANES), BF16), pltpu.VMEM((tm, M_QK_W), BF16),
                    pltpu.VMEM((tm, 2 * A_KV_W), F32)]
    return pl.pallas_call(
        functools.partial(_front_kernel, tps, len(cast_ws)),
        grid=(n_tiles + 1,),
        in_specs=[pl.BlockSpec(memory_space=pltpu.SMEM), cur(D_MODEL)]
                 + [_const_spec(a.shape) for a in (nw, wmt, wqgt, wn, bd, qcol, krow, bias_col, nwm_col)]
                 + cast_in_specs,
        out_specs=[cur(2 * D_MODEL), prev(M_V_W), prev(A_Q_W), per_seq(WINDOW, A_KV_W), per_seq(WINDOW, A_KV_W),
                   per_seq(S_ROWS, M_QK_W), per_seq(SUBLANES, LANES)] + cast_out_specs,
        out_shape=[jax.ShapeDtypeStruct((n, 2 * D_MODEL), BF16),
                   jax.ShapeDtypeStruct((n, M_V_W), BF16),
                   jax.ShapeDtypeStruct((n, A_Q_W), BF16),
                   jax.ShapeDtypeStruct((nb, WINDOW, A_KV_W), F32),
                   jax.ShapeDtypeStruct((nb, WINDOW, A_KV_W), F32),
                   jax.ShapeDtypeStruct((nb, S_ROWS, M_QK_W), F32),
                   jax.ShapeDtypeStruct((nb, SUBLANES, LANES), F32)]
                  + [jax.ShapeDtypeStruct(a.shape, BF16) for a in cast_ws],
        scratch_shapes=proj_scratch + proj_scratch + [pltpu.VMEM((S_ROWS, M_QK_W), F32),
                                                       pltpu.VMEM((SUBLANES, LANES), F32),
                                                       pltpu.VMEM((WINDOW, 2 * A_KV_W), F32)],
        compiler_params=_params(("arbitrary",)),
        name="front",
    )(sinks, x2d, nw, wmt, wqgt, wn, bd, qcol, krow, bias_col, nwm_col, *cast_ws)


def _swa_sample_kernel(seq_len, sink_ref, qt_ref, kv_ref, ck_ref, cv_ref, h_ref, kwin_ref, vwin_ref):
    n_seq = GROUP // seq_len
    wb = ck_ref.shape[1]
    n_keys = wb + BF16_ROWS
    lane = lax.broadcasted_iota(jnp.int32, (1, LANES), 1)
    lo_half = lane < HALF
    q_rows = qt_ref[0].astype(F32).T
    k_new = kv_ref[:, 0:A_KV_W].reshape(n_seq, seq_len, A_KV_W)
    v_new = kv_ref[:, A_KV_W:].reshape(n_seq, seq_len, A_KV_W)
    kwin_ref[:, 0:wb - seq_len, :] = ck_ref[:, seq_len:wb, :]
    vwin_ref[:, 0:wb - seq_len, :] = cv_ref[:, seq_len:wb, :]
    kwin_ref[:, wb - seq_len:wb, :] = k_new
    vwin_ref[:, wb - seq_len:wb, :] = v_new

    def to_kv_half(x, head):
        kvh = head // A_GROUPS
        if head % 2 != kvh:
            x = pltpu.roll(x, HALF, axis=1)
        return jnp.where(lo_half if kvh == 0 else ~lo_half, x, 0.0)

    lhs = jnp.concatenate(
        [to_kv_half(q_rows[:, (h // 2) * LANES:(h // 2 + 1) * LANES], h).reshape(n_seq, seq_len, LANES)
         for h in range(A_HEADS)], axis=1).astype(BF16)
    zpad = jnp.zeros((n_seq, BF16_ROWS - seq_len, A_KV_W), F32)
    k_all = jnp.concatenate([ck_ref[...], k_new, zpad], axis=1).astype(BF16)
    v_all = jnp.concatenate([cv_ref[...], v_new, zpad], axis=1).astype(BF16)
    s = jnp.einsum('sqf,skf->sqk', lhs, k_all, preferred_element_type=F32)
    nrow = A_HEADS * seq_len
    ti = lax.broadcasted_iota(jnp.int32, (nrow, n_keys), 0) & (seq_len - 1)
    ki = lax.broadcasted_iota(jnp.int32, (nrow, n_keys), 1)
    mask = ((ki < wb) & (ti + wb - ki < WINDOW)) | ((ki >= wb) & (ki - wb <= ti))
    row_head = lax.broadcasted_iota(jnp.int32, (nrow, 1), 0) >> (seq_len.bit_length() - 1)
    sk = jnp.zeros((nrow, 1), F32)
    for h in range(A_HEADS):
        sk = jnp.where(row_head == h, sink_ref[h], sk)
    s = jnp.where(mask, s, -jnp.inf)
    mx = jnp.maximum(jnp.max(s, axis=-1, keepdims=True), sk)
    p = jnp.exp(s - mx)
    den = jnp.sum(p, axis=-1, keepdims=True) + jnp.exp(sk - mx)
    o = jnp.einsum('sqk,skf->sqf', p.astype(BF16), v_all, preferred_element_type=F32) / den

    def from_kv_half(head):
        x = o[:, head * seq_len:(head + 1) * seq_len, :].reshape(GROUP, LANES)
        return pltpu.roll(x, HALF, axis=1) if head % 2 != head // A_GROUPS else x

    for c in range(A_HEADS // 2):
        h_ref[:, c * LANES:(c + 1) * LANES] = jnp.where(lo_half, from_kv_half(2 * c),
                                                        from_kv_half(2 * c + 1)).astype(h_ref.dtype)


def _swa_sample(qat, kv_a, cache_k, cache_v, sinks, seq_len):
    ngrp = qat.shape[0]
    n_seq = GROUP // seq_len
    wb = cache_k.shape[1]
    row = lambda w: pl.BlockSpec((GROUP, w), lambda i: (i, 0))
    cache = pl.BlockSpec((n_seq, wb, A_KV_W), lambda i: (i, 0, 0))
    return pl.pallas_call(
        functools.partial(_swa_sample_kernel, seq_len),
        grid=(ngrp,),
        in_specs=[pl.BlockSpec(memory_space=pltpu.SMEM), pl.BlockSpec((1, A_Q_W, LANES), lambda i: (i, 0, 0)),
                  row(2 * A_KV_W), cache, cache],
        out_specs=[row(A_Q_W), cache, cache],
        out_shape=[jax.ShapeDtypeStruct((ngrp * GROUP, A_Q_W), BF16),
                   jax.ShapeDtypeStruct(cache_k.shape, F32),
                   jax.ShapeDtypeStruct(cache_v.shape, F32)],
        compiler_params=_params(("arbitrary",)),
        name="swa_sample",
    )(sinks, qat, kv_a, cache_k, cache_v)


def _merge_ffn_kernel(x_ref, hm_ref, ha_ref, gab_ref, wa_ref, wb_ref, wo_ref, nw_ref, wg_ref, wu_ref, wd_ref,
                      y_ref):
    ga = jax.nn.sigmoid(gab_ref[:, 0:D_MODEL].astype(F32))
    gb = jax.nn.sigmoid(gab_ref[:, D_MODEL:].astype(F32))
    mix = ga * _dot(hm_ref[...], wa_ref[...]) + gb * _dot(ha_ref[...], wb_ref[...])
    x1 = x_ref[...] + _dot(mix.astype(BF16), wo_ref[...])
    hf = _rms_rows(x1, nw_ref[...]).astype(BF16)
    gate = _dot(hf, wg_ref[...])
    up = _dot(hf, wu_ref[...])
    act = (jax.nn.silu(gate) * up).astype(BF16)
    y_ref[...] = x1 + _dot(act, wd_ref[...])


def _merge_ffn(x2d, h_m, h_a, g_ab, wa, wb, wo, nw, wg, wu, wd, tm):
    n = x2d.shape[0]
    row = lambda w: pl.BlockSpec((tm, w), lambda i: (i, 0))
    return pl.pallas_call(
        _merge_ffn_kernel,
        grid=(n // tm,),
        in_specs=[row(D_MODEL), row(M_V_W), row(A_Q_W), row(2 * D_MODEL)]
                 + [_const_spec(w.shape) for w in (wa, wb, wo, nw, wg, wu, wd)],
        out_specs=row(D_MODEL),
        out_shape=jax.ShapeDtypeStruct((n, D_MODEL), F32),
        compiler_params=_params(("arbitrary",)),
        name="merge_ffn",
    )(x2d, h_m, h_a, g_ab, wa, wb, wo, nw, wg, wu, wd)


def kernel(x_prompt, x_sample, state_mlstm_C, state_mlstm_n, state_mlstm_m, cache_swa_k, cache_swa_v,
           norm_mix_w, w_in, mlstm_i_bias, mlstm_f_bias, mlstm_norm_w, q_norm_w, k_norm_w, attn_sinks,
           w_branch_a, w_branch_b, w_out, norm_ffn_w, w_gate, w_up, w_down):
    depth = w_in.shape[0]
    assert depth == 1, "single trunk layer"
    l = 0
    bp, tp = x_prompt.shape[0], x_prompt.shape[1]
    bs, ts = x_sample.shape[0], x_sample.shape[1]
    assert tp % 512 == 0 and (bs * ts) % GROUP == 0 and GROUP % ts == 0 and ts & (ts - 1) == 0
    assert ts <= SUBLANES, "sample chunk must fit one sublane tile"

    wt = jnp.transpose(w_in[l])
    c_km, c_vm = M_QK_W, 2 * M_QK_W
    c_g = 2 * M_QK_W + 2 * M_V_W
    c_qa = c_g + 2 * M_HEADS
    c_ka = c_qa + A_Q_W
    gate_pad = jnp.zeros((BF16_ROWS - 2 * M_HEADS, D_MODEL), F32)
    wmt = jnp.concatenate([wt[0:c_km], wt[c_vm:c_g]], axis=0).astype(BF16)
    wqgt = jnp.concatenate([wt[c_qa:c_ka], wt[c_g:c_qa], gate_pad], axis=0).astype(BF16)
    wn = jnp.concatenate([wt[c_km:c_vm], wt[c_ka:]], axis=0).astype(BF16)
    head_of = jnp.arange(A_KV_W) // A_HEAD_DIM
    bd = (head_of[:, None] == head_of[None, :]).astype(BF16)
    qcol = (jnp.tile(q_norm_w[l], A_HEADS) * (A_HEAD_DIM ** -0.5)).reshape(A_Q_W, 1)
    krow = jnp.tile(k_norm_w[l], A_KV_HEADS).reshape(1, A_KV_W)
    nw_mix = norm_mix_w[l].reshape(1, D_MODEL)
    nw_ffn = norm_ffn_w[l].reshape(1, D_MODEL)
    bias_col = jnp.concatenate([mlstm_i_bias[l], mlstm_f_bias[l]]).reshape(2 * M_HEADS, 1)
    nw_col = mlstm_norm_w[l].reshape(M_V_W, 1)
    sinks = attn_sinks[l]
    proj = lambda x2d, tm: _proj(x2d, nw_mix, wmt, wqgt, wn, bd, qcol, krow, tm)

    xp = x_prompt.reshape(bp * tp, D_MODEL)
    later_ws = (w_branch_a[l], w_branch_b[l], w_out[l], w_gate[l], w_up[l], w_down[l])
    g_ab, h_m, h_a, kwin_p, vwin_p, st_p, m_p, wa, wb, wo, wg, wu, wd = _front(
        xp, bp, sinks, nw_mix, wmt, wqgt, wn, bd, qcol, krow, bias_col, nw_col, later_ws, 512)
    merge = lambda x2d, h_m, h_a, g_ab, tm: _merge_ffn(x2d, h_m, h_a, g_ab, wa, wb, wo, nw_ffn, wg, wu, wd, tm)
    yp = merge(xp, h_m, h_a, g_ab, 512).reshape(bp, tp, D_MODEL)
    c_p = jnp.swapaxes(st_p[:, :M_DV, :], 1, 2).reshape(bp, M_HEADS, M_DK, M_DV)
    n_p = st_p[:, M_DV, :].reshape(bp, M_HEADS, M_DK)
    m_pr = m_p[:, :M_HEADS, 0]

    ns = bs * ts
    xs = x_sample.reshape(ns, D_MODEL)
    tms = 512 if ns % 512 == 0 else GROUP
    qvot, gt, qat, k_m, kv_a, g_ab = proj(xs, tms)
    ngrp = ns // GROUP
    m_lanes = jnp.repeat(state_mlstm_m[l], ts, axis=0).reshape(ngrp, GROUP, M_HEADS)
    mrow = jnp.pad(jnp.swapaxes(m_lanes, 1, 2), ((0, 0), (0, SUBLANES - M_HEADS), (0, 0)))
    h_m, c_s, n_s, mt_s = _mlstm_sample(qvot, gt, k_m, mrow, bias_col, nw_col,
                                        state_mlstm_C[l].reshape(bs, M_QK_W, M_DV),
                                        state_mlstm_n[l].reshape(bs, M_QK_W), ts)
    wbuf = cache_swa_k.shape[2]
    h_a, kwin_s, vwin_s = _swa_sample(qat, kv_a, cache_swa_k[l].reshape(bs, wbuf, A_KV_W),
                                      cache_swa_v[l].reshape(bs, wbuf, A_KV_W), sinks, ts)
    ys = merge(xs, h_m, h_a, g_ab, tms).reshape(bs, ts, D_MODEL)
    m_s = jnp.swapaxes(mt_s[:, :M_HEADS, :], 1, 2).reshape(bs, ts, M_HEADS)[:, ts - 1, :]

    kv5 = lambda a: a.reshape(a.shape[0], a.shape[1], A_KV_HEADS, A_HEAD_DIM)[None]
    return (yp, ys,
            c_p[None], n_p[None], m_pr[None], kv5(kwin_p), kv5(vwin_p),
            c_s.reshape(bs, M_HEADS, M_DK, M_DV)[None], n_s.reshape(bs, M_HEADS, M_DK)[None], m_s[None],
            kv5(kwin_s), kv5(vwin_s))
```

```python
import functools

import jax
import jax.numpy as jnp
from jax import lax
from jax.experimental import pallas as pl
from jax.experimental.pallas import tpu as pltpu

F32 = jnp.float32
BF16 = jnp.bfloat16

D_MODEL = 1024
M_HEADS = 4
M_DK = 64
M_DV = 128
M_CHUNK = 64
M_QK_W = M_HEADS * M_DK
M_V_W = M_HEADS * M_DV
A_HEADS = 8
A_KV_HEADS = 2
A_HEAD_DIM = 64
A_GROUPS = A_HEADS // A_KV_HEADS
A_Q_W = A_HEADS * A_HEAD_DIM
A_KV_W = A_KV_HEADS * A_HEAD_DIM
WINDOW = 128
D_FF = 2816
EPS = 1e-6

LANES = 128
SUBLANES = 8
BF16_ROWS = 16
GROUP = 128
S_ROWS = M_DV + BF16_ROWS
VMEM_LIMIT = 56 * 1024 * 1024
DK_SHIFT = M_DK.bit_length() - 1
WINDOW_SHIFT = WINDOW.bit_length() - 1
HALF = LANES // 2
assert A_HEAD_DIM == HALF and A_KV_W == LANES, "attention head pairs share one lane-width"

NT_DIMS = (((1,), (1,)), ((), ()))


def _dot(a, b):
    return jnp.dot(a, b, preferred_element_type=F32)


def _dot_nt(a, b):
    return lax.dot_general(a, b, NT_DIMS, preferred_element_type=F32)


def _const_spec(shape):
    nd = len(shape)
    return pl.BlockSpec(shape, lambda *_: (0,) * nd, pipeline_mode=pl.Buffered(1))


def _params(sem):
    return pltpu.CompilerParams(dimension_semantics=sem, vmem_limit_bytes=VMEM_LIMIT)


def _rms_rows(x, nw):
    ms = jnp.mean(x * x, axis=-1, keepdims=True)
    return (x * lax.rsqrt(ms + EPS)) * nw


GAB_CHUNK = 512


def _proj_chunks(x_ref, nw_ref, wmt_ref, wqgt_ref, wn_ref, bd_ref, qcol_ref, krow_ref,
                 qvot_ref, gt_ref, qat_ref, km_ref, kva_ref, gab_ref):
    hn = _rms_rows(x_ref[...], nw_ref[...]).astype(BF16)
    n_blk = qat_ref.shape[0]

    def put(ref, rows, val):
        for c in range(n_blk):
            ref[c, rows, :] = val[:, c * LANES:(c + 1) * LANES].astype(ref.dtype)

    def mlstm_qvo():
        put(qvot_ref, slice(None), _dot_nt(wmt_ref[...], hn))

    def attn_q_and_gates():
        qt = _dot_nt(wqgt_ref[...], hn)
        put(gt_ref, slice(None), qt[A_Q_W:A_Q_W + 2 * M_HEADS])
        for h in range(A_HEADS):
            hs = slice(h * A_HEAD_DIM, (h + 1) * A_HEAD_DIM)
            blk = qt[hs]
            ssq_q = jnp.sum(blk * blk, axis=0, keepdims=True)
            put(qat_ref, hs, (blk * lax.rsqrt(ssq_q * (1.0 / A_HEAD_DIM) + EPS)) * qcol_ref[hs])

    def mlstm_k():
        km_ref[...] = _dot_nt(hn, wn_ref[0:M_QK_W, :]).astype(km_ref.dtype)

    def branch_gates(c0):
        def run():
            w0 = M_QK_W + 2 * A_KV_W + c0
            gab_ref[:, c0:c0 + GAB_CHUNK] = _dot_nt(hn, wn_ref[w0:w0 + GAB_CHUNK, :]).astype(gab_ref.dtype)
        return run

    def attn_kv():
        kv = _dot_nt(hn, wn_ref[M_QK_W:M_QK_W + 2 * A_KV_W, :])
        k = kv[:, 0:A_KV_W]
        ksq = k * k
        hi = ksq.astype(BF16)
        lo = (ksq - hi.astype(F32)).astype(BF16)
        ssq = _dot(hi, bd_ref[...]) + _dot(lo, bd_ref[...])
        kva_ref[:, 0:A_KV_W] = (k * lax.rsqrt(ssq * (1.0 / A_HEAD_DIM) + EPS)) * krow_ref[...]
        kva_ref[:, A_KV_W:] = kv[:, A_KV_W:]

    return ([mlstm_qvo, attn_q_and_gates, mlstm_k]
            + [branch_gates(c0) for c0 in range(0, gab_ref.shape[1], GAB_CHUNK)] + [attn_kv])


def _proj_kernel(*refs):
    for piece in _proj_chunks(*refs):
        piece()


def _proj(x2d, nw, wmt, wqgt, wn, bd, qcol, krow, tm):
    n = x2d.shape[0]
    row = lambda w: pl.BlockSpec((tm, w), lambda i: (i, 0))
    n_blk = tm // LANES
    slab = lambda r: pl.BlockSpec((n_blk, r, LANES), lambda i: (i, 0, 0))
    w_qvo = M_QK_W + 2 * M_V_W
    return pl.pallas_call(
        _proj_kernel,
        grid=(n // tm,),
        in_specs=[row(D_MODEL)] + [_const_spec(a.shape) for a in (nw, wmt, wqgt, wn, bd, qcol, krow)],
        out_specs=[slab(w_qvo), slab(2 * M_HEADS), slab(A_Q_W), row(M_QK_W), row(2 * A_KV_W), row(2 * D_MODEL)],
        out_shape=[jax.ShapeDtypeStruct((n // LANES, w_qvo, LANES), BF16),
                   jax.ShapeDtypeStruct((n // LANES, 2 * M_HEADS, LANES), F32),
                   jax.ShapeDtypeStruct((n // LANES, A_Q_W, LANES), BF16),
                   jax.ShapeDtypeStruct((n, M_QK_W), BF16),
                   jax.ShapeDtypeStruct((n, 2 * A_KV_W), F32),
                   jax.ShapeDtypeStruct((n, 2 * D_MODEL), BF16)],
        compiler_params=_params(("arbitrary",)),
        name="proj",
    )(x2d, nw, wmt, wqgt, wn, bd, qcol, krow)


def _split3_rows(x):
    hi = x.astype(BF16).astype(F32)
    r1 = x - hi
    mid = r1.astype(BF16).astype(F32)
    lo = r1 - mid
    return jnp.concatenate([hi, mid, lo], axis=0).astype(BF16)


def _log_sigmoid(x):
    return jnp.minimum(x, 0.0) - jnp.log1p(jnp.exp(-jnp.abs(x)))


def _chunk_masks(chunk_shift):
    s = lax.broadcasted_iota(jnp.int32, (GROUP, GROUP), 0)
    t = lax.broadcasted_iota(jnp.int32, (GROUP, GROUP), 1)
    same = (s >> chunk_shift) == (t >> chunk_shift)
    return same, same & (s <= t)


def _group_gates(gt, same, causal):
    cm_bf = jnp.where(causal, 1.0, 0.0).astype(BF16)
    lf = _log_sigmoid(gt)
    nr = gt.shape[0]
    bt3 = _dot(_split3_rows(lf), cm_bf)
    bt = (bt3[0:nr] + bt3[nr:2 * nr]) + bt3[2 * nr:3 * nr]
    b4 = bt[M_HEADS:2 * M_HEADS]
    a4 = gt[0:M_HEADS] - b4
    a8 = jnp.concatenate([a4, a4], axis=0)
    a_cols = jnp.concatenate([a8, jnp.zeros((GROUP - SUBLANES, GROUP), F32)], axis=0).T
    at_mats, run_rows, chunk_rows = [], [], []
    for h in range(M_HEADS):
        at = jnp.broadcast_to(a_cols[:, h:h + 1], (GROUP, GROUP))
        at_mats.append(at)
        run_rows.append(jnp.max(jnp.where(causal, at, -jnp.inf), axis=0, keepdims=True))
        chunk_rows.append(jnp.max(jnp.where(same, at, -jnp.inf), axis=0, keepdims=True))
    return b4, a4, at_mats, jnp.concatenate(run_rows, axis=0), jnp.concatenate(chunk_rows, axis=0)


def _group_weights(m_prev, b4, a4, run4, chunk4):
    big_m = jnp.maximum(m_prev, run4)
    m_last = jnp.maximum(m_prev, chunk4)
    w_inter = jnp.exp(m_prev - big_m)
    g_vec = jnp.exp(m_prev - m_last)
    w_last = jnp.exp(a4 - m_last)
    m_t = b4 + big_m
    return big_m, w_inter, g_vec, w_last, m_t, jnp.exp(-m_t)


def _group_scores(qvot, ks):
    lane_head = lax.broadcasted_iota(jnp.int32, (1, M_QK_W), 1) >> DK_SHIFT
    row_head = lax.broadcasted_iota(jnp.int32, (M_QK_W, 1), 0) >> DK_SHIFT
    qt = qvot(0, M_QK_W)
    k_stack = jnp.concatenate([jnp.where(lane_head == h, ks, jnp.zeros_like(ks)) for h in range(M_HEADS)], axis=0)
    qw = jnp.concatenate([jnp.where(row_head == h, qt, jnp.zeros_like(qt)) for h in range(M_HEADS)], axis=1)
    zero_blk = jnp.zeros((M_DK, GROUP), BF16)
    sc_t = []
    for h in range(0, M_HEADS, 2):
        q_pair = jnp.concatenate(
            [jnp.concatenate([qvot(h * M_DK, (h + 1) * M_DK), zero_blk], axis=1),
             jnp.concatenate([zero_blk, qvot((h + 1) * M_DK, (h + 2) * M_DK)], axis=1)], axis=0)
        sc = _dot(ks[:, h * M_DK:(h + 2) * M_DK], q_pair)
        sc_t += [sc[:, 0:GROUP], sc[:, GROUP:]]
    return k_stack, qw, sc_t


def _group_values(qvot, sc_t, at_mats, big_m, causal):
    ones_rows = jnp.where(lax.broadcasted_iota(jnp.int32, (BF16_ROWS, GROUP), 0) == 0, 1.0, 0.0).astype(BF16)
    zero_blk = jnp.zeros((GROUP, GROUP), BF16)
    vta, s_t = [], []
    for h in range(M_HEADS):
        w_t = jnp.where(causal, jnp.exp(at_mats[h] - big_m[h:h + 1]), 0.0)
        s_t.append((sc_t[h] * w_t).astype(BF16))
        vta.append(jnp.concatenate([qvot(M_QK_W + h * M_DV, M_QK_W + (h + 1) * M_DV), ones_rows], axis=0))
    intra = []
    for h in range(0, M_HEADS, 2):
        pair = jnp.concatenate([jnp.concatenate([s_t[h], zero_blk], axis=1),
                                jnp.concatenate([zero_blk, s_t[h + 1]], axis=1)], axis=0)
        intra.append(_dot(jnp.concatenate(vta[h:h + 2], axis=1), pair))
    return vta, jnp.concatenate(intra, axis=1)


def _weighted_values(vta, w_rows):
    return jnp.concatenate([(vta[h].astype(F32) * w_rows[h:h + 1]).astype(BF16) for h in range(M_HEADS)], axis=1)


def _lanes_x(rows):
    return jnp.concatenate([rows[h:h + 1] for h in range(M_HEADS)], axis=1)


def _group_out(qvot, inter, intra, w_inter, e_neg_m, nw_ref):
    outs = []
    out_all = inter * _lanes_x(w_inter) + intra
    for h in range(M_HEADS):
        out_t = out_all[:, h * GROUP:(h + 1) * GROUP]
        hh = out_t[0:M_DV] / jnp.maximum(jnp.abs(out_t[M_DV:M_DV + 1]), e_neg_m[h:h + 1])
        ms = jnp.mean(hh * hh, axis=0, keepdims=True)
        hn = (hh * lax.rsqrt(ms + EPS)) * nw_ref[h * M_DV:(h + 1) * M_DV]
        o_t = qvot(M_QK_W + M_V_W + h * M_DV, M_QK_W + M_V_W + (h + 1) * M_DV)
        outs.append(hn * jax.nn.sigmoid(o_t.astype(F32)))
    return jnp.concatenate(outs, axis=0).T


def _decay_row(g_vec, lane0):
    lane_head = lax.broadcasted_iota(jnp.int32, (1, M_QK_W), 1) >> DK_SHIFT
    g_row = jnp.zeros((1, M_QK_W), F32)
    for h in range(M_HEADS):
        g_row = jnp.where(lane_head == h, g_vec[h:h + 1, lane0:lane0 + 1], g_row)
    return g_row


def _mlstm_sample_kernel(seq_len, qvot_ref, gt_ref, k_ref, mrow_ref, bias_ref, nw_ref, c_ref, n_ref,
                         h_ref, c_out_ref, n_out_ref, mt_ref):
    n_seq = GROUP // seq_len
    shift = seq_len.bit_length() - 1
    same, causal = _chunk_masks(shift)
    lane_seq = lax.broadcasted_iota(jnp.int32, (1, GROUP), 1) >> shift
    lane_seq_x = jnp.concatenate([lane_seq] * M_HEADS, axis=1)
    row0 = lax.broadcasted_iota(jnp.int32, (BF16_ROWS, M_QK_W), 0) == 0
    qvot = lambda r0, r1: qvot_ref[0, r0:r1, :]
    ks = k_ref[...] * (M_DK ** -0.5)
    b4, a4, at_mats, run4, chunk4 = _group_gates(gt_ref[0] + bias_ref[...], same, causal)
    big_m, w_inter, g_vec, w_last, m_t, e_neg_m = _group_weights(mrow_ref[0, 0:M_HEADS, :], b4, a4, run4, chunk4)
    mt_ref[0] = jnp.concatenate([m_t, m_t], axis=0)
    k_stack, qw, sc_t = _group_scores(qvot, ks)

    st_old = []
    for s in range(n_seq):
        n_rows = jnp.where(row0, jnp.broadcast_to(n_ref[s:s + 1, :], (BF16_ROWS, M_QK_W)), 0.0)
        st_old.append(jnp.concatenate([c_ref[s].T, n_rows], axis=0))
    inter_all = _dot(jnp.concatenate(st_old, axis=0).astype(BF16), qw)
    vta, intra = _group_values(qvot, sc_t, at_mats, big_m, causal)
    inter = inter_all[0:S_ROWS]
    for s in range(1, n_seq):
        inter = jnp.where(lane_seq_x == s, inter_all[s * S_ROWS:(s + 1) * S_ROWS], inter)
    tall = jnp.concatenate([_weighted_values(vta, jnp.where(lane_seq == s, w_last, 0.0)) for s in range(n_seq)],
                           axis=0)
    d_st = _dot(tall, k_stack)
    for s in range(n_seq):
        st_new = _decay_row(g_vec, s * seq_len) * st_old[s] + d_st[s * S_ROWS:(s + 1) * S_ROWS]
        c_out_ref[s] = st_new[0:M_DV].T
        n_out_ref[s:s + 1, :] = st_new[M_DV:M_DV + 1]
    h_ref[...] = _group_out(qvot, inter, intra, w_inter, e_neg_m, nw_ref).astype(h_ref.dtype)


def _mlstm_sample(qvot, gt, k_m, mrow, bias_col, nw_col, c, n, seq_len):
    ngrp = qvot.shape[0]
    n_seq = GROUP // seq_len
    full = lambda a: pl.BlockSpec(a.shape, lambda i: (0,) * a.ndim)
    slab = lambda a: pl.BlockSpec((1,) + a.shape[1:], lambda i: (i, 0, 0))
    row = lambda w: pl.BlockSpec((GROUP, w), lambda i: (i, 0))
    c_spec = pl.BlockSpec((n_seq,) + c.shape[1:], lambda i: (i, 0, 0))
    n_spec = pl.BlockSpec((n_seq, n.shape[1]), lambda i: (i, 0))
    return pl.pallas_call(
        functools.partial(_mlstm_sample_kernel, seq_len),
        grid=(ngrp,),
        in_specs=[slab(qvot), slab(gt), row(M_QK_W), slab(mrow), full(bias_col), full(nw_col), c_spec, n_spec],
        out_specs=[row(M_V_W), c_spec, n_spec, slab(mrow)],
        out_shape=[jax.ShapeDtypeStruct((ngrp * GROUP, M_V_W), BF16),
                   jax.ShapeDtypeStruct(c.shape, F32),
                   jax.ShapeDtypeStruct(n.shape, F32),
                   jax.ShapeDtypeStruct(mrow.shape, F32)],
        compiler_params=_params(("arbitrary",)),
        name="mlstm_sample",
    )(qvot, gt, k_m, mrow, bias_col, nw_col, c, n)


def _swa_scores(qbs, has_prev, sink_ref, qt_ref, kv_ref, kvp_ref):
    hd = A_HEAD_DIM
    nq = A_GROUPS * WINDOW
    si = lax.broadcasted_iota(jnp.int32, (2 * WINDOW, nq), 0)
    qi = lax.broadcasted_iota(jnp.int32, (2 * WINDOW, nq), 1) & (WINDOW - 1)
    local = ((si < WINDOW) & (si > qi)) | ((si >= WINDOW) & (si - WINDOW <= qi))
    first = local & (has_prev | (si >= WINDOW))
    lane_grp = lax.broadcasted_iota(jnp.int32, (1, nq), 1) >> WINDOW_SHIFT
    kv_block = lambda i: kvp_ref[...] if i == 0 else kv_ref[(i - 1) * WINDOW:i * WINDOW, :]
    k_bf = lambda i: kv_block(i)[:, 0:A_KV_W].astype(BF16)
    vt_bf = lambda i: kv_block(i)[:, A_KV_W:].T.astype(BF16)
    zeros = jnp.zeros((hd, nq), BF16)
    sinks = []
    for kvh in range(A_KV_HEADS):
        sk = jnp.zeros((1, nq), F32)
        for g in range(A_GROUPS):
            sk = jnp.where(lane_grp == g, sink_ref[kvh * A_GROUPS + g], sk)
        sinks.append(sk)
    units = []
    for qb in qbs:
        kk = jnp.concatenate([k_bf(qb), k_bf(qb + 1)], axis=0)
        vt = jnp.concatenate([vt_bf(qb), vt_bf(qb + 1)], axis=1)
        mask = local if qb > 0 else first
        for kvh in range(A_KV_HEADS):
            q4t = jnp.concatenate(
                [qt_ref[qb, (kvh * A_GROUPS + g) * hd:(kvh * A_GROUPS + g + 1) * hd, :] for g in range(A_GROUPS)],
                axis=1)
            wq = jnp.concatenate([q4t, zeros] if kvh == 0 else [zeros, q4t], axis=0)
            units.append((jnp.where(mask, _dot(kk, wq), -jnp.inf), sinks[kvh], vt))
    return units


def _swa_finish(qbs, units, h_ref):
    hd = A_HEAD_DIM
    for i, qb in enumerate(qbs):
        pieces = []
        for kvh in range(A_KV_HEADS):
            s, sk, vt = units[i * A_KV_HEADS + kvh]
            mx = jnp.maximum(jnp.max(s, axis=0, keepdims=True), sk)
            p = jnp.exp(s - mx)
            den = jnp.sum(p, axis=0, keepdims=True) + jnp.exp(sk - mx)
            ot = _dot(vt[kvh * hd:(kvh + 1) * hd], p.astype(BF16)) / den
            pieces += [ot[:, g * WINDOW:(g + 1) * WINDOW] for g in range(A_GROUPS)]
        h_t = jnp.concatenate(pieces, axis=0)
        h_ref[qb * WINDOW:(qb + 1) * WINDOW, :] = h_t.T.astype(h_ref.dtype)


def _front_kernel(tiles_per_seq, n_cast, *refs):
    (sink_ref, x_ref, nw_ref, wmt_ref, wqgt_ref, wn_ref, bd_ref, qcol_ref, krow_ref, bias_ref, nwm_ref) = refs[:11]
    cast_in = refs[11:11 + n_cast]
    (gab_ref, hm_ref, ha_ref, kwin_ref, vwin_ref, st_ref, m_ref) = refs[11 + n_cast:18 + n_cast]
    cast_out = refs[18 + n_cast:18 + 2 * n_cast]
    (q_qvot, q_gt, q_qat, q_km, q_kva, p_qvot, p_gt, p_qat, p_km, p_kva, st_s, m_s, kvp_s) = refs[18 + 2 * n_cast:]
    _front_body(tiles_per_seq, sink_ref, x_ref, nw_ref, wmt_ref, wqgt_ref, wn_ref, bd_ref, qcol_ref, krow_ref,
                bias_ref, nwm_ref, gab_ref, hm_ref, ha_ref, kwin_ref, vwin_ref, st_ref, m_ref,
                q_qvot, q_gt, q_qat, q_km, q_kva, p_qvot, p_gt, p_qat, p_km, p_kva, st_s, m_s, kvp_s)
    for src, dst in zip(cast_in, cast_out):
        dst[...] = src[...].astype(dst.dtype)


def _front_body(tiles_per_seq, sink_ref, x_ref, nw_ref, wmt_ref, wqgt_ref, wn_ref, bd_ref, qcol_ref, krow_ref,
                bias_ref, nwm_ref,
                gab_ref, hm_ref, ha_ref, kwin_ref, vwin_ref, st_ref, m_ref,
                q_qvot, q_gt, q_qat, q_km, q_kva, p_qvot, p_gt, p_qat, p_km, p_kva, st_s, m_s, kvp_s):
    k = pl.program_id(0)

    @pl.when(k == 0)
    def _():
        for r in (p_qvot, p_gt, p_qat, p_km, p_kva, st_s, m_s, kvp_s):
            r[...] = jnp.zeros(r.shape, r.dtype)

    pieces = _proj_chunks(x_ref, nw_ref, wmt_ref, wqgt_ref, wn_ref, bd_ref, qcol_ref, krow_ref,
                          q_qvot, q_gt, q_qat, q_km, q_kva, gab_ref)
    n_piece = len(pieces)

    seq_start = lax.rem(k - 1 + tiles_per_seq, tiles_per_seq) == 0
    n_blk = p_qat.shape[0]
    same, causal = _chunk_masks(GROUP.bit_length() - 1)
    st = jnp.where(seq_start, 0.0, st_s[...])
    m_col = jnp.where(seq_start, 0.0, m_s[0:M_HEADS, 0:1])

    grp = []
    for g in range(n_blk):
        qvot = lambda r0, r1, g=g: p_qvot[g, r0:r1, :]
        ks = p_km[g * GROUP:(g + 1) * GROUP, :] * (M_DK ** -0.5)
        b4, a4, at_mats, run4, chunk4 = _group_gates(p_gt[g] + bias_ref[...], same, causal)
        weights = _group_weights(jnp.broadcast_to(m_col, (M_HEADS, GROUP)), b4, a4, run4, chunk4)
        m_col = weights[4][:, GROUP - 1:GROUP]
        grp.append((qvot, at_mats, weights) + _group_scores(qvot, ks))
    has_prev = jnp.logical_not(seq_start)
    qb_lo, qb_hi = list(range(0, n_blk // 2)), list(range(n_blk // 2, n_blk))
    units_lo = _swa_scores(qb_lo, has_prev, sink_ref, p_qat, p_kva, kvp_s)
    inter = [_dot(st.astype(BF16), grp[0][4])]
    for piece in pieces[0:1]:
        piece()

    vals = []
    for qvot, at_mats, (big_m, _, _, w_last, _, _), k_stack, _, sc_t in grp:
        vta, intra = _group_values(qvot, sc_t, at_mats, big_m, causal)
        vals.append((intra, _dot(_weighted_values(vta, w_last), k_stack)))
    _swa_finish(qb_lo, units_lo, ha_ref)
    units_hi = _swa_scores(qb_hi, has_prev, sink_ref, p_qat, p_kva, kvp_s)
    for piece in pieces[1:3]:
        piece()

    for g in range(n_blk):
        st = _decay_row(grp[g][2][2], 0) * st + vals[g][1]
        if g + 1 < n_blk:
            inter.append(_dot(st.astype(BF16), grp[g + 1][4]))
    _swa_finish(qb_hi, units_hi, ha_ref)
    for piece in pieces[3:]:
        piece()

    for g in range(n_blk):
        qvot, _, (_, w_inter, _, _, _, e_neg_m) = grp[g][0:3]
        hm_ref[g * GROUP:(g + 1) * GROUP, :] = _group_out(qvot, inter[g], vals[g][0], w_inter, e_neg_m,
                                                          nwm_ref).astype(hm_ref.dtype)
    st_s[...] = st
    st_ref[0] = st
    m_rows = jnp.broadcast_to(m_col, (M_HEADS, LANES))
    m_rows = jnp.concatenate([m_rows, m_rows], axis=0)
    m_s[...] = m_rows
    m_ref[0] = m_rows
    tm = p_kva.shape[0]
    kwin_ref[0] = p_kva[tm - WINDOW:, 0:A_KV_W]
    vwin_ref[0] = p_kva[tm - WINDOW:, A_KV_W:]
    kvp_s[...] = p_kva[tm - WINDOW:, :]

    for p, q in ((p_qvot, q_qvot), (p_gt, q_gt), (p_qat, q_qat), (p_km, q_km), (p_kva, q_kva)):
        p[...] = q[...]


def _cast_rows(rows, n_steps):
    rb = BF16_ROWS
    while rows % rb or rows // rb > n_steps:
        rb += BF16_ROWS
    return rb


def _front(x2d, nb, sinks, nw, wmt, wqgt, wn, bd, qcol, krow, bias_col, nwm_col, cast_ws, tm):
    n = x2d.shape[0]
    n_tiles = n // tm
    tps = n_tiles // nb
    n_blk = tm // LANES
    w_qvo = M_QK_W + 2 * M_V_W
    cur = lambda w: pl.BlockSpec((tm, w), lambda k: (jnp.minimum(k, n_tiles - 1), 0))
    prev = lambda w: pl.BlockSpec((tm, w), lambda k: (jnp.maximum(k - 1, 0), 0))
    per_seq = lambda r, w: pl.BlockSpec((1, r, w), lambda k: (jnp.maximum(k - 1, 0) // tps, 0, 0))

    def cast_spec(a):
        rb = _cast_rows(a.shape[0], n_tiles)
        return pl.BlockSpec((rb, a.shape[1]), lambda k: (jnp.minimum(k, a.shape[0] // rb - 1), 0))

    cast_in_specs = [cast_spec(a) for a in cast_ws]
    cast_out_specs = [cast_spec(a) for a in cast_ws]
    proj_scratch = [pltpu.VMEM((n_blk, w_qvo, LANES), BF16), pltpu.VMEM((n_blk, 2 * M_HEADS, LANES), F32),
                    pltpu.VMEM((n_blk, A_Q_W, LANES), BF16), pltpu.VMEM((tm, M_QK_W), BF16),
                    pltpu.VMEM((tm, 2 * A_KV_W), F32)]
    return pl.pallas_call(
        functools.partial(_front_kernel, tps, len(cast_ws)),
        grid=(n_tiles + 1,),
        in_specs=[pl.BlockSpec(memory_space=pltpu.SMEM), cur(D_MODEL)]
                 + [_const_spec(a.shape) for a in (nw, wmt, wqgt, wn, bd, qcol, krow, bias_col, nwm_col)]
                 + cast_in_specs,
        out_specs=[cur(2 * D_MODEL), prev(M_V_W), prev(A_Q_W), per_seq(WINDOW, A_KV_W), per_seq(WINDOW, A_KV_W),
                   per_seq(S_ROWS, M_QK_W), per_seq(SUBLANES, LANES)] + cast_out_specs,
        out_shape=[jax.ShapeDtypeStruct((n, 2 * D_MODEL), BF16),
                   jax.ShapeDtypeStruct((n, M_V_W), BF16),
                   jax.ShapeDtypeStruct((n, A_Q_W), BF16),
                   jax.ShapeDtypeStruct((nb, WINDOW, A_KV_W), F32),
                   jax.ShapeDtypeStruct((nb, WINDOW, A_KV_W), F32),
                   jax.ShapeDtypeStruct((nb, S_ROWS, M_QK_W), F32),
                   jax.ShapeDtypeStruct((nb, SUBLANES, LANES), F32)]
                  + [jax.ShapeDtypeStruct(a.shape, BF16) for a in cast_ws],
        scratch_shapes=proj_scratch + proj_scratch + [pltpu.VMEM((S_ROWS, M_QK_W), F32),
                                                       pltpu.VMEM((SUBLANES, LANES), F32),
                                                       pltpu.VMEM((WINDOW, 2 * A_KV_W), F32)],
        compiler_params=_params(("arbitrary",)),
        name="front",
    )(sinks, x2d, nw, wmt, wqgt, wn, bd, qcol, krow, bias_col, nwm_col, *cast_ws)


def _swa_sample_kernel(seq_len, sink_ref, qt_ref, kv_ref, ck_ref, cv_ref, h_ref, kwin_ref, vwin_ref):
    n_seq = GROUP // seq_len
    wb = ck_ref.shape[1]
    n_keys = wb + BF16_ROWS
    lane = lax.broadcasted_iota(jnp.int32, (1, LANES), 1)
    lo_half = lane < HALF
    q_rows = qt_ref[0].astype(F32).T
    k_new = kv_ref[:, 0:A_KV_W].reshape(n_seq, seq_len, A_KV_W)
    v_new = kv_ref[:, A_KV_W:].reshape(n_seq, seq_len, A_KV_W)
    kwin_ref[:, 0:wb - seq_len, :] = ck_ref[:, seq_len:wb, :]
    vwin_ref[:, 0:wb - seq_len, :] = cv_ref[:, seq_len:wb, :]
    kwin_ref[:, wb - seq_len:wb, :] = k_new
    vwin_ref[:, wb - seq_len:wb, :] = v_new

    def to_kv_half(x, head):
        kvh = head // A_GROUPS
        if head % 2 != kvh:
            x = pltpu.roll(x, HALF, axis=1)
        return jnp.where(lo_half if kvh == 0 else ~lo_half, x, 0.0)

    lhs = jnp.concatenate(
        [to_kv_half(q_rows[:, (h // 2) * LANES:(h // 2 + 1) * LANES], h).reshape(n_seq, seq_len, LANES)
         for h in range(A_HEADS)], axis=1).astype(BF16)
    zpad = jnp.zeros((n_seq, BF16_ROWS - seq_len, A_KV_W), F32)
    k_all = jnp.concatenate([ck_ref[...], k_new, zpad], axis=1).astype(BF16)
    v_all = jnp.concatenate([cv_ref[...], v_new, zpad], axis=1).astype(BF16)
    halves = [slice(0, n_seq // 2), slice(n_seq // 2, n_seq)]
    scores = [jnp.einsum('sqf,skf->sqk', lhs[hs], k_all[hs], preferred_element_type=F32) for hs in halves]
    nrow = A_HEADS * seq_len
    ti = lax.broadcasted_iota(jnp.int32, (nrow, n_keys), 0) & (seq_len - 1)
    ki = lax.broadcasted_iota(jnp.int32, (nrow, n_keys), 1)
    mask = ((ki < wb) & (ti + wb - ki < WINDOW)) | ((ki >= wb) & (ki - wb <= ti))
    row_head = lax.broadcasted_iota(jnp.int32, (nrow, 1), 0) >> (seq_len.bit_length() - 1)
    sk = jnp.zeros((nrow, 1), F32)
    for h in range(A_HEADS):
        sk = jnp.where(row_head == h, sink_ref[h], sk)
    outs = []
    for hs, s in zip(halves, scores):
        s = jnp.where(mask, s, -jnp.inf)
        mx = jnp.maximum(jnp.max(s, axis=-1, keepdims=True), sk)
        p = jnp.exp(s - mx)
        den = jnp.sum(p, axis=-1, keepdims=True) + jnp.exp(sk - mx)
        outs.append(jnp.einsum('sqk,skf->sqf', p.astype(BF16), v_all[hs], preferred_element_type=F32) / den)
    o = jnp.concatenate(outs, axis=0)

    def from_kv_half(head):
        x = o[:, head * seq_len:(head + 1) * seq_len, :].reshape(GROUP, LANES)
        return pltpu.roll(x, HALF, axis=1) if head % 2 != head // A_GROUPS else x

    for c in range(A_HEADS // 2):
        h_ref[:, c * LANES:(c + 1) * LANES] = jnp.where(lo_half, from_kv_half(2 * c),
                                                        from_kv_half(2 * c + 1)).astype(h_ref.dtype)


def _swa_sample(qat, kv_a, cache_k, cache_v, sinks, seq_len):
    ngrp = qat.shape[0]
    n_seq = GROUP // seq_len
    wb = cache_k.shape[1]
    row = lambda w: pl.BlockSpec((GROUP, w), lambda i: (i, 0))
    cache = pl.BlockSpec((n_seq, wb, A_KV_W), lambda i: (i, 0, 0))
    return pl.pallas_call(
        functools.partial(_swa_sample_kernel, seq_len),
        grid=(ngrp,),
        in_specs=[pl.BlockSpec(memory_space=pltpu.SMEM), pl.BlockSpec((1, A_Q_W, LANES), lambda i: (i, 0, 0)),
                  row(2 * A_KV_W), cache, cache],
        out_specs=[row(A_Q_W), cache, cache],
        out_shape=[jax.ShapeDtypeStruct((ngrp * GROUP, A_Q_W), BF16),
                   jax.ShapeDtypeStruct(cache_k.shape, F32),
                   jax.ShapeDtypeStruct(cache_v.shape, F32)],
        compiler_params=_params(("arbitrary",)),
        name="swa_sample",
    )(sinks, qat, kv_a, cache_k, cache_v)


def _merge_ffn_kernel(x_ref, hm_ref, ha_ref, gab_ref, wa_ref, wb_ref, wo_ref, nw_ref, wg_ref, wu_ref, wd_ref,
                      y_ref):
    ga = jax.nn.sigmoid(gab_ref[:, 0:D_MODEL].astype(F32))
    gb = jax.nn.sigmoid(gab_ref[:, D_MODEL:].astype(F32))
    mix = ga * _dot(hm_ref[...], wa_ref[...]) + gb * _dot(ha_ref[...], wb_ref[...])
    x1 = x_ref[...] + _dot(mix.astype(BF16), wo_ref[...])
    hf = _rms_rows(x1, nw_ref[...]).astype(BF16)
    gate = _dot(hf, wg_ref[...])
    up = _dot(hf, wu_ref[...])
    act = (jax.nn.silu(gate) * up).astype(BF16)
    y_ref[...] = x1 + _dot(act, wd_ref[...])


def _merge_ffn(x2d, h_m, h_a, g_ab, wa, wb, wo, nw, wg, wu, wd, tm):
    n = x2d.shape[0]
    row = lambda w: pl.BlockSpec((tm, w), lambda i: (i, 0))
    return pl.pallas_call(
        _merge_ffn_kernel,
        grid=(n // tm,),
        in_specs=[row(D_MODEL), row(M_V_W), row(A_Q_W), row(2 * D_MODEL)]
                 + [_const_spec(w.shape) for w in (wa, wb, wo, nw, wg, wu, wd)],
        out_specs=row(D_MODEL),
        out_shape=jax.ShapeDtypeStruct((n, D_MODEL), F32),
        compiler_params=_params(("arbitrary",)),
        name="merge_ffn",
    )(x2d, h_m, h_a, g_ab, wa, wb, wo, nw, wg, wu, wd)


def kernel(x_prompt, x_sample, state_mlstm_C, state_mlstm_n, state_mlstm_m, cache_swa_k, cache_swa_v,
           norm_mix_w, w_in, mlstm_i_bias, mlstm_f_bias, mlstm_norm_w, q_norm_w, k_norm_w, attn_sinks,
           w_branch_a, w_branch_b, w_out, norm_ffn_w, w_gate, w_up, w_down):
    depth = w_in.shape[0]
    assert depth == 1, "single trunk layer"
    l = 0
    bp, tp = x_prompt.shape[0], x_prompt.shape[1]
    bs, ts = x_sample.shape[0], x_sample.shape[1]
    assert tp % 512 == 0 and (bs * ts) % GROUP == 0 and GROUP % ts == 0 and ts & (ts - 1) == 0
    assert ts <= SUBLANES, "sample chunk must fit one sublane tile"

    wt = jnp.transpose(w_in[l])
    c_km, c_vm = M_QK_W, 2 * M_QK_W
    c_g = 2 * M_QK_W + 2 * M_V_W
    c_qa = c_g + 2 * M_HEADS
    c_ka = c_qa + A_Q_W
    gate_pad = jnp.zeros((BF16_ROWS - 2 * M_HEADS, D_MODEL), F32)
    wmt = jnp.concatenate([wt[0:c_km], wt[c_vm:c_g]], axis=0).astype(BF16)
    wqgt = jnp.concatenate([wt[c_qa:c_ka], wt[c_g:c_qa], gate_pad], axis=0).astype(BF16)
    wn = jnp.concatenate([wt[c_km:c_vm], wt[c_ka:]], axis=0).astype(BF16)
    head_of = jnp.arange(A_KV_W) // A_HEAD_DIM
    bd = (head_of[:, None] == head_of[None, :]).astype(BF16)
    qcol = (jnp.tile(q_norm_w[l], A_HEADS) * (A_HEAD_DIM ** -0.5)).reshape(A_Q_W, 1)
    krow = jnp.tile(k_norm_w[l], A_KV_HEADS).reshape(1, A_KV_W)
    nw_mix = norm_mix_w[l].reshape(1, D_MODEL)
    nw_ffn = norm_ffn_w[l].reshape(1, D_MODEL)
    bias_col = jnp.concatenate([mlstm_i_bias[l], mlstm_f_bias[l]]).reshape(2 * M_HEADS, 1)
    nw_col = mlstm_norm_w[l].reshape(M_V_W, 1)
    sinks = attn_sinks[l]
    proj = lambda x2d, tm: _proj(x2d, nw_mix, wmt, wqgt, wn, bd, qcol, krow, tm)

    xp = x_prompt.reshape(bp * tp, D_MODEL)
    later_ws = (w_branch_a[l], w_branch_b[l], w_out[l], w_gate[l], w_up[l], w_down[l])
    g_ab, h_m, h_a, kwin_p, vwin_p, st_p, m_p, wa, wb, wo, wg, wu, wd = _front(
        xp, bp, sinks, nw_mix, wmt, wqgt, wn, bd, qcol, krow, bias_col, nw_col, later_ws, 512)
    merge = lambda x2d, h_m, h_a, g_ab, tm: _merge_ffn(x2d, h_m, h_a, g_ab, wa, wb, wo, nw_ffn, wg, wu, wd, tm)
    yp = merge(xp, h_m, h_a, g_ab, 512).reshape(bp, tp, D_MODEL)
    c_p = jnp.swapaxes(st_p[:, :M_DV, :], 1, 2).reshape(bp, M_HEADS, M_DK, M_DV)
    n_p = st_p[:, M_DV, :].reshape(bp, M_HEADS, M_DK)
    m_pr = m_p[:, :M_HEADS, 0]

    ns = bs * ts
    xs = x_sample.reshape(ns, D_MODEL)
    tms = 512 if ns % 512 == 0 else GROUP
    qvot, gt, qat, k_m, kv_a, g_ab = proj(xs, tms)
    ngrp = ns // GROUP
    m_lanes = jnp.repeat(state_mlstm_m[l], ts, axis=0).reshape(ngrp, GROUP, M_HEADS)
    mrow = jnp.pad(jnp.swapaxes(m_lanes, 1, 2), ((0, 0), (0, SUBLANES - M_HEADS), (0, 0)))
    h_m, c_s, n_s, mt_s = _mlstm_sample(qvot, gt, k_m, mrow, bias_col, nw_col,
                                        state_mlstm_C[l].reshape(bs, M_QK_W, M_DV),
                                        state_mlstm_n[l].reshape(bs, M_QK_W), ts)
    wbuf = cache_swa_k.shape[2]
    h_a, kwin_s, vwin_s = _swa_sample(qat, kv_a, cache_swa_k[l].reshape(bs, wbuf, A_KV_W),
                                      cache_swa_v[l].reshape(bs, wbuf, A_KV_W), sinks, ts)
    ys = merge(xs, h_m, h_a, g_ab, tms).reshape(bs, ts, D_MODEL)
    m_s = jnp.swapaxes(mt_s[:, :M_HEADS, :], 1, 2).reshape(bs, ts, M_HEADS)[:, ts - 1, :]

    kv5 = lambda a: a.reshape(a.shape[0], a.shape[1], A_KV_HEADS, A_HEAD_DIM)[None]
    return (yp, ys,
            c_p[None], n_p[None], m_pr[None], kv5(kwin_p), kv5(vwin_p),
            c_s.reshape(bs, M_HEADS, M_DK, M_DV)[None], n_s.reshape(bs, M_HEADS, M_DK)[None], m_s[None],
            kv5(kwin_s), kv5(vwin_s))
```

```python
import functools

import jax
import jax.numpy as jnp
from jax import lax
from jax.experimental import pallas as pl
from jax.experimental.pallas import tpu as pltpu

F32 = jnp.float32
BF16 = jnp.bfloat16

D_MODEL = 1024
M_HEADS = 4
M_DK = 64
M_DV = 128
M_CHUNK = 64
M_QK_W = M_HEADS * M_DK
M_V_W = M_HEADS * M_DV
A_HEADS = 8
A_KV_HEADS = 2
A_HEAD_DIM = 64
A_GROUPS = A_HEADS // A_KV_HEADS
A_Q_W = A_HEADS * A_HEAD_DIM
A_KV_W = A_KV_HEADS * A_HEAD_DIM
WINDOW = 128
D_FF = 2816
EPS = 1e-6

LANES = 128
SUBLANES = 8
BF16_ROWS = 16
GROUP = 128
S_ROWS = M_DV + BF16_ROWS
VMEM_LIMIT = 56 * 1024 * 1024
DK_SHIFT = M_DK.bit_length() - 1
WINDOW_SHIFT = WINDOW.bit_length() - 1
HALF = LANES // 2
assert A_HEAD_DIM == HALF and A_KV_W == LANES, "attention head pairs share one lane-width"

NT_DIMS = (((1,), (1,)), ((), ()))


def _dot(a, b):
    return jnp.dot(a, b, preferred_element_type=F32)


def _dot_nt(a, b):
    return lax.dot_general(a, b, NT_DIMS, preferred_element_type=F32)


def _const_spec(shape):
    nd = len(shape)
    return pl.BlockSpec(shape, lambda *_: (0,) * nd, pipeline_mode=pl.Buffered(1))


def _params(sem):
    return pltpu.CompilerParams(dimension_semantics=sem, vmem_limit_bytes=VMEM_LIMIT)


def _rms_rows(x, nw):
    ms = jnp.mean(x * x, axis=-1, keepdims=True)
    return (x * lax.rsqrt(ms + EPS)) * nw


GAB_CHUNK = 512


def _proj_chunks(x_ref, nw_ref, wmt_ref, wqgt_ref, wn_ref, bd_ref, qcol_ref, krow_ref,
                 qvot_ref, gt_ref, qat_ref, km_ref, kva_ref, gab_ref):
    hn = _rms_rows(x_ref[...], nw_ref[...]).astype(BF16)
    n_blk = qat_ref.shape[0]

    def put(ref, rows, val):
        for c in range(n_blk):
            ref[c, rows, :] = val[:, c * LANES:(c + 1) * LANES].astype(ref.dtype)

    def mlstm_qvo():
        put(qvot_ref, slice(None), _dot_nt(wmt_ref[...], hn))

    def attn_q_and_gates():
        qt = _dot_nt(wqgt_ref[...], hn)
        put(gt_ref, slice(None), qt[A_Q_W:A_Q_W + 2 * M_HEADS])
        for h in range(A_HEADS):
            hs = slice(h * A_HEAD_DIM, (h + 1) * A_HEAD_DIM)
            blk = qt[hs]
            ssq_q = jnp.sum(blk * blk, axis=0, keepdims=True)
            put(qat_ref, hs, (blk * lax.rsqrt(ssq_q * (1.0 / A_HEAD_DIM) + EPS)) * qcol_ref[hs])

    def mlstm_k():
        km_ref[...] = _dot_nt(hn, wn_ref[0:M_QK_W, :]).astype(km_ref.dtype)

    def branch_gates(c0):
        def run():
            w0 = M_QK_W + 2 * A_KV_W + c0
            gab_ref[:, c0:c0 + GAB_CHUNK] = _dot_nt(hn, wn_ref[w0:w0 + GAB_CHUNK, :]).astype(gab_ref.dtype)
        return run

    def attn_kv():
        kv = _dot_nt(hn, wn_ref[M_QK_W:M_QK_W + 2 * A_KV_W, :])
        k = kv[:, 0:A_KV_W]
        ksq = k * k
        hi = ksq.astype(BF16)
        lo = (ksq - hi.astype(F32)).astype(BF16)
        ssq = _dot(hi, bd_ref[...]) + _dot(lo, bd_ref[...])
        kva_ref[:, 0:A_KV_W] = (k * lax.rsqrt(ssq * (1.0 / A_HEAD_DIM) + EPS)) * krow_ref[...]
        kva_ref[:, A_KV_W:] = kv[:, A_KV_W:]

    return ([mlstm_qvo, attn_q_and_gates, mlstm_k]
            + [branch_gates(c0) for c0 in range(0, gab_ref.shape[1], GAB_CHUNK)] + [attn_kv])


def _proj_kernel(*refs):
    for piece in _proj_chunks(*refs):
        piece()


def _proj(x2d, nw, wmt, wqgt, wn, bd, qcol, krow, tm):
    n = x2d.shape[0]
    row = lambda w: pl.BlockSpec((tm, w), lambda i: (i, 0))
    n_blk = tm // LANES
    slab = lambda r: pl.BlockSpec((n_blk, r, LANES), lambda i: (i, 0, 0))
    w_qvo = M_QK_W + 2 * M_V_W
    return pl.pallas_call(
        _proj_kernel,
        grid=(n // tm,),
        in_specs=[row(D_MODEL)] + [_const_spec(a.shape) for a in (nw, wmt, wqgt, wn, bd, qcol, krow)],
        out_specs=[slab(w_qvo), slab(2 * M_HEADS), slab(A_Q_W), row(M_QK_W), row(2 * A_KV_W), row(2 * D_MODEL)],
        out_shape=[jax.ShapeDtypeStruct((n // LANES, w_qvo, LANES), BF16),
                   jax.ShapeDtypeStruct((n // LANES, 2 * M_HEADS, LANES), F32),
                   jax.ShapeDtypeStruct((n // LANES, A_Q_W, LANES), BF16),
                   jax.ShapeDtypeStruct((n, M_QK_W), BF16),
                   jax.ShapeDtypeStruct((n, 2 * A_KV_W), F32),
                   jax.ShapeDtypeStruct((n, 2 * D_MODEL), BF16)],
        compiler_params=_params(("arbitrary",)),
        name="proj",
    )(x2d, nw, wmt, wqgt, wn, bd, qcol, krow)


def _split3_rows(x):
    hi = x.astype(BF16).astype(F32)
    r1 = x - hi
    mid = r1.astype(BF16).astype(F32)
    lo = r1 - mid
    return jnp.concatenate([hi, mid, lo], axis=0).astype(BF16)


def _log_sigmoid(x):
    return jnp.minimum(x, 0.0) - jnp.log1p(jnp.exp(-jnp.abs(x)))


def _chunk_masks(chunk_shift):
    s = lax.broadcasted_iota(jnp.int32, (GROUP, GROUP), 0)
    t = lax.broadcasted_iota(jnp.int32, (GROUP, GROUP), 1)
    same = (s >> chunk_shift) == (t >> chunk_shift)
    return same, same & (s <= t)


def _group_gates(gt, same, causal):
    cm_bf = jnp.where(causal, 1.0, 0.0).astype(BF16)
    lf = _log_sigmoid(gt)
    nr = gt.shape[0]
    bt3 = _dot(_split3_rows(lf), cm_bf)
    bt = (bt3[0:nr] + bt3[nr:2 * nr]) + bt3[2 * nr:3 * nr]
    b4 = bt[M_HEADS:2 * M_HEADS]
    a4 = gt[0:M_HEADS] - b4
    a8 = jnp.concatenate([a4, a4], axis=0)
    a_cols = jnp.concatenate([a8, jnp.zeros((GROUP - SUBLANES, GROUP), F32)], axis=0).T
    at_mats, run_rows, chunk_rows = [], [], []
    for h in range(M_HEADS):
        at = jnp.broadcast_to(a_cols[:, h:h + 1], (GROUP, GROUP))
        at_mats.append(at)
        run_rows.append(jnp.max(jnp.where(causal, at, -jnp.inf), axis=0, keepdims=True))
        chunk_rows.append(jnp.max(jnp.where(same, at, -jnp.inf), axis=0, keepdims=True))
    return b4, a4, at_mats, jnp.concatenate(run_rows, axis=0), jnp.concatenate(chunk_rows, axis=0)


def _group_weights(m_prev, b4, a4, run4, chunk4):
    big_m = jnp.maximum(m_prev, run4)
    m_last = jnp.maximum(m_prev, chunk4)
    w_inter = jnp.exp(m_prev - big_m)
    g_vec = jnp.exp(m_prev - m_last)
    w_last = jnp.exp(a4 - m_last)
    m_t = b4 + big_m
    return big_m, w_inter, g_vec, w_last, m_t, jnp.exp(-m_t)


def _group_scores(qvot, ks):
    lane_head = lax.broadcasted_iota(jnp.int32, (1, M_QK_W), 1) >> DK_SHIFT
    row_head = lax.broadcasted_iota(jnp.int32, (M_QK_W, 1), 0) >> DK_SHIFT
    qt = qvot(0, M_QK_W)
    k_stack = jnp.concatenate([jnp.where(lane_head == h, ks, jnp.zeros_like(ks)) for h in range(M_HEADS)], axis=0)
    qw = jnp.concatenate([jnp.where(row_head == h, qt, jnp.zeros_like(qt)) for h in range(M_HEADS)], axis=1)
    zero_blk = jnp.zeros((M_DK, GROUP), BF16)
    sc_t = []
    for h in range(0, M_HEADS, 2):
        q_pair = jnp.concatenate(
            [jnp.concatenate([qvot(h * M_DK, (h + 1) * M_DK), zero_blk], axis=1),
             jnp.concatenate([zero_blk, qvot((h + 1) * M_DK, (h + 2) * M_DK)], axis=1)], axis=0)
        sc = _dot(ks[:, h * M_DK:(h + 2) * M_DK], q_pair)
        sc_t += [sc[:, 0:GROUP], sc[:, GROUP:]]
    return k_stack, qw, sc_t


def _group_values(qvot, sc_t, at_mats, big_m, causal):
    ones_rows = jnp.where(lax.broadcasted_iota(jnp.int32, (BF16_ROWS, GROUP), 0) == 0, 1.0, 0.0).astype(BF16)
    zero_blk = jnp.zeros((GROUP, GROUP), BF16)
    vta, s_t = [], []
    for h in range(M_HEADS):
        w_t = jnp.where(causal, jnp.exp(at_mats[h] - big_m[h:h + 1]), 0.0)
        s_t.append((sc_t[h] * w_t).astype(BF16))
        vta.append(jnp.concatenate([qvot(M_QK_W + h * M_DV, M_QK_W + (h + 1) * M_DV), ones_rows], axis=0))
    intra = []
    for h in range(0, M_HEADS, 2):
        pair = jnp.concatenate([jnp.concatenate([s_t[h], zero_blk], axis=1),
                                jnp.concatenate([zero_blk, s_t[h + 1]], axis=1)], axis=0)
        intra.append(_dot(jnp.concatenate(vta[h:h + 2], axis=1), pair))
    return vta, jnp.concatenate(intra, axis=1)


def _weighted_values(vta, w_rows):
    return jnp.concatenate([(vta[h].astype(F32) * w_rows[h:h + 1]).astype(BF16) for h in range(M_HEADS)], axis=1)


def _lanes_x(rows):
    return jnp.concatenate([rows[h:h + 1] for h in range(M_HEADS)], axis=1)


def _group_out(qvot, inter, intra, w_inter, e_neg_m, nw_ref):
    outs = []
    out_all = inter * _lanes_x(w_inter) + intra
    for h in range(M_HEADS):
        out_t = out_all[:, h * GROUP:(h + 1) * GROUP]
        hh = out_t[0:M_DV] / jnp.maximum(jnp.abs(out_t[M_DV:M_DV + 1]), e_neg_m[h:h + 1])
        ms = jnp.mean(hh * hh, axis=0, keepdims=True)
        hn = (hh * lax.rsqrt(ms + EPS)) * nw_ref[h * M_DV:(h + 1) * M_DV]
        o_t = qvot(M_QK_W + M_V_W + h * M_DV, M_QK_W + M_V_W + (h + 1) * M_DV)
        outs.append(hn * jax.nn.sigmoid(o_t.astype(F32)))
    return jnp.concatenate(outs, axis=0).T


def _decay_row(g_vec, lane0):
    lane_head = lax.broadcasted_iota(jnp.int32, (1, M_QK_W), 1) >> DK_SHIFT
    g_row = jnp.zeros((1, M_QK_W), F32)
    for h in range(M_HEADS):
        g_row = jnp.where(lane_head == h, g_vec[h:h + 1, lane0:lane0 + 1], g_row)
    return g_row


def _mlstm_sample_kernel(seq_len, qvot_ref, gt_ref, k_ref, mrow_ref, bias_ref, nw_ref, c_ref, n_ref,
                         h_ref, c_out_ref, n_out_ref, mt_ref):
    n_seq = GROUP // seq_len
    shift = seq_len.bit_length() - 1
    same, causal = _chunk_masks(shift)
    lane_seq = lax.broadcasted_iota(jnp.int32, (1, GROUP), 1) >> shift
    lane_seq_x = jnp.concatenate([lane_seq] * M_HEADS, axis=1)
    row0 = lax.broadcasted_iota(jnp.int32, (BF16_ROWS, M_QK_W), 0) == 0
    qvot = lambda r0, r1: qvot_ref[0, r0:r1, :]
    ks = k_ref[...] * (M_DK ** -0.5)
    b4, a4, at_mats, run4, chunk4 = _group_gates(gt_ref[0] + bias_ref[...], same, causal)
    big_m, w_inter, g_vec, w_last, m_t, e_neg_m = _group_weights(mrow_ref[0, 0:M_HEADS, :], b4, a4, run4, chunk4)
    mt_ref[0] = jnp.concatenate([m_t, m_t], axis=0)
    k_stack, qw, sc_t = _group_scores(qvot, ks)

    st_old = []
    for s in range(n_seq):
        n_rows = jnp.where(row0, jnp.broadcast_to(n_ref[s:s + 1, :], (BF16_ROWS, M_QK_W)), 0.0)
        st_old.append(jnp.concatenate([c_ref[s].T, n_rows], axis=0))
    inter_all = _dot(jnp.concatenate(st_old, axis=0).astype(BF16), qw)
    vta, intra = _group_values(qvot, sc_t, at_mats, big_m, causal)
    inter = inter_all[0:S_ROWS]
    for s in range(1, n_seq):
        inter = jnp.where(lane_seq_x == s, inter_all[s * S_ROWS:(s + 1) * S_ROWS], inter)
    tall = jnp.concatenate([_weighted_values(vta, jnp.where(lane_seq == s, w_last, 0.0)) for s in range(n_seq)],
                           axis=0)
    d_st = _dot(tall, k_stack)
    for s in range(n_seq):
        st_new = _decay_row(g_vec, s * seq_len) * st_old[s] + d_st[s * S_ROWS:(s + 1) * S_ROWS]
        c_out_ref[s] = st_new[0:M_DV].T
        n_out_ref[s:s + 1, :] = st_new[M_DV:M_DV + 1]
    h_ref[...] = _group_out(qvot, inter, intra, w_inter, e_neg_m, nw_ref).astype(h_ref.dtype)


def _mlstm_sample(qvot, gt, k_m, mrow, bias_col, nw_col, c, n, seq_len):
    ngrp = qvot.shape[0]
    n_seq = GROUP // seq_len
    full = lambda a: pl.BlockSpec(a.shape, lambda i: (0,) * a.ndim)
    slab = lambda a: pl.BlockSpec((1,) + a.shape[1:], lambda i: (i, 0, 0))
    row = lambda w: pl.BlockSpec((GROUP, w), lambda i: (i, 0))
    c_spec = pl.BlockSpec((n_seq,) + c.shape[1:], lambda i: (i, 0, 0))
    n_spec = pl.BlockSpec((n_seq, n.shape[1]), lambda i: (i, 0))
    return pl.pallas_call(
        functools.partial(_mlstm_sample_kernel, seq_len),
        grid=(ngrp,),
        in_specs=[slab(qvot), slab(gt), row(M_QK_W), slab(mrow), full(bias_col), full(nw_col), c_spec, n_spec],
        out_specs=[row(M_V_W), c_spec, n_spec, slab(mrow)],
        out_shape=[jax.ShapeDtypeStruct((ngrp * GROUP, M_V_W), BF16),
                   jax.ShapeDtypeStruct(c.shape, F32),
                   jax.ShapeDtypeStruct(n.shape, F32),
                   jax.ShapeDtypeStruct(mrow.shape, F32)],
        compiler_params=_params(("arbitrary",)),
        name="mlstm_sample",
    )(qvot, gt, k_m, mrow, bias_col, nw_col, c, n)


def _swa_scores(qbs, has_prev, sink_ref, qt_ref, kv_ref, kvp_ref):
    hd = A_HEAD_DIM
    nq = A_GROUPS * WINDOW
    si = lax.broadcasted_iota(jnp.int32, (2 * WINDOW, nq), 0)
    qi = lax.broadcasted_iota(jnp.int32, (2 * WINDOW, nq), 1) & (WINDOW - 1)
    local = ((si < WINDOW) & (si > qi)) | ((si >= WINDOW) & (si - WINDOW <= qi))
    first = local & (has_prev | (si >= WINDOW))
    lane_grp = lax.broadcasted_iota(jnp.int32, (1, nq), 1) >> WINDOW_SHIFT
    kv_block = lambda i: kvp_ref[...] if i == 0 else kv_ref[(i - 1) * WINDOW:i * WINDOW, :]
    k_bf = lambda i: kv_block(i)[:, 0:A_KV_W].astype(BF16)
    vt_bf = lambda i: kv_block(i)[:, A_KV_W:].T.astype(BF16)
    zeros = jnp.zeros((hd, nq), BF16)
    sinks = []
    for kvh in range(A_KV_HEADS):
        sk = jnp.zeros((1, nq), F32)
        for g in range(A_GROUPS):
            sk = jnp.where(lane_grp == g, sink_ref[kvh * A_GROUPS + g], sk)
        sinks.append(sk)
    units = []
    for qb in qbs:
        kk = jnp.concatenate([k_bf(qb), k_bf(qb + 1)], axis=0)
        vt = jnp.concatenate([vt_bf(qb), vt_bf(qb + 1)], axis=1)
        mask = local if qb > 0 else first
        for kvh in range(A_KV_HEADS):
            q4t = jnp.concatenate(
                [qt_ref[qb, (kvh * A_GROUPS + g) * hd:(kvh * A_GROUPS + g + 1) * hd, :] for g in range(A_GROUPS)],
                axis=1)
            wq = jnp.concatenate([q4t, zeros] if kvh == 0 else [zeros, q4t], axis=0)
            units.append((jnp.where(mask, _dot(kk, wq), -jnp.inf), sinks[kvh], vt))
    return units


def _swa_finish(qbs, units, h_ref):
    hd = A_HEAD_DIM
    for i, qb in enumerate(qbs):
        pieces = []
        for kvh in range(A_KV_HEADS):
            s, sk, vt = units[i * A_KV_HEADS + kvh]
            mx = jnp.maximum(jnp.max(s, axis=0, keepdims=True), sk)
            p = jnp.exp(s - mx)
            den = jnp.sum(p, axis=0, keepdims=True) + jnp.exp(sk - mx)
            ot = _dot(vt[kvh * hd:(kvh + 1) * hd], p.astype(BF16)) / den
            pieces += [ot[:, g * WINDOW:(g + 1) * WINDOW] for g in range(A_GROUPS)]
        h_t = jnp.concatenate(pieces, axis=0)
        h_ref[qb * WINDOW:(qb + 1) * WINDOW, :] = h_t.T.astype(h_ref.dtype)


def _front_kernel(tiles_per_seq, n_cast, *refs):
    (sink_ref, x_ref, nw_ref, wmt_ref, wqgt_ref, wn_ref, bd_ref, qcol_ref, krow_ref, bias_ref, nwm_ref) = refs[:11]
    cast_in = refs[11:11 + n_cast]
    (gab_ref, hm_ref, ha_ref, kwin_ref, vwin_ref, st_ref, m_ref) = refs[11 + n_cast:18 + n_cast]
    cast_out = refs[18 + n_cast:18 + 2 * n_cast]
    (q_qvot, q_gt, q_qat, q_km, q_kva, p_qvot, p_gt, p_qat, p_km, p_kva, st_s, m_s, kvp_s) = refs[18 + 2 * n_cast:]
    _front_body(tiles_per_seq, sink_ref, x_ref, nw_ref, wmt_ref, wqgt_ref, wn_ref, bd_ref, qcol_ref, krow_ref,
                bias_ref, nwm_ref, gab_ref, hm_ref, ha_ref, kwin_ref, vwin_ref, st_ref, m_ref,
                q_qvot, q_gt, q_qat, q_km, q_kva, p_qvot, p_gt, p_qat, p_km, p_kva, st_s, m_s, kvp_s)
    for src, dst in zip(cast_in, cast_out):
        dst[...] = src[...].astype(dst.dtype)


def _front_body(tiles_per_seq, sink_ref, x_ref, nw_ref, wmt_ref, wqgt_ref, wn_ref, bd_ref, qcol_ref, krow_ref,
                bias_ref, nwm_ref,
                gab_ref, hm_ref, ha_ref, kwin_ref, vwin_ref, st_ref, m_ref,
                q_qvot, q_gt, q_qat, q_km, q_kva, p_qvot, p_gt, p_qat, p_km, p_kva, st_s, m_s, kvp_s):
    k = pl.program_id(0)

    @pl.when(k == 0)
    def _():
        for r in (p_qvot, p_gt, p_qat, p_km, p_kva, st_s, m_s, kvp_s):
            r[...] = jnp.zeros(r.shape, r.dtype)

    pieces = _proj_chunks(x_ref, nw_ref, wmt_ref, wqgt_ref, wn_ref, bd_ref, qcol_ref, krow_ref,
                          q_qvot, q_gt, q_qat, q_km, q_kva, gab_ref)
    n_piece = len(pieces)

    seq_start = lax.rem(k - 1 + tiles_per_seq, tiles_per_seq) == 0
    n_blk = p_qat.shape[0]
    same, causal = _chunk_masks(GROUP.bit_length() - 1)
    st = jnp.where(seq_start, 0.0, st_s[...])
    m_col = jnp.where(seq_start, 0.0, m_s[0:M_HEADS, 0:1])

    grp = []
    for g in range(n_blk):
        qvot = lambda r0, r1, g=g: p_qvot[g, r0:r1, :]
        ks = p_km[g * GROUP:(g + 1) * GROUP, :] * (M_DK ** -0.5)
        b4, a4, at_mats, run4, chunk4 = _group_gates(p_gt[g] + bias_ref[...], same, causal)
        weights = _group_weights(jnp.broadcast_to(m_col, (M_HEADS, GROUP)), b4, a4, run4, chunk4)
        m_col = weights[4][:, GROUP - 1:GROUP]
        grp.append((qvot, at_mats, weights) + _group_scores(qvot, ks))
    has_prev = jnp.logical_not(seq_start)
    qb_lo, qb_hi = list(range(0, n_blk // 2)), list(range(n_blk // 2, n_blk))
    units_lo = _swa_scores(qb_lo, has_prev, sink_ref, p_qat, p_kva, kvp_s)
    inter = [_dot(st.astype(BF16), grp[0][4])]
    for piece in pieces[0:1]:
        piece()

    vals = []
    for qvot, at_mats, (big_m, _, _, w_last, _, _), k_stack, _, sc_t in grp:
        vta, intra = _group_values(qvot, sc_t, at_mats, big_m, causal)
        vals.append((intra, _dot(_weighted_values(vta, w_last), k_stack)))
    _swa_finish(qb_lo, units_lo, ha_ref)
    units_hi = _swa_scores(qb_hi, has_prev, sink_ref, p_qat, p_kva, kvp_s)
    for piece in pieces[1:3]:
        piece()

    for g in range(n_blk):
        st = _decay_row(grp[g][2][2], 0) * st + vals[g][1]
        if g + 1 < n_blk:
            inter.append(_dot(st.astype(BF16), grp[g + 1][4]))
    _swa_finish(qb_hi, units_hi, ha_ref)
    for piece in pieces[3:]:
        piece()

    for g in range(n_blk):
        qvot, _, (_, w_inter, _, _, _, e_neg_m) = grp[g][0:3]
        hm_ref[g * GROUP:(g + 1) * GROUP, :] = _group_out(qvot, inter[g], vals[g][0], w_inter, e_neg_m,
                                                          nwm_ref).astype(hm_ref.dtype)
    st_s[...] = st
    st_ref[0] = st
    m_rows = jnp.broadcast_to(m_col, (M_HEADS, LANES))
    m_rows = jnp.concatenate([m_rows, m_rows], axis=0)
    m_s[...] = m_rows
    m_ref[0] = m_rows
    tm = p_kva.shape[0]
    kwin_ref[0] = p_kva[tm - WINDOW:, 0:A_KV_W]
    vwin_ref[0] = p_kva[tm - WINDOW:, A_KV_W:]
    kvp_s[...] = p_kva[tm - WINDOW:, :]

    for p, q in ((p_qvot, q_qvot), (p_gt, q_gt), (p_qat, q_qat), (p_km, q_km), (p_kva, q_kva)):
        p[...] = q[...]


def _cast_rows(rows, n_steps):
    rb = BF16_ROWS
    while rows % rb or rows // rb > n_steps:
        rb += BF16_ROWS
    return rb


def _front(x2d, nb, sinks, nw, wmt, wqgt, wn, bd, qcol, krow, bias_col, nwm_col, cast_ws, tm):
    n = x2d.shape[0]
    n_tiles = n // tm
    tps = n_tiles // nb
    n_blk = tm // LANES
    w_qvo = M_QK_W + 2 * M_V_W
    cur = lambda w: pl.BlockSpec((tm, w), lambda k: (jnp.minimum(k, n_tiles - 1), 0))
    prev = lambda w: pl.BlockSpec((tm, w), lambda k: (jnp.maximum(k - 1, 0), 0))
    per_seq = lambda r, w: pl.BlockSpec((1, r, w), lambda k: (jnp.maximum(k - 1, 0) // tps, 0, 0))

    def cast_spec(a):
        rb = _cast_rows(a.shape[0], n_tiles)
        return pl.BlockSpec((rb, a.shape[1]), lambda k: (jnp.minimum(k, a.shape[0] // rb - 1), 0))

    cast_in_specs = [cast_spec(a) for a in cast_ws]
    cast_out_specs = [cast_spec(a) for a in cast_ws]
    proj_scratch = [pltpu.VMEM((n_blk, w_qvo, LANES), BF16), pltpu.VMEM((n_blk, 2 * M_HEADS, LANES), F32),
                    pltpu.VMEM((n_blk, A_Q_W, LANES), BF16), pltpu.VMEM((tm, M_QK_W), BF16),
                    pltpu.VMEM((tm, 2 * A_KV_W), F32)]
    return pl.pallas_call(
        functools.partial(_front_kernel, tps, len(cast_ws)),
        grid=(n_tiles + 1,),
        in_specs=[pl.BlockSpec(memory_space=pltpu.SMEM), cur(D_MODEL)]
                 + [_const_spec(a.shape) for a in (nw, wmt, wqgt, wn, bd, qcol, krow, bias_col, nwm_col)]
                 + cast_in_specs,
        out_specs=[cur(2 * D_MODEL), prev(M_V_W), prev(A_Q_W), per_seq(WINDOW, A_KV_W), per_seq(WINDOW, A_KV_W),
                   per_seq(S_ROWS, M_QK_W), per_seq(SUBLANES, LANES)] + cast_out_specs,
        out_shape=[jax.ShapeDtypeStruct((n, 2 * D_MODEL), BF16),
                   jax.ShapeDtypeStruct((n, M_V_W), BF16),
                   jax.ShapeDtypeStruct((n, A_Q_W), BF16),
                   jax.ShapeDtypeStruct((nb, WINDOW, A_KV_W), F32),
                   jax.ShapeDtypeStruct((nb, WINDOW, A_KV_W), F32),
                   jax.ShapeDtypeStruct((nb, S_ROWS, M_QK_W), F32),
                   jax.ShapeDtypeStruct((nb, SUBLANES, LANES), F32)]
                  + [jax.ShapeDtypeStruct(a.shape, BF16) for a in cast_ws],
        scratch_shapes=proj_scratch + proj_scratch + [pltpu.VMEM((S_ROWS, M_QK_W), F32),
                                                       pltpu.VMEM((SUBLANES, LANES), F32),
                                                       pltpu.VMEM((WINDOW, 2 * A_KV_W), F32)],
        compiler_params=_params(("arbitrary",)),
        name="front",
    )(sinks, x2d, nw, wmt, wqgt, wn, bd, qcol, krow, bias_col, nwm_col, *cast_ws)


def _swa_sample_kernel(seq_len, sink_ref, qt_ref, kv_ref, ck_ref, cv_ref, h_ref, kwin_ref, vwin_ref):
    n_seq = GROUP // seq_len
    wb = ck_ref.shape[2]
    lane = lax.broadcasted_iota(jnp.int32, (1, LANES), 1)
    lo_half = lane < HALF
    q_rows = qt_ref[0].astype(F32).T
    kv_new = kv_ref[...]
    k_new = kv_new[:, 0:A_KV_W].reshape(n_seq, seq_len, A_KV_W)
    v_new = kv_new[:, A_KV_W:].reshape(n_seq, seq_len, A_KV_W)
    kt_new, vt_new = kv_new[:, 0:A_KV_W].T, kv_new[:, A_KV_W:].T
    for s in range(n_seq):
        put = (wb - seq_len - s * seq_len) % LANES
        for new_t, c_ref, win_ref in ((kt_new, ck_ref, kwin_ref), (vt_new, cv_ref, vwin_ref)):
            win_ref[s] = jnp.where(lane >= wb - seq_len, pltpu.roll(new_t, put, axis=1) if put else new_t,
                                   pltpu.roll(c_ref[s], wb - seq_len, axis=1))

    def to_kv_half(x, head):
        kvh = head // A_GROUPS
        if head % 2 != kvh:
            x = pltpu.roll(x, HALF, axis=1)
        return jnp.where(lo_half if kvh == 0 else ~lo_half, x, 0.0)

    lhs = jnp.concatenate(
        [to_kv_half(q_rows[:, (h // 2) * LANES:(h // 2 + 1) * LANES], h).reshape(n_seq, seq_len, LANES)
         for h in range(A_HEADS)], axis=1).astype(BF16)
    zpad = jnp.zeros((n_seq, BF16_ROWS - seq_len, A_KV_W), F32)
    k_nb = jnp.concatenate([k_new, zpad], axis=1).astype(BF16)
    v_nb = jnp.concatenate([v_new, zpad], axis=1).astype(BF16)
    s_c = jnp.einsum('sqf,sfk->sqk', lhs, ck_ref[...].astype(BF16), preferred_element_type=F32)
    s_n = jnp.einsum('sqf,skf->sqk', lhs, k_nb, preferred_element_type=F32)
    nrow = A_HEADS * seq_len
    ti = lax.broadcasted_iota(jnp.int32, (nrow, wb), 0) & (seq_len - 1)
    ki = lax.broadcasted_iota(jnp.int32, (nrow, wb), 1)
    mask_c = (ti + wb - ki) < WINDOW
    ti_n = lax.broadcasted_iota(jnp.int32, (nrow, BF16_ROWS), 0) & (seq_len - 1)
    ki_n = lax.broadcasted_iota(jnp.int32, (nrow, BF16_ROWS), 1)
    mask_n = ki_n <= ti_n
    row_head = lax.broadcasted_iota(jnp.int32, (nrow, 1), 0) >> (seq_len.bit_length() - 1)
    sk = jnp.zeros((nrow, 1), F32)
    for h in range(A_HEADS):
        sk = jnp.where(row_head == h, sink_ref[h], sk)
    s_c = jnp.where(mask_c, s_c, -jnp.inf)
    s_n = jnp.where(mask_n, s_n, -jnp.inf)
    mx = jnp.maximum(jnp.maximum(jnp.max(s_c, axis=-1, keepdims=True), jnp.max(s_n, axis=-1, keepdims=True)), sk)
    p_c = jnp.exp(s_c - mx)
    p_n = jnp.exp(s_n - mx)
    den = jnp.sum(p_c, axis=-1, keepdims=True) + jnp.sum(p_n, axis=-1, keepdims=True) + jnp.exp(sk - mx)
    o = (jnp.einsum('sqk,sfk->sqf', p_c.astype(BF16), cv_ref[...].astype(BF16), preferred_element_type=F32)
         + jnp.einsum('sqk,skf->sqf', p_n.astype(BF16), v_nb, preferred_element_type=F32)) / den

    def from_kv_half(head):
        x = o[:, head * seq_len:(head + 1) * seq_len, :].reshape(GROUP, LANES)
        return pltpu.roll(x, HALF, axis=1) if head % 2 != head // A_GROUPS else x

    for c in range(A_HEADS // 2):
        h_ref[:, c * LANES:(c + 1) * LANES] = jnp.where(lo_half, from_kv_half(2 * c),
                                                        from_kv_half(2 * c + 1)).astype(h_ref.dtype)


def _swa_sample(qat, kv_a, cache_k, cache_v, sinks, seq_len):
    ngrp = qat.shape[0]
    n_seq = GROUP // seq_len
    wb = cache_k.shape[2]
    assert wb == LANES, "window positions fill one lane-width"
    row = lambda w: pl.BlockSpec((GROUP, w), lambda i: (i, 0))
    cache = pl.BlockSpec((n_seq, A_KV_W, wb), lambda i: (i, 0, 0))
    return pl.pallas_call(
        functools.partial(_swa_sample_kernel, seq_len),
        grid=(ngrp,),
        in_specs=[pl.BlockSpec(memory_space=pltpu.SMEM), pl.BlockSpec((1, A_Q_W, LANES), lambda i: (i, 0, 0)),
                  row(2 * A_KV_W), cache, cache],
        out_specs=[row(A_Q_W), cache, cache],
        out_shape=[jax.ShapeDtypeStruct((ngrp * GROUP, A_Q_W), BF16),
                   jax.ShapeDtypeStruct(cache_k.shape, F32),
                   jax.ShapeDtypeStruct(cache_v.shape, F32)],
        compiler_params=_params(("arbitrary",)),
        name="swa_sample",
    )(sinks, qat, kv_a, cache_k, cache_v)


def _merge_ffn_kernel(x_ref, hm_ref, ha_ref, gab_ref, wa_ref, wb_ref, wo_ref, nw_ref, wg_ref, wu_ref, wd_ref,
                      y_ref):
    ga = jax.nn.sigmoid(gab_ref[:, 0:D_MODEL].astype(F32))
    gb = jax.nn.sigmoid(gab_ref[:, D_MODEL:].astype(F32))
    mix = ga * _dot(hm_ref[...], wa_ref[...]) + gb * _dot(ha_ref[...], wb_ref[...])
    x1 = x_ref[...] + _dot(mix.astype(BF16), wo_ref[...])
    hf = _rms_rows(x1, nw_ref[...]).astype(BF16)
    gate = _dot(hf, wg_ref[...])
    up = _dot(hf, wu_ref[...])
    act = (jax.nn.silu(gate) * up).astype(BF16)
    y_ref[...] = x1 + _dot(act, wd_ref[...])


def _merge_ffn(x2d, h_m, h_a, g_ab, wa, wb, wo, nw, wg, wu, wd, tm):
    n = x2d.shape[0]
    row = lambda w: pl.BlockSpec((tm, w), lambda i: (i, 0))
    return pl.pallas_call(
        _merge_ffn_kernel,
        grid=(n // tm,),
        in_specs=[row(D_MODEL), row(M_V_W), row(A_Q_W), row(2 * D_MODEL)]
                 + [_const_spec(w.shape) for w in (wa, wb, wo, nw, wg, wu, wd)],
        out_specs=row(D_MODEL),
        out_shape=jax.ShapeDtypeStruct((n, D_MODEL), F32),
        compiler_params=_params(("arbitrary",)),
        name="merge_ffn",
    )(x2d, h_m, h_a, g_ab, wa, wb, wo, nw, wg, wu, wd)


def kernel(x_prompt, x_sample, state_mlstm_C, state_mlstm_n, state_mlstm_m, cache_swa_k, cache_swa_v,
           norm_mix_w, w_in, mlstm_i_bias, mlstm_f_bias, mlstm_norm_w, q_norm_w, k_norm_w, attn_sinks,
           w_branch_a, w_branch_b, w_out, norm_ffn_w, w_gate, w_up, w_down):
    depth = w_in.shape[0]
    assert depth == 1, "single trunk layer"
    l = 0
    bp, tp = x_prompt.shape[0], x_prompt.shape[1]
    bs, ts = x_sample.shape[0], x_sample.shape[1]
    assert tp % 512 == 0 and (bs * ts) % GROUP == 0 and GROUP % ts == 0 and ts & (ts - 1) == 0
    assert ts <= SUBLANES, "sample chunk must fit one sublane tile"

    wt = jnp.transpose(w_in[l])
    c_km, c_vm = M_QK_W, 2 * M_QK_W
    c_g = 2 * M_QK_W + 2 * M_V_W
    c_qa = c_g + 2 * M_HEADS
    c_ka = c_qa + A_Q_W
    gate_pad = jnp.zeros((BF16_ROWS - 2 * M_HEADS, D_MODEL), F32)
    wmt = jnp.concatenate([wt[0:c_km], wt[c_vm:c_g]], axis=0).astype(BF16)
    wqgt = jnp.concatenate([wt[c_qa:c_ka], wt[c_g:c_qa], gate_pad], axis=0).astype(BF16)
    wn = jnp.concatenate([wt[c_km:c_vm], wt[c_ka:]], axis=0).astype(BF16)
    head_of = jnp.arange(A_KV_W) // A_HEAD_DIM
    bd = (head_of[:, None] == head_of[None, :]).astype(BF16)
    qcol = (jnp.tile(q_norm_w[l], A_HEADS) * (A_HEAD_DIM ** -0.5)).reshape(A_Q_W, 1)
    krow = jnp.tile(k_norm_w[l], A_KV_HEADS).reshape(1, A_KV_W)
    nw_mix = norm_mix_w[l].reshape(1, D_MODEL)
    nw_ffn = norm_ffn_w[l].reshape(1, D_MODEL)
    bias_col = jnp.concatenate([mlstm_i_bias[l], mlstm_f_bias[l]]).reshape(2 * M_HEADS, 1)
    nw_col = mlstm_norm_w[l].reshape(M_V_W, 1)
    sinks = attn_sinks[l]
    proj = lambda x2d, tm: _proj(x2d, nw_mix, wmt, wqgt, wn, bd, qcol, krow, tm)

    xp = x_prompt.reshape(bp * tp, D_MODEL)
    later_ws = (w_branch_a[l], w_branch_b[l], w_out[l], w_gate[l], w_up[l], w_down[l])
    g_ab, h_m, h_a, kwin_p, vwin_p, st_p, m_p, wa, wb, wo, wg, wu, wd = _front(
        xp, bp, sinks, nw_mix, wmt, wqgt, wn, bd, qcol, krow, bias_col, nw_col, later_ws, 512)
    merge = lambda x2d, h_m, h_a, g_ab, tm: _merge_ffn(x2d, h_m, h_a, g_ab, wa, wb, wo, nw_ffn, wg, wu, wd, tm)
    yp = merge(xp, h_m, h_a, g_ab, 512).reshape(bp, tp, D_MODEL)
    c_p = jnp.swapaxes(st_p[:, :M_DV, :], 1, 2).reshape(bp, M_HEADS, M_DK, M_DV)
    n_p = st_p[:, M_DV, :].reshape(bp, M_HEADS, M_DK)
    m_pr = m_p[:, :M_HEADS, 0]

    ns = bs * ts
    xs = x_sample.reshape(ns, D_MODEL)
    tms = 512 if ns % 512 == 0 else GROUP
    qvot, gt, qat, k_m, kv_a, g_ab = proj(xs, tms)
    ngrp = ns // GROUP
    m_lanes = jnp.repeat(state_mlstm_m[l], ts, axis=0).reshape(ngrp, GROUP, M_HEADS)
    mrow = jnp.pad(jnp.swapaxes(m_lanes, 1, 2), ((0, 0), (0, SUBLANES - M_HEADS), (0, 0)))
    h_m, c_s, n_s, mt_s = _mlstm_sample(qvot, gt, k_m, mrow, bias_col, nw_col,
                                        state_mlstm_C[l].reshape(bs, M_QK_W, M_DV),
                                        state_mlstm_n[l].reshape(bs, M_QK_W), ts)
    wbuf = cache_swa_k.shape[2]
    to_fm = lambda a: jnp.transpose(a, (0, 2, 3, 1)).reshape(bs, A_KV_W, wbuf)
    from_fm = lambda a: jnp.transpose(a.reshape(bs, A_KV_HEADS, A_HEAD_DIM, wbuf), (0, 3, 1, 2))[None]
    h_a, kwin_s, vwin_s = _swa_sample(qat, kv_a, to_fm(cache_swa_k[l]), to_fm(cache_swa_v[l]), sinks, ts)
    ys = merge(xs, h_m, h_a, g_ab, tms).reshape(bs, ts, D_MODEL)
    m_s = jnp.swapaxes(mt_s[:, :M_HEADS, :], 1, 2).reshape(bs, ts, M_HEADS)[:, ts - 1, :]

    kv5 = lambda a: a.reshape(a.shape[0], a.shape[1], A_KV_HEADS, A_HEAD_DIM)[None]
    return (yp, ys,
            c_p[None], n_p[None], m_pr[None], kv5(kwin_p), kv5(vwin_p),
            c_s.reshape(bs, M_HEADS, M_DK, M_DV)[None], n_s.reshape(bs, M_HEADS, M_DK)[None], m_s[None],
            from_fm(kwin_s), from_fm(vwin_s))
```

```python
import functools

import jax
import jax.numpy as jnp
from jax import lax
from jax.experimental import pallas as pl
from jax.experimental.pallas import tpu as pltpu

F32 = jnp.float32
BF16 = jnp.bfloat16

D_MODEL = 1024
M_HEADS = 4
M_DK = 64
M_DV = 128
M_CHUNK = 64
M_QK_W = M_HEADS * M_DK
M_V_W = M_HEADS * M_DV
A_HEADS = 8
A_KV_HEADS = 2
A_HEAD_DIM = 64
A_GROUPS = A_HEADS // A_KV_HEADS
A_Q_W = A_HEADS * A_HEAD_DIM
A_KV_W = A_KV_HEADS * A_HEAD_DIM
WINDOW = 128
D_FF = 2816
EPS = 1e-6

LANES = 128
SUBLANES = 8
BF16_ROWS = 16
GROUP = 128
S_ROWS = M_DV + BF16_ROWS
VMEM_LIMIT = 60 * 1024 * 1024
DK_SHIFT = M_DK.bit_length() - 1
WINDOW_SHIFT = WINDOW.bit_length() - 1
HALF = LANES // 2
assert A_HEAD_DIM == HALF and A_KV_W == LANES, "attention head pairs share one lane-width"

NT_DIMS = (((1,), (1,)), ((), ()))


def _dot(a, b):
    return jnp.dot(a, b, preferred_element_type=F32)


def _dot_nt(a, b):
    return lax.dot_general(a, b, NT_DIMS, preferred_element_type=F32)


def _const_spec(shape):
    nd = len(shape)
    return pl.BlockSpec(shape, lambda *_: (0,) * nd, pipeline_mode=pl.Buffered(1))


def _params(sem):
    return pltpu.CompilerParams(dimension_semantics=sem, vmem_limit_bytes=VMEM_LIMIT)


def _rms_rows(x, nw):
    ms = jnp.mean(x * x, axis=-1, keepdims=True)
    return (x * lax.rsqrt(ms + EPS)) * nw


GAB_CHUNK = 512


def _proj_chunks(x_ref, nw_ref, wmt_ref, wqgt_ref, wn_ref, bd_ref, qcol_ref, krow_ref,
                 qvot_ref, gt_ref, qat_ref, km_ref, kva_ref, gab_ref):
    hn = _rms_rows(x_ref[...], nw_ref[...]).astype(BF16)
    n_blk = qat_ref.shape[0]

    def put(ref, rows, val):
        for c in range(n_blk):
            ref[c, rows, :] = val[:, c * LANES:(c + 1) * LANES].astype(ref.dtype)

    def mlstm_qvo():
        put(qvot_ref, slice(None), _dot_nt(wmt_ref[...], hn))

    def attn_q_and_gates():
        qt = _dot_nt(wqgt_ref[...], hn)
        put(gt_ref, slice(None), qt[A_Q_W:A_Q_W + 2 * M_HEADS])
        for h in range(A_HEADS):
            hs = slice(h * A_HEAD_DIM, (h + 1) * A_HEAD_DIM)
            blk = qt[hs]
            ssq_q = jnp.sum(blk * blk, axis=0, keepdims=True)
            put(qat_ref, hs, (blk * lax.rsqrt(ssq_q * (1.0 / A_HEAD_DIM) + EPS)) * qcol_ref[hs])

    def mlstm_k():
        km_ref[...] = _dot_nt(hn, wn_ref[0:M_QK_W, :]).astype(km_ref.dtype)

    def branch_gates(c0):
        def run():
            w0 = M_QK_W + 2 * A_KV_W + c0
            gab_ref[:, c0:c0 + GAB_CHUNK] = _dot_nt(hn, wn_ref[w0:w0 + GAB_CHUNK, :]).astype(gab_ref.dtype)
        return run

    def attn_kv():
        kv = _dot_nt(hn, wn_ref[M_QK_W:M_QK_W + 2 * A_KV_W, :])
        k = kv[:, 0:A_KV_W]
        ksq = k * k
        hi = ksq.astype(BF16)
        lo = (ksq - hi.astype(F32)).astype(BF16)
        ssq = _dot(hi, bd_ref[...]) + _dot(lo, bd_ref[...])
        kva_ref[:, 0:A_KV_W] = (k * lax.rsqrt(ssq * (1.0 / A_HEAD_DIM) + EPS)) * krow_ref[...]
        kva_ref[:, A_KV_W:] = kv[:, A_KV_W:]

    return ([mlstm_qvo, attn_q_and_gates, mlstm_k]
            + [branch_gates(c0) for c0 in range(0, gab_ref.shape[1], GAB_CHUNK)] + [attn_kv])


def _proj_kernel(*refs):
    for piece in _proj_chunks(*refs):
        piece()


def _proj(x2d, nw, wmt, wqgt, wn, bd, qcol, krow, tm):
    n = x2d.shape[0]
    row = lambda w: pl.BlockSpec((tm, w), lambda i: (i, 0))
    n_blk = tm // LANES
    slab = lambda r: pl.BlockSpec((n_blk, r, LANES), lambda i: (i, 0, 0))
    w_qvo = M_QK_W + 2 * M_V_W
    return pl.pallas_call(
        _proj_kernel,
        grid=(n // tm,),
        in_specs=[row(D_MODEL)] + [_const_spec(a.shape) for a in (nw, wmt, wqgt, wn, bd, qcol, krow)],
        out_specs=[slab(w_qvo), slab(2 * M_HEADS), slab(A_Q_W), row(M_QK_W), row(2 * A_KV_W), row(2 * D_MODEL)],
        out_shape=[jax.ShapeDtypeStruct((n // LANES, w_qvo, LANES), BF16),
                   jax.ShapeDtypeStruct((n // LANES, 2 * M_HEADS, LANES), F32),
                   jax.ShapeDtypeStruct((n // LANES, A_Q_W, LANES), BF16),
                   jax.ShapeDtypeStruct((n, M_QK_W), BF16),
                   jax.ShapeDtypeStruct((n, 2 * A_KV_W), F32),
                   jax.ShapeDtypeStruct((n, 2 * D_MODEL), BF16)],
        compiler_params=_params(("arbitrary",)),
        name="proj",
    )(x2d, nw, wmt, wqgt, wn, bd, qcol, krow)


def _split3_rows(x):
    hi = x.astype(BF16).astype(F32)
    r1 = x - hi
    mid = r1.astype(BF16).astype(F32)
    lo = r1 - mid
    return jnp.concatenate([hi, mid, lo], axis=0).astype(BF16)


def _log_sigmoid(x):
    return jnp.minimum(x, 0.0) - jnp.log1p(jnp.exp(-jnp.abs(x)))


def _chunk_masks(chunk_shift):
    s = lax.broadcasted_iota(jnp.int32, (GROUP, GROUP), 0)
    t = lax.broadcasted_iota(jnp.int32, (GROUP, GROUP), 1)
    same = (s >> chunk_shift) == (t >> chunk_shift)
    return same, same & (s <= t)


def _group_gates(gt, same, causal):
    cm_bf = jnp.where(causal, 1.0, 0.0).astype(BF16)
    lf = _log_sigmoid(gt)
    nr = gt.shape[0]
    bt3 = _dot(_split3_rows(lf), cm_bf)
    bt = (bt3[0:nr] + bt3[nr:2 * nr]) + bt3[2 * nr:3 * nr]
    b4 = bt[M_HEADS:2 * M_HEADS]
    a4 = gt[0:M_HEADS] - b4
    a8 = jnp.concatenate([a4, a4], axis=0)
    a_cols = jnp.concatenate([a8, jnp.zeros((GROUP - SUBLANES, GROUP), F32)], axis=0).T
    at_mats, run_rows, chunk_rows = [], [], []
    for h in range(M_HEADS):
        at = jnp.broadcast_to(a_cols[:, h:h + 1], (GROUP, GROUP))
        at_mats.append(at)
        run_rows.append(jnp.max(jnp.where(causal, at, -jnp.inf), axis=0, keepdims=True))
        chunk_rows.append(jnp.max(jnp.where(same, at, -jnp.inf), axis=0, keepdims=True))
    return b4, a4, at_mats, jnp.concatenate(run_rows, axis=0), jnp.concatenate(chunk_rows, axis=0)


def _group_weights(m_prev, b4, a4, run4, chunk4):
    big_m = jnp.maximum(m_prev, run4)
    m_last = jnp.maximum(m_prev, chunk4)
    w_inter = jnp.exp(m_prev - big_m)
    g_vec = jnp.exp(m_prev - m_last)
    w_last = jnp.exp(a4 - m_last)
    m_t = b4 + big_m
    return big_m, w_inter, g_vec, w_last, m_t, jnp.exp(-m_t)


def _group_scores(qvot, ks):
    lane_head = lax.broadcasted_iota(jnp.int32, (1, M_QK_W), 1) >> DK_SHIFT
    row_head = lax.broadcasted_iota(jnp.int32, (M_QK_W, 1), 0) >> DK_SHIFT
    qt = qvot(0, M_QK_W)
    k_stack = jnp.concatenate([jnp.where(lane_head == h, ks, jnp.zeros_like(ks)) for h in range(M_HEADS)], axis=0)
    qw = jnp.concatenate([jnp.where(row_head == h, qt, jnp.zeros_like(qt)) for h in range(M_HEADS)], axis=1)
    zero_blk = jnp.zeros((M_DK, GROUP), BF16)
    sc_t = []
    for h in range(0, M_HEADS, 2):
        q_pair = jnp.concatenate(
            [jnp.concatenate([qvot(h * M_DK, (h + 1) * M_DK), zero_blk], axis=1),
             jnp.concatenate([zero_blk, qvot((h + 1) * M_DK, (h + 2) * M_DK)], axis=1)], axis=0)
        sc = _dot(ks[:, h * M_DK:(h + 2) * M_DK], q_pair)
        sc_t += [sc[:, 0:GROUP], sc[:, GROUP:]]
    return k_stack, qw, sc_t


def _group_values(qvot, sc_t, at_mats, big_m, causal):
    ones_rows = jnp.where(lax.broadcasted_iota(jnp.int32, (BF16_ROWS, GROUP), 0) == 0, 1.0, 0.0).astype(BF16)
    zero_blk = jnp.zeros((GROUP, GROUP), BF16)
    vta, s_t = [], []
    for h in range(M_HEADS):
        w_t = jnp.where(causal, jnp.exp(at_mats[h] - big_m[h:h + 1]), 0.0)
        s_t.append((sc_t[h] * w_t).astype(BF16))
        vta.append(jnp.concatenate([qvot(M_QK_W + h * M_DV, M_QK_W + (h + 1) * M_DV), ones_rows], axis=0))
    intra = []
    for h in range(0, M_HEADS, 2):
        pair = jnp.concatenate([jnp.concatenate([s_t[h], zero_blk], axis=1),
                                jnp.concatenate([zero_blk, s_t[h + 1]], axis=1)], axis=0)
        intra.append(_dot(jnp.concatenate(vta[h:h + 2], axis=1), pair))
    return vta, jnp.concatenate(intra, axis=1)


def _weighted_values(vta, w_rows):
    return jnp.concatenate([(vta[h].astype(F32) * w_rows[h:h + 1]).astype(BF16) for h in range(M_HEADS)], axis=1)


def _lanes_x(rows):
    return jnp.concatenate([rows[h:h + 1] for h in range(M_HEADS)], axis=1)


def _group_out(qvot, inter, intra, w_inter, e_neg_m, nw_ref):
    outs = []
    out_all = inter * _lanes_x(w_inter) + intra
    for h in range(M_HEADS):
        out_t = out_all[:, h * GROUP:(h + 1) * GROUP]
        hh = out_t[0:M_DV] / jnp.maximum(jnp.abs(out_t[M_DV:M_DV + 1]), e_neg_m[h:h + 1])
        ms = jnp.mean(hh * hh, axis=0, keepdims=True)
        hn = (hh * lax.rsqrt(ms + EPS)) * nw_ref[h * M_DV:(h + 1) * M_DV]
        o_t = qvot(M_QK_W + M_V_W + h * M_DV, M_QK_W + M_V_W + (h + 1) * M_DV)
        outs.append(hn * jax.nn.sigmoid(o_t.astype(F32)))
    return jnp.concatenate(outs, axis=0).T


def _decay_row(g_vec, lane0):
    lane_head = lax.broadcasted_iota(jnp.int32, (1, M_QK_W), 1) >> DK_SHIFT
    g_row = jnp.zeros((1, M_QK_W), F32)
    for h in range(M_HEADS):
        g_row = jnp.where(lane_head == h, g_vec[h:h + 1, lane0:lane0 + 1], g_row)
    return g_row


def _mlstm_sample_kernel(seq_len, qvot_ref, gt_ref, k_ref, mrow_ref, bias_ref, nw_ref, c_ref, n_ref,
                         h_ref, c_out_ref, n_out_ref, mt_ref):
    n_seq = GROUP // seq_len
    shift = seq_len.bit_length() - 1
    same, causal = _chunk_masks(shift)
    lane_seq = lax.broadcasted_iota(jnp.int32, (1, GROUP), 1) >> shift
    lane_seq_x = jnp.concatenate([lane_seq] * M_HEADS, axis=1)
    row0 = lax.broadcasted_iota(jnp.int32, (BF16_ROWS, M_QK_W), 0) == 0
    qvot = lambda r0, r1: qvot_ref[0, r0:r1, :]
    ks = k_ref[...] * (M_DK ** -0.5)
    b4, a4, at_mats, run4, chunk4 = _group_gates(gt_ref[0] + bias_ref[...], same, causal)
    big_m, w_inter, g_vec, w_last, m_t, e_neg_m = _group_weights(mrow_ref[0, 0:M_HEADS, :], b4, a4, run4, chunk4)
    mt_ref[0] = jnp.concatenate([m_t, m_t], axis=0)
    k_stack, qw, sc_t = _group_scores(qvot, ks)

    st_old = []
    for s in range(n_seq):
        n_rows = jnp.where(row0, jnp.broadcast_to(n_ref[s:s + 1, :], (BF16_ROWS, M_QK_W)), 0.0)
        st_old.append(jnp.concatenate([c_ref[s].T, n_rows], axis=0))
    inter_all = _dot(jnp.concatenate(st_old, axis=0).astype(BF16), qw)
    vta, intra = _group_values(qvot, sc_t, at_mats, big_m, causal)
    inter = inter_all[0:S_ROWS]
    for s in range(1, n_seq):
        inter = jnp.where(lane_seq_x == s, inter_all[s * S_ROWS:(s + 1) * S_ROWS], inter)
    tall = jnp.concatenate([_weighted_values(vta, jnp.where(lane_seq == s, w_last, 0.0)) for s in range(n_seq)],
                           axis=0)
    d_st = _dot(tall, k_stack)
    for s in range(n_seq):
        st_new = _decay_row(g_vec, s * seq_len) * st_old[s] + d_st[s * S_ROWS:(s + 1) * S_ROWS]
        c_out_ref[s] = st_new[0:M_DV].T
        n_out_ref[s:s + 1, :] = st_new[M_DV:M_DV + 1]
    h_ref[...] = _group_out(qvot, inter, intra, w_inter, e_neg_m, nw_ref).astype(h_ref.dtype)


def _swa_scores(qbs, has_prev, sink_ref, qt_ref, kv_ref, kvp_ref):
    hd = A_HEAD_DIM
    nq = A_GROUPS * WINDOW
    si = lax.broadcasted_iota(jnp.int32, (2 * WINDOW, nq), 0)
    qi = lax.broadcasted_iota(jnp.int32, (2 * WINDOW, nq), 1) & (WINDOW - 1)
    local = ((si < WINDOW) & (si > qi)) | ((si >= WINDOW) & (si - WINDOW <= qi))
    first = local & (has_prev | (si >= WINDOW))
    lane_grp = lax.broadcasted_iota(jnp.int32, (1, nq), 1) >> WINDOW_SHIFT
    kv_block = lambda i: kvp_ref[...] if i == 0 else kv_ref[(i - 1) * WINDOW:i * WINDOW, :]
    k_bf = lambda i: kv_block(i)[:, 0:A_KV_W].astype(BF16)
    vt_bf = lambda i: kv_block(i)[:, A_KV_W:].T.astype(BF16)
    zeros = jnp.zeros((hd, nq), BF16)
    sinks = []
    for kvh in range(A_KV_HEADS):
        sk = jnp.zeros((1, nq), F32)
        for g in range(A_GROUPS):
            sk = jnp.where(lane_grp == g, sink_ref[kvh * A_GROUPS + g], sk)
        sinks.append(sk)
    units = []
    for qb in qbs:
        kk = jnp.concatenate([k_bf(qb), k_bf(qb + 1)], axis=0)
        vt = jnp.concatenate([vt_bf(qb), vt_bf(qb + 1)], axis=1)
        mask = local if qb > 0 else first
        for kvh in range(A_KV_HEADS):
            q4t = jnp.concatenate(
                [qt_ref[qb, (kvh * A_GROUPS + g) * hd:(kvh * A_GROUPS + g + 1) * hd, :] for g in range(A_GROUPS)],
                axis=1)
            wq = jnp.concatenate([q4t, zeros] if kvh == 0 else [zeros, q4t], axis=0)
            units.append((jnp.where(mask, _dot(kk, wq), -jnp.inf), sinks[kvh], vt))
    return units


def _swa_finish(qbs, units, h_ref):
    hd = A_HEAD_DIM
    for i, qb in enumerate(qbs):
        pieces = []
        for kvh in range(A_KV_HEADS):
            s, sk, vt = units[i * A_KV_HEADS + kvh]
            mx = jnp.maximum(jnp.max(s, axis=0, keepdims=True), sk)
            p = jnp.exp(s - mx)
            den = jnp.sum(p, axis=0, keepdims=True) + jnp.exp(sk - mx)
            ot = _dot(vt[kvh * hd:(kvh + 1) * hd], p.astype(BF16)) / den
            pieces += [ot[:, g * WINDOW:(g + 1) * WINDOW] for g in range(A_GROUPS)]
        h_t = jnp.concatenate(pieces, axis=0)
        h_ref[qb * WINDOW:(qb + 1) * WINDOW, :] = h_t.T.astype(h_ref.dtype)


def _front_kernel(tiles_per_seq, n_cast, *refs):
    (sink_ref, x_ref, nw_ref, wmt_ref, wqgt_ref, wn_ref, bd_ref, qcol_ref, krow_ref, bias_ref, nwm_ref) = refs[:11]
    cast_in = refs[11:11 + n_cast]
    (gab_ref, hm_ref, ha_ref, kwin_ref, vwin_ref, st_ref, m_ref) = refs[11 + n_cast:18 + n_cast]
    cast_out = refs[18 + n_cast:18 + 2 * n_cast]
    (q_qvot, q_gt, q_qat, q_km, q_kva, p_qvot, p_gt, p_qat, p_km, p_kva, st_s, m_s, kvp_s) = refs[18 + 2 * n_cast:]
    _front_body(tiles_per_seq, sink_ref, x_ref, nw_ref, wmt_ref, wqgt_ref, wn_ref, bd_ref, qcol_ref, krow_ref,
                bias_ref, nwm_ref, gab_ref, hm_ref, ha_ref, kwin_ref, vwin_ref, st_ref, m_ref,
                q_qvot, q_gt, q_qat, q_km, q_kva, p_qvot, p_gt, p_qat, p_km, p_kva, st_s, m_s, kvp_s)
    for src, dst in zip(cast_in, cast_out):
        dst[...] = src[...].astype(dst.dtype)


def _front_body(tiles_per_seq, sink_ref, x_ref, nw_ref, wmt_ref, wqgt_ref, wn_ref, bd_ref, qcol_ref, krow_ref,
                bias_ref, nwm_ref,
                gab_ref, hm_ref, ha_ref, kwin_ref, vwin_ref, st_ref, m_ref,
                q_qvot, q_gt, q_qat, q_km, q_kva, p_qvot, p_gt, p_qat, p_km, p_kva, st_s, m_s, kvp_s):
    k = pl.program_id(0)

    @pl.when(k == 0)
    def _():
        for r in (p_qvot, p_gt, p_qat, p_km, p_kva, st_s, m_s, kvp_s):
            r[...] = jnp.zeros(r.shape, r.dtype)

    pieces = _proj_chunks(x_ref, nw_ref, wmt_ref, wqgt_ref, wn_ref, bd_ref, qcol_ref, krow_ref,
                          q_qvot, q_gt, q_qat, q_km, q_kva, gab_ref)
    n_piece = len(pieces)

    seq_start = lax.rem(k - 1 + tiles_per_seq, tiles_per_seq) == 0
    n_blk = p_qat.shape[0]
    same, causal = _chunk_masks(GROUP.bit_length() - 1)
    st = jnp.where(seq_start, 0.0, st_s[...])
    m_col = jnp.where(seq_start, 0.0, m_s[0:M_HEADS, 0:1])

    grp = []
    for g in range(n_blk):
        qvot = lambda r0, r1, g=g: p_qvot[g, r0:r1, :]
        ks = p_km[g * GROUP:(g + 1) * GROUP, :] * (M_DK ** -0.5)
        b4, a4, at_mats, run4, chunk4 = _group_gates(p_gt[g] + bias_ref[...], same, causal)
        weights = _group_weights(jnp.broadcast_to(m_col, (M_HEADS, GROUP)), b4, a4, run4, chunk4)
        m_col = weights[4][:, GROUP - 1:GROUP]
        grp.append((qvot, at_mats, weights) + _group_scores(qvot, ks))
    has_prev = jnp.logical_not(seq_start)
    qb_lo, qb_hi = list(range(0, n_blk // 2)), list(range(n_blk // 2, n_blk))
    units_lo = _swa_scores(qb_lo, has_prev, sink_ref, p_qat, p_kva, kvp_s)
    inter = [_dot(st.astype(BF16), grp[0][4])]
    for piece in pieces[0:1]:
        piece()

    vals = []
    for qvot, at_mats, (big_m, _, _, w_last, _, _), k_stack, _, sc_t in grp:
        vta, intra = _group_values(qvot, sc_t, at_mats, big_m, causal)
        vals.append((intra, _dot(_weighted_values(vta, w_last), k_stack)))
    _swa_finish(qb_lo, units_lo, ha_ref)
    units_hi = _swa_scores(qb_hi, has_prev, sink_ref, p_qat, p_kva, kvp_s)
    for piece in pieces[1:3]:
        piece()

    for g in range(n_blk):
        st = _decay_row(grp[g][2][2], 0) * st + vals[g][1]
        if g + 1 < n_blk:
            inter.append(_dot(st.astype(BF16), grp[g + 1][4]))
    _swa_finish(qb_hi, units_hi, ha_ref)
    for piece in pieces[3:]:
        piece()

    for g in range(n_blk):
        qvot, _, (_, w_inter, _, _, _, e_neg_m) = grp[g][0:3]
        hm_ref[g * GROUP:(g + 1) * GROUP, :] = _group_out(qvot, inter[g], vals[g][0], w_inter, e_neg_m,
                                                          nwm_ref).astype(hm_ref.dtype)
    st_s[...] = st
    st_ref[0] = st
    m_rows = jnp.broadcast_to(m_col, (M_HEADS, LANES))
    m_rows = jnp.concatenate([m_rows, m_rows], axis=0)
    m_s[...] = m_rows
    m_ref[0] = m_rows
    tm = p_kva.shape[0]
    kwin_ref[0] = p_kva[tm - WINDOW:, 0:A_KV_W]
    vwin_ref[0] = p_kva[tm - WINDOW:, A_KV_W:]
    kvp_s[...] = p_kva[tm - WINDOW:, :]

    for p, q in ((p_qvot, q_qvot), (p_gt, q_gt), (p_qat, q_qat), (p_km, q_km), (p_kva, q_kva)):
        p[...] = q[...]


def _cast_rows(rows, n_steps):
    rb = BF16_ROWS
    while rows % rb or rows // rb > n_steps:
        rb += BF16_ROWS
    return rb


def _front(x2d, nb, sinks, nw, wmt, wqgt, wn, bd, qcol, krow, bias_col, nwm_col, cast_ws, tm):
    n = x2d.shape[0]
    n_tiles = n // tm
    tps = n_tiles // nb
    n_blk = tm // LANES
    w_qvo = M_QK_W + 2 * M_V_W
    cur = lambda w: pl.BlockSpec((tm, w), lambda k: (jnp.minimum(k, n_tiles - 1), 0))
    prev = lambda w: pl.BlockSpec((tm, w), lambda k: (jnp.maximum(k - 1, 0), 0))
    per_seq = lambda r, w: pl.BlockSpec((1, r, w), lambda k: (jnp.maximum(k - 1, 0) // tps, 0, 0))

    def cast_spec(a):
        rb = _cast_rows(a.shape[0], n_tiles)
        return pl.BlockSpec((rb, a.shape[1]), lambda k: (jnp.minimum(k, a.shape[0] // rb - 1), 0))

    cast_in_specs = [cast_spec(a) for a in cast_ws]
    cast_out_specs = [cast_spec(a) for a in cast_ws]
    proj_scratch = [pltpu.VMEM((n_blk, w_qvo, LANES), BF16), pltpu.VMEM((n_blk, 2 * M_HEADS, LANES), F32),
                    pltpu.VMEM((n_blk, A_Q_W, LANES), BF16), pltpu.VMEM((tm, M_QK_W), BF16),
                    pltpu.VMEM((tm, 2 * A_KV_W), F32)]
    return pl.pallas_call(
        functools.partial(_front_kernel, tps, len(cast_ws)),
        grid=(n_tiles + 1,),
        in_specs=[pl.BlockSpec(memory_space=pltpu.SMEM), cur(D_MODEL)]
                 + [_const_spec(a.shape) for a in (nw, wmt, wqgt, wn, bd, qcol, krow, bias_col, nwm_col)]
                 + cast_in_specs,
        out_specs=[cur(2 * D_MODEL), prev(M_V_W), prev(A_Q_W), per_seq(WINDOW, A_KV_W), per_seq(WINDOW, A_KV_W),
                   per_seq(S_ROWS, M_QK_W), per_seq(SUBLANES, LANES)] + cast_out_specs,
        out_shape=[jax.ShapeDtypeStruct((n, 2 * D_MODEL), BF16),
                   jax.ShapeDtypeStruct((n, M_V_W), BF16),
                   jax.ShapeDtypeStruct((n, A_Q_W), BF16),
                   jax.ShapeDtypeStruct((nb, WINDOW, A_KV_W), F32),
                   jax.ShapeDtypeStruct((nb, WINDOW, A_KV_W), F32),
                   jax.ShapeDtypeStruct((nb, S_ROWS, M_QK_W), F32),
                   jax.ShapeDtypeStruct((nb, SUBLANES, LANES), F32)]
                  + [jax.ShapeDtypeStruct(a.shape, BF16) for a in cast_ws],
        scratch_shapes=proj_scratch + proj_scratch + [pltpu.VMEM((S_ROWS, M_QK_W), F32),
                                                       pltpu.VMEM((SUBLANES, LANES), F32),
                                                       pltpu.VMEM((WINDOW, 2 * A_KV_W), F32)],
        compiler_params=_params(("arbitrary",)),
        name="front",
    )(sinks, x2d, nw, wmt, wqgt, wn, bd, qcol, krow, bias_col, nwm_col, *cast_ws)


def _window_update(seq_len, kv_ref, ck_ref, cv_ref, kwin_ref, vwin_ref):
    n_seq = GROUP // seq_len
    wb = ck_ref.shape[2]
    lane = lax.broadcasted_iota(jnp.int32, (1, LANES), 1)
    kt_new, vt_new = kv_ref[:, 0:A_KV_W].T, kv_ref[:, A_KV_W:].T
    for s in range(n_seq):
        put = (wb - seq_len - s * seq_len) % LANES
        for new_t, c_ref, win_ref in ((kt_new, ck_ref, kwin_ref), (vt_new, cv_ref, vwin_ref)):
            win_ref[s] = jnp.where(lane >= wb - seq_len, pltpu.roll(new_t, put, axis=1) if put else new_t,
                                   pltpu.roll(c_ref[s], wb - seq_len, axis=1))


def _swa_sample_attend(seq_len, sink_ref, qt_ref, kv_ref, ck_ref, cv_ref, h_ref):
    n_seq = GROUP // seq_len
    wb = ck_ref.shape[2]
    lane = lax.broadcasted_iota(jnp.int32, (1, LANES), 1)
    lo_half = lane < HALF
    q_rows = qt_ref[0].astype(F32).T
    k_new = kv_ref[:, 0:A_KV_W].reshape(n_seq, seq_len, A_KV_W)
    v_new = kv_ref[:, A_KV_W:].reshape(n_seq, seq_len, A_KV_W)

    def to_kv_half(x, head):
        kvh = head // A_GROUPS
        if head % 2 != kvh:
            x = pltpu.roll(x, HALF, axis=1)
        return jnp.where(lo_half if kvh == 0 else ~lo_half, x, 0.0)

    lhs = jnp.concatenate(
        [to_kv_half(q_rows[:, (h // 2) * LANES:(h // 2 + 1) * LANES], h).reshape(n_seq, seq_len, LANES)
         for h in range(A_HEADS)], axis=1).astype(BF16)
    zpad = jnp.zeros((n_seq, BF16_ROWS - seq_len, A_KV_W), F32)
    k_nb = jnp.concatenate([k_new, zpad], axis=1).astype(BF16)
    v_nb = jnp.concatenate([v_new, zpad], axis=1).astype(BF16)
    s_c = jnp.einsum('sqf,sfk->sqk', lhs, ck_ref[...].astype(BF16), preferred_element_type=F32)
    s_n = jnp.einsum('sqf,skf->sqk', lhs, k_nb, preferred_element_type=F32)
    nrow = A_HEADS * seq_len
    ti = lax.broadcasted_iota(jnp.int32, (nrow, wb), 0) & (seq_len - 1)
    ki = lax.broadcasted_iota(jnp.int32, (nrow, wb), 1)
    mask_c = (ti + wb - ki) < WINDOW
    ti_n = lax.broadcasted_iota(jnp.int32, (nrow, BF16_ROWS), 0) & (seq_len - 1)
    ki_n = lax.broadcasted_iota(jnp.int32, (nrow, BF16_ROWS), 1)
    mask_n = ki_n <= ti_n
    row_head = lax.broadcasted_iota(jnp.int32, (nrow, 1), 0) >> (seq_len.bit_length() - 1)
    sk = jnp.zeros((nrow, 1), F32)
    for h in range(A_HEADS):
        sk = jnp.where(row_head == h, sink_ref[h], sk)
    s_c = jnp.where(mask_c, s_c, -jnp.inf)
    s_n = jnp.where(mask_n, s_n, -jnp.inf)
    mx = jnp.maximum(jnp.maximum(jnp.max(s_c, axis=-1, keepdims=True), jnp.max(s_n, axis=-1, keepdims=True)), sk)
    p_c = jnp.exp(s_c - mx)
    p_n = jnp.exp(s_n - mx)
    den = jnp.sum(p_c, axis=-1, keepdims=True) + jnp.sum(p_n, axis=-1, keepdims=True) + jnp.exp(sk - mx)
    o = (jnp.einsum('sqk,sfk->sqf', p_c.astype(BF16), cv_ref[...].astype(BF16), preferred_element_type=F32)
         + jnp.einsum('sqk,skf->sqf', p_n.astype(BF16), v_nb, preferred_element_type=F32)) / den

    def from_kv_half(head):
        x = o[:, head * seq_len:(head + 1) * seq_len, :].reshape(GROUP, LANES)
        return pltpu.roll(x, HALF, axis=1) if head % 2 != head // A_GROUPS else x

    for c in range(A_HEADS // 2):
        h_ref[:, c * LANES:(c + 1) * LANES] = jnp.where(lo_half, from_kv_half(2 * c),
                                                        from_kv_half(2 * c + 1)).astype(h_ref.dtype)


def _mixers_sample_kernel(seq_len, sink_ref, qvot_ref, gt_ref, k_ref, mrow_ref, bias_ref, nw_ref, c_ref, n_ref,
                          qt_ref, kv_ref, ck_ref, cv_ref,
                          hm_ref, c_out_ref, n_out_ref, mt_ref, ha_ref, kwin_ref, vwin_ref):
    _mlstm_sample_kernel(seq_len, qvot_ref, gt_ref, k_ref, mrow_ref, bias_ref, nw_ref, c_ref, n_ref,
                         hm_ref, c_out_ref, n_out_ref, mt_ref)
    _window_update(seq_len, kv_ref, ck_ref, cv_ref, kwin_ref, vwin_ref)
    _swa_sample_attend(seq_len, sink_ref, qt_ref, kv_ref, ck_ref, cv_ref, ha_ref)


def _mixers_sample(sinks, qvot, gt, k_m, mrow, bias_col, nw_col, c, n, qat, kv_a, cache_k, cache_v, seq_len):
    ngrp = qvot.shape[0]
    n_seq = GROUP // seq_len
    wb = cache_k.shape[2]
    assert wb == LANES, "window positions fill one lane-width"
    full = lambda a: pl.BlockSpec(a.shape, lambda i: (0,) * a.ndim)
    slab = lambda a: pl.BlockSpec((1,) + a.shape[1:], lambda i: (i, 0, 0))
    row = lambda w: pl.BlockSpec((GROUP, w), lambda i: (i, 0))
    c_spec = pl.BlockSpec((n_seq,) + c.shape[1:], lambda i: (i, 0, 0))
    n_spec = pl.BlockSpec((n_seq, n.shape[1]), lambda i: (i, 0))
    cache = pl.BlockSpec((n_seq, A_KV_W, wb), lambda i: (i, 0, 0))
    return pl.pallas_call(
        functools.partial(_mixers_sample_kernel, seq_len),
        grid=(ngrp,),
        in_specs=[pl.BlockSpec(memory_space=pltpu.SMEM), slab(qvot), slab(gt), row(M_QK_W), slab(mrow),
                  full(bias_col), full(nw_col), c_spec, n_spec, slab(qat), row(2 * A_KV_W), cache, cache],
        out_specs=[row(M_V_W), c_spec, n_spec, slab(mrow), row(A_Q_W), cache, cache],
        out_shape=[jax.ShapeDtypeStruct((ngrp * GROUP, M_V_W), BF16),
                   jax.ShapeDtypeStruct(c.shape, F32),
                   jax.ShapeDtypeStruct(n.shape, F32),
                   jax.ShapeDtypeStruct(mrow.shape, F32),
                   jax.ShapeDtypeStruct((ngrp * GROUP, A_Q_W), BF16),
                   jax.ShapeDtypeStruct(cache_k.shape, F32),
                   jax.ShapeDtypeStruct(cache_v.shape, F32)],
        compiler_params=_params(("arbitrary",)),
        name="mixers_sample",
    )(sinks, qvot, gt, k_m, mrow, bias_col, nw_col, c, n, qat, kv_a, cache_k, cache_v)


def _merge_ffn_kernel(n_first, x1_ref, hm1_ref, ha1_ref, gab1_ref, x2_ref, hm2_ref, ha2_ref, gab2_ref,
                      wa_ref, wb_ref, wo_ref, nw_ref, wg_ref, wu_ref, wd_ref, y1_ref, y2_ref):
    second = pl.program_id(0) >= n_first

    def body(x_ref, hm_ref, ha_ref, gab_ref, y_ref):
        ga = jax.nn.sigmoid(gab_ref[:, 0:D_MODEL].astype(F32))
        gb = jax.nn.sigmoid(gab_ref[:, D_MODEL:].astype(F32))
        mix = ga * _dot(hm_ref[...], wa_ref[...]) + gb * _dot(ha_ref[...], wb_ref[...])
        x1 = x_ref[...] + _dot(mix.astype(BF16), wo_ref[...])
        hf = _rms_rows(x1, nw_ref[...]).astype(BF16)
        gate = _dot(hf, wg_ref[...])
        up = _dot(hf, wu_ref[...])
        act = (jax.nn.silu(gate) * up).astype(BF16)
        y_ref[...] = x1 + _dot(act, wd_ref[...])

    pl.when(jnp.logical_not(second))(lambda: body(x1_ref, hm1_ref, ha1_ref, gab1_ref, y1_ref))
    pl.when(second)(lambda: body(x2_ref, hm2_ref, ha2_ref, gab2_ref, y2_ref))


def _merge_ffn(group1, group2, wa, wb, wo, nw, wg, wu, wd, tm):
    n1, n2 = group1[0].shape[0], group2[0].shape[0]
    t1, t2 = n1 // tm, n2 // tm
    first = lambda w: pl.BlockSpec((tm, w), lambda k: (jnp.minimum(k, t1 - 1), 0))
    later = lambda w: pl.BlockSpec((tm, w), lambda k: (jnp.maximum(k - t1, 0), 0))
    widths = (D_MODEL, M_V_W, A_Q_W, 2 * D_MODEL)
    return pl.pallas_call(
        functools.partial(_merge_ffn_kernel, t1),
        grid=(t1 + t2,),
        in_specs=[first(w) for w in widths] + [later(w) for w in widths]
                 + [_const_spec(w.shape) for w in (wa, wb, wo, nw, wg, wu, wd)],
        out_specs=[first(D_MODEL), later(D_MODEL)],
        out_shape=[jax.ShapeDtypeStruct((n1, D_MODEL), F32), jax.ShapeDtypeStruct((n2, D_MODEL), F32)],
        compiler_params=_params(("arbitrary",)),
        name="merge_ffn",
    )(*group1, *group2, wa, wb, wo, nw, wg, wu, wd)


def kernel(x_prompt, x_sample, state_mlstm_C, state_mlstm_n, state_mlstm_m, cache_swa_k, cache_swa_v,
           norm_mix_w, w_in, mlstm_i_bias, mlstm_f_bias, mlstm_norm_w, q_norm_w, k_norm_w, attn_sinks,
           w_branch_a, w_branch_b, w_out, norm_ffn_w, w_gate, w_up, w_down):
    depth = w_in.shape[0]
    assert depth == 1, "single trunk layer"
    l = 0
    bp, tp = x_prompt.shape[0], x_prompt.shape[1]
    bs, ts = x_sample.shape[0], x_sample.shape[1]
    assert tp % 512 == 0 and (bs * ts) % GROUP == 0 and GROUP % ts == 0 and ts & (ts - 1) == 0
    assert ts <= SUBLANES, "sample chunk must fit one sublane tile"

    wt = jnp.transpose(w_in[l])
    c_km, c_vm = M_QK_W, 2 * M_QK_W
    c_g = 2 * M_QK_W + 2 * M_V_W
    c_qa = c_g + 2 * M_HEADS
    c_ka = c_qa + A_Q_W
    gate_pad = jnp.zeros((BF16_ROWS - 2 * M_HEADS, D_MODEL), F32)
    wmt = jnp.concatenate([wt[0:c_km], wt[c_vm:c_g]], axis=0).astype(BF16)
    wqgt = jnp.concatenate([wt[c_qa:c_ka], wt[c_g:c_qa], gate_pad], axis=0).astype(BF16)
    wn = jnp.concatenate([wt[c_km:c_vm], wt[c_ka:]], axis=0).astype(BF16)
    head_of = jnp.arange(A_KV_W) // A_HEAD_DIM
    bd = (head_of[:, None] == head_of[None, :]).astype(BF16)
    qcol = (jnp.tile(q_norm_w[l], A_HEADS) * (A_HEAD_DIM ** -0.5)).reshape(A_Q_W, 1)
    krow = jnp.tile(k_norm_w[l], A_KV_HEADS).reshape(1, A_KV_W)
    nw_mix = norm_mix_w[l].reshape(1, D_MODEL)
    nw_ffn = norm_ffn_w[l].reshape(1, D_MODEL)
    bias_col = jnp.concatenate([mlstm_i_bias[l], mlstm_f_bias[l]]).reshape(2 * M_HEADS, 1)
    nw_col = mlstm_norm_w[l].reshape(M_V_W, 1)
    sinks = attn_sinks[l]
    proj = lambda x2d, tm: _proj(x2d, nw_mix, wmt, wqgt, wn, bd, qcol, krow, tm)

    xp = x_prompt.reshape(bp * tp, D_MODEL)
    later_ws = (w_branch_a[l], w_branch_b[l], w_out[l], w_gate[l], w_up[l], w_down[l])
    g_ab, h_m, h_a, kwin_p, vwin_p, st_p, m_p, wa, wb, wo, wg, wu, wd = _front(
        xp, bp, sinks, nw_mix, wmt, wqgt, wn, bd, qcol, krow, bias_col, nw_col, later_ws, 512)
    prompt_rows = (xp, h_m, h_a, g_ab)
    c_p = jnp.swapaxes(st_p[:, :M_DV, :], 1, 2).reshape(bp, M_HEADS, M_DK, M_DV)
    n_p = st_p[:, M_DV, :].reshape(bp, M_HEADS, M_DK)
    m_pr = m_p[:, :M_HEADS, 0]

    ns = bs * ts
    xs = x_sample.reshape(ns, D_MODEL)
    tms = 512 if ns % 512 == 0 else GROUP
    qvot, gt, qat, k_m, kv_a, g_ab = proj(xs, tms)
    ngrp = ns // GROUP
    m_lanes = jnp.repeat(state_mlstm_m[l], ts, axis=0).reshape(ngrp, GROUP, M_HEADS)
    mrow = jnp.pad(jnp.swapaxes(m_lanes, 1, 2), ((0, 0), (0, SUBLANES - M_HEADS), (0, 0)))
    wbuf = cache_swa_k.shape[2]
    to_fm = lambda a: jnp.transpose(a, (0, 2, 3, 1)).reshape(bs, A_KV_W, wbuf)
    from_fm = lambda a: jnp.transpose(a.reshape(bs, A_KV_HEADS, A_HEAD_DIM, wbuf), (0, 3, 1, 2))[None]
    h_m, c_s, n_s, mt_s, h_a, kwin_s, vwin_s = _mixers_sample(
        sinks, qvot, gt, k_m, mrow, bias_col, nw_col, state_mlstm_C[l].reshape(bs, M_QK_W, M_DV),
        state_mlstm_n[l].reshape(bs, M_QK_W), qat, kv_a, to_fm(cache_swa_k[l]), to_fm(cache_swa_v[l]), ts)
    yp, ys = _merge_ffn(prompt_rows, (xs, h_m, h_a, g_ab), wa, wb, wo, nw_ffn, wg, wu, wd, tms)
    yp, ys = yp.reshape(bp, tp, D_MODEL), ys.reshape(bs, ts, D_MODEL)
    m_s = jnp.swapaxes(mt_s[:, :M_HEADS, :], 1, 2).reshape(bs, ts, M_HEADS)[:, ts - 1, :]

    kv5 = lambda a: a.reshape(a.shape[0], a.shape[1], A_KV_HEADS, A_HEAD_DIM)[None]
    return (yp, ys,
            c_p[None], n_p[None], m_pr[None], kv5(kwin_p), kv5(vwin_p),
            c_s.reshape(bs, M_HEADS, M_DK, M_DV)[None], n_s.reshape(bs, M_HEADS, M_DK)[None], m_s[None],
            from_fm(kwin_s), from_fm(vwin_s))
```

```python
import functools

import jax
import jax.numpy as jnp
from jax import lax
from jax.experimental import pallas as pl
from jax.experimental.pallas import tpu as pltpu

F32 = jnp.float32
BF16 = jnp.bfloat16

D_MODEL = 1024
M_HEADS = 4
M_DK = 64
M_DV = 128
M_CHUNK = 64
M_QK_W = M_HEADS * M_DK
M_V_W = M_HEADS * M_DV
A_HEADS = 8
A_KV_HEADS = 2
A_HEAD_DIM = 64
A_GROUPS = A_HEADS // A_KV_HEADS
A_Q_W = A_HEADS * A_HEAD_DIM
A_KV_W = A_KV_HEADS * A_HEAD_DIM
WINDOW = 128
D_FF = 2816
EPS = 1e-6

LANES = 128
SUBLANES = 8
BF16_ROWS = 16
GROUP = 128
S_ROWS = M_DV + BF16_ROWS
VMEM_LIMIT = 56 * 1024 * 1024
DK_SHIFT = M_DK.bit_length() - 1
WINDOW_SHIFT = WINDOW.bit_length() - 1
HALF = LANES // 2
assert A_HEAD_DIM == HALF and A_KV_W == LANES, "attention head pairs share one lane-width"

NT_DIMS = (((1,), (1,)), ((), ()))


def _dot(a, b):
    return jnp.dot(a, b, preferred_element_type=F32)


def _dot_nt(a, b):
    return lax.dot_general(a, b, NT_DIMS, preferred_element_type=F32)


def _const_spec(shape):
    nd = len(shape)
    return pl.BlockSpec(shape, lambda *_: (0,) * nd, pipeline_mode=pl.Buffered(1))


def _params(sem):
    return pltpu.CompilerParams(dimension_semantics=sem, vmem_limit_bytes=VMEM_LIMIT)


def _rms_rows(x, nw):
    ms = jnp.mean(x * x, axis=-1, keepdims=True)
    return (x * lax.rsqrt(ms + EPS)) * nw


GAB_CHUNK = 512


def _proj_chunks(x_ref, nw_ref, wmt_ref, wqgt_ref, wn_ref, bd_ref, qcol_ref, krow_ref,
                 qvot_ref, gt_ref, qat_ref, km_ref, kva_ref, gab_ref):
    hn = _rms_rows(x_ref[...], nw_ref[...]).astype(BF16)
    n_blk = qat_ref.shape[0]

    def put(ref, rows, val):
        for c in range(n_blk):
            ref[c, rows, :] = val[:, c * LANES:(c + 1) * LANES].astype(ref.dtype)

    def mlstm_qvo():
        put(qvot_ref, slice(None), _dot_nt(wmt_ref[...], hn))

    def attn_q_and_gates():
        qt = _dot_nt(wqgt_ref[...], hn)
        put(gt_ref, slice(None), qt[A_Q_W:A_Q_W + 2 * M_HEADS])
        for h in range(A_HEADS):
            hs = slice(h * A_HEAD_DIM, (h + 1) * A_HEAD_DIM)
            blk = qt[hs]
            ssq_q = jnp.sum(blk * blk, axis=0, keepdims=True)
            put(qat_ref, hs, (blk * lax.rsqrt(ssq_q * (1.0 / A_HEAD_DIM) + EPS)) * qcol_ref[hs])

    def mlstm_k():
        km_ref[...] = _dot_nt(hn, wn_ref[0:M_QK_W, :]).astype(km_ref.dtype)

    def branch_gates(c0):
        def run():
            w0 = M_QK_W + 2 * A_KV_W + c0
            gab_ref[:, c0:c0 + GAB_CHUNK] = _dot_nt(hn, wn_ref[w0:w0 + GAB_CHUNK, :]).astype(gab_ref.dtype)
        return run

    def attn_kv():
        kv = _dot_nt(hn, wn_ref[M_QK_W:M_QK_W + 2 * A_KV_W, :])
        k = kv[:, 0:A_KV_W]
        ksq = k * k
        hi = ksq.astype(BF16)
        lo = (ksq - hi.astype(F32)).astype(BF16)
        ssq = _dot(hi, bd_ref[...]) + _dot(lo, bd_ref[...])
        kva_ref[:, 0:A_KV_W] = (k * lax.rsqrt(ssq * (1.0 / A_HEAD_DIM) + EPS)) * krow_ref[...]
        kva_ref[:, A_KV_W:] = kv[:, A_KV_W:]

    return ([mlstm_qvo, attn_q_and_gates, mlstm_k]
            + [branch_gates(c0) for c0 in range(0, gab_ref.shape[1], GAB_CHUNK)] + [attn_kv])


def _proj_kernel(*refs):
    for piece in _proj_chunks(*refs):
        piece()


def _proj(x2d, nw, wmt, wqgt, wn, bd, qcol, krow, tm):
    n = x2d.shape[0]
    row = lambda w: pl.BlockSpec((tm, w), lambda i: (i, 0))
    n_blk = tm // LANES
    slab = lambda r: pl.BlockSpec((n_blk, r, LANES), lambda i: (i, 0, 0))
    w_qvo = M_QK_W + 2 * M_V_W
    return pl.pallas_call(
        _proj_kernel,
        grid=(n // tm,),
        in_specs=[row(D_MODEL)] + [_const_spec(a.shape) for a in (nw, wmt, wqgt, wn, bd, qcol, krow)],
        out_specs=[slab(w_qvo), slab(2 * M_HEADS), slab(A_Q_W), row(M_QK_W), row(2 * A_KV_W), row(2 * D_MODEL)],
        out_shape=[jax.ShapeDtypeStruct((n // LANES, w_qvo, LANES), BF16),
                   jax.ShapeDtypeStruct((n // LANES, 2 * M_HEADS, LANES), F32),
                   jax.ShapeDtypeStruct((n // LANES, A_Q_W, LANES), BF16),
                   jax.ShapeDtypeStruct((n, M_QK_W), BF16),
                   jax.ShapeDtypeStruct((n, 2 * A_KV_W), F32),
                   jax.ShapeDtypeStruct((n, 2 * D_MODEL), BF16)],
        compiler_params=_params(("arbitrary",)),
        name="proj",
    )(x2d, nw, wmt, wqgt, wn, bd, qcol, krow)


def _split3_rows(x):
    hi = x.astype(BF16).astype(F32)
    r1 = x - hi
    mid = r1.astype(BF16).astype(F32)
    lo = r1 - mid
    return jnp.concatenate([hi, mid, lo], axis=0).astype(BF16)


def _log_sigmoid(x):
    return jnp.minimum(x, 0.0) - jnp.log1p(jnp.exp(-jnp.abs(x)))


def _chunk_masks(chunk_shift):
    s = lax.broadcasted_iota(jnp.int32, (GROUP, GROUP), 0)
    t = lax.broadcasted_iota(jnp.int32, (GROUP, GROUP), 1)
    same = (s >> chunk_shift) == (t >> chunk_shift)
    return same, same & (s <= t)


def _group_gates(gt, same, causal):
    cm_bf = jnp.where(causal, 1.0, 0.0).astype(BF16)
    lf = _log_sigmoid(gt)
    nr = gt.shape[0]
    bt3 = _dot(_split3_rows(lf), cm_bf)
    bt = (bt3[0:nr] + bt3[nr:2 * nr]) + bt3[2 * nr:3 * nr]
    b4 = bt[M_HEADS:2 * M_HEADS]
    a4 = gt[0:M_HEADS] - b4
    a8 = jnp.concatenate([a4, a4], axis=0)
    a_cols = jnp.concatenate([a8, jnp.zeros((GROUP - SUBLANES, GROUP), F32)], axis=0).T
    at_mats, run_rows, chunk_rows = [], [], []
    for h in range(M_HEADS):
        at = jnp.broadcast_to(a_cols[:, h:h + 1], (GROUP, GROUP))
        at_mats.append(at)
        run_rows.append(jnp.max(jnp.where(causal, at, -jnp.inf), axis=0, keepdims=True))
        chunk_rows.append(jnp.max(jnp.where(same, at, -jnp.inf), axis=0, keepdims=True))
    return b4, a4, at_mats, jnp.concatenate(run_rows, axis=0), jnp.concatenate(chunk_rows, axis=0)


def _group_weights(m_prev, b4, a4, run4, chunk4):
    big_m = jnp.maximum(m_prev, run4)
    m_last = jnp.maximum(m_prev, chunk4)
    w_inter = jnp.exp(m_prev - big_m)
    g_vec = jnp.exp(m_prev - m_last)
    w_last = jnp.exp(a4 - m_last)
    m_t = b4 + big_m
    return big_m, w_inter, g_vec, w_last, m_t, jnp.exp(-m_t)


def _group_scores(qvot, ks):
    lane_head = lax.broadcasted_iota(jnp.int32, (1, M_QK_W), 1) >> DK_SHIFT
    row_head = lax.broadcasted_iota(jnp.int32, (M_QK_W, 1), 0) >> DK_SHIFT
    qt = qvot(0, M_QK_W)
    k_stack = jnp.concatenate([jnp.where(lane_head == h, ks, jnp.zeros_like(ks)) for h in range(M_HEADS)], axis=0)
    qw = jnp.concatenate([jnp.where(row_head == h, qt, jnp.zeros_like(qt)) for h in range(M_HEADS)], axis=1)
    zero_blk = jnp.zeros((M_DK, GROUP), BF16)
    sc_t = []
    for h in range(0, M_HEADS, 2):
        q_pair = jnp.concatenate(
            [jnp.concatenate([qvot(h * M_DK, (h + 1) * M_DK), zero_blk], axis=1),
             jnp.concatenate([zero_blk, qvot((h + 1) * M_DK, (h + 2) * M_DK)], axis=1)], axis=0)
        sc = _dot(ks[:, h * M_DK:(h + 2) * M_DK], q_pair)
        sc_t += [sc[:, 0:GROUP], sc[:, GROUP:]]
    return k_stack, qw, sc_t


def _group_values(qvot, sc_t, at_mats, big_m, causal):
    ones_rows = jnp.where(lax.broadcasted_iota(jnp.int32, (BF16_ROWS, GROUP), 0) == 0, 1.0, 0.0).astype(BF16)
    zero_blk = jnp.zeros((GROUP, GROUP), BF16)
    vta, s_t = [], []
    for h in range(M_HEADS):
        w_t = jnp.where(causal, jnp.exp(at_mats[h] - big_m[h:h + 1]), 0.0)
        s_t.append((sc_t[h] * w_t).astype(BF16))
        vta.append(jnp.concatenate([qvot(M_QK_W + h * M_DV, M_QK_W + (h + 1) * M_DV), ones_rows], axis=0))
    intra = []
    for h in range(0, M_HEADS, 2):
        pair = jnp.concatenate([jnp.concatenate([s_t[h], zero_blk], axis=1),
                                jnp.concatenate([zero_blk, s_t[h + 1]], axis=1)], axis=0)
        intra.append(_dot(jnp.concatenate(vta[h:h + 2], axis=1), pair))
    return vta, jnp.concatenate(intra, axis=1)


def _weighted_values(vta, w_rows):
    return jnp.concatenate([(vta[h].astype(F32) * w_rows[h:h + 1]).astype(BF16) for h in range(M_HEADS)], axis=1)


def _lanes_x(rows):
    return jnp.concatenate([rows[h:h + 1] for h in range(M_HEADS)], axis=1)


def _group_out(qvot, inter, intra, w_inter, e_neg_m, nw_ref):
    outs = []
    out_all = inter * _lanes_x(w_inter) + intra
    for h in range(M_HEADS):
        out_t = out_all[:, h * GROUP:(h + 1) * GROUP]
        hh = out_t[0:M_DV] / jnp.maximum(jnp.abs(out_t[M_DV:M_DV + 1]), e_neg_m[h:h + 1])
        ms = jnp.mean(hh * hh, axis=0, keepdims=True)
        hn = (hh * lax.rsqrt(ms + EPS)) * nw_ref[h * M_DV:(h + 1) * M_DV]
        o_t = qvot(M_QK_W + M_V_W + h * M_DV, M_QK_W + M_V_W + (h + 1) * M_DV)
        outs.append(hn * jax.nn.sigmoid(o_t.astype(F32)))
    return jnp.concatenate(outs, axis=0).T


def _decay_row(g_vec, lane0):
    lane_head = lax.broadcasted_iota(jnp.int32, (1, M_QK_W), 1) >> DK_SHIFT
    g_row = jnp.zeros((1, M_QK_W), F32)
    for h in range(M_HEADS):
        g_row = jnp.where(lane_head == h, g_vec[h:h + 1, lane0:lane0 + 1], g_row)
    return g_row


def _mlstm_sample_kernel(seq_len, qvot_ref, gt_ref, k_ref, mrow_ref, bias_ref, nw_ref, c_ref, n_ref,
                         h_ref, c_out_ref, n_out_ref, mt_ref):
    n_seq = GROUP // seq_len
    shift = seq_len.bit_length() - 1
    same, causal = _chunk_masks(shift)
    lane_seq = lax.broadcasted_iota(jnp.int32, (1, GROUP), 1) >> shift
    lane_seq_x = jnp.concatenate([lane_seq] * M_HEADS, axis=1)
    row0 = lax.broadcasted_iota(jnp.int32, (BF16_ROWS, M_QK_W), 0) == 0
    qvot = lambda r0, r1: qvot_ref[0, r0:r1, :]
    ks = k_ref[...] * (M_DK ** -0.5)
    b4, a4, at_mats, run4, chunk4 = _group_gates(gt_ref[0] + bias_ref[...], same, causal)
    big_m, w_inter, g_vec, w_last, m_t, e_neg_m = _group_weights(mrow_ref[0, 0:M_HEADS, :], b4, a4, run4, chunk4)
    mt_ref[0] = jnp.concatenate([m_t, m_t], axis=0)
    k_stack, qw, sc_t = _group_scores(qvot, ks)

    st_old = []
    for s in range(n_seq):
        n_rows = jnp.where(row0, jnp.broadcast_to(n_ref[s:s + 1, :], (BF16_ROWS, M_QK_W)), 0.0)
        st_old.append(jnp.concatenate([c_ref[s].T, n_rows], axis=0))
    inter_all = _dot(jnp.concatenate(st_old, axis=0).astype(BF16), qw)
    vta, intra = _group_values(qvot, sc_t, at_mats, big_m, causal)
    inter = inter_all[0:S_ROWS]
    for s in range(1, n_seq):
        inter = jnp.where(lane_seq_x == s, inter_all[s * S_ROWS:(s + 1) * S_ROWS], inter)
    tall = jnp.concatenate([_weighted_values(vta, jnp.where(lane_seq == s, w_last, 0.0)) for s in range(n_seq)],
                           axis=0)
    d_st = _dot(tall, k_stack)
    for s in range(n_seq):
        st_new = _decay_row(g_vec, s * seq_len) * st_old[s] + d_st[s * S_ROWS:(s + 1) * S_ROWS]
        c_out_ref[s] = st_new[0:M_DV].T
        n_out_ref[s:s + 1, :] = st_new[M_DV:M_DV + 1]
    h_ref[...] = _group_out(qvot, inter, intra, w_inter, e_neg_m, nw_ref).astype(h_ref.dtype)


def _mlstm_sample(qvot, gt, k_m, mrow, bias_col, nw_col, c, n, seq_len):
    ngrp = qvot.shape[0]
    n_seq = GROUP // seq_len
    full = lambda a: pl.BlockSpec(a.shape, lambda i: (0,) * a.ndim)
    slab = lambda a: pl.BlockSpec((1,) + a.shape[1:], lambda i: (i, 0, 0))
    row = lambda w: pl.BlockSpec((GROUP, w), lambda i: (i, 0))
    c_spec = pl.BlockSpec((n_seq,) + c.shape[1:], lambda i: (i, 0, 0))
    n_spec = pl.BlockSpec((n_seq, n.shape[1]), lambda i: (i, 0))
    return pl.pallas_call(
        functools.partial(_mlstm_sample_kernel, seq_len),
        grid=(ngrp,),
        in_specs=[slab(qvot), slab(gt), row(M_QK_W), slab(mrow), full(bias_col), full(nw_col), c_spec, n_spec],
        out_specs=[row(M_V_W), c_spec, n_spec, slab(mrow)],
        out_shape=[jax.ShapeDtypeStruct((ngrp * GROUP, M_V_W), BF16),
                   jax.ShapeDtypeStruct(c.shape, F32),
                   jax.ShapeDtypeStruct(n.shape, F32),
                   jax.ShapeDtypeStruct(mrow.shape, F32)],
        compiler_params=_params(("arbitrary",)),
        name="mlstm_sample",
    )(qvot, gt, k_m, mrow, bias_col, nw_col, c, n)


def _swa_scores(qbs, has_prev, sink_ref, qt_ref, kv_ref, kvp_ref):
    hd = A_HEAD_DIM
    nq = A_GROUPS * WINDOW
    si = lax.broadcasted_iota(jnp.int32, (2 * WINDOW, nq), 0)
    qi = lax.broadcasted_iota(jnp.int32, (2 * WINDOW, nq), 1) & (WINDOW - 1)
    local = ((si < WINDOW) & (si > qi)) | ((si >= WINDOW) & (si - WINDOW <= qi))
    first = local & (has_prev | (si >= WINDOW))
    lane_grp = lax.broadcasted_iota(jnp.int32, (1, nq), 1) >> WINDOW_SHIFT
    kv_block = lambda i: kvp_ref[...] if i == 0 else kv_ref[(i - 1) * WINDOW:i * WINDOW, :]
    k_bf = lambda i: kv_block(i)[:, 0:A_KV_W].astype(BF16)
    vt_bf = lambda i: kv_block(i)[:, A_KV_W:].T.astype(BF16)
    zeros = jnp.zeros((hd, nq), BF16)
    sinks = []
    for kvh in range(A_KV_HEADS):
        sk = jnp.zeros((1, nq), F32)
        for g in range(A_GROUPS):
            sk = jnp.where(lane_grp == g, sink_ref[kvh * A_GROUPS + g], sk)
        sinks.append(sk)
    units = []
    for qb in qbs:
        kk = jnp.concatenate([k_bf(qb), k_bf(qb + 1)], axis=0)
        vt = jnp.concatenate([vt_bf(qb), vt_bf(qb + 1)], axis=1)
        mask = local if qb > 0 else first
        for kvh in range(A_KV_HEADS):
            q4t = jnp.concatenate(
                [qt_ref[qb, (kvh * A_GROUPS + g) * hd:(kvh * A_GROUPS + g + 1) * hd, :] for g in range(A_GROUPS)],
                axis=1)
            wq = jnp.concatenate([q4t, zeros] if kvh == 0 else [zeros, q4t], axis=0)
            units.append((jnp.where(mask, _dot(kk, wq), -jnp.inf), sinks[kvh], vt))
    return units


def _swa_finish(qbs, units, h_ref):
    hd = A_HEAD_DIM
    for i, qb in enumerate(qbs):
        pieces = []
        for kvh in range(A_KV_HEADS):
            s, sk, vt = units[i * A_KV_HEADS + kvh]
            mx = jnp.maximum(jnp.max(s, axis=0, keepdims=True), sk)
            p = jnp.exp(s - mx)
            den = jnp.sum(p, axis=0, keepdims=True) + jnp.exp(sk - mx)
            ot = _dot(vt[kvh * hd:(kvh + 1) * hd], p.astype(BF16)) / den
            pieces += [ot[:, g * WINDOW:(g + 1) * WINDOW] for g in range(A_GROUPS)]
        h_t = jnp.concatenate(pieces, axis=0)
        h_ref[qb * WINDOW:(qb + 1) * WINDOW, :] = h_t.T.astype(h_ref.dtype)


def _front_kernel(tiles_per_seq, n_cast, *refs):
    (sink_ref, x_ref, nw_ref, wmt_ref, wqgt_ref, wn_ref, bd_ref, qcol_ref, krow_ref, bias_ref, nwm_ref) = refs[:11]
    cast_in = refs[11:11 + n_cast]
    (gab_ref, hm_ref, ha_ref, kwin_ref, vwin_ref, st_ref, m_ref) = refs[11 + n_cast:18 + n_cast]
    cast_out = refs[18 + n_cast:18 + 2 * n_cast]
    set_a, set_b = refs[18 + 2 * n_cast:23 + 2 * n_cast], refs[23 + 2 * n_cast:28 + 2 * n_cast]
    st_s, m_s, kvp_s = refs[28 + 2 * n_cast:]
    odd = (pl.program_id(0) & 1) == 1
    for parity, q_set, p_set in ((jnp.logical_not(odd), set_a, set_b), (odd, set_b, set_a)):
        pl.when(parity)(functools.partial(
            _front_body, tiles_per_seq, sink_ref, x_ref, nw_ref, wmt_ref, wqgt_ref, wn_ref, bd_ref, qcol_ref,
            krow_ref, bias_ref, nwm_ref, gab_ref, hm_ref, ha_ref, kwin_ref, vwin_ref, st_ref, m_ref,
            *q_set, *p_set, st_s, m_s, kvp_s))
    for src, dst in zip(cast_in, cast_out):
        dst[...] = src[...].astype(dst.dtype)


def _front_body(tiles_per_seq, sink_ref, x_ref, nw_ref, wmt_ref, wqgt_ref, wn_ref, bd_ref, qcol_ref, krow_ref,
                bias_ref, nwm_ref,
                gab_ref, hm_ref, ha_ref, kwin_ref, vwin_ref, st_ref, m_ref,
                q_qvot, q_gt, q_qat, q_km, q_kva, p_qvot, p_gt, p_qat, p_km, p_kva, st_s, m_s, kvp_s):
    k = pl.program_id(0)

    @pl.when(k == 0)
    def _():
        for r in (p_qvot, p_gt, p_qat, p_km, p_kva, st_s, m_s, kvp_s):
            r[...] = jnp.zeros(r.shape, r.dtype)

    pieces = _proj_chunks(x_ref, nw_ref, wmt_ref, wqgt_ref, wn_ref, bd_ref, qcol_ref, krow_ref,
                          q_qvot, q_gt, q_qat, q_km, q_kva, gab_ref)

    seq_start = lax.rem(k - 1 + tiles_per_seq, tiles_per_seq) == 0
    n_blk = p_qat.shape[0]
    same, causal = _chunk_masks(GROUP.bit_length() - 1)
    st = jnp.where(seq_start, 0.0, st_s[...])
    m_col = jnp.where(seq_start, 0.0, m_s[0:M_HEADS, 0:1])

    grp = []
    for g in range(n_blk):
        qvot = lambda r0, r1, g=g: p_qvot[g, r0:r1, :]
        ks = p_km[g * GROUP:(g + 1) * GROUP, :] * (M_DK ** -0.5)
        b4, a4, at_mats, run4, chunk4 = _group_gates(p_gt[g] + bias_ref[...], same, causal)
        weights = _group_weights(jnp.broadcast_to(m_col, (M_HEADS, GROUP)), b4, a4, run4, chunk4)
        m_col = weights[4][:, GROUP - 1:GROUP]
        grp.append((qvot, at_mats, weights) + _group_scores(qvot, ks))
    has_prev = jnp.logical_not(seq_start)
    qb_lo, qb_hi = list(range(0, n_blk // 2)), list(range(n_blk // 2, n_blk))
    units_lo = _swa_scores(qb_lo, has_prev, sink_ref, p_qat, p_kva, kvp_s)
    inter = [_dot(st.astype(BF16), grp[0][4])]
    for piece in pieces[0:1]:
        piece()

    vals = []
    for qvot, at_mats, (big_m, _, _, w_last, _, _), k_stack, _, sc_t in grp:
        vta, intra = _group_values(qvot, sc_t, at_mats, big_m, causal)
        vals.append((intra, _dot(_weighted_values(vta, w_last), k_stack)))
    _swa_finish(qb_lo, units_lo, ha_ref)
    units_hi = _swa_scores(qb_hi, has_prev, sink_ref, p_qat, p_kva, kvp_s)
    for piece in pieces[1:3]:
        piece()

    for g in range(n_blk):
        st = _decay_row(grp[g][2][2], 0) * st + vals[g][1]
        if g + 1 < n_blk:
            inter.append(_dot(st.astype(BF16), grp[g + 1][4]))
    _swa_finish(qb_hi, units_hi, ha_ref)
    for piece in pieces[3:]:
        piece()

    for g in range(n_blk):
        qvot, _, (_, w_inter, _, _, _, e_neg_m) = grp[g][0:3]
        hm_ref[g * GROUP:(g + 1) * GROUP, :] = _group_out(qvot, inter[g], vals[g][0], w_inter, e_neg_m,
                                                          nwm_ref).astype(hm_ref.dtype)
    st_s[...] = st
    st_ref[0] = st
    m_rows = jnp.broadcast_to(m_col, (M_HEADS, LANES))
    m_rows = jnp.concatenate([m_rows, m_rows], axis=0)
    m_s[...] = m_rows
    m_ref[0] = m_rows
    tm = p_kva.shape[0]
    kwin_ref[0] = p_kva[tm - WINDOW:, 0:A_KV_W]
    vwin_ref[0] = p_kva[tm - WINDOW:, A_KV_W:]
    kvp_s[...] = p_kva[tm - WINDOW:, :]


def _cast_rows(rows, n_steps):
    rb = BF16_ROWS
    while rows % rb or rows // rb > n_steps:
        rb += BF16_ROWS
    return rb


def _front(x2d, nb, sinks, nw, wmt, wqgt, wn, bd, qcol, krow, bias_col, nwm_col, cast_ws, tm):
    n = x2d.shape[0]
    n_tiles = n // tm
    tps = n_tiles // nb
    n_blk = tm // LANES
    w_qvo = M_QK_W + 2 * M_V_W
    cur = lambda w: pl.BlockSpec((tm, w), lambda k: (jnp.minimum(k, n_tiles - 1), 0))
    prev = lambda w: pl.BlockSpec((tm, w), lambda k: (jnp.maximum(k - 1, 0), 0))
    per_seq = lambda r, w: pl.BlockSpec((1, r, w), lambda k: (jnp.maximum(k - 1, 0) // tps, 0, 0))

    def cast_spec(a):
        rb = _cast_rows(a.shape[0], n_tiles)
        return pl.BlockSpec((rb, a.shape[1]), lambda k: (jnp.minimum(k, a.shape[0] // rb - 1), 0))

    cast_in_specs = [cast_spec(a) for a in cast_ws]
    cast_out_specs = [cast_spec(a) for a in cast_ws]
    proj_scratch = [pltpu.VMEM((n_blk, w_qvo, LANES), BF16), pltpu.VMEM((n_blk, 2 * M_HEADS, LANES), F32),
                    pltpu.VMEM((n_blk, A_Q_W, LANES), BF16), pltpu.VMEM((tm, M_QK_W), BF16),
                    pltpu.VMEM((tm, 2 * A_KV_W), F32)]
    return pl.pallas_call(
        functools.partial(_front_kernel, tps, len(cast_ws)),
        grid=(n_tiles + 1,),
        in_specs=[pl.BlockSpec(memory_space=pltpu.SMEM), cur(D_MODEL)]
                 + [_const_spec(a.shape) for a in (nw, wmt, wqgt, wn, bd, qcol, krow, bias_col, nwm_col)]
                 + cast_in_specs,
        out_specs=[cur(2 * D_MODEL), prev(M_V_W), prev(A_Q_W), per_seq(WINDOW, A_KV_W), per_seq(WINDOW, A_KV_W),
                   per_seq(S_ROWS, M_QK_W), per_seq(SUBLANES, LANES)] + cast_out_specs,
        out_shape=[jax.ShapeDtypeStruct((n, 2 * D_MODEL), BF16),
                   jax.ShapeDtypeStruct((n, M_V_W), BF16),
                   jax.ShapeDtypeStruct((n, A_Q_W), BF16),
                   jax.ShapeDtypeStruct((nb, WINDOW, A_KV_W), F32),
                   jax.ShapeDtypeStruct((nb, WINDOW, A_KV_W), F32),
                   jax.ShapeDtypeStruct((nb, S_ROWS, M_QK_W), F32),
                   jax.ShapeDtypeStruct((nb, SUBLANES, LANES), F32)]
                  + [jax.ShapeDtypeStruct(a.shape, BF16) for a in cast_ws],
        scratch_shapes=proj_scratch + proj_scratch + [pltpu.VMEM((S_ROWS, M_QK_W), F32),
                                                       pltpu.VMEM((SUBLANES, LANES), F32),
                                                       pltpu.VMEM((WINDOW, 2 * A_KV_W), F32)],
        compiler_params=_params(("arbitrary",)),
        name="front",
    )(sinks, x2d, nw, wmt, wqgt, wn, bd, qcol, krow, bias_col, nwm_col, *cast_ws)


def _swa_sample_kernel(seq_len, sink_ref, qt_ref, kv_ref, ck_ref, cv_ref, h_ref, kwin_ref, vwin_ref):
    n_seq = GROUP // seq_len
    wb = ck_ref.shape[2]
    lane = lax.broadcasted_iota(jnp.int32, (1, LANES), 1)
    lo_half = lane < HALF
    q_rows = qt_ref[0].astype(F32).T
    kv_new = kv_ref[...]
    k_new = kv_new[:, 0:A_KV_W].reshape(n_seq, seq_len, A_KV_W)
    v_new = kv_new[:, A_KV_W:].reshape(n_seq, seq_len, A_KV_W)
    kt_new, vt_new = kv_new[:, 0:A_KV_W].T, kv_new[:, A_KV_W:].T
    for s in range(n_seq):
        put = (wb - seq_len - s * seq_len) % LANES
        for new_t, c_ref, win_ref in ((kt_new, ck_ref, kwin_ref), (vt_new, cv_ref, vwin_ref)):
            win_ref[s] = jnp.where(lane >= wb - seq_len, pltpu.roll(new_t, put, axis=1) if put else new_t,
                                   pltpu.roll(c_ref[s], wb - seq_len, axis=1))

    def to_kv_half(x, head):
        kvh = head // A_GROUPS
        if head % 2 != kvh:
            x = pltpu.roll(x, HALF, axis=1)
        return jnp.where(lo_half if kvh == 0 else ~lo_half, x, 0.0)

    lhs = jnp.concatenate(
        [to_kv_half(q_rows[:, (h // 2) * LANES:(h // 2 + 1) * LANES], h).reshape(n_seq, seq_len, LANES)
         for h in range(A_HEADS)], axis=1).astype(BF16)
    zpad = jnp.zeros((n_seq, BF16_ROWS - seq_len, A_KV_W), F32)
    k_nb = jnp.concatenate([k_new, zpad], axis=1).astype(BF16)
    v_nb = jnp.concatenate([v_new, zpad], axis=1).astype(BF16)
    s_c = jnp.einsum('sqf,sfk->sqk', lhs, ck_ref[...].astype(BF16), preferred_element_type=F32)
    s_n = jnp.einsum('sqf,skf->sqk', lhs, k_nb, preferred_element_type=F32)
    nrow = A_HEADS * seq_len
    ti = lax.broadcasted_iota(jnp.int32, (nrow, wb), 0) & (seq_len - 1)
    ki = lax.broadcasted_iota(jnp.int32, (nrow, wb), 1)
    mask_c = (ti + wb - ki) < WINDOW
    ti_n = lax.broadcasted_iota(jnp.int32, (nrow, BF16_ROWS), 0) & (seq_len - 1)
    ki_n = lax.broadcasted_iota(jnp.int32, (nrow, BF16_ROWS), 1)
    mask_n = ki_n <= ti_n
    row_head = lax.broadcasted_iota(jnp.int32, (nrow, 1), 0) >> (seq_len.bit_length() - 1)
    sk = jnp.zeros((nrow, 1), F32)
    for h in range(A_HEADS):
        sk = jnp.where(row_head == h, sink_ref[h], sk)
    s_c = jnp.where(mask_c, s_c, -jnp.inf)
    s_n = jnp.where(mask_n, s_n, -jnp.inf)
    mx = jnp.maximum(jnp.maximum(jnp.max(s_c, axis=-1, keepdims=True), jnp.max(s_n, axis=-1, keepdims=True)), sk)
    p_c = jnp.exp(s_c - mx)
    p_n = jnp.exp(s_n - mx)
    den = jnp.sum(p_c, axis=-1, keepdims=True) + jnp.sum(p_n, axis=-1, keepdims=True) + jnp.exp(sk - mx)
    o = (jnp.einsum('sqk,sfk->sqf', p_c.astype(BF16), cv_ref[...].astype(BF16), preferred_element_type=F32)
         + jnp.einsum('sqk,skf->sqf', p_n.astype(BF16), v_nb, preferred_element_type=F32)) / den

    def from_kv_half(head):
        x = o[:, head * seq_len:(head + 1) * seq_len, :].reshape(GROUP, LANES)
        return pltpu.roll(x, HALF, axis=1) if head % 2 != head // A_GROUPS else x

    for c in range(A_HEADS // 2):
        h_ref[:, c * LANES:(c + 1) * LANES] = jnp.where(lo_half, from_kv_half(2 * c),
                                                        from_kv_half(2 * c + 1)).astype(h_ref.dtype)


def _swa_sample(qat, kv_a, cache_k, cache_v, sinks, seq_len):
    ngrp = qat.shape[0]
    n_seq = GROUP // seq_len
    wb = cache_k.shape[2]
    assert wb == LANES, "window positions fill one lane-width"
    row = lambda w: pl.BlockSpec((GROUP, w), lambda i: (i, 0))
    cache = pl.BlockSpec((n_seq, A_KV_W, wb), lambda i: (i, 0, 0))
    return pl.pallas_call(
        functools.partial(_swa_sample_kernel, seq_len),
        grid=(ngrp,),
        in_specs=[pl.BlockSpec(memory_space=pltpu.SMEM), pl.BlockSpec((1, A_Q_W, LANES), lambda i: (i, 0, 0)),
                  row(2 * A_KV_W), cache, cache],
        out_specs=[row(A_Q_W), cache, cache],
        out_shape=[jax.ShapeDtypeStruct((ngrp * GROUP, A_Q_W), BF16),
                   jax.ShapeDtypeStruct(cache_k.shape, F32),
                   jax.ShapeDtypeStruct(cache_v.shape, F32)],
        compiler_params=_params(("arbitrary",)),
        name="swa_sample",
    )(sinks, qat, kv_a, cache_k, cache_v)


def _merge_ffn_kernel(x_ref, hm_ref, ha_ref, gab_ref, wa_ref, wb_ref, wo_ref, nw_ref, wg_ref, wu_ref, wd_ref,
                      y_ref):
    ga = jax.nn.sigmoid(gab_ref[:, 0:D_MODEL].astype(F32))
    gb = jax.nn.sigmoid(gab_ref[:, D_MODEL:].astype(F32))
    mix = ga * _dot(hm_ref[...], wa_ref[...]) + gb * _dot(ha_ref[...], wb_ref[...])
    x1 = x_ref[...] + _dot(mix.astype(BF16), wo_ref[...])
    hf = _rms_rows(x1, nw_ref[...]).astype(BF16)
    gate = _dot(hf, wg_ref[...])
    up = _dot(hf, wu_ref[...])
    act = (jax.nn.silu(gate) * up).astype(BF16)
    y_ref[...] = x1 + _dot(act, wd_ref[...])


def _merge_ffn(x2d, h_m, h_a, g_ab, wa, wb, wo, nw, wg, wu, wd, tm):
    n = x2d.shape[0]
    row = lambda w: pl.BlockSpec((tm, w), lambda i: (i, 0))
    return pl.pallas_call(
        _merge_ffn_kernel,
        grid=(n // tm,),
        in_specs=[row(D_MODEL), row(M_V_W), row(A_Q_W), row(2 * D_MODEL)]
                 + [_const_spec(w.shape) for w in (wa, wb, wo, nw, wg, wu, wd)],
        out_specs=row(D_MODEL),
        out_shape=jax.ShapeDtypeStruct((n, D_MODEL), F32),
        compiler_params=_params(("arbitrary",)),
        name="merge_ffn",
    )(x2d, h_m, h_a, g_ab, wa, wb, wo, nw, wg, wu, wd)


def kernel(x_prompt, x_sample, state_mlstm_C, state_mlstm_n, state_mlstm_m, cache_swa_k, cache_swa_v,
           norm_mix_w, w_in, mlstm_i_bias, mlstm_f_bias, mlstm_norm_w, q_norm_w, k_norm_w, attn_sinks,
           w_branch_a, w_branch_b, w_out, norm_ffn_w, w_gate, w_up, w_down):
    depth = w_in.shape[0]
    assert depth == 1, "single trunk layer"
    l = 0
    bp, tp = x_prompt.shape[0], x_prompt.shape[1]
    bs, ts = x_sample.shape[0], x_sample.shape[1]
    assert tp % 512 == 0 and (bs * ts) % GROUP == 0 and GROUP % ts == 0 and ts & (ts - 1) == 0
    assert ts <= SUBLANES, "sample chunk must fit one sublane tile"

    wt = jnp.transpose(w_in[l])
    c_km, c_vm = M_QK_W, 2 * M_QK_W
    c_g = 2 * M_QK_W + 2 * M_V_W
    c_qa = c_g + 2 * M_HEADS
    c_ka = c_qa + A_Q_W
    gate_pad = jnp.zeros((BF16_ROWS - 2 * M_HEADS, D_MODEL), F32)
    wmt = jnp.concatenate([wt[0:c_km], wt[c_vm:c_g]], axis=0).astype(BF16)
    wqgt = jnp.concatenate([wt[c_qa:c_ka], wt[c_g:c_qa], gate_pad], axis=0).astype(BF16)
    wn = jnp.concatenate([wt[c_km:c_vm], wt[c_ka:]], axis=0).astype(BF16)
    head_of = jnp.arange(A_KV_W) // A_HEAD_DIM
    bd = (head_of[:, None] == head_of[None, :]).astype(BF16)
    qcol = (jnp.tile(q_norm_w[l], A_HEADS) * (A_HEAD_DIM ** -0.5)).reshape(A_Q_W, 1)
    krow = jnp.tile(k_norm_w[l], A_KV_HEADS).reshape(1, A_KV_W)
    nw_mix = norm_mix_w[l].reshape(1, D_MODEL)
    nw_ffn = norm_ffn_w[l].reshape(1, D_MODEL)
    bias_col = jnp.concatenate([mlstm_i_bias[l], mlstm_f_bias[l]]).reshape(2 * M_HEADS, 1)
    nw_col = mlstm_norm_w[l].reshape(M_V_W, 1)
    sinks = attn_sinks[l]
    proj = lambda x2d, tm: _proj(x2d, nw_mix, wmt, wqgt, wn, bd, qcol, krow, tm)

    xp = x_prompt.reshape(bp * tp, D_MODEL)
    later_ws = (w_branch_a[l], w_branch_b[l], w_out[l], w_gate[l], w_up[l], w_down[l])
    g_ab, h_m, h_a, kwin_p, vwin_p, st_p, m_p, wa, wb, wo, wg, wu, wd = _front(
        xp, bp, sinks, nw_mix, wmt, wqgt, wn, bd, qcol, krow, bias_col, nw_col, later_ws, 512)
    merge = lambda x2d, h_m, h_a, g_ab, tm: _merge_ffn(x2d, h_m, h_a, g_ab, wa, wb, wo, nw_ffn, wg, wu, wd, tm)
    yp = merge(xp, h_m, h_a, g_ab, 512).reshape(bp, tp, D_MODEL)
    c_p = jnp.swapaxes(st_p[:, :M_DV, :], 1, 2).reshape(bp, M_HEADS, M_DK, M_DV)
    n_p = st_p[:, M_DV, :].reshape(bp, M_HEADS, M_DK)
    m_pr = m_p[:, :M_HEADS, 0]

    ns = bs * ts
    xs = x_sample.reshape(ns, D_MODEL)
    tms = 512 if ns % 512 == 0 else GROUP
    qvot, gt, qat, k_m, kv_a, g_ab = proj(xs, tms)
    ngrp = ns // GROUP
    m_lanes = jnp.repeat(state_mlstm_m[l], ts, axis=0).reshape(ngrp, GROUP, M_HEADS)
    mrow = jnp.pad(jnp.swapaxes(m_lanes, 1, 2), ((0, 0), (0, SUBLANES - M_HEADS), (0, 0)))
    h_m, c_s, n_s, mt_s = _mlstm_sample(qvot, gt, k_m, mrow, bias_col, nw_col,
                                        state_mlstm_C[l].reshape(bs, M_QK_W, M_DV),
                                        state_mlstm_n[l].reshape(bs, M_QK_W), ts)
    wbuf = cache_swa_k.shape[2]
    to_fm = lambda a: jnp.transpose(a, (0, 2, 3, 1)).reshape(bs, A_KV_W, wbuf)
    from_fm = lambda a: jnp.transpose(a.reshape(bs, A_KV_HEADS, A_HEAD_DIM, wbuf), (0, 3, 1, 2))[None]
    h_a, kwin_s, vwin_s = _swa_sample(qat, kv_a, to_fm(cache_swa_k[l]), to_fm(cache_swa_v[l]), sinks, ts)
    ys = merge(xs, h_m, h_a, g_ab, tms).reshape(bs, ts, D_MODEL)
    m_s = jnp.swapaxes(mt_s[:, :M_HEADS, :], 1, 2).reshape(bs, ts, M_HEADS)[:, ts - 1, :]

    kv5 = lambda a: a.reshape(a.shape[0], a.shape[1], A_KV_HEADS, A_HEAD_DIM)[None]
    return (yp, ys,
            c_p[None], n_p[None], m_pr[None], kv5(kwin_p), kv5(vwin_p),
            c_s.reshape(bs, M_HEADS, M_DK, M_DV)[None], n_s.reshape(bs, M_HEADS, M_DK)[None], m_s[None],
            from_fm(kwin_s), from_fm(vwin_s))
```

```python
import functools

import jax
import jax.numpy as jnp
from jax import lax
from jax.experimental import pallas as pl
from jax.experimental.pallas import tpu as pltpu

F32 = jnp.float32
BF16 = jnp.bfloat16

D_MODEL = 1024
M_HEADS = 4
M_DK = 64
M_DV = 128
M_CHUNK = 64
M_QK_W = M_HEADS * M_DK
M_V_W = M_HEADS * M_DV
A_HEADS = 8
A_KV_HEADS = 2
A_HEAD_DIM = 64
A_GROUPS = A_HEADS // A_KV_HEADS
A_Q_W = A_HEADS * A_HEAD_DIM
A_KV_W = A_KV_HEADS * A_HEAD_DIM
WINDOW = 128
D_FF = 2816
EPS = 1e-6

LANES = 128
SUBLANES = 8
BF16_ROWS = 16
GROUP = 128
S_ROWS = M_DV + BF16_ROWS
VMEM_LIMIT = 56 * 1024 * 1024
DK_SHIFT = M_DK.bit_length() - 1
WINDOW_SHIFT = WINDOW.bit_length() - 1
HALF = LANES // 2
assert A_HEAD_DIM == HALF and A_KV_W == LANES, "attention head pairs share one lane-width"

NT_DIMS = (((1,), (1,)), ((), ()))


def _dot(a, b):
    return jnp.dot(a, b, preferred_element_type=F32)


def _dot_nt(a, b):
    return lax.dot_general(a, b, NT_DIMS, preferred_element_type=F32)


def _const_spec(shape):
    nd = len(shape)
    return pl.BlockSpec(shape, lambda *_: (0,) * nd, pipeline_mode=pl.Buffered(1))


def _params(sem):
    return pltpu.CompilerParams(dimension_semantics=sem, vmem_limit_bytes=VMEM_LIMIT)


def _rms_rows(x, nw):
    ms = jnp.mean(x * x, axis=-1, keepdims=True)
    return (x * lax.rsqrt(ms + EPS)) * nw


GAB_CHUNK = 512


def _proj_chunks(x_ref, nw_ref, wmt_ref, wqgt_ref, wn_ref, bd_ref, qcol_ref, krow_ref,
                 qvot_ref, gt_ref, qat_ref, km_ref, kva_ref, gab_ref):
    hn = _rms_rows(x_ref[...], nw_ref[...]).astype(BF16)
    n_blk = qat_ref.shape[0]

    def put(ref, rows, val):
        for c in range(n_blk):
            ref[c, rows, :] = val[:, c * LANES:(c + 1) * LANES].astype(ref.dtype)

    def mlstm_qvo():
        put(qvot_ref, slice(None), _dot_nt(wmt_ref[...], hn))

    def attn_q_and_gates():
        qt = _dot_nt(wqgt_ref[...], hn)
        put(gt_ref, slice(None), qt[A_Q_W:A_Q_W + 2 * M_HEADS])
        for h in range(A_HEADS):
            hs = slice(h * A_HEAD_DIM, (h + 1) * A_HEAD_DIM)
            blk = qt[hs]
            ssq_q = jnp.sum(blk * blk, axis=0, keepdims=True)
            put(qat_ref, hs, (blk * lax.rsqrt(ssq_q * (1.0 / A_HEAD_DIM) + EPS)) * qcol_ref[hs])

    def mlstm_k():
        km_ref[...] = _dot_nt(hn, wn_ref[0:M_QK_W, :]).astype(km_ref.dtype)

    def branch_gates(c0):
        def run():
            w0 = M_QK_W + 2 * A_KV_W + c0
            gab_ref[:, c0:c0 + GAB_CHUNK] = _dot_nt(hn, wn_ref[w0:w0 + GAB_CHUNK, :]).astype(gab_ref.dtype)
        return run

    def attn_kv():
        kv = _dot_nt(hn, wn_ref[M_QK_W:M_QK_W + 2 * A_KV_W, :])
        k = kv[:, 0:A_KV_W]
        ksq = k * k
        hi = ksq.astype(BF16)
        lo = (ksq - hi.astype(F32)).astype(BF16)
        ssq = _dot(hi, bd_ref[...]) + _dot(lo, bd_ref[...])
        kva_ref[:, 0:A_KV_W] = (k * lax.rsqrt(ssq * (1.0 / A_HEAD_DIM) + EPS)) * krow_ref[...]
        kva_ref[:, A_KV_W:] = kv[:, A_KV_W:]

    return ([mlstm_qvo, attn_q_and_gates, mlstm_k]
            + [branch_gates(c0) for c0 in range(0, gab_ref.shape[1], GAB_CHUNK)] + [attn_kv])


def _proj_kernel(*refs):
    for piece in _proj_chunks(*refs):
        piece()


def _proj(x2d, nw, wmt, wqgt, wn, bd, qcol, krow, tm):
    n = x2d.shape[0]
    row = lambda w: pl.BlockSpec((tm, w), lambda i: (i, 0))
    n_blk = tm // LANES
    slab = lambda r: pl.BlockSpec((n_blk, r, LANES), lambda i: (i, 0, 0))
    w_qvo = M_QK_W + 2 * M_V_W
    return pl.pallas_call(
        _proj_kernel,
        grid=(n // tm,),
        in_specs=[row(D_MODEL)] + [_const_spec(a.shape) for a in (nw, wmt, wqgt, wn, bd, qcol, krow)],
        out_specs=[slab(w_qvo), slab(2 * M_HEADS), slab(A_Q_W), row(M_QK_W), row(2 * A_KV_W), row(2 * D_MODEL)],
        out_shape=[jax.ShapeDtypeStruct((n // LANES, w_qvo, LANES), BF16),
                   jax.ShapeDtypeStruct((n // LANES, 2 * M_HEADS, LANES), F32),
                   jax.ShapeDtypeStruct((n // LANES, A_Q_W, LANES), BF16),
                   jax.ShapeDtypeStruct((n, M_QK_W), BF16),
                   jax.ShapeDtypeStruct((n, 2 * A_KV_W), F32),
                   jax.ShapeDtypeStruct((n, 2 * D_MODEL), BF16)],
        compiler_params=_params(("arbitrary",)),
        name="proj",
    )(x2d, nw, wmt, wqgt, wn, bd, qcol, krow)


def _split3_rows(x):
    hi = x.astype(BF16).astype(F32)
    r1 = x - hi
    mid = r1.astype(BF16).astype(F32)
    lo = r1 - mid
    return jnp.concatenate([hi, mid, lo], axis=0).astype(BF16)


def _log_sigmoid(x):
    return jnp.minimum(x, 0.0) - jnp.log1p(jnp.exp(-jnp.abs(x)))


def _chunk_masks(chunk_shift):
    s = lax.broadcasted_iota(jnp.int32, (GROUP, GROUP), 0)
    t = lax.broadcasted_iota(jnp.int32, (GROUP, GROUP), 1)
    same = (s >> chunk_shift) == (t >> chunk_shift)
    return same, same & (s <= t)


def _group_gates(gt, same, causal):
    cm_bf = jnp.where(causal, 1.0, 0.0).astype(BF16)
    lf = _log_sigmoid(gt)
    nr = gt.shape[0]
    bt3 = _dot(_split3_rows(lf), cm_bf)
    bt = (bt3[0:nr] + bt3[nr:2 * nr]) + bt3[2 * nr:3 * nr]
    b4 = bt[M_HEADS:2 * M_HEADS]
    a4 = gt[0:M_HEADS] - b4
    a8 = jnp.concatenate([a4, a4], axis=0)
    a_cols = jnp.concatenate([a8, jnp.zeros((GROUP - SUBLANES, GROUP), F32)], axis=0).T
    at_mats, run_rows, chunk_rows = [], [], []
    for h in range(M_HEADS):
        at = jnp.broadcast_to(a_cols[:, h:h + 1], (GROUP, GROUP))
        at_mats.append(at)
        run_rows.append(jnp.max(jnp.where(causal, at, -jnp.inf), axis=0, keepdims=True))
        chunk_rows.append(jnp.max(jnp.where(same, at, -jnp.inf), axis=0, keepdims=True))
    return b4, a4, at_mats, jnp.concatenate(run_rows, axis=0), jnp.concatenate(chunk_rows, axis=0)


def _group_weights(m_prev, b4, a4, run4, chunk4):
    big_m = jnp.maximum(m_prev, run4)
    m_last = jnp.maximum(m_prev, chunk4)
    w_inter = jnp.exp(m_prev - big_m)
    g_vec = jnp.exp(m_prev - m_last)
    w_last = jnp.exp(a4 - m_last)
    m_t = b4 + big_m
    return big_m, w_inter, g_vec, w_last, m_t, jnp.exp(-m_t)


def _group_scores(qvot, ks):
    lane_head = lax.broadcasted_iota(jnp.int32, (1, M_QK_W), 1) >> DK_SHIFT
    row_head = lax.broadcasted_iota(jnp.int32, (M_QK_W, 1), 0) >> DK_SHIFT
    qt = qvot(0, M_QK_W)
    k_stack = jnp.concatenate([jnp.where(lane_head == h, ks, jnp.zeros_like(ks)) for h in range(M_HEADS)], axis=0)
    qw = jnp.concatenate([jnp.where(row_head == h, qt, jnp.zeros_like(qt)) for h in range(M_HEADS)], axis=1)
    zero_blk = jnp.zeros((M_DK, GROUP), BF16)
    sc_t = []
    for h in range(0, M_HEADS, 2):
        q_pair = jnp.concatenate(
            [jnp.concatenate([qvot(h * M_DK, (h + 1) * M_DK), zero_blk], axis=1),
             jnp.concatenate([zero_blk, qvot((h + 1) * M_DK, (h + 2) * M_DK)], axis=1)], axis=0)
        sc = _dot(ks[:, h * M_DK:(h + 2) * M_DK], q_pair)
        sc_t += [sc[:, 0:GROUP], sc[:, GROUP:]]
    return k_stack, qw, sc_t


def _group_values(qvot, sc_t, at_mats, big_m, causal):
    ones_rows = jnp.where(lax.broadcasted_iota(jnp.int32, (BF16_ROWS, GROUP), 0) == 0, 1.0, 0.0).astype(BF16)
    zero_blk = jnp.zeros((GROUP, GROUP), BF16)
    vta, s_t = [], []
    for h in range(M_HEADS):
        w_t = jnp.where(causal, jnp.exp(at_mats[h] - big_m[h:h + 1]), 0.0)
        s_t.append((sc_t[h] * w_t).astype(BF16))
        vta.append(jnp.concatenate([qvot(M_QK_W + h * M_DV, M_QK_W + (h + 1) * M_DV), ones_rows], axis=0))
    intra = []
    for h in range(0, M_HEADS, 2):
        pair = jnp.concatenate([jnp.concatenate([s_t[h], zero_blk], axis=1),
                                jnp.concatenate([zero_blk, s_t[h + 1]], axis=1)], axis=0)
        intra.append(_dot(jnp.concatenate(vta[h:h + 2], axis=1), pair))
    return vta, jnp.concatenate(intra, axis=1)


def _weighted_values(vta, w_rows):
    return jnp.concatenate([(vta[h].astype(F32) * w_rows[h:h + 1]).astype(BF16) for h in range(M_HEADS)], axis=1)


def _lanes_x(rows):
    return jnp.concatenate([rows[h:h + 1] for h in range(M_HEADS)], axis=1)


def _group_out(qvot, inter, intra, w_inter, e_neg_m, nw_ref):
    outs = []
    out_all = inter * _lanes_x(w_inter) + intra
    for h in range(M_HEADS):
        out_t = out_all[:, h * GROUP:(h + 1) * GROUP]
        hh = out_t[0:M_DV] / jnp.maximum(jnp.abs(out_t[M_DV:M_DV + 1]), e_neg_m[h:h + 1])
        ms = jnp.mean(hh * hh, axis=0, keepdims=True)
        hn = (hh * lax.rsqrt(ms + EPS)) * nw_ref[h * M_DV:(h + 1) * M_DV]
        o_t = qvot(M_QK_W + M_V_W + h * M_DV, M_QK_W + M_V_W + (h + 1) * M_DV)
        outs.append(hn * jax.nn.sigmoid(o_t.astype(F32)))
    return jnp.concatenate(outs, axis=0).T


def _decay_row(g_vec, lane0):
    lane_head = lax.broadcasted_iota(jnp.int32, (1, M_QK_W), 1) >> DK_SHIFT
    g_row = jnp.zeros((1, M_QK_W), F32)
    for h in range(M_HEADS):
        g_row = jnp.where(lane_head == h, g_vec[h:h + 1, lane0:lane0 + 1], g_row)
    return g_row


def _mlstm_sample_kernel(seq_len, qvot_ref, gt_ref, k_ref, mrow_ref, bias_ref, nw_ref, c_ref, n_ref,
                         h_ref, c_out_ref, n_out_ref, mt_ref):
    n_seq = GROUP // seq_len
    shift = seq_len.bit_length() - 1
    same, causal = _chunk_masks(shift)
    lane_seq = lax.broadcasted_iota(jnp.int32, (1, GROUP), 1) >> shift
    lane_seq_x = jnp.concatenate([lane_seq] * M_HEADS, axis=1)
    row0 = lax.broadcasted_iota(jnp.int32, (BF16_ROWS, M_QK_W), 0) == 0
    qvot = lambda r0, r1: qvot_ref[0, r0:r1, :]
    ks = k_ref[...] * (M_DK ** -0.5)
    b4, a4, at_mats, run4, chunk4 = _group_gates(gt_ref[0] + bias_ref[...], same, causal)
    big_m, w_inter, g_vec, w_last, m_t, e_neg_m = _group_weights(mrow_ref[0, 0:M_HEADS, :], b4, a4, run4, chunk4)
    mt_ref[0] = jnp.concatenate([m_t, m_t], axis=0)
    k_stack, qw, sc_t = _group_scores(qvot, ks)

    st_old = []
    for s in range(n_seq):
        n_rows = jnp.where(row0, jnp.broadcast_to(n_ref[s:s + 1, :], (BF16_ROWS, M_QK_W)), 0.0)
        st_old.append(jnp.concatenate([c_ref[s].T, n_rows], axis=0))
    inter_all = _dot(jnp.concatenate(st_old, axis=0).astype(BF16), qw)
    vta, intra = _group_values(qvot, sc_t, at_mats, big_m, causal)
    inter = inter_all[0:S_ROWS]
    for s in range(1, n_seq):
        inter = jnp.where(lane_seq_x == s, inter_all[s * S_ROWS:(s + 1) * S_ROWS], inter)
    tall = jnp.concatenate([_weighted_values(vta, jnp.where(lane_seq == s, w_last, 0.0)) for s in range(n_seq)],
                           axis=0)
    d_st = _dot(tall, k_stack)
    for s in range(n_seq):
        st_new = _decay_row(g_vec, s * seq_len) * st_old[s] + d_st[s * S_ROWS:(s + 1) * S_ROWS]
        c_out_ref[s] = st_new[0:M_DV].T
        n_out_ref[s:s + 1, :] = st_new[M_DV:M_DV + 1]
    h_ref[...] = _group_out(qvot, inter, intra, w_inter, e_neg_m, nw_ref).astype(h_ref.dtype)


def _mlstm_sample(qvot, gt, k_m, mrow, bias_col, nw_col, c, n, seq_len):
    ngrp = qvot.shape[0]
    n_seq = GROUP // seq_len
    full = lambda a: pl.BlockSpec(a.shape, lambda i: (0,) * a.ndim)
    slab = lambda a: pl.BlockSpec((1,) + a.shape[1:], lambda i: (i, 0, 0))
    row = lambda w: pl.BlockSpec((GROUP, w), lambda i: (i, 0))
    c_spec = pl.BlockSpec((n_seq,) + c.shape[1:], lambda i: (i, 0, 0))
    n_spec = pl.BlockSpec((n_seq, n.shape[1]), lambda i: (i, 0))
    return pl.pallas_call(
        functools.partial(_mlstm_sample_kernel, seq_len),
        grid=(ngrp,),
        in_specs=[slab(qvot), slab(gt), row(M_QK_W), slab(mrow), full(bias_col), full(nw_col), c_spec, n_spec],
        out_specs=[row(M_V_W), c_spec, n_spec, slab(mrow)],
        out_shape=[jax.ShapeDtypeStruct((ngrp * GROUP, M_V_W), BF16),
                   jax.ShapeDtypeStruct(c.shape, F32),
                   jax.ShapeDtypeStruct(n.shape, F32),
                   jax.ShapeDtypeStruct(mrow.shape, F32)],
        compiler_params=_params(("arbitrary",)),
        name="mlstm_sample",
    )(qvot, gt, k_m, mrow, bias_col, nw_col, c, n)


def _swa_scores(qbs, has_prev, sink_ref, qt_ref, kv_ref, kvp_ref):
    hd = A_HEAD_DIM
    nq = A_GROUPS * WINDOW
    si = lax.broadcasted_iota(jnp.int32, (2 * WINDOW, nq), 0)
    qi = lax.broadcasted_iota(jnp.int32, (2 * WINDOW, nq), 1) & (WINDOW - 1)
    local = ((si < WINDOW) & (si > qi)) | ((si >= WINDOW) & (si - WINDOW <= qi))
    first = local & (has_prev | (si >= WINDOW))
    lane_grp = lax.broadcasted_iota(jnp.int32, (1, nq), 1) >> WINDOW_SHIFT
    kv_block = lambda i: kvp_ref[...] if i == 0 else kv_ref[(i - 1) * WINDOW:i * WINDOW, :]
    k_bf = lambda i: kv_block(i)[:, 0:A_KV_W].astype(BF16)
    vt_bf = lambda i: kv_block(i)[:, A_KV_W:].T.astype(BF16)
    zeros = jnp.zeros((hd, nq), BF16)
    sinks = []
    for kvh in range(A_KV_HEADS):
        sk = jnp.zeros((1, nq), F32)
        for g in range(A_GROUPS):
            sk = jnp.where(lane_grp == g, sink_ref[kvh * A_GROUPS + g], sk)
        sinks.append(sk)
    units = []
    for qb in qbs:
        kk = jnp.concatenate([k_bf(qb), k_bf(qb + 1)], axis=0)
        vt = jnp.concatenate([vt_bf(qb), vt_bf(qb + 1)], axis=1)
        mask = local if qb > 0 else first
        for kvh in range(A_KV_HEADS):
            q4t = jnp.concatenate(
                [qt_ref[qb, (kvh * A_GROUPS + g) * hd:(kvh * A_GROUPS + g + 1) * hd, :] for g in range(A_GROUPS)],
                axis=1)
            wq = jnp.concatenate([q4t, zeros] if kvh == 0 else [zeros, q4t], axis=0)
            units.append((jnp.where(mask, _dot(kk, wq), -jnp.inf), sinks[kvh], vt))
    return units


def _swa_finish(qbs, units, h_ref):
    hd = A_HEAD_DIM
    for i, qb in enumerate(qbs):
        pieces = []
        for kvh in range(A_KV_HEADS):
            s, sk, vt = units[i * A_KV_HEADS + kvh]
            mx = jnp.maximum(jnp.max(s, axis=0, keepdims=True), sk)
            p = jnp.exp(s - mx)
            den = jnp.sum(p, axis=0, keepdims=True) + jnp.exp(sk - mx)
            ot = _dot(vt[kvh * hd:(kvh + 1) * hd], p.astype(BF16)) / den
            pieces += [ot[:, g * WINDOW:(g + 1) * WINDOW] for g in range(A_GROUPS)]
        h_t = jnp.concatenate(pieces, axis=0)
        h_ref[qb * WINDOW:(qb + 1) * WINDOW, :] = h_t.T.astype(h_ref.dtype)


def _front_kernel(tiles_per_seq, n_cast, last, *refs):
    (sink_ref, x_ref, nw_ref, wmt_ref, wqgt_ref, wn_ref, bd_ref, qcol_ref, krow_ref, bias_ref, nwm_ref) = refs[:11]
    cast_in = refs[11:11 + n_cast]
    (gab_ref, hm_ref, ha_ref, kwin_ref, vwin_ref, st_ref, m_ref) = refs[11 + n_cast:18 + n_cast]
    cast_out = refs[18 + n_cast:18 + 2 * n_cast]
    set_a, set_b = refs[18 + 2 * n_cast:23 + 2 * n_cast], refs[23 + 2 * n_cast:28 + 2 * n_cast]
    st_s, m_s, kvp_s = refs[28 + 2 * n_cast:]
    k = pl.program_id(0)
    odd = (k & 1) == 1
    middle = (k > 0) & (k < last)

    def instance(q_set, p_set, **stages):
        return functools.partial(
            _front_body, tiles_per_seq, sink_ref, x_ref, nw_ref, wmt_ref, wqgt_ref, wn_ref, bd_ref, qcol_ref,
            krow_ref, bias_ref, nwm_ref, gab_ref, hm_ref, ha_ref, kwin_ref, vwin_ref, st_ref, m_ref,
            *q_set, *p_set, st_s, m_s, kvp_s, **stages)

    @pl.when(k == 0)
    def _():
        for r in (st_s, m_s, kvp_s):
            r[...] = jnp.zeros(r.shape, r.dtype)

    pl.when(k == 0)(instance(set_a, set_b, do_attn=False))
    pl.when(middle & jnp.logical_not(odd))(instance(set_a, set_b))
    pl.when(middle & odd)(instance(set_b, set_a))
    pl.when(k == last)(instance(*((set_b, set_a) if last % 2 else (set_a, set_b)), do_proj=False))
    for src, dst in zip(cast_in, cast_out):
        dst[...] = src[...].astype(dst.dtype)


def _front_body(tiles_per_seq, sink_ref, x_ref, nw_ref, wmt_ref, wqgt_ref, wn_ref, bd_ref, qcol_ref, krow_ref,
                bias_ref, nwm_ref,
                gab_ref, hm_ref, ha_ref, kwin_ref, vwin_ref, st_ref, m_ref,
                q_qvot, q_gt, q_qat, q_km, q_kva, p_qvot, p_gt, p_qat, p_km, p_kva, st_s, m_s, kvp_s,
                do_proj=True, do_attn=True):
    k = pl.program_id(0)
    pieces = _proj_chunks(x_ref, nw_ref, wmt_ref, wqgt_ref, wn_ref, bd_ref, qcol_ref, krow_ref,
                          q_qvot, q_gt, q_qat, q_km, q_kva, gab_ref) if do_proj else []
    if not do_attn:
        for piece in pieces:
            piece()
        return

    seq_start = lax.rem(k - 1 + tiles_per_seq, tiles_per_seq) == 0
    n_blk = p_qat.shape[0]
    same, causal = _chunk_masks(GROUP.bit_length() - 1)
    st = jnp.where(seq_start, 0.0, st_s[...])
    m_col = jnp.where(seq_start, 0.0, m_s[0:M_HEADS, 0:1])

    grp = []
    for g in range(n_blk):
        qvot = lambda r0, r1, g=g: p_qvot[g, r0:r1, :]
        ks = p_km[g * GROUP:(g + 1) * GROUP, :] * (M_DK ** -0.5)
        b4, a4, at_mats, run4, chunk4 = _group_gates(p_gt[g] + bias_ref[...], same, causal)
        weights = _group_weights(jnp.broadcast_to(m_col, (M_HEADS, GROUP)), b4, a4, run4, chunk4)
        m_col = weights[4][:, GROUP - 1:GROUP]
        grp.append((qvot, at_mats, weights) + _group_scores(qvot, ks))
    has_prev = jnp.logical_not(seq_start)
    qb_lo, qb_hi = list(range(0, n_blk // 2)), list(range(n_blk // 2, n_blk))
    units_lo = _swa_scores(qb_lo, has_prev, sink_ref, p_qat, p_kva, kvp_s)
    inter = [_dot(st.astype(BF16), grp[0][4])]
    for piece in pieces[0:1]:
        piece()

    vals = []
    for qvot, at_mats, (big_m, _, _, w_last, _, _), k_stack, _, sc_t in grp:
        vta, intra = _group_values(qvot, sc_t, at_mats, big_m, causal)
        vals.append((intra, _dot(_weighted_values(vta, w_last), k_stack)))
    _swa_finish(qb_lo, units_lo, ha_ref)
    units_hi = _swa_scores(qb_hi, has_prev, sink_ref, p_qat, p_kva, kvp_s)
    for piece in pieces[1:3]:
        piece()

    for g in range(n_blk):
        st = _decay_row(grp[g][2][2], 0) * st + vals[g][1]
        if g + 1 < n_blk:
            inter.append(_dot(st.astype(BF16), grp[g + 1][4]))
    _swa_finish(qb_hi, units_hi, ha_ref)
    for piece in pieces[3:]:
        piece()

    for g in range(n_blk):
        qvot, _, (_, w_inter, _, _, _, e_neg_m) = grp[g][0:3]
        hm_ref[g * GROUP:(g + 1) * GROUP, :] = _group_out(qvot, inter[g], vals[g][0], w_inter, e_neg_m,
                                                          nwm_ref).astype(hm_ref.dtype)
    st_s[...] = st
    st_ref[0] = st
    m_rows = jnp.broadcast_to(m_col, (M_HEADS, LANES))
    m_rows = jnp.concatenate([m_rows, m_rows], axis=0)
    m_s[...] = m_rows
    m_ref[0] = m_rows
    tm = p_kva.shape[0]
    kwin_ref[0] = p_kva[tm - WINDOW:, 0:A_KV_W]
    vwin_ref[0] = p_kva[tm - WINDOW:, A_KV_W:]
    kvp_s[...] = p_kva[tm - WINDOW:, :]


def _cast_rows(rows, n_steps):
    rb = BF16_ROWS
    while rows % rb or rows // rb > n_steps:
        rb += BF16_ROWS
    return rb


def _front(x2d, nb, sinks, nw, wmt, wqgt, wn, bd, qcol, krow, bias_col, nwm_col, cast_ws, tm):
    n = x2d.shape[0]
    n_tiles = n // tm
    tps = n_tiles // nb
    n_blk = tm // LANES
    w_qvo = M_QK_W + 2 * M_V_W
    cur = lambda w: pl.BlockSpec((tm, w), lambda k: (jnp.minimum(k, n_tiles - 1), 0))
    prev = lambda w: pl.BlockSpec((tm, w), lambda k: (jnp.maximum(k - 1, 0), 0))
    per_seq = lambda r, w: pl.BlockSpec((1, r, w), lambda k: (jnp.maximum(k - 1, 0) // tps, 0, 0))

    def cast_spec(a):
        rb = _cast_rows(a.shape[0], n_tiles)
        return pl.BlockSpec((rb, a.shape[1]), lambda k: (jnp.minimum(k, a.shape[0] // rb - 1), 0))

    cast_in_specs = [cast_spec(a) for a in cast_ws]
    cast_out_specs = [cast_spec(a) for a in cast_ws]
    proj_scratch = [pltpu.VMEM((n_blk, w_qvo, LANES), BF16), pltpu.VMEM((n_blk, 2 * M_HEADS, LANES), F32),
                    pltpu.VMEM((n_blk, A_Q_W, LANES), BF16), pltpu.VMEM((tm, M_QK_W), BF16),
                    pltpu.VMEM((tm, 2 * A_KV_W), F32)]
    return pl.pallas_call(
        functools.partial(_front_kernel, tps, len(cast_ws), n_tiles),
        grid=(n_tiles + 1,),
        in_specs=[pl.BlockSpec(memory_space=pltpu.SMEM), cur(D_MODEL)]
                 + [_const_spec(a.shape) for a in (nw, wmt, wqgt, wn, bd, qcol, krow, bias_col, nwm_col)]
                 + cast_in_specs,
        out_specs=[cur(2 * D_MODEL), prev(M_V_W), prev(A_Q_W), per_seq(WINDOW, A_KV_W), per_seq(WINDOW, A_KV_W),
                   per_seq(S_ROWS, M_QK_W), per_seq(SUBLANES, LANES)] + cast_out_specs,
        out_shape=[jax.ShapeDtypeStruct((n, 2 * D_MODEL), BF16),
                   jax.ShapeDtypeStruct((n, M_V_W), BF16),
                   jax.ShapeDtypeStruct((n, A_Q_W), BF16),
                   jax.ShapeDtypeStruct((nb, WINDOW, A_KV_W), F32),
                   jax.ShapeDtypeStruct((nb, WINDOW, A_KV_W), F32),
                   jax.ShapeDtypeStruct((nb, S_ROWS, M_QK_W), F32),
                   jax.ShapeDtypeStruct((nb, SUBLANES, LANES), F32)]
                  + [jax.ShapeDtypeStruct(a.shape, BF16) for a in cast_ws],
        scratch_shapes=proj_scratch + proj_scratch + [pltpu.VMEM((S_ROWS, M_QK_W), F32),
                                                       pltpu.VMEM((SUBLANES, LANES), F32),
                                                       pltpu.VMEM((WINDOW, 2 * A_KV_W), F32)],
        compiler_params=_params(("arbitrary",)),
        name="front",
    )(sinks, x2d, nw, wmt, wqgt, wn, bd, qcol, krow, bias_col, nwm_col, *cast_ws)


def _swa_sample_kernel(seq_len, sink_ref, qt_ref, kv_ref, ck_ref, cv_ref, h_ref, kwin_ref, vwin_ref):
    n_seq = GROUP // seq_len
    wb = ck_ref.shape[2]
    lane = lax.broadcasted_iota(jnp.int32, (1, LANES), 1)
    lo_half = lane < HALF
    q_rows = qt_ref[0].astype(F32).T
    kv_new = kv_ref[...]
    k_new = kv_new[:, 0:A_KV_W].reshape(n_seq, seq_len, A_KV_W)
    v_new = kv_new[:, A_KV_W:].reshape(n_seq, seq_len, A_KV_W)
    kt_new, vt_new = kv_new[:, 0:A_KV_W].T, kv_new[:, A_KV_W:].T
    for s in range(n_seq):
        put = (wb - seq_len - s * seq_len) % LANES
        for new_t, c_ref, win_ref in ((kt_new, ck_ref, kwin_ref), (vt_new, cv_ref, vwin_ref)):
            win_ref[s] = jnp.where(lane >= wb - seq_len, pltpu.roll(new_t, put, axis=1) if put else new_t,
                                   pltpu.roll(c_ref[s], wb - seq_len, axis=1))

    def to_kv_half(x, head):
        kvh = head // A_GROUPS
        if head % 2 != kvh:
            x = pltpu.roll(x, HALF, axis=1)
        return jnp.where(lo_half if kvh == 0 else ~lo_half, x, 0.0)

    lhs = jnp.concatenate(
        [to_kv_half(q_rows[:, (h // 2) * LANES:(h // 2 + 1) * LANES], h).reshape(n_seq, seq_len, LANES)
         for h in range(A_HEADS)], axis=1).astype(BF16)
    zpad = jnp.zeros((n_seq, BF16_ROWS - seq_len, A_KV_W), F32)
    k_nb = jnp.concatenate([k_new, zpad], axis=1).astype(BF16)
    v_nb = jnp.concatenate([v_new, zpad], axis=1).astype(BF16)
    s_c = jnp.einsum('sqf,sfk->sqk', lhs, ck_ref[...].astype(BF16), preferred_element_type=F32)
    s_n = jnp.einsum('sqf,skf->sqk', lhs, k_nb, preferred_element_type=F32)
    nrow = A_HEADS * seq_len
    ti = lax.broadcasted_iota(jnp.int32, (nrow, wb), 0) & (seq_len - 1)
    ki = lax.broadcasted_iota(jnp.int32, (nrow, wb), 1)
    mask_c = (ti + wb - ki) < WINDOW
    ti_n = lax.broadcasted_iota(jnp.int32, (nrow, BF16_ROWS), 0) & (seq_len - 1)
    ki_n = lax.broadcasted_iota(jnp.int32, (nrow, BF16_ROWS), 1)
    mask_n = ki_n <= ti_n
    row_head = lax.broadcasted_iota(jnp.int32, (nrow, 1), 0) >> (seq_len.bit_length() - 1)
    sk = jnp.zeros((nrow, 1), F32)
    for h in range(A_HEADS):
        sk = jnp.where(row_head == h, sink_ref[h], sk)
    s_c = jnp.where(mask_c, s_c, -jnp.inf)
    s_n = jnp.where(mask_n, s_n, -jnp.inf)
    mx = jnp.maximum(jnp.maximum(jnp.max(s_c, axis=-1, keepdims=True), jnp.max(s_n, axis=-1, keepdims=True)), sk)
    p_c = jnp.exp(s_c - mx)
    p_n = jnp.exp(s_n - mx)
    den = jnp.sum(p_c, axis=-1, keepdims=True) + jnp.sum(p_n, axis=-1, keepdims=True) + jnp.exp(sk - mx)
    o = (jnp.einsum('sqk,sfk->sqf', p_c.astype(BF16), cv_ref[...].astype(BF16), preferred_element_type=F32)
         + jnp.einsum('sqk,skf->sqf', p_n.astype(BF16), v_nb, preferred_element_type=F32)) / den

    def from_kv_half(head):
        x = o[:, head * seq_len:(head + 1) * seq_len, :].reshape(GROUP, LANES)
        return pltpu.roll(x, HALF, axis=1) if head % 2 != head // A_GROUPS else x

    for c in range(A_HEADS // 2):
        h_ref[:, c * LANES:(c + 1) * LANES] = jnp.where(lo_half, from_kv_half(2 * c),
                                                        from_kv_half(2 * c + 1)).astype(h_ref.dtype)


def _swa_sample(qat, kv_a, cache_k, cache_v, sinks, seq_len):
    ngrp = qat.shape[0]
    n_seq = GROUP // seq_len
    wb = cache_k.shape[2]
    assert wb == LANES, "window positions fill one lane-width"
    row = lambda w: pl.BlockSpec((GROUP, w), lambda i: (i, 0))
    cache = pl.BlockSpec((n_seq, A_KV_W, wb), lambda i: (i, 0, 0))
    return pl.pallas_call(
        functools.partial(_swa_sample_kernel, seq_len),
        grid=(ngrp,),
        in_specs=[pl.BlockSpec(memory_space=pltpu.SMEM), pl.BlockSpec((1, A_Q_W, LANES), lambda i: (i, 0, 0)),
                  row(2 * A_KV_W), cache, cache],
        out_specs=[row(A_Q_W), cache, cache],
        out_shape=[jax.ShapeDtypeStruct((ngrp * GROUP, A_Q_W), BF16),
                   jax.ShapeDtypeStruct(cache_k.shape, F32),
                   jax.ShapeDtypeStruct(cache_v.shape, F32)],
        compiler_params=_params(("arbitrary",)),
        name="swa_sample",
    )(sinks, qat, kv_a, cache_k, cache_v)


def _merge_ffn_kernel(x_ref, hm_ref, ha_ref, gab_ref, wa_ref, wb_ref, wo_ref, nw_ref, wg_ref, wu_ref, wd_ref,
                      y_ref):
    ga = jax.nn.sigmoid(gab_ref[:, 0:D_MODEL].astype(F32))
    gb = jax.nn.sigmoid(gab_ref[:, D_MODEL:].astype(F32))
    mix = ga * _dot(hm_ref[...], wa_ref[...]) + gb * _dot(ha_ref[...], wb_ref[...])
    x1 = x_ref[...] + _dot(mix.astype(BF16), wo_ref[...])
    hf = _rms_rows(x1, nw_ref[...]).astype(BF16)
    gate = _dot(hf, wg_ref[...])
    up = _dot(hf, wu_ref[...])
    act = (jax.nn.silu(gate) * up).astype(BF16)
    y_ref[...] = x1 + _dot(act, wd_ref[...])


def _merge_ffn(x2d, h_m, h_a, g_ab, wa, wb, wo, nw, wg, wu, wd, tm):
    n = x2d.shape[0]
    row = lambda w: pl.BlockSpec((tm, w), lambda i: (i, 0))
    return pl.pallas_call(
        _merge_ffn_kernel,
        grid=(n // tm,),
        in_specs=[row(D_MODEL), row(M_V_W), row(A_Q_W), row(2 * D_MODEL)]
                 + [_const_spec(w.shape) for w in (wa, wb, wo, nw, wg, wu, wd)],
        out_specs=row(D_MODEL),
        out_shape=jax.ShapeDtypeStruct((n, D_MODEL), F32),
        compiler_params=_params(("arbitrary",)),
        name="merge_ffn",
    )(x2d, h_m, h_a, g_ab, wa, wb, wo, nw, wg, wu, wd)


def kernel(x_prompt, x_sample, state_mlstm_C, state_mlstm_n, state_mlstm_m, cache_swa_k, cache_swa_v,
           norm_mix_w, w_in, mlstm_i_bias, mlstm_f_bias, mlstm_norm_w, q_norm_w, k_norm_w, attn_sinks,
           w_branch_a, w_branch_b, w_out, norm_ffn_w, w_gate, w_up, w_down):
    depth = w_in.shape[0]
    assert depth == 1, "single trunk layer"
    l = 0
    bp, tp = x_prompt.shape[0], x_prompt.shape[1]
    bs, ts = x_sample.shape[0], x_sample.shape[1]
    assert tp % 512 == 0 and (bs * ts) % GROUP == 0 and GROUP % ts == 0 and ts & (ts - 1) == 0
    assert ts <= SUBLANES, "sample chunk must fit one sublane tile"

    wt = jnp.transpose(w_in[l])
    c_km, c_vm = M_QK_W, 2 * M_QK_W
    c_g = 2 * M_QK_W + 2 * M_V_W
    c_qa = c_g + 2 * M_HEADS
    c_ka = c_qa + A_Q_W
    gate_pad = jnp.zeros((BF16_ROWS - 2 * M_HEADS, D_MODEL), F32)
    wmt = jnp.concatenate([wt[0:c_km], wt[c_vm:c_g]], axis=0).astype(BF16)
    wqgt = jnp.concatenate([wt[c_qa:c_ka], wt[c_g:c_qa], gate_pad], axis=0).astype(BF16)
    wn = jnp.concatenate([wt[c_km:c_vm], wt[c_ka:]], axis=0).astype(BF16)
    head_of = jnp.arange(A_KV_W) // A_HEAD_DIM
    bd = (head_of[:, None] == head_of[None, :]).astype(BF16)
    qcol = (jnp.tile(q_norm_w[l], A_HEADS) * (A_HEAD_DIM ** -0.5)).reshape(A_Q_W, 1)
    krow = jnp.tile(k_norm_w[l], A_KV_HEADS).reshape(1, A_KV_W)
    nw_mix = norm_mix_w[l].reshape(1, D_MODEL)
    nw_ffn = norm_ffn_w[l].reshape(1, D_MODEL)
    bias_col = jnp.concatenate([mlstm_i_bias[l], mlstm_f_bias[l]]).reshape(2 * M_HEADS, 1)
    nw_col = mlstm_norm_w[l].reshape(M_V_W, 1)
    sinks = attn_sinks[l]
    proj = lambda x2d, tm: _proj(x2d, nw_mix, wmt, wqgt, wn, bd, qcol, krow, tm)

    xp = x_prompt.reshape(bp * tp, D_MODEL)
    later_ws = (w_branch_a[l], w_branch_b[l], w_out[l], w_gate[l], w_up[l], w_down[l])
    g_ab, h_m, h_a, kwin_p, vwin_p, st_p, m_p, wa, wb, wo, wg, wu, wd = _front(
        xp, bp, sinks, nw_mix, wmt, wqgt, wn, bd, qcol, krow, bias_col, nw_col, later_ws, 512)
    merge = lambda x2d, h_m, h_a, g_ab, tm: _merge_ffn(x2d, h_m, h_a, g_ab, wa, wb, wo, nw_ffn, wg, wu, wd, tm)
    yp = merge(xp, h_m, h_a, g_ab, 512).reshape(bp, tp, D_MODEL)
    c_p = jnp.swapaxes(st_p[:, :M_DV, :], 1, 2).reshape(bp, M_HEADS, M_DK, M_DV)
    n_p = st_p[:, M_DV, :].reshape(bp, M_HEADS, M_DK)
    m_pr = m_p[:, :M_HEADS, 0]

    ns = bs * ts
    xs = x_sample.reshape(ns, D_MODEL)
    tms = 512 if ns % 512 == 0 else GROUP
    qvot, gt, qat, k_m, kv_a, g_ab = proj(xs, tms)
    ngrp = ns // GROUP
    m_lanes = jnp.repeat(state_mlstm_m[l], ts, axis=0).reshape(ngrp, GROUP, M_HEADS)
    mrow = jnp.pad(jnp.swapaxes(m_lanes, 1, 2), ((0, 0), (0, SUBLANES - M_HEADS), (0, 0)))
    h_m, c_s, n_s, mt_s = _mlstm_sample(qvot, gt, k_m, mrow, bias_col, nw_col,
                                        state_mlstm_C[l].reshape(bs, M_QK_W, M_DV),
                                        state_mlstm_n[l].reshape(bs, M_QK_W), ts)
    wbuf = cache_swa_k.shape[2]
    to_fm = lambda a: jnp.transpose(a, (0, 2, 3, 1)).reshape(bs, A_KV_W, wbuf)
    from_fm = lambda a: jnp.transpose(a.reshape(bs, A_KV_HEADS, A_HEAD_DIM, wbuf), (0, 3, 1, 2))[None]
    h_a, kwin_s, vwin_s = _swa_sample(qat, kv_a, to_fm(cache_swa_k[l]), to_fm(cache_swa_v[l]), sinks, ts)
    ys = merge(xs, h_m, h_a, g_ab, tms).reshape(bs, ts, D_MODEL)
    m_s = jnp.swapaxes(mt_s[:, :M_HEADS, :], 1, 2).reshape(bs, ts, M_HEADS)[:, ts - 1, :]

    kv5 = lambda a: a.reshape(a.shape[0], a.shape[1], A_KV_HEADS, A_HEAD_DIM)[None]
    return (yp, ys,
            c_p[None], n_p[None], m_pr[None], kv5(kwin_p), kv5(vwin_p),
            c_s.reshape(bs, M_HEADS, M_DK, M_DV)[None], n_s.reshape(bs, M_HEADS, M_DK)[None], m_s[None],
            from_fm(kwin_s), from_fm(vwin_s))
```

```python
import functools

import jax
import jax.numpy as jnp
from jax import lax
from jax.experimental import pallas as pl
from jax.experimental.pallas import tpu as pltpu

F32 = jnp.float32
BF16 = jnp.bfloat16

D_MODEL = 1024
M_HEADS = 4
M_DK = 64
M_DV = 128
M_CHUNK = 64
M_QK_W = M_HEADS * M_DK
M_V_W = M_HEADS * M_DV
A_HEADS = 8
A_KV_HEADS = 2
A_HEAD_DIM = 64
A_GROUPS = A_HEADS // A_KV_HEADS
A_Q_W = A_HEADS * A_HEAD_DIM
A_KV_W = A_KV_HEADS * A_HEAD_DIM
WINDOW = 128
D_FF = 2816
EPS = 1e-6

LANES = 128
SUBLANES = 8
BF16_ROWS = 16
GROUP = 128
S_ROWS = M_DV + BF16_ROWS
VMEM_LIMIT = 56 * 1024 * 1024
DK_SHIFT = M_DK.bit_length() - 1
WINDOW_SHIFT = WINDOW.bit_length() - 1
HALF = LANES // 2
assert A_HEAD_DIM == HALF and A_KV_W == LANES, "attention head pairs share one lane-width"

NT_DIMS = (((1,), (1,)), ((), ()))


def _dot(a, b):
    return jnp.dot(a, b, preferred_element_type=F32)


def _dot_nt(a, b):
    return lax.dot_general(a, b, NT_DIMS, preferred_element_type=F32)


def _const_spec(shape):
    nd = len(shape)
    return pl.BlockSpec(shape, lambda *_: (0,) * nd, pipeline_mode=pl.Buffered(1))


def _params(sem):
    return pltpu.CompilerParams(dimension_semantics=sem, vmem_limit_bytes=VMEM_LIMIT)


def _rms_rows(x, nw):
    ms = jnp.mean(x * x, axis=-1, keepdims=True)
    return (x * lax.rsqrt(ms + EPS)) * nw


GAB_CHUNK = 512


def _proj_chunks(x_ref, nw_ref, wmt_ref, wqgt_ref, wn_ref, bd_ref, qcol_ref, krow_ref,
                 qvot_ref, gt_ref, qat_ref, km_ref, kva_ref, gab_ref):
    hn = _rms_rows(x_ref[...], nw_ref[...]).astype(BF16)
    n_blk = qat_ref.shape[0]

    def put(ref, rows, val):
        for c in range(n_blk):
            ref[c, rows, :] = val[:, c * LANES:(c + 1) * LANES].astype(ref.dtype)

    def mlstm_qvo():
        put(qvot_ref, slice(None), _dot_nt(wmt_ref[...], hn))

    def attn_q_and_gates():
        qt = _dot_nt(wqgt_ref[...], hn)
        put(gt_ref, slice(None), qt[A_Q_W:A_Q_W + 2 * M_HEADS])
        for h in range(A_HEADS):
            hs = slice(h * A_HEAD_DIM, (h + 1) * A_HEAD_DIM)
            blk = qt[hs]
            ssq_q = jnp.sum(blk * blk, axis=0, keepdims=True)
            put(qat_ref, hs, (blk * lax.rsqrt(ssq_q * (1.0 / A_HEAD_DIM) + EPS)) * qcol_ref[hs])

    def mlstm_k():
        km_ref[...] = _dot_nt(hn, wn_ref[0:M_QK_W, :]).astype(km_ref.dtype)

    def branch_gates(c0):
        def run():
            w0 = M_QK_W + 2 * A_KV_W + c0
            gab_ref[:, c0:c0 + GAB_CHUNK] = _dot_nt(hn, wn_ref[w0:w0 + GAB_CHUNK, :]).astype(gab_ref.dtype)
        return run

    def attn_kv():
        kv = _dot_nt(hn, wn_ref[M_QK_W:M_QK_W + 2 * A_KV_W, :])
        k = kv[:, 0:A_KV_W]
        ksq = k * k
        hi = ksq.astype(BF16)
        lo = (ksq - hi.astype(F32)).astype(BF16)
        ssq = _dot(hi, bd_ref[...]) + _dot(lo, bd_ref[...])
        kva_ref[:, 0:A_KV_W] = (k * lax.rsqrt(ssq * (1.0 / A_HEAD_DIM) + EPS)) * krow_ref[...]
        kva_ref[:, A_KV_W:] = kv[:, A_KV_W:]

    return ([mlstm_qvo, attn_q_and_gates, mlstm_k]
            + [branch_gates(c0) for c0 in range(0, gab_ref.shape[1], GAB_CHUNK)] + [attn_kv])


def _proj_kernel(*refs):
    for piece in _proj_chunks(*refs):
        piece()


def _proj(x2d, nw, wmt, wqgt, wn, bd, qcol, krow, tm):
    n = x2d.shape[0]
    row = lambda w: pl.BlockSpec((tm, w), lambda i: (i, 0))
    n_blk = tm // LANES
    slab = lambda r: pl.BlockSpec((n_blk, r, LANES), lambda i: (i, 0, 0))
    w_qvo = M_QK_W + 2 * M_V_W
    return pl.pallas_call(
        _proj_kernel,
        grid=(n // tm,),
        in_specs=[row(D_MODEL)] + [_const_spec(a.shape) for a in (nw, wmt, wqgt, wn, bd, qcol, krow)],
        out_specs=[slab(w_qvo), slab(2 * M_HEADS), slab(A_Q_W), row(M_QK_W), row(2 * A_KV_W), row(2 * D_MODEL)],
        out_shape=[jax.ShapeDtypeStruct((n // LANES, w_qvo, LANES), BF16),
                   jax.ShapeDtypeStruct((n // LANES, 2 * M_HEADS, LANES), F32),
                   jax.ShapeDtypeStruct((n // LANES, A_Q_W, LANES), BF16),
                   jax.ShapeDtypeStruct((n, M_QK_W), BF16),
                   jax.ShapeDtypeStruct((n, 2 * A_KV_W), F32),
                   jax.ShapeDtypeStruct((n, 2 * D_MODEL), BF16)],
        compiler_params=_params(("arbitrary",)),
        name="proj",
    )(x2d, nw, wmt, wqgt, wn, bd, qcol, krow)


def _split3_rows(x):
    hi = x.astype(BF16).astype(F32)
    r1 = x - hi
    mid = r1.astype(BF16).astype(F32)
    lo = r1 - mid
    return jnp.concatenate([hi, mid, lo], axis=0).astype(BF16)


def _log_sigmoid(x):
    return jnp.minimum(x, 0.0) - jnp.log1p(jnp.exp(-jnp.abs(x)))


def _chunk_masks(chunk_shift):
    s = lax.broadcasted_iota(jnp.int32, (GROUP, GROUP), 0)
    t = lax.broadcasted_iota(jnp.int32, (GROUP, GROUP), 1)
    same = (s >> chunk_shift) == (t >> chunk_shift)
    return same, same & (s <= t)


def _group_gates(gt, same, causal):
    cm_bf = jnp.where(causal, 1.0, 0.0).astype(BF16)
    lf = _log_sigmoid(gt)
    nr = gt.shape[0]
    bt3 = _dot(_split3_rows(lf), cm_bf)
    bt = (bt3[0:nr] + bt3[nr:2 * nr]) + bt3[2 * nr:3 * nr]
    b4 = bt[M_HEADS:2 * M_HEADS]
    a4 = gt[0:M_HEADS] - b4
    a8 = jnp.concatenate([a4, a4], axis=0)
    a_cols = jnp.concatenate([a8, jnp.zeros((GROUP - SUBLANES, GROUP), F32)], axis=0).T
    at_mats, run_rows, chunk_rows = [], [], []
    for h in range(M_HEADS):
        at = jnp.broadcast_to(a_cols[:, h:h + 1], (GROUP, GROUP))
        at_mats.append(at)
        run_rows.append(jnp.max(jnp.where(causal, at, -jnp.inf), axis=0, keepdims=True))
        chunk_rows.append(jnp.max(jnp.where(same, at, -jnp.inf), axis=0, keepdims=True))
    return b4, a4, at_mats, jnp.concatenate(run_rows, axis=0), jnp.concatenate(chunk_rows, axis=0)


def _group_weights(m_prev, b4, a4, run4, chunk4):
    big_m = jnp.maximum(m_prev, run4)
    m_last = jnp.maximum(m_prev, chunk4)
    w_inter = jnp.exp(m_prev - big_m)
    g_vec = jnp.exp(m_prev - m_last)
    w_last = jnp.exp(a4 - m_last)
    m_t = b4 + big_m
    return big_m, w_inter, g_vec, w_last, m_t, jnp.exp(-m_t)


def _group_scores(qvot, ks):
    lane_head = lax.broadcasted_iota(jnp.int32, (1, M_QK_W), 1) >> DK_SHIFT
    row_head = lax.broadcasted_iota(jnp.int32, (M_QK_W, 1), 0) >> DK_SHIFT
    qt = qvot(0, M_QK_W)
    k_stack = jnp.concatenate([jnp.where(lane_head == h, ks, jnp.zeros_like(ks)) for h in range(M_HEADS)], axis=0)
    qw = jnp.concatenate([jnp.where(row_head == h, qt, jnp.zeros_like(qt)) for h in range(M_HEADS)], axis=1)
    zero_blk = jnp.zeros((M_DK, GROUP), BF16)
    sc_t = []
    for h in range(0, M_HEADS, 2):
        q_pair = jnp.concatenate(
            [jnp.concatenate([qvot(h * M_DK, (h + 1) * M_DK), zero_blk], axis=1),
             jnp.concatenate([zero_blk, qvot((h + 1) * M_DK, (h + 2) * M_DK)], axis=1)], axis=0)
        sc = _dot(ks[:, h * M_DK:(h + 2) * M_DK], q_pair)
        sc_t += [sc[:, 0:GROUP], sc[:, GROUP:]]
    return k_stack, qw, sc_t


def _group_values(qvot, sc_t, at_mats, big_m, causal):
    ones_rows = jnp.where(lax.broadcasted_iota(jnp.int32, (BF16_ROWS, GROUP), 0) == 0, 1.0, 0.0).astype(BF16)
    zero_blk = jnp.zeros((GROUP, GROUP), BF16)
    vta, s_t = [], []
    for h in range(M_HEADS):
        w_t = jnp.where(causal, jnp.exp(at_mats[h] - big_m[h:h + 1]), 0.0)
        s_t.append((sc_t[h] * w_t).astype(BF16))
        vta.append(jnp.concatenate([qvot(M_QK_W + h * M_DV, M_QK_W + (h + 1) * M_DV), ones_rows], axis=0))
    intra = []
    for h in range(0, M_HEADS, 2):
        pair = jnp.concatenate([jnp.concatenate([s_t[h], zero_blk], axis=1),
                                jnp.concatenate([zero_blk, s_t[h + 1]], axis=1)], axis=0)
        intra.append(_dot(jnp.concatenate(vta[h:h + 2], axis=1), pair))
    return vta, jnp.concatenate(intra, axis=1)


def _weighted_values(vta, w_rows):
    return jnp.concatenate([(vta[h].astype(F32) * w_rows[h:h + 1]).astype(BF16) for h in range(M_HEADS)], axis=1)


def _lanes_x(rows):
    return jnp.concatenate([rows[h:h + 1] for h in range(M_HEADS)], axis=1)


def _group_out(qvot, inter, intra, w_inter, e_neg_m, nw_ref):
    outs = []
    out_all = inter * _lanes_x(w_inter) + intra
    for h in range(M_HEADS):
        out_t = out_all[:, h * GROUP:(h + 1) * GROUP]
        hh = out_t[0:M_DV] / jnp.maximum(jnp.abs(out_t[M_DV:M_DV + 1]), e_neg_m[h:h + 1])
        ms = jnp.mean(hh * hh, axis=0, keepdims=True)
        hn = (hh * lax.rsqrt(ms + EPS)) * nw_ref[h * M_DV:(h + 1) * M_DV]
        o_t = qvot(M_QK_W + M_V_W + h * M_DV, M_QK_W + M_V_W + (h + 1) * M_DV)
        outs.append(hn * jax.nn.sigmoid(o_t.astype(F32)))
    return jnp.concatenate(outs, axis=0).T


def _decay_row(g_vec, lane0):
    lane_head = lax.broadcasted_iota(jnp.int32, (1, M_QK_W), 1) >> DK_SHIFT
    g_row = jnp.zeros((1, M_QK_W), F32)
    for h in range(M_HEADS):
        g_row = jnp.where(lane_head == h, g_vec[h:h + 1, lane0:lane0 + 1], g_row)
    return g_row


def _mlstm_sample_kernel(seq_len, qvot_ref, gt_ref, k_ref, mrow_ref, bias_ref, nw_ref, c_ref, n_ref,
                         h_ref, c_out_ref, n_out_ref, mt_ref):
    n_seq = GROUP // seq_len
    shift = seq_len.bit_length() - 1
    same, causal = _chunk_masks(shift)
    lane_seq = lax.broadcasted_iota(jnp.int32, (1, GROUP), 1) >> shift
    lane_seq_x = jnp.concatenate([lane_seq] * M_HEADS, axis=1)
    row0 = lax.broadcasted_iota(jnp.int32, (BF16_ROWS, M_QK_W), 0) == 0
    qvot = lambda r0, r1: qvot_ref[0, r0:r1, :]
    ks = k_ref[...] * (M_DK ** -0.5)
    b4, a4, at_mats, run4, chunk4 = _group_gates(gt_ref[0] + bias_ref[...], same, causal)
    big_m, w_inter, g_vec, w_last, m_t, e_neg_m = _group_weights(mrow_ref[0, 0:M_HEADS, :], b4, a4, run4, chunk4)
    mt_ref[0] = jnp.concatenate([m_t, m_t], axis=0)
    k_stack, qw, sc_t = _group_scores(qvot, ks)

    st_old = []
    for s in range(n_seq):
        n_rows = jnp.where(row0, jnp.broadcast_to(n_ref[s:s + 1, :], (BF16_ROWS, M_QK_W)), 0.0)
        st_old.append(jnp.concatenate([c_ref[s].T, n_rows], axis=0))
    inter_all = _dot(jnp.concatenate(st_old, axis=0).astype(BF16), qw)
    vta, intra = _group_values(qvot, sc_t, at_mats, big_m, causal)
    inter = inter_all[0:S_ROWS]
    for s in range(1, n_seq):
        inter = jnp.where(lane_seq_x == s, inter_all[s * S_ROWS:(s + 1) * S_ROWS], inter)
    tall = jnp.concatenate([_weighted_values(vta, jnp.where(lane_seq == s, w_last, 0.0)) for s in range(n_seq)],
                           axis=0)
    d_st = _dot(tall, k_stack)
    for s in range(n_seq):
        st_new = _decay_row(g_vec, s * seq_len) * st_old[s] + d_st[s * S_ROWS:(s + 1) * S_ROWS]
        c_out_ref[s] = st_new[0:M_DV].T
        n_out_ref[s:s + 1, :] = st_new[M_DV:M_DV + 1]
    h_ref[...] = _group_out(qvot, inter, intra, w_inter, e_neg_m, nw_ref).astype(h_ref.dtype)


def _mlstm_sample(qvot, gt, k_m, mrow, bias_col, nw_col, c, n, seq_len):
    ngrp = qvot.shape[0]
    n_seq = GROUP // seq_len
    full = lambda a: pl.BlockSpec(a.shape, lambda i: (0,) * a.ndim)
    slab = lambda a: pl.BlockSpec((1,) + a.shape[1:], lambda i: (i, 0, 0))
    row = lambda w: pl.BlockSpec((GROUP, w), lambda i: (i, 0))
    c_spec = pl.BlockSpec((n_seq,) + c.shape[1:], lambda i: (i, 0, 0))
    n_spec = pl.BlockSpec((n_seq, n.shape[1]), lambda i: (i, 0))
    return pl.pallas_call(
        functools.partial(_mlstm_sample_kernel, seq_len),
        grid=(ngrp,),
        in_specs=[slab(qvot), slab(gt), row(M_QK_W), slab(mrow), full(bias_col), full(nw_col), c_spec, n_spec],
        out_specs=[row(M_V_W), c_spec, n_spec, slab(mrow)],
        out_shape=[jax.ShapeDtypeStruct((ngrp * GROUP, M_V_W), BF16),
                   jax.ShapeDtypeStruct(c.shape, F32),
                   jax.ShapeDtypeStruct(n.shape, F32),
                   jax.ShapeDtypeStruct(mrow.shape, F32)],
        compiler_params=_params(("arbitrary",)),
        name="mlstm_sample",
    )(qvot, gt, k_m, mrow, bias_col, nw_col, c, n)


def _swa_scores(qbs, has_prev, sink_ref, qt_ref, kv_ref, kvp_ref):
    hd = A_HEAD_DIM
    nq = A_GROUPS * WINDOW
    si = lax.broadcasted_iota(jnp.int32, (2 * WINDOW, nq), 0)
    qi = lax.broadcasted_iota(jnp.int32, (2 * WINDOW, nq), 1) & (WINDOW - 1)
    local = ((si < WINDOW) & (si > qi)) | ((si >= WINDOW) & (si - WINDOW <= qi))
    first = local & (has_prev | (si >= WINDOW))
    lane_grp = lax.broadcasted_iota(jnp.int32, (1, nq), 1) >> WINDOW_SHIFT
    kv_block = lambda i: kvp_ref[...] if i == 0 else kv_ref[(i - 1) * WINDOW:i * WINDOW, :]
    k_bf = lambda i: kv_block(i)[:, 0:A_KV_W].astype(BF16)
    vt_bf = lambda i: kv_block(i)[:, A_KV_W:].T.astype(BF16)
    zeros = jnp.zeros((hd, nq), BF16)
    sinks = []
    for kvh in range(A_KV_HEADS):
        sk = jnp.zeros((1, nq), F32)
        for g in range(A_GROUPS):
            sk = jnp.where(lane_grp == g, sink_ref[kvh * A_GROUPS + g], sk)
        sinks.append(sk)
    units = []
    for qb in qbs:
        kk = jnp.concatenate([k_bf(qb), k_bf(qb + 1)], axis=0)
        vt = jnp.concatenate([vt_bf(qb), vt_bf(qb + 1)], axis=1)
        mask = local if qb > 0 else first
        for kvh in range(A_KV_HEADS):
            q4t = jnp.concatenate(
                [qt_ref[qb, (kvh * A_GROUPS + g) * hd:(kvh * A_GROUPS + g + 1) * hd, :] for g in range(A_GROUPS)],
                axis=1)
            wq = jnp.concatenate([q4t, zeros] if kvh == 0 else [zeros, q4t], axis=0)
            units.append((jnp.where(mask, _dot(kk, wq), -jnp.inf), sinks[kvh], vt))
    return units


def _swa_finish(qbs, units, h_ref):
    hd = A_HEAD_DIM
    for i, qb in enumerate(qbs):
        pieces = []
        for kvh in range(A_KV_HEADS):
            s, sk, vt = units[i * A_KV_HEADS + kvh]
            mx = jnp.maximum(jnp.max(s, axis=0, keepdims=True), sk)
            p = jnp.exp(s - mx)
            den = jnp.sum(p, axis=0, keepdims=True) + jnp.exp(sk - mx)
            ot = _dot(vt[kvh * hd:(kvh + 1) * hd], p.astype(BF16)) / den
            pieces += [ot[:, g * WINDOW:(g + 1) * WINDOW] for g in range(A_GROUPS)]
        h_t = jnp.concatenate(pieces, axis=0)
        h_ref[qb * WINDOW:(qb + 1) * WINDOW, :] = h_t.T.astype(h_ref.dtype)


def _front_kernel(tiles_per_seq, n_cast, *refs):
    (sink_ref, x_ref, nw_ref, wmt_ref, wqgt_ref, wn_ref, bd_ref, qcol_ref, krow_ref, bias_ref, nwm_ref) = refs[:11]
    cast_in = refs[11:11 + n_cast]
    (gab_ref, hm_ref, ha_ref, kwin_ref, vwin_ref, st_ref, m_ref) = refs[11 + n_cast:18 + n_cast]
    cast_out = refs[18 + n_cast:18 + 2 * n_cast]
    set_a, set_b = refs[18 + 2 * n_cast:23 + 2 * n_cast], refs[23 + 2 * n_cast:28 + 2 * n_cast]
    st_s, m_s, kvp_s = refs[28 + 2 * n_cast:]
    odd = (pl.program_id(0) & 1) == 1
    for parity, q_set, p_set in ((jnp.logical_not(odd), set_a, set_b), (odd, set_b, set_a)):
        pl.when(parity)(functools.partial(
            _front_body, tiles_per_seq, sink_ref, x_ref, nw_ref, wmt_ref, wqgt_ref, wn_ref, bd_ref, qcol_ref,
            krow_ref, bias_ref, nwm_ref, gab_ref, hm_ref, ha_ref, kwin_ref, vwin_ref, st_ref, m_ref,
            *q_set, *p_set, st_s, m_s, kvp_s))
    for src, dst in zip(cast_in, cast_out):
        dst[...] = src[...].astype(dst.dtype)


def _front_body(tiles_per_seq, sink_ref, x_ref, nw_ref, wmt_ref, wqgt_ref, wn_ref, bd_ref, qcol_ref, krow_ref,
                bias_ref, nwm_ref,
                gab_ref, hm_ref, ha_ref, kwin_ref, vwin_ref, st_ref, m_ref,
                q_qvot, q_gt, q_qat, q_km, q_kva, p_qvot, p_gt, p_qat, p_km, p_kva, st_s, m_s, kvp_s):
    k = pl.program_id(0)

    @pl.when(k == 0)
    def _():
        for r in (p_qvot, p_gt, p_qat, p_km, p_kva, st_s, m_s, kvp_s):
            r[...] = jnp.zeros(r.shape, r.dtype)

    pieces = _proj_chunks(x_ref, nw_ref, wmt_ref, wqgt_ref, wn_ref, bd_ref, qcol_ref, krow_ref,
                          q_qvot, q_gt, q_qat, q_km, q_kva, gab_ref)

    seq_start = lax.rem(k - 1 + tiles_per_seq, tiles_per_seq) == 0
    n_blk = p_qat.shape[0]
    same, causal = _chunk_masks(GROUP.bit_length() - 1)
    st = jnp.where(seq_start, 0.0, st_s[...])
    m_col = jnp.where(seq_start, 0.0, m_s[0:M_HEADS, 0:1])

    grp = []
    for g in range(n_blk):
        qvot = lambda r0, r1, g=g: p_qvot[g, r0:r1, :]
        ks = p_km[g * GROUP:(g + 1) * GROUP, :] * (M_DK ** -0.5)
        b4, a4, at_mats, run4, chunk4 = _group_gates(p_gt[g] + bias_ref[...], same, causal)
        weights = _group_weights(jnp.broadcast_to(m_col, (M_HEADS, GROUP)), b4, a4, run4, chunk4)
        m_col = weights[4][:, GROUP - 1:GROUP]
        grp.append((qvot, at_mats, weights) + _group_scores(qvot, ks))
    has_prev = jnp.logical_not(seq_start)
    qb_lo, qb_hi = list(range(0, n_blk // 2)), list(range(n_blk // 2, n_blk))
    units_lo = _swa_scores(qb_lo, has_prev, sink_ref, p_qat, p_kva, kvp_s)
    inter = [_dot(st.astype(BF16), grp[0][4])]
    for piece in pieces[0:2]:
        piece()

    vals = []
    for qvot, at_mats, (big_m, _, _, w_last, _, _), k_stack, _, sc_t in grp:
        vta, intra = _group_values(qvot, sc_t, at_mats, big_m, causal)
        vals.append((intra, _dot(_weighted_values(vta, w_last), k_stack)))
    _swa_finish(qb_lo, units_lo, ha_ref)
    units_hi = _swa_scores(qb_hi, has_prev, sink_ref, p_qat, p_kva, kvp_s)
    for piece in pieces[2:5]:
        piece()

    for g in range(n_blk):
        st = _decay_row(grp[g][2][2], 0) * st + vals[g][1]
        if g + 1 < n_blk:
            inter.append(_dot(st.astype(BF16), grp[g + 1][4]))
    _swa_finish(qb_hi, units_hi, ha_ref)
    for piece in pieces[5:]:
        piece()

    for g in range(n_blk):
        qvot, _, (_, w_inter, _, _, _, e_neg_m) = grp[g][0:3]
        hm_ref[g * GROUP:(g + 1) * GROUP, :] = _group_out(qvot, inter[g], vals[g][0], w_inter, e_neg_m,
                                                          nwm_ref).astype(hm_ref.dtype)
    st_s[...] = st
    st_ref[0] = st
    m_rows = jnp.broadcast_to(m_col, (M_HEADS, LANES))
    m_rows = jnp.concatenate([m_rows, m_rows], axis=0)
    m_s[...] = m_rows
    m_ref[0] = m_rows
    tm = p_kva.shape[0]
    kwin_ref[0] = p_kva[tm - WINDOW:, 0:A_KV_W]
    vwin_ref[0] = p_kva[tm - WINDOW:, A_KV_W:]
    kvp_s[...] = p_kva[tm - WINDOW:, :]


def _cast_rows(rows, n_steps):
    rb = BF16_ROWS
    while rows % rb or rows // rb > n_steps:
        rb += BF16_ROWS
    return rb


def _front(x2d, nb, sinks, nw, wmt, wqgt, wn, bd, qcol, krow, bias_col, nwm_col, cast_ws, tm):
    n = x2d.shape[0]
    n_tiles = n // tm
    tps = n_tiles // nb
    n_blk = tm // LANES
    w_qvo = M_QK_W + 2 * M_V_W
    cur = lambda w: pl.BlockSpec((tm, w), lambda k: (jnp.minimum(k, n_tiles - 1), 0))
    prev = lambda w: pl.BlockSpec((tm, w), lambda k: (jnp.maximum(k - 1, 0), 0))
    per_seq = lambda r, w: pl.BlockSpec((1, r, w), lambda k: (jnp.maximum(k - 1, 0) // tps, 0, 0))

    def cast_spec(a):
        rb = _cast_rows(a.shape[0], n_tiles)
        return pl.BlockSpec((rb, a.shape[1]), lambda k: (jnp.minimum(k, a.shape[0] // rb - 1), 0))

    cast_in_specs = [cast_spec(a) for a in cast_ws]
    cast_out_specs = [cast_spec(a) for a in cast_ws]
    proj_scratch = [pltpu.VMEM((n_blk, w_qvo, LANES), BF16), pltpu.VMEM((n_blk, 2 * M_HEADS, LANES), F32),
                    pltpu.VMEM((n_blk, A_Q_W, LANES), BF16), pltpu.VMEM((tm, M_QK_W), BF16),
                    pltpu.VMEM((tm, 2 * A_KV_W), F32)]
    return pl.pallas_call(
        functools.partial(_front_kernel, tps, len(cast_ws)),
        grid=(n_tiles + 1,),
        in_specs=[pl.BlockSpec(memory_space=pltpu.SMEM), cur(D_MODEL)]
                 + [_const_spec(a.shape) for a in (nw, wmt, wqgt, wn, bd, qcol, krow, bias_col, nwm_col)]
                 + cast_in_specs,
        out_specs=[cur(2 * D_MODEL), prev(M_V_W), prev(A_Q_W), per_seq(WINDOW, A_KV_W), per_seq(WINDOW, A_KV_W),
                   per_seq(S_ROWS, M_QK_W), per_seq(SUBLANES, LANES)] + cast_out_specs,
        out_shape=[jax.ShapeDtypeStruct((n, 2 * D_MODEL), BF16),
                   jax.ShapeDtypeStruct((n, M_V_W), BF16),
                   jax.ShapeDtypeStruct((n, A_Q_W), BF16),
                   jax.ShapeDtypeStruct((nb, WINDOW, A_KV_W), F32),
                   jax.ShapeDtypeStruct((nb, WINDOW, A_KV_W), F32),
                   jax.ShapeDtypeStruct((nb, S_ROWS, M_QK_W), F32),
                   jax.ShapeDtypeStruct((nb, SUBLANES, LANES), F32)]
                  + [jax.ShapeDtypeStruct(a.shape, BF16) for a in cast_ws],
        scratch_shapes=proj_scratch + proj_scratch + [pltpu.VMEM((S_ROWS, M_QK_W), F32),
                                                       pltpu.VMEM((SUBLANES, LANES), F32),
                                                       pltpu.VMEM((WINDOW, 2 * A_KV_W), F32)],
        compiler_params=_params(("arbitrary",)),
        name="front",
    )(sinks, x2d, nw, wmt, wqgt, wn, bd, qcol, krow, bias_col, nwm_col, *cast_ws)


def _swa_sample_kernel(seq_len, sink_ref, qt_ref, kv_ref, ck_ref, cv_ref, h_ref, kwin_ref, vwin_ref):
    n_seq = GROUP // seq_len
    wb = ck_ref.shape[2]
    lane = lax.broadcasted_iota(jnp.int32, (1, LANES), 1)
    lo_half = lane < HALF
    q_rows = qt_ref[0].astype(F32).T
    kv_new = kv_ref[...]
    k_new = kv_new[:, 0:A_KV_W].reshape(n_seq, seq_len, A_KV_W)
    v_new = kv_new[:, A_KV_W:].reshape(n_seq, seq_len, A_KV_W)
    kt_new, vt_new = kv_new[:, 0:A_KV_W].T, kv_new[:, A_KV_W:].T
    for s in range(n_seq):
        put = (wb - seq_len - s * seq_len) % LANES
        for new_t, c_ref, win_ref in ((kt_new, ck_ref, kwin_ref), (vt_new, cv_ref, vwin_ref)):
            win_ref[s] = jnp.where(lane >= wb - seq_len, pltpu.roll(new_t, put, axis=1) if put else new_t,
                                   pltpu.roll(c_ref[s], wb - seq_len, axis=1))

    def to_kv_half(x, head):
        kvh = head // A_GROUPS
        if head % 2 != kvh:
            x = pltpu.roll(x, HALF, axis=1)
        return jnp.where(lo_half if kvh == 0 else ~lo_half, x, 0.0)

    lhs = jnp.concatenate(
        [to_kv_half(q_rows[:, (h // 2) * LANES:(h // 2 + 1) * LANES], h).reshape(n_seq, seq_len, LANES)
         for h in range(A_HEADS)], axis=1).astype(BF16)
    zpad = jnp.zeros((n_seq, BF16_ROWS - seq_len, A_KV_W), F32)
    k_nb = jnp.concatenate([k_new, zpad], axis=1).astype(BF16)
    v_nb = jnp.concatenate([v_new, zpad], axis=1).astype(BF16)
    s_c = jnp.einsum('sqf,sfk->sqk', lhs, ck_ref[...].astype(BF16), preferred_element_type=F32)
    s_n = jnp.einsum('sqf,skf->sqk', lhs, k_nb, preferred_element_type=F32)
    nrow = A_HEADS * seq_len
    ti = lax.broadcasted_iota(jnp.int32, (nrow, wb), 0) & (seq_len - 1)
    ki = lax.broadcasted_iota(jnp.int32, (nrow, wb), 1)
    mask_c = (ti + wb - ki) < WINDOW
    ti_n = lax.broadcasted_iota(jnp.int32, (nrow, BF16_ROWS), 0) & (seq_len - 1)
    ki_n = lax.broadcasted_iota(jnp.int32, (nrow, BF16_ROWS), 1)
    mask_n = ki_n <= ti_n
    row_head = lax.broadcasted_iota(jnp.int32, (nrow, 1), 0) >> (seq_len.bit_length() - 1)
    sk = jnp.zeros((nrow, 1), F32)
    for h in range(A_HEADS):
        sk = jnp.where(row_head == h, sink_ref[h], sk)
    s_c = jnp.where(mask_c, s_c, -jnp.inf)
    s_n = jnp.where(mask_n, s_n, -jnp.inf)
    mx = jnp.maximum(jnp.maximum(jnp.max(s_c, axis=-1, keepdims=True), jnp.max(s_n, axis=-1, keepdims=True)), sk)
    p_c = jnp.exp(s_c - mx)
    p_n = jnp.exp(s_n - mx)
    den = jnp.sum(p_c, axis=-1, keepdims=True) + jnp.sum(p_n, axis=-1, keepdims=True) + jnp.exp(sk - mx)
    o = (jnp.einsum('sqk,sfk->sqf', p_c.astype(BF16), cv_ref[...].astype(BF16), preferred_element_type=F32)
         + jnp.einsum('sqk,skf->sqf', p_n.astype(BF16), v_nb, preferred_element_type=F32)) / den

    def from_kv_half(head):
        x = o[:, head * seq_len:(head + 1) * seq_len, :].reshape(GROUP, LANES)
        return pltpu.roll(x, HALF, axis=1) if head % 2 != head // A_GROUPS else x

    for c in range(A_HEADS // 2):
        h_ref[:, c * LANES:(c + 1) * LANES] = jnp.where(lo_half, from_kv_half(2 * c),
                                                        from_kv_half(2 * c + 1)).astype(h_ref.dtype)


def _swa_sample(qat, kv_a, cache_k, cache_v, sinks, seq_len):
    ngrp = qat.shape[0]
    n_seq = GROUP // seq_len
    wb = cache_k.shape[2]
    assert wb == LANES, "window positions fill one lane-width"
    row = lambda w: pl.BlockSpec((GROUP, w), lambda i: (i, 0))
    cache = pl.BlockSpec((n_seq, A_KV_W, wb), lambda i: (i, 0, 0))
    return pl.pallas_call(
        functools.partial(_swa_sample_kernel, seq_len),
        grid=(ngrp,),
        in_specs=[pl.BlockSpec(memory_space=pltpu.SMEM), pl.BlockSpec((1, A_Q_W, LANES), lambda i: (i, 0, 0)),
                  row(2 * A_KV_W), cache, cache],
        out_specs=[row(A_Q_W), cache, cache],
        out_shape=[jax.ShapeDtypeStruct((ngrp * GROUP, A_Q_W), BF16),
                   jax.ShapeDtypeStruct(cache_k.shape, F32),
                   jax.ShapeDtypeStruct(cache_v.shape, F32)],
        compiler_params=_params(("arbitrary",)),
        name="swa_sample",
    )(sinks, qat, kv_a, cache_k, cache_v)


def _merge_ffn_kernel(x_ref, hm_ref, ha_ref, gab_ref, wa_ref, wb_ref, wo_ref, nw_ref, wg_ref, wu_ref, wd_ref,
                      y_ref):
    ga = jax.nn.sigmoid(gab_ref[:, 0:D_MODEL].astype(F32))
    gb = jax.nn.sigmoid(gab_ref[:, D_MODEL:].astype(F32))
    mix = ga * _dot(hm_ref[...], wa_ref[...]) + gb * _dot(ha_ref[...], wb_ref[...])
    x1 = x_ref[...] + _dot(mix.astype(BF16), wo_ref[...])
    hf = _rms_rows(x1, nw_ref[...]).astype(BF16)
    gate = _dot(hf, wg_ref[...])
    up = _dot(hf, wu_ref[...])
    act = (jax.nn.silu(gate) * up).astype(BF16)
    y_ref[...] = x1 + _dot(act, wd_ref[...])


def _merge_ffn(x2d, h_m, h_a, g_ab, wa, wb, wo, nw, wg, wu, wd, tm):
    n = x2d.shape[0]
    row = lambda w: pl.BlockSpec((tm, w), lambda i: (i, 0))
    return pl.pallas_call(
        _merge_ffn_kernel,
        grid=(n // tm,),
        in_specs=[row(D_MODEL), row(M_V_W), row(A_Q_W), row(2 * D_MODEL)]
                 + [_const_spec(w.shape) for w in (wa, wb, wo, nw, wg, wu, wd)],
        out_specs=row(D_MODEL),
        out_shape=jax.ShapeDtypeStruct((n, D_MODEL), F32),
        compiler_params=_params(("arbitrary",)),
        name="merge_ffn",
    )(x2d, h_m, h_a, g_ab, wa, wb, wo, nw, wg, wu, wd)


def kernel(x_prompt, x_sample, state_mlstm_C, state_mlstm_n, state_mlstm_m, cache_swa_k, cache_swa_v,
           norm_mix_w, w_in, mlstm_i_bias, mlstm_f_bias, mlstm_norm_w, q_norm_w, k_norm_w, attn_sinks,
           w_branch_a, w_branch_b, w_out, norm_ffn_w, w_gate, w_up, w_down):
    depth = w_in.shape[0]
    assert depth == 1, "single trunk layer"
    l = 0
    bp, tp = x_prompt.shape[0], x_prompt.shape[1]
    bs, ts = x_sample.shape[0], x_sample.shape[1]
    assert tp % 512 == 0 and (bs * ts) % GROUP == 0 and GROUP % ts == 0 and ts & (ts - 1) == 0
    assert ts <= SUBLANES, "sample chunk must fit one sublane tile"

    wt = jnp.transpose(w_in[l])
    c_km, c_vm = M_QK_W, 2 * M_QK_W
    c_g = 2 * M_QK_W + 2 * M_V_W
    c_qa = c_g + 2 * M_HEADS
    c_ka = c_qa + A_Q_W
    gate_pad = jnp.zeros((BF16_ROWS - 2 * M_HEADS, D_MODEL), F32)
    wmt = jnp.concatenate([wt[0:c_km], wt[c_vm:c_g]], axis=0).astype(BF16)
    wqgt = jnp.concatenate([wt[c_qa:c_ka], wt[c_g:c_qa], gate_pad], axis=0).astype(BF16)
    wn = jnp.concatenate([wt[c_km:c_vm], wt[c_ka:]], axis=0).astype(BF16)
    head_of = jnp.arange(A_KV_W) // A_HEAD_DIM
    bd = (head_of[:, None] == head_of[None, :]).astype(BF16)
    qcol = (jnp.tile(q_norm_w[l], A_HEADS) * (A_HEAD_DIM ** -0.5)).reshape(A_Q_W, 1)
    krow = jnp.tile(k_norm_w[l], A_KV_HEADS).reshape(1, A_KV_W)
    nw_mix = norm_mix_w[l].reshape(1, D_MODEL)
    nw_ffn = norm_ffn_w[l].reshape(1, D_MODEL)
    bias_col = jnp.concatenate([mlstm_i_bias[l], mlstm_f_bias[l]]).reshape(2 * M_HEADS, 1)
    nw_col = mlstm_norm_w[l].reshape(M_V_W, 1)
    sinks = attn_sinks[l]
    proj = lambda x2d, tm: _proj(x2d, nw_mix, wmt, wqgt, wn, bd, qcol, krow, tm)

    xp = x_prompt.reshape(bp * tp, D_MODEL)
    later_ws = (w_branch_a[l], w_branch_b[l], w_out[l], w_gate[l], w_up[l], w_down[l])
    g_ab, h_m, h_a, kwin_p, vwin_p, st_p, m_p, wa, wb, wo, wg, wu, wd = _front(
        xp, bp, sinks, nw_mix, wmt, wqgt, wn, bd, qcol, krow, bias_col, nw_col, later_ws, 512)
    merge = lambda x2d, h_m, h_a, g_ab, tm: _merge_ffn(x2d, h_m, h_a, g_ab, wa, wb, wo, nw_ffn, wg, wu, wd, tm)
    yp = merge(xp, h_m, h_a, g_ab, 512).reshape(bp, tp, D_MODEL)
    c_p = jnp.swapaxes(st_p[:, :M_DV, :], 1, 2).reshape(bp, M_HEADS, M_DK, M_DV)
    n_p = st_p[:, M_DV, :].reshape(bp, M_HEADS, M_DK)
    m_pr = m_p[:, :M_HEADS, 0]

    ns = bs * ts
    xs = x_sample.reshape(ns, D_MODEL)
    tms = 512 if ns % 512 == 0 else GROUP
    qvot, gt, qat, k_m, kv_a, g_ab = proj(xs, tms)
    ngrp = ns // GROUP
    m_lanes = jnp.repeat(state_mlstm_m[l], ts, axis=0).reshape(ngrp, GROUP, M_HEADS)
    mrow = jnp.pad(jnp.swapaxes(m_lanes, 1, 2), ((0, 0), (0, SUBLANES - M_HEADS), (0, 0)))
    h_m, c_s, n_s, mt_s = _mlstm_sample(qvot, gt, k_m, mrow, bias_col, nw_col,
                                        state_mlstm_C[l].reshape(bs, M_QK_W, M_DV),
                                        state_mlstm_n[l].reshape(bs, M_QK_W), ts)
    wbuf = cache_swa_k.shape[2]
    to_fm = lambda a: jnp.transpose(a, (0, 2, 3, 1)).reshape(bs, A_KV_W, wbuf)
    from_fm = lambda a: jnp.transpose(a.reshape(bs, A_KV_HEADS, A_HEAD_DIM, wbuf), (0, 3, 1, 2))[None]
    h_a, kwin_s, vwin_s = _swa_sample(qat, kv_a, to_fm(cache_swa_k[l]), to_fm(cache_swa_v[l]), sinks, ts)
    ys = merge(xs, h_m, h_a, g_ab, tms).reshape(bs, ts, D_MODEL)
    m_s = jnp.swapaxes(mt_s[:, :M_HEADS, :], 1, 2).reshape(bs, ts, M_HEADS)[:, ts - 1, :]

    kv5 = lambda a: a.reshape(a.shape[0], a.shape[1], A_KV_HEADS, A_HEAD_DIM)[None]
    return (yp, ys,
            c_p[None], n_p[None], m_pr[None], kv5(kwin_p), kv5(vwin_p),
            c_s.reshape(bs, M_HEADS, M_DK, M_DV)[None], n_s.reshape(bs, M_HEADS, M_DK)[None], m_s[None],
            from_fm(kwin_s), from_fm(vwin_s))
```

```python
import functools

import jax
import jax.numpy as jnp
from jax import lax
from jax.experimental import pallas as pl
from jax.experimental.pallas import tpu as pltpu

F32 = jnp.float32
BF16 = jnp.bfloat16

D_MODEL = 1024
M_HEADS = 4
M_DK = 64
M_DV = 128
M_CHUNK = 64
M_QK_W = M_HEADS * M_DK
M_V_W = M_HEADS * M_DV
A_HEADS = 8
A_KV_HEADS = 2
A_HEAD_DIM = 64
A_GROUPS = A_HEADS // A_KV_HEADS
A_Q_W = A_HEADS * A_HEAD_DIM
A_KV_W = A_KV_HEADS * A_HEAD_DIM
WINDOW = 128
D_FF = 2816
EPS = 1e-6

LANES = 128
SUBLANES = 8
BF16_ROWS = 16
GROUP = 128
S_ROWS = M_DV + BF16_ROWS
VMEM_LIMIT = 56 * 1024 * 1024
DK_SHIFT = M_DK.bit_length() - 1
WINDOW_SHIFT = WINDOW.bit_length() - 1
HALF = LANES // 2
assert A_HEAD_DIM == HALF and A_KV_W == LANES, "attention head pairs share one lane-width"

NT_DIMS = (((1,), (1,)), ((), ()))


def _dot(a, b):
    return jnp.dot(a, b, preferred_element_type=F32)


def _dot_nt(a, b):
    return lax.dot_general(a, b, NT_DIMS, preferred_element_type=F32)


def _const_spec(shape):
    nd = len(shape)
    return pl.BlockSpec(shape, lambda *_: (0,) * nd, pipeline_mode=pl.Buffered(1))


def _params(sem):
    return pltpu.CompilerParams(dimension_semantics=sem, vmem_limit_bytes=VMEM_LIMIT)


def _rms_rows(x, nw):
    ms = jnp.mean(x * x, axis=-1, keepdims=True)
    return (x * lax.rsqrt(ms + EPS)) * nw


GAB_CHUNK = 512


def _proj_chunks(x_ref, nw_ref, wmt_ref, wqgt_ref, wn_ref, bd_ref, qcol_ref, krow_ref,
                 qvot_ref, gt_ref, qat_ref, km_ref, kva_ref, gab_ref):
    hn = _rms_rows(x_ref[...], nw_ref[...]).astype(BF16)
    n_blk = qat_ref.shape[0]

    def put(ref, rows, val):
        for c in range(n_blk):
            ref[c, rows, :] = val[:, c * LANES:(c + 1) * LANES].astype(ref.dtype)

    def mlstm_qvo():
        put(qvot_ref, slice(None), _dot_nt(wmt_ref[...], hn))

    def attn_q_and_gates():
        qt = _dot_nt(wqgt_ref[...], hn)
        put(gt_ref, slice(None), qt[A_Q_W:A_Q_W + 2 * M_HEADS])
        for h in range(A_HEADS):
            hs = slice(h * A_HEAD_DIM, (h + 1) * A_HEAD_DIM)
            blk = qt[hs]
            ssq_q = jnp.sum(blk * blk, axis=0, keepdims=True)
            put(qat_ref, hs, (blk * lax.rsqrt(ssq_q * (1.0 / A_HEAD_DIM) + EPS)) * qcol_ref[hs])

    def mlstm_k():
        km_ref[...] = _dot_nt(hn, wn_ref[0:M_QK_W, :]).astype(km_ref.dtype)

    def branch_gates(c0):
        def run():
            w0 = M_QK_W + 2 * A_KV_W + c0
            gab_ref[:, c0:c0 + GAB_CHUNK] = _dot_nt(hn, wn_ref[w0:w0 + GAB_CHUNK, :]).astype(gab_ref.dtype)
        return run

    def attn_kv():
        kv = _dot_nt(hn, wn_ref[M_QK_W:M_QK_W + 2 * A_KV_W, :])
        k = kv[:, 0:A_KV_W]
        ksq = k * k
        hi = ksq.astype(BF16)
        lo = (ksq - hi.astype(F32)).astype(BF16)
        ssq = _dot(hi, bd_ref[...]) + _dot(lo, bd_ref[...])
        kva_ref[:, 0:A_KV_W] = (k * lax.rsqrt(ssq * (1.0 / A_HEAD_DIM) + EPS)) * krow_ref[...]
        kva_ref[:, A_KV_W:] = kv[:, A_KV_W:]

    return ([mlstm_qvo, attn_q_and_gates, mlstm_k]
            + [branch_gates(c0) for c0 in range(0, gab_ref.shape[1], GAB_CHUNK)] + [attn_kv])


def _proj_kernel(*refs):
    for piece in _proj_chunks(*refs):
        piece()


def _proj(x2d, nw, wmt, wqgt, wn, bd, qcol, krow, tm):
    n = x2d.shape[0]
    row = lambda w: pl.BlockSpec((tm, w), lambda i: (i, 0))
    n_blk = tm // LANES
    slab = lambda r: pl.BlockSpec((n_blk, r, LANES), lambda i: (i, 0, 0))
    w_qvo = M_QK_W + 2 * M_V_W
    return pl.pallas_call(
        _proj_kernel,
        grid=(n // tm,),
        in_specs=[row(D_MODEL)] + [_const_spec(a.shape) for a in (nw, wmt, wqgt, wn, bd, qcol, krow)],
        out_specs=[slab(w_qvo), slab(2 * M_HEADS), slab(A_Q_W), row(M_QK_W), row(2 * A_KV_W), row(2 * D_MODEL)],
        out_shape=[jax.ShapeDtypeStruct((n // LANES, w_qvo, LANES), BF16),
                   jax.ShapeDtypeStruct((n // LANES, 2 * M_HEADS, LANES), F32),
                   jax.ShapeDtypeStruct((n // LANES, A_Q_W, LANES), BF16),
                   jax.ShapeDtypeStruct((n, M_QK_W), BF16),
                   jax.ShapeDtypeStruct((n, 2 * A_KV_W), F32),
                   jax.ShapeDtypeStruct((n, 2 * D_MODEL), BF16)],
        compiler_params=_params(("arbitrary",)),
        name="proj",
    )(x2d, nw, wmt, wqgt, wn, bd, qcol, krow)


def _split3_rows(x):
    hi = x.astype(BF16).astype(F32)
    r1 = x - hi
    mid = r1.astype(BF16).astype(F32)
    lo = r1 - mid
    return jnp.concatenate([hi, mid, lo], axis=0).astype(BF16)


def _log_sigmoid(x):
    return jnp.minimum(x, 0.0) - jnp.log1p(jnp.exp(-jnp.abs(x)))


def _chunk_masks(chunk_shift):
    s = lax.broadcasted_iota(jnp.int32, (GROUP, GROUP), 0)
    t = lax.broadcasted_iota(jnp.int32, (GROUP, GROUP), 1)
    same = (s >> chunk_shift) == (t >> chunk_shift)
    return same, same & (s <= t)


def _group_gates(gt, same, causal):
    cm_bf = jnp.where(causal, 1.0, 0.0).astype(BF16)
    lf = _log_sigmoid(gt)
    nr = gt.shape[0]
    bt3 = _dot(_split3_rows(lf), cm_bf)
    bt = (bt3[0:nr] + bt3[nr:2 * nr]) + bt3[2 * nr:3 * nr]
    b4 = bt[M_HEADS:2 * M_HEADS]
    a4 = gt[0:M_HEADS] - b4
    a8 = jnp.concatenate([a4, a4], axis=0)
    a_cols = jnp.concatenate([a8, jnp.zeros((GROUP - SUBLANES, GROUP), F32)], axis=0).T
    at_mats, run_rows, chunk_rows = [], [], []
    for h in range(M_HEADS):
        at = jnp.broadcast_to(a_cols[:, h:h + 1], (GROUP, GROUP))
        at_mats.append(at)
        run_rows.append(jnp.max(jnp.where(causal, at, -jnp.inf), axis=0, keepdims=True))
        chunk_rows.append(jnp.max(jnp.where(same, at, -jnp.inf), axis=0, keepdims=True))
    return b4, a4, at_mats, jnp.concatenate(run_rows, axis=0), jnp.concatenate(chunk_rows, axis=0)


def _group_weights(m_prev, b4, a4, run4, chunk4):
    big_m = jnp.maximum(m_prev, run4)
    m_last = jnp.maximum(m_prev, chunk4)
    w_inter = jnp.exp(m_prev - big_m)
    g_vec = jnp.exp(m_prev - m_last)
    w_last = jnp.exp(a4 - m_last)
    m_t = b4 + big_m
    return big_m, w_inter, g_vec, w_last, m_t, jnp.exp(-m_t)


def _group_scores(qvot, ks):
    lane_head = lax.broadcasted_iota(jnp.int32, (1, M_QK_W), 1) >> DK_SHIFT
    row_head = lax.broadcasted_iota(jnp.int32, (M_QK_W, 1), 0) >> DK_SHIFT
    qt = qvot(0, M_QK_W)
    k_stack = jnp.concatenate([jnp.where(lane_head == h, ks, jnp.zeros_like(ks)) for h in range(M_HEADS)], axis=0)
    qw = jnp.concatenate([jnp.where(row_head == h, qt, jnp.zeros_like(qt)) for h in range(M_HEADS)], axis=1)
    zero_blk = jnp.zeros((M_DK, GROUP), BF16)
    sc_t = []
    for h in range(0, M_HEADS, 2):
        q_pair = jnp.concatenate(
            [jnp.concatenate([qvot(h * M_DK, (h + 1) * M_DK), zero_blk], axis=1),
             jnp.concatenate([zero_blk, qvot((h + 1) * M_DK, (h + 2) * M_DK)], axis=1)], axis=0)
        sc = _dot(ks[:, h * M_DK:(h + 2) * M_DK], q_pair)
        sc_t += [sc[:, 0:GROUP], sc[:, GROUP:]]
    return k_stack, qw, sc_t


def _group_values(qvot, sc_t, at_mats, big_m, causal):
    ones_rows = jnp.where(lax.broadcasted_iota(jnp.int32, (BF16_ROWS, GROUP), 0) == 0, 1.0, 0.0).astype(BF16)
    zero_blk = jnp.zeros((GROUP, GROUP), BF16)
    vta, s_t = [], []
    for h in range(M_HEADS):
        w_t = jnp.where(causal, jnp.exp(at_mats[h] - big_m[h:h + 1]), 0.0)
        s_t.append((sc_t[h] * w_t).astype(BF16))
        vta.append(jnp.concatenate([qvot(M_QK_W + h * M_DV, M_QK_W + (h + 1) * M_DV), ones_rows], axis=0))
    intra = []
    for h in range(0, M_HEADS, 2):
        pair = jnp.concatenate([jnp.concatenate([s_t[h], zero_blk], axis=1),
                                jnp.concatenate([zero_blk, s_t[h + 1]], axis=1)], axis=0)
        intra.append(_dot(jnp.concatenate(vta[h:h + 2], axis=1), pair))
    return vta, jnp.concatenate(intra, axis=1)


def _weighted_values(vta, w_rows):
    return jnp.concatenate([(vta[h].astype(F32) * w_rows[h:h + 1]).astype(BF16) for h in range(M_HEADS)], axis=1)


def _lanes_x(rows):
    return jnp.concatenate([rows[h:h + 1] for h in range(M_HEADS)], axis=1)


def _group_out(qvot, inter, intra, w_inter, e_neg_m, nw_ref):
    outs = []
    out_all = inter * _lanes_x(w_inter) + intra
    for h in range(M_HEADS):
        out_t = out_all[:, h * GROUP:(h + 1) * GROUP]
        hh = out_t[0:M_DV] / jnp.maximum(jnp.abs(out_t[M_DV:M_DV + 1]), e_neg_m[h:h + 1])
        ms = jnp.mean(hh * hh, axis=0, keepdims=True)
        hn = (hh * lax.rsqrt(ms + EPS)) * nw_ref[h * M_DV:(h + 1) * M_DV]
        o_t = qvot(M_QK_W + M_V_W + h * M_DV, M_QK_W + M_V_W + (h + 1) * M_DV)
        outs.append(hn * jax.nn.sigmoid(o_t.astype(F32)))
    return jnp.concatenate(outs, axis=0).T


def _decay_row(g_vec, lane0):
    lane_head = lax.broadcasted_iota(jnp.int32, (1, M_QK_W), 1) >> DK_SHIFT
    g_row = jnp.zeros((1, M_QK_W), F32)
    for h in range(M_HEADS):
        g_row = jnp.where(lane_head == h, g_vec[h:h + 1, lane0:lane0 + 1], g_row)
    return g_row


def _mlstm_sample_kernel(seq_len, qvot_ref, gt_ref, k_ref, mrow_ref, bias_ref, nw_ref, c_ref, n_ref,
                         h_ref, c_out_ref, n_out_ref, mt_ref):
    n_seq = GROUP // seq_len
    shift = seq_len.bit_length() - 1
    same, causal = _chunk_masks(shift)
    lane_seq = lax.broadcasted_iota(jnp.int32, (1, GROUP), 1) >> shift
    lane_seq_x = jnp.concatenate([lane_seq] * M_HEADS, axis=1)
    row0 = lax.broadcasted_iota(jnp.int32, (BF16_ROWS, M_QK_W), 0) == 0
    qvot = lambda r0, r1: qvot_ref[0, r0:r1, :]
    ks = k_ref[...] * (M_DK ** -0.5)
    b4, a4, at_mats, run4, chunk4 = _group_gates(gt_ref[0] + bias_ref[...], same, causal)
    big_m, w_inter, g_vec, w_last, m_t, e_neg_m = _group_weights(mrow_ref[0, 0:M_HEADS, :], b4, a4, run4, chunk4)
    mt_ref[0] = jnp.concatenate([m_t, m_t], axis=0)
    k_stack, qw, sc_t = _group_scores(qvot, ks)

    st_old = []
    for s in range(n_seq):
        n_rows = jnp.where(row0, jnp.broadcast_to(n_ref[s:s + 1, :], (BF16_ROWS, M_QK_W)), 0.0)
        st_old.append(jnp.concatenate([c_ref[s].T, n_rows], axis=0))
    inter_all = _dot(jnp.concatenate(st_old, axis=0).astype(BF16), qw)
    vta, intra = _group_values(qvot, sc_t, at_mats, big_m, causal)
    inter = inter_all[0:S_ROWS]
    for s in range(1, n_seq):
        inter = jnp.where(lane_seq_x == s, inter_all[s * S_ROWS:(s + 1) * S_ROWS], inter)
    tall = jnp.concatenate([_weighted_values(vta, jnp.where(lane_seq == s, w_last, 0.0)) for s in range(n_seq)],
                           axis=0)
    d_st = _dot(tall, k_stack)
    for s in range(n_seq):
        st_new = _decay_row(g_vec, s * seq_len) * st_old[s] + d_st[s * S_ROWS:(s + 1) * S_ROWS]
        c_out_ref[s] = st_new[0:M_DV].T
        n_out_ref[s:s + 1, :] = st_new[M_DV:M_DV + 1]
    h_ref[...] = _group_out(qvot, inter, intra, w_inter, e_neg_m, nw_ref).astype(h_ref.dtype)


def _mlstm_sample(qvot, gt, k_m, mrow, bias_col, nw_col, c, n, seq_len):
    ngrp = qvot.shape[0]
    n_seq = GROUP // seq_len
    full = lambda a: pl.BlockSpec(a.shape, lambda i: (0,) * a.ndim)
    slab = lambda a: pl.BlockSpec((1,) + a.shape[1:], lambda i: (i, 0, 0))
    row = lambda w: pl.BlockSpec((GROUP, w), lambda i: (i, 0))
    c_spec = pl.BlockSpec((n_seq,) + c.shape[1:], lambda i: (i, 0, 0))
    n_spec = pl.BlockSpec((n_seq, n.shape[1]), lambda i: (i, 0))
    return pl.pallas_call(
        functools.partial(_mlstm_sample_kernel, seq_len),
        grid=(ngrp,),
        in_specs=[slab(qvot), slab(gt), row(M_QK_W), slab(mrow), full(bias_col), full(nw_col), c_spec, n_spec],
        out_specs=[row(M_V_W), c_spec, n_spec, slab(mrow)],
        out_shape=[jax.ShapeDtypeStruct((ngrp * GROUP, M_V_W), BF16),
                   jax.ShapeDtypeStruct(c.shape, F32),
                   jax.ShapeDtypeStruct(n.shape, F32),
                   jax.ShapeDtypeStruct(mrow.shape, F32)],
        compiler_params=_params(("arbitrary",)),
        name="mlstm_sample",
    )(qvot, gt, k_m, mrow, bias_col, nw_col, c, n)


def _swa_scores(qbs, has_prev, sink_ref, qt_ref, kv_ref, kvp_ref):
    hd = A_HEAD_DIM
    nq = A_GROUPS * WINDOW
    si = lax.broadcasted_iota(jnp.int32, (2 * WINDOW, nq), 0)
    qi = lax.broadcasted_iota(jnp.int32, (2 * WINDOW, nq), 1) & (WINDOW - 1)
    local = ((si < WINDOW) & (si > qi)) | ((si >= WINDOW) & (si - WINDOW <= qi))
    first = local & (has_prev | (si >= WINDOW))
    lane_grp = lax.broadcasted_iota(jnp.int32, (1, nq), 1) >> WINDOW_SHIFT
    kv_block = lambda i: kvp_ref[...] if i == 0 else kv_ref[(i - 1) * WINDOW:i * WINDOW, :]
    k_bf = lambda i: kv_block(i)[:, 0:A_KV_W].astype(BF16)
    vt_bf = lambda i: kv_block(i)[:, A_KV_W:].T.astype(BF16)
    zeros = jnp.zeros((hd, nq), BF16)
    sinks = []
    for kvh in range(A_KV_HEADS):
        sk = jnp.zeros((1, nq), F32)
        for g in range(A_GROUPS):
            sk = jnp.where(lane_grp == g, sink_ref[kvh * A_GROUPS + g], sk)
        sinks.append(sk)
    units = []
    for qb in qbs:
        kk = jnp.concatenate([k_bf(qb), k_bf(qb + 1)], axis=0)
        vt = jnp.concatenate([vt_bf(qb), vt_bf(qb + 1)], axis=1)
        mask = local if qb > 0 else first
        for kvh in range(A_KV_HEADS):
            q4t = jnp.concatenate(
                [qt_ref[qb, (kvh * A_GROUPS + g) * hd:(kvh * A_GROUPS + g + 1) * hd, :] for g in range(A_GROUPS)],
                axis=1)
            wq = jnp.concatenate([q4t, zeros] if kvh == 0 else [zeros, q4t], axis=0)
            units.append((jnp.where(mask, _dot(kk, wq), -jnp.inf), sinks[kvh], vt))
    return units


def _swa_finish(qbs, units, h_ref):
    hd = A_HEAD_DIM
    for i, qb in enumerate(qbs):
        pieces = []
        for kvh in range(A_KV_HEADS):
            s, sk, vt = units[i * A_KV_HEADS + kvh]
            mx = jnp.maximum(jnp.max(s, axis=0, keepdims=True), sk)
            p = jnp.exp(s - mx)
            den = jnp.sum(p, axis=0, keepdims=True) + jnp.exp(sk - mx)
            ot = _dot(vt[kvh * hd:(kvh + 1) * hd], p.astype(BF16)) / den
            pieces += [ot[:, g * WINDOW:(g + 1) * WINDOW] for g in range(A_GROUPS)]
        h_t = jnp.concatenate(pieces, axis=0)
        h_ref[qb * WINDOW:(qb + 1) * WINDOW, :] = h_t.T.astype(h_ref.dtype)


def _front_kernel(tiles_per_seq, n_cast, *refs):
    (sink_ref, x_ref, nw_ref, wmt_ref, wqgt_ref, wn_ref, bd_ref, qcol_ref, krow_ref, bias_ref, nwm_ref) = refs[:11]
    cast_in = refs[11:11 + n_cast]
    (gab_ref, hm_ref, ha_ref, kwin_ref, vwin_ref, st_ref, m_ref) = refs[11 + n_cast:18 + n_cast]
    cast_out = refs[18 + n_cast:18 + 2 * n_cast]
    set_a, set_b = refs[18 + 2 * n_cast:23 + 2 * n_cast], refs[23 + 2 * n_cast:28 + 2 * n_cast]
    st_s, m_s, kvp_s = refs[28 + 2 * n_cast:]
    odd = (pl.program_id(0) & 1) == 1
    for parity, q_set, p_set in ((jnp.logical_not(odd), set_a, set_b), (odd, set_b, set_a)):
        pl.when(parity)(functools.partial(
            _front_body, tiles_per_seq, sink_ref, x_ref, nw_ref, wmt_ref, wqgt_ref, wn_ref, bd_ref, qcol_ref,
            krow_ref, bias_ref, nwm_ref, gab_ref, hm_ref, ha_ref, kwin_ref, vwin_ref, st_ref, m_ref,
            *q_set, *p_set, st_s, m_s, kvp_s))
    for src, dst in zip(cast_in, cast_out):
        dst[...] = src[...].astype(dst.dtype)


def _front_body(tiles_per_seq, sink_ref, x_ref, nw_ref, wmt_ref, wqgt_ref, wn_ref, bd_ref, qcol_ref, krow_ref,
                bias_ref, nwm_ref,
                gab_ref, hm_ref, ha_ref, kwin_ref, vwin_ref, st_ref, m_ref,
                q_qvot, q_gt, q_qat, q_km, q_kva, p_qvot, p_gt, p_qat, p_km, p_kva, st_s, m_s, kvp_s):
    k = pl.program_id(0)

    @pl.when(k == 0)
    def _():
        for r in (p_qvot, p_gt, p_qat, p_km, p_kva, st_s, m_s, kvp_s):
            r[...] = jnp.zeros(r.shape, r.dtype)

    pieces = _proj_chunks(x_ref, nw_ref, wmt_ref, wqgt_ref, wn_ref, bd_ref, qcol_ref, krow_ref,
                          q_qvot, q_gt, q_qat, q_km, q_kva, gab_ref)

    seq_start = lax.rem(k - 1 + tiles_per_seq, tiles_per_seq) == 0
    n_blk = p_qat.shape[0]
    same, causal = _chunk_masks(GROUP.bit_length() - 1)
    st = jnp.where(seq_start, 0.0, st_s[...])
    m_col = jnp.where(seq_start, 0.0, m_s[0:M_HEADS, 0:1])

    gates = [_group_gates(p_gt[g] + bias_ref[...], same, causal) for g in range(n_blk)]
    for piece in pieces[0:1]:
        piece()
    grp = []
    for g in range(n_blk):
        qvot = lambda r0, r1, g=g: p_qvot[g, r0:r1, :]
        ks = p_km[g * GROUP:(g + 1) * GROUP, :] * (M_DK ** -0.5)
        b4, a4, at_mats, run4, chunk4 = gates[g]
        weights = _group_weights(jnp.broadcast_to(m_col, (M_HEADS, GROUP)), b4, a4, run4, chunk4)
        m_col = weights[4][:, GROUP - 1:GROUP]
        grp.append((qvot, at_mats, weights) + _group_scores(qvot, ks))
    has_prev = jnp.logical_not(seq_start)
    qb_lo, qb_hi = list(range(0, n_blk // 2)), list(range(n_blk // 2, n_blk))
    units_lo = _swa_scores(qb_lo, has_prev, sink_ref, p_qat, p_kva, kvp_s)
    inter = [_dot(st.astype(BF16), grp[0][4])]
    for piece in pieces[1:3]:
        piece()

    vals = []

    def group_values(g):
        qvot, at_mats, (big_m, _, _, w_last, _, _), k_stack, _, sc_t = grp[g]
        vta, intra = _group_values(qvot, sc_t, at_mats, big_m, causal)
        vals.append((intra, _dot(_weighted_values(vta, w_last), k_stack)))

    for g in range(0, n_blk // 2):
        group_values(g)
    _swa_finish(qb_lo[:1], units_lo[:A_KV_HEADS], ha_ref)
    for piece in pieces[3:4]:
        piece()
    for g in range(n_blk // 2, n_blk):
        group_values(g)
    _swa_finish(qb_lo[1:], units_lo[A_KV_HEADS:], ha_ref)
    units_hi = _swa_scores(qb_hi, has_prev, sink_ref, p_qat, p_kva, kvp_s)
    for piece in pieces[4:6]:
        piece()

    for g in range(n_blk):
        st = _decay_row(grp[g][2][2], 0) * st + vals[g][1]
        if g + 1 < n_blk:
            inter.append(_dot(st.astype(BF16), grp[g + 1][4]))
        if g == n_blk // 2 - 1:
            _swa_finish(qb_hi[:1], units_hi[:A_KV_HEADS], ha_ref)
            for piece in pieces[6:7]:
                piece()
    _swa_finish(qb_hi[1:], units_hi[A_KV_HEADS:], ha_ref)
    for piece in pieces[7:]:
        piece()

    for g in range(n_blk):
        qvot, _, (_, w_inter, _, _, _, e_neg_m) = grp[g][0:3]
        hm_ref[g * GROUP:(g + 1) * GROUP, :] = _group_out(qvot, inter[g], vals[g][0], w_inter, e_neg_m,
                                                          nwm_ref).astype(hm_ref.dtype)
    st_s[...] = st
    st_ref[0] = st
    m_rows = jnp.broadcast_to(m_col, (M_HEADS, LANES))
    m_rows = jnp.concatenate([m_rows, m_rows], axis=0)
    m_s[...] = m_rows
    m_ref[0] = m_rows
    tm = p_kva.shape[0]
    kwin_ref[0] = p_kva[tm - WINDOW:, 0:A_KV_W]
    vwin_ref[0] = p_kva[tm - WINDOW:, A_KV_W:]
    kvp_s[...] = p_kva[tm - WINDOW:, :]


def _cast_rows(rows, n_steps):
    rb = BF16_ROWS
    while rows % rb or rows // rb > n_steps:
        rb += BF16_ROWS
    return rb


def _front(x2d, nb, sinks, nw, wmt, wqgt, wn, bd, qcol, krow, bias_col, nwm_col, cast_ws, tm):
    n = x2d.shape[0]
    n_tiles = n // tm
    tps = n_tiles // nb
    n_blk = tm // LANES
    w_qvo = M_QK_W + 2 * M_V_W
    cur = lambda w: pl.BlockSpec((tm, w), lambda k: (jnp.minimum(k, n_tiles - 1), 0))
    prev = lambda w: pl.BlockSpec((tm, w), lambda k: (jnp.maximum(k - 1, 0), 0))
    per_seq = lambda r, w: pl.BlockSpec((1, r, w), lambda k: (jnp.maximum(k - 1, 0) // tps, 0, 0))

    def cast_spec(a):
        rb = _cast_rows(a.shape[0], n_tiles)
        return pl.BlockSpec((rb, a.shape[1]), lambda k: (jnp.minimum(k, a.shape[0] // rb - 1), 0))

    cast_in_specs = [cast_spec(a) for a in cast_ws]
    cast_out_specs = [cast_spec(a) for a in cast_ws]
    proj_scratch = [pltpu.VMEM((n_blk, w_qvo, LANES), BF16), pltpu.VMEM((n_blk, 2 * M_HEADS, LANES), F32),
                    pltpu.VMEM((n_blk, A_Q_W, LANES), BF16), pltpu.VMEM((tm, M_QK_W), BF16),
                    pltpu.VMEM((tm, 2 * A_KV_W), F32)]
    return pl.pallas_call(
        functools.partial(_front_kernel, tps, len(cast_ws)),
        grid=(n_tiles + 1,),
        in_specs=[pl.BlockSpec(memory_space=pltpu.SMEM), cur(D_MODEL)]
                 + [_const_spec(a.shape) for a in (nw, wmt, wqgt, wn, bd, qcol, krow, bias_col, nwm_col)]
                 + cast_in_specs,
        out_specs=[cur(2 * D_MODEL), prev(M_V_W), prev(A_Q_W), per_seq(WINDOW, A_KV_W), per_seq(WINDOW, A_KV_W),
                   per_seq(S_ROWS, M_QK_W), per_seq(SUBLANES, LANES)] + cast_out_specs,
        out_shape=[jax.ShapeDtypeStruct((n, 2 * D_MODEL), BF16),
                   jax.ShapeDtypeStruct((n, M_V_W), BF16),
                   jax.ShapeDtypeStruct((n, A_Q_W), BF16),
                   jax.ShapeDtypeStruct((nb, WINDOW, A_KV_W), F32),
                   jax.ShapeDtypeStruct((nb, WINDOW, A_KV_W), F32),
                   jax.ShapeDtypeStruct((nb, S_ROWS, M_QK_W), F32),
                   jax.ShapeDtypeStruct((nb, SUBLANES, LANES), F32)]
                  + [jax.ShapeDtypeStruct(a.shape, BF16) for a in cast_ws],
        scratch_shapes=proj_scratch + proj_scratch + [pltpu.VMEM((S_ROWS, M_QK_W), F32),
                                                       pltpu.VMEM((SUBLANES, LANES), F32),
                                                       pltpu.VMEM((WINDOW, 2 * A_KV_W), F32)],
        compiler_params=_params(("arbitrary",)),
        name="front",
    )(sinks, x2d, nw, wmt, wqgt, wn, bd, qcol, krow, bias_col, nwm_col, *cast_ws)


def _swa_sample_kernel(seq_len, sink_ref, qt_ref, kv_ref, ck_ref, cv_ref, h_ref, kwin_ref, vwin_ref):
    n_seq = GROUP // seq_len
    wb = ck_ref.shape[2]
    lane = lax.broadcasted_iota(jnp.int32, (1, LANES), 1)
    lo_half = lane < HALF
    q_rows = qt_ref[0].astype(F32).T
    kv_new = kv_ref[...]
    k_new = kv_new[:, 0:A_KV_W].reshape(n_seq, seq_len, A_KV_W)
    v_new = kv_new[:, A_KV_W:].reshape(n_seq, seq_len, A_KV_W)
    kt_new, vt_new = kv_new[:, 0:A_KV_W].T, kv_new[:, A_KV_W:].T
    for s in range(n_seq):
        put = (wb - seq_len - s * seq_len) % LANES
        for new_t, c_ref, win_ref in ((kt_new, ck_ref, kwin_ref), (vt_new, cv_ref, vwin_ref)):
            win_ref[s] = jnp.where(lane >= wb - seq_len, pltpu.roll(new_t, put, axis=1) if put else new_t,
                                   pltpu.roll(c_ref[s], wb - seq_len, axis=1))

    def to_kv_half(x, head):
        kvh = head // A_GROUPS
        if head % 2 != kvh:
            x = pltpu.roll(x, HALF, axis=1)
        return jnp.where(lo_half if kvh == 0 else ~lo_half, x, 0.0)

    lhs = jnp.concatenate(
        [to_kv_half(q_rows[:, (h // 2) * LANES:(h // 2 + 1) * LANES], h).reshape(n_seq, seq_len, LANES)
         for h in range(A_HEADS)], axis=1).astype(BF16)
    zpad = jnp.zeros((n_seq, BF16_ROWS - seq_len, A_KV_W), F32)
    k_nb = jnp.concatenate([k_new, zpad], axis=1).astype(BF16)
    v_nb = jnp.concatenate([v_new, zpad], axis=1).astype(BF16)
    s_c = jnp.einsum('sqf,sfk->sqk', lhs, ck_ref[...].astype(BF16), preferred_element_type=F32)
    s_n = jnp.einsum('sqf,skf->sqk', lhs, k_nb, preferred_element_type=F32)
    nrow = A_HEADS * seq_len
    ti = lax.broadcasted_iota(jnp.int32, (nrow, wb), 0) & (seq_len - 1)
    ki = lax.broadcasted_iota(jnp.int32, (nrow, wb), 1)
    mask_c = (ti + wb - ki) < WINDOW
    ti_n = lax.broadcasted_iota(jnp.int32, (nrow, BF16_ROWS), 0) & (seq_len - 1)
    ki_n = lax.broadcasted_iota(jnp.int32, (nrow, BF16_ROWS), 1)
    mask_n = ki_n <= ti_n
    row_head = lax.broadcasted_iota(jnp.int32, (nrow, 1), 0) >> (seq_len.bit_length() - 1)
    sk = jnp.zeros((nrow, 1), F32)
    for h in range(A_HEADS):
        sk = jnp.where(row_head == h, sink_ref[h], sk)
    s_c = jnp.where(mask_c, s_c, -jnp.inf)
    s_n = jnp.where(mask_n, s_n, -jnp.inf)
    mx = jnp.maximum(jnp.maximum(jnp.max(s_c, axis=-1, keepdims=True), jnp.max(s_n, axis=-1, keepdims=True)), sk)
    p_c = jnp.exp(s_c - mx)
    p_n = jnp.exp(s_n - mx)
    den = jnp.sum(p_c, axis=-1, keepdims=True) + jnp.sum(p_n, axis=-1, keepdims=True) + jnp.exp(sk - mx)
    o = (jnp.einsum('sqk,sfk->sqf', p_c.astype(BF16), cv_ref[...].astype(BF16), preferred_element_type=F32)
         + jnp.einsum('sqk,skf->sqf', p_n.astype(BF16), v_nb, preferred_element_type=F32)) / den

    def from_kv_half(head):
        x = o[:, head * seq_len:(head + 1) * seq_len, :].reshape(GROUP, LANES)
        return pltpu.roll(x, HALF, axis=1) if head % 2 != head // A_GROUPS else x

    for c in range(A_HEADS // 2):
        h_ref[:, c * LANES:(c + 1) * LANES] = jnp.where(lo_half, from_kv_half(2 * c),
                                                        from_kv_half(2 * c + 1)).astype(h_ref.dtype)


def _swa_sample(qat, kv_a, cache_k, cache_v, sinks, seq_len):
    ngrp = qat.shape[0]
    n_seq = GROUP // seq_len
    wb = cache_k.shape[2]
    assert wb == LANES, "window positions fill one lane-width"
    row = lambda w: pl.BlockSpec((GROUP, w), lambda i: (i, 0))
    cache = pl.BlockSpec((n_seq, A_KV_W, wb), lambda i: (i, 0, 0))
    return pl.pallas_call(
        functools.partial(_swa_sample_kernel, seq_len),
        grid=(ngrp,),
        in_specs=[pl.BlockSpec(memory_space=pltpu.SMEM), pl.BlockSpec((1, A_Q_W, LANES), lambda i: (i, 0, 0)),
                  row(2 * A_KV_W), cache, cache],
        out_specs=[row(A_Q_W), cache, cache],
        out_shape=[jax.ShapeDtypeStruct((ngrp * GROUP, A_Q_W), BF16),
                   jax.ShapeDtypeStruct(cache_k.shape, F32),
                   jax.ShapeDtypeStruct(cache_v.shape, F32)],
        compiler_params=_params(("arbitrary",)),
        name="swa_sample",
    )(sinks, qat, kv_a, cache_k, cache_v)


def _merge_ffn_kernel(x_ref, hm_ref, ha_ref, gab_ref, wa_ref, wb_ref, wo_ref, nw_ref, wg_ref, wu_ref, wd_ref,
                      y_ref):
    ga = jax.nn.sigmoid(gab_ref[:, 0:D_MODEL].astype(F32))
    gb = jax.nn.sigmoid(gab_ref[:, D_MODEL:].astype(F32))
    mix = ga * _dot(hm_ref[...], wa_ref[...]) + gb * _dot(ha_ref[...], wb_ref[...])
    x1 = x_ref[...] + _dot(mix.astype(BF16), wo_ref[...])
    hf = _rms_rows(x1, nw_ref[...]).astype(BF16)
    gate = _dot(hf, wg_ref[...])
    up = _dot(hf, wu_ref[...])
    act = (jax.nn.silu(gate) * up).astype(BF16)
    y_ref[...] = x1 + _dot(act, wd_ref[...])


def _merge_ffn(x2d, h_m, h_a, g_ab, wa, wb, wo, nw, wg, wu, wd, tm):
    n = x2d.shape[0]
    row = lambda w: pl.BlockSpec((tm, w), lambda i: (i, 0))
    return pl.pallas_call(
        _merge_ffn_kernel,
        grid=(n // tm,),
        in_specs=[row(D_MODEL), row(M_V_W), row(A_Q_W), row(2 * D_MODEL)]
                 + [_const_spec(w.shape) for w in (wa, wb, wo, nw, wg, wu, wd)],
        out_specs=row(D_MODEL),
        out_shape=jax.ShapeDtypeStruct((n, D_MODEL), F32),
        compiler_params=_params(("arbitrary",)),
        name="merge_ffn",
    )(x2d, h_m, h_a, g_ab, wa, wb, wo, nw, wg, wu, wd)


def kernel(x_prompt, x_sample, state_mlstm_C, state_mlstm_n, state_mlstm_m, cache_swa_k, cache_swa_v,
           norm_mix_w, w_in, mlstm_i_bias, mlstm_f_bias, mlstm_norm_w, q_norm_w, k_norm_w, attn_sinks,
           w_branch_a, w_branch_b, w_out, norm_ffn_w, w_gate, w_up, w_down):
    depth = w_in.shape[0]
    assert depth == 1, "single trunk layer"
    l = 0
    bp, tp = x_prompt.shape[0], x_prompt.shape[1]
    bs, ts = x_sample.shape[0], x_sample.shape[1]
    assert tp % 512 == 0 and (bs * ts) % GROUP == 0 and GROUP % ts == 0 and ts & (ts - 1) == 0
    assert ts <= SUBLANES, "sample chunk must fit one sublane tile"

    wt = jnp.transpose(w_in[l])
    c_km, c_vm = M_QK_W, 2 * M_QK_W
    c_g = 2 * M_QK_W + 2 * M_V_W
    c_qa = c_g + 2 * M_HEADS
    c_ka = c_qa + A_Q_W
    gate_pad = jnp.zeros((BF16_ROWS - 2 * M_HEADS, D_MODEL), F32)
    wmt = jnp.concatenate([wt[0:c_km], wt[c_vm:c_g]], axis=0).astype(BF16)
    wqgt = jnp.concatenate([wt[c_qa:c_ka], wt[c_g:c_qa], gate_pad], axis=0).astype(BF16)
    wn = jnp.concatenate([wt[c_km:c_vm], wt[c_ka:]], axis=0).astype(BF16)
    head_of = jnp.arange(A_KV_W) // A_HEAD_DIM
    bd = (head_of[:, None] == head_of[None, :]).astype(BF16)
    qcol = (jnp.tile(q_norm_w[l], A_HEADS) * (A_HEAD_DIM ** -0.5)).reshape(A_Q_W, 1)
    krow = jnp.tile(k_norm_w[l], A_KV_HEADS).reshape(1, A_KV_W)
    nw_mix = norm_mix_w[l].reshape(1, D_MODEL)
    nw_ffn = norm_ffn_w[l].reshape(1, D_MODEL)
    bias_col = jnp.concatenate([mlstm_i_bias[l], mlstm_f_bias[l]]).reshape(2 * M_HEADS, 1)
    nw_col = mlstm_norm_w[l].reshape(M_V_W, 1)
    sinks = attn_sinks[l]
    proj = lambda x2d, tm: _proj(x2d, nw_mix, wmt, wqgt, wn, bd, qcol, krow, tm)

    xp = x_prompt.reshape(bp * tp, D_MODEL)
    later_ws = (w_branch_a[l], w_branch_b[l], w_out[l], w_gate[l], w_up[l], w_down[l])
    g_ab, h_m, h_a, kwin_p, vwin_p, st_p, m_p, wa, wb, wo, wg, wu, wd = _front(
        xp, bp, sinks, nw_mix, wmt, wqgt, wn, bd, qcol, krow, bias_col, nw_col, later_ws, 512)
    merge = lambda x2d, h_m, h_a, g_ab, tm: _merge_ffn(x2d, h_m, h_a, g_ab, wa, wb, wo, nw_ffn, wg, wu, wd, tm)
    yp = merge(xp, h_m, h_a, g_ab, 512).reshape(bp, tp, D_MODEL)
    c_p = jnp.swapaxes(st_p[:, :M_DV, :], 1, 2).reshape(bp, M_HEADS, M_DK, M_DV)
    n_p = st_p[:, M_DV, :].reshape(bp, M_HEADS, M_DK)
    m_pr = m_p[:, :M_HEADS, 0]

    ns = bs * ts
    xs = x_sample.reshape(ns, D_MODEL)
    tms = 512 if ns % 512 == 0 else GROUP
    qvot, gt, qat, k_m, kv_a, g_ab = proj(xs, tms)
    ngrp = ns // GROUP
    m_lanes = jnp.repeat(state_mlstm_m[l], ts, axis=0).reshape(ngrp, GROUP, M_HEADS)
    mrow = jnp.pad(jnp.swapaxes(m_lanes, 1, 2), ((0, 0), (0, SUBLANES - M_HEADS), (0, 0)))
    h_m, c_s, n_s, mt_s = _mlstm_sample(qvot, gt, k_m, mrow, bias_col, nw_col,
                                        state_mlstm_C[l].reshape(bs, M_QK_W, M_DV),
                                        state_mlstm_n[l].reshape(bs, M_QK_W), ts)
    wbuf = cache_swa_k.shape[2]
    to_fm = lambda a: jnp.transpose(a, (0, 2, 3, 1)).reshape(bs, A_KV_W, wbuf)
    from_fm = lambda a: jnp.transpose(a.reshape(bs, A_KV_HEADS, A_HEAD_DIM, wbuf), (0, 3, 1, 2))[None]
    h_a, kwin_s, vwin_s = _swa_sample(qat, kv_a, to_fm(cache_swa_k[l]), to_fm(cache_swa_v[l]), sinks, ts)
    ys = merge(xs, h_m, h_a, g_ab, tms).reshape(bs, ts, D_MODEL)
    m_s = jnp.swapaxes(mt_s[:, :M_HEADS, :], 1, 2).reshape(bs, ts, M_HEADS)[:, ts - 1, :]

    kv5 = lambda a: a.reshape(a.shape[0], a.shape[1], A_KV_HEADS, A_HEAD_DIM)[None]
    return (yp, ys,
            c_p[None], n_p[None], m_pr[None], kv5(kwin_p), kv5(vwin_p),
            c_s.reshape(bs, M_HEADS, M_DK, M_DV)[None], n_s.reshape(bs, M_HEADS, M_DK)[None], m_s[None],
            from_fm(kwin_s), from_fm(vwin_s))
```

```python
import functools

import jax
import jax.numpy as jnp
from jax import lax
from jax.experimental import pallas as pl
from jax.experimental.pallas import tpu as pltpu

F32 = jnp.float32
BF16 = jnp.bfloat16

D_MODEL = 1024
M_HEADS = 4
M_DK = 64
M_DV = 128
M_CHUNK = 64
M_QK_W = M_HEADS * M_DK
M_V_W = M_HEADS * M_DV
A_HEADS = 8
A_KV_HEADS = 2
A_HEAD_DIM = 64
A_GROUPS = A_HEADS // A_KV_HEADS
A_Q_W = A_HEADS * A_HEAD_DIM
A_KV_W = A_KV_HEADS * A_HEAD_DIM
WINDOW = 128
D_FF = 2816
EPS = 1e-6

LANES = 128
SUBLANES = 8
BF16_ROWS = 16
GROUP = 128
S_ROWS = M_DV + BF16_ROWS
VMEM_LIMIT = 56 * 1024 * 1024
DK_SHIFT = M_DK.bit_length() - 1
WINDOW_SHIFT = WINDOW.bit_length() - 1
HALF = LANES // 2
assert A_HEAD_DIM == HALF and A_KV_W == LANES, "attention head pairs share one lane-width"

NT_DIMS = (((1,), (1,)), ((), ()))


def _dot(a, b):
    return jnp.dot(a, b, preferred_element_type=F32)


def _dot_nt(a, b):
    return lax.dot_general(a, b, NT_DIMS, preferred_element_type=F32)


def _const_spec(shape):
    nd = len(shape)
    return pl.BlockSpec(shape, lambda *_: (0,) * nd, pipeline_mode=pl.Buffered(1))


def _params(sem):
    return pltpu.CompilerParams(dimension_semantics=sem, vmem_limit_bytes=VMEM_LIMIT)


def _rms_rows(x, nw):
    ms = jnp.mean(x * x, axis=-1, keepdims=True)
    return (x * lax.rsqrt(ms + EPS)) * nw


GAB_CHUNK = 512


def _proj_chunks(x_ref, nw_ref, wmt_ref, wqgt_ref, wn_ref, bd_ref, qcol_ref, krow_ref,
                 qvot_ref, gt_ref, qat_ref, km_ref, kva_ref, gab_ref):
    hn = _rms_rows(x_ref[...], nw_ref[...]).astype(BF16)
    n_blk = qat_ref.shape[0]

    def put(ref, rows, val):
        for c in range(n_blk):
            ref[c, rows, :] = val[:, c * LANES:(c + 1) * LANES].astype(ref.dtype)

    def mlstm_qvo():
        put(qvot_ref, slice(None), _dot_nt(wmt_ref[...], hn))

    def attn_q_and_gates():
        qt = _dot_nt(wqgt_ref[...], hn)
        put(gt_ref, slice(None), qt[A_Q_W:A_Q_W + 2 * M_HEADS])
        for h in range(A_HEADS):
            hs = slice(h * A_HEAD_DIM, (h + 1) * A_HEAD_DIM)
            blk = qt[hs]
            ssq_q = jnp.sum(blk * blk, axis=0, keepdims=True)
            put(qat_ref, hs, (blk * lax.rsqrt(ssq_q * (1.0 / A_HEAD_DIM) + EPS)) * qcol_ref[hs])

    def mlstm_k():
        km_ref[...] = _dot_nt(hn, wn_ref[0:M_QK_W, :]).astype(km_ref.dtype)

    def branch_gates(c0):
        def run():
            w0 = M_QK_W + 2 * A_KV_W + c0
            gab_ref[:, c0:c0 + GAB_CHUNK] = _dot_nt(hn, wn_ref[w0:w0 + GAB_CHUNK, :]).astype(gab_ref.dtype)
        return run

    def attn_kv():
        kv = _dot_nt(hn, wn_ref[M_QK_W:M_QK_W + 2 * A_KV_W, :])
        k = kv[:, 0:A_KV_W]
        ksq = k * k
        hi = ksq.astype(BF16)
        lo = (ksq - hi.astype(F32)).astype(BF16)
        ssq = _dot(hi, bd_ref[...]) + _dot(lo, bd_ref[...])
        kva_ref[:, 0:A_KV_W] = (k * lax.rsqrt(ssq * (1.0 / A_HEAD_DIM) + EPS)) * krow_ref[...]
        kva_ref[:, A_KV_W:] = kv[:, A_KV_W:]

    return ([mlstm_qvo, attn_q_and_gates, mlstm_k]
            + [branch_gates(c0) for c0 in range(0, gab_ref.shape[1], GAB_CHUNK)] + [attn_kv])


def _proj_kernel(*refs):
    for piece in _proj_chunks(*refs):
        piece()


def _proj(x2d, nw, wmt, wqgt, wn, bd, qcol, krow, tm):
    n = x2d.shape[0]
    row = lambda w: pl.BlockSpec((tm, w), lambda i: (i, 0))
    n_blk = tm // LANES
    slab = lambda r: pl.BlockSpec((n_blk, r, LANES), lambda i: (i, 0, 0))
    w_qvo = M_QK_W + 2 * M_V_W
    return pl.pallas_call(
        _proj_kernel,
        grid=(n // tm,),
        in_specs=[row(D_MODEL)] + [_const_spec(a.shape) for a in (nw, wmt, wqgt, wn, bd, qcol, krow)],
        out_specs=[slab(w_qvo), slab(2 * M_HEADS), slab(A_Q_W), row(M_QK_W), row(2 * A_KV_W), row(2 * D_MODEL)],
        out_shape=[jax.ShapeDtypeStruct((n // LANES, w_qvo, LANES), BF16),
                   jax.ShapeDtypeStruct((n // LANES, 2 * M_HEADS, LANES), F32),
                   jax.ShapeDtypeStruct((n // LANES, A_Q_W, LANES), BF16),
                   jax.ShapeDtypeStruct((n, M_QK_W), BF16),
                   jax.ShapeDtypeStruct((n, 2 * A_KV_W), F32),
                   jax.ShapeDtypeStruct((n, 2 * D_MODEL), BF16)],
        compiler_params=_params(("arbitrary",)),
        name="proj",
    )(x2d, nw, wmt, wqgt, wn, bd, qcol, krow)


def _split3_rows(x):
    hi = x.astype(BF16).astype(F32)
    r1 = x - hi
    mid = r1.astype(BF16).astype(F32)
    lo = r1 - mid
    return jnp.concatenate([hi, mid, lo], axis=0).astype(BF16)


def _log_sigmoid(x):
    return jnp.minimum(x, 0.0) - jnp.log1p(jnp.exp(-jnp.abs(x)))


def _chunk_masks(chunk_shift):
    s = lax.broadcasted_iota(jnp.int32, (GROUP, GROUP), 0)
    t = lax.broadcasted_iota(jnp.int32, (GROUP, GROUP), 1)
    same = (s >> chunk_shift) == (t >> chunk_shift)
    return same, same & (s <= t)


def _group_gates(gt, same, causal):
    cm_bf = jnp.where(causal, 1.0, 0.0).astype(BF16)
    lf = _log_sigmoid(gt)
    nr = gt.shape[0]
    bt3 = _dot(_split3_rows(lf), cm_bf)
    bt = (bt3[0:nr] + bt3[nr:2 * nr]) + bt3[2 * nr:3 * nr]
    b4 = bt[M_HEADS:2 * M_HEADS]
    a4 = gt[0:M_HEADS] - b4
    a8 = jnp.concatenate([a4, a4], axis=0)
    a_cols = jnp.concatenate([a8, jnp.zeros((GROUP - SUBLANES, GROUP), F32)], axis=0).T
    at_mats, run_rows, chunk_rows = [], [], []
    for h in range(M_HEADS):
        at = jnp.broadcast_to(a_cols[:, h:h + 1], (GROUP, GROUP))
        at_mats.append(at)
        run_rows.append(jnp.max(jnp.where(causal, at, -jnp.inf), axis=0, keepdims=True))
        chunk_rows.append(jnp.max(jnp.where(same, at, -jnp.inf), axis=0, keepdims=True))
    return b4, a4, at_mats, jnp.concatenate(run_rows, axis=0), jnp.concatenate(chunk_rows, axis=0)


def _group_weights(m_prev, b4, a4, run4, chunk4):
    big_m = jnp.maximum(m_prev, run4)
    m_last = jnp.maximum(m_prev, chunk4)
    w_inter = jnp.exp(m_prev - big_m)
    g_vec = jnp.exp(m_prev - m_last)
    w_last = jnp.exp(a4 - m_last)
    m_t = b4 + big_m
    return big_m, w_inter, g_vec, w_last, m_t, jnp.exp(-m_t)


def _group_scores(qvot, ks):
    lane_head = lax.broadcasted_iota(jnp.int32, (1, M_QK_W), 1) >> DK_SHIFT
    row_head = lax.broadcasted_iota(jnp.int32, (M_QK_W, 1), 0) >> DK_SHIFT
    qt = qvot(0, M_QK_W)
    k_stack = jnp.concatenate([jnp.where(lane_head == h, ks, jnp.zeros_like(ks)) for h in range(M_HEADS)], axis=0)
    qw = jnp.concatenate([jnp.where(row_head == h, qt, jnp.zeros_like(qt)) for h in range(M_HEADS)], axis=1)
    zero_blk = jnp.zeros((M_DK, GROUP), BF16)
    sc_t = []
    for h in range(0, M_HEADS, 2):
        q_pair = jnp.concatenate(
            [jnp.concatenate([qvot(h * M_DK, (h + 1) * M_DK), zero_blk], axis=1),
             jnp.concatenate([zero_blk, qvot((h + 1) * M_DK, (h + 2) * M_DK)], axis=1)], axis=0)
        sc = _dot(ks[:, h * M_DK:(h + 2) * M_DK], q_pair)
        sc_t += [sc[:, 0:GROUP], sc[:, GROUP:]]
    return k_stack, qw, sc_t


def _group_values(qvot, sc_t, at_mats, big_m, causal):
    ones_rows = jnp.where(lax.broadcasted_iota(jnp.int32, (BF16_ROWS, GROUP), 0) == 0, 1.0, 0.0).astype(BF16)
    zero_blk = jnp.zeros((GROUP, GROUP), BF16)
    vta, s_t = [], []
    for h in range(M_HEADS):
        w_t = jnp.where(causal, jnp.exp(at_mats[h] - big_m[h:h + 1]), 0.0)
        s_t.append((sc_t[h] * w_t).astype(BF16))
        vta.append(jnp.concatenate([qvot(M_QK_W + h * M_DV, M_QK_W + (h + 1) * M_DV), ones_rows], axis=0))
    intra = []
    for h in range(0, M_HEADS, 2):
        pair = jnp.concatenate([jnp.concatenate([s_t[h], zero_blk], axis=1),
                                jnp.concatenate([zero_blk, s_t[h + 1]], axis=1)], axis=0)
        intra.append(_dot(jnp.concatenate(vta[h:h + 2], axis=1), pair))
    return vta, jnp.concatenate(intra, axis=1)


def _weighted_values(vta, w_rows):
    return jnp.concatenate([(vta[h].astype(F32) * w_rows[h:h + 1]).astype(BF16) for h in range(M_HEADS)], axis=1)


def _lanes_x(rows):
    return jnp.concatenate([rows[h:h + 1] for h in range(M_HEADS)], axis=1)


def _group_out(qvot, inter, intra, w_inter, e_neg_m, nw_ref):
    outs = []
    out_all = inter * _lanes_x(w_inter) + intra
    for h in range(M_HEADS):
        out_t = out_all[:, h * GROUP:(h + 1) * GROUP]
        hh = out_t[0:M_DV] / jnp.maximum(jnp.abs(out_t[M_DV:M_DV + 1]), e_neg_m[h:h + 1])
        ms = jnp.mean(hh * hh, axis=0, keepdims=True)
        hn = (hh * lax.rsqrt(ms + EPS)) * nw_ref[h * M_DV:(h + 1) * M_DV]
        o_t = qvot(M_QK_W + M_V_W + h * M_DV, M_QK_W + M_V_W + (h + 1) * M_DV)
        outs.append(hn * jax.nn.sigmoid(o_t.astype(F32)))
    return jnp.concatenate(outs, axis=0).T


def _decay_row(g_vec, lane0):
    lane_head = lax.broadcasted_iota(jnp.int32, (1, M_QK_W), 1) >> DK_SHIFT
    g_row = jnp.zeros((1, M_QK_W), F32)
    for h in range(M_HEADS):
        g_row = jnp.where(lane_head == h, g_vec[h:h + 1, lane0:lane0 + 1], g_row)
    return g_row


def _mlstm_sample_kernel(seq_len, qvot_ref, gt_ref, k_ref, mrow_ref, bias_ref, nw_ref, c_ref, n_ref,
                         h_ref, c_out_ref, n_out_ref, mt_ref):
    n_seq = GROUP // seq_len
    shift = seq_len.bit_length() - 1
    same, causal = _chunk_masks(shift)
    lane_seq = lax.broadcasted_iota(jnp.int32, (1, GROUP), 1) >> shift
    lane_seq_x = jnp.concatenate([lane_seq] * M_HEADS, axis=1)
    row0 = lax.broadcasted_iota(jnp.int32, (BF16_ROWS, M_QK_W), 0) == 0
    qvot = lambda r0, r1: qvot_ref[0, r0:r1, :]
    ks = k_ref[...] * (M_DK ** -0.5)
    b4, a4, at_mats, run4, chunk4 = _group_gates(gt_ref[0] + bias_ref[...], same, causal)
    big_m, w_inter, g_vec, w_last, m_t, e_neg_m = _group_weights(mrow_ref[0, 0:M_HEADS, :], b4, a4, run4, chunk4)
    mt_ref[0] = jnp.concatenate([m_t, m_t], axis=0)
    k_stack, qw, sc_t = _group_scores(qvot, ks)

    st_old = []
    for s in range(n_seq):
        n_rows = jnp.where(row0, jnp.broadcast_to(n_ref[s:s + 1, :], (BF16_ROWS, M_QK_W)), 0.0)
        st_old.append(jnp.concatenate([c_ref[s].T, n_rows], axis=0))
    inter_all = _dot(jnp.concatenate(st_old, axis=0).astype(BF16), qw)
    vta, intra = _group_values(qvot, sc_t, at_mats, big_m, causal)
    inter = inter_all[0:S_ROWS]
    for s in range(1, n_seq):
        inter = jnp.where(lane_seq_x == s, inter_all[s * S_ROWS:(s + 1) * S_ROWS], inter)
    tall = jnp.concatenate([_weighted_values(vta, jnp.where(lane_seq == s, w_last, 0.0)) for s in range(n_seq)],
                           axis=0)
    d_st = _dot(tall, k_stack)
    for s in range(n_seq):
        st_new = _decay_row(g_vec, s * seq_len) * st_old[s] + d_st[s * S_ROWS:(s + 1) * S_ROWS]
        c_out_ref[s] = st_new[0:M_DV].T
        n_out_ref[s:s + 1, :] = st_new[M_DV:M_DV + 1]
    h_ref[...] = _group_out(qvot, inter, intra, w_inter, e_neg_m, nw_ref).astype(h_ref.dtype)


def _mlstm_sample(qvot, gt, k_m, mrow, bias_col, nw_col, c, n, seq_len):
    ngrp = qvot.shape[0]
    n_seq = GROUP // seq_len
    full = lambda a: pl.BlockSpec(a.shape, lambda i: (0,) * a.ndim)
    slab = lambda a: pl.BlockSpec((1,) + a.shape[1:], lambda i: (i, 0, 0))
    row = lambda w: pl.BlockSpec((GROUP, w), lambda i: (i, 0))
    c_spec = pl.BlockSpec((n_seq,) + c.shape[1:], lambda i: (i, 0, 0))
    n_spec = pl.BlockSpec((n_seq, n.shape[1]), lambda i: (i, 0))
    return pl.pallas_call(
        functools.partial(_mlstm_sample_kernel, seq_len),
        grid=(ngrp,),
        in_specs=[slab(qvot), slab(gt), row(M_QK_W), slab(mrow), full(bias_col), full(nw_col), c_spec, n_spec],
        out_specs=[row(M_V_W), c_spec, n_spec, slab(mrow)],
        out_shape=[jax.ShapeDtypeStruct((ngrp * GROUP, M_V_W), BF16),
                   jax.ShapeDtypeStruct(c.shape, F32),
                   jax.ShapeDtypeStruct(n.shape, F32),
                   jax.ShapeDtypeStruct(mrow.shape, F32)],
        compiler_params=_params(("arbitrary",)),
        name="mlstm_sample",
    )(qvot, gt, k_m, mrow, bias_col, nw_col, c, n)


def _swa_scores(qbs, has_prev, sink_ref, qt_ref, kv_ref, kvp_ref):
    hd = A_HEAD_DIM
    nq = A_GROUPS * WINDOW
    si = lax.broadcasted_iota(jnp.int32, (2 * WINDOW, nq), 0)
    qi = lax.broadcasted_iota(jnp.int32, (2 * WINDOW, nq), 1) & (WINDOW - 1)
    local = ((si < WINDOW) & (si > qi)) | ((si >= WINDOW) & (si - WINDOW <= qi))
    first = local & (has_prev | (si >= WINDOW))
    lane_grp = lax.broadcasted_iota(jnp.int32, (1, nq), 1) >> WINDOW_SHIFT
    kv_block = lambda i: kvp_ref[...] if i == 0 else kv_ref[(i - 1) * WINDOW:i * WINDOW, :]
    k_bf = lambda i: kv_block(i)[:, 0:A_KV_W].astype(BF16)
    vt_bf = lambda i: kv_block(i)[:, A_KV_W:].T.astype(BF16)
    zeros = jnp.zeros((hd, nq), BF16)
    sinks = []
    for kvh in range(A_KV_HEADS):
        sk = jnp.zeros((1, nq), F32)
        for g in range(A_GROUPS):
            sk = jnp.where(lane_grp == g, sink_ref[kvh * A_GROUPS + g], sk)
        sinks.append(sk)
    units = []
    for qb in qbs:
        kk = jnp.concatenate([k_bf(qb), k_bf(qb + 1)], axis=0)
        vt = jnp.concatenate([vt_bf(qb), vt_bf(qb + 1)], axis=1)
        mask = local if qb > 0 else first
        for kvh in range(A_KV_HEADS):
            q4t = jnp.concatenate(
                [qt_ref[qb, (kvh * A_GROUPS + g) * hd:(kvh * A_GROUPS + g + 1) * hd, :] for g in range(A_GROUPS)],
                axis=1)
            wq = jnp.concatenate([q4t, zeros] if kvh == 0 else [zeros, q4t], axis=0)
            units.append((jnp.where(mask, _dot(kk, wq), -jnp.inf), sinks[kvh], vt))
    return units


def _swa_finish(qbs, units, h_ref):
    hd = A_HEAD_DIM
    for i, qb in enumerate(qbs):
        pieces = []
        for kvh in range(A_KV_HEADS):
            s, sk, vt = units[i * A_KV_HEADS + kvh]
            mx = jnp.maximum(jnp.max(s, axis=0, keepdims=True), sk)
            p = jnp.exp(s - mx)
            den = jnp.sum(p, axis=0, keepdims=True) + jnp.exp(sk - mx)
            ot = _dot(vt[kvh * hd:(kvh + 1) * hd], p.astype(BF16)) / den
            pieces += [ot[:, g * WINDOW:(g + 1) * WINDOW] for g in range(A_GROUPS)]
        h_t = jnp.concatenate(pieces, axis=0)
        h_ref[qb * WINDOW:(qb + 1) * WINDOW, :] = h_t.T.astype(h_ref.dtype)


def _front_kernel(tiles_per_seq, n_cast, *refs):
    (sink_ref, x_ref, nw_ref, wmt_ref, wqgt_ref, wn_ref, bd_ref, qcol_ref, krow_ref, bias_ref, nwm_ref) = refs[:11]
    cast_in = refs[11:11 + n_cast]
    (gab_ref, hm_ref, ha_ref, kwin_ref, vwin_ref, st_ref, m_ref) = refs[11 + n_cast:18 + n_cast]
    cast_out = refs[18 + n_cast:18 + 2 * n_cast]
    set_a, set_b = refs[18 + 2 * n_cast:23 + 2 * n_cast], refs[23 + 2 * n_cast:28 + 2 * n_cast]
    st_s, m_s, kvp_s = refs[28 + 2 * n_cast:]
    odd = (pl.program_id(0) & 1) == 1
    for parity, q_set, p_set in ((jnp.logical_not(odd), set_a, set_b), (odd, set_b, set_a)):
        pl.when(parity)(functools.partial(
            _front_body, tiles_per_seq, sink_ref, x_ref, nw_ref, wmt_ref, wqgt_ref, wn_ref, bd_ref, qcol_ref,
            krow_ref, bias_ref, nwm_ref, gab_ref, hm_ref, ha_ref, kwin_ref, vwin_ref, st_ref, m_ref,
            *q_set, *p_set, st_s, m_s, kvp_s))
    for src, dst in zip(cast_in, cast_out):
        dst[...] = src[...].astype(dst.dtype)


def _front_body(tiles_per_seq, sink_ref, x_ref, nw_ref, wmt_ref, wqgt_ref, wn_ref, bd_ref, qcol_ref, krow_ref,
                bias_ref, nwm_ref,
                gab_ref, hm_ref, ha_ref, kwin_ref, vwin_ref, st_ref, m_ref,
                q_qvot, q_gt, q_qat, q_km, q_kva, p_qvot, p_gt, p_qat, p_km, p_kva, st_s, m_s, kvp_s):
    k = pl.program_id(0)

    @pl.when(k == 0)
    def _():
        for r in (p_qvot, p_gt, p_qat, p_km, p_kva, st_s, m_s, kvp_s):
            r[...] = jnp.zeros(r.shape, r.dtype)

    pieces = _proj_chunks(x_ref, nw_ref, wmt_ref, wqgt_ref, wn_ref, bd_ref, qcol_ref, krow_ref,
                          q_qvot, q_gt, q_qat, q_km, q_kva, gab_ref)

    seq_start = lax.rem(k - 1 + tiles_per_seq, tiles_per_seq) == 0
    n_blk = p_qat.shape[0]
    same, causal = _chunk_masks(GROUP.bit_length() - 1)
    st = jnp.where(seq_start, 0.0, st_s[...])
    m_col = jnp.where(seq_start, 0.0, m_s[0:M_HEADS, 0:1])

    gates = [_group_gates(p_gt[g] + bias_ref[...], same, causal) for g in range(n_blk)]
    for piece in pieces[0:1]:
        piece()
    grp = []
    for g in range(n_blk):
        qvot = lambda r0, r1, g=g: p_qvot[g, r0:r1, :]
        ks = p_km[g * GROUP:(g + 1) * GROUP, :] * (M_DK ** -0.5)
        b4, a4, at_mats, run4, chunk4 = gates[g]
        weights = _group_weights(jnp.broadcast_to(m_col, (M_HEADS, GROUP)), b4, a4, run4, chunk4)
        m_col = weights[4][:, GROUP - 1:GROUP]
        grp.append((qvot, at_mats, weights) + _group_scores(qvot, ks))
    has_prev = jnp.logical_not(seq_start)
    qb_lo, qb_hi = list(range(0, n_blk // 2)), list(range(n_blk // 2, n_blk))
    units_lo = _swa_scores(qb_lo, has_prev, sink_ref, p_qat, p_kva, kvp_s)
    inter = [_dot(st.astype(BF16), grp[0][4])]
    for piece in pieces[1:3]:
        piece()

    vals = []

    def group_values(g):
        qvot, at_mats, (big_m, _, _, w_last, _, _), k_stack, _, sc_t = grp[g]
        vta, intra = _group_values(qvot, sc_t, at_mats, big_m, causal)
        vals.append((intra, _dot(_weighted_values(vta, w_last), k_stack)))

    for g in range(0, n_blk // 2):
        group_values(g)
    _swa_finish(qb_lo[:1], units_lo[:A_KV_HEADS], ha_ref)
    for piece in pieces[3:4]:
        piece()
    for g in range(n_blk // 2, n_blk):
        group_values(g)
    _swa_finish(qb_lo[1:], units_lo[A_KV_HEADS:], ha_ref)
    units_hi = _swa_scores(qb_hi, has_prev, sink_ref, p_qat, p_kva, kvp_s)
    for piece in pieces[4:6]:
        piece()

    for g in range(n_blk):
        st = _decay_row(grp[g][2][2], 0) * st + vals[g][1]
        if g + 1 < n_blk:
            inter.append(_dot(st.astype(BF16), grp[g + 1][4]))
        if g == n_blk // 2 - 1:
            _swa_finish(qb_hi[:1], units_hi[:A_KV_HEADS], ha_ref)
            for piece in pieces[6:7]:
                piece()
    _swa_finish(qb_hi[1:], units_hi[A_KV_HEADS:], ha_ref)
    for piece in pieces[7:]:
        piece()

    for g in range(n_blk):
        qvot, _, (_, w_inter, _, _, _, e_neg_m) = grp[g][0:3]
        hm_ref[g * GROUP:(g + 1) * GROUP, :] = _group_out(qvot, inter[g], vals[g][0], w_inter, e_neg_m,
                                                          nwm_ref).astype(hm_ref.dtype)
    st_s[...] = st
    st_ref[0] = st
    m_rows = jnp.broadcast_to(m_col, (M_HEADS, LANES))
    m_rows = jnp.concatenate([m_rows, m_rows], axis=0)
    m_s[...] = m_rows
    m_ref[0] = m_rows
    tm = p_kva.shape[0]
    kwin_ref[0] = p_kva[tm - WINDOW:, 0:A_KV_W]
    vwin_ref[0] = p_kva[tm - WINDOW:, A_KV_W:]
    kvp_s[...] = p_kva[tm - WINDOW:, :]


def _cast_rows(rows, n_steps):
    rb = BF16_ROWS
    while rows % rb or rows // rb > n_steps:
        rb += BF16_ROWS
    return rb


def _front(x2d, nb, sinks, nw, wmt, wqgt, wn, bd, qcol, krow, bias_col, nwm_col, cast_ws, tm):
    n = x2d.shape[0]
    n_tiles = n // tm
    tps = n_tiles // nb
    n_blk = tm // LANES
    w_qvo = M_QK_W + 2 * M_V_W
    cur = lambda w: pl.BlockSpec((tm, w), lambda k: (jnp.minimum(k, n_tiles - 1), 0))
    prev = lambda w: pl.BlockSpec((tm, w), lambda k: (jnp.maximum(k - 1, 0), 0))
    per_seq = lambda r, w: pl.BlockSpec((1, r, w), lambda k: (jnp.maximum(k - 1, 0) // tps, 0, 0))

    def cast_spec(a):
        rb = _cast_rows(a.shape[0], n_tiles)
        return pl.BlockSpec((rb, a.shape[1]), lambda k: (jnp.minimum(k, a.shape[0] // rb - 1), 0))

    cast_in_specs = [cast_spec(a) for a in cast_ws]
    cast_out_specs = [cast_spec(a) for a in cast_ws]
    proj_scratch = [pltpu.VMEM((n_blk, w_qvo, LANES), BF16), pltpu.VMEM((n_blk, 2 * M_HEADS, LANES), F32),
                    pltpu.VMEM((n_blk, A_Q_W, LANES), BF16), pltpu.VMEM((tm, M_QK_W), BF16),
                    pltpu.VMEM((tm, 2 * A_KV_W), F32)]
    return pl.pallas_call(
        functools.partial(_front_kernel, tps, len(cast_ws)),
        grid=(n_tiles + 1,),
        in_specs=[pl.BlockSpec(memory_space=pltpu.SMEM), cur(D_MODEL)]
                 + [_const_spec(a.shape) for a in (nw, wmt, wqgt, wn, bd, qcol, krow, bias_col, nwm_col)]
                 + cast_in_specs,
        out_specs=[cur(2 * D_MODEL), prev(M_V_W), prev(A_Q_W), per_seq(WINDOW, A_KV_W), per_seq(WINDOW, A_KV_W),
                   per_seq(S_ROWS, M_QK_W), per_seq(SUBLANES, LANES)] + cast_out_specs,
        out_shape=[jax.ShapeDtypeStruct((n, 2 * D_MODEL), BF16),
                   jax.ShapeDtypeStruct((n, M_V_W), BF16),
                   jax.ShapeDtypeStruct((n, A_Q_W), BF16),
                   jax.ShapeDtypeStruct((nb, WINDOW, A_KV_W), F32),
                   jax.ShapeDtypeStruct((nb, WINDOW, A_KV_W), F32),
                   jax.ShapeDtypeStruct((nb, S_ROWS, M_QK_W), F32),
                   jax.ShapeDtypeStruct((nb, SUBLANES, LANES), F32)]
                  + [jax.ShapeDtypeStruct(a.shape, BF16) for a in cast_ws],
        scratch_shapes=proj_scratch + proj_scratch + [pltpu.VMEM((S_ROWS, M_QK_W), F32),
                                                       pltpu.VMEM((SUBLANES, LANES), F32),
                                                       pltpu.VMEM((WINDOW, 2 * A_KV_W), F32)],
        compiler_params=_params(("arbitrary",)),
        name="front",
    )(sinks, x2d, nw, wmt, wqgt, wn, bd, qcol, krow, bias_col, nwm_col, *cast_ws)


def _swa_sample_kernel(seq_len, sink_ref, qt_ref, kv_ref, ck_ref, cv_ref, h_ref, kwin_ref, vwin_ref):
    n_seq = GROUP // seq_len
    wb = ck_ref.shape[2]
    lane = lax.broadcasted_iota(jnp.int32, (1, LANES), 1)
    lo_half = lane < HALF
    q_rows = qt_ref[0].astype(F32).T
    kv_new = kv_ref[...]
    k_new = kv_new[:, 0:A_KV_W].reshape(n_seq, seq_len, A_KV_W)
    v_new = kv_new[:, A_KV_W:].reshape(n_seq, seq_len, A_KV_W)
    kt_new, vt_new = kv_new[:, 0:A_KV_W].T, kv_new[:, A_KV_W:].T
    for s in range(n_seq):
        put = (wb - seq_len - s * seq_len) % LANES
        for new_t, c_ref, win_ref in ((kt_new, ck_ref, kwin_ref), (vt_new, cv_ref, vwin_ref)):
            win_ref[s] = jnp.where(lane >= wb - seq_len, pltpu.roll(new_t, put, axis=1) if put else new_t,
                                   pltpu.roll(c_ref[s], wb - seq_len, axis=1))

    def to_kv_half(x, head):
        kvh = head // A_GROUPS
        if head % 2 != kvh:
            x = pltpu.roll(x, HALF, axis=1)
        return jnp.where(lo_half if kvh == 0 else ~lo_half, x, 0.0)

    lhs = jnp.concatenate(
        [to_kv_half(q_rows[:, (h // 2) * LANES:(h // 2 + 1) * LANES], h).reshape(n_seq, seq_len, LANES)
         for h in range(A_HEADS)], axis=1).astype(BF16)
    zpad = jnp.zeros((n_seq, BF16_ROWS - seq_len, A_KV_W), F32)
    k_nb = jnp.concatenate([k_new, zpad], axis=1).astype(BF16)
    v_nb = jnp.concatenate([v_new, zpad], axis=1).astype(BF16)
    s_c = jnp.einsum('sqf,sfk->sqk', lhs, ck_ref[...].astype(BF16), preferred_element_type=F32)
    s_n = jnp.einsum('sqf,skf->sqk', lhs, k_nb, preferred_element_type=F32)
    nrow = A_HEADS * seq_len
    ti = lax.broadcasted_iota(jnp.int32, (nrow, wb), 0) & (seq_len - 1)
    ki = lax.broadcasted_iota(jnp.int32, (nrow, wb), 1)
    mask_c = (ti + wb - ki) < WINDOW
    ti_n = lax.broadcasted_iota(jnp.int32, (nrow, BF16_ROWS), 0) & (seq_len - 1)
    ki_n = lax.broadcasted_iota(jnp.int32, (nrow, BF16_ROWS), 1)
    mask_n = ki_n <= ti_n
    row_head = lax.broadcasted_iota(jnp.int32, (nrow, 1), 0) >> (seq_len.bit_length() - 1)
    sk = jnp.zeros((nrow, 1), F32)
    for h in range(A_HEADS):
        sk = jnp.where(row_head == h, sink_ref[h], sk)
    s_c = jnp.where(mask_c, s_c, -jnp.inf)
    s_n = jnp.where(mask_n, s_n, -jnp.inf)
    mx = jnp.maximum(jnp.maximum(jnp.max(s_c, axis=-1, keepdims=True), jnp.max(s_n, axis=-1, keepdims=True)), sk)
    p_c = jnp.exp(s_c - mx)
    p_n = jnp.exp(s_n - mx)
    den = jnp.sum(p_c, axis=-1, keepdims=True) + jnp.sum(p_n, axis=-1, keepdims=True) + jnp.exp(sk - mx)
    o = (jnp.einsum('sqk,sfk->sqf', p_c.astype(BF16), cv_ref[...].astype(BF16), preferred_element_type=F32)
         + jnp.einsum('sqk,skf->sqf', p_n.astype(BF16), v_nb, preferred_element_type=F32)) / den

    def from_kv_half(head):
        x = o[:, head * seq_len:(head + 1) * seq_len, :].reshape(GROUP, LANES)
        return pltpu.roll(x, HALF, axis=1) if head % 2 != head // A_GROUPS else x

    for c in range(A_HEADS // 2):
        h_ref[:, c * LANES:(c + 1) * LANES] = jnp.where(lo_half, from_kv_half(2 * c),
                                                        from_kv_half(2 * c + 1)).astype(h_ref.dtype)


def _swa_sample(qat, kv_a, cache_k, cache_v, sinks, seq_len):
    ngrp = qat.shape[0]
    n_seq = GROUP // seq_len
    wb = cache_k.shape[2]
    assert wb == LANES, "window positions fill one lane-width"
    row = lambda w: pl.BlockSpec((GROUP, w), lambda i: (i, 0))
    cache = pl.BlockSpec((n_seq, A_KV_W, wb), lambda i: (i, 0, 0))
    return pl.pallas_call(
        functools.partial(_swa_sample_kernel, seq_len),
        grid=(ngrp,),
        in_specs=[pl.BlockSpec(memory_space=pltpu.SMEM), pl.BlockSpec((1, A_Q_W, LANES), lambda i: (i, 0, 0)),
                  row(2 * A_KV_W), cache, cache],
        out_specs=[row(A_Q_W), cache, cache],
        out_shape=[jax.ShapeDtypeStruct((ngrp * GROUP, A_Q_W), BF16),
                   jax.ShapeDtypeStruct(cache_k.shape, F32),
                   jax.ShapeDtypeStruct(cache_v.shape, F32)],
        compiler_params=_params(("arbitrary",)),
        name="swa_sample",
    )(sinks, qat, kv_a, cache_k, cache_v)


def _merge_ffn_kernel(x_ref, hm_ref, ha_ref, gab_ref, wa_ref, wb_ref, wo_ref, nw_ref, wg_ref, wu_ref, wd_ref,
                      y_ref):
    tm = x_ref.shape[0]
    halves = [slice(0, tm // 2), slice(tm // 2, tm)]
    mix = []
    for rs in halves:
        ga = jax.nn.sigmoid(gab_ref[rs, 0:D_MODEL].astype(F32))
        gb = jax.nn.sigmoid(gab_ref[rs, D_MODEL:].astype(F32))
        mix.append(ga * _dot(hm_ref[rs, :], wa_ref[...]) + gb * _dot(ha_ref[rs, :], wb_ref[...]))
    x1 = [x_ref[rs, :] + _dot(m.astype(BF16), wo_ref[...]) for rs, m in zip(halves, mix)]
    hf = [_rms_rows(v, nw_ref[...]).astype(BF16) for v in x1]
    act = [(jax.nn.silu(_dot(h, wg_ref[...])) * _dot(h, wu_ref[...])).astype(BF16) for h in hf]
    for rs, v, a in zip(halves, x1, act):
        y_ref[rs, :] = v + _dot(a, wd_ref[...])


def _merge_ffn(x2d, h_m, h_a, g_ab, wa, wb, wo, nw, wg, wu, wd, tm):
    n = x2d.shape[0]
    row = lambda w: pl.BlockSpec((tm, w), lambda i: (i, 0))
    return pl.pallas_call(
        _merge_ffn_kernel,
        grid=(n // tm,),
        in_specs=[row(D_MODEL), row(M_V_W), row(A_Q_W), row(2 * D_MODEL)]
                 + [_const_spec(w.shape) for w in (wa, wb, wo, nw, wg, wu, wd)],
        out_specs=row(D_MODEL),
        out_shape=jax.ShapeDtypeStruct((n, D_MODEL), F32),
        compiler_params=_params(("arbitrary",)),
        name="merge_ffn",
    )(x2d, h_m, h_a, g_ab, wa, wb, wo, nw, wg, wu, wd)


def kernel(x_prompt, x_sample, state_mlstm_C, state_mlstm_n, state_mlstm_m, cache_swa_k, cache_swa_v,
           norm_mix_w, w_in, mlstm_i_bias, mlstm_f_bias, mlstm_norm_w, q_norm_w, k_norm_w, attn_sinks,
           w_branch_a, w_branch_b, w_out, norm_ffn_w, w_gate, w_up, w_down):
    depth = w_in.shape[0]
    assert depth == 1, "single trunk layer"
    l = 0
    bp, tp = x_prompt.shape[0], x_prompt.shape[1]
    bs, ts = x_sample.shape[0], x_sample.shape[1]
    assert tp % 512 == 0 and (bs * ts) % GROUP == 0 and GROUP % ts == 0 and ts & (ts - 1) == 0
    assert ts <= SUBLANES, "sample chunk must fit one sublane tile"

    wt = jnp.transpose(w_in[l])
    c_km, c_vm = M_QK_W, 2 * M_QK_W
    c_g = 2 * M_QK_W + 2 * M_V_W
    c_qa = c_g + 2 * M_HEADS
    c_ka = c_qa + A_Q_W
    gate_pad = jnp.zeros((BF16_ROWS - 2 * M_HEADS, D_MODEL), F32)
    wmt = jnp.concatenate([wt[0:c_km], wt[c_vm:c_g]], axis=0).astype(BF16)
    wqgt = jnp.concatenate([wt[c_qa:c_ka], wt[c_g:c_qa], gate_pad], axis=0).astype(BF16)
    wn = jnp.concatenate([wt[c_km:c_vm], wt[c_ka:]], axis=0).astype(BF16)
    head_of = jnp.arange(A_KV_W) // A_HEAD_DIM
    bd = (head_of[:, None] == head_of[None, :]).astype(BF16)
    qcol = (jnp.tile(q_norm_w[l], A_HEADS) * (A_HEAD_DIM ** -0.5)).reshape(A_Q_W, 1)
    krow = jnp.tile(k_norm_w[l], A_KV_HEADS).reshape(1, A_KV_W)
    nw_mix = norm_mix_w[l].reshape(1, D_MODEL)
    nw_ffn = norm_ffn_w[l].reshape(1, D_MODEL)
    bias_col = jnp.concatenate([mlstm_i_bias[l], mlstm_f_bias[l]]).reshape(2 * M_HEADS, 1)
    nw_col = mlstm_norm_w[l].reshape(M_V_W, 1)
    sinks = attn_sinks[l]
    proj = lambda x2d, tm: _proj(x2d, nw_mix, wmt, wqgt, wn, bd, qcol, krow, tm)

    xp = x_prompt.reshape(bp * tp, D_MODEL)
    later_ws = (w_branch_a[l], w_branch_b[l], w_out[l], w_gate[l], w_up[l], w_down[l])
    g_ab, h_m, h_a, kwin_p, vwin_p, st_p, m_p, wa, wb, wo, wg, wu, wd = _front(
        xp, bp, sinks, nw_mix, wmt, wqgt, wn, bd, qcol, krow, bias_col, nw_col, later_ws, 512)
    merge = lambda x2d, h_m, h_a, g_ab, tm: _merge_ffn(x2d, h_m, h_a, g_ab, wa, wb, wo, nw_ffn, wg, wu, wd, tm)
    yp = merge(xp, h_m, h_a, g_ab, 512).reshape(bp, tp, D_MODEL)
    c_p = jnp.swapaxes(st_p[:, :M_DV, :], 1, 2).reshape(bp, M_HEADS, M_DK, M_DV)
    n_p = st_p[:, M_DV, :].reshape(bp, M_HEADS, M_DK)
    m_pr = m_p[:, :M_HEADS, 0]

    ns = bs * ts
    xs = x_sample.reshape(ns, D_MODEL)
    tms = 512 if ns % 512 == 0 else GROUP
    qvot, gt, qat, k_m, kv_a, g_ab = proj(xs, tms)
    ngrp = ns // GROUP
    m_lanes = jnp.repeat(state_mlstm_m[l], ts, axis=0).reshape(ngrp, GROUP, M_HEADS)
    mrow = jnp.pad(jnp.swapaxes(m_lanes, 1, 2), ((0, 0), (0, SUBLANES - M_HEADS), (0, 0)))
    h_m, c_s, n_s, mt_s = _mlstm_sample(qvot, gt, k_m, mrow, bias_col, nw_col,
                                        state_mlstm_C[l].reshape(bs, M_QK_W, M_DV),
                                        state_mlstm_n[l].reshape(bs, M_QK_W), ts)
    wbuf = cache_swa_k.shape[2]
    to_fm = lambda a: jnp.transpose(a, (0, 2, 3, 1)).reshape(bs, A_KV_W, wbuf)
    from_fm = lambda a: jnp.transpose(a.reshape(bs, A_KV_HEADS, A_HEAD_DIM, wbuf), (0, 3, 1, 2))[None]
    h_a, kwin_s, vwin_s = _swa_sample(qat, kv_a, to_fm(cache_swa_k[l]), to_fm(cache_swa_v[l]), sinks, ts)
    ys = merge(xs, h_m, h_a, g_ab, tms).reshape(bs, ts, D_MODEL)
    m_s = jnp.swapaxes(mt_s[:, :M_HEADS, :], 1, 2).reshape(bs, ts, M_HEADS)[:, ts - 1, :]

    kv5 = lambda a: a.reshape(a.shape[0], a.shape[1], A_KV_HEADS, A_HEAD_DIM)[None]
    return (yp, ys,
            c_p[None], n_p[None], m_pr[None], kv5(kwin_p), kv5(vwin_p),
            c_s.reshape(bs, M_HEADS, M_DK, M_DV)[None], n_s.reshape(bs, M_HEADS, M_DK)[None], m_s[None],
            from_fm(kwin_s), from_fm(vwin_s))
```

```python
import functools

import jax
import jax.numpy as jnp
from jax import lax
from jax.experimental import pallas as pl
from jax.experimental.pallas import tpu as pltpu

F32 = jnp.float32
BF16 = jnp.bfloat16

D_MODEL = 1024
M_HEADS = 4
M_DK = 64
M_DV = 128
M_CHUNK = 64
M_QK_W = M_HEADS * M_DK
M_V_W = M_HEADS * M_DV
A_HEADS = 8
A_KV_HEADS = 2
A_HEAD_DIM = 64
A_GROUPS = A_HEADS // A_KV_HEADS
A_Q_W = A_HEADS * A_HEAD_DIM
A_KV_W = A_KV_HEADS * A_HEAD_DIM
WINDOW = 128
D_FF = 2816
EPS = 1e-6

LANES = 128
SUBLANES = 8
BF16_ROWS = 16
GROUP = 128
S_ROWS = M_DV + BF16_ROWS
TOKEN_TILE = 512
VMEM_LIMIT = 56 * 1024 * 1024
DK_SHIFT = M_DK.bit_length() - 1
WINDOW_SHIFT = WINDOW.bit_length() - 1
HALF = LANES // 2
assert A_HEAD_DIM == HALF and A_KV_W == LANES, "attention head pairs share one lane-width"

NT_DIMS = (((1,), (1,)), ((), ()))


def _dot(a, b):
    return jnp.dot(a, b, preferred_element_type=F32)


def _dot_nt(a, b):
    return lax.dot_general(a, b, NT_DIMS, preferred_element_type=F32)


def _const_spec(shape):
    nd = len(shape)
    return pl.BlockSpec(shape, lambda *_: (0,) * nd, pipeline_mode=pl.Buffered(1))


def _params(sem):
    return pltpu.CompilerParams(dimension_semantics=sem, vmem_limit_bytes=VMEM_LIMIT)


def _rms_rows(x, nw):
    ms = jnp.mean(x * x, axis=-1, keepdims=True)
    return (x * lax.rsqrt(ms + EPS)) * nw


GAB_CHUNK = 512


def _proj_chunks(x_ref, nw_ref, wmt_ref, wqgt_ref, wn_ref, bd_ref, qcol_ref, krow_ref,
                 qvot_ref, gt_ref, qat_ref, km_ref, kva_ref, gab_ref):
    hn = _rms_rows(x_ref[...], nw_ref[...]).astype(BF16)
    n_blk = qat_ref.shape[0]

    def put(ref, rows, val):
        for c in range(n_blk):
            ref[c, rows, :] = val[:, c * LANES:(c + 1) * LANES].astype(ref.dtype)

    def mlstm_qvo():
        put(qvot_ref, slice(None), _dot_nt(wmt_ref[...], hn))

    def attn_q_and_gates():
        qt = _dot_nt(wqgt_ref[...], hn)
        put(gt_ref, slice(None), qt[A_Q_W:A_Q_W + 2 * M_HEADS])
        for h in range(A_HEADS):
            hs = slice(h * A_HEAD_DIM, (h + 1) * A_HEAD_DIM)
            blk = qt[hs]
            ssq_q = jnp.sum(blk * blk, axis=0, keepdims=True)
            put(qat_ref, hs, (blk * lax.rsqrt(ssq_q * (1.0 / A_HEAD_DIM) + EPS)) * qcol_ref[hs])

    def mlstm_k():
        km_ref[...] = _dot_nt(hn, wn_ref[0:M_QK_W, :]).astype(km_ref.dtype)

    def branch_gates(c0):
        def run():
            w0 = M_QK_W + 2 * A_KV_W + c0
            gab_ref[:, c0:c0 + GAB_CHUNK] = _dot_nt(hn, wn_ref[w0:w0 + GAB_CHUNK, :]).astype(gab_ref.dtype)
        return run

    def attn_kv():
        kv = _dot_nt(hn, wn_ref[M_QK_W:M_QK_W + 2 * A_KV_W, :])
        k = kv[:, 0:A_KV_W]
        ksq = k * k
        hi = ksq.astype(BF16)
        lo = (ksq - hi.astype(F32)).astype(BF16)
        ssq = _dot(hi, bd_ref[...]) + _dot(lo, bd_ref[...])
        kva_ref[:, 0:A_KV_W] = (k * lax.rsqrt(ssq * (1.0 / A_HEAD_DIM) + EPS)) * krow_ref[...]
        kva_ref[:, A_KV_W:] = kv[:, A_KV_W:]

    return ([mlstm_qvo, attn_q_and_gates, mlstm_k]
            + [branch_gates(c0) for c0 in range(0, gab_ref.shape[1], GAB_CHUNK)] + [attn_kv])


def _proj_kernel(*refs):
    for piece in _proj_chunks(*refs):
        piece()


def _proj(x2d, nw, wmt, wqgt, wn, bd, qcol, krow, tm):
    n = x2d.shape[0]
    row = lambda w: pl.BlockSpec((tm, w), lambda i: (i, 0))
    n_blk = tm // LANES
    slab = lambda r: pl.BlockSpec((n_blk, r, LANES), lambda i: (i, 0, 0))
    w_qvo = M_QK_W + 2 * M_V_W
    return pl.pallas_call(
        _proj_kernel,
        grid=(n // tm,),
        in_specs=[row(D_MODEL)] + [_const_spec(a.shape) for a in (nw, wmt, wqgt, wn, bd, qcol, krow)],
        out_specs=[slab(w_qvo), slab(2 * M_HEADS), slab(A_Q_W), row(M_QK_W), row(2 * A_KV_W), row(2 * D_MODEL)],
        out_shape=[jax.ShapeDtypeStruct((n // LANES, w_qvo, LANES), BF16),
                   jax.ShapeDtypeStruct((n // LANES, 2 * M_HEADS, LANES), F32),
                   jax.ShapeDtypeStruct((n // LANES, A_Q_W, LANES), BF16),
                   jax.ShapeDtypeStruct((n, M_QK_W), BF16),
                   jax.ShapeDtypeStruct((n, 2 * A_KV_W), F32),
                   jax.ShapeDtypeStruct((n, 2 * D_MODEL), BF16)],
        compiler_params=_params(("arbitrary",)),
        name="proj",
    )(x2d, nw, wmt, wqgt, wn, bd, qcol, krow)


def _split3_rows(x):
    hi = x.astype(BF16).astype(F32)
    r1 = x - hi
    mid = r1.astype(BF16).astype(F32)
    lo = r1 - mid
    return jnp.concatenate([hi, mid, lo], axis=0).astype(BF16)


def _log_sigmoid(x):
    return jnp.minimum(x, 0.0) - jnp.log1p(jnp.exp(-jnp.abs(x)))


def _chunk_masks(chunk_shift):
    s = lax.broadcasted_iota(jnp.int32, (GROUP, GROUP), 0)
    t = lax.broadcasted_iota(jnp.int32, (GROUP, GROUP), 1)
    same = (s >> chunk_shift) == (t >> chunk_shift)
    return same, same & (s <= t)


def _group_gates(gt, same, causal):
    cm_bf = jnp.where(causal, 1.0, 0.0).astype(BF16)
    lf = _log_sigmoid(gt)
    nr = gt.shape[0]
    bt3 = _dot(_split3_rows(lf), cm_bf)
    bt = (bt3[0:nr] + bt3[nr:2 * nr]) + bt3[2 * nr:3 * nr]
    b4 = bt[M_HEADS:2 * M_HEADS]
    a4 = gt[0:M_HEADS] - b4
    a8 = jnp.concatenate([a4, a4], axis=0)
    a_cols = jnp.concatenate([a8, jnp.zeros((GROUP - SUBLANES, GROUP), F32)], axis=0).T
    at_mats, run_rows, chunk_rows = [], [], []
    for h in range(M_HEADS):
        at = jnp.broadcast_to(a_cols[:, h:h + 1], (GROUP, GROUP))
        at_mats.append(at)
        run_rows.append(jnp.max(jnp.where(causal, at, -jnp.inf), axis=0, keepdims=True))
        chunk_rows.append(jnp.max(jnp.where(same, at, -jnp.inf), axis=0, keepdims=True))
    return b4, a4, at_mats, jnp.concatenate(run_rows, axis=0), jnp.concatenate(chunk_rows, axis=0)


def _group_weights(m_prev, b4, a4, run4, chunk4):
    big_m = jnp.maximum(m_prev, run4)
    m_last = jnp.maximum(m_prev, chunk4)
    w_inter = jnp.exp(m_prev - big_m)
    g_vec = jnp.exp(m_prev - m_last)
    w_last = jnp.exp(a4 - m_last)
    m_t = b4 + big_m
    return big_m, w_inter, g_vec, w_last, m_t, jnp.exp(-m_t)


def _group_scores(qvot, ks):
    lane_head = lax.broadcasted_iota(jnp.int32, (1, M_QK_W), 1) >> DK_SHIFT
    row_head = lax.broadcasted_iota(jnp.int32, (M_QK_W, 1), 0) >> DK_SHIFT
    qt = qvot(0, M_QK_W)
    k_stack = jnp.concatenate([jnp.where(lane_head == h, ks, jnp.zeros_like(ks)) for h in range(M_HEADS)], axis=0)
    qw = jnp.concatenate([jnp.where(row_head == h, qt, jnp.zeros_like(qt)) for h in range(M_HEADS)], axis=1)
    zero_blk = jnp.zeros((M_DK, GROUP), BF16)
    sc_t = []
    for h in range(0, M_HEADS, 2):
        q_pair = jnp.concatenate(
            [jnp.concatenate([qvot(h * M_DK, (h + 1) * M_DK), zero_blk], axis=1),
             jnp.concatenate([zero_blk, qvot((h + 1) * M_DK, (h + 2) * M_DK)], axis=1)], axis=0)
        sc = _dot(ks[:, h * M_DK:(h + 2) * M_DK], q_pair)
        sc_t += [sc[:, 0:GROUP], sc[:, GROUP:]]
    return k_stack, qw, sc_t


def _group_values(qvot, sc_t, at_mats, big_m, causal):
    ones_rows = jnp.where(lax.broadcasted_iota(jnp.int32, (BF16_ROWS, GROUP), 0) == 0, 1.0, 0.0).astype(BF16)
    zero_blk = jnp.zeros((GROUP, GROUP), BF16)
    vta, s_t = [], []
    for h in range(M_HEADS):
        w_t = jnp.where(causal, jnp.exp(at_mats[h] - big_m[h:h + 1]), 0.0)
        s_t.append((sc_t[h] * w_t).astype(BF16))
        vta.append(jnp.concatenate([qvot(M_QK_W + h * M_DV, M_QK_W + (h + 1) * M_DV), ones_rows], axis=0))
    intra = []
    for h in range(0, M_HEADS, 2):
        pair = jnp.concatenate([jnp.concatenate([s_t[h], zero_blk], axis=1),
                                jnp.concatenate([zero_blk, s_t[h + 1]], axis=1)], axis=0)
        intra.append(_dot(jnp.concatenate(vta[h:h + 2], axis=1), pair))
    return vta, jnp.concatenate(intra, axis=1)


def _weighted_values(vta, w_rows):
    return jnp.concatenate([(vta[h].astype(F32) * w_rows[h:h + 1]).astype(BF16) for h in range(M_HEADS)], axis=1)


def _lanes_x(rows):
    return jnp.concatenate([rows[h:h + 1] for h in range(M_HEADS)], axis=1)


def _group_out(qvot, inter, intra, w_inter, e_neg_m, nw_ref):
    outs = []
    out_all = inter * _lanes_x(w_inter) + intra
    for h in range(M_HEADS):
        out_t = out_all[:, h * GROUP:(h + 1) * GROUP]
        hh = out_t[0:M_DV] / jnp.maximum(jnp.abs(out_t[M_DV:M_DV + 1]), e_neg_m[h:h + 1])
        ms = jnp.mean(hh * hh, axis=0, keepdims=True)
        hn = (hh * lax.rsqrt(ms + EPS)) * nw_ref[h * M_DV:(h + 1) * M_DV]
        o_t = qvot(M_QK_W + M_V_W + h * M_DV, M_QK_W + M_V_W + (h + 1) * M_DV)
        outs.append(hn * jax.nn.sigmoid(o_t.astype(F32)))
    return jnp.concatenate(outs, axis=0).T


def _decay_row(g_vec, lane0):
    lane_head = lax.broadcasted_iota(jnp.int32, (1, M_QK_W), 1) >> DK_SHIFT
    g_row = jnp.zeros((1, M_QK_W), F32)
    for h in range(M_HEADS):
        g_row = jnp.where(lane_head == h, g_vec[h:h + 1, lane0:lane0 + 1], g_row)
    return g_row


def _mlstm_sample_kernel(seq_len, qvot_ref, gt_ref, k_ref, mrow_ref, bias_ref, nw_ref, c_ref, n_ref,
                         h_ref, c_out_ref, n_out_ref, mt_ref):
    n_seq = GROUP // seq_len
    shift = seq_len.bit_length() - 1
    same, causal = _chunk_masks(shift)
    lane_seq = lax.broadcasted_iota(jnp.int32, (1, GROUP), 1) >> shift
    lane_seq_x = jnp.concatenate([lane_seq] * M_HEADS, axis=1)
    row0 = lax.broadcasted_iota(jnp.int32, (BF16_ROWS, M_QK_W), 0) == 0
    qvot = lambda r0, r1: qvot_ref[0, r0:r1, :]
    ks = k_ref[...] * (M_DK ** -0.5)
    b4, a4, at_mats, run4, chunk4 = _group_gates(gt_ref[0] + bias_ref[...], same, causal)
    big_m, w_inter, g_vec, w_last, m_t, e_neg_m = _group_weights(mrow_ref[0, 0:M_HEADS, :], b4, a4, run4, chunk4)
    mt_ref[0] = jnp.concatenate([m_t, m_t], axis=0)
    k_stack, qw, sc_t = _group_scores(qvot, ks)

    st_old = []
    for s in range(n_seq):
        n_rows = jnp.where(row0, jnp.broadcast_to(n_ref[s:s + 1, :], (BF16_ROWS, M_QK_W)), 0.0)
        st_old.append(jnp.concatenate([c_ref[s].T, n_rows], axis=0))
    inter_all = _dot(jnp.concatenate(st_old, axis=0).astype(BF16), qw)
    vta, intra = _group_values(qvot, sc_t, at_mats, big_m, causal)
    inter = inter_all[0:S_ROWS]
    for s in range(1, n_seq):
        inter = jnp.where(lane_seq_x == s, inter_all[s * S_ROWS:(s + 1) * S_ROWS], inter)
    tall = jnp.concatenate([_weighted_values(vta, jnp.where(lane_seq == s, w_last, 0.0)) for s in range(n_seq)],
                           axis=0)
    d_st = _dot(tall, k_stack)
    for s in range(n_seq):
        st_new = _decay_row(g_vec, s * seq_len) * st_old[s] + d_st[s * S_ROWS:(s + 1) * S_ROWS]
        c_out_ref[s] = st_new[0:M_DV].T
        n_out_ref[s:s + 1, :] = st_new[M_DV:M_DV + 1]
    h_ref[...] = _group_out(qvot, inter, intra, w_inter, e_neg_m, nw_ref).astype(h_ref.dtype)


def _mlstm_sample(qvot, gt, k_m, mrow, bias_col, nw_col, c, n, seq_len):
    ngrp = qvot.shape[0]
    n_seq = GROUP // seq_len
    full = lambda a: pl.BlockSpec(a.shape, lambda i: (0,) * a.ndim)
    slab = lambda a: pl.BlockSpec((1,) + a.shape[1:], lambda i: (i, 0, 0))
    row = lambda w: pl.BlockSpec((GROUP, w), lambda i: (i, 0))
    c_spec = pl.BlockSpec((n_seq,) + c.shape[1:], lambda i: (i, 0, 0))
    n_spec = pl.BlockSpec((n_seq, n.shape[1]), lambda i: (i, 0))
    return pl.pallas_call(
        functools.partial(_mlstm_sample_kernel, seq_len),
        grid=(ngrp,),
        in_specs=[slab(qvot), slab(gt), row(M_QK_W), slab(mrow), full(bias_col), full(nw_col), c_spec, n_spec],
        out_specs=[row(M_V_W), c_spec, n_spec, slab(mrow)],
        out_shape=[jax.ShapeDtypeStruct((ngrp * GROUP, M_V_W), BF16),
                   jax.ShapeDtypeStruct(c.shape, F32),
                   jax.ShapeDtypeStruct(n.shape, F32),
                   jax.ShapeDtypeStruct(mrow.shape, F32)],
        compiler_params=_params(("arbitrary",)),
        name="mlstm_sample",
    )(qvot, gt, k_m, mrow, bias_col, nw_col, c, n)


def _swa_scores(qbs, has_prev, sink_ref, qt_ref, kv_ref, kvp_ref):
    hd = A_HEAD_DIM
    nq = A_GROUPS * WINDOW
    si = lax.broadcasted_iota(jnp.int32, (2 * WINDOW, nq), 0)
    qi = lax.broadcasted_iota(jnp.int32, (2 * WINDOW, nq), 1) & (WINDOW - 1)
    local = ((si < WINDOW) & (si > qi)) | ((si >= WINDOW) & (si - WINDOW <= qi))
    first = local & (has_prev | (si >= WINDOW))
    lane_grp = lax.broadcasted_iota(jnp.int32, (1, nq), 1) >> WINDOW_SHIFT
    kv_block = lambda i: kvp_ref[...] if i == 0 else kv_ref[(i - 1) * WINDOW:i * WINDOW, :]
    k_bf = lambda i: kv_block(i)[:, 0:A_KV_W].astype(BF16)
    vt_bf = lambda i: kv_block(i)[:, A_KV_W:].T.astype(BF16)
    zeros = jnp.zeros((hd, nq), BF16)
    sinks = []
    for kvh in range(A_KV_HEADS):
        sk = jnp.zeros((1, nq), F32)
        for g in range(A_GROUPS):
            sk = jnp.where(lane_grp == g, sink_ref[kvh * A_GROUPS + g], sk)
        sinks.append(sk)
    units = []
    for qb in qbs:
        kk = jnp.concatenate([k_bf(qb), k_bf(qb + 1)], axis=0)
        vt = jnp.concatenate([vt_bf(qb), vt_bf(qb + 1)], axis=1)
        mask = local if qb > 0 else first
        for kvh in range(A_KV_HEADS):
            q4t = jnp.concatenate(
                [qt_ref[qb, (kvh * A_GROUPS + g) * hd:(kvh * A_GROUPS + g + 1) * hd, :] for g in range(A_GROUPS)],
                axis=1)
            wq = jnp.concatenate([q4t, zeros] if kvh == 0 else [zeros, q4t], axis=0)
            units.append((jnp.where(mask, _dot(kk, wq), -jnp.inf), sinks[kvh], vt))
    return units


def _swa_finish(qbs, units, h_ref):
    hd = A_HEAD_DIM
    for i, qb in enumerate(qbs):
        pieces = []
        for kvh in range(A_KV_HEADS):
            s, sk, vt = units[i * A_KV_HEADS + kvh]
            mx = jnp.maximum(jnp.max(s, axis=0, keepdims=True), sk)
            p = jnp.exp(s - mx)
            den = jnp.sum(p, axis=0, keepdims=True) + jnp.exp(sk - mx)
            ot = _dot(vt[kvh * hd:(kvh + 1) * hd], p.astype(BF16)) / den
            pieces += [ot[:, g * WINDOW:(g + 1) * WINDOW] for g in range(A_GROUPS)]
        h_t = jnp.concatenate(pieces, axis=0)
        h_ref[qb * WINDOW:(qb + 1) * WINDOW, :] = h_t.T.astype(h_ref.dtype)


def _front_kernel(tiles_per_seq, n_cast, *refs):
    (sink_ref, x_ref, nw_ref, wmt_ref, wqgt_ref, wn_ref, bd_ref, qcol_ref, krow_ref, bias_ref, nwm_ref) = refs[:11]
    cast_in = refs[11:11 + n_cast]
    (gab_ref, hm_ref, ha_ref, kwin_ref, vwin_ref, st_ref, m_ref) = refs[11 + n_cast:18 + n_cast]
    cast_out = refs[18 + n_cast:18 + 2 * n_cast]
    set_a, set_b = refs[18 + 2 * n_cast:23 + 2 * n_cast], refs[23 + 2 * n_cast:28 + 2 * n_cast]
    st_s, m_s, kvp_s = refs[28 + 2 * n_cast:]
    odd = (pl.program_id(0) & 1) == 1
    for parity, q_set, p_set in ((jnp.logical_not(odd), set_a, set_b), (odd, set_b, set_a)):
        pl.when(parity)(functools.partial(
            _front_body, tiles_per_seq, sink_ref, x_ref, nw_ref, wmt_ref, wqgt_ref, wn_ref, bd_ref, qcol_ref,
            krow_ref, bias_ref, nwm_ref, gab_ref, hm_ref, ha_ref, kwin_ref, vwin_ref, st_ref, m_ref,
            *q_set, *p_set, st_s, m_s, kvp_s))
    for src, dst in zip(cast_in, cast_out):
        dst[...] = src[...].astype(dst.dtype)


def _front_body(tiles_per_seq, sink_ref, x_ref, nw_ref, wmt_ref, wqgt_ref, wn_ref, bd_ref, qcol_ref, krow_ref,
                bias_ref, nwm_ref,
                gab_ref, hm_ref, ha_ref, kwin_ref, vwin_ref, st_ref, m_ref,
                q_qvot, q_gt, q_qat, q_km, q_kva, p_qvot, p_gt, p_qat, p_km, p_kva, st_s, m_s, kvp_s):
    k = pl.program_id(0)

    @pl.when(k == 0)
    def _():
        for r in (p_qvot, p_gt, p_qat, p_km, p_kva, st_s, m_s, kvp_s):
            r[...] = jnp.zeros(r.shape, r.dtype)

    pieces = _proj_chunks(x_ref, nw_ref, wmt_ref, wqgt_ref, wn_ref, bd_ref, qcol_ref, krow_ref,
                          q_qvot, q_gt, q_qat, q_km, q_kva, gab_ref)

    seq_start = lax.rem(k - 1 + tiles_per_seq, tiles_per_seq) == 0
    n_blk = p_qat.shape[0]
    same, causal = _chunk_masks(GROUP.bit_length() - 1)
    st = jnp.where(seq_start, 0.0, st_s[...])
    m_col = jnp.where(seq_start, 0.0, m_s[0:M_HEADS, 0:1])

    gates = [_group_gates(p_gt[g] + bias_ref[...], same, causal) for g in range(n_blk)]
    for piece in pieces[0:1]:
        piece()
    grp = []
    for g in range(n_blk):
        qvot = lambda r0, r1, g=g: p_qvot[g, r0:r1, :]
        ks = p_km[g * GROUP:(g + 1) * GROUP, :] * (M_DK ** -0.5)
        b4, a4, at_mats, run4, chunk4 = gates[g]
        weights = _group_weights(jnp.broadcast_to(m_col, (M_HEADS, GROUP)), b4, a4, run4, chunk4)
        m_col = weights[4][:, GROUP - 1:GROUP]
        grp.append((qvot, at_mats, weights) + _group_scores(qvot, ks))
    has_prev = jnp.logical_not(seq_start)
    qb_lo, qb_hi = list(range(0, n_blk // 2)), list(range(n_blk // 2, n_blk))
    units_lo = _swa_scores(qb_lo, has_prev, sink_ref, p_qat, p_kva, kvp_s)
    inter = [_dot(st.astype(BF16), grp[0][4])]
    for piece in pieces[1:3]:
        piece()

    vals = []

    def group_values(g):
        qvot, at_mats, (big_m, _, _, w_last, _, _), k_stack, _, sc_t = grp[g]
        vta, intra = _group_values(qvot, sc_t, at_mats, big_m, causal)
        vals.append((intra, _dot(_weighted_values(vta, w_last), k_stack)))

    for g in range(0, n_blk // 2):
        group_values(g)
    _swa_finish(qb_lo[:1], units_lo[:A_KV_HEADS], ha_ref)
    for piece in pieces[3:4]:
        piece()
    for g in range(n_blk // 2, n_blk):
        group_values(g)
    _swa_finish(qb_lo[1:], units_lo[A_KV_HEADS:], ha_ref)
    units_hi = _swa_scores(qb_hi, has_prev, sink_ref, p_qat, p_kva, kvp_s)
    for piece in pieces[4:6]:
        piece()

    for g in range(n_blk):
        st = _decay_row(grp[g][2][2], 0) * st + vals[g][1]
        if g + 1 < n_blk:
            inter.append(_dot(st.astype(BF16), grp[g + 1][4]))
        if g == n_blk // 2 - 1:
            _swa_finish(qb_hi[:1], units_hi[:A_KV_HEADS], ha_ref)
            for piece in pieces[6:7]:
                piece()
    _swa_finish(qb_hi[1:], units_hi[A_KV_HEADS:], ha_ref)
    for piece in pieces[7:]:
        piece()

    for g in range(n_blk):
        qvot, _, (_, w_inter, _, _, _, e_neg_m) = grp[g][0:3]
        hm_ref[g * GROUP:(g + 1) * GROUP, :] = _group_out(qvot, inter[g], vals[g][0], w_inter, e_neg_m,
                                                          nwm_ref).astype(hm_ref.dtype)
    st_s[...] = st
    st_ref[0] = st
    m_rows = jnp.broadcast_to(m_col, (M_HEADS, LANES))
    m_rows = jnp.concatenate([m_rows, m_rows], axis=0)
    m_s[...] = m_rows
    m_ref[0] = m_rows
    tm = p_kva.shape[0]
    kwin_ref[0] = p_kva[tm - WINDOW:, 0:A_KV_W]
    vwin_ref[0] = p_kva[tm - WINDOW:, A_KV_W:]
    kvp_s[...] = p_kva[tm - WINDOW:, :]


def _cast_rows(rows, n_steps):
    rb = BF16_ROWS
    while rows % rb or rows // rb > n_steps:
        rb += BF16_ROWS
    return rb


def _front(x2d, nb, sinks, nw, wmt, wqgt, wn, bd, qcol, krow, bias_col, nwm_col, cast_ws, tm):
    n = x2d.shape[0]
    n_tiles = n // tm
    tps = n_tiles // nb
    n_blk = tm // LANES
    w_qvo = M_QK_W + 2 * M_V_W
    cur = lambda w: pl.BlockSpec((tm, w), lambda k: (jnp.minimum(k, n_tiles - 1), 0))
    prev = lambda w: pl.BlockSpec((tm, w), lambda k: (jnp.maximum(k - 1, 0), 0))
    per_seq = lambda r, w: pl.BlockSpec((1, r, w), lambda k: (jnp.maximum(k - 1, 0) // tps, 0, 0))

    def cast_spec(a):
        rb = _cast_rows(a.shape[0], n_tiles)
        return pl.BlockSpec((rb, a.shape[1]), lambda k: (jnp.minimum(k, a.shape[0] // rb - 1), 0))

    cast_in_specs = [cast_spec(a) for a in cast_ws]
    cast_out_specs = [cast_spec(a) for a in cast_ws]
    proj_scratch = [pltpu.VMEM((n_blk, w_qvo, LANES), BF16), pltpu.VMEM((n_blk, 2 * M_HEADS, LANES), F32),
                    pltpu.VMEM((n_blk, A_Q_W, LANES), BF16), pltpu.VMEM((tm, M_QK_W), BF16),
                    pltpu.VMEM((tm, 2 * A_KV_W), F32)]
    return pl.pallas_call(
        functools.partial(_front_kernel, tps, len(cast_ws)),
        grid=(n_tiles + 1,),
        in_specs=[pl.BlockSpec(memory_space=pltpu.SMEM), cur(D_MODEL)]
                 + [_const_spec(a.shape) for a in (nw, wmt, wqgt, wn, bd, qcol, krow, bias_col, nwm_col)]
                 + cast_in_specs,
        out_specs=[cur(2 * D_MODEL), prev(M_V_W), prev(A_Q_W), per_seq(WINDOW, A_KV_W), per_seq(WINDOW, A_KV_W),
                   per_seq(S_ROWS, M_QK_W), per_seq(SUBLANES, LANES)] + cast_out_specs,
        out_shape=[jax.ShapeDtypeStruct((n, 2 * D_MODEL), BF16),
                   jax.ShapeDtypeStruct((n, M_V_W), BF16),
                   jax.ShapeDtypeStruct((n, A_Q_W), BF16),
                   jax.ShapeDtypeStruct((nb, WINDOW, A_KV_W), F32),
                   jax.ShapeDtypeStruct((nb, WINDOW, A_KV_W), F32),
                   jax.ShapeDtypeStruct((nb, S_ROWS, M_QK_W), F32),
                   jax.ShapeDtypeStruct((nb, SUBLANES, LANES), F32)]
                  + [jax.ShapeDtypeStruct(a.shape, BF16) for a in cast_ws],
        scratch_shapes=proj_scratch + proj_scratch + [pltpu.VMEM((S_ROWS, M_QK_W), F32),
                                                       pltpu.VMEM((SUBLANES, LANES), F32),
                                                       pltpu.VMEM((WINDOW, 2 * A_KV_W), F32)],
        compiler_params=_params(("arbitrary",)),
        name="front",
    )(sinks, x2d, nw, wmt, wqgt, wn, bd, qcol, krow, bias_col, nwm_col, *cast_ws)


def _swa_sample_kernel(seq_len, sink_ref, qt_ref, kv_ref, ck_ref, cv_ref, h_ref, kwin_ref, vwin_ref):
    n_seq = GROUP // seq_len
    wb = ck_ref.shape[2]
    lane = lax.broadcasted_iota(jnp.int32, (1, LANES), 1)
    lo_half = lane < HALF
    q_rows = qt_ref[0].astype(F32).T
    kv_new = kv_ref[...]
    k_new = kv_new[:, 0:A_KV_W].reshape(n_seq, seq_len, A_KV_W)
    v_new = kv_new[:, A_KV_W:].reshape(n_seq, seq_len, A_KV_W)

    def to_kv_half(x, head):
        kvh = head // A_GROUPS
        if head % 2 != kvh:
            x = pltpu.roll(x, HALF, axis=1)
        return jnp.where(lo_half if kvh == 0 else ~lo_half, x, 0.0)

    lhs = jnp.concatenate(
        [to_kv_half(q_rows[:, (h // 2) * LANES:(h // 2 + 1) * LANES], h).reshape(n_seq, seq_len, LANES)
         for h in range(A_HEADS)], axis=1).astype(BF16)
    zpad = jnp.zeros((n_seq, BF16_ROWS - seq_len, A_KV_W), F32)
    k_nb = jnp.concatenate([k_new, zpad], axis=1).astype(BF16)
    v_nb = jnp.concatenate([v_new, zpad], axis=1).astype(BF16)
    s_c = jnp.einsum('sqf,sfk->sqk', lhs, ck_ref[...].astype(BF16), preferred_element_type=F32)
    s_n = jnp.einsum('sqf,skf->sqk', lhs, k_nb, preferred_element_type=F32)
    nrow = A_HEADS * seq_len
    ti = lax.broadcasted_iota(jnp.int32, (nrow, wb), 0) & (seq_len - 1)
    ki = lax.broadcasted_iota(jnp.int32, (nrow, wb), 1)
    mask_c = (ti + wb - ki) < WINDOW
    ti_n = lax.broadcasted_iota(jnp.int32, (nrow, BF16_ROWS), 0) & (seq_len - 1)
    ki_n = lax.broadcasted_iota(jnp.int32, (nrow, BF16_ROWS), 1)
    mask_n = ki_n <= ti_n
    row_head = lax.broadcasted_iota(jnp.int32, (nrow, 1), 0) >> (seq_len.bit_length() - 1)
    sk = jnp.zeros((nrow, 1), F32)
    for h in range(A_HEADS):
        sk = jnp.where(row_head == h, sink_ref[h], sk)
    s_c = jnp.where(mask_c, s_c, -jnp.inf)
    s_n = jnp.where(mask_n, s_n, -jnp.inf)
    mx = jnp.maximum(jnp.maximum(jnp.max(s_c, axis=-1, keepdims=True), jnp.max(s_n, axis=-1, keepdims=True)), sk)
    p_c = jnp.exp(s_c - mx)
    p_n = jnp.exp(s_n - mx)
    den = jnp.sum(p_c, axis=-1, keepdims=True) + jnp.sum(p_n, axis=-1, keepdims=True) + jnp.exp(sk - mx)
    o = (jnp.einsum('sqk,sfk->sqf', p_c.astype(BF16), cv_ref[...].astype(BF16), preferred_element_type=F32)
         + jnp.einsum('sqk,skf->sqf', p_n.astype(BF16), v_nb, preferred_element_type=F32)) / den

    def from_kv_half(head):
        x = o[:, head * seq_len:(head + 1) * seq_len, :].reshape(GROUP, LANES)
        return pltpu.roll(x, HALF, axis=1) if head % 2 != head // A_GROUPS else x

    for c in range(A_HEADS // 2):
        h_ref[:, c * LANES:(c + 1) * LANES] = jnp.where(lo_half, from_kv_half(2 * c),
                                                        from_kv_half(2 * c + 1)).astype(h_ref.dtype)

    kt_new, vt_new = kv_new[:, 0:A_KV_W].T, kv_new[:, A_KV_W:].T
    for s in range(n_seq):
        put = (wb - seq_len - s * seq_len) % LANES
        for new_t, c_ref, win_ref in ((kt_new, ck_ref, kwin_ref), (vt_new, cv_ref, vwin_ref)):
            win_ref[s] = jnp.where(lane >= wb - seq_len, pltpu.roll(new_t, put, axis=1) if put else new_t,
                                   pltpu.roll(c_ref[s], wb - seq_len, axis=1))


def _swa_sample(qat, kv_a, cache_k, cache_v, sinks, seq_len):
    ngrp = qat.shape[0]
    n_seq = GROUP // seq_len
    wb = cache_k.shape[2]
    assert wb == LANES, "window positions fill one lane-width"
    row = lambda w: pl.BlockSpec((GROUP, w), lambda i: (i, 0))
    cache = pl.BlockSpec((n_seq, A_KV_W, wb), lambda i: (i, 0, 0))
    return pl.pallas_call(
        functools.partial(_swa_sample_kernel, seq_len),
        grid=(ngrp,),
        in_specs=[pl.BlockSpec(memory_space=pltpu.SMEM), pl.BlockSpec((1, A_Q_W, LANES), lambda i: (i, 0, 0)),
                  row(2 * A_KV_W), cache, cache],
        out_specs=[row(A_Q_W), cache, cache],
        out_shape=[jax.ShapeDtypeStruct((ngrp * GROUP, A_Q_W), BF16),
                   jax.ShapeDtypeStruct(cache_k.shape, F32),
                   jax.ShapeDtypeStruct(cache_v.shape, F32)],
        compiler_params=_params(("arbitrary",)),
        name="swa_sample",
    )(sinks, qat, kv_a, cache_k, cache_v)


def _merge_ffn_kernel(x_ref, hm_ref, ha_ref, gab_ref, wa_ref, wb_ref, wo_ref, nw_ref, wg_ref, wu_ref, wd_ref,
                      y_ref):
    tm = x_ref.shape[0]
    halves = [slice(0, tm // 2), slice(tm // 2, tm)]
    mix = []
    for rs in halves:
        ga = jax.nn.sigmoid(gab_ref[rs, 0:D_MODEL].astype(F32))
        gb = jax.nn.sigmoid(gab_ref[rs, D_MODEL:].astype(F32))
        mix.append(ga * _dot(hm_ref[rs, :], wa_ref[...]) + gb * _dot(ha_ref[rs, :], wb_ref[...]))
    x1 = [x_ref[rs, :] + _dot(m.astype(BF16), wo_ref[...]) for rs, m in zip(halves, mix)]
    hf = [_rms_rows(v, nw_ref[...]).astype(BF16) for v in x1]
    act = [(jax.nn.silu(_dot(h, wg_ref[...])) * _dot(h, wu_ref[...])).astype(BF16) for h in hf]
    for rs, v, a in zip(halves, x1, act):
        y_ref[rs, :] = v + _dot(a, wd_ref[...])


def _merge_ffn(x2d, h_m, h_a, g_ab, wa, wb, wo, nw, wg, wu, wd, tm):
    n = x2d.shape[0]
    row = lambda w: pl.BlockSpec((tm, w), lambda i: (i, 0))
    return pl.pallas_call(
        _merge_ffn_kernel,
        grid=(n // tm,),
        in_specs=[row(D_MODEL), row(M_V_W), row(A_Q_W), row(2 * D_MODEL)]
                 + [_const_spec(w.shape) for w in (wa, wb, wo, nw, wg, wu, wd)],
        out_specs=row(D_MODEL),
        out_shape=jax.ShapeDtypeStruct((n, D_MODEL), F32),
        compiler_params=_params(("arbitrary",)),
        name="merge_ffn",
    )(x2d, h_m, h_a, g_ab, wa, wb, wo, nw, wg, wu, wd)


def kernel(x_prompt, x_sample, state_mlstm_C, state_mlstm_n, state_mlstm_m, cache_swa_k, cache_swa_v,
           norm_mix_w, w_in, mlstm_i_bias, mlstm_f_bias, mlstm_norm_w, q_norm_w, k_norm_w, attn_sinks,
           w_branch_a, w_branch_b, w_out, norm_ffn_w, w_gate, w_up, w_down):
    depth = w_in.shape[0]
    assert depth == 1, "single trunk layer"
    l = 0
    bp, tp = x_prompt.shape[0], x_prompt.shape[1]
    bs, ts = x_sample.shape[0], x_sample.shape[1]
    assert tp % TOKEN_TILE == 0 and (bs * ts) % GROUP == 0 and GROUP % ts == 0 and ts & (ts - 1) == 0
    assert ts <= SUBLANES, "sample chunk must fit one sublane tile"

    wt = jnp.transpose(w_in[l])
    c_km, c_vm = M_QK_W, 2 * M_QK_W
    c_g = 2 * M_QK_W + 2 * M_V_W
    c_qa = c_g + 2 * M_HEADS
    c_ka = c_qa + A_Q_W
    gate_pad = jnp.zeros((BF16_ROWS - 2 * M_HEADS, D_MODEL), F32)
    wmt = jnp.concatenate([wt[0:c_km], wt[c_vm:c_g]], axis=0).astype(BF16)
    wqgt = jnp.concatenate([wt[c_qa:c_ka], wt[c_g:c_qa], gate_pad], axis=0).astype(BF16)
    wn = jnp.concatenate([wt[c_km:c_vm], wt[c_ka:]], axis=0).astype(BF16)
    head_of = jnp.arange(A_KV_W) // A_HEAD_DIM
    bd = (head_of[:, None] == head_of[None, :]).astype(BF16)
    qcol = (jnp.tile(q_norm_w[l], A_HEADS) * (A_HEAD_DIM ** -0.5)).reshape(A_Q_W, 1)
    krow = jnp.tile(k_norm_w[l], A_KV_HEADS).reshape(1, A_KV_W)
    nw_mix = norm_mix_w[l].reshape(1, D_MODEL)
    nw_ffn = norm_ffn_w[l].reshape(1, D_MODEL)
    bias_col = jnp.concatenate([mlstm_i_bias[l], mlstm_f_bias[l]]).reshape(2 * M_HEADS, 1)
    nw_col = mlstm_norm_w[l].reshape(M_V_W, 1)
    sinks = attn_sinks[l]
    proj = lambda x2d, tm: _proj(x2d, nw_mix, wmt, wqgt, wn, bd, qcol, krow, tm)

    xp = x_prompt.reshape(bp * tp, D_MODEL)
    later_ws = (w_branch_a[l], w_branch_b[l], w_out[l], w_gate[l], w_up[l], w_down[l])
    g_ab, h_m, h_a, kwin_p, vwin_p, st_p, m_p, wa, wb, wo, wg, wu, wd = _front(
        xp, bp, sinks, nw_mix, wmt, wqgt, wn, bd, qcol, krow, bias_col, nw_col, later_ws, TOKEN_TILE)
    merge = lambda x2d, h_m, h_a, g_ab, tm: _merge_ffn(x2d, h_m, h_a, g_ab, wa, wb, wo, nw_ffn, wg, wu, wd, tm)
    yp = merge(xp, h_m, h_a, g_ab, TOKEN_TILE).reshape(bp, tp, D_MODEL)
    c_p = jnp.swapaxes(st_p[:, :M_DV, :], 1, 2).reshape(bp, M_HEADS, M_DK, M_DV)
    n_p = st_p[:, M_DV, :].reshape(bp, M_HEADS, M_DK)
    m_pr = m_p[:, :M_HEADS, 0]

    ns = bs * ts
    xs = x_sample.reshape(ns, D_MODEL)
    tms = TOKEN_TILE if ns % TOKEN_TILE == 0 else GROUP
    qvot, gt, qat, k_m, kv_a, g_ab = proj(xs, tms)
    ngrp = ns // GROUP
    m_lanes = jnp.repeat(state_mlstm_m[l], ts, axis=0).reshape(ngrp, GROUP, M_HEADS)
    mrow = jnp.pad(jnp.swapaxes(m_lanes, 1, 2), ((0, 0), (0, SUBLANES - M_HEADS), (0, 0)))
    h_m, c_s, n_s, mt_s = _mlstm_sample(qvot, gt, k_m, mrow, bias_col, nw_col,
                                        state_mlstm_C[l].reshape(bs, M_QK_W, M_DV),
                                        state_mlstm_n[l].reshape(bs, M_QK_W), ts)
    wbuf = cache_swa_k.shape[2]
    to_fm = lambda a: jnp.transpose(a, (0, 2, 3, 1)).reshape(bs, A_KV_W, wbuf)
    from_fm = lambda a: jnp.transpose(a.reshape(bs, A_KV_HEADS, A_HEAD_DIM, wbuf), (0, 3, 1, 2))[None]
    h_a, kwin_s, vwin_s = _swa_sample(qat, kv_a, to_fm(cache_swa_k[l]), to_fm(cache_swa_v[l]), sinks, ts)
    ys = merge(xs, h_m, h_a, g_ab, tms).reshape(bs, ts, D_MODEL)
    m_s = jnp.swapaxes(mt_s[:, :M_HEADS, :], 1, 2).reshape(bs, ts, M_HEADS)[:, ts - 1, :]

    kv5 = lambda a: a.reshape(a.shape[0], a.shape[1], A_KV_HEADS, A_HEAD_DIM)[None]
    return (yp, ys,
            c_p[None], n_p[None], m_pr[None], kv5(kwin_p), kv5(vwin_p),
            c_s.reshape(bs, M_HEADS, M_DK, M_DV)[None], n_s.reshape(bs, M_HEADS, M_DK)[None], m_s[None],
            from_fm(kwin_s), from_fm(vwin_s))
```

```python
import functools

import jax
import jax.numpy as jnp
from jax import lax
from jax.experimental import pallas as pl
from jax.experimental.pallas import tpu as pltpu

F32 = jnp.float32
BF16 = jnp.bfloat16

D_MODEL = 1024
M_HEADS = 4
M_DK = 64
M_DV = 128
M_CHUNK = 64
M_QK_W = M_HEADS * M_DK
M_V_W = M_HEADS * M_DV
A_HEADS = 8
A_KV_HEADS = 2
A_HEAD_DIM = 64
A_GROUPS = A_HEADS // A_KV_HEADS
A_Q_W = A_HEADS * A_HEAD_DIM
A_KV_W = A_KV_HEADS * A_HEAD_DIM
WINDOW = 128
D_FF = 2816
EPS = 1e-6

LANES = 128
SUBLANES = 8
BF16_ROWS = 16
GROUP = 128
S_ROWS = M_DV + BF16_ROWS
TOKEN_TILE = 512
VMEM_LIMIT = 56 * 1024 * 1024
DK_SHIFT = M_DK.bit_length() - 1
WINDOW_SHIFT = WINDOW.bit_length() - 1
HALF = LANES // 2
assert A_HEAD_DIM == HALF and A_KV_W == LANES, "attention head pairs share one lane-width"

NT_DIMS = (((1,), (1,)), ((), ()))


def _dot(a, b):
    return jnp.dot(a, b, preferred_element_type=F32)


def _dot_nt(a, b):
    return lax.dot_general(a, b, NT_DIMS, preferred_element_type=F32)


def _const_spec(shape):
    nd = len(shape)
    return pl.BlockSpec(shape, lambda *_: (0,) * nd, pipeline_mode=pl.Buffered(1))


def _params(sem):
    return pltpu.CompilerParams(dimension_semantics=sem, vmem_limit_bytes=VMEM_LIMIT)


def _rms_rows(x, nw):
    ms = jnp.mean(x * x, axis=-1, keepdims=True)
    return (x * lax.rsqrt(ms + EPS)) * nw


GAB_CHUNK = 512


def _proj_chunks(x_ref, nw_ref, wmt_ref, wqgt_ref, wn_ref, bd_ref, qcol_ref, krow_ref,
                 qvot_ref, gt_ref, qat_ref, km_ref, kva_ref, gab_ref):
    hn = _rms_rows(x_ref[...], nw_ref[...]).astype(BF16)
    n_blk = qat_ref.shape[0]

    def put(ref, rows, val):
        for c in range(n_blk):
            ref[c, rows, :] = val[:, c * LANES:(c + 1) * LANES].astype(ref.dtype)

    def mlstm_qvo():
        put(qvot_ref, slice(None), _dot_nt(wmt_ref[...], hn))

    def attn_q_and_gates():
        qt = _dot_nt(wqgt_ref[...], hn)
        put(gt_ref, slice(None), qt[A_Q_W:A_Q_W + 2 * M_HEADS])
        for h in range(A_HEADS):
            hs = slice(h * A_HEAD_DIM, (h + 1) * A_HEAD_DIM)
            blk = qt[hs]
            ssq_q = jnp.sum(blk * blk, axis=0, keepdims=True)
            put(qat_ref, hs, (blk * lax.rsqrt(ssq_q * (1.0 / A_HEAD_DIM) + EPS)) * qcol_ref[hs])

    def mlstm_k():
        km_ref[...] = _dot_nt(hn, wn_ref[0:M_QK_W, :]).astype(km_ref.dtype)

    def branch_gates(c0):
        def run():
            w0 = M_QK_W + 2 * A_KV_W + c0
            gab_ref[:, c0:c0 + GAB_CHUNK] = _dot_nt(hn, wn_ref[w0:w0 + GAB_CHUNK, :]).astype(gab_ref.dtype)
        return run

    def attn_kv():
        kv = _dot_nt(hn, wn_ref[M_QK_W:M_QK_W + 2 * A_KV_W, :])
        k = kv[:, 0:A_KV_W]
        ksq = k * k
        hi = ksq.astype(BF16)
        lo = (ksq - hi.astype(F32)).astype(BF16)
        ssq = _dot(hi, bd_ref[...]) + _dot(lo, bd_ref[...])
        kva_ref[:, 0:A_KV_W] = (k * lax.rsqrt(ssq * (1.0 / A_HEAD_DIM) + EPS)) * krow_ref[...]
        kva_ref[:, A_KV_W:] = kv[:, A_KV_W:]

    return ([mlstm_qvo, attn_q_and_gates, mlstm_k]
            + [branch_gates(c0) for c0 in range(0, gab_ref.shape[1], GAB_CHUNK)] + [attn_kv])


def _proj_kernel(*refs):
    for piece in _proj_chunks(*refs):
        piece()


def _proj(x2d, nw, wmt, wqgt, wn, bd, qcol, krow, tm):
    n = x2d.shape[0]
    row = lambda w: pl.BlockSpec((tm, w), lambda i: (i, 0))
    n_blk = tm // LANES
    slab = lambda r: pl.BlockSpec((n_blk, r, LANES), lambda i: (i, 0, 0))
    w_qvo = M_QK_W + 2 * M_V_W
    return pl.pallas_call(
        _proj_kernel,
        grid=(n // tm,),
        in_specs=[row(D_MODEL)] + [_const_spec(a.shape) for a in (nw, wmt, wqgt, wn, bd, qcol, krow)],
        out_specs=[slab(w_qvo), slab(2 * M_HEADS), slab(A_Q_W), row(M_QK_W), row(2 * A_KV_W), row(2 * D_MODEL)],
        out_shape=[jax.ShapeDtypeStruct((n // LANES, w_qvo, LANES), BF16),
                   jax.ShapeDtypeStruct((n // LANES, 2 * M_HEADS, LANES), F32),
                   jax.ShapeDtypeStruct((n // LANES, A_Q_W, LANES), BF16),
                   jax.ShapeDtypeStruct((n, M_QK_W), BF16),
                   jax.ShapeDtypeStruct((n, 2 * A_KV_W), F32),
                   jax.ShapeDtypeStruct((n, 2 * D_MODEL), BF16)],
        compiler_params=_params(("arbitrary",)),
        name="proj",
    )(x2d, nw, wmt, wqgt, wn, bd, qcol, krow)


def _split3_rows(x):
    hi = x.astype(BF16).astype(F32)
    r1 = x - hi
    mid = r1.astype(BF16).astype(F32)
    lo = r1 - mid
    return jnp.concatenate([hi, mid, lo], axis=0).astype(BF16)


def _log_sigmoid(x):
    return jnp.minimum(x, 0.0) - jnp.log1p(jnp.exp(-jnp.abs(x)))


def _chunk_masks(chunk_shift):
    s = lax.broadcasted_iota(jnp.int32, (GROUP, GROUP), 0)
    t = lax.broadcasted_iota(jnp.int32, (GROUP, GROUP), 1)
    same = (s >> chunk_shift) == (t >> chunk_shift)
    return same, same & (s <= t)


def _group_gates(gt, same, causal):
    cm_bf = jnp.where(causal, 1.0, 0.0).astype(BF16)
    lf = _log_sigmoid(gt)
    nr = gt.shape[0]
    bt3 = _dot(_split3_rows(lf), cm_bf)
    bt = (bt3[0:nr] + bt3[nr:2 * nr]) + bt3[2 * nr:3 * nr]
    b4 = bt[M_HEADS:2 * M_HEADS]
    a4 = gt[0:M_HEADS] - b4
    a8 = jnp.concatenate([a4, a4], axis=0)
    a_cols = jnp.concatenate([a8, jnp.zeros((GROUP - SUBLANES, GROUP), F32)], axis=0).T
    at_mats, run_rows, chunk_rows = [], [], []
    for h in range(M_HEADS):
        at = jnp.broadcast_to(a_cols[:, h:h + 1], (GROUP, GROUP))
        at_mats.append(at)
        run_rows.append(jnp.max(jnp.where(causal, at, -jnp.inf), axis=0, keepdims=True))
        chunk_rows.append(jnp.max(jnp.where(same, at, -jnp.inf), axis=0, keepdims=True))
    return b4, a4, at_mats, jnp.concatenate(run_rows, axis=0), jnp.concatenate(chunk_rows, axis=0)


def _group_weights(m_prev, b4, a4, run4, chunk4):
    big_m = jnp.maximum(m_prev, run4)
    m_last = jnp.maximum(m_prev, chunk4)
    w_inter = jnp.exp(m_prev - big_m)
    g_vec = jnp.exp(m_prev - m_last)
    w_last = jnp.exp(a4 - m_last)
    m_t = b4 + big_m
    return big_m, w_inter, g_vec, w_last, m_t, jnp.exp(-m_t)


def _group_scores(qvot, ks):
    lane_head = lax.broadcasted_iota(jnp.int32, (1, M_QK_W), 1) >> DK_SHIFT
    row_head = lax.broadcasted_iota(jnp.int32, (M_QK_W, 1), 0) >> DK_SHIFT
    qt = qvot(0, M_QK_W)
    k_stack = jnp.concatenate([jnp.where(lane_head == h, ks, jnp.zeros_like(ks)) for h in range(M_HEADS)], axis=0)
    qw = jnp.concatenate([jnp.where(row_head == h, qt, jnp.zeros_like(qt)) for h in range(M_HEADS)], axis=1)
    zero_blk = jnp.zeros((M_DK, GROUP), BF16)
    sc_t = []
    for h in range(0, M_HEADS, 2):
        q_pair = jnp.concatenate(
            [jnp.concatenate([qvot(h * M_DK, (h + 1) * M_DK), zero_blk], axis=1),
             jnp.concatenate([zero_blk, qvot((h + 1) * M_DK, (h + 2) * M_DK)], axis=1)], axis=0)
        sc = _dot(ks[:, h * M_DK:(h + 2) * M_DK], q_pair)
        sc_t += [sc[:, 0:GROUP], sc[:, GROUP:]]
    return k_stack, qw, sc_t


def _group_values(qvot, sc_t, at_mats, big_m, causal):
    ones_rows = jnp.where(lax.broadcasted_iota(jnp.int32, (BF16_ROWS, GROUP), 0) == 0, 1.0, 0.0).astype(BF16)
    zero_blk = jnp.zeros((GROUP, GROUP), BF16)
    vta, s_t = [], []
    for h in range(M_HEADS):
        w_t = jnp.where(causal, jnp.exp(at_mats[h] - big_m[h:h + 1]), 0.0)
        s_t.append((sc_t[h] * w_t).astype(BF16))
        vta.append(jnp.concatenate([qvot(M_QK_W + h * M_DV, M_QK_W + (h + 1) * M_DV), ones_rows], axis=0))
    intra = []
    for h in range(0, M_HEADS, 2):
        pair = jnp.concatenate([jnp.concatenate([s_t[h], zero_blk], axis=1),
                                jnp.concatenate([zero_blk, s_t[h + 1]], axis=1)], axis=0)
        intra.append(_dot(jnp.concatenate(vta[h:h + 2], axis=1), pair))
    return vta, jnp.concatenate(intra, axis=1)


def _weighted_values(vta, w_rows):
    return jnp.concatenate([(vta[h].astype(F32) * w_rows[h:h + 1]).astype(BF16) for h in range(M_HEADS)], axis=1)


def _lanes_x(rows):
    return jnp.concatenate([rows[h:h + 1] for h in range(M_HEADS)], axis=1)


def _group_out(qvot, inter, intra, w_inter, e_neg_m, nw_ref):
    outs = []
    out_all = inter * _lanes_x(w_inter) + intra
    for h in range(M_HEADS):
        out_t = out_all[:, h * GROUP:(h + 1) * GROUP]
        hh = out_t[0:M_DV] / jnp.maximum(jnp.abs(out_t[M_DV:M_DV + 1]), e_neg_m[h:h + 1])
        ms = jnp.mean(hh * hh, axis=0, keepdims=True)
        hn = (hh * lax.rsqrt(ms + EPS)) * nw_ref[h * M_DV:(h + 1) * M_DV]
        o_t = qvot(M_QK_W + M_V_W + h * M_DV, M_QK_W + M_V_W + (h + 1) * M_DV)
        outs.append(hn * jax.nn.sigmoid(o_t.astype(F32)))
    return jnp.concatenate(outs, axis=0).T


def _decay_row(g_vec, lane0):
    lane_head = lax.broadcasted_iota(jnp.int32, (1, M_QK_W), 1) >> DK_SHIFT
    g_row = jnp.zeros((1, M_QK_W), F32)
    for h in range(M_HEADS):
        g_row = jnp.where(lane_head == h, g_vec[h:h + 1, lane0:lane0 + 1], g_row)
    return g_row


def _mlstm_sample_kernel(seq_len, qvot_ref, gt_ref, k_ref, mrow_ref, bias_ref, nw_ref, c_ref, n_ref,
                         h_ref, c_out_ref, n_out_ref, mt_ref):
    n_seq = GROUP // seq_len
    shift = seq_len.bit_length() - 1
    same, causal = _chunk_masks(shift)
    lane_seq = lax.broadcasted_iota(jnp.int32, (1, GROUP), 1) >> shift
    lane_seq_x = jnp.concatenate([lane_seq] * M_HEADS, axis=1)
    row0 = lax.broadcasted_iota(jnp.int32, (BF16_ROWS, M_QK_W), 0) == 0
    qvot = lambda r0, r1: qvot_ref[0, r0:r1, :]
    ks = k_ref[...] * (M_DK ** -0.5)
    b4, a4, at_mats, run4, chunk4 = _group_gates(gt_ref[0] + bias_ref[...], same, causal)
    big_m, w_inter, g_vec, w_last, m_t, e_neg_m = _group_weights(mrow_ref[0, 0:M_HEADS, :], b4, a4, run4, chunk4)
    mt_ref[0] = jnp.concatenate([m_t, m_t], axis=0)
    k_stack, qw, sc_t = _group_scores(qvot, ks)

    st_old = []
    for s in range(n_seq):
        n_rows = jnp.where(row0, jnp.broadcast_to(n_ref[s:s + 1, :], (BF16_ROWS, M_QK_W)), 0.0)
        st_old.append(jnp.concatenate([c_ref[s].T, n_rows], axis=0))
    inter_all = _dot(jnp.concatenate(st_old, axis=0).astype(BF16), qw)
    vta, intra = _group_values(qvot, sc_t, at_mats, big_m, causal)
    inter = inter_all[0:S_ROWS]
    for s in range(1, n_seq):
        inter = jnp.where(lane_seq_x == s, inter_all[s * S_ROWS:(s + 1) * S_ROWS], inter)
    tall = jnp.concatenate([_weighted_values(vta, jnp.where(lane_seq == s, w_last, 0.0)) for s in range(n_seq)],
                           axis=0)
    d_st = _dot(tall, k_stack)
    for s in range(n_seq):
        st_new = _decay_row(g_vec, s * seq_len) * st_old[s] + d_st[s * S_ROWS:(s + 1) * S_ROWS]
        c_out_ref[s] = st_new[0:M_DV].T
        n_out_ref[s:s + 1, :] = st_new[M_DV:M_DV + 1]
    h_ref[...] = _group_out(qvot, inter, intra, w_inter, e_neg_m, nw_ref).astype(h_ref.dtype)


def _mlstm_sample(qvot, gt, k_m, mrow, bias_col, nw_col, c, n, seq_len):
    ngrp = qvot.shape[0]
    n_seq = GROUP // seq_len
    full = lambda a: pl.BlockSpec(a.shape, lambda i: (0,) * a.ndim)
    slab = lambda a: pl.BlockSpec((1,) + a.shape[1:], lambda i: (i, 0, 0))
    row = lambda w: pl.BlockSpec((GROUP, w), lambda i: (i, 0))
    c_spec = pl.BlockSpec((n_seq,) + c.shape[1:], lambda i: (i, 0, 0))
    n_spec = pl.BlockSpec((n_seq, n.shape[1]), lambda i: (i, 0))
    return pl.pallas_call(
        functools.partial(_mlstm_sample_kernel, seq_len),
        grid=(ngrp,),
        in_specs=[slab(qvot), slab(gt), row(M_QK_W), slab(mrow), full(bias_col), full(nw_col), c_spec, n_spec],
        out_specs=[row(M_V_W), c_spec, n_spec, slab(mrow)],
        out_shape=[jax.ShapeDtypeStruct((ngrp * GROUP, M_V_W), BF16),
                   jax.ShapeDtypeStruct(c.shape, F32),
                   jax.ShapeDtypeStruct(n.shape, F32),
                   jax.ShapeDtypeStruct(mrow.shape, F32)],
        compiler_params=_params(("arbitrary",)),
        name="mlstm_sample",
    )(qvot, gt, k_m, mrow, bias_col, nw_col, c, n)


def _swa_scores(qbs, has_prev, sink_ref, qt_ref, kv_ref, kvp_ref):
    hd = A_HEAD_DIM
    nq = A_GROUPS * WINDOW
    si = lax.broadcasted_iota(jnp.int32, (2 * WINDOW, nq), 0)
    qi = lax.broadcasted_iota(jnp.int32, (2 * WINDOW, nq), 1) & (WINDOW - 1)
    local = ((si < WINDOW) & (si > qi)) | ((si >= WINDOW) & (si - WINDOW <= qi))
    first = local & (has_prev | (si >= WINDOW))
    lane_grp = lax.broadcasted_iota(jnp.int32, (1, nq), 1) >> WINDOW_SHIFT
    kv_block = lambda i: kvp_ref[...] if i == 0 else kv_ref[(i - 1) * WINDOW:i * WINDOW, :]
    k_bf = lambda i: kv_block(i)[:, 0:A_KV_W].astype(BF16)
    vt_bf = lambda i: kv_block(i)[:, A_KV_W:].T.astype(BF16)
    zeros = jnp.zeros((hd, nq), BF16)
    sinks = []
    for kvh in range(A_KV_HEADS):
        sk = jnp.zeros((1, nq), F32)
        for g in range(A_GROUPS):
            sk = jnp.where(lane_grp == g, sink_ref[kvh * A_GROUPS + g], sk)
        sinks.append(sk)
    units = []
    for qb in qbs:
        kk = jnp.concatenate([k_bf(qb), k_bf(qb + 1)], axis=0)
        vt = jnp.concatenate([vt_bf(qb), vt_bf(qb + 1)], axis=1)
        mask = local if qb > 0 else first
        for kvh in range(A_KV_HEADS):
            q4t = jnp.concatenate(
                [qt_ref[qb, (kvh * A_GROUPS + g) * hd:(kvh * A_GROUPS + g + 1) * hd, :] for g in range(A_GROUPS)],
                axis=1)
            wq = jnp.concatenate([q4t, zeros] if kvh == 0 else [zeros, q4t], axis=0)
            units.append((jnp.where(mask, _dot(kk, wq), -jnp.inf), sinks[kvh], vt))
    return units


def _swa_finish(qbs, units, h_ref):
    hd = A_HEAD_DIM
    for i, qb in enumerate(qbs):
        pieces = []
        for kvh in range(A_KV_HEADS):
            s, sk, vt = units[i * A_KV_HEADS + kvh]
            mx = jnp.maximum(jnp.max(s, axis=0, keepdims=True), sk)
            p = jnp.exp(s - mx)
            den = jnp.sum(p, axis=0, keepdims=True) + jnp.exp(sk - mx)
            ot = _dot(vt[kvh * hd:(kvh + 1) * hd], p.astype(BF16)) / den
            pieces += [ot[:, g * WINDOW:(g + 1) * WINDOW] for g in range(A_GROUPS)]
        h_t = jnp.concatenate(pieces, axis=0)
        h_ref[qb * WINDOW:(qb + 1) * WINDOW, :] = h_t.T.astype(h_ref.dtype)


def _front_kernel(tiles_per_seq, n_cast, *refs):
    (sink_ref, x_ref, nw_ref, wmt_ref, wqgt_ref, wn_ref, bd_ref, qcol_ref, krow_ref, bias_ref, nwm_ref) = refs[:11]
    cast_in = refs[11:11 + n_cast]
    (gab_ref, hm_ref, ha_ref, kwin_ref, vwin_ref, st_ref, m_ref) = refs[11 + n_cast:18 + n_cast]
    cast_out = refs[18 + n_cast:18 + 2 * n_cast]
    set_a, set_b = refs[18 + 2 * n_cast:23 + 2 * n_cast], refs[23 + 2 * n_cast:28 + 2 * n_cast]
    st_s, m_s, kvp_s = refs[28 + 2 * n_cast:]
    odd = (pl.program_id(0) & 1) == 1
    for parity, q_set, p_set in ((jnp.logical_not(odd), set_a, set_b), (odd, set_b, set_a)):
        pl.when(parity)(functools.partial(
            _front_body, tiles_per_seq, sink_ref, x_ref, nw_ref, wmt_ref, wqgt_ref, wn_ref, bd_ref, qcol_ref,
            krow_ref, bias_ref, nwm_ref, gab_ref, hm_ref, ha_ref, kwin_ref, vwin_ref, st_ref, m_ref,
            *q_set, *p_set, st_s, m_s, kvp_s))
    for src, dst in zip(cast_in, cast_out):
        dst[...] = src[...].astype(dst.dtype)


def _front_body(tiles_per_seq, sink_ref, x_ref, nw_ref, wmt_ref, wqgt_ref, wn_ref, bd_ref, qcol_ref, krow_ref,
                bias_ref, nwm_ref,
                gab_ref, hm_ref, ha_ref, kwin_ref, vwin_ref, st_ref, m_ref,
                q_qvot, q_gt, q_qat, q_km, q_kva, p_qvot, p_gt, p_qat, p_km, p_kva, st_s, m_s, kvp_s):
    k = pl.program_id(0)

    @pl.when(k == 0)
    def _():
        for r in (p_qvot, p_gt, p_qat, p_km, p_kva, st_s, m_s, kvp_s):
            r[...] = jnp.zeros(r.shape, r.dtype)

    pieces = _proj_chunks(x_ref, nw_ref, wmt_ref, wqgt_ref, wn_ref, bd_ref, qcol_ref, krow_ref,
                          q_qvot, q_gt, q_qat, q_km, q_kva, gab_ref)

    seq_start = lax.rem(k - 1 + tiles_per_seq, tiles_per_seq) == 0
    n_blk = p_qat.shape[0]
    same, causal = _chunk_masks(GROUP.bit_length() - 1)
    st = jnp.where(seq_start, 0.0, st_s[...])
    m_col = jnp.where(seq_start, 0.0, m_s[0:M_HEADS, 0:1])

    qvots = [lambda r0, r1, g=g: p_qvot[g, r0:r1, :] for g in range(n_blk)]
    scores = [_group_scores(qvots[g], p_km[g * GROUP:(g + 1) * GROUP, :] * (M_DK ** -0.5))
              for g in range(n_blk)]
    inter = [_dot(st.astype(BF16), scores[0][1])]
    gates = [_group_gates(p_gt[g] + bias_ref[...], same, causal) for g in range(n_blk)]
    for piece in pieces[0:1]:
        piece()
    grp = []
    for g in range(n_blk):
        b4, a4, at_mats, run4, chunk4 = gates[g]
        weights = _group_weights(jnp.broadcast_to(m_col, (M_HEADS, GROUP)), b4, a4, run4, chunk4)
        m_col = weights[4][:, GROUP - 1:GROUP]
        grp.append((qvots[g], at_mats, weights) + scores[g])
    has_prev = jnp.logical_not(seq_start)
    qb_lo, qb_hi = list(range(0, n_blk // 2)), list(range(n_blk // 2, n_blk))
    units_lo = _swa_scores(qb_lo, has_prev, sink_ref, p_qat, p_kva, kvp_s)
    for piece in pieces[1:3]:
        piece()

    vals = []

    def group_values(g):
        qvot, at_mats, (big_m, _, _, w_last, _, _), k_stack, _, sc_t = grp[g]
        vta, intra = _group_values(qvot, sc_t, at_mats, big_m, causal)
        vals.append((intra, _dot(_weighted_values(vta, w_last), k_stack)))

    for g in range(0, n_blk // 2):
        group_values(g)
    _swa_finish(qb_lo[:1], units_lo[:A_KV_HEADS], ha_ref)
    for piece in pieces[3:4]:
        piece()
    for g in range(n_blk // 2, n_blk):
        group_values(g)
    _swa_finish(qb_lo[1:], units_lo[A_KV_HEADS:], ha_ref)
    units_hi = _swa_scores(qb_hi, has_prev, sink_ref, p_qat, p_kva, kvp_s)
    for piece in pieces[4:6]:
        piece()

    for g in range(n_blk):
        st = _decay_row(grp[g][2][2], 0) * st + vals[g][1]
        if g + 1 < n_blk:
            inter.append(_dot(st.astype(BF16), grp[g + 1][4]))
        if g == n_blk // 2 - 1:
            _swa_finish(qb_hi[:1], units_hi[:A_KV_HEADS], ha_ref)
            for piece in pieces[6:7]:
                piece()
    _swa_finish(qb_hi[1:], units_hi[A_KV_HEADS:], ha_ref)
    for piece in pieces[7:]:
        piece()

    for g in range(n_blk):
        qvot, _, (_, w_inter, _, _, _, e_neg_m) = grp[g][0:3]
        hm_ref[g * GROUP:(g + 1) * GROUP, :] = _group_out(qvot, inter[g], vals[g][0], w_inter, e_neg_m,
                                                          nwm_ref).astype(hm_ref.dtype)
    st_s[...] = st
    st_ref[0] = st
    m_rows = jnp.broadcast_to(m_col, (M_HEADS, LANES))
    m_rows = jnp.concatenate([m_rows, m_rows], axis=0)
    m_s[...] = m_rows
    m_ref[0] = m_rows
    tm = p_kva.shape[0]
    kwin_ref[0] = p_kva[tm - WINDOW:, 0:A_KV_W]
    vwin_ref[0] = p_kva[tm - WINDOW:, A_KV_W:]
    kvp_s[...] = p_kva[tm - WINDOW:, :]


def _cast_rows(rows, n_steps):
    rb = BF16_ROWS
    while rows % rb or rows // rb > n_steps:
        rb += BF16_ROWS
    return rb


def _front(x2d, nb, sinks, nw, wmt, wqgt, wn, bd, qcol, krow, bias_col, nwm_col, cast_ws, tm):
    n = x2d.shape[0]
    n_tiles = n // tm
    tps = n_tiles // nb
    n_blk = tm // LANES
    w_qvo = M_QK_W + 2 * M_V_W
    cur = lambda w: pl.BlockSpec((tm, w), lambda k: (jnp.minimum(k, n_tiles - 1), 0))
    prev = lambda w: pl.BlockSpec((tm, w), lambda k: (jnp.maximum(k - 1, 0), 0))
    per_seq = lambda r, w: pl.BlockSpec((1, r, w), lambda k: (jnp.maximum(k - 1, 0) // tps, 0, 0))

    def cast_spec(a):
        rb = _cast_rows(a.shape[0], n_tiles)
        return pl.BlockSpec((rb, a.shape[1]), lambda k: (jnp.minimum(k, a.shape[0] // rb - 1), 0))

    cast_in_specs = [cast_spec(a) for a in cast_ws]
    cast_out_specs = [cast_spec(a) for a in cast_ws]
    proj_scratch = [pltpu.VMEM((n_blk, w_qvo, LANES), BF16), pltpu.VMEM((n_blk, 2 * M_HEADS, LANES), F32),
                    pltpu.VMEM((n_blk, A_Q_W, LANES), BF16), pltpu.VMEM((tm, M_QK_W), BF16),
                    pltpu.VMEM((tm, 2 * A_KV_W), F32)]
    return pl.pallas_call(
        functools.partial(_front_kernel, tps, len(cast_ws)),
        grid=(n_tiles + 1,),
        in_specs=[pl.BlockSpec(memory_space=pltpu.SMEM), cur(D_MODEL)]
                 + [_const_spec(a.shape) for a in (nw, wmt, wqgt, wn, bd, qcol, krow, bias_col, nwm_col)]
                 + cast_in_specs,
        out_specs=[cur(2 * D_MODEL), prev(M_V_W), prev(A_Q_W), per_seq(WINDOW, A_KV_W), per_seq(WINDOW, A_KV_W),
                   per_seq(S_ROWS, M_QK_W), per_seq(SUBLANES, LANES)] + cast_out_specs,
        out_shape=[jax.ShapeDtypeStruct((n, 2 * D_MODEL), BF16),
                   jax.ShapeDtypeStruct((n, M_V_W), BF16),
                   jax.ShapeDtypeStruct((n, A_Q_W), BF16),
                   jax.ShapeDtypeStruct((nb, WINDOW, A_KV_W), F32),
                   jax.ShapeDtypeStruct((nb, WINDOW, A_KV_W), F32),
                   jax.ShapeDtypeStruct((nb, S_ROWS, M_QK_W), F32),
                   jax.ShapeDtypeStruct((nb, SUBLANES, LANES), F32)]
                  + [jax.ShapeDtypeStruct(a.shape, BF16) for a in cast_ws],
        scratch_shapes=proj_scratch + proj_scratch + [pltpu.VMEM((S_ROWS, M_QK_W), F32),
                                                       pltpu.VMEM((SUBLANES, LANES), F32),
                                                       pltpu.VMEM((WINDOW, 2 * A_KV_W), F32)],
        compiler_params=_params(("arbitrary",)),
        name="front",
    )(sinks, x2d, nw, wmt, wqgt, wn, bd, qcol, krow, bias_col, nwm_col, *cast_ws)


def _swa_sample_kernel(seq_len, sink_ref, qt_ref, kv_ref, ck_ref, cv_ref, h_ref, kwin_ref, vwin_ref):
    n_seq = GROUP // seq_len
    wb = ck_ref.shape[2]
    lane = lax.broadcasted_iota(jnp.int32, (1, LANES), 1)
    lo_half = lane < HALF
    q_rows = qt_ref[0].astype(F32).T
    kv_new = kv_ref[...]
    k_new = kv_new[:, 0:A_KV_W].reshape(n_seq, seq_len, A_KV_W)
    v_new = kv_new[:, A_KV_W:].reshape(n_seq, seq_len, A_KV_W)

    def to_kv_half(x, head):
        kvh = head // A_GROUPS
        if head % 2 != kvh:
            x = pltpu.roll(x, HALF, axis=1)
        return jnp.where(lo_half if kvh == 0 else ~lo_half, x, 0.0)

    lhs = jnp.concatenate(
        [to_kv_half(q_rows[:, (h // 2) * LANES:(h // 2 + 1) * LANES], h).reshape(n_seq, seq_len, LANES)
         for h in range(A_HEADS)], axis=1).astype(BF16)
    zpad = jnp.zeros((n_seq, BF16_ROWS - seq_len, A_KV_W), F32)
    k_nb = jnp.concatenate([k_new, zpad], axis=1).astype(BF16)
    v_nb = jnp.concatenate([v_new, zpad], axis=1).astype(BF16)
    s_c = jnp.einsum('sqf,sfk->sqk', lhs, ck_ref[...].astype(BF16), preferred_element_type=F32)
    s_n = jnp.einsum('sqf,skf->sqk', lhs, k_nb, preferred_element_type=F32)
    nrow = A_HEADS * seq_len
    ti = lax.broadcasted_iota(jnp.int32, (nrow, wb), 0) & (seq_len - 1)
    ki = lax.broadcasted_iota(jnp.int32, (nrow, wb), 1)
    mask_c = (ti + wb - ki) < WINDOW
    ti_n = lax.broadcasted_iota(jnp.int32, (nrow, BF16_ROWS), 0) & (seq_len - 1)
    ki_n = lax.broadcasted_iota(jnp.int32, (nrow, BF16_ROWS), 1)
    mask_n = ki_n <= ti_n
    row_head = lax.broadcasted_iota(jnp.int32, (nrow, 1), 0) >> (seq_len.bit_length() - 1)
    sk = jnp.zeros((nrow, 1), F32)
    for h in range(A_HEADS):
        sk = jnp.where(row_head == h, sink_ref[h], sk)
    s_c = jnp.where(mask_c, s_c, -jnp.inf)
    s_n = jnp.where(mask_n, s_n, -jnp.inf)
    mx = jnp.maximum(jnp.maximum(jnp.max(s_c, axis=-1, keepdims=True), jnp.max(s_n, axis=-1, keepdims=True)), sk)
    p_c = jnp.exp(s_c - mx)
    p_n = jnp.exp(s_n - mx)
    den = jnp.sum(p_c, axis=-1, keepdims=True) + jnp.sum(p_n, axis=-1, keepdims=True) + jnp.exp(sk - mx)
    o = (jnp.einsum('sqk,sfk->sqf', p_c.astype(BF16), cv_ref[...].astype(BF16), preferred_element_type=F32)
         + jnp.einsum('sqk,skf->sqf', p_n.astype(BF16), v_nb, preferred_element_type=F32)) / den

    def from_kv_half(head):
        x = o[:, head * seq_len:(head + 1) * seq_len, :].reshape(GROUP, LANES)
        return pltpu.roll(x, HALF, axis=1) if head % 2 != head // A_GROUPS else x

    for c in range(A_HEADS // 2):
        h_ref[:, c * LANES:(c + 1) * LANES] = jnp.where(lo_half, from_kv_half(2 * c),
                                                        from_kv_half(2 * c + 1)).astype(h_ref.dtype)

    kt_new, vt_new = kv_new[:, 0:A_KV_W].T, kv_new[:, A_KV_W:].T
    for s in range(n_seq):
        put = (wb - seq_len - s * seq_len) % LANES
        for new_t, c_ref, win_ref in ((kt_new, ck_ref, kwin_ref), (vt_new, cv_ref, vwin_ref)):
            win_ref[s] = jnp.where(lane >= wb - seq_len, pltpu.roll(new_t, put, axis=1) if put else new_t,
                                   pltpu.roll(c_ref[s], wb - seq_len, axis=1))


def _swa_sample(qat, kv_a, cache_k, cache_v, sinks, seq_len):
    ngrp = qat.shape[0]
    n_seq = GROUP // seq_len
    wb = cache_k.shape[2]
    assert wb == LANES, "window positions fill one lane-width"
    row = lambda w: pl.BlockSpec((GROUP, w), lambda i: (i, 0))
    cache = pl.BlockSpec((n_seq, A_KV_W, wb), lambda i: (i, 0, 0))
    return pl.pallas_call(
        functools.partial(_swa_sample_kernel, seq_len),
        grid=(ngrp,),
        in_specs=[pl.BlockSpec(memory_space=pltpu.SMEM), pl.BlockSpec((1, A_Q_W, LANES), lambda i: (i, 0, 0)),
                  row(2 * A_KV_W), cache, cache],
        out_specs=[row(A_Q_W), cache, cache],
        out_shape=[jax.ShapeDtypeStruct((ngrp * GROUP, A_Q_W), BF16),
                   jax.ShapeDtypeStruct(cache_k.shape, F32),
                   jax.ShapeDtypeStruct(cache_v.shape, F32)],
        compiler_params=_params(("arbitrary",)),
        name="swa_sample",
    )(sinks, qat, kv_a, cache_k, cache_v)


def _merge_ffn_kernel(x_ref, hm_ref, ha_ref, gab_ref, wa_ref, wb_ref, wo_ref, nw_ref, wg_ref, wu_ref, wd_ref,
                      y_ref):
    tm = x_ref.shape[0]
    halves = [slice(0, tm // 2), slice(tm // 2, tm)]
    mix = []
    for rs in halves:
        ga = jax.nn.sigmoid(gab_ref[rs, 0:D_MODEL].astype(F32))
        gb = jax.nn.sigmoid(gab_ref[rs, D_MODEL:].astype(F32))
        mix.append(ga * _dot(hm_ref[rs, :], wa_ref[...]) + gb * _dot(ha_ref[rs, :], wb_ref[...]))
    x1 = [x_ref[rs, :] + _dot(m.astype(BF16), wo_ref[...]) for rs, m in zip(halves, mix)]
    hf = [_rms_rows(v, nw_ref[...]).astype(BF16) for v in x1]
    act = [(jax.nn.silu(_dot(h, wg_ref[...])) * _dot(h, wu_ref[...])).astype(BF16) for h in hf]
    for rs, v, a in zip(halves, x1, act):
        y_ref[rs, :] = v + _dot(a, wd_ref[...])


def _merge_ffn(x2d, h_m, h_a, g_ab, wa, wb, wo, nw, wg, wu, wd, tm):
    n = x2d.shape[0]
    row = lambda w: pl.BlockSpec((tm, w), lambda i: (i, 0))
    return pl.pallas_call(
        _merge_ffn_kernel,
        grid=(n // tm,),
        in_specs=[row(D_MODEL), row(M_V_W), row(A_Q_W), row(2 * D_MODEL)]
                 + [_const_spec(w.shape) for w in (wa, wb, wo, nw, wg, wu, wd)],
        out_specs=row(D_MODEL),
        out_shape=jax.ShapeDtypeStruct((n, D_MODEL), F32),
        compiler_params=_params(("arbitrary",)),
        name="merge_ffn",
    )(x2d, h_m, h_a, g_ab, wa, wb, wo, nw, wg, wu, wd)


def kernel(x_prompt, x_sample, state_mlstm_C, state_mlstm_n, state_mlstm_m, cache_swa_k, cache_swa_v,
           norm_mix_w, w_in, mlstm_i_bias, mlstm_f_bias, mlstm_norm_w, q_norm_w, k_norm_w, attn_sinks,
           w_branch_a, w_branch_b, w_out, norm_ffn_w, w_gate, w_up, w_down):
    depth = w_in.shape[0]
    assert depth == 1, "single trunk layer"
    l = 0
    bp, tp = x_prompt.shape[0], x_prompt.shape[1]
    bs, ts = x_sample.shape[0], x_sample.shape[1]
    assert tp % TOKEN_TILE == 0 and (bs * ts) % GROUP == 0 and GROUP % ts == 0 and ts & (ts - 1) == 0
    assert ts <= SUBLANES, "sample chunk must fit one sublane tile"

    wt = jnp.transpose(w_in[l])
    c_km, c_vm = M_QK_W, 2 * M_QK_W
    c_g = 2 * M_QK_W + 2 * M_V_W
    c_qa = c_g + 2 * M_HEADS
    c_ka = c_qa + A_Q_W
    gate_pad = jnp.zeros((BF16_ROWS - 2 * M_HEADS, D_MODEL), F32)
    wmt = jnp.concatenate([wt[0:c_km], wt[c_vm:c_g]], axis=0).astype(BF16)
    wqgt = jnp.concatenate([wt[c_qa:c_ka], wt[c_g:c_qa], gate_pad], axis=0).astype(BF16)
    wn = jnp.concatenate([wt[c_km:c_vm], wt[c_ka:]], axis=0).astype(BF16)
    head_of = jnp.arange(A_KV_W) // A_HEAD_DIM
    bd = (head_of[:, None] == head_of[None, :]).astype(BF16)
    qcol = (jnp.tile(q_norm_w[l], A_HEADS) * (A_HEAD_DIM ** -0.5)).reshape(A_Q_W, 1)
    krow = jnp.tile(k_norm_w[l], A_KV_HEADS).reshape(1, A_KV_W)
    nw_mix = norm_mix_w[l].reshape(1, D_MODEL)
    nw_ffn = norm_ffn_w[l].reshape(1, D_MODEL)
    bias_col = jnp.concatenate([mlstm_i_bias[l], mlstm_f_bias[l]]).reshape(2 * M_HEADS, 1)
    nw_col = mlstm_norm_w[l].reshape(M_V_W, 1)
    sinks = attn_sinks[l]
    proj = lambda x2d, tm: _proj(x2d, nw_mix, wmt, wqgt, wn, bd, qcol, krow, tm)

    xp = x_prompt.reshape(bp * tp, D_MODEL)
    later_ws = (w_branch_a[l], w_branch_b[l], w_out[l], w_gate[l], w_up[l], w_down[l])
    g_ab, h_m, h_a, kwin_p, vwin_p, st_p, m_p, wa, wb, wo, wg, wu, wd = _front(
        xp, bp, sinks, nw_mix, wmt, wqgt, wn, bd, qcol, krow, bias_col, nw_col, later_ws, TOKEN_TILE)
    merge = lambda x2d, h_m, h_a, g_ab, tm: _merge_ffn(x2d, h_m, h_a, g_ab, wa, wb, wo, nw_ffn, wg, wu, wd, tm)
    yp = merge(xp, h_m, h_a, g_ab, TOKEN_TILE).reshape(bp, tp, D_MODEL)
    c_p = jnp.swapaxes(st_p[:, :M_DV, :], 1, 2).reshape(bp, M_HEADS, M_DK, M_DV)
    n_p = st_p[:, M_DV, :].reshape(bp, M_HEADS, M_DK)
    m_pr = m_p[:, :M_HEADS, 0]

    ns = bs * ts
    xs = x_sample.reshape(ns, D_MODEL)
    tms = TOKEN_TILE if ns % TOKEN_TILE == 0 else GROUP
    qvot, gt, qat, k_m, kv_a, g_ab = proj(xs, tms)
    ngrp = ns // GROUP
    m_lanes = jnp.repeat(state_mlstm_m[l], ts, axis=0).reshape(ngrp, GROUP, M_HEADS)
    mrow = jnp.pad(jnp.swapaxes(m_lanes, 1, 2), ((0, 0), (0, SUBLANES - M_HEADS), (0, 0)))
    h_m, c_s, n_s, mt_s = _mlstm_sample(qvot, gt, k_m, mrow, bias_col, nw_col,
                                        state_mlstm_C[l].reshape(bs, M_QK_W, M_DV),
                                        state_mlstm_n[l].reshape(bs, M_QK_W), ts)
    wbuf = cache_swa_k.shape[2]
    to_fm = lambda a: jnp.transpose(a, (0, 2, 3, 1)).reshape(bs, A_KV_W, wbuf)
    from_fm = lambda a: jnp.transpose(a.reshape(bs, A_KV_HEADS, A_HEAD_DIM, wbuf), (0, 3, 1, 2))[None]
    h_a, kwin_s, vwin_s = _swa_sample(qat, kv_a, to_fm(cache_swa_k[l]), to_fm(cache_swa_v[l]), sinks, ts)
    ys = merge(xs, h_m, h_a, g_ab, tms).reshape(bs, ts, D_MODEL)
    m_s = jnp.swapaxes(mt_s[:, :M_HEADS, :], 1, 2).reshape(bs, ts, M_HEADS)[:, ts - 1, :]

    kv5 = lambda a: a.reshape(a.shape[0], a.shape[1], A_KV_HEADS, A_HEAD_DIM)[None]
    return (yp, ys,
            c_p[None], n_p[None], m_pr[None], kv5(kwin_p), kv5(vwin_p),
            c_s.reshape(bs, M_HEADS, M_DK, M_DV)[None], n_s.reshape(bs, M_HEADS, M_DK)[None], m_s[None],
            from_fm(kwin_s), from_fm(vwin_s))
```

```python
import functools

import jax
import jax.numpy as jnp
from jax import lax
from jax.experimental import pallas as pl
from jax.experimental.pallas import tpu as pltpu

F32 = jnp.float32
BF16 = jnp.bfloat16

D_MODEL = 1024
M_HEADS = 4
M_DK = 64
M_DV = 128
M_CHUNK = 64
M_QK_W = M_HEADS * M_DK
M_V_W = M_HEADS * M_DV
A_HEADS = 8
A_KV_HEADS = 2
A_HEAD_DIM = 64
A_GROUPS = A_HEADS // A_KV_HEADS
A_Q_W = A_HEADS * A_HEAD_DIM
A_KV_W = A_KV_HEADS * A_HEAD_DIM
WINDOW = 128
D_FF = 2816
EPS = 1e-6

LANES = 128
SUBLANES = 8
BF16_ROWS = 16
GROUP = 128
S_ROWS = M_DV + BF16_ROWS
TOKEN_TILE = 512
VMEM_LIMIT = 56 * 1024 * 1024
DK_SHIFT = M_DK.bit_length() - 1
WINDOW_SHIFT = WINDOW.bit_length() - 1
HALF = LANES // 2
assert A_HEAD_DIM == HALF and A_KV_W == LANES, "attention head pairs share one lane-width"

NT_DIMS = (((1,), (1,)), ((), ()))


def _dot(a, b):
    return jnp.dot(a, b, preferred_element_type=F32)


def _dot_nt(a, b):
    return lax.dot_general(a, b, NT_DIMS, preferred_element_type=F32)


def _const_spec(shape):
    nd = len(shape)
    return pl.BlockSpec(shape, lambda *_: (0,) * nd, pipeline_mode=pl.Buffered(1))


def _params(sem):
    return pltpu.CompilerParams(dimension_semantics=sem, vmem_limit_bytes=VMEM_LIMIT)


def _rms_rows(x, nw):
    ms = jnp.mean(x * x, axis=-1, keepdims=True)
    return (x * lax.rsqrt(ms + EPS)) * nw


GAB_CHUNK = 256


def _proj_chunks(x_ref, nw_ref, wmt_ref, wqgt_ref, wn_ref, bd_ref, qcol_ref, krow_ref,
                 qvot_ref, gt_ref, qat_ref, km_ref, kva_ref, gab_ref):
    hn = _rms_rows(x_ref[...], nw_ref[...]).astype(BF16)
    n_blk = qat_ref.shape[0]

    def put(ref, rows, val):
        for c in range(n_blk):
            ref[c, rows, :] = val[:, c * LANES:(c + 1) * LANES].astype(ref.dtype)

    def mlstm_qvo():
        put(qvot_ref, slice(None), _dot_nt(wmt_ref[...], hn))

    def attn_q_and_gates():
        qt = _dot_nt(wqgt_ref[...], hn)
        put(gt_ref, slice(None), qt[A_Q_W:A_Q_W + 2 * M_HEADS])
        for h in range(A_HEADS):
            hs = slice(h * A_HEAD_DIM, (h + 1) * A_HEAD_DIM)
            blk = qt[hs]
            ssq_q = jnp.sum(blk * blk, axis=0, keepdims=True)
            put(qat_ref, hs, (blk * lax.rsqrt(ssq_q * (1.0 / A_HEAD_DIM) + EPS)) * qcol_ref[hs])

    def mlstm_k():
        km_ref[...] = _dot_nt(hn, wn_ref[0:M_QK_W, :]).astype(km_ref.dtype)

    def branch_gates(c0):
        def run():
            w0 = M_QK_W + 2 * A_KV_W + c0
            gab_ref[:, c0:c0 + GAB_CHUNK] = _dot_nt(hn, wn_ref[w0:w0 + GAB_CHUNK, :]).astype(gab_ref.dtype)
        return run

    def attn_kv():
        kv = _dot_nt(hn, wn_ref[M_QK_W:M_QK_W + 2 * A_KV_W, :])
        k = kv[:, 0:A_KV_W]
        ksq = k * k
        hi = ksq.astype(BF16)
        lo = (ksq - hi.astype(F32)).astype(BF16)
        ssq = _dot(hi, bd_ref[...]) + _dot(lo, bd_ref[...])
        kva_ref[:, 0:A_KV_W] = (k * lax.rsqrt(ssq * (1.0 / A_HEAD_DIM) + EPS)) * krow_ref[...]
        kva_ref[:, A_KV_W:] = kv[:, A_KV_W:]

    return ([mlstm_qvo, attn_q_and_gates, mlstm_k]
            + [branch_gates(c0) for c0 in range(0, gab_ref.shape[1], GAB_CHUNK)] + [attn_kv])


def _proj_kernel(*refs):
    for piece in _proj_chunks(*refs):
        piece()


def _proj(x2d, nw, wmt, wqgt, wn, bd, qcol, krow, tm):
    n = x2d.shape[0]
    row = lambda w: pl.BlockSpec((tm, w), lambda i: (i, 0))
    n_blk = tm // LANES
    slab = lambda r: pl.BlockSpec((n_blk, r, LANES), lambda i: (i, 0, 0))
    w_qvo = M_QK_W + 2 * M_V_W
    return pl.pallas_call(
        _proj_kernel,
        grid=(n // tm,),
        in_specs=[row(D_MODEL)] + [_const_spec(a.shape) for a in (nw, wmt, wqgt, wn, bd, qcol, krow)],
        out_specs=[slab(w_qvo), slab(2 * M_HEADS), slab(A_Q_W), row(M_QK_W), row(2 * A_KV_W), row(2 * D_MODEL)],
        out_shape=[jax.ShapeDtypeStruct((n // LANES, w_qvo, LANES), BF16),
                   jax.ShapeDtypeStruct((n // LANES, 2 * M_HEADS, LANES), F32),
                   jax.ShapeDtypeStruct((n // LANES, A_Q_W, LANES), BF16),
                   jax.ShapeDtypeStruct((n, M_QK_W), BF16),
                   jax.ShapeDtypeStruct((n, 2 * A_KV_W), F32),
                   jax.ShapeDtypeStruct((n, 2 * D_MODEL), BF16)],
        compiler_params=_params(("arbitrary",)),
        name="proj",
    )(x2d, nw, wmt, wqgt, wn, bd, qcol, krow)


def _split3_rows(x):
    hi = x.astype(BF16).astype(F32)
    r1 = x - hi
    mid = r1.astype(BF16).astype(F32)
    lo = r1 - mid
    return jnp.concatenate([hi, mid, lo], axis=0).astype(BF16)


def _log_sigmoid(x):
    return jnp.minimum(x, 0.0) - jnp.log1p(jnp.exp(-jnp.abs(x)))


def _chunk_masks(chunk_shift):
    s = lax.broadcasted_iota(jnp.int32, (GROUP, GROUP), 0)
    t = lax.broadcasted_iota(jnp.int32, (GROUP, GROUP), 1)
    same = (s >> chunk_shift) == (t >> chunk_shift)
    return same, same & (s <= t)


def _group_gates(gt, same, causal):
    cm_bf = jnp.where(causal, 1.0, 0.0).astype(BF16)
    lf = _log_sigmoid(gt)
    nr = gt.shape[0]
    bt3 = _dot(_split3_rows(lf), cm_bf)
    bt = (bt3[0:nr] + bt3[nr:2 * nr]) + bt3[2 * nr:3 * nr]
    b4 = bt[M_HEADS:2 * M_HEADS]
    a4 = gt[0:M_HEADS] - b4
    a8 = jnp.concatenate([a4, a4], axis=0)
    a_cols = jnp.concatenate([a8, jnp.zeros((GROUP - SUBLANES, GROUP), F32)], axis=0).T
    at_mats, run_rows, chunk_rows = [], [], []
    for h in range(M_HEADS):
        at = jnp.broadcast_to(a_cols[:, h:h + 1], (GROUP, GROUP))
        at_mats.append(at)
        run_rows.append(jnp.max(jnp.where(causal, at, -jnp.inf), axis=0, keepdims=True))
        chunk_rows.append(jnp.max(jnp.where(same, at, -jnp.inf), axis=0, keepdims=True))
    return b4, a4, at_mats, jnp.concatenate(run_rows, axis=0), jnp.concatenate(chunk_rows, axis=0)


def _group_weights(m_prev, b4, a4, run4, chunk4):
    big_m = jnp.maximum(m_prev, run4)
    m_last = jnp.maximum(m_prev, chunk4)
    w_inter = jnp.exp(m_prev - big_m)
    g_vec = jnp.exp(m_prev - m_last)
    w_last = jnp.exp(a4 - m_last)
    m_t = b4 + big_m
    return big_m, w_inter, g_vec, w_last, m_t, jnp.exp(-m_t)


def _group_scores(qvot, ks):
    lane_head = lax.broadcasted_iota(jnp.int32, (1, M_QK_W), 1) >> DK_SHIFT
    row_head = lax.broadcasted_iota(jnp.int32, (M_QK_W, 1), 0) >> DK_SHIFT
    qt = qvot(0, M_QK_W)
    k_stack = jnp.concatenate([jnp.where(lane_head == h, ks, jnp.zeros_like(ks)) for h in range(M_HEADS)], axis=0)
    qw = jnp.concatenate([jnp.where(row_head == h, qt, jnp.zeros_like(qt)) for h in range(M_HEADS)], axis=1)
    zero_blk = jnp.zeros((M_DK, GROUP), BF16)
    sc_t = []
    for h in range(0, M_HEADS, 2):
        q_pair = jnp.concatenate(
            [jnp.concatenate([qvot(h * M_DK, (h + 1) * M_DK), zero_blk], axis=1),
             jnp.concatenate([zero_blk, qvot((h + 1) * M_DK, (h + 2) * M_DK)], axis=1)], axis=0)
        sc = _dot(ks[:, h * M_DK:(h + 2) * M_DK], q_pair)
        sc_t += [sc[:, 0:GROUP], sc[:, GROUP:]]
    return k_stack, qw, sc_t


def _group_values(qvot, sc_t, at_mats, big_m, causal):
    ones_rows = jnp.where(lax.broadcasted_iota(jnp.int32, (BF16_ROWS, GROUP), 0) == 0, 1.0, 0.0).astype(BF16)
    zero_blk = jnp.zeros((GROUP, GROUP), BF16)
    vta, s_t = [], []
    for h in range(M_HEADS):
        w_t = jnp.where(causal, jnp.exp(at_mats[h] - big_m[h:h + 1]), 0.0)
        s_t.append((sc_t[h] * w_t).astype(BF16))
        vta.append(jnp.concatenate([qvot(M_QK_W + h * M_DV, M_QK_W + (h + 1) * M_DV), ones_rows], axis=0))
    intra = []
    for h in range(0, M_HEADS, 2):
        pair = jnp.concatenate([jnp.concatenate([s_t[h], zero_blk], axis=1),
                                jnp.concatenate([zero_blk, s_t[h + 1]], axis=1)], axis=0)
        intra.append(_dot(jnp.concatenate(vta[h:h + 2], axis=1), pair))
    return vta, jnp.concatenate(intra, axis=1)


def _weighted_values(vta, w_rows):
    return jnp.concatenate([(vta[h].astype(F32) * w_rows[h:h + 1]).astype(BF16) for h in range(M_HEADS)], axis=1)


def _lanes_x(rows):
    return jnp.concatenate([rows[h:h + 1] for h in range(M_HEADS)], axis=1)


def _group_out(qvot, inter, intra, w_inter, e_neg_m, nw_ref):
    outs = []
    out_all = inter * _lanes_x(w_inter) + intra
    for h in range(M_HEADS):
        out_t = out_all[:, h * GROUP:(h + 1) * GROUP]
        hh = out_t[0:M_DV] / jnp.maximum(jnp.abs(out_t[M_DV:M_DV + 1]), e_neg_m[h:h + 1])
        ms = jnp.mean(hh * hh, axis=0, keepdims=True)
        hn = (hh * lax.rsqrt(ms + EPS)) * nw_ref[h * M_DV:(h + 1) * M_DV]
        o_t = qvot(M_QK_W + M_V_W + h * M_DV, M_QK_W + M_V_W + (h + 1) * M_DV)
        outs.append(hn * jax.nn.sigmoid(o_t.astype(F32)))
    return jnp.concatenate(outs, axis=0).T


def _decay_row(g_vec, lane0):
    lane_head = lax.broadcasted_iota(jnp.int32, (1, M_QK_W), 1) >> DK_SHIFT
    g_row = jnp.zeros((1, M_QK_W), F32)
    for h in range(M_HEADS):
        g_row = jnp.where(lane_head == h, g_vec[h:h + 1, lane0:lane0 + 1], g_row)
    return g_row


def _mlstm_sample_kernel(seq_len, qvot_ref, gt_ref, k_ref, mrow_ref, bias_ref, nw_ref, c_ref, n_ref,
                         h_ref, c_out_ref, n_out_ref, mt_ref):
    n_seq = GROUP // seq_len
    shift = seq_len.bit_length() - 1
    same, causal = _chunk_masks(shift)
    lane_seq = lax.broadcasted_iota(jnp.int32, (1, GROUP), 1) >> shift
    lane_seq_x = jnp.concatenate([lane_seq] * M_HEADS, axis=1)
    row0 = lax.broadcasted_iota(jnp.int32, (BF16_ROWS, M_QK_W), 0) == 0
    qvot = lambda r0, r1: qvot_ref[0, r0:r1, :]
    ks = k_ref[...] * (M_DK ** -0.5)
    b4, a4, at_mats, run4, chunk4 = _group_gates(gt_ref[0] + bias_ref[...], same, causal)
    big_m, w_inter, g_vec, w_last, m_t, e_neg_m = _group_weights(mrow_ref[0, 0:M_HEADS, :], b4, a4, run4, chunk4)
    mt_ref[0] = jnp.concatenate([m_t, m_t], axis=0)
    k_stack, qw, sc_t = _group_scores(qvot, ks)

    st_old = []
    for s in range(n_seq):
        n_rows = jnp.where(row0, jnp.broadcast_to(n_ref[s:s + 1, :], (BF16_ROWS, M_QK_W)), 0.0)
        st_old.append(jnp.concatenate([c_ref[s].T, n_rows], axis=0))
    inter_all = _dot(jnp.concatenate(st_old, axis=0).astype(BF16), qw)
    vta, intra = _group_values(qvot, sc_t, at_mats, big_m, causal)
    inter = inter_all[0:S_ROWS]
    for s in range(1, n_seq):
        inter = jnp.where(lane_seq_x == s, inter_all[s * S_ROWS:(s + 1) * S_ROWS], inter)
    tall = jnp.concatenate([_weighted_values(vta, jnp.where(lane_seq == s, w_last, 0.0)) for s in range(n_seq)],
                           axis=0)
    d_st = _dot(tall, k_stack)
    for s in range(n_seq):
        st_new = _decay_row(g_vec, s * seq_len) * st_old[s] + d_st[s * S_ROWS:(s + 1) * S_ROWS]
        c_out_ref[s] = st_new[0:M_DV].T
        n_out_ref[s:s + 1, :] = st_new[M_DV:M_DV + 1]
    h_ref[...] = _group_out(qvot, inter, intra, w_inter, e_neg_m, nw_ref).astype(h_ref.dtype)


def _mlstm_sample(qvot, gt, k_m, mrow, bias_col, nw_col, c, n, seq_len):
    ngrp = qvot.shape[0]
    n_seq = GROUP // seq_len
    full = lambda a: pl.BlockSpec(a.shape, lambda i: (0,) * a.ndim)
    slab = lambda a: pl.BlockSpec((1,) + a.shape[1:], lambda i: (i, 0, 0))
    row = lambda w: pl.BlockSpec((GROUP, w), lambda i: (i, 0))
    c_spec = pl.BlockSpec((n_seq,) + c.shape[1:], lambda i: (i, 0, 0))
    n_spec = pl.BlockSpec((n_seq, n.shape[1]), lambda i: (i, 0))
    return pl.pallas_call(
        functools.partial(_mlstm_sample_kernel, seq_len),
        grid=(ngrp,),
        in_specs=[slab(qvot), slab(gt), row(M_QK_W), slab(mrow), full(bias_col), full(nw_col), c_spec, n_spec],
        out_specs=[row(M_V_W), c_spec, n_spec, slab(mrow)],
        out_shape=[jax.ShapeDtypeStruct((ngrp * GROUP, M_V_W), BF16),
                   jax.ShapeDtypeStruct(c.shape, F32),
                   jax.ShapeDtypeStruct(n.shape, F32),
                   jax.ShapeDtypeStruct(mrow.shape, F32)],
        compiler_params=_params(("arbitrary",)),
        name="mlstm_sample",
    )(qvot, gt, k_m, mrow, bias_col, nw_col, c, n)


def _swa_scores(qbs, has_prev, sink_ref, qt_ref, kv_ref, kvp_ref):
    hd = A_HEAD_DIM
    nq = A_GROUPS * WINDOW
    si = lax.broadcasted_iota(jnp.int32, (2 * WINDOW, nq), 0)
    qi = lax.broadcasted_iota(jnp.int32, (2 * WINDOW, nq), 1) & (WINDOW - 1)
    local = ((si < WINDOW) & (si > qi)) | ((si >= WINDOW) & (si - WINDOW <= qi))
    first = local & (has_prev | (si >= WINDOW))
    lane_grp = lax.broadcasted_iota(jnp.int32, (1, nq), 1) >> WINDOW_SHIFT
    kv_block = lambda i: kvp_ref[...] if i == 0 else kv_ref[(i - 1) * WINDOW:i * WINDOW, :]
    k_bf = lambda i: kv_block(i)[:, 0:A_KV_W].astype(BF16)
    vt_bf = lambda i: kv_block(i)[:, A_KV_W:].T.astype(BF16)
    zeros = jnp.zeros((hd, nq), BF16)
    sinks = []
    for kvh in range(A_KV_HEADS):
        sk = jnp.zeros((1, nq), F32)
        for g in range(A_GROUPS):
            sk = jnp.where(lane_grp == g, sink_ref[kvh * A_GROUPS + g], sk)
        sinks.append(sk)
    units = []
    for qb in qbs:
        kk = jnp.concatenate([k_bf(qb), k_bf(qb + 1)], axis=0)
        vt = jnp.concatenate([vt_bf(qb), vt_bf(qb + 1)], axis=1)
        mask = local if qb > 0 else first
        for kvh in range(A_KV_HEADS):
            q4t = jnp.concatenate(
                [qt_ref[qb, (kvh * A_GROUPS + g) * hd:(kvh * A_GROUPS + g + 1) * hd, :] for g in range(A_GROUPS)],
                axis=1)
            wq = jnp.concatenate([q4t, zeros] if kvh == 0 else [zeros, q4t], axis=0)
            units.append((jnp.where(mask, _dot(kk, wq), -jnp.inf), sinks[kvh], vt))
    return units


def _swa_finish(qbs, units, h_ref):
    hd = A_HEAD_DIM
    for i, qb in enumerate(qbs):
        pieces = []
        for kvh in range(A_KV_HEADS):
            s, sk, vt = units[i * A_KV_HEADS + kvh]
            mx = jnp.maximum(jnp.max(s, axis=0, keepdims=True), sk)
            p = jnp.exp(s - mx)
            den = jnp.sum(p, axis=0, keepdims=True) + jnp.exp(sk - mx)
            ot = _dot(vt[kvh * hd:(kvh + 1) * hd], p.astype(BF16)) / den
            pieces += [ot[:, g * WINDOW:(g + 1) * WINDOW] for g in range(A_GROUPS)]
        h_t = jnp.concatenate(pieces, axis=0)
        h_ref[qb * WINDOW:(qb + 1) * WINDOW, :] = h_t.T.astype(h_ref.dtype)


def _front_kernel(tiles_per_seq, n_cast, *refs):
    (sink_ref, x_ref, nw_ref, wmt_ref, wqgt_ref, wn_ref, bd_ref, qcol_ref, krow_ref, bias_ref, nwm_ref) = refs[:11]
    cast_in = refs[11:11 + n_cast]
    (gab_ref, hm_ref, ha_ref, kwin_ref, vwin_ref, st_ref, m_ref) = refs[11 + n_cast:18 + n_cast]
    cast_out = refs[18 + n_cast:18 + 2 * n_cast]
    set_a, set_b = refs[18 + 2 * n_cast:23 + 2 * n_cast], refs[23 + 2 * n_cast:28 + 2 * n_cast]
    st_s, m_s, kvp_s = refs[28 + 2 * n_cast:]
    odd = (pl.program_id(0) & 1) == 1
    for parity, q_set, p_set in ((jnp.logical_not(odd), set_a, set_b), (odd, set_b, set_a)):
        pl.when(parity)(functools.partial(
            _front_body, tiles_per_seq, sink_ref, x_ref, nw_ref, wmt_ref, wqgt_ref, wn_ref, bd_ref, qcol_ref,
            krow_ref, bias_ref, nwm_ref, gab_ref, hm_ref, ha_ref, kwin_ref, vwin_ref, st_ref, m_ref,
            *q_set, *p_set, st_s, m_s, kvp_s))
    for src, dst in zip(cast_in, cast_out):
        dst[...] = src[...].astype(dst.dtype)


def _front_body(tiles_per_seq, sink_ref, x_ref, nw_ref, wmt_ref, wqgt_ref, wn_ref, bd_ref, qcol_ref, krow_ref,
                bias_ref, nwm_ref,
                gab_ref, hm_ref, ha_ref, kwin_ref, vwin_ref, st_ref, m_ref,
                q_qvot, q_gt, q_qat, q_km, q_kva, p_qvot, p_gt, p_qat, p_km, p_kva, st_s, m_s, kvp_s):
    k = pl.program_id(0)

    @pl.when(k == 0)
    def _():
        for r in (p_qvot, p_gt, p_qat, p_km, p_kva, st_s, m_s, kvp_s):
            r[...] = jnp.zeros(r.shape, r.dtype)

    pieces = _proj_chunks(x_ref, nw_ref, wmt_ref, wqgt_ref, wn_ref, bd_ref, qcol_ref, krow_ref,
                          q_qvot, q_gt, q_qat, q_km, q_kva, gab_ref)

    seq_start = lax.rem(k - 1 + tiles_per_seq, tiles_per_seq) == 0
    n_blk = p_qat.shape[0]
    same, causal = _chunk_masks(GROUP.bit_length() - 1)
    st = jnp.where(seq_start, 0.0, st_s[...])
    m_col = jnp.where(seq_start, 0.0, m_s[0:M_HEADS, 0:1])

    qvots = [lambda r0, r1, g=g: p_qvot[g, r0:r1, :] for g in range(n_blk)]
    scores = [_group_scores(qvots[g], p_km[g * GROUP:(g + 1) * GROUP, :] * (M_DK ** -0.5))
              for g in range(n_blk)]
    inter = [_dot(st.astype(BF16), scores[0][1])]
    gates = [_group_gates(p_gt[g] + bias_ref[...], same, causal) for g in range(n_blk)]
    for piece in pieces[0:1]:
        piece()
    grp = []
    for g in range(n_blk):
        b4, a4, at_mats, run4, chunk4 = gates[g]
        weights = _group_weights(jnp.broadcast_to(m_col, (M_HEADS, GROUP)), b4, a4, run4, chunk4)
        m_col = weights[4][:, GROUP - 1:GROUP]
        grp.append((qvots[g], at_mats, weights) + scores[g])
    has_prev = jnp.logical_not(seq_start)
    qb_lo, qb_hi = list(range(0, n_blk // 2)), list(range(n_blk // 2, n_blk))
    units_lo = _swa_scores(qb_lo, has_prev, sink_ref, p_qat, p_kva, kvp_s)
    for piece in pieces[1:5]:
        piece()

    vals = []

    def group_values(g):
        qvot, at_mats, (big_m, _, _, w_last, _, _), k_stack, _, sc_t = grp[g]
        vta, intra = _group_values(qvot, sc_t, at_mats, big_m, causal)
        vals.append((intra, _dot(_weighted_values(vta, w_last), k_stack)))

    for g in range(0, n_blk // 2):
        group_values(g)
    _swa_finish(qb_lo[:1], units_lo[:A_KV_HEADS], ha_ref)
    for piece in pieces[5:7]:
        piece()
    for g in range(n_blk // 2, n_blk):
        group_values(g)
    _swa_finish(qb_lo[1:], units_lo[A_KV_HEADS:], ha_ref)
    units_hi = _swa_scores(qb_hi, has_prev, sink_ref, p_qat, p_kva, kvp_s)
    for piece in pieces[7:10]:
        piece()

    for g in range(n_blk):
        st = _decay_row(grp[g][2][2], 0) * st + vals[g][1]
        if g + 1 < n_blk:
            inter.append(_dot(st.astype(BF16), grp[g + 1][4]))
        if g == n_blk // 2 - 1:
            _swa_finish(qb_hi[:1], units_hi[:A_KV_HEADS], ha_ref)
            for piece in pieces[10:11]:
                piece()
    _swa_finish(qb_hi[1:], units_hi[A_KV_HEADS:], ha_ref)
    for piece in pieces[11:]:
        piece()

    for g in range(n_blk):
        qvot, _, (_, w_inter, _, _, _, e_neg_m) = grp[g][0:3]
        hm_ref[g * GROUP:(g + 1) * GROUP, :] = _group_out(qvot, inter[g], vals[g][0], w_inter, e_neg_m,
                                                          nwm_ref).astype(hm_ref.dtype)
    st_s[...] = st
    st_ref[0] = st
    m_rows = jnp.broadcast_to(m_col, (M_HEADS, LANES))
    m_rows = jnp.concatenate([m_rows, m_rows], axis=0)
    m_s[...] = m_rows
    m_ref[0] = m_rows
    tm = p_kva.shape[0]
    kwin_ref[0] = p_kva[tm - WINDOW:, 0:A_KV_W]
    vwin_ref[0] = p_kva[tm - WINDOW:, A_KV_W:]
    kvp_s[...] = p_kva[tm - WINDOW:, :]


def _cast_rows(rows, n_steps):
    rb = BF16_ROWS
    while rows % rb or rows // rb > n_steps:
        rb += BF16_ROWS
    return rb


def _front(x2d, nb, sinks, nw, wmt, wqgt, wn, bd, qcol, krow, bias_col, nwm_col, cast_ws, tm):
    n = x2d.shape[0]
    n_tiles = n // tm
    tps = n_tiles // nb
    n_blk = tm // LANES
    w_qvo = M_QK_W + 2 * M_V_W
    cur = lambda w: pl.BlockSpec((tm, w), lambda k: (jnp.minimum(k, n_tiles - 1), 0))
    prev = lambda w: pl.BlockSpec((tm, w), lambda k: (jnp.maximum(k - 1, 0), 0))
    per_seq = lambda r, w: pl.BlockSpec((1, r, w), lambda k: (jnp.maximum(k - 1, 0) // tps, 0, 0))

    def cast_spec(a):
        rb = _cast_rows(a.shape[0], n_tiles)
        return pl.BlockSpec((rb, a.shape[1]), lambda k: (jnp.minimum(k, a.shape[0] // rb - 1), 0))

    cast_in_specs = [cast_spec(a) for a in cast_ws]
    cast_out_specs = [cast_spec(a) for a in cast_ws]
    proj_scratch = [pltpu.VMEM((n_blk, w_qvo, LANES), BF16), pltpu.VMEM((n_blk, 2 * M_HEADS, LANES), F32),
                    pltpu.VMEM((n_blk, A_Q_W, LANES), BF16), pltpu.VMEM((tm, M_QK_W), BF16),
                    pltpu.VMEM((tm, 2 * A_KV_W), F32)]
    return pl.pallas_call(
        functools.partial(_front_kernel, tps, len(cast_ws)),
        grid=(n_tiles + 1,),
        in_specs=[pl.BlockSpec(memory_space=pltpu.SMEM), cur(D_MODEL)]
                 + [_const_spec(a.shape) for a in (nw, wmt, wqgt, wn, bd, qcol, krow, bias_col, nwm_col)]
                 + cast_in_specs,
        out_specs=[cur(2 * D_MODEL), prev(M_V_W), prev(A_Q_W), per_seq(WINDOW, A_KV_W), per_seq(WINDOW, A_KV_W),
                   per_seq(S_ROWS, M_QK_W), per_seq(SUBLANES, LANES)] + cast_out_specs,
        out_shape=[jax.ShapeDtypeStruct((n, 2 * D_MODEL), BF16),
                   jax.ShapeDtypeStruct((n, M_V_W), BF16),
                   jax.ShapeDtypeStruct((n, A_Q_W), BF16),
                   jax.ShapeDtypeStruct((nb, WINDOW, A_KV_W), F32),
                   jax.ShapeDtypeStruct((nb, WINDOW, A_KV_W), F32),
                   jax.ShapeDtypeStruct((nb, S_ROWS, M_QK_W), F32),
                   jax.ShapeDtypeStruct((nb, SUBLANES, LANES), F32)]
                  + [jax.ShapeDtypeStruct(a.shape, BF16) for a in cast_ws],
        scratch_shapes=proj_scratch + proj_scratch + [pltpu.VMEM((S_ROWS, M_QK_W), F32),
                                                       pltpu.VMEM((SUBLANES, LANES), F32),
                                                       pltpu.VMEM((WINDOW, 2 * A_KV_W), F32)],
        compiler_params=_params(("arbitrary",)),
        name="front",
    )(sinks, x2d, nw, wmt, wqgt, wn, bd, qcol, krow, bias_col, nwm_col, *cast_ws)


def _swa_sample_kernel(seq_len, sink_ref, qt_ref, kv_ref, ck_ref, cv_ref, h_ref, kwin_ref, vwin_ref):
    n_seq = GROUP // seq_len
    wb = ck_ref.shape[2]
    lane = lax.broadcasted_iota(jnp.int32, (1, LANES), 1)
    lo_half = lane < HALF
    q_rows = qt_ref[0].astype(F32).T
    kv_new = kv_ref[...]
    k_new = kv_new[:, 0:A_KV_W].reshape(n_seq, seq_len, A_KV_W)
    v_new = kv_new[:, A_KV_W:].reshape(n_seq, seq_len, A_KV_W)

    def to_kv_half(x, head):
        kvh = head // A_GROUPS
        if head % 2 != kvh:
            x = pltpu.roll(x, HALF, axis=1)
        return jnp.where(lo_half if kvh == 0 else ~lo_half, x, 0.0)

    lhs = jnp.concatenate(
        [to_kv_half(q_rows[:, (h // 2) * LANES:(h // 2 + 1) * LANES], h).reshape(n_seq, seq_len, LANES)
         for h in range(A_HEADS)], axis=1).astype(BF16)
    zpad = jnp.zeros((n_seq, BF16_ROWS - seq_len, A_KV_W), F32)
    k_nb = jnp.concatenate([k_new, zpad], axis=1).astype(BF16)
    v_nb = jnp.concatenate([v_new, zpad], axis=1).astype(BF16)
    s_c = jnp.einsum('sqf,sfk->sqk', lhs, ck_ref[...].astype(BF16), preferred_element_type=F32)
    s_n = jnp.einsum('sqf,skf->sqk', lhs, k_nb, preferred_element_type=F32)
    nrow = A_HEADS * seq_len
    ti = lax.broadcasted_iota(jnp.int32, (nrow, wb), 0) & (seq_len - 1)
    ki = lax.broadcasted_iota(jnp.int32, (nrow, wb), 1)
    mask_c = (ti + wb - ki) < WINDOW
    ti_n = lax.broadcasted_iota(jnp.int32, (nrow, BF16_ROWS), 0) & (seq_len - 1)
    ki_n = lax.broadcasted_iota(jnp.int32, (nrow, BF16_ROWS), 1)
    mask_n = ki_n <= ti_n
    row_head = lax.broadcasted_iota(jnp.int32, (nrow, 1), 0) >> (seq_len.bit_length() - 1)
    sk = jnp.zeros((nrow, 1), F32)
    for h in range(A_HEADS):
        sk = jnp.where(row_head == h, sink_ref[h], sk)
    s_c = jnp.where(mask_c, s_c, -jnp.inf)
    s_n = jnp.where(mask_n, s_n, -jnp.inf)
    mx = jnp.maximum(jnp.maximum(jnp.max(s_c, axis=-1, keepdims=True), jnp.max(s_n, axis=-1, keepdims=True)), sk)
    p_c = jnp.exp(s_c - mx)
    p_n = jnp.exp(s_n - mx)
    den = jnp.sum(p_c, axis=-1, keepdims=True) + jnp.sum(p_n, axis=-1, keepdims=True) + jnp.exp(sk - mx)
    o = (jnp.einsum('sqk,sfk->sqf', p_c.astype(BF16), cv_ref[...].astype(BF16), preferred_element_type=F32)
         + jnp.einsum('sqk,skf->sqf', p_n.astype(BF16), v_nb, preferred_element_type=F32)) / den

    def from_kv_half(head):
        x = o[:, head * seq_len:(head + 1) * seq_len, :].reshape(GROUP, LANES)
        return pltpu.roll(x, HALF, axis=1) if head % 2 != head // A_GROUPS else x

    for c in range(A_HEADS // 2):
        h_ref[:, c * LANES:(c + 1) * LANES] = jnp.where(lo_half, from_kv_half(2 * c),
                                                        from_kv_half(2 * c + 1)).astype(h_ref.dtype)

    kt_new, vt_new = kv_new[:, 0:A_KV_W].T, kv_new[:, A_KV_W:].T
    for s in range(n_seq):
        put = (wb - seq_len - s * seq_len) % LANES
        for new_t, c_ref, win_ref in ((kt_new, ck_ref, kwin_ref), (vt_new, cv_ref, vwin_ref)):
            win_ref[s] = jnp.where(lane >= wb - seq_len, pltpu.roll(new_t, put, axis=1) if put else new_t,
                                   pltpu.roll(c_ref[s], wb - seq_len, axis=1))


def _swa_sample(qat, kv_a, cache_k, cache_v, sinks, seq_len):
    ngrp = qat.shape[0]
    n_seq = GROUP // seq_len
    wb = cache_k.shape[2]
    assert wb == LANES, "window positions fill one lane-width"
    row = lambda w: pl.BlockSpec((GROUP, w), lambda i: (i, 0))
    cache = pl.BlockSpec((n_seq, A_KV_W, wb), lambda i: (i, 0, 0))
    return pl.pallas_call(
        functools.partial(_swa_sample_kernel, seq_len),
        grid=(ngrp,),
        in_specs=[pl.BlockSpec(memory_space=pltpu.SMEM), pl.BlockSpec((1, A_Q_W, LANES), lambda i: (i, 0, 0)),
                  row(2 * A_KV_W), cache, cache],
        out_specs=[row(A_Q_W), cache, cache],
        out_shape=[jax.ShapeDtypeStruct((ngrp * GROUP, A_Q_W), BF16),
                   jax.ShapeDtypeStruct(cache_k.shape, F32),
                   jax.ShapeDtypeStruct(cache_v.shape, F32)],
        compiler_params=_params(("arbitrary",)),
        name="swa_sample",
    )(sinks, qat, kv_a, cache_k, cache_v)


def _merge_ffn_kernel(x_ref, hm_ref, ha_ref, gab_ref, wa_ref, wb_ref, wo_ref, nw_ref, wg_ref, wu_ref, wd_ref,
                      y_ref):
    tm = x_ref.shape[0]
    halves = [slice(0, tm // 2), slice(tm // 2, tm)]
    mix = []
    for rs in halves:
        ga = jax.nn.sigmoid(gab_ref[rs, 0:D_MODEL].astype(F32))
        gb = jax.nn.sigmoid(gab_ref[rs, D_MODEL:].astype(F32))
        mix.append(ga * _dot(hm_ref[rs, :], wa_ref[...]) + gb * _dot(ha_ref[rs, :], wb_ref[...]))
    x1 = [x_ref[rs, :] + _dot(m.astype(BF16), wo_ref[...]) for rs, m in zip(halves, mix)]
    hf = [_rms_rows(v, nw_ref[...]).astype(BF16) for v in x1]
    act = [(jax.nn.silu(_dot(h, wg_ref[...])) * _dot(h, wu_ref[...])).astype(BF16) for h in hf]
    for rs, v, a in zip(halves, x1, act):
        y_ref[rs, :] = v + _dot(a, wd_ref[...])


def _merge_ffn(x2d, h_m, h_a, g_ab, wa, wb, wo, nw, wg, wu, wd, tm):
    n = x2d.shape[0]
    row = lambda w: pl.BlockSpec((tm, w), lambda i: (i, 0))
    return pl.pallas_call(
        _merge_ffn_kernel,
        grid=(n // tm,),
        in_specs=[row(D_MODEL), row(M_V_W), row(A_Q_W), row(2 * D_MODEL)]
                 + [_const_spec(w.shape) for w in (wa, wb, wo, nw, wg, wu, wd)],
        out_specs=row(D_MODEL),
        out_shape=jax.ShapeDtypeStruct((n, D_MODEL), F32),
        compiler_params=_params(("arbitrary",)),
        name="merge_ffn",
    )(x2d, h_m, h_a, g_ab, wa, wb, wo, nw, wg, wu, wd)


def kernel(x_prompt, x_sample, state_mlstm_C, state_mlstm_n, state_mlstm_m, cache_swa_k, cache_swa_v,
           norm_mix_w, w_in, mlstm_i_bias, mlstm_f_bias, mlstm_norm_w, q_norm_w, k_norm_w, attn_sinks,
           w_branch_a, w_branch_b, w_out, norm_ffn_w, w_gate, w_up, w_down):
    depth = w_in.shape[0]
    assert depth == 1, "single trunk layer"
    l = 0
    bp, tp = x_prompt.shape[0], x_prompt.shape[1]
    bs, ts = x_sample.shape[0], x_sample.shape[1]
    assert tp % TOKEN_TILE == 0 and (bs * ts) % GROUP == 0 and GROUP % ts == 0 and ts & (ts - 1) == 0
    assert ts <= SUBLANES, "sample chunk must fit one sublane tile"

    wt = jnp.transpose(w_in[l])
    c_km, c_vm = M_QK_W, 2 * M_QK_W
    c_g = 2 * M_QK_W + 2 * M_V_W
    c_qa = c_g + 2 * M_HEADS
    c_ka = c_qa + A_Q_W
    gate_pad = jnp.zeros((BF16_ROWS - 2 * M_HEADS, D_MODEL), F32)
    wmt = jnp.concatenate([wt[0:c_km], wt[c_vm:c_g]], axis=0).astype(BF16)
    wqgt = jnp.concatenate([wt[c_qa:c_ka], wt[c_g:c_qa], gate_pad], axis=0).astype(BF16)
    wn = jnp.concatenate([wt[c_km:c_vm], wt[c_ka:]], axis=0).astype(BF16)
    head_of = jnp.arange(A_KV_W) // A_HEAD_DIM
    bd = (head_of[:, None] == head_of[None, :]).astype(BF16)
    qcol = (jnp.tile(q_norm_w[l], A_HEADS) * (A_HEAD_DIM ** -0.5)).reshape(A_Q_W, 1)
    krow = jnp.tile(k_norm_w[l], A_KV_HEADS).reshape(1, A_KV_W)
    nw_mix = norm_mix_w[l].reshape(1, D_MODEL)
    nw_ffn = norm_ffn_w[l].reshape(1, D_MODEL)
    bias_col = jnp.concatenate([mlstm_i_bias[l], mlstm_f_bias[l]]).reshape(2 * M_HEADS, 1)
    nw_col = mlstm_norm_w[l].reshape(M_V_W, 1)
    sinks = attn_sinks[l]
    proj = lambda x2d, tm: _proj(x2d, nw_mix, wmt, wqgt, wn, bd, qcol, krow, tm)

    xp = x_prompt.reshape(bp * tp, D_MODEL)
    later_ws = (w_branch_a[l], w_branch_b[l], w_out[l], w_gate[l], w_up[l], w_down[l])
    g_ab, h_m, h_a, kwin_p, vwin_p, st_p, m_p, wa, wb, wo, wg, wu, wd = _front(
        xp, bp, sinks, nw_mix, wmt, wqgt, wn, bd, qcol, krow, bias_col, nw_col, later_ws, TOKEN_TILE)
    merge = lambda x2d, h_m, h_a, g_ab, tm: _merge_ffn(x2d, h_m, h_a, g_ab, wa, wb, wo, nw_ffn, wg, wu, wd, tm)
    yp = merge(xp, h_m, h_a, g_ab, TOKEN_TILE).reshape(bp, tp, D_MODEL)
    c_p = jnp.swapaxes(st_p[:, :M_DV, :], 1, 2).reshape(bp, M_HEADS, M_DK, M_DV)
    n_p = st_p[:, M_DV, :].reshape(bp, M_HEADS, M_DK)
    m_pr = m_p[:, :M_HEADS, 0]

    ns = bs * ts
    xs = x_sample.reshape(ns, D_MODEL)
    tms = TOKEN_TILE if ns % TOKEN_TILE == 0 else GROUP
    qvot, gt, qat, k_m, kv_a, g_ab = proj(xs, tms)
    ngrp = ns // GROUP
    m_lanes = jnp.repeat(state_mlstm_m[l], ts, axis=0).reshape(ngrp, GROUP, M_HEADS)
    mrow = jnp.pad(jnp.swapaxes(m_lanes, 1, 2), ((0, 0), (0, SUBLANES - M_HEADS), (0, 0)))
    h_m, c_s, n_s, mt_s = _mlstm_sample(qvot, gt, k_m, mrow, bias_col, nw_col,
                                        state_mlstm_C[l].reshape(bs, M_QK_W, M_DV),
                                        state_mlstm_n[l].reshape(bs, M_QK_W), ts)
    wbuf = cache_swa_k.shape[2]
    to_fm = lambda a: jnp.transpose(a, (0, 2, 3, 1)).reshape(bs, A_KV_W, wbuf)
    from_fm = lambda a: jnp.transpose(a.reshape(bs, A_KV_HEADS, A_HEAD_DIM, wbuf), (0, 3, 1, 2))[None]
    h_a, kwin_s, vwin_s = _swa_sample(qat, kv_a, to_fm(cache_swa_k[l]), to_fm(cache_swa_v[l]), sinks, ts)
    ys = merge(xs, h_m, h_a, g_ab, tms).reshape(bs, ts, D_MODEL)
    m_s = jnp.swapaxes(mt_s[:, :M_HEADS, :], 1, 2).reshape(bs, ts, M_HEADS)[:, ts - 1, :]

    kv5 = lambda a: a.reshape(a.shape[0], a.shape[1], A_KV_HEADS, A_HEAD_DIM)[None]
    return (yp, ys,
            c_p[None], n_p[None], m_pr[None], kv5(kwin_p), kv5(vwin_p),
            c_s.reshape(bs, M_HEADS, M_DK, M_DV)[None], n_s.reshape(bs, M_HEADS, M_DK)[None], m_s[None],
            from_fm(kwin_s), from_fm(vwin_s))
```

```python
import functools

import jax
import jax.numpy as jnp
from jax import lax
from jax.experimental import pallas as pl
from jax.experimental.pallas import tpu as pltpu

F32 = jnp.float32
BF16 = jnp.bfloat16

D_MODEL = 1024
M_HEADS = 4
M_DK = 64
M_DV = 128
M_CHUNK = 64
M_QK_W = M_HEADS * M_DK
M_V_W = M_HEADS * M_DV
A_HEADS = 8
A_KV_HEADS = 2
A_HEAD_DIM = 64
A_GROUPS = A_HEADS // A_KV_HEADS
A_Q_W = A_HEADS * A_HEAD_DIM
A_KV_W = A_KV_HEADS * A_HEAD_DIM
WINDOW = 128
D_FF = 2816
EPS = 1e-6

LANES = 128
SUBLANES = 8
BF16_ROWS = 16
GROUP = 128
S_ROWS = M_DV + BF16_ROWS
TOKEN_TILE = 512
VMEM_LIMIT = 56 * 1024 * 1024
DK_SHIFT = M_DK.bit_length() - 1
WINDOW_SHIFT = WINDOW.bit_length() - 1
HALF = LANES // 2
assert A_HEAD_DIM == HALF and A_KV_W == LANES, "attention head pairs share one lane-width"

NT_DIMS = (((1,), (1,)), ((), ()))


def _dot(a, b):
    return jnp.dot(a, b, preferred_element_type=F32)


def _dot_nt(a, b):
    return lax.dot_general(a, b, NT_DIMS, preferred_element_type=F32)


def _const_spec(shape):
    nd = len(shape)
    return pl.BlockSpec(shape, lambda *_: (0,) * nd, pipeline_mode=pl.Buffered(1))


def _params(sem):
    return pltpu.CompilerParams(dimension_semantics=sem, vmem_limit_bytes=VMEM_LIMIT)


def _rms_rows(x, nw):
    ms = jnp.mean(x * x, axis=-1, keepdims=True)
    return (x * lax.rsqrt(ms + EPS)) * nw


GAB_CHUNK = 512


def _proj_chunks(x_ref, nw_ref, wmt_ref, wqgt_ref, wn_ref, bd_ref, qcol_ref, krow_ref,
                 qvot_ref, gt_ref, qat_ref, km_ref, kva_ref, gab_ref):
    hn = _rms_rows(x_ref[...], nw_ref[...]).astype(BF16)
    n_blk = qat_ref.shape[0]

    def put(ref, rows, val):
        for c in range(n_blk):
            ref[c, rows, :] = val[:, c * LANES:(c + 1) * LANES].astype(ref.dtype)

    def mlstm_qvo():
        put(qvot_ref, slice(None), _dot_nt(wmt_ref[...], hn))

    def attn_q_and_gates():
        qt = _dot_nt(wqgt_ref[...], hn)
        put(gt_ref, slice(None), qt[A_Q_W:A_Q_W + 2 * M_HEADS])
        for h in range(A_HEADS):
            hs = slice(h * A_HEAD_DIM, (h + 1) * A_HEAD_DIM)
            blk = qt[hs]
            ssq_q = jnp.sum(blk * blk, axis=0, keepdims=True)
            put(qat_ref, hs, (blk * lax.rsqrt(ssq_q * (1.0 / A_HEAD_DIM) + EPS)) * qcol_ref[hs])

    def mlstm_k():
        km_ref[...] = _dot_nt(hn, wn_ref[0:M_QK_W, :]).astype(km_ref.dtype)

    def branch_gates(c0):
        def run():
            w0 = M_QK_W + 2 * A_KV_W + c0
            gab_ref[:, c0:c0 + GAB_CHUNK] = _dot_nt(hn, wn_ref[w0:w0 + GAB_CHUNK, :]).astype(gab_ref.dtype)
        return run

    def attn_kv():
        kv = _dot_nt(hn, wn_ref[M_QK_W:M_QK_W + 2 * A_KV_W, :])
        k = kv[:, 0:A_KV_W]
        ksq = k * k
        hi = ksq.astype(BF16)
        lo = (ksq - hi.astype(F32)).astype(BF16)
        ssq = _dot(hi, bd_ref[...]) + _dot(lo, bd_ref[...])
        kva_ref[:, 0:A_KV_W] = (k * lax.rsqrt(ssq * (1.0 / A_HEAD_DIM) + EPS)) * krow_ref[...]
        kva_ref[:, A_KV_W:] = kv[:, A_KV_W:]

    return ([mlstm_qvo, attn_q_and_gates, mlstm_k]
            + [branch_gates(c0) for c0 in range(0, gab_ref.shape[1], GAB_CHUNK)] + [attn_kv])


def _proj_kernel(*refs):
    for piece in _proj_chunks(*refs):
        piece()


def _proj(x2d, nw, wmt, wqgt, wn, bd, qcol, krow, tm):
    n = x2d.shape[0]
    row = lambda w: pl.BlockSpec((tm, w), lambda i: (i, 0))
    n_blk = tm // LANES
    slab = lambda r: pl.BlockSpec((n_blk, r, LANES), lambda i: (i, 0, 0))
    w_qvo = M_QK_W + 2 * M_V_W
    return pl.pallas_call(
        _proj_kernel,
        grid=(n // tm,),
        in_specs=[row(D_MODEL)] + [_const_spec(a.shape) for a in (nw, wmt, wqgt, wn, bd, qcol, krow)],
        out_specs=[slab(w_qvo), slab(2 * M_HEADS), slab(A_Q_W), row(M_QK_W), row(2 * A_KV_W), row(2 * D_MODEL)],
        out_shape=[jax.ShapeDtypeStruct((n // LANES, w_qvo, LANES), BF16),
                   jax.ShapeDtypeStruct((n // LANES, 2 * M_HEADS, LANES), F32),
                   jax.ShapeDtypeStruct((n // LANES, A_Q_W, LANES), BF16),
                   jax.ShapeDtypeStruct((n, M_QK_W), BF16),
                   jax.ShapeDtypeStruct((n, 2 * A_KV_W), F32),
                   jax.ShapeDtypeStruct((n, 2 * D_MODEL), BF16)],
        compiler_params=_params(("arbitrary",)),
        name="proj",
    )(x2d, nw, wmt, wqgt, wn, bd, qcol, krow)


def _split3_rows(x):
    hi = x.astype(BF16).astype(F32)
    r1 = x - hi
    mid = r1.astype(BF16).astype(F32)
    lo = r1 - mid
    return jnp.concatenate([hi, mid, lo], axis=0).astype(BF16)


def _log_sigmoid(x):
    return jnp.minimum(x, 0.0) - jnp.log1p(jnp.exp(-jnp.abs(x)))


def _chunk_masks(chunk_shift):
    s = lax.broadcasted_iota(jnp.int32, (GROUP, GROUP), 0)
    t = lax.broadcasted_iota(jnp.int32, (GROUP, GROUP), 1)
    same = (s >> chunk_shift) == (t >> chunk_shift)
    return same, same & (s <= t)


def _groups_gates(gts, same, causal):
    n = len(gts)
    assert 2 * M_HEADS == SUBLANES and n * SUBLANES <= GROUP
    cm_bf = jnp.where(causal, 1.0, 0.0).astype(BF16)
    lf = _log_sigmoid(jnp.concatenate(gts, axis=0))
    nr = n * SUBLANES
    bt3 = _dot(_split3_rows(lf), cm_bf)
    bt = (bt3[0:nr] + bt3[nr:2 * nr]) + bt3[2 * nr:3 * nr]
    b4 = [bt[g * SUBLANES + M_HEADS:(g + 1) * SUBLANES] for g in range(n)]
    a4 = [gts[g][0:M_HEADS] - b4[g] for g in range(n)]
    a_rows = [a for g in range(n) for a in (a4[g], a4[g])]
    if nr < GROUP:
        a_rows.append(jnp.zeros((GROUP - nr, GROUP), F32))
    a_cols = jnp.concatenate(a_rows, axis=0).T
    out = []
    for g in range(n):
        at_mats, run_rows, chunk_rows = [], [], []
        for h in range(M_HEADS):
            c = g * SUBLANES + h
            at = jnp.broadcast_to(a_cols[:, c:c + 1], (GROUP, GROUP))
            at_mats.append(at)
            run_rows.append(jnp.max(jnp.where(causal, at, -jnp.inf), axis=0, keepdims=True))
            chunk_rows.append(jnp.max(jnp.where(same, at, -jnp.inf), axis=0, keepdims=True))
        out.append((b4[g], a4[g], at_mats, jnp.concatenate(run_rows, axis=0), jnp.concatenate(chunk_rows, axis=0)))
    return out


def _group_weights(m_prev, b4, a4, run4, chunk4):
    big_m = jnp.maximum(m_prev, run4)
    m_last = jnp.maximum(m_prev, chunk4)
    w_inter = jnp.exp(m_prev - big_m)
    g_vec = jnp.exp(m_prev - m_last)
    w_last = jnp.exp(a4 - m_last)
    m_t = b4 + big_m
    return big_m, w_inter, g_vec, w_last, m_t, jnp.exp(-m_t)


def _group_scores(qvot, ks):
    lane_head = lax.broadcasted_iota(jnp.int32, (1, M_QK_W), 1) >> DK_SHIFT
    row_head = lax.broadcasted_iota(jnp.int32, (M_QK_W, 1), 0) >> DK_SHIFT
    qt = qvot(0, M_QK_W)
    k_stack = jnp.concatenate([jnp.where(lane_head == h, ks, jnp.zeros_like(ks)) for h in range(M_HEADS)], axis=0)
    qw = jnp.concatenate([jnp.where(row_head == h, qt, jnp.zeros_like(qt)) for h in range(M_HEADS)], axis=1)
    zero_blk = jnp.zeros((M_DK, GROUP), BF16)
    sc_t = []
    for h in range(0, M_HEADS, 2):
        q_pair = jnp.concatenate(
            [jnp.concatenate([qvot(h * M_DK, (h + 1) * M_DK), zero_blk], axis=1),
             jnp.concatenate([zero_blk, qvot((h + 1) * M_DK, (h + 2) * M_DK)], axis=1)], axis=0)
        sc = _dot(ks[:, h * M_DK:(h + 2) * M_DK], q_pair)
        sc_t += [sc[:, 0:GROUP], sc[:, GROUP:]]
    return k_stack, qw, sc_t


def _group_values(qvot, sc_t, at_mats, big_m, causal):
    ones_rows = jnp.where(lax.broadcasted_iota(jnp.int32, (BF16_ROWS, GROUP), 0) == 0, 1.0, 0.0).astype(BF16)
    zero_blk = jnp.zeros((GROUP, GROUP), BF16)
    vta, s_t = [], []
    for h in range(M_HEADS):
        w_t = jnp.where(causal, jnp.exp(at_mats[h] - big_m[h:h + 1]), 0.0)
        s_t.append((sc_t[h] * w_t).astype(BF16))
        vta.append(jnp.concatenate([qvot(M_QK_W + h * M_DV, M_QK_W + (h + 1) * M_DV), ones_rows], axis=0))
    intra = []
    for h in range(0, M_HEADS, 2):
        pair = jnp.concatenate([jnp.concatenate([s_t[h], zero_blk], axis=1),
                                jnp.concatenate([zero_blk, s_t[h + 1]], axis=1)], axis=0)
        intra.append(_dot(jnp.concatenate(vta[h:h + 2], axis=1), pair))
    return vta, jnp.concatenate(intra, axis=1)


def _weighted_values(vta, w_rows):
    return jnp.concatenate([(vta[h].astype(F32) * w_rows[h:h + 1]).astype(BF16) for h in range(M_HEADS)], axis=1)


def _lanes_x(rows):
    return jnp.concatenate([rows[h:h + 1] for h in range(M_HEADS)], axis=1)


def _group_out(qvot, inter, intra, w_inter, e_neg_m, nw_ref):
    outs = []
    out_all = inter * _lanes_x(w_inter) + intra
    for h in range(M_HEADS):
        out_t = out_all[:, h * GROUP:(h + 1) * GROUP]
        hh = out_t[0:M_DV] / jnp.maximum(jnp.abs(out_t[M_DV:M_DV + 1]), e_neg_m[h:h + 1])
        ms = jnp.mean(hh * hh, axis=0, keepdims=True)
        hn = (hh * lax.rsqrt(ms + EPS)) * nw_ref[h * M_DV:(h + 1) * M_DV]
        o_t = qvot(M_QK_W + M_V_W + h * M_DV, M_QK_W + M_V_W + (h + 1) * M_DV)
        outs.append(hn * jax.nn.sigmoid(o_t.astype(F32)))
    return jnp.concatenate(outs, axis=0).T


def _decay_row(g_vec, lane0):
    lane_head = lax.broadcasted_iota(jnp.int32, (1, M_QK_W), 1) >> DK_SHIFT
    g_row = jnp.zeros((1, M_QK_W), F32)
    for h in range(M_HEADS):
        g_row = jnp.where(lane_head == h, g_vec[h:h + 1, lane0:lane0 + 1], g_row)
    return g_row


def _mlstm_sample_kernel(seq_len, qvot_ref, gt_ref, k_ref, mrow_ref, bias_ref, nw_ref, c_ref, n_ref,
                         h_ref, c_out_ref, n_out_ref, mt_ref):
    n_seq = GROUP // seq_len
    shift = seq_len.bit_length() - 1
    same, causal = _chunk_masks(shift)
    lane_seq = lax.broadcasted_iota(jnp.int32, (1, GROUP), 1) >> shift
    lane_seq_x = jnp.concatenate([lane_seq] * M_HEADS, axis=1)
    row0 = lax.broadcasted_iota(jnp.int32, (BF16_ROWS, M_QK_W), 0) == 0
    qvot = lambda r0, r1: qvot_ref[0, r0:r1, :]
    ks = k_ref[...] * (M_DK ** -0.5)
    (b4, a4, at_mats, run4, chunk4), = _groups_gates([gt_ref[0] + bias_ref[...]], same, causal)
    big_m, w_inter, g_vec, w_last, m_t, e_neg_m = _group_weights(mrow_ref[0, 0:M_HEADS, :], b4, a4, run4, chunk4)
    mt_ref[0] = jnp.concatenate([m_t, m_t], axis=0)
    k_stack, qw, sc_t = _group_scores(qvot, ks)

    st_old = []
    for s in range(n_seq):
        n_rows = jnp.where(row0, jnp.broadcast_to(n_ref[s:s + 1, :], (BF16_ROWS, M_QK_W)), 0.0)
        st_old.append(jnp.concatenate([c_ref[s].T, n_rows], axis=0))
    inter_all = _dot(jnp.concatenate(st_old, axis=0).astype(BF16), qw)
    vta, intra = _group_values(qvot, sc_t, at_mats, big_m, causal)
    inter = inter_all[0:S_ROWS]
    for s in range(1, n_seq):
        inter = jnp.where(lane_seq_x == s, inter_all[s * S_ROWS:(s + 1) * S_ROWS], inter)
    tall = jnp.concatenate([_weighted_values(vta, jnp.where(lane_seq == s, w_last, 0.0)) for s in range(n_seq)],
                           axis=0)
    d_st = _dot(tall, k_stack)
    for s in range(n_seq):
        st_new = _decay_row(g_vec, s * seq_len) * st_old[s] + d_st[s * S_ROWS:(s + 1) * S_ROWS]
        c_out_ref[s] = st_new[0:M_DV].T
        n_out_ref[s:s + 1, :] = st_new[M_DV:M_DV + 1]
    h_ref[...] = _group_out(qvot, inter, intra, w_inter, e_neg_m, nw_ref).astype(h_ref.dtype)


def _mlstm_sample(qvot, gt, k_m, mrow, bias_col, nw_col, c, n, seq_len):
    ngrp = qvot.shape[0]
    n_seq = GROUP // seq_len
    full = lambda a: pl.BlockSpec(a.shape, lambda i: (0,) * a.ndim)
    slab = lambda a: pl.BlockSpec((1,) + a.shape[1:], lambda i: (i, 0, 0))
    row = lambda w: pl.BlockSpec((GROUP, w), lambda i: (i, 0))
    c_spec = pl.BlockSpec((n_seq,) + c.shape[1:], lambda i: (i, 0, 0))
    n_spec = pl.BlockSpec((n_seq, n.shape[1]), lambda i: (i, 0))
    return pl.pallas_call(
        functools.partial(_mlstm_sample_kernel, seq_len),
        grid=(ngrp,),
        in_specs=[slab(qvot), slab(gt), row(M_QK_W), slab(mrow), full(bias_col), full(nw_col), c_spec, n_spec],
        out_specs=[row(M_V_W), c_spec, n_spec, slab(mrow)],
        out_shape=[jax.ShapeDtypeStruct((ngrp * GROUP, M_V_W), BF16),
                   jax.ShapeDtypeStruct(c.shape, F32),
                   jax.ShapeDtypeStruct(n.shape, F32),
                   jax.ShapeDtypeStruct(mrow.shape, F32)],
        compiler_params=_params(("arbitrary",)),
        name="mlstm_sample",
    )(qvot, gt, k_m, mrow, bias_col, nw_col, c, n)


def _swa_scores(qbs, has_prev, sink_ref, qt_ref, kv_ref, kvp_ref):
    hd = A_HEAD_DIM
    nq = A_GROUPS * WINDOW
    si = lax.broadcasted_iota(jnp.int32, (2 * WINDOW, nq), 0)
    qi = lax.broadcasted_iota(jnp.int32, (2 * WINDOW, nq), 1) & (WINDOW - 1)
    local = ((si < WINDOW) & (si > qi)) | ((si >= WINDOW) & (si - WINDOW <= qi))
    first = local & (has_prev | (si >= WINDOW))
    lane_grp = lax.broadcasted_iota(jnp.int32, (1, nq), 1) >> WINDOW_SHIFT
    kv_block = lambda i: kvp_ref[...] if i == 0 else kv_ref[(i - 1) * WINDOW:i * WINDOW, :]
    k_bf = lambda i: kv_block(i)[:, 0:A_KV_W].astype(BF16)
    vt_bf = lambda i: kv_block(i)[:, A_KV_W:].T.astype(BF16)
    zeros = jnp.zeros((hd, nq), BF16)
    sinks = []
    for kvh in range(A_KV_HEADS):
        sk = jnp.zeros((1, nq), F32)
        for g in range(A_GROUPS):
            sk = jnp.where(lane_grp == g, sink_ref[kvh * A_GROUPS + g], sk)
        sinks.append(sk)
    units = []
    for qb in qbs:
        kk = jnp.concatenate([k_bf(qb), k_bf(qb + 1)], axis=0)
        vt = jnp.concatenate([vt_bf(qb), vt_bf(qb + 1)], axis=1)
        mask = local if qb > 0 else first
        for kvh in range(A_KV_HEADS):
            q4t = jnp.concatenate(
                [qt_ref[qb, (kvh * A_GROUPS + g) * hd:(kvh * A_GROUPS + g + 1) * hd, :] for g in range(A_GROUPS)],
                axis=1)
            wq = jnp.concatenate([q4t, zeros] if kvh == 0 else [zeros, q4t], axis=0)
            units.append((jnp.where(mask, _dot(kk, wq), -jnp.inf), sinks[kvh], vt))
    return units


def _swa_finish(qbs, units, h_ref):
    hd = A_HEAD_DIM
    for i, qb in enumerate(qbs):
        pieces = []
        for kvh in range(A_KV_HEADS):
            s, sk, vt = units[i * A_KV_HEADS + kvh]
            mx = jnp.maximum(jnp.max(s, axis=0, keepdims=True), sk)
            p = jnp.exp(s - mx)
            den = jnp.sum(p, axis=0, keepdims=True) + jnp.exp(sk - mx)
            ot = _dot(vt[kvh * hd:(kvh + 1) * hd], p.astype(BF16)) / den
            pieces += [ot[:, g * WINDOW:(g + 1) * WINDOW] for g in range(A_GROUPS)]
        h_t = jnp.concatenate(pieces, axis=0)
        h_ref[qb * WINDOW:(qb + 1) * WINDOW, :] = h_t.T.astype(h_ref.dtype)


def _front_kernel(tiles_per_seq, n_cast, *refs):
    (sink_ref, x_ref, nw_ref, wmt_ref, wqgt_ref, wn_ref, bd_ref, qcol_ref, krow_ref, bias_ref, nwm_ref) = refs[:11]
    cast_in = refs[11:11 + n_cast]
    (gab_ref, hm_ref, ha_ref, kwin_ref, vwin_ref, st_ref, m_ref) = refs[11 + n_cast:18 + n_cast]
    cast_out = refs[18 + n_cast:18 + 2 * n_cast]
    set_a, set_b = refs[18 + 2 * n_cast:23 + 2 * n_cast], refs[23 + 2 * n_cast:28 + 2 * n_cast]
    st_s, m_s, kvp_s = refs[28 + 2 * n_cast:]
    odd = (pl.program_id(0) & 1) == 1
    for parity, q_set, p_set in ((jnp.logical_not(odd), set_a, set_b), (odd, set_b, set_a)):
        pl.when(parity)(functools.partial(
            _front_body, tiles_per_seq, sink_ref, x_ref, nw_ref, wmt_ref, wqgt_ref, wn_ref, bd_ref, qcol_ref,
            krow_ref, bias_ref, nwm_ref, gab_ref, hm_ref, ha_ref, kwin_ref, vwin_ref, st_ref, m_ref,
            *q_set, *p_set, st_s, m_s, kvp_s))
    for src, dst in zip(cast_in, cast_out):
        dst[...] = src[...].astype(dst.dtype)


def _front_body(tiles_per_seq, sink_ref, x_ref, nw_ref, wmt_ref, wqgt_ref, wn_ref, bd_ref, qcol_ref, krow_ref,
                bias_ref, nwm_ref,
                gab_ref, hm_ref, ha_ref, kwin_ref, vwin_ref, st_ref, m_ref,
                q_qvot, q_gt, q_qat, q_km, q_kva, p_qvot, p_gt, p_qat, p_km, p_kva, st_s, m_s, kvp_s):
    k = pl.program_id(0)

    @pl.when(k == 0)
    def _():
        for r in (p_qvot, p_gt, p_qat, p_km, p_kva, st_s, m_s, kvp_s):
            r[...] = jnp.zeros(r.shape, r.dtype)

    pieces = _proj_chunks(x_ref, nw_ref, wmt_ref, wqgt_ref, wn_ref, bd_ref, qcol_ref, krow_ref,
                          q_qvot, q_gt, q_qat, q_km, q_kva, gab_ref)

    seq_start = lax.rem(k - 1 + tiles_per_seq, tiles_per_seq) == 0
    n_blk = p_qat.shape[0]
    same, causal = _chunk_masks(GROUP.bit_length() - 1)
    st = jnp.where(seq_start, 0.0, st_s[...])
    m_col = jnp.where(seq_start, 0.0, m_s[0:M_HEADS, 0:1])

    qvots = [lambda r0, r1, g=g: p_qvot[g, r0:r1, :] for g in range(n_blk)]
    scores = [_group_scores(qvots[g], p_km[g * GROUP:(g + 1) * GROUP, :] * (M_DK ** -0.5))
              for g in range(n_blk)]
    inter = [_dot(st.astype(BF16), scores[0][1])]
    gates = _groups_gates([p_gt[g] + bias_ref[...] for g in range(n_blk)], same, causal)
    for piece in pieces[0:1]:
        piece()
    grp = []
    for g in range(n_blk):
        b4, a4, at_mats, run4, chunk4 = gates[g]
        weights = _group_weights(jnp.broadcast_to(m_col, (M_HEADS, GROUP)), b4, a4, run4, chunk4)
        m_col = weights[4][:, GROUP - 1:GROUP]
        grp.append((qvots[g], at_mats, weights) + scores[g])
    has_prev = jnp.logical_not(seq_start)
    qb_lo, qb_hi = list(range(0, n_blk // 2)), list(range(n_blk // 2, n_blk))
    units_lo = _swa_scores(qb_lo, has_prev, sink_ref, p_qat, p_kva, kvp_s)
    for piece in pieces[1:3]:
        piece()

    vals = []

    def group_values(g):
        qvot, at_mats, (big_m, _, _, w_last, _, _), k_stack, _, sc_t = grp[g]
        vta, intra = _group_values(qvot, sc_t, at_mats, big_m, causal)
        vals.append((intra, _dot(_weighted_values(vta, w_last), k_stack)))

    for g in range(0, n_blk // 2):
        group_values(g)
    _swa_finish(qb_lo[:1], units_lo[:A_KV_HEADS], ha_ref)
    for piece in pieces[3:4]:
        piece()
    for g in range(n_blk // 2, n_blk):
        group_values(g)
    _swa_finish(qb_lo[1:], units_lo[A_KV_HEADS:], ha_ref)
    units_hi = _swa_scores(qb_hi, has_prev, sink_ref, p_qat, p_kva, kvp_s)
    for piece in pieces[4:6]:
        piece()

    for g in range(n_blk):
        st = _decay_row(grp[g][2][2], 0) * st + vals[g][1]
        if g + 1 < n_blk:
            inter.append(_dot(st.astype(BF16), grp[g + 1][4]))
        if g == n_blk // 2 - 1:
            _swa_finish(qb_hi[:1], units_hi[:A_KV_HEADS], ha_ref)
            for piece in pieces[6:7]:
                piece()
    _swa_finish(qb_hi[1:], units_hi[A_KV_HEADS:], ha_ref)
    for piece in pieces[7:]:
        piece()

    for g in range(n_blk):
        qvot, _, (_, w_inter, _, _, _, e_neg_m) = grp[g][0:3]
        hm_ref[g * GROUP:(g + 1) * GROUP, :] = _group_out(qvot, inter[g], vals[g][0], w_inter, e_neg_m,
                                                          nwm_ref).astype(hm_ref.dtype)
    st_s[...] = st
    st_ref[0] = st
    m_rows = jnp.broadcast_to(m_col, (M_HEADS, LANES))
    m_rows = jnp.concatenate([m_rows, m_rows], axis=0)
    m_s[...] = m_rows
    m_ref[0] = m_rows
    tm = p_kva.shape[0]
    kwin_ref[0] = p_kva[tm - WINDOW:, 0:A_KV_W]
    vwin_ref[0] = p_kva[tm - WINDOW:, A_KV_W:]
    kvp_s[...] = p_kva[tm - WINDOW:, :]


def _cast_rows(rows, n_steps):
    rb = BF16_ROWS
    while rows % rb or rows // rb > n_steps:
        rb += BF16_ROWS
    return rb


def _front(x2d, nb, sinks, nw, wmt, wqgt, wn, bd, qcol, krow, bias_col, nwm_col, cast_ws, tm):
    n = x2d.shape[0]
    n_tiles = n // tm
    tps = n_tiles // nb
    n_blk = tm // LANES
    w_qvo = M_QK_W + 2 * M_V_W
    cur = lambda w: pl.BlockSpec((tm, w), lambda k: (jnp.minimum(k, n_tiles - 1), 0))
    prev = lambda w: pl.BlockSpec((tm, w), lambda k: (jnp.maximum(k - 1, 0), 0))
    per_seq = lambda r, w: pl.BlockSpec((1, r, w), lambda k: (jnp.maximum(k - 1, 0) // tps, 0, 0))

    def cast_spec(a):
        rb = _cast_rows(a.shape[0], n_tiles)
        return pl.BlockSpec((rb, a.shape[1]), lambda k: (jnp.minimum(k, a.shape[0] // rb - 1), 0))

    cast_in_specs = [cast_spec(a) for a in cast_ws]
    cast_out_specs = [cast_spec(a) for a in cast_ws]
    proj_scratch = [pltpu.VMEM((n_blk, w_qvo, LANES), BF16), pltpu.VMEM((n_blk, 2 * M_HEADS, LANES), F32),
                    pltpu.VMEM((n_blk, A_Q_W, LANES), BF16), pltpu.VMEM((tm, M_QK_W), BF16),
                    pltpu.VMEM((tm, 2 * A_KV_W), F32)]
    return pl.pallas_call(
        functools.partial(_front_kernel, tps, len(cast_ws)),
        grid=(n_tiles + 1,),
        in_specs=[pl.BlockSpec(memory_space=pltpu.SMEM), cur(D_MODEL)]
                 + [_const_spec(a.shape) for a in (nw, wmt, wqgt, wn, bd, qcol, krow, bias_col, nwm_col)]
                 + cast_in_specs,
        out_specs=[cur(2 * D_MODEL), prev(M_V_W), prev(A_Q_W), per_seq(WINDOW, A_KV_W), per_seq(WINDOW, A_KV_W),
                   per_seq(S_ROWS, M_QK_W), per_seq(SUBLANES, LANES)] + cast_out_specs,
        out_shape=[jax.ShapeDtypeStruct((n, 2 * D_MODEL), BF16),
                   jax.ShapeDtypeStruct((n, M_V_W), BF16),
                   jax.ShapeDtypeStruct((n, A_Q_W), BF16),
                   jax.ShapeDtypeStruct((nb, WINDOW, A_KV_W), F32),
                   jax.ShapeDtypeStruct((nb, WINDOW, A_KV_W), F32),
                   jax.ShapeDtypeStruct((nb, S_ROWS, M_QK_W), F32),
                   jax.ShapeDtypeStruct((nb, SUBLANES, LANES), F32)]
                  + [jax.ShapeDtypeStruct(a.shape, BF16) for a in cast_ws],
        scratch_shapes=proj_scratch + proj_scratch + [pltpu.VMEM((S_ROWS, M_QK_W), F32),
                                                       pltpu.VMEM((SUBLANES, LANES), F32),
                                                       pltpu.VMEM((WINDOW, 2 * A_KV_W), F32)],
        compiler_params=_params(("arbitrary",)),
        name="front",
    )(sinks, x2d, nw, wmt, wqgt, wn, bd, qcol, krow, bias_col, nwm_col, *cast_ws)


def _swa_sample_kernel(seq_len, sink_ref, qt_ref, kv_ref, ck_ref, cv_ref, h_ref, kwin_ref, vwin_ref):
    n_seq = GROUP // seq_len
    wb = ck_ref.shape[2]
    lane = lax.broadcasted_iota(jnp.int32, (1, LANES), 1)
    lo_half = lane < HALF
    q_rows = qt_ref[0].astype(F32).T
    kv_new = kv_ref[...]
    k_new = kv_new[:, 0:A_KV_W].reshape(n_seq, seq_len, A_KV_W)
    v_new = kv_new[:, A_KV_W:].reshape(n_seq, seq_len, A_KV_W)

    def to_kv_half(x, head):
        kvh = head // A_GROUPS
        if head % 2 != kvh:
            x = pltpu.roll(x, HALF, axis=1)
        return jnp.where(lo_half if kvh == 0 else ~lo_half, x, 0.0)

    lhs = jnp.concatenate(
        [to_kv_half(q_rows[:, (h // 2) * LANES:(h // 2 + 1) * LANES], h).reshape(n_seq, seq_len, LANES)
         for h in range(A_HEADS)], axis=1).astype(BF16)
    zpad = jnp.zeros((n_seq, BF16_ROWS - seq_len, A_KV_W), F32)
    k_nb = jnp.concatenate([k_new, zpad], axis=1).astype(BF16)
    v_nb = jnp.concatenate([v_new, zpad], axis=1).astype(BF16)
    s_c = jnp.einsum('sqf,sfk->sqk', lhs, ck_ref[...].astype(BF16), preferred_element_type=F32)
    s_n = jnp.einsum('sqf,skf->sqk', lhs, k_nb, preferred_element_type=F32)
    nrow = A_HEADS * seq_len
    ti = lax.broadcasted_iota(jnp.int32, (nrow, wb), 0) & (seq_len - 1)
    ki = lax.broadcasted_iota(jnp.int32, (nrow, wb), 1)
    mask_c = (ti + wb - ki) < WINDOW
    ti_n = lax.broadcasted_iota(jnp.int32, (nrow, BF16_ROWS), 0) & (seq_len - 1)
    ki_n = lax.broadcasted_iota(jnp.int32, (nrow, BF16_ROWS), 1)
    mask_n = ki_n <= ti_n
    row_head = lax.broadcasted_iota(jnp.int32, (nrow, 1), 0) >> (seq_len.bit_length() - 1)
    sk = jnp.zeros((nrow, 1), F32)
    for h in range(A_HEADS):
        sk = jnp.where(row_head == h, sink_ref[h], sk)
    s_c = jnp.where(mask_c, s_c, -jnp.inf)
    s_n = jnp.where(mask_n, s_n, -jnp.inf)
    mx = jnp.maximum(jnp.maximum(jnp.max(s_c, axis=-1, keepdims=True), jnp.max(s_n, axis=-1, keepdims=True)), sk)
    p_c = jnp.exp(s_c - mx)
    p_n = jnp.exp(s_n - mx)
    den = jnp.sum(p_c, axis=-1, keepdims=True) + jnp.sum(p_n, axis=-1, keepdims=True) + jnp.exp(sk - mx)
    o = (jnp.einsum('sqk,sfk->sqf', p_c.astype(BF16), cv_ref[...].astype(BF16), preferred_element_type=F32)
         + jnp.einsum('sqk,skf->sqf', p_n.astype(BF16), v_nb, preferred_element_type=F32)) / den

    def from_kv_half(head):
        x = o[:, head * seq_len:(head + 1) * seq_len, :].reshape(GROUP, LANES)
        return pltpu.roll(x, HALF, axis=1) if head % 2 != head // A_GROUPS else x

    for c in range(A_HEADS // 2):
        h_ref[:, c * LANES:(c + 1) * LANES] = jnp.where(lo_half, from_kv_half(2 * c),
                                                        from_kv_half(2 * c + 1)).astype(h_ref.dtype)

    kt_new, vt_new = kv_new[:, 0:A_KV_W].T, kv_new[:, A_KV_W:].T
    for s in range(n_seq):
        put = (wb - seq_len - s * seq_len) % LANES
        for new_t, c_ref, win_ref in ((kt_new, ck_ref, kwin_ref), (vt_new, cv_ref, vwin_ref)):
            win_ref[s] = jnp.where(lane >= wb - seq_len, pltpu.roll(new_t, put, axis=1) if put else new_t,
                                   pltpu.roll(c_ref[s], wb - seq_len, axis=1))


def _swa_sample(qat, kv_a, cache_k, cache_v, sinks, seq_len):
    ngrp = qat.shape[0]
    n_seq = GROUP // seq_len
    wb = cache_k.shape[2]
    assert wb == LANES, "window positions fill one lane-width"
    row = lambda w: pl.BlockSpec((GROUP, w), lambda i: (i, 0))
    cache = pl.BlockSpec((n_seq, A_KV_W, wb), lambda i: (i, 0, 0))
    return pl.pallas_call(
        functools.partial(_swa_sample_kernel, seq_len),
        grid=(ngrp,),
        in_specs=[pl.BlockSpec(memory_space=pltpu.SMEM), pl.BlockSpec((1, A_Q_W, LANES), lambda i: (i, 0, 0)),
                  row(2 * A_KV_W), cache, cache],
        out_specs=[row(A_Q_W), cache, cache],
        out_shape=[jax.ShapeDtypeStruct((ngrp * GROUP, A_Q_W), BF16),
                   jax.ShapeDtypeStruct(cache_k.shape, F32),
                   jax.ShapeDtypeStruct(cache_v.shape, F32)],
        compiler_params=_params(("arbitrary",)),
        name="swa_sample",
    )(sinks, qat, kv_a, cache_k, cache_v)


def _merge_ffn_kernel(x_ref, hm_ref, ha_ref, gab_ref, wa_ref, wb_ref, wo_ref, nw_ref, wg_ref, wu_ref, wd_ref,
                      y_ref):
    tm = x_ref.shape[0]
    halves = [slice(0, tm // 2), slice(tm // 2, tm)]
    mix = []
    for rs in halves:
        ga = jax.nn.sigmoid(gab_ref[rs, 0:D_MODEL].astype(F32))
        gb = jax.nn.sigmoid(gab_ref[rs, D_MODEL:].astype(F32))
        mix.append(ga * _dot(hm_ref[rs, :], wa_ref[...]) + gb * _dot(ha_ref[rs, :], wb_ref[...]))
    x1 = [x_ref[rs, :] + _dot(m.astype(BF16), wo_ref[...]) for rs, m in zip(halves, mix)]
    hf = [_rms_rows(v, nw_ref[...]).astype(BF16) for v in x1]
    act = [(jax.nn.silu(_dot(h, wg_ref[...])) * _dot(h, wu_ref[...])).astype(BF16) for h in hf]
    for rs, v, a in zip(halves, x1, act):
        y_ref[rs, :] = v + _dot(a, wd_ref[...])


def _merge_ffn(x2d, h_m, h_a, g_ab, wa, wb, wo, nw, wg, wu, wd, tm):
    n = x2d.shape[0]
    row = lambda w: pl.BlockSpec((tm, w), lambda i: (i, 0))
    return pl.pallas_call(
        _merge_ffn_kernel,
        grid=(n // tm,),
        in_specs=[row(D_MODEL), row(M_V_W), row(A_Q_W), row(2 * D_MODEL)]
                 + [_const_spec(w.shape) for w in (wa, wb, wo, nw, wg, wu, wd)],
        out_specs=row(D_MODEL),
        out_shape=jax.ShapeDtypeStruct((n, D_MODEL), F32),
        compiler_params=_params(("arbitrary",)),
        name="merge_ffn",
    )(x2d, h_m, h_a, g_ab, wa, wb, wo, nw, wg, wu, wd)


def kernel(x_prompt, x_sample, state_mlstm_C, state_mlstm_n, state_mlstm_m, cache_swa_k, cache_swa_v,
           norm_mix_w, w_in, mlstm_i_bias, mlstm_f_bias, mlstm_norm_w, q_norm_w, k_norm_w, attn_sinks,
           w_branch_a, w_branch_b, w_out, norm_ffn_w, w_gate, w_up, w_down):
    depth = w_in.shape[0]
    assert depth == 1, "single trunk layer"
    l = 0
    bp, tp = x_prompt.shape[0], x_prompt.shape[1]
    bs, ts = x_sample.shape[0], x_sample.shape[1]
    assert tp % TOKEN_TILE == 0 and (bs * ts) % GROUP == 0 and GROUP % ts == 0 and ts & (ts - 1) == 0
    assert ts <= SUBLANES, "sample chunk must fit one sublane tile"

    wt = jnp.transpose(w_in[l])
    c_km, c_vm = M_QK_W, 2 * M_QK_W
    c_g = 2 * M_QK_W + 2 * M_V_W
    c_qa = c_g + 2 * M_HEADS
    c_ka = c_qa + A_Q_W
    gate_pad = jnp.zeros((BF16_ROWS - 2 * M_HEADS, D_MODEL), F32)
    wmt = jnp.concatenate([wt[0:c_km], wt[c_vm:c_g]], axis=0).astype(BF16)
    wqgt = jnp.concatenate([wt[c_qa:c_ka], wt[c_g:c_qa], gate_pad], axis=0).astype(BF16)
    wn = jnp.concatenate([wt[c_km:c_vm], wt[c_ka:]], axis=0).astype(BF16)
    head_of = jnp.arange(A_KV_W) // A_HEAD_DIM
    bd = (head_of[:, None] == head_of[None, :]).astype(BF16)
    qcol = (jnp.tile(q_norm_w[l], A_HEADS) * (A_HEAD_DIM ** -0.5)).reshape(A_Q_W, 1)
    krow = jnp.tile(k_norm_w[l], A_KV_HEADS).reshape(1, A_KV_W)
    nw_mix = norm_mix_w[l].reshape(1, D_MODEL)
    nw_ffn = norm_ffn_w[l].reshape(1, D_MODEL)
    bias_col = jnp.concatenate([mlstm_i_bias[l], mlstm_f_bias[l]]).reshape(2 * M_HEADS, 1)
    nw_col = mlstm_norm_w[l].reshape(M_V_W, 1)
    sinks = attn_sinks[l]
    proj = lambda x2d, tm: _proj(x2d, nw_mix, wmt, wqgt, wn, bd, qcol, krow, tm)

    xp = x_prompt.reshape(bp * tp, D_MODEL)
    later_ws = (w_branch_a[l], w_branch_b[l], w_out[l], w_gate[l], w_up[l], w_down[l])
    g_ab, h_m, h_a, kwin_p, vwin_p, st_p, m_p, wa, wb, wo, wg, wu, wd = _front(
        xp, bp, sinks, nw_mix, wmt, wqgt, wn, bd, qcol, krow, bias_col, nw_col, later_ws, TOKEN_TILE)
    merge = lambda x2d, h_m, h_a, g_ab, tm: _merge_ffn(x2d, h_m, h_a, g_ab, wa, wb, wo, nw_ffn, wg, wu, wd, tm)
    yp = merge(xp, h_m, h_a, g_ab, TOKEN_TILE).reshape(bp, tp, D_MODEL)
    c_p = jnp.swapaxes(st_p[:, :M_DV, :], 1, 2).reshape(bp, M_HEADS, M_DK, M_DV)
    n_p = st_p[:, M_DV, :].reshape(bp, M_HEADS, M_DK)
    m_pr = m_p[:, :M_HEADS, 0]

    ns = bs * ts
    xs = x_sample.reshape(ns, D_MODEL)
    tms = TOKEN_TILE if ns % TOKEN_TILE == 0 else GROUP
    qvot, gt, qat, k_m, kv_a, g_ab = proj(xs, tms)
    ngrp = ns // GROUP
    m_lanes = jnp.repeat(state_mlstm_m[l], ts, axis=0).reshape(ngrp, GROUP, M_HEADS)
    mrow = jnp.pad(jnp.swapaxes(m_lanes, 1, 2), ((0, 0), (0, SUBLANES - M_HEADS), (0, 0)))
    h_m, c_s, n_s, mt_s = _mlstm_sample(qvot, gt, k_m, mrow, bias_col, nw_col,
                                        state_mlstm_C[l].reshape(bs, M_QK_W, M_DV),
                                        state_mlstm_n[l].reshape(bs, M_QK_W), ts)
    wbuf = cache_swa_k.shape[2]
    to_fm = lambda a: jnp.transpose(a, (0, 2, 3, 1)).reshape(bs, A_KV_W, wbuf)
    from_fm = lambda a: jnp.transpose(a.reshape(bs, A_KV_HEADS, A_HEAD_DIM, wbuf), (0, 3, 1, 2))[None]
    h_a, kwin_s, vwin_s = _swa_sample(qat, kv_a, to_fm(cache_swa_k[l]), to_fm(cache_swa_v[l]), sinks, ts)
    ys = merge(xs, h_m, h_a, g_ab, tms).reshape(bs, ts, D_MODEL)
    m_s = jnp.swapaxes(mt_s[:, :M_HEADS, :], 1, 2).reshape(bs, ts, M_HEADS)[:, ts - 1, :]

    kv5 = lambda a: a.reshape(a.shape[0], a.shape[1], A_KV_HEADS, A_HEAD_DIM)[None]
    return (yp, ys,
            c_p[None], n_p[None], m_pr[None], kv5(kwin_p), kv5(vwin_p),
            c_s.reshape(bs, M_HEADS, M_DK, M_DV)[None], n_s.reshape(bs, M_HEADS, M_DK)[None], m_s[None],
            from_fm(kwin_s), from_fm(vwin_s))
```

```python
import functools

import jax
import jax.numpy as jnp
from jax import lax
from jax.experimental import pallas as pl
from jax.experimental.pallas import tpu as pltpu

F32 = jnp.float32
BF16 = jnp.bfloat16

D_MODEL = 1024
M_HEADS = 4
M_DK = 64
M_DV = 128
M_CHUNK = 64
M_QK_W = M_HEADS * M_DK
M_V_W = M_HEADS * M_DV
A_HEADS = 8
A_KV_HEADS = 2
A_HEAD_DIM = 64
A_GROUPS = A_HEADS // A_KV_HEADS
A_Q_W = A_HEADS * A_HEAD_DIM
A_KV_W = A_KV_HEADS * A_HEAD_DIM
WINDOW = 128
D_FF = 2816
EPS = 1e-6

LANES = 128
SUBLANES = 8
BF16_ROWS = 16
GROUP = 128
S_ROWS = M_DV + BF16_ROWS
TOKEN_TILE = 512
VMEM_LIMIT = 56 * 1024 * 1024
DK_SHIFT = M_DK.bit_length() - 1
WINDOW_SHIFT = WINDOW.bit_length() - 1
HALF = LANES // 2
assert A_HEAD_DIM == HALF and A_KV_W == LANES, "attention head pairs share one lane-width"

NT_DIMS = (((1,), (1,)), ((), ()))


def _dot(a, b):
    return jnp.dot(a, b, preferred_element_type=F32)


def _dot_nt(a, b):
    return lax.dot_general(a, b, NT_DIMS, preferred_element_type=F32)


def _const_spec(shape):
    nd = len(shape)
    return pl.BlockSpec(shape, lambda *_: (0,) * nd, pipeline_mode=pl.Buffered(1))


def _params(sem):
    return pltpu.CompilerParams(dimension_semantics=sem, vmem_limit_bytes=VMEM_LIMIT)


def _rms_rows(x, nw):
    ms = jnp.mean(x * x, axis=-1, keepdims=True)
    return (x * lax.rsqrt(ms + EPS)) * nw


GAB_CHUNK = 512


def _proj_chunks(x_ref, nw_ref, wmt_ref, wqgt_ref, wn_ref, bd_ref, qcol_ref, krow_ref,
                 qvot_ref, gt_ref, qat_ref, km_ref, kva_ref, gab_ref):
    hn = _rms_rows(x_ref[...], nw_ref[...]).astype(BF16)
    n_blk = qat_ref.shape[0]

    def put(ref, rows, val):
        for c in range(n_blk):
            ref[c, rows, :] = val[:, c * LANES:(c + 1) * LANES].astype(ref.dtype)

    def mlstm_qvo():
        put(qvot_ref, slice(None), _dot_nt(wmt_ref[...], hn))

    def attn_q_and_gates():
        qt = _dot_nt(wqgt_ref[...], hn)
        put(gt_ref, slice(None), qt[A_Q_W:A_Q_W + 2 * M_HEADS])
        for h in range(A_HEADS):
            hs = slice(h * A_HEAD_DIM, (h + 1) * A_HEAD_DIM)
            blk = qt[hs]
            ssq_q = jnp.sum(blk * blk, axis=0, keepdims=True)
            put(qat_ref, hs, (blk * lax.rsqrt(ssq_q * (1.0 / A_HEAD_DIM) + EPS)) * qcol_ref[hs])

    def mlstm_k():
        km_ref[...] = _dot_nt(hn, wn_ref[0:M_QK_W, :]).astype(km_ref.dtype)

    def branch_gates(c0):
        def run():
            w0 = M_QK_W + 2 * A_KV_W + c0
            gab_ref[:, c0:c0 + GAB_CHUNK] = _dot_nt(hn, wn_ref[w0:w0 + GAB_CHUNK, :]).astype(gab_ref.dtype)
        return run

    def attn_kv():
        kv = _dot_nt(hn, wn_ref[M_QK_W:M_QK_W + 2 * A_KV_W, :])
        k = kv[:, 0:A_KV_W]
        ksq = k * k
        hi = ksq.astype(BF16)
        lo = (ksq - hi.astype(F32)).astype(BF16)
        ssq = _dot(jnp.concatenate([hi, lo], axis=1), bd_ref[...])
        kva_ref[:, 0:A_KV_W] = (k * lax.rsqrt(ssq * (1.0 / A_HEAD_DIM) + EPS)) * krow_ref[...]
        kva_ref[:, A_KV_W:] = kv[:, A_KV_W:]

    return ([mlstm_qvo, attn_q_and_gates, mlstm_k]
            + [branch_gates(c0) for c0 in range(0, gab_ref.shape[1], GAB_CHUNK)] + [attn_kv])


def _proj_kernel(*refs):
    for piece in _proj_chunks(*refs):
        piece()


def _proj(x2d, nw, wmt, wqgt, wn, bd, qcol, krow, tm):
    n = x2d.shape[0]
    row = lambda w: pl.BlockSpec((tm, w), lambda i: (i, 0))
    n_blk = tm // LANES
    slab = lambda r: pl.BlockSpec((n_blk, r, LANES), lambda i: (i, 0, 0))
    w_qvo = M_QK_W + 2 * M_V_W
    return pl.pallas_call(
        _proj_kernel,
        grid=(n // tm,),
        in_specs=[row(D_MODEL)] + [_const_spec(a.shape) for a in (nw, wmt, wqgt, wn, bd, qcol, krow)],
        out_specs=[slab(w_qvo), slab(2 * M_HEADS), slab(A_Q_W), row(M_QK_W), row(2 * A_KV_W), row(2 * D_MODEL)],
        out_shape=[jax.ShapeDtypeStruct((n // LANES, w_qvo, LANES), BF16),
                   jax.ShapeDtypeStruct((n // LANES, 2 * M_HEADS, LANES), F32),
                   jax.ShapeDtypeStruct((n // LANES, A_Q_W, LANES), BF16),
                   jax.ShapeDtypeStruct((n, M_QK_W), BF16),
                   jax.ShapeDtypeStruct((n, 2 * A_KV_W), F32),
                   jax.ShapeDtypeStruct((n, 2 * D_MODEL), BF16)],
        compiler_params=_params(("arbitrary",)),
        name="proj",
    )(x2d, nw, wmt, wqgt, wn, bd, qcol, krow)


def _split3_rows(x):
    hi = x.astype(BF16).astype(F32)
    r1 = x - hi
    mid = r1.astype(BF16).astype(F32)
    lo = r1 - mid
    return jnp.concatenate([hi, mid, lo], axis=0).astype(BF16)


def _log_sigmoid(x):
    return jnp.minimum(x, 0.0) - jnp.log1p(jnp.exp(-jnp.abs(x)))


def _chunk_masks(chunk_shift):
    s = lax.broadcasted_iota(jnp.int32, (GROUP, GROUP), 0)
    t = lax.broadcasted_iota(jnp.int32, (GROUP, GROUP), 1)
    same = (s >> chunk_shift) == (t >> chunk_shift)
    return same, same & (s <= t)


def _groups_gates(gts, same, causal):
    n = len(gts)
    assert 2 * M_HEADS == SUBLANES and n * SUBLANES <= GROUP
    cm_bf = jnp.where(causal, 1.0, 0.0).astype(BF16)
    lf = _log_sigmoid(jnp.concatenate(gts, axis=0))
    nr = n * SUBLANES
    bt3 = _dot(_split3_rows(lf), cm_bf)
    bt = (bt3[0:nr] + bt3[nr:2 * nr]) + bt3[2 * nr:3 * nr]
    b4 = [bt[g * SUBLANES + M_HEADS:(g + 1) * SUBLANES] for g in range(n)]
    a4 = [gts[g][0:M_HEADS] - b4[g] for g in range(n)]
    a_rows = [a for g in range(n) for a in (a4[g], a4[g])]
    if nr < GROUP:
        a_rows.append(jnp.zeros((GROUP - nr, GROUP), F32))
    a_cols = jnp.concatenate(a_rows, axis=0).T
    out = []
    for g in range(n):
        at_mats, run_rows, chunk_rows = [], [], []
        for h in range(M_HEADS):
            c = g * SUBLANES + h
            at = jnp.broadcast_to(a_cols[:, c:c + 1], (GROUP, GROUP))
            at_mats.append(at)
            run_rows.append(jnp.max(jnp.where(causal, at, -jnp.inf), axis=0, keepdims=True))
            chunk_rows.append(jnp.max(jnp.where(same, at, -jnp.inf), axis=0, keepdims=True))
        out.append((b4[g], a4[g], at_mats, jnp.concatenate(run_rows, axis=0), jnp.concatenate(chunk_rows, axis=0)))
    return out


def _group_weights(m_prev, b4, a4, run4, chunk4):
    big_m = jnp.maximum(m_prev, run4)
    m_last = jnp.maximum(m_prev, chunk4)
    w_inter = jnp.exp(m_prev - big_m)
    g_vec = jnp.exp(m_prev - m_last)
    w_last = jnp.exp(a4 - m_last)
    m_t = b4 + big_m
    return big_m, w_inter, g_vec, w_last, m_t, jnp.exp(-m_t)


def _group_scores(qvot, ks):
    lane_head = lax.broadcasted_iota(jnp.int32, (1, M_QK_W), 1) >> DK_SHIFT
    row_head = lax.broadcasted_iota(jnp.int32, (M_QK_W, 1), 0) >> DK_SHIFT
    qt = qvot(0, M_QK_W)
    k_stack = jnp.concatenate([jnp.where(lane_head == h, ks, jnp.zeros_like(ks)) for h in range(M_HEADS)], axis=0)
    qw = jnp.concatenate([jnp.where(row_head == h, qt, jnp.zeros_like(qt)) for h in range(M_HEADS)], axis=1)
    sc = _dot(ks, qw)
    return k_stack, qw, [sc[:, h * GROUP:(h + 1) * GROUP] for h in range(M_HEADS)]


def _group_values(qvot, sc_t, at_mats, big_m, causal):
    ones_rows = jnp.where(lax.broadcasted_iota(jnp.int32, (BF16_ROWS, GROUP), 0) == 0, 1.0, 0.0).astype(BF16)
    zero_blk = jnp.zeros((GROUP, GROUP), BF16)
    vta, s_t = [], []
    for h in range(M_HEADS):
        w_t = jnp.where(causal, jnp.exp(at_mats[h] - big_m[h:h + 1]), 0.0)
        s_t.append((sc_t[h] * w_t).astype(BF16))
        vta.append(jnp.concatenate([qvot(M_QK_W + h * M_DV, M_QK_W + (h + 1) * M_DV), ones_rows], axis=0))
    intra = []
    for h in range(0, M_HEADS, 2):
        pair = jnp.concatenate([jnp.concatenate([s_t[h], zero_blk], axis=1),
                                jnp.concatenate([zero_blk, s_t[h + 1]], axis=1)], axis=0)
        intra.append(_dot(jnp.concatenate(vta[h:h + 2], axis=1), pair))
    return vta, jnp.concatenate(intra, axis=1)


def _weighted_values(vta, w_rows):
    return jnp.concatenate([(vta[h].astype(F32) * w_rows[h:h + 1]).astype(BF16) for h in range(M_HEADS)], axis=1)


def _lanes_x(rows):
    return jnp.concatenate([rows[h:h + 1] for h in range(M_HEADS)], axis=1)


def _group_out(qvot, inter, intra, w_inter, e_neg_m, nw_ref):
    outs = []
    out_all = inter * _lanes_x(w_inter) + intra
    for h in range(M_HEADS):
        out_t = out_all[:, h * GROUP:(h + 1) * GROUP]
        hh = out_t[0:M_DV] / jnp.maximum(jnp.abs(out_t[M_DV:M_DV + 1]), e_neg_m[h:h + 1])
        ms = jnp.mean(hh * hh, axis=0, keepdims=True)
        hn = (hh * lax.rsqrt(ms + EPS)) * nw_ref[h * M_DV:(h + 1) * M_DV]
        o_t = qvot(M_QK_W + M_V_W + h * M_DV, M_QK_W + M_V_W + (h + 1) * M_DV)
        outs.append(hn * jax.nn.sigmoid(o_t.astype(F32)))
    return jnp.concatenate(outs, axis=0).T


def _decay_row(g_vec, lane0):
    lane_head = lax.broadcasted_iota(jnp.int32, (1, M_QK_W), 1) >> DK_SHIFT
    g_row = jnp.zeros((1, M_QK_W), F32)
    for h in range(M_HEADS):
        g_row = jnp.where(lane_head == h, g_vec[h:h + 1, lane0:lane0 + 1], g_row)
    return g_row


def _mlstm_sample_kernel(seq_len, qvot_ref, gt_ref, k_ref, mrow_ref, bias_ref, nw_ref, c_ref, n_ref,
                         h_ref, c_out_ref, n_out_ref, mt_ref):
    n_seq = GROUP // seq_len
    shift = seq_len.bit_length() - 1
    same, causal = _chunk_masks(shift)
    lane_seq = lax.broadcasted_iota(jnp.int32, (1, GROUP), 1) >> shift
    lane_seq_x = jnp.concatenate([lane_seq] * M_HEADS, axis=1)
    row0 = lax.broadcasted_iota(jnp.int32, (BF16_ROWS, M_QK_W), 0) == 0
    qvot = lambda r0, r1: qvot_ref[0, r0:r1, :]
    ks = k_ref[...] * (M_DK ** -0.5)
    (b4, a4, at_mats, run4, chunk4), = _groups_gates([gt_ref[0] + bias_ref[...]], same, causal)
    big_m, w_inter, g_vec, w_last, m_t, e_neg_m = _group_weights(mrow_ref[0, 0:M_HEADS, :], b4, a4, run4, chunk4)
    mt_ref[0] = jnp.concatenate([m_t, m_t], axis=0)
    k_stack, qw, sc_t = _group_scores(qvot, ks)

    st_old = []
    for s in range(n_seq):
        n_rows = jnp.where(row0, jnp.broadcast_to(n_ref[s:s + 1, :], (BF16_ROWS, M_QK_W)), 0.0)
        st_old.append(jnp.concatenate([c_ref[s].T, n_rows], axis=0))
    inter_all = _dot(jnp.concatenate(st_old, axis=0).astype(BF16), qw)
    vta, intra = _group_values(qvot, sc_t, at_mats, big_m, causal)
    inter = inter_all[0:S_ROWS]
    for s in range(1, n_seq):
        inter = jnp.where(lane_seq_x == s, inter_all[s * S_ROWS:(s + 1) * S_ROWS], inter)
    tall = jnp.concatenate([_weighted_values(vta, jnp.where(lane_seq == s, w_last, 0.0)) for s in range(n_seq)],
                           axis=0)
    d_st = _dot(tall, k_stack)
    for s in range(n_seq):
        st_new = _decay_row(g_vec, s * seq_len) * st_old[s] + d_st[s * S_ROWS:(s + 1) * S_ROWS]
        c_out_ref[s] = st_new[0:M_DV].T
        n_out_ref[s:s + 1, :] = st_new[M_DV:M_DV + 1]
    h_ref[...] = _group_out(qvot, inter, intra, w_inter, e_neg_m, nw_ref).astype(h_ref.dtype)


def _mlstm_sample(qvot, gt, k_m, mrow, bias_col, nw_col, c, n, seq_len):
    ngrp = qvot.shape[0]
    n_seq = GROUP // seq_len
    full = lambda a: pl.BlockSpec(a.shape, lambda i: (0,) * a.ndim)
    slab = lambda a: pl.BlockSpec((1,) + a.shape[1:], lambda i: (i, 0, 0))
    row = lambda w: pl.BlockSpec((GROUP, w), lambda i: (i, 0))
    c_spec = pl.BlockSpec((n_seq,) + c.shape[1:], lambda i: (i, 0, 0))
    n_spec = pl.BlockSpec((n_seq, n.shape[1]), lambda i: (i, 0))
    return pl.pallas_call(
        functools.partial(_mlstm_sample_kernel, seq_len),
        grid=(ngrp,),
        in_specs=[slab(qvot), slab(gt), row(M_QK_W), slab(mrow), full(bias_col), full(nw_col), c_spec, n_spec],
        out_specs=[row(M_V_W), c_spec, n_spec, slab(mrow)],
        out_shape=[jax.ShapeDtypeStruct((ngrp * GROUP, M_V_W), BF16),
                   jax.ShapeDtypeStruct(c.shape, F32),
                   jax.ShapeDtypeStruct(n.shape, F32),
                   jax.ShapeDtypeStruct(mrow.shape, F32)],
        compiler_params=_params(("arbitrary",)),
        name="mlstm_sample",
    )(qvot, gt, k_m, mrow, bias_col, nw_col, c, n)


def _swa_scores(qbs, has_prev, sink_ref, qt_ref, kv_ref, kvp_ref):
    hd = A_HEAD_DIM
    nq = A_GROUPS * WINDOW
    si = lax.broadcasted_iota(jnp.int32, (2 * WINDOW, nq), 0)
    qi = lax.broadcasted_iota(jnp.int32, (2 * WINDOW, nq), 1) & (WINDOW - 1)
    local = ((si < WINDOW) & (si > qi)) | ((si >= WINDOW) & (si - WINDOW <= qi))
    first = local & (has_prev | (si >= WINDOW))
    lane_grp = lax.broadcasted_iota(jnp.int32, (1, nq), 1) >> WINDOW_SHIFT
    kv_block = lambda i: kvp_ref[...] if i == 0 else kv_ref[(i - 1) * WINDOW:i * WINDOW, :]
    k_bf = lambda i: kv_block(i)[:, 0:A_KV_W].astype(BF16)
    vt_bf = lambda i: kv_block(i)[:, A_KV_W:].T.astype(BF16)
    zeros = jnp.zeros((hd, nq), BF16)
    sinks = []
    for kvh in range(A_KV_HEADS):
        sk = jnp.zeros((1, nq), F32)
        for g in range(A_GROUPS):
            sk = jnp.where(lane_grp == g, sink_ref[kvh * A_GROUPS + g], sk)
        sinks.append(sk)
    units = []
    for qb in qbs:
        kk = jnp.concatenate([k_bf(qb), k_bf(qb + 1)], axis=0)
        vt = jnp.concatenate([vt_bf(qb), vt_bf(qb + 1)], axis=1)
        mask = local if qb > 0 else first
        for kvh in range(A_KV_HEADS):
            q4t = jnp.concatenate(
                [qt_ref[qb, (kvh * A_GROUPS + g) * hd:(kvh * A_GROUPS + g + 1) * hd, :] for g in range(A_GROUPS)],
                axis=1)
            wq = jnp.concatenate([q4t, zeros] if kvh == 0 else [zeros, q4t], axis=0)
            units.append((jnp.where(mask, _dot(kk, wq), -jnp.inf), sinks[kvh], vt))
    return units


def _swa_finish(qbs, units, h_ref):
    hd = A_HEAD_DIM
    for i, qb in enumerate(qbs):
        pieces = []
        for kvh in range(A_KV_HEADS):
            s, sk, vt = units[i * A_KV_HEADS + kvh]
            mx = jnp.maximum(jnp.max(s, axis=0, keepdims=True), sk)
            p = jnp.exp(s - mx)
            den = jnp.sum(p, axis=0, keepdims=True) + jnp.exp(sk - mx)
            ot = _dot(vt[kvh * hd:(kvh + 1) * hd], p.astype(BF16)) / den
            pieces += [ot[:, g * WINDOW:(g + 1) * WINDOW] for g in range(A_GROUPS)]
        h_t = jnp.concatenate(pieces, axis=0)
        h_ref[qb * WINDOW:(qb + 1) * WINDOW, :] = h_t.T.astype(h_ref.dtype)


def _front_kernel(tiles_per_seq, n_cast, *refs):
    (sink_ref, x_ref, nw_ref, wmt_ref, wqgt_ref, wn_ref, bd_ref, qcol_ref, krow_ref, bias_ref, nwm_ref) = refs[:11]
    cast_in = refs[11:11 + n_cast]
    (gab_ref, hm_ref, ha_ref, kwin_ref, vwin_ref, st_ref, m_ref) = refs[11 + n_cast:18 + n_cast]
    cast_out = refs[18 + n_cast:18 + 2 * n_cast]
    set_a, set_b = refs[18 + 2 * n_cast:23 + 2 * n_cast], refs[23 + 2 * n_cast:28 + 2 * n_cast]
    st_s, m_s, kvp_s = refs[28 + 2 * n_cast:]
    odd = (pl.program_id(0) & 1) == 1
    for parity, q_set, p_set in ((jnp.logical_not(odd), set_a, set_b), (odd, set_b, set_a)):
        pl.when(parity)(functools.partial(
            _front_body, tiles_per_seq, sink_ref, x_ref, nw_ref, wmt_ref, wqgt_ref, wn_ref, bd_ref, qcol_ref,
            krow_ref, bias_ref, nwm_ref, gab_ref, hm_ref, ha_ref, kwin_ref, vwin_ref, st_ref, m_ref,
            *q_set, *p_set, st_s, m_s, kvp_s))
    for src, dst in zip(cast_in, cast_out):
        dst[...] = src[...].astype(dst.dtype)


def _front_body(tiles_per_seq, sink_ref, x_ref, nw_ref, wmt_ref, wqgt_ref, wn_ref, bd_ref, qcol_ref, krow_ref,
                bias_ref, nwm_ref,
                gab_ref, hm_ref, ha_ref, kwin_ref, vwin_ref, st_ref, m_ref,
                q_qvot, q_gt, q_qat, q_km, q_kva, p_qvot, p_gt, p_qat, p_km, p_kva, st_s, m_s, kvp_s):
    k = pl.program_id(0)

    @pl.when(k == 0)
    def _():
        for r in (p_qvot, p_gt, p_qat, p_km, p_kva, st_s, m_s, kvp_s):
            r[...] = jnp.zeros(r.shape, r.dtype)

    pieces = _proj_chunks(x_ref, nw_ref, wmt_ref, wqgt_ref, wn_ref, bd_ref, qcol_ref, krow_ref,
                          q_qvot, q_gt, q_qat, q_km, q_kva, gab_ref)

    seq_start = lax.rem(k - 1 + tiles_per_seq, tiles_per_seq) == 0
    n_blk = p_qat.shape[0]
    same, causal = _chunk_masks(GROUP.bit_length() - 1)
    st = jnp.where(seq_start, 0.0, st_s[...])
    m_col = jnp.where(seq_start, 0.0, m_s[0:M_HEADS, 0:1])

    qvots = [lambda r0, r1, g=g: p_qvot[g, r0:r1, :] for g in range(n_blk)]
    scores = [_group_scores(qvots[g], p_km[g * GROUP:(g + 1) * GROUP, :] * (M_DK ** -0.5))
              for g in range(n_blk)]
    inter = [_dot(st.astype(BF16), scores[0][1])]
    gates = _groups_gates([p_gt[g] + bias_ref[...] for g in range(n_blk)], same, causal)
    for piece in pieces[0:1]:
        piece()
    grp = []
    for g in range(n_blk):
        b4, a4, at_mats, run4, chunk4 = gates[g]
        weights = _group_weights(jnp.broadcast_to(m_col, (M_HEADS, GROUP)), b4, a4, run4, chunk4)
        m_col = weights[4][:, GROUP - 1:GROUP]
        grp.append((qvots[g], at_mats, weights) + scores[g])
    has_prev = jnp.logical_not(seq_start)
    qb_lo, qb_hi = list(range(0, n_blk // 2)), list(range(n_blk // 2, n_blk))
    units_lo = _swa_scores(qb_lo, has_prev, sink_ref, p_qat, p_kva, kvp_s)
    for piece in pieces[1:3]:
        piece()

    vals = []

    def group_values(g):
        qvot, at_mats, (big_m, _, _, w_last, _, _), k_stack, _, sc_t = grp[g]
        vta, intra = _group_values(qvot, sc_t, at_mats, big_m, causal)
        vals.append((intra, _dot(_weighted_values(vta, w_last), k_stack)))

    for g in range(0, n_blk // 2):
        group_values(g)
    _swa_finish(qb_lo[:1], units_lo[:A_KV_HEADS], ha_ref)
    for piece in pieces[3:4]:
        piece()
    for g in range(n_blk // 2, n_blk):
        group_values(g)
    _swa_finish(qb_lo[1:], units_lo[A_KV_HEADS:], ha_ref)
    units_hi = _swa_scores(qb_hi, has_prev, sink_ref, p_qat, p_kva, kvp_s)
    for piece in pieces[4:6]:
        piece()

    for g in range(n_blk):
        st = _decay_row(grp[g][2][2], 0) * st + vals[g][1]
        if g + 1 < n_blk:
            inter.append(_dot(st.astype(BF16), grp[g + 1][4]))
        if g == n_blk // 2 - 1:
            _swa_finish(qb_hi[:1], units_hi[:A_KV_HEADS], ha_ref)
            for piece in pieces[6:7]:
                piece()
    _swa_finish(qb_hi[1:], units_hi[A_KV_HEADS:], ha_ref)
    for piece in pieces[7:]:
        piece()

    for g in range(n_blk):
        qvot, _, (_, w_inter, _, _, _, e_neg_m) = grp[g][0:3]
        hm_ref[g * GROUP:(g + 1) * GROUP, :] = _group_out(qvot, inter[g], vals[g][0], w_inter, e_neg_m,
                                                          nwm_ref).astype(hm_ref.dtype)
    st_s[...] = st
    st_ref[0] = st
    m_rows = jnp.broadcast_to(m_col, (M_HEADS, LANES))
    m_rows = jnp.concatenate([m_rows, m_rows], axis=0)
    m_s[...] = m_rows
    m_ref[0] = m_rows
    tm = p_kva.shape[0]
    kwin_ref[0] = p_kva[tm - WINDOW:, 0:A_KV_W]
    vwin_ref[0] = p_kva[tm - WINDOW:, A_KV_W:]
    kvp_s[...] = p_kva[tm - WINDOW:, :]


def _cast_rows(rows, n_steps):
    rb = BF16_ROWS
    while rows % rb or rows // rb > n_steps:
        rb += BF16_ROWS
    return rb


def _front(x2d, nb, sinks, nw, wmt, wqgt, wn, bd, qcol, krow, bias_col, nwm_col, cast_ws, tm):
    n = x2d.shape[0]
    n_tiles = n // tm
    tps = n_tiles // nb
    n_blk = tm // LANES
    w_qvo = M_QK_W + 2 * M_V_W
    cur = lambda w: pl.BlockSpec((tm, w), lambda k: (jnp.minimum(k, n_tiles - 1), 0))
    prev = lambda w: pl.BlockSpec((tm, w), lambda k: (jnp.maximum(k - 1, 0), 0))
    per_seq = lambda r, w: pl.BlockSpec((1, r, w), lambda k: (jnp.maximum(k - 1, 0) // tps, 0, 0))

    def cast_spec(a):
        rb = _cast_rows(a.shape[0], n_tiles)
        return pl.BlockSpec((rb, a.shape[1]), lambda k: (jnp.minimum(k, a.shape[0] // rb - 1), 0))

    cast_in_specs = [cast_spec(a) for a in cast_ws]
    cast_out_specs = [cast_spec(a) for a in cast_ws]
    proj_scratch = [pltpu.VMEM((n_blk, w_qvo, LANES), BF16), pltpu.VMEM((n_blk, 2 * M_HEADS, LANES), F32),
                    pltpu.VMEM((n_blk, A_Q_W, LANES), BF16), pltpu.VMEM((tm, M_QK_W), BF16),
                    pltpu.VMEM((tm, 2 * A_KV_W), F32)]
    return pl.pallas_call(
        functools.partial(_front_kernel, tps, len(cast_ws)),
        grid=(n_tiles + 1,),
        in_specs=[pl.BlockSpec(memory_space=pltpu.SMEM), cur(D_MODEL)]
                 + [_const_spec(a.shape) for a in (nw, wmt, wqgt, wn, bd, qcol, krow, bias_col, nwm_col)]
                 + cast_in_specs,
        out_specs=[cur(2 * D_MODEL), prev(M_V_W), prev(A_Q_W), per_seq(WINDOW, A_KV_W), per_seq(WINDOW, A_KV_W),
                   per_seq(S_ROWS, M_QK_W), per_seq(SUBLANES, LANES)] + cast_out_specs,
        out_shape=[jax.ShapeDtypeStruct((n, 2 * D_MODEL), BF16),
                   jax.ShapeDtypeStruct((n, M_V_W), BF16),
                   jax.ShapeDtypeStruct((n, A_Q_W), BF16),
                   jax.ShapeDtypeStruct((nb, WINDOW, A_KV_W), F32),
                   jax.ShapeDtypeStruct((nb, WINDOW, A_KV_W), F32),
                   jax.ShapeDtypeStruct((nb, S_ROWS, M_QK_W), F32),
                   jax.ShapeDtypeStruct((nb, SUBLANES, LANES), F32)]
                  + [jax.ShapeDtypeStruct(a.shape, BF16) for a in cast_ws],
        scratch_shapes=proj_scratch + proj_scratch + [pltpu.VMEM((S_ROWS, M_QK_W), F32),
                                                       pltpu.VMEM((SUBLANES, LANES), F32),
                                                       pltpu.VMEM((WINDOW, 2 * A_KV_W), F32)],
        compiler_params=_params(("arbitrary",)),
        name="front",
    )(sinks, x2d, nw, wmt, wqgt, wn, bd, qcol, krow, bias_col, nwm_col, *cast_ws)


def _swa_sample_kernel(seq_len, sink_ref, qt_ref, kv_ref, ck_ref, cv_ref, h_ref, kwin_ref, vwin_ref):
    n_seq = GROUP // seq_len
    wb = ck_ref.shape[2]
    lane = lax.broadcasted_iota(jnp.int32, (1, LANES), 1)
    lo_half = lane < HALF
    q_rows = qt_ref[0].astype(F32).T
    kv_new = kv_ref[...]
    k_new = kv_new[:, 0:A_KV_W].reshape(n_seq, seq_len, A_KV_W)
    v_new = kv_new[:, A_KV_W:].reshape(n_seq, seq_len, A_KV_W)

    def to_kv_half(x, head):
        kvh = head // A_GROUPS
        if head % 2 != kvh:
            x = pltpu.roll(x, HALF, axis=1)
        return jnp.where(lo_half if kvh == 0 else ~lo_half, x, 0.0)

    lhs = jnp.concatenate(
        [to_kv_half(q_rows[:, (h // 2) * LANES:(h // 2 + 1) * LANES], h).reshape(n_seq, seq_len, LANES)
         for h in range(A_HEADS)], axis=1).astype(BF16)
    zpad = jnp.zeros((n_seq, BF16_ROWS - seq_len, A_KV_W), F32)
    k_nb = jnp.concatenate([k_new, zpad], axis=1).astype(BF16)
    v_nb = jnp.concatenate([v_new, zpad], axis=1).astype(BF16)
    s_c = jnp.einsum('sqf,sfk->sqk', lhs, ck_ref[...].astype(BF16), preferred_element_type=F32)
    s_n = jnp.einsum('sqf,skf->sqk', lhs, k_nb, preferred_element_type=F32)
    nrow = A_HEADS * seq_len
    ti = lax.broadcasted_iota(jnp.int32, (nrow, wb), 0) & (seq_len - 1)
    ki = lax.broadcasted_iota(jnp.int32, (nrow, wb), 1)
    mask_c = (ti + wb - ki) < WINDOW
    ti_n = lax.broadcasted_iota(jnp.int32, (nrow, BF16_ROWS), 0) & (seq_len - 1)
    ki_n = lax.broadcasted_iota(jnp.int32, (nrow, BF16_ROWS), 1)
    mask_n = ki_n <= ti_n
    row_head = lax.broadcasted_iota(jnp.int32, (nrow, 1), 0) >> (seq_len.bit_length() - 1)
    sk = jnp.zeros((nrow, 1), F32)
    for h in range(A_HEADS):
        sk = jnp.where(row_head == h, sink_ref[h], sk)
    s_c = jnp.where(mask_c, s_c, -jnp.inf)
    s_n = jnp.where(mask_n, s_n, -jnp.inf)
    mx = jnp.maximum(jnp.maximum(jnp.max(s_c, axis=-1, keepdims=True), jnp.max(s_n, axis=-1, keepdims=True)), sk)
    p_c = jnp.exp(s_c - mx)
    p_n = jnp.exp(s_n - mx)
    den = jnp.sum(p_c, axis=-1, keepdims=True) + jnp.sum(p_n, axis=-1, keepdims=True) + jnp.exp(sk - mx)
    o = (jnp.einsum('sqk,sfk->sqf', p_c.astype(BF16), cv_ref[...].astype(BF16), preferred_element_type=F32)
         + jnp.einsum('sqk,skf->sqf', p_n.astype(BF16), v_nb, preferred_element_type=F32)) / den

    def from_kv_half(head):
        x = o[:, head * seq_len:(head + 1) * seq_len, :].reshape(GROUP, LANES)
        return pltpu.roll(x, HALF, axis=1) if head % 2 != head // A_GROUPS else x

    for c in range(A_HEADS // 2):
        h_ref[:, c * LANES:(c + 1) * LANES] = jnp.where(lo_half, from_kv_half(2 * c),
                                                        from_kv_half(2 * c + 1)).astype(h_ref.dtype)

    kt_new, vt_new = kv_new[:, 0:A_KV_W].T, kv_new[:, A_KV_W:].T
    for s in range(n_seq):
        put = (wb - seq_len - s * seq_len) % LANES
        for new_t, c_ref, win_ref in ((kt_new, ck_ref, kwin_ref), (vt_new, cv_ref, vwin_ref)):
            win_ref[s] = jnp.where(lane >= wb - seq_len, pltpu.roll(new_t, put, axis=1) if put else new_t,
                                   pltpu.roll(c_ref[s], wb - seq_len, axis=1))


def _swa_sample(qat, kv_a, cache_k, cache_v, sinks, seq_len):
    ngrp = qat.shape[0]
    n_seq = GROUP // seq_len
    wb = cache_k.shape[2]
    assert wb == LANES, "window positions fill one lane-width"
    row = lambda w: pl.BlockSpec((GROUP, w), lambda i: (i, 0))
    cache = pl.BlockSpec((n_seq, A_KV_W, wb), lambda i: (i, 0, 0))
    return pl.pallas_call(
        functools.partial(_swa_sample_kernel, seq_len),
        grid=(ngrp,),
        in_specs=[pl.BlockSpec(memory_space=pltpu.SMEM), pl.BlockSpec((1, A_Q_W, LANES), lambda i: (i, 0, 0)),
                  row(2 * A_KV_W), cache, cache],
        out_specs=[row(A_Q_W), cache, cache],
        out_shape=[jax.ShapeDtypeStruct((ngrp * GROUP, A_Q_W), BF16),
                   jax.ShapeDtypeStruct(cache_k.shape, F32),
                   jax.ShapeDtypeStruct(cache_v.shape, F32)],
        compiler_params=_params(("arbitrary",)),
        name="swa_sample",
    )(sinks, qat, kv_a, cache_k, cache_v)


def _merge_ffn_kernel(x_ref, hm_ref, ha_ref, gab_ref, wa_ref, wb_ref, wo_ref, nw_ref, wg_ref, wu_ref, wd_ref,
                      y_ref):
    tm = x_ref.shape[0]
    halves = [slice(0, tm // 2), slice(tm // 2, tm)]
    mix = []
    for rs in halves:
        ga = jax.nn.sigmoid(gab_ref[rs, 0:D_MODEL].astype(F32))
        gb = jax.nn.sigmoid(gab_ref[rs, D_MODEL:].astype(F32))
        mix.append(ga * _dot(hm_ref[rs, :], wa_ref[...]) + gb * _dot(ha_ref[rs, :], wb_ref[...]))
    x1 = [x_ref[rs, :] + _dot(m.astype(BF16), wo_ref[...]) for rs, m in zip(halves, mix)]
    hf = [_rms_rows(v, nw_ref[...]).astype(BF16) for v in x1]
    act = [(jax.nn.silu(_dot(h, wg_ref[...])) * _dot(h, wu_ref[...])).astype(BF16) for h in hf]
    for rs, v, a in zip(halves, x1, act):
        y_ref[rs, :] = v + _dot(a, wd_ref[...])


def _merge_ffn(x2d, h_m, h_a, g_ab, wa, wb, wo, nw, wg, wu, wd, tm):
    n = x2d.shape[0]
    row = lambda w: pl.BlockSpec((tm, w), lambda i: (i, 0))
    return pl.pallas_call(
        _merge_ffn_kernel,
        grid=(n // tm,),
        in_specs=[row(D_MODEL), row(M_V_W), row(A_Q_W), row(2 * D_MODEL)]
                 + [_const_spec(w.shape) for w in (wa, wb, wo, nw, wg, wu, wd)],
        out_specs=row(D_MODEL),
        out_shape=jax.ShapeDtypeStruct((n, D_MODEL), F32),
        compiler_params=_params(("arbitrary",)),
        name="merge_ffn",
    )(x2d, h_m, h_a, g_ab, wa, wb, wo, nw, wg, wu, wd)


def kernel(x_prompt, x_sample, state_mlstm_C, state_mlstm_n, state_mlstm_m, cache_swa_k, cache_swa_v,
           norm_mix_w, w_in, mlstm_i_bias, mlstm_f_bias, mlstm_norm_w, q_norm_w, k_norm_w, attn_sinks,
           w_branch_a, w_branch_b, w_out, norm_ffn_w, w_gate, w_up, w_down):
    depth = w_in.shape[0]
    assert depth == 1, "single trunk layer"
    l = 0
    bp, tp = x_prompt.shape[0], x_prompt.shape[1]
    bs, ts = x_sample.shape[0], x_sample.shape[1]
    assert tp % TOKEN_TILE == 0 and (bs * ts) % GROUP == 0 and GROUP % ts == 0 and ts & (ts - 1) == 0
    assert ts <= SUBLANES, "sample chunk must fit one sublane tile"

    wt = jnp.transpose(w_in[l])
    c_km, c_vm = M_QK_W, 2 * M_QK_W
    c_g = 2 * M_QK_W + 2 * M_V_W
    c_qa = c_g + 2 * M_HEADS
    c_ka = c_qa + A_Q_W
    gate_pad = jnp.zeros((BF16_ROWS - 2 * M_HEADS, D_MODEL), F32)
    wmt = jnp.concatenate([wt[0:c_km], wt[c_vm:c_g]], axis=0).astype(BF16)
    wqgt = jnp.concatenate([wt[c_qa:c_ka], wt[c_g:c_qa], gate_pad], axis=0).astype(BF16)
    wn = jnp.concatenate([wt[c_km:c_vm], wt[c_ka:]], axis=0).astype(BF16)
    head_of = jnp.arange(A_KV_W) // A_HEAD_DIM
    bd = (head_of[:, None] == head_of[None, :]).astype(BF16)
    bd = jnp.concatenate([bd, bd], axis=0)
    qcol = (jnp.tile(q_norm_w[l], A_HEADS) * (A_HEAD_DIM ** -0.5)).reshape(A_Q_W, 1)
    krow = jnp.tile(k_norm_w[l], A_KV_HEADS).reshape(1, A_KV_W)
    nw_mix = norm_mix_w[l].reshape(1, D_MODEL)
    nw_ffn = norm_ffn_w[l].reshape(1, D_MODEL)
    bias_col = jnp.concatenate([mlstm_i_bias[l], mlstm_f_bias[l]]).reshape(2 * M_HEADS, 1)
    nw_col = mlstm_norm_w[l].reshape(M_V_W, 1)
    sinks = attn_sinks[l]
    proj = lambda x2d, tm: _proj(x2d, nw_mix, wmt, wqgt, wn, bd, qcol, krow, tm)

    xp = x_prompt.reshape(bp * tp, D_MODEL)
    later_ws = (w_branch_a[l], w_branch_b[l], w_out[l], w_gate[l], w_up[l], w_down[l])
    g_ab, h_m, h_a, kwin_p, vwin_p, st_p, m_p, wa, wb, wo, wg, wu, wd = _front(
        xp, bp, sinks, nw_mix, wmt, wqgt, wn, bd, qcol, krow, bias_col, nw_col, later_ws, TOKEN_TILE)
    merge = lambda x2d, h_m, h_a, g_ab, tm: _merge_ffn(x2d, h_m, h_a, g_ab, wa, wb, wo, nw_ffn, wg, wu, wd, tm)
    yp = merge(xp, h_m, h_a, g_ab, TOKEN_TILE).reshape(bp, tp, D_MODEL)
    c_p = jnp.swapaxes(st_p[:, :M_DV, :], 1, 2).reshape(bp, M_HEADS, M_DK, M_DV)
    n_p = st_p[:, M_DV, :].reshape(bp, M_HEADS, M_DK)
    m_pr = m_p[:, :M_HEADS, 0]

    ns = bs * ts
    xs = x_sample.reshape(ns, D_MODEL)
    tms = TOKEN_TILE if ns % TOKEN_TILE == 0 else GROUP
    qvot, gt, qat, k_m, kv_a, g_ab = proj(xs, tms)
    ngrp = ns // GROUP
    m_lanes = jnp.repeat(state_mlstm_m[l], ts, axis=0).reshape(ngrp, GROUP, M_HEADS)
    mrow = jnp.pad(jnp.swapaxes(m_lanes, 1, 2), ((0, 0), (0, SUBLANES - M_HEADS), (0, 0)))
    h_m, c_s, n_s, mt_s = _mlstm_sample(qvot, gt, k_m, mrow, bias_col, nw_col,
                                        state_mlstm_C[l].reshape(bs, M_QK_W, M_DV),
                                        state_mlstm_n[l].reshape(bs, M_QK_W), ts)
    wbuf = cache_swa_k.shape[2]
    to_fm = lambda a: jnp.transpose(a, (0, 2, 3, 1)).reshape(bs, A_KV_W, wbuf)
    from_fm = lambda a: jnp.transpose(a.reshape(bs, A_KV_HEADS, A_HEAD_DIM, wbuf), (0, 3, 1, 2))[None]
    h_a, kwin_s, vwin_s = _swa_sample(qat, kv_a, to_fm(cache_swa_k[l]), to_fm(cache_swa_v[l]), sinks, ts)
    ys = merge(xs, h_m, h_a, g_ab, tms).reshape(bs, ts, D_MODEL)
    m_s = jnp.swapaxes(mt_s[:, :M_HEADS, :], 1, 2).reshape(bs, ts, M_HEADS)[:, ts - 1, :]

    kv5 = lambda a: a.reshape(a.shape[0], a.shape[1], A_KV_HEADS, A_HEAD_DIM)[None]
    return (yp, ys,
            c_p[None], n_p[None], m_pr[None], kv5(kwin_p), kv5(vwin_p),
            c_s.reshape(bs, M_HEADS, M_DK, M_DV)[None], n_s.reshape(bs, M_HEADS, M_DK)[None], m_s[None],
            from_fm(kwin_s), from_fm(vwin_s))
```

```python
import functools

import jax
import jax.numpy as jnp
from jax import lax
from jax.experimental import pallas as pl
from jax.experimental.pallas import tpu as pltpu

F32 = jnp.float32
BF16 = jnp.bfloat16

D_MODEL = 1024
M_HEADS = 4
M_DK = 64
M_DV = 128
M_CHUNK = 64
M_QK_W = M_HEADS * M_DK
M_V_W = M_HEADS * M_DV
A_HEADS = 8
A_KV_HEADS = 2
A_HEAD_DIM = 64
A_GROUPS = A_HEADS // A_KV_HEADS
A_Q_W = A_HEADS * A_HEAD_DIM
A_KV_W = A_KV_HEADS * A_HEAD_DIM
WINDOW = 128
D_FF = 2816
EPS = 1e-6

LANES = 128
SUBLANES = 8
BF16_ROWS = 16
GROUP = 128
S_ROWS = M_DV + BF16_ROWS
TOKEN_TILE = 512
VMEM_LIMIT = 56 * 1024 * 1024
DK_SHIFT = M_DK.bit_length() - 1
WINDOW_SHIFT = WINDOW.bit_length() - 1
HALF = LANES // 2
assert A_HEAD_DIM == HALF and A_KV_W == LANES, "attention head pairs share one lane-width"

NT_DIMS = (((1,), (1,)), ((), ()))


def _dot(a, b):
    return jnp.dot(a, b, preferred_element_type=F32)


def _dot_nt(a, b):
    return lax.dot_general(a, b, NT_DIMS, preferred_element_type=F32)


def _const_spec(shape):
    nd = len(shape)
    return pl.BlockSpec(shape, lambda *_: (0,) * nd, pipeline_mode=pl.Buffered(1))


def _params(sem):
    return pltpu.CompilerParams(dimension_semantics=sem, vmem_limit_bytes=VMEM_LIMIT)


def _rms_rows(x, nw):
    ms = jnp.mean(x * x, axis=-1, keepdims=True)
    return (x * lax.rsqrt(ms + EPS)) * nw


GAB_CHUNK = 512


def _proj_chunks(x_ref, nw_ref, wmt_ref, wqgt_ref, wn_ref, bd_ref, qcol_ref, krow_ref,
                 qvot_ref, gt_ref, qat_ref, km_ref, kva_ref, gab_ref):
    hn = _rms_rows(x_ref[...], nw_ref[...]).astype(BF16)
    n_blk = qat_ref.shape[0]

    def put(ref, rows, val):
        for c in range(n_blk):
            ref[c, rows, :] = val[:, c * LANES:(c + 1) * LANES].astype(ref.dtype)

    def mlstm_qvo():
        put(qvot_ref, slice(None), _dot_nt(wmt_ref[...], hn))

    def attn_q_and_gates():
        qt = _dot_nt(wqgt_ref[...], hn)
        put(gt_ref, slice(None), qt[A_Q_W:A_Q_W + 2 * M_HEADS])
        for h in range(A_HEADS):
            hs = slice(h * A_HEAD_DIM, (h + 1) * A_HEAD_DIM)
            blk = qt[hs]
            ssq_q = jnp.sum(blk * blk, axis=0, keepdims=True)
            put(qat_ref, hs, (blk * lax.rsqrt(ssq_q * (1.0 / A_HEAD_DIM) + EPS)) * qcol_ref[hs])

    def mlstm_k():
        km_ref[...] = _dot_nt(hn, wn_ref[0:M_QK_W, :]).astype(km_ref.dtype)

    def branch_gates(c0):
        def run():
            w0 = M_QK_W + 2 * A_KV_W + c0
            gab_ref[:, c0:c0 + GAB_CHUNK] = _dot_nt(hn, wn_ref[w0:w0 + GAB_CHUNK, :]).astype(gab_ref.dtype)
        return run

    def attn_kv():
        kv = _dot_nt(hn, wn_ref[M_QK_W:M_QK_W + 2 * A_KV_W, :])
        k = kv[:, 0:A_KV_W]
        ksq = k * k
        hi = ksq.astype(BF16)
        lo = (ksq - hi.astype(F32)).astype(BF16)
        ssq = _dot(hi, bd_ref[...]) + _dot(lo, bd_ref[...])
        kva_ref[:, 0:A_KV_W] = (k * lax.rsqrt(ssq * (1.0 / A_HEAD_DIM) + EPS)) * krow_ref[...]
        kva_ref[:, A_KV_W:] = kv[:, A_KV_W:]

    return ([mlstm_qvo, attn_q_and_gates, mlstm_k]
            + [branch_gates(c0) for c0 in range(0, gab_ref.shape[1], GAB_CHUNK)] + [attn_kv])


def _proj_kernel(*refs):
    for piece in _proj_chunks(*refs):
        piece()


def _proj(x2d, nw, wmt, wqgt, wn, bd, qcol, krow, tm):
    n = x2d.shape[0]
    row = lambda w: pl.BlockSpec((tm, w), lambda i: (i, 0))
    n_blk = tm // LANES
    slab = lambda r: pl.BlockSpec((n_blk, r, LANES), lambda i: (i, 0, 0))
    w_qvo = M_QK_W + 2 * M_V_W
    return pl.pallas_call(
        _proj_kernel,
        grid=(n // tm,),
        in_specs=[row(D_MODEL)] + [_const_spec(a.shape) for a in (nw, wmt, wqgt, wn, bd, qcol, krow)],
        out_specs=[slab(w_qvo), slab(2 * M_HEADS), slab(A_Q_W), row(M_QK_W), row(2 * A_KV_W), row(2 * D_MODEL)],
        out_shape=[jax.ShapeDtypeStruct((n // LANES, w_qvo, LANES), BF16),
                   jax.ShapeDtypeStruct((n // LANES, 2 * M_HEADS, LANES), F32),
                   jax.ShapeDtypeStruct((n // LANES, A_Q_W, LANES), BF16),
                   jax.ShapeDtypeStruct((n, M_QK_W), BF16),
                   jax.ShapeDtypeStruct((n, 2 * A_KV_W), F32),
                   jax.ShapeDtypeStruct((n, 2 * D_MODEL), BF16)],
        compiler_params=_params(("arbitrary",)),
        name="proj",
    )(x2d, nw, wmt, wqgt, wn, bd, qcol, krow)


def _split3_rows(x):
    hi = x.astype(BF16).astype(F32)
    r1 = x - hi
    mid = r1.astype(BF16).astype(F32)
    lo = r1 - mid
    return jnp.concatenate([hi, mid, lo], axis=0).astype(BF16)


def _log_sigmoid(x):
    return jnp.minimum(x, 0.0) - jnp.log1p(jnp.exp(-jnp.abs(x)))


def _chunk_masks(chunk_shift):
    s = lax.broadcasted_iota(jnp.int32, (GROUP, GROUP), 0)
    t = lax.broadcasted_iota(jnp.int32, (GROUP, GROUP), 1)
    same = (s >> chunk_shift) == (t >> chunk_shift)
    return same, same & (s <= t)


def _groups_gates(gts, same, causal):
    n = len(gts)
    assert 2 * M_HEADS == SUBLANES and n * SUBLANES <= GROUP
    cm_bf = jnp.where(causal, 1.0, 0.0).astype(BF16)
    lf = _log_sigmoid(jnp.concatenate(gts, axis=0))
    nr = n * SUBLANES
    bt3 = _dot(_split3_rows(lf), cm_bf)
    bt = (bt3[0:nr] + bt3[nr:2 * nr]) + bt3[2 * nr:3 * nr]
    b4 = [bt[g * SUBLANES + M_HEADS:(g + 1) * SUBLANES] for g in range(n)]
    a4 = [gts[g][0:M_HEADS] - b4[g] for g in range(n)]
    a_rows = [a for g in range(n) for a in (a4[g], a4[g])]
    if nr < GROUP:
        a_rows.append(jnp.zeros((GROUP - nr, GROUP), F32))
    a_cols = jnp.concatenate(a_rows, axis=0).T
    out = []
    for g in range(n):
        at_mats, run_rows, chunk_rows = [], [], []
        for h in range(M_HEADS):
            c = g * SUBLANES + h
            at = jnp.broadcast_to(a_cols[:, c:c + 1], (GROUP, GROUP))
            at_mats.append(at)
            run_rows.append(jnp.max(jnp.where(causal, at, -jnp.inf), axis=0, keepdims=True))
            chunk_rows.append(jnp.max(jnp.where(same, at, -jnp.inf), axis=0, keepdims=True))
        out.append((b4[g], a4[g], at_mats, jnp.concatenate(run_rows, axis=0), jnp.concatenate(chunk_rows, axis=0)))
    return out


def _group_weights(m_prev, b4, a4, run4, chunk4):
    big_m = jnp.maximum(m_prev, run4)
    m_last = jnp.maximum(m_prev, chunk4)
    w_inter = jnp.exp(m_prev - big_m)
    g_vec = jnp.exp(m_prev - m_last)
    w_last = jnp.exp(a4 - m_last)
    m_t = b4 + big_m
    return big_m, w_inter, g_vec, w_last, m_t, jnp.exp(-m_t)


def _group_scores(qvot, ks):
    lane_head = lax.broadcasted_iota(jnp.int32, (1, M_QK_W), 1) >> DK_SHIFT
    row_head = lax.broadcasted_iota(jnp.int32, (M_QK_W, 1), 0) >> DK_SHIFT
    qt = qvot(0, M_QK_W)
    k_stack = jnp.concatenate([jnp.where(lane_head == h, ks, jnp.zeros_like(ks)) for h in range(M_HEADS)], axis=0)
    qw = jnp.concatenate([jnp.where(row_head == h, qt, jnp.zeros_like(qt)) for h in range(M_HEADS)], axis=1)
    zero_blk = jnp.zeros((M_DK, GROUP), BF16)
    sc_t = []
    for h in range(0, M_HEADS, 2):
        q_pair = jnp.concatenate(
            [jnp.concatenate([qvot(h * M_DK, (h + 1) * M_DK), zero_blk], axis=1),
             jnp.concatenate([zero_blk, qvot((h + 1) * M_DK, (h + 2) * M_DK)], axis=1)], axis=0)
        sc = _dot(ks[:, h * M_DK:(h + 2) * M_DK], q_pair)
        sc_t += [sc[:, 0:GROUP], sc[:, GROUP:]]
    return k_stack, qw, sc_t


def _group_values(qvot, sc_t, at_mats, big_m, causal):
    ones_rows = jnp.where(lax.broadcasted_iota(jnp.int32, (BF16_ROWS, GROUP), 0) == 0, 1.0, 0.0).astype(BF16)
    zero_blk = jnp.zeros((GROUP, GROUP), BF16)
    vta, s_t = [], []
    for h in range(M_HEADS):
        w_t = jnp.where(causal, jnp.exp(at_mats[h] - big_m[h:h + 1]), 0.0)
        s_t.append((sc_t[h] * w_t).astype(BF16))
        vta.append(jnp.concatenate([qvot(M_QK_W + h * M_DV, M_QK_W + (h + 1) * M_DV), ones_rows], axis=0))
    intra = []
    for h in range(0, M_HEADS, 2):
        pair = jnp.concatenate([jnp.concatenate([s_t[h], zero_blk], axis=1),
                                jnp.concatenate([zero_blk, s_t[h + 1]], axis=1)], axis=0)
        intra.append(_dot(jnp.concatenate(vta[h:h + 2], axis=1), pair))
    return vta, jnp.concatenate(intra, axis=1)


def _weighted_values(vta, w_rows):
    return jnp.concatenate([(vta[h].astype(F32) * w_rows[h:h + 1]).astype(BF16) for h in range(M_HEADS)], axis=1)


def _lanes_x(rows):
    return jnp.concatenate([rows[h:h + 1] for h in range(M_HEADS)], axis=1)


def _group_out(qvot, inter, intra, w_inter, e_neg_m, nw_ref):
    outs = []
    out_all = inter * _lanes_x(w_inter) + intra
    for h in range(M_HEADS):
        out_t = out_all[:, h * GROUP:(h + 1) * GROUP]
        hh = out_t[0:M_DV] / jnp.maximum(jnp.abs(out_t[M_DV:M_DV + 1]), e_neg_m[h:h + 1])
        ms = jnp.mean(hh * hh, axis=0, keepdims=True)
        hn = (hh * lax.rsqrt(ms + EPS)) * nw_ref[h * M_DV:(h + 1) * M_DV]
        o_t = qvot(M_QK_W + M_V_W + h * M_DV, M_QK_W + M_V_W + (h + 1) * M_DV)
        outs.append(hn * jax.nn.sigmoid(o_t.astype(F32)))
    return jnp.concatenate(outs, axis=0).T


def _decay_row(g_vec, lane0):
    lane_head = lax.broadcasted_iota(jnp.int32, (1, M_QK_W), 1) >> DK_SHIFT
    g_row = jnp.zeros((1, M_QK_W), F32)
    for h in range(M_HEADS):
        g_row = jnp.where(lane_head == h, g_vec[h:h + 1, lane0:lane0 + 1], g_row)
    return g_row


def _mlstm_sample_kernel(seq_len, qvot_ref, gt_ref, k_ref, mrow_ref, bias_ref, nw_ref, c_ref, n_ref,
                         h_ref, c_out_ref, n_out_ref, mt_ref):
    n_seq = GROUP // seq_len
    shift = seq_len.bit_length() - 1
    same, causal = _chunk_masks(shift)
    lane_seq = lax.broadcasted_iota(jnp.int32, (1, GROUP), 1) >> shift
    lane_seq_x = jnp.concatenate([lane_seq] * M_HEADS, axis=1)
    row0 = lax.broadcasted_iota(jnp.int32, (BF16_ROWS, M_QK_W), 0) == 0
    qvot = lambda r0, r1: qvot_ref[0, r0:r1, :]
    ks = k_ref[...] * (M_DK ** -0.5)
    (b4, a4, at_mats, run4, chunk4), = _groups_gates([gt_ref[0] + bias_ref[...]], same, causal)
    big_m, w_inter, g_vec, w_last, m_t, e_neg_m = _group_weights(mrow_ref[0, 0:M_HEADS, :], b4, a4, run4, chunk4)
    mt_ref[0] = jnp.concatenate([m_t, m_t], axis=0)
    k_stack, qw, sc_t = _group_scores(qvot, ks)

    st_old = []
    for s in range(n_seq):
        n_rows = jnp.where(row0, jnp.broadcast_to(n_ref[s:s + 1, :], (BF16_ROWS, M_QK_W)), 0.0)
        st_old.append(jnp.concatenate([c_ref[s].T, n_rows], axis=0))
    inter_all = _dot(jnp.concatenate(st_old, axis=0).astype(BF16), qw)
    vta, intra = _group_values(qvot, sc_t, at_mats, big_m, causal)
    inter = inter_all[0:S_ROWS]
    for s in range(1, n_seq):
        inter = jnp.where(lane_seq_x == s, inter_all[s * S_ROWS:(s + 1) * S_ROWS], inter)
    tall = jnp.concatenate([_weighted_values(vta, jnp.where(lane_seq == s, w_last, 0.0)) for s in range(n_seq)],
                           axis=0)
    d_st = _dot(tall, k_stack)
    for s in range(n_seq):
        st_new = _decay_row(g_vec, s * seq_len) * st_old[s] + d_st[s * S_ROWS:(s + 1) * S_ROWS]
        c_out_ref[s] = st_new[0:M_DV].T
        n_out_ref[s:s + 1, :] = st_new[M_DV:M_DV + 1]
    h_ref[...] = _group_out(qvot, inter, intra, w_inter, e_neg_m, nw_ref).astype(h_ref.dtype)


def _mlstm_sample(qvot, gt, k_m, mrow, bias_col, nw_col, c, n, seq_len):
    ngrp = qvot.shape[0]
    n_seq = GROUP // seq_len
    full = lambda a: pl.BlockSpec(a.shape, lambda i: (0,) * a.ndim)
    slab = lambda a: pl.BlockSpec((1,) + a.shape[1:], lambda i: (i, 0, 0))
    row = lambda w: pl.BlockSpec((GROUP, w), lambda i: (i, 0))
    c_spec = pl.BlockSpec((n_seq,) + c.shape[1:], lambda i: (i, 0, 0))
    n_spec = pl.BlockSpec((n_seq, n.shape[1]), lambda i: (i, 0))
    return pl.pallas_call(
        functools.partial(_mlstm_sample_kernel, seq_len),
        grid=(ngrp,),
        in_specs=[slab(qvot), slab(gt), row(M_QK_W), slab(mrow), full(bias_col), full(nw_col), c_spec, n_spec],
        out_specs=[row(M_V_W), c_spec, n_spec, slab(mrow)],
        out_shape=[jax.ShapeDtypeStruct((ngrp * GROUP, M_V_W), BF16),
                   jax.ShapeDtypeStruct(c.shape, F32),
                   jax.ShapeDtypeStruct(n.shape, F32),
                   jax.ShapeDtypeStruct(mrow.shape, F32)],
        compiler_params=_params(("arbitrary",)),
        name="mlstm_sample",
    )(qvot, gt, k_m, mrow, bias_col, nw_col, c, n)


def _swa_scores(qbs, has_prev, sink_ref, qt_ref, kv_ref, kvp_ref):
    hd = A_HEAD_DIM
    nq = A_GROUPS * WINDOW
    si = lax.broadcasted_iota(jnp.int32, (2 * WINDOW, nq), 0)
    qi = lax.broadcasted_iota(jnp.int32, (2 * WINDOW, nq), 1) & (WINDOW - 1)
    local = ((si < WINDOW) & (si > qi)) | ((si >= WINDOW) & (si - WINDOW <= qi))
    first = local & (has_prev | (si >= WINDOW))
    lane_grp = lax.broadcasted_iota(jnp.int32, (1, nq), 1) >> WINDOW_SHIFT
    kv_block = lambda i: kvp_ref[...] if i == 0 else kv_ref[(i - 1) * WINDOW:i * WINDOW, :]
    k_bf = lambda i: kv_block(i)[:, 0:A_KV_W].astype(BF16)
    vt_bf = lambda i: kv_block(i)[:, A_KV_W:].T.astype(BF16)
    zeros = jnp.zeros((hd, nq), BF16)
    sinks = []
    for kvh in range(A_KV_HEADS):
        sk = jnp.zeros((1, nq), F32)
        for g in range(A_GROUPS):
            sk = jnp.where(lane_grp == g, sink_ref[kvh * A_GROUPS + g], sk)
        sinks.append(sk)
    units = []
    for qb in qbs:
        kk = jnp.concatenate([k_bf(qb), k_bf(qb + 1)], axis=0)
        vt = jnp.concatenate([vt_bf(qb), vt_bf(qb + 1)], axis=1)
        mask = local if qb > 0 else first
        for kvh in range(A_KV_HEADS):
            q4t = jnp.concatenate(
                [qt_ref[qb, (kvh * A_GROUPS + g) * hd:(kvh * A_GROUPS + g + 1) * hd, :] for g in range(A_GROUPS)],
                axis=1)
            wq = jnp.concatenate([q4t, zeros] if kvh == 0 else [zeros, q4t], axis=0)
            units.append((jnp.where(mask, _dot(kk, wq), -jnp.inf), sinks[kvh], vt))
    return units


def _swa_finish(qbs, units, h_ref):
    hd = A_HEAD_DIM
    for i, qb in enumerate(qbs):
        pieces = []
        for kvh in range(A_KV_HEADS):
            s, sk, vt = units[i * A_KV_HEADS + kvh]
            mx = jnp.maximum(jnp.max(s, axis=0, keepdims=True), sk)
            p = jnp.exp(s - mx)
            den = jnp.sum(p, axis=0, keepdims=True) + jnp.exp(sk - mx)
            ot = _dot(vt[kvh * hd:(kvh + 1) * hd], p.astype(BF16)) / den
            pieces += [ot[:, g * WINDOW:(g + 1) * WINDOW] for g in range(A_GROUPS)]
        h_t = jnp.concatenate(pieces, axis=0)
        h_ref[qb * WINDOW:(qb + 1) * WINDOW, :] = h_t.T.astype(h_ref.dtype)


def _front_kernel(tiles_per_seq, n_cast, *refs):
    (sink_ref, x_ref, nw_ref, wmt_ref, wqgt_ref, wn_ref, bd_ref, qcol_ref, krow_ref, bias_ref, nwm_ref) = refs[:11]
    cast_in = refs[11:11 + n_cast]
    (gab_ref, hm_ref, ha_ref, kwin_ref, vwin_ref, st_ref, m_ref) = refs[11 + n_cast:18 + n_cast]
    cast_out = refs[18 + n_cast:18 + 2 * n_cast]
    set_a, set_b = refs[18 + 2 * n_cast:23 + 2 * n_cast], refs[23 + 2 * n_cast:28 + 2 * n_cast]
    st_s, m_s, kvp_s = refs[28 + 2 * n_cast:]
    odd = (pl.program_id(0) & 1) == 1
    for parity, q_set, p_set in ((jnp.logical_not(odd), set_a, set_b), (odd, set_b, set_a)):
        pl.when(parity)(functools.partial(
            _front_body, tiles_per_seq, sink_ref, x_ref, nw_ref, wmt_ref, wqgt_ref, wn_ref, bd_ref, qcol_ref,
            krow_ref, bias_ref, nwm_ref, gab_ref, hm_ref, ha_ref, kwin_ref, vwin_ref, st_ref, m_ref,
            *q_set, *p_set, st_s, m_s, kvp_s))
    for src, dst in zip(cast_in, cast_out):
        dst[...] = src[...].astype(dst.dtype)


def _front_body(tiles_per_seq, sink_ref, x_ref, nw_ref, wmt_ref, wqgt_ref, wn_ref, bd_ref, qcol_ref, krow_ref,
                bias_ref, nwm_ref,
                gab_ref, hm_ref, ha_ref, kwin_ref, vwin_ref, st_ref, m_ref,
                q_qvot, q_gt, q_qat, q_km, q_kva, p_qvot, p_gt, p_qat, p_km, p_kva, st_s, m_s, kvp_s):
    k = pl.program_id(0)

    @pl.when(k == 0)
    def _():
        for r in (p_qvot, p_gt, p_qat, p_km, p_kva, st_s, m_s, kvp_s):
            r[...] = jnp.zeros(r.shape, r.dtype)

    pieces = _proj_chunks(x_ref, nw_ref, wmt_ref, wqgt_ref, wn_ref, bd_ref, qcol_ref, krow_ref,
                          q_qvot, q_gt, q_qat, q_km, q_kva, gab_ref)

    seq_start = lax.rem(k - 1 + tiles_per_seq, tiles_per_seq) == 0
    n_blk = p_qat.shape[0]
    same, causal = _chunk_masks(GROUP.bit_length() - 1)
    st = jnp.where(seq_start, 0.0, st_s[...])
    m_col = jnp.where(seq_start, 0.0, m_s[0:M_HEADS, 0:1])

    qvots = [lambda r0, r1, g=g: p_qvot[g, r0:r1, :] for g in range(n_blk)]
    scores = [_group_scores(qvots[g], p_km[g * GROUP:(g + 1) * GROUP, :] * (M_DK ** -0.5))
              for g in range(n_blk)]
    inter = [_dot(st.astype(BF16), scores[0][1])]
    gates = _groups_gates([p_gt[g] + bias_ref[...] for g in range(n_blk)], same, causal)
    for piece in pieces[0:1]:
        piece()
    grp = []
    for g in range(n_blk):
        b4, a4, at_mats, run4, chunk4 = gates[g]
        weights = _group_weights(jnp.broadcast_to(m_col, (M_HEADS, GROUP)), b4, a4, run4, chunk4)
        m_col = weights[4][:, GROUP - 1:GROUP]
        grp.append((qvots[g], at_mats, weights) + scores[g])
    has_prev = jnp.logical_not(seq_start)
    qb_lo, qb_hi = list(range(0, n_blk // 2)), list(range(n_blk // 2, n_blk))
    units_lo = _swa_scores(qb_lo, has_prev, sink_ref, p_qat, p_kva, kvp_s)
    for piece in pieces[1:3]:
        piece()

    vals = []

    def group_values(g):
        qvot, at_mats, (big_m, _, _, w_last, _, _), k_stack, _, sc_t = grp[g]
        vta, intra = _group_values(qvot, sc_t, at_mats, big_m, causal)
        vals.append((intra, _dot(_weighted_values(vta, w_last), k_stack)))

    for g in range(0, n_blk // 2):
        group_values(g)
    _swa_finish(qb_lo[:1], units_lo[:A_KV_HEADS], ha_ref)
    for piece in pieces[3:4]:
        piece()
    for g in range(n_blk // 2, n_blk):
        group_values(g)
    _swa_finish(qb_lo[1:], units_lo[A_KV_HEADS:], ha_ref)
    units_hi = _swa_scores(qb_hi, has_prev, sink_ref, p_qat, p_kva, kvp_s)
    for piece in pieces[4:6]:
        piece()

    def group_out(g):
        qvot, _, (_, w_inter, _, _, _, e_neg_m) = grp[g][0:3]
        hm_ref[g * GROUP:(g + 1) * GROUP, :] = _group_out(qvot, inter[g], vals[g][0], w_inter, e_neg_m,
                                                          nwm_ref).astype(hm_ref.dtype)

    for g in range(n_blk):
        st = _decay_row(grp[g][2][2], 0) * st + vals[g][1]
        if g + 1 < n_blk:
            inter.append(_dot(st.astype(BF16), grp[g + 1][4]))
        group_out(g)
        if g == n_blk // 2 - 1:
            _swa_finish(qb_hi[:1], units_hi[:A_KV_HEADS], ha_ref)
            for piece in pieces[6:7]:
                piece()
    _swa_finish(qb_hi[1:], units_hi[A_KV_HEADS:], ha_ref)
    for piece in pieces[7:]:
        piece()
    st_s[...] = st
    st_ref[0] = st
    m_rows = jnp.broadcast_to(m_col, (M_HEADS, LANES))
    m_rows = jnp.concatenate([m_rows, m_rows], axis=0)
    m_s[...] = m_rows
    m_ref[0] = m_rows
    tm = p_kva.shape[0]
    kwin_ref[0] = p_kva[tm - WINDOW:, 0:A_KV_W]
    vwin_ref[0] = p_kva[tm - WINDOW:, A_KV_W:]
    kvp_s[...] = p_kva[tm - WINDOW:, :]


def _cast_rows(rows, n_steps):
    rb = BF16_ROWS
    while rows % rb or rows // rb > n_steps:
        rb += BF16_ROWS
    return rb


def _front(x2d, nb, sinks, nw, wmt, wqgt, wn, bd, qcol, krow, bias_col, nwm_col, cast_ws, tm):
    n = x2d.shape[0]
    n_tiles = n // tm
    tps = n_tiles // nb
    n_blk = tm // LANES
    w_qvo = M_QK_W + 2 * M_V_W
    cur = lambda w: pl.BlockSpec((tm, w), lambda k: (jnp.minimum(k, n_tiles - 1), 0))
    prev = lambda w: pl.BlockSpec((tm, w), lambda k: (jnp.maximum(k - 1, 0), 0))
    per_seq = lambda r, w: pl.BlockSpec((1, r, w), lambda k: (jnp.maximum(k - 1, 0) // tps, 0, 0))

    def cast_spec(a):
        rb = _cast_rows(a.shape[0], n_tiles)
        return pl.BlockSpec((rb, a.shape[1]), lambda k: (jnp.minimum(k, a.shape[0] // rb - 1), 0))

    cast_in_specs = [cast_spec(a) for a in cast_ws]
    cast_out_specs = [cast_spec(a) for a in cast_ws]
    proj_scratch = [pltpu.VMEM((n_blk, w_qvo, LANES), BF16), pltpu.VMEM((n_blk, 2 * M_HEADS, LANES), F32),
                    pltpu.VMEM((n_blk, A_Q_W, LANES), BF16), pltpu.VMEM((tm, M_QK_W), BF16),
                    pltpu.VMEM((tm, 2 * A_KV_W), F32)]
    return pl.pallas_call(
        functools.partial(_front_kernel, tps, len(cast_ws)),
        grid=(n_tiles + 1,),
        in_specs=[pl.BlockSpec(memory_space=pltpu.SMEM), cur(D_MODEL)]
                 + [_const_spec(a.shape) for a in (nw, wmt, wqgt, wn, bd, qcol, krow, bias_col, nwm_col)]
                 + cast_in_specs,
        out_specs=[cur(2 * D_MODEL), prev(M_V_W), prev(A_Q_W), per_seq(WINDOW, A_KV_W), per_seq(WINDOW, A_KV_W),
                   per_seq(S_ROWS, M_QK_W), per_seq(SUBLANES, LANES)] + cast_out_specs,
        out_shape=[jax.ShapeDtypeStruct((n, 2 * D_MODEL), BF16),
                   jax.ShapeDtypeStruct((n, M_V_W), BF16),
                   jax.ShapeDtypeStruct((n, A_Q_W), BF16),
                   jax.ShapeDtypeStruct((nb, WINDOW, A_KV_W), F32),
                   jax.ShapeDtypeStruct((nb, WINDOW, A_KV_W), F32),
                   jax.ShapeDtypeStruct((nb, S_ROWS, M_QK_W), F32),
                   jax.ShapeDtypeStruct((nb, SUBLANES, LANES), F32)]
                  + [jax.ShapeDtypeStruct(a.shape, BF16) for a in cast_ws],
        scratch_shapes=proj_scratch + proj_scratch + [pltpu.VMEM((S_ROWS, M_QK_W), F32),
                                                       pltpu.VMEM((SUBLANES, LANES), F32),
                                                       pltpu.VMEM((WINDOW, 2 * A_KV_W), F32)],
        compiler_params=_params(("arbitrary",)),
        name="front",
    )(sinks, x2d, nw, wmt, wqgt, wn, bd, qcol, krow, bias_col, nwm_col, *cast_ws)


def _swa_sample_kernel(seq_len, sink_ref, qt_ref, kv_ref, ck_ref, cv_ref, h_ref, kwin_ref, vwin_ref):
    n_seq = GROUP // seq_len
    wb = ck_ref.shape[2]
    lane = lax.broadcasted_iota(jnp.int32, (1, LANES), 1)
    lo_half = lane < HALF
    q_rows = qt_ref[0].astype(F32).T
    kv_new = kv_ref[...]
    k_new = kv_new[:, 0:A_KV_W].reshape(n_seq, seq_len, A_KV_W)
    v_new = kv_new[:, A_KV_W:].reshape(n_seq, seq_len, A_KV_W)

    def to_kv_half(x, head):
        kvh = head // A_GROUPS
        if head % 2 != kvh:
            x = pltpu.roll(x, HALF, axis=1)
        return jnp.where(lo_half if kvh == 0 else ~lo_half, x, 0.0)

    lhs = jnp.concatenate(
        [to_kv_half(q_rows[:, (h // 2) * LANES:(h // 2 + 1) * LANES], h).reshape(n_seq, seq_len, LANES)
         for h in range(A_HEADS)], axis=1).astype(BF16)
    zpad = jnp.zeros((n_seq, BF16_ROWS - seq_len, A_KV_W), F32)
    k_nb = jnp.concatenate([k_new, zpad], axis=1).astype(BF16)
    v_nb = jnp.concatenate([v_new, zpad], axis=1).astype(BF16)
    s_c = jnp.einsum('sqf,sfk->sqk', lhs, ck_ref[...].astype(BF16), preferred_element_type=F32)
    s_n = jnp.einsum('sqf,skf->sqk', lhs, k_nb, preferred_element_type=F32)
    nrow = A_HEADS * seq_len
    ti = lax.broadcasted_iota(jnp.int32, (nrow, wb), 0) & (seq_len - 1)
    ki = lax.broadcasted_iota(jnp.int32, (nrow, wb), 1)
    mask_c = (ti + wb - ki) < WINDOW
    ti_n = lax.broadcasted_iota(jnp.int32, (nrow, BF16_ROWS), 0) & (seq_len - 1)
    ki_n = lax.broadcasted_iota(jnp.int32, (nrow, BF16_ROWS), 1)
    mask_n = ki_n <= ti_n
    row_head = lax.broadcasted_iota(jnp.int32, (nrow, 1), 0) >> (seq_len.bit_length() - 1)
    sk = jnp.zeros((nrow, 1), F32)
    for h in range(A_HEADS):
        sk = jnp.where(row_head == h, sink_ref[h], sk)
    s_c = jnp.where(mask_c, s_c, -jnp.inf)
    s_n = jnp.where(mask_n, s_n, -jnp.inf)
    mx = jnp.maximum(jnp.maximum(jnp.max(s_c, axis=-1, keepdims=True), jnp.max(s_n, axis=-1, keepdims=True)), sk)
    p_c = jnp.exp(s_c - mx)
    p_n = jnp.exp(s_n - mx)
    den = jnp.sum(p_c, axis=-1, keepdims=True) + jnp.sum(p_n, axis=-1, keepdims=True) + jnp.exp(sk - mx)
    o = (jnp.einsum('sqk,sfk->sqf', p_c.astype(BF16), cv_ref[...].astype(BF16), preferred_element_type=F32)
         + jnp.einsum('sqk,skf->sqf', p_n.astype(BF16), v_nb, preferred_element_type=F32)) / den

    def from_kv_half(head):
        x = o[:, head * seq_len:(head + 1) * seq_len, :].reshape(GROUP, LANES)
        return pltpu.roll(x, HALF, axis=1) if head % 2 != head // A_GROUPS else x

    for c in range(A_HEADS // 2):
        h_ref[:, c * LANES:(c + 1) * LANES] = jnp.where(lo_half, from_kv_half(2 * c),
                                                        from_kv_half(2 * c + 1)).astype(h_ref.dtype)

    kt_new, vt_new = kv_new[:, 0:A_KV_W].T, kv_new[:, A_KV_W:].T
    for s in range(n_seq):
        put = (wb - seq_len - s * seq_len) % LANES
        for new_t, c_ref, win_ref in ((kt_new, ck_ref, kwin_ref), (vt_new, cv_ref, vwin_ref)):
            win_ref[s] = jnp.where(lane >= wb - seq_len, pltpu.roll(new_t, put, axis=1) if put else new_t,
                                   pltpu.roll(c_ref[s], wb - seq_len, axis=1))


def _swa_sample(qat, kv_a, cache_k, cache_v, sinks, seq_len):
    ngrp = qat.shape[0]
    n_seq = GROUP // seq_len
    wb = cache_k.shape[2]
    assert wb == LANES, "window positions fill one lane-width"
    row = lambda w: pl.BlockSpec((GROUP, w), lambda i: (i, 0))
    cache = pl.BlockSpec((n_seq, A_KV_W, wb), lambda i: (i, 0, 0))
    return pl.pallas_call(
        functools.partial(_swa_sample_kernel, seq_len),
        grid=(ngrp,),
        in_specs=[pl.BlockSpec(memory_space=pltpu.SMEM), pl.BlockSpec((1, A_Q_W, LANES), lambda i: (i, 0, 0)),
                  row(2 * A_KV_W), cache, cache],
        out_specs=[row(A_Q_W), cache, cache],
        out_shape=[jax.ShapeDtypeStruct((ngrp * GROUP, A_Q_W), BF16),
                   jax.ShapeDtypeStruct(cache_k.shape, F32),
                   jax.ShapeDtypeStruct(cache_v.shape, F32)],
        compiler_params=_params(("arbitrary",)),
        name="swa_sample",
    )(sinks, qat, kv_a, cache_k, cache_v)


def _merge_ffn_kernel(x_ref, hm_ref, ha_ref, gab_ref, wa_ref, wb_ref, wo_ref, nw_ref, wg_ref, wu_ref, wd_ref,
                      y_ref):
    tm = x_ref.shape[0]
    halves = [slice(0, tm // 2), slice(tm // 2, tm)]
    mix = []
    for rs in halves:
        ga = jax.nn.sigmoid(gab_ref[rs, 0:D_MODEL].astype(F32))
        gb = jax.nn.sigmoid(gab_ref[rs, D_MODEL:].astype(F32))
        mix.append(ga * _dot(hm_ref[rs, :], wa_ref[...]) + gb * _dot(ha_ref[rs, :], wb_ref[...]))
    x1 = [x_ref[rs, :] + _dot(m.astype(BF16), wo_ref[...]) for rs, m in zip(halves, mix)]
    hf = [_rms_rows(v, nw_ref[...]).astype(BF16) for v in x1]
    act = [(jax.nn.silu(_dot(h, wg_ref[...])) * _dot(h, wu_ref[...])).astype(BF16) for h in hf]
    for rs, v, a in zip(halves, x1, act):
        y_ref[rs, :] = v + _dot(a, wd_ref[...])


def _merge_ffn(x2d, h_m, h_a, g_ab, wa, wb, wo, nw, wg, wu, wd, tm):
    n = x2d.shape[0]
    row = lambda w: pl.BlockSpec((tm, w), lambda i: (i, 0))
    return pl.pallas_call(
        _merge_ffn_kernel,
        grid=(n // tm,),
        in_specs=[row(D_MODEL), row(M_V_W), row(A_Q_W), row(2 * D_MODEL)]
                 + [_const_spec(w.shape) for w in (wa, wb, wo, nw, wg, wu, wd)],
        out_specs=row(D_MODEL),
        out_shape=jax.ShapeDtypeStruct((n, D_MODEL), F32),
        compiler_params=_params(("arbitrary",)),
        name="merge_ffn",
    )(x2d, h_m, h_a, g_ab, wa, wb, wo, nw, wg, wu, wd)


def kernel(x_prompt, x_sample, state_mlstm_C, state_mlstm_n, state_mlstm_m, cache_swa_k, cache_swa_v,
           norm_mix_w, w_in, mlstm_i_bias, mlstm_f_bias, mlstm_norm_w, q_norm_w, k_norm_w, attn_sinks,
           w_branch_a, w_branch_b, w_out, norm_ffn_w, w_gate, w_up, w_down):
    depth = w_in.shape[0]
    assert depth == 1, "single trunk layer"
    l = 0
    bp, tp = x_prompt.shape[0], x_prompt.shape[1]
    bs, ts = x_sample.shape[0], x_sample.shape[1]
    assert tp % TOKEN_TILE == 0 and (bs * ts) % GROUP == 0 and GROUP % ts == 0 and ts & (ts - 1) == 0
    assert ts <= SUBLANES, "sample chunk must fit one sublane tile"

    wt = jnp.transpose(w_in[l])
    c_km, c_vm = M_QK_W, 2 * M_QK_W
    c_g = 2 * M_QK_W + 2 * M_V_W
    c_qa = c_g + 2 * M_HEADS
    c_ka = c_qa + A_Q_W
    gate_pad = jnp.zeros((BF16_ROWS - 2 * M_HEADS, D_MODEL), F32)
    wmt = jnp.concatenate([wt[0:c_km], wt[c_vm:c_g]], axis=0).astype(BF16)
    wqgt = jnp.concatenate([wt[c_qa:c_ka], wt[c_g:c_qa], gate_pad], axis=0).astype(BF16)
    wn = jnp.concatenate([wt[c_km:c_vm], wt[c_ka:]], axis=0).astype(BF16)
    head_of = jnp.arange(A_KV_W) // A_HEAD_DIM
    bd = (head_of[:, None] == head_of[None, :]).astype(BF16)
    qcol = (jnp.tile(q_norm_w[l], A_HEADS) * (A_HEAD_DIM ** -0.5)).reshape(A_Q_W, 1)
    krow = jnp.tile(k_norm_w[l], A_KV_HEADS).reshape(1, A_KV_W)
    nw_mix = norm_mix_w[l].reshape(1, D_MODEL)
    nw_ffn = norm_ffn_w[l].reshape(1, D_MODEL)
    bias_col = jnp.concatenate([mlstm_i_bias[l], mlstm_f_bias[l]]).reshape(2 * M_HEADS, 1)
    nw_col = mlstm_norm_w[l].reshape(M_V_W, 1)
    sinks = attn_sinks[l]
    proj = lambda x2d, tm: _proj(x2d, nw_mix, wmt, wqgt, wn, bd, qcol, krow, tm)

    xp = x_prompt.reshape(bp * tp, D_MODEL)
    later_ws = (w_branch_a[l], w_branch_b[l], w_out[l], w_gate[l], w_up[l], w_down[l])
    g_ab, h_m, h_a, kwin_p, vwin_p, st_p, m_p, wa, wb, wo, wg, wu, wd = _front(
        xp, bp, sinks, nw_mix, wmt, wqgt, wn, bd, qcol, krow, bias_col, nw_col, later_ws, TOKEN_TILE)
    merge = lambda x2d, h_m, h_a, g_ab, tm: _merge_ffn(x2d, h_m, h_a, g_ab, wa, wb, wo, nw_ffn, wg, wu, wd, tm)
    yp = merge(xp, h_m, h_a, g_ab, TOKEN_TILE).reshape(bp, tp, D_MODEL)
    c_p = jnp.swapaxes(st_p[:, :M_DV, :], 1, 2).reshape(bp, M_HEADS, M_DK, M_DV)
    n_p = st_p[:, M_DV, :].reshape(bp, M_HEADS, M_DK)
    m_pr = m_p[:, :M_HEADS, 0]

    ns = bs * ts
    xs = x_sample.reshape(ns, D_MODEL)
    tms = TOKEN_TILE if ns % TOKEN_TILE == 0 else GROUP
    qvot, gt, qat, k_m, kv_a, g_ab = proj(xs, tms)
    ngrp = ns // GROUP
    m_lanes = jnp.repeat(state_mlstm_m[l], ts, axis=0).reshape(ngrp, GROUP, M_HEADS)
    mrow = jnp.pad(jnp.swapaxes(m_lanes, 1, 2), ((0, 0), (0, SUBLANES - M_HEADS), (0, 0)))
    h_m, c_s, n_s, mt_s = _mlstm_sample(qvot, gt, k_m, mrow, bias_col, nw_col,
                                        state_mlstm_C[l].reshape(bs, M_QK_W, M_DV),
                                        state_mlstm_n[l].reshape(bs, M_QK_W), ts)
    wbuf = cache_swa_k.shape[2]
    to_fm = lambda a: jnp.transpose(a, (0, 2, 3, 1)).reshape(bs, A_KV_W, wbuf)
    from_fm = lambda a: jnp.transpose(a.reshape(bs, A_KV_HEADS, A_HEAD_DIM, wbuf), (0, 3, 1, 2))[None]
    h_a, kwin_s, vwin_s = _swa_sample(qat, kv_a, to_fm(cache_swa_k[l]), to_fm(cache_swa_v[l]), sinks, ts)
    ys = merge(xs, h_m, h_a, g_ab, tms).reshape(bs, ts, D_MODEL)
    m_s = jnp.swapaxes(mt_s[:, :M_HEADS, :], 1, 2).reshape(bs, ts, M_HEADS)[:, ts - 1, :]

    kv5 = lambda a: a.reshape(a.shape[0], a.shape[1], A_KV_HEADS, A_HEAD_DIM)[None]
    return (yp, ys,
            c_p[None], n_p[None], m_pr[None], kv5(kwin_p), kv5(vwin_p),
            c_s.reshape(bs, M_HEADS, M_DK, M_DV)[None], n_s.reshape(bs, M_HEADS, M_DK)[None], m_s[None],
            from_fm(kwin_s), from_fm(vwin_s))
```

```python
import functools

import jax
import jax.numpy as jnp
from jax import lax
from jax.experimental import pallas as pl
from jax.experimental.pallas import tpu as pltpu

F32 = jnp.float32
BF16 = jnp.bfloat16

D_MODEL = 1024
M_HEADS = 4
M_DK = 64
M_DV = 128
M_QK_W = M_HEADS * M_DK
M_V_W = M_HEADS * M_DV
A_HEADS = 8
A_KV_HEADS = 2
A_HEAD_DIM = 64
A_GROUPS = A_HEADS // A_KV_HEADS
A_Q_W = A_HEADS * A_HEAD_DIM
A_KV_W = A_KV_HEADS * A_HEAD_DIM
WINDOW = 128
EPS = 1e-6

LANES = 128
SUBLANES = 8
BF16_ROWS = 16
GROUP = 128
S_ROWS = M_DV + BF16_ROWS
TOKEN_TILE = 512
VMEM_LIMIT = 56 * 1024 * 1024
DK_SHIFT = M_DK.bit_length() - 1
WINDOW_SHIFT = WINDOW.bit_length() - 1
HALF = LANES // 2
assert A_HEAD_DIM == HALF and A_KV_W == LANES, "attention head pairs share one lane-width"

NT_DIMS = (((1,), (1,)), ((), ()))


def _dot(a, b):
    return jnp.dot(a, b, preferred_element_type=F32)


def _dot_nt(a, b):
    return lax.dot_general(a, b, NT_DIMS, preferred_element_type=F32)


def _const_spec(shape):
    nd = len(shape)
    return pl.BlockSpec(shape, lambda *_: (0,) * nd, pipeline_mode=pl.Buffered(1))


def _params(sem):
    return pltpu.CompilerParams(dimension_semantics=sem, vmem_limit_bytes=VMEM_LIMIT)


def _rms_rows(x, nw):
    ms = jnp.mean(x * x, axis=-1, keepdims=True)
    return (x * lax.rsqrt(ms + EPS)) * nw


GAB_CHUNK = 512


def _proj_chunks(x_ref, nw_ref, wmt_ref, wqgt_ref, wn_ref, bd_ref, qcol_ref, krow_ref,
                 qvot_ref, gt_ref, qat_ref, km_ref, kva_ref, gab_ref):
    hn = _rms_rows(x_ref[...], nw_ref[...]).astype(BF16)
    n_blk = qat_ref.shape[0]

    def put(ref, rows, val):
        for c in range(n_blk):
            ref[c, rows, :] = val[:, c * LANES:(c + 1) * LANES].astype(ref.dtype)

    def mlstm_qvo():
        put(qvot_ref, slice(None), _dot_nt(wmt_ref[...], hn))

    def attn_q_and_gates():
        qt = _dot_nt(wqgt_ref[...], hn)
        put(gt_ref, slice(None), qt[A_Q_W:A_Q_W + 2 * M_HEADS])
        for h in range(A_HEADS):
            hs = slice(h * A_HEAD_DIM, (h + 1) * A_HEAD_DIM)
            blk = qt[hs]
            ssq_q = jnp.sum(blk * blk, axis=0, keepdims=True)
            put(qat_ref, hs, (blk * lax.rsqrt(ssq_q * (1.0 / A_HEAD_DIM) + EPS)) * qcol_ref[hs])

    def mlstm_k():
        km_ref[...] = _dot_nt(hn, wn_ref[0:M_QK_W, :]).astype(km_ref.dtype)

    def branch_gates(c0):
        def run():
            w0 = M_QK_W + 2 * A_KV_W + c0
            gab_ref[:, c0:c0 + GAB_CHUNK] = _dot_nt(hn, wn_ref[w0:w0 + GAB_CHUNK, :]).astype(gab_ref.dtype)
        return run

    def attn_kv():
        kv = _dot_nt(hn, wn_ref[M_QK_W:M_QK_W + 2 * A_KV_W, :])
        k = kv[:, 0:A_KV_W]
        ksq = k * k
        hi = ksq.astype(BF16)
        lo = (ksq - hi.astype(F32)).astype(BF16)
        ssq = _dot(hi, bd_ref[...]) + _dot(lo, bd_ref[...])
        kva_ref[:, 0:A_KV_W] = (k * lax.rsqrt(ssq * (1.0 / A_HEAD_DIM) + EPS)) * krow_ref[...]
        kva_ref[:, A_KV_W:] = kv[:, A_KV_W:]

    return ([mlstm_qvo, attn_q_and_gates, mlstm_k]
            + [branch_gates(c0) for c0 in range(0, gab_ref.shape[1], GAB_CHUNK)] + [attn_kv])


def _proj_kernel(*refs):
    for piece in _proj_chunks(*refs):
        piece()


def _proj(x2d, nw, wmt, wqgt, wn, bd, qcol, krow, tm):
    n = x2d.shape[0]
    row = lambda w: pl.BlockSpec((tm, w), lambda i: (i, 0))
    n_blk = tm // LANES
    slab = lambda r: pl.BlockSpec((n_blk, r, LANES), lambda i: (i, 0, 0))
    w_qvo = M_QK_W + 2 * M_V_W
    return pl.pallas_call(
        _proj_kernel,
        grid=(n // tm,),
        in_specs=[row(D_MODEL)] + [_const_spec(a.shape) for a in (nw, wmt, wqgt, wn, bd, qcol, krow)],
        out_specs=[slab(w_qvo), slab(2 * M_HEADS), slab(A_Q_W), row(M_QK_W), row(2 * A_KV_W), row(2 * D_MODEL)],
        out_shape=[jax.ShapeDtypeStruct((n // LANES, w_qvo, LANES), BF16),
                   jax.ShapeDtypeStruct((n // LANES, 2 * M_HEADS, LANES), F32),
                   jax.ShapeDtypeStruct((n // LANES, A_Q_W, LANES), BF16),
                   jax.ShapeDtypeStruct((n, M_QK_W), BF16),
                   jax.ShapeDtypeStruct((n, 2 * A_KV_W), F32),
                   jax.ShapeDtypeStruct((n, 2 * D_MODEL), BF16)],
        compiler_params=_params(("arbitrary",)),
        name="proj",
    )(x2d, nw, wmt, wqgt, wn, bd, qcol, krow)


def _split3_rows(x):
    hi = x.astype(BF16).astype(F32)
    r1 = x - hi
    mid = r1.astype(BF16).astype(F32)
    lo = r1 - mid
    return jnp.concatenate([hi, mid, lo], axis=0).astype(BF16)


def _log_sigmoid(x):
    return jnp.minimum(x, 0.0) - jnp.log1p(jnp.exp(-jnp.abs(x)))


def _chunk_masks(chunk_shift):
    s = lax.broadcasted_iota(jnp.int32, (GROUP, GROUP), 0)
    t = lax.broadcasted_iota(jnp.int32, (GROUP, GROUP), 1)
    same = (s >> chunk_shift) == (t >> chunk_shift)
    return same, same & (s <= t)


def _groups_gates(gts, same, causal):
    n = len(gts)
    assert 2 * M_HEADS == SUBLANES and n * SUBLANES <= GROUP
    cm_bf = jnp.where(causal, 1.0, 0.0).astype(BF16)
    lf = _log_sigmoid(jnp.concatenate(gts, axis=0))
    nr = n * SUBLANES
    bt3 = _dot(_split3_rows(lf), cm_bf)
    bt = (bt3[0:nr] + bt3[nr:2 * nr]) + bt3[2 * nr:3 * nr]
    b4 = [bt[g * SUBLANES + M_HEADS:(g + 1) * SUBLANES] for g in range(n)]
    a4 = [gts[g][0:M_HEADS] - b4[g] for g in range(n)]
    a_rows = [a for g in range(n) for a in (a4[g], a4[g])]
    if nr < GROUP:
        a_rows.append(jnp.zeros((GROUP - nr, GROUP), F32))
    a_cols = jnp.concatenate(a_rows, axis=0).T
    out = []
    for g in range(n):
        at_mats, run_rows, chunk_rows = [], [], []
        for h in range(M_HEADS):
            c = g * SUBLANES + h
            at = jnp.broadcast_to(a_cols[:, c:c + 1], (GROUP, GROUP))
            at_mats.append(at)
            run_rows.append(jnp.max(jnp.where(causal, at, -jnp.inf), axis=0, keepdims=True))
            chunk_rows.append(jnp.max(jnp.where(same, at, -jnp.inf), axis=0, keepdims=True))
        out.append((b4[g], a4[g], at_mats, jnp.concatenate(run_rows, axis=0), jnp.concatenate(chunk_rows, axis=0)))
    return out


def _group_weights(m_prev, b4, a4, run4, chunk4):
    big_m = jnp.maximum(m_prev, run4)
    m_last = jnp.maximum(m_prev, chunk4)
    w_inter = jnp.exp(m_prev - big_m)
    g_vec = jnp.exp(m_prev - m_last)
    w_last = jnp.exp(a4 - m_last)
    m_t = b4 + big_m
    return big_m, w_inter, g_vec, w_last, m_t, jnp.exp(-m_t)


def _group_scores(qvot, ks):
    lane_head = lax.broadcasted_iota(jnp.int32, (1, M_QK_W), 1) >> DK_SHIFT
    row_head = lax.broadcasted_iota(jnp.int32, (M_QK_W, 1), 0) >> DK_SHIFT
    qt = qvot(0, M_QK_W)
    k_stack = jnp.concatenate([jnp.where(lane_head == h, ks, jnp.zeros_like(ks)) for h in range(M_HEADS)], axis=0)
    qw = jnp.concatenate([jnp.where(row_head == h, qt, jnp.zeros_like(qt)) for h in range(M_HEADS)], axis=1)
    zero_blk = jnp.zeros((M_DK, GROUP), BF16)
    sc_t = []
    for h in range(0, M_HEADS, 2):
        q_pair = jnp.concatenate(
            [jnp.concatenate([qvot(h * M_DK, (h + 1) * M_DK), zero_blk], axis=1),
             jnp.concatenate([zero_blk, qvot((h + 1) * M_DK, (h + 2) * M_DK)], axis=1)], axis=0)
        sc = _dot(ks[:, h * M_DK:(h + 2) * M_DK], q_pair)
        sc_t += [sc[:, 0:GROUP], sc[:, GROUP:]]
    return k_stack, qw, sc_t


def _group_values(qvot, sc_t, at_mats, big_m, causal):
    ones_rows = jnp.where(lax.broadcasted_iota(jnp.int32, (BF16_ROWS, GROUP), 0) == 0, 1.0, 0.0).astype(BF16)
    zero_blk = jnp.zeros((GROUP, GROUP), BF16)
    vta, s_t = [], []
    for h in range(M_HEADS):
        w_t = jnp.where(causal, jnp.exp(at_mats[h] - big_m[h:h + 1]), 0.0)
        s_t.append((sc_t[h] * w_t).astype(BF16))
        vta.append(jnp.concatenate([qvot(M_QK_W + h * M_DV, M_QK_W + (h + 1) * M_DV), ones_rows], axis=0))
    intra = []
    for h in range(0, M_HEADS, 2):
        pair = jnp.concatenate([jnp.concatenate([s_t[h], zero_blk], axis=1),
                                jnp.concatenate([zero_blk, s_t[h + 1]], axis=1)], axis=0)
        intra.append(_dot(jnp.concatenate(vta[h:h + 2], axis=1), pair))
    return vta, jnp.concatenate(intra, axis=1)


def _weighted_values(vta, w_rows):
    return jnp.concatenate([(vta[h].astype(F32) * w_rows[h:h + 1]).astype(BF16) for h in range(M_HEADS)], axis=1)


def _lanes_x(rows):
    return jnp.concatenate([rows[h:h + 1] for h in range(M_HEADS)], axis=1)


def _group_out(qvot, inter, intra, w_inter, e_neg_m, nw_ref):
    outs = []
    out_all = inter * _lanes_x(w_inter) + intra
    for h in range(M_HEADS):
        out_t = out_all[:, h * GROUP:(h + 1) * GROUP]
        hh = out_t[0:M_DV] / jnp.maximum(jnp.abs(out_t[M_DV:M_DV + 1]), e_neg_m[h:h + 1])
        ms = jnp.mean(hh * hh, axis=0, keepdims=True)
        hn = (hh * lax.rsqrt(ms + EPS)) * nw_ref[h * M_DV:(h + 1) * M_DV]
        o_t = qvot(M_QK_W + M_V_W + h * M_DV, M_QK_W + M_V_W + (h + 1) * M_DV)
        outs.append(hn * jax.nn.sigmoid(o_t.astype(F32)))
    return jnp.concatenate(outs, axis=0).T


def _decay_row(g_vec, lane0):
    lane_head = lax.broadcasted_iota(jnp.int32, (1, M_QK_W), 1) >> DK_SHIFT
    g_row = jnp.zeros((1, M_QK_W), F32)
    for h in range(M_HEADS):
        g_row = jnp.where(lane_head == h, g_vec[h:h + 1, lane0:lane0 + 1], g_row)
    return g_row


def _mlstm_sample_kernel(seq_len, qvot_ref, gt_ref, k_ref, mrow_ref, bias_ref, nw_ref, c_ref, n_ref,
                         h_ref, c_out_ref, n_out_ref, mt_ref):
    n_seq = GROUP // seq_len
    shift = seq_len.bit_length() - 1
    same, causal = _chunk_masks(shift)
    lane_seq = lax.broadcasted_iota(jnp.int32, (1, GROUP), 1) >> shift
    lane_seq_x = jnp.concatenate([lane_seq] * M_HEADS, axis=1)
    row0 = lax.broadcasted_iota(jnp.int32, (BF16_ROWS, M_QK_W), 0) == 0
    qvot = lambda r0, r1: qvot_ref[0, r0:r1, :]
    ks = k_ref[...] * (M_DK ** -0.5)
    (b4, a4, at_mats, run4, chunk4), = _groups_gates([gt_ref[0] + bias_ref[...]], same, causal)
    big_m, w_inter, g_vec, w_last, m_t, e_neg_m = _group_weights(mrow_ref[0, 0:M_HEADS, :], b4, a4, run4, chunk4)
    mt_ref[0] = jnp.concatenate([m_t, m_t], axis=0)
    k_stack, qw, sc_t = _group_scores(qvot, ks)

    st_old = []
    for s in range(n_seq):
        n_rows = jnp.where(row0, jnp.broadcast_to(n_ref[s:s + 1, :], (BF16_ROWS, M_QK_W)), 0.0)
        st_old.append(jnp.concatenate([c_ref[s].T, n_rows], axis=0))
    inter_all = _dot(jnp.concatenate(st_old, axis=0).astype(BF16), qw)
    vta, intra = _group_values(qvot, sc_t, at_mats, big_m, causal)
    inter = inter_all[0:S_ROWS]
    for s in range(1, n_seq):
        inter = jnp.where(lane_seq_x == s, inter_all[s * S_ROWS:(s + 1) * S_ROWS], inter)
    tall = jnp.concatenate([_weighted_values(vta, jnp.where(lane_seq == s, w_last, 0.0)) for s in range(n_seq)],
                           axis=0)
    d_st = _dot(tall, k_stack)
    for s in range(n_seq):
        st_new = _decay_row(g_vec, s * seq_len) * st_old[s] + d_st[s * S_ROWS:(s + 1) * S_ROWS]
        c_out_ref[s] = st_new[0:M_DV].T
        n_out_ref[s:s + 1, :] = st_new[M_DV:M_DV + 1]
    h_ref[...] = _group_out(qvot, inter, intra, w_inter, e_neg_m, nw_ref).astype(h_ref.dtype)


def _mlstm_sample(qvot, gt, k_m, mrow, bias_col, nw_col, c, n, seq_len):
    ngrp = qvot.shape[0]
    n_seq = GROUP // seq_len
    full = lambda a: pl.BlockSpec(a.shape, lambda i: (0,) * a.ndim)
    slab = lambda a: pl.BlockSpec((1,) + a.shape[1:], lambda i: (i, 0, 0))
    row = lambda w: pl.BlockSpec((GROUP, w), lambda i: (i, 0))
    c_spec = pl.BlockSpec((n_seq,) + c.shape[1:], lambda i: (i, 0, 0))
    n_spec = pl.BlockSpec((n_seq, n.shape[1]), lambda i: (i, 0))
    return pl.pallas_call(
        functools.partial(_mlstm_sample_kernel, seq_len),
        grid=(ngrp,),
        in_specs=[slab(qvot), slab(gt), row(M_QK_W), slab(mrow), full(bias_col), full(nw_col), c_spec, n_spec],
        out_specs=[row(M_V_W), c_spec, n_spec, slab(mrow)],
        out_shape=[jax.ShapeDtypeStruct((ngrp * GROUP, M_V_W), BF16),
                   jax.ShapeDtypeStruct(c.shape, F32),
                   jax.ShapeDtypeStruct(n.shape, F32),
                   jax.ShapeDtypeStruct(mrow.shape, F32)],
        compiler_params=_params(("arbitrary",)),
        name="mlstm_sample",
    )(qvot, gt, k_m, mrow, bias_col, nw_col, c, n)


def _swa_scores(qbs, has_prev, sink_ref, qt_ref, kv_ref, kvp_ref):
    hd = A_HEAD_DIM
    nq = A_GROUPS * WINDOW
    si = lax.broadcasted_iota(jnp.int32, (2 * WINDOW, nq), 0)
    qi = lax.broadcasted_iota(jnp.int32, (2 * WINDOW, nq), 1) & (WINDOW - 1)
    local = ((si < WINDOW) & (si > qi)) | ((si >= WINDOW) & (si - WINDOW <= qi))
    first = local & (has_prev | (si >= WINDOW))
    lane_grp = lax.broadcasted_iota(jnp.int32, (1, nq), 1) >> WINDOW_SHIFT
    kv_block = lambda i: kvp_ref[...] if i == 0 else kv_ref[(i - 1) * WINDOW:i * WINDOW, :]
    k_bf = lambda i: kv_block(i)[:, 0:A_KV_W].astype(BF16)
    vt_bf = lambda i: kv_block(i)[:, A_KV_W:].T.astype(BF16)
    zeros = jnp.zeros((hd, nq), BF16)
    sinks = []
    for kvh in range(A_KV_HEADS):
        sk = jnp.zeros((1, nq), F32)
        for g in range(A_GROUPS):
            sk = jnp.where(lane_grp == g, sink_ref[kvh * A_GROUPS + g], sk)
        sinks.append(sk)
    units = []
    for qb in qbs:
        kk = jnp.concatenate([k_bf(qb), k_bf(qb + 1)], axis=0)
        vt = jnp.concatenate([vt_bf(qb), vt_bf(qb + 1)], axis=1)
        mask = local if qb > 0 else first
        for kvh in range(A_KV_HEADS):
            q4t = jnp.concatenate(
                [qt_ref[qb, (kvh * A_GROUPS + g) * hd:(kvh * A_GROUPS + g + 1) * hd, :] for g in range(A_GROUPS)],
                axis=1)
            wq = jnp.concatenate([q4t, zeros] if kvh == 0 else [zeros, q4t], axis=0)
            units.append((jnp.where(mask, _dot(kk, wq), -jnp.inf), sinks[kvh], vt))
    return units


def _swa_finish(qbs, units, h_ref):
    hd = A_HEAD_DIM
    for i, qb in enumerate(qbs):
        pieces = []
        for kvh in range(A_KV_HEADS):
            s, sk, vt = units[i * A_KV_HEADS + kvh]
            mx = jnp.maximum(jnp.max(s, axis=0, keepdims=True), sk)
            p = jnp.exp(s - mx)
            den = jnp.sum(p, axis=0, keepdims=True) + jnp.exp(sk - mx)
            ot = _dot(vt[kvh * hd:(kvh + 1) * hd], p.astype(BF16)) / den
            pieces += [ot[:, g * WINDOW:(g + 1) * WINDOW] for g in range(A_GROUPS)]
        h_t = jnp.concatenate(pieces, axis=0)
        h_ref[qb * WINDOW:(qb + 1) * WINDOW, :] = h_t.T.astype(h_ref.dtype)


def _front_kernel(tiles_per_seq, n_cast, *refs):
    (sink_ref, x_ref, nw_ref, wmt_ref, wqgt_ref, wn_ref, bd_ref, qcol_ref, krow_ref, bias_ref, nwm_ref) = refs[:11]
    cast_in = refs[11:11 + n_cast]
    (gab_ref, hm_ref, ha_ref, kwin_ref, vwin_ref, st_ref, m_ref) = refs[11 + n_cast:18 + n_cast]
    cast_out = refs[18 + n_cast:18 + 2 * n_cast]
    set_a, set_b = refs[18 + 2 * n_cast:23 + 2 * n_cast], refs[23 + 2 * n_cast:28 + 2 * n_cast]
    st_s, m_s, kvp_s = refs[28 + 2 * n_cast:]
    odd = (pl.program_id(0) & 1) == 1
    for parity, q_set, p_set in ((jnp.logical_not(odd), set_a, set_b), (odd, set_b, set_a)):
        pl.when(parity)(functools.partial(
            _front_body, tiles_per_seq, sink_ref, x_ref, nw_ref, wmt_ref, wqgt_ref, wn_ref, bd_ref, qcol_ref,
            krow_ref, bias_ref, nwm_ref, gab_ref, hm_ref, ha_ref, kwin_ref, vwin_ref, st_ref, m_ref,
            *q_set, *p_set, st_s, m_s, kvp_s))
    for src, dst in zip(cast_in, cast_out):
        dst[...] = src[...].astype(dst.dtype)


def _front_body(tiles_per_seq, sink_ref, x_ref, nw_ref, wmt_ref, wqgt_ref, wn_ref, bd_ref, qcol_ref, krow_ref,
                bias_ref, nwm_ref,
                gab_ref, hm_ref, ha_ref, kwin_ref, vwin_ref, st_ref, m_ref,
                q_qvot, q_gt, q_qat, q_km, q_kva, p_qvot, p_gt, p_qat, p_km, p_kva, st_s, m_s, kvp_s):
    k = pl.program_id(0)

    @pl.when(k == 0)
    def _():
        for r in (p_qvot, p_gt, p_qat, p_km, p_kva, st_s, m_s, kvp_s):
            r[...] = jnp.zeros(r.shape, r.dtype)

    pieces = _proj_chunks(x_ref, nw_ref, wmt_ref, wqgt_ref, wn_ref, bd_ref, qcol_ref, krow_ref,
                          q_qvot, q_gt, q_qat, q_km, q_kva, gab_ref)

    seq_start = lax.rem(k - 1 + tiles_per_seq, tiles_per_seq) == 0
    n_blk = p_qat.shape[0]
    same, causal = _chunk_masks(GROUP.bit_length() - 1)
    st = jnp.where(seq_start, 0.0, st_s[...])
    m_col = jnp.where(seq_start, 0.0, m_s[0:M_HEADS, 0:1])

    qvots = [lambda r0, r1, g=g: p_qvot[g, r0:r1, :] for g in range(n_blk)]
    scores = [_group_scores(qvots[g], p_km[g * GROUP:(g + 1) * GROUP, :] * (M_DK ** -0.5))
              for g in range(n_blk)]
    inter = [_dot(st.astype(BF16), scores[0][1])]
    gates = _groups_gates([p_gt[g] + bias_ref[...] for g in range(n_blk)], same, causal)
    for piece in pieces[0:1]:
        piece()
    grp = []
    for g in range(n_blk):
        b4, a4, at_mats, run4, chunk4 = gates[g]
        weights = _group_weights(jnp.broadcast_to(m_col, (M_HEADS, GROUP)), b4, a4, run4, chunk4)
        m_col = weights[4][:, GROUP - 1:GROUP]
        grp.append((qvots[g], at_mats, weights) + scores[g])
    has_prev = jnp.logical_not(seq_start)
    qb_lo, qb_hi = list(range(0, n_blk // 2)), list(range(n_blk // 2, n_blk))
    units_lo = _swa_scores(qb_lo, has_prev, sink_ref, p_qat, p_kva, kvp_s)
    for piece in pieces[1:3]:
        piece()

    vals = []

    def group_values(g):
        qvot, at_mats, (big_m, _, _, w_last, _, _), k_stack, _, sc_t = grp[g]
        vta, intra = _group_values(qvot, sc_t, at_mats, big_m, causal)
        vals.append((intra, _dot(_weighted_values(vta, w_last), k_stack)))

    for g in range(0, n_blk // 2):
        group_values(g)
    _swa_finish(qb_lo[:1], units_lo[:A_KV_HEADS], ha_ref)
    for piece in pieces[3:4]:
        piece()
    for g in range(n_blk // 2, n_blk):
        group_values(g)
    _swa_finish(qb_lo[1:], units_lo[A_KV_HEADS:], ha_ref)
    units_hi = _swa_scores(qb_hi, has_prev, sink_ref, p_qat, p_kva, kvp_s)
    for piece in pieces[4:6]:
        piece()

    for g in range(n_blk):
        st = _decay_row(grp[g][2][2], 0) * st + vals[g][1]
        if g + 1 < n_blk:
            inter.append(_dot(st.astype(BF16), grp[g + 1][4]))
        if g == n_blk // 2 - 1:
            _swa_finish(qb_hi[:1], units_hi[:A_KV_HEADS], ha_ref)
            for piece in pieces[6:7]:
                piece()
    _swa_finish(qb_hi[1:], units_hi[A_KV_HEADS:], ha_ref)
    for piece in pieces[7:]:
        piece()

    for g in range(n_blk):
        qvot, _, (_, w_inter, _, _, _, e_neg_m) = grp[g][0:3]
        hm_ref[g * GROUP:(g + 1) * GROUP, :] = _group_out(qvot, inter[g], vals[g][0], w_inter, e_neg_m,
                                                          nwm_ref).astype(hm_ref.dtype)
    st_s[...] = st
    st_ref[0] = st
    m_rows = jnp.broadcast_to(m_col, (M_HEADS, LANES))
    m_rows = jnp.concatenate([m_rows, m_rows], axis=0)
    m_s[...] = m_rows
    m_ref[0] = m_rows
    tm = p_kva.shape[0]
    kwin_ref[0] = p_kva[tm - WINDOW:, 0:A_KV_W]
    vwin_ref[0] = p_kva[tm - WINDOW:, A_KV_W:]
    kvp_s[...] = p_kva[tm - WINDOW:, :]


def _cast_rows(rows, n_steps):
    rb = BF16_ROWS
    while rows % rb or rows // rb > n_steps:
        rb += BF16_ROWS
    return rb


def _front(x2d, nb, sinks, nw, wmt, wqgt, wn, bd, qcol, krow, bias_col, nwm_col, cast_ws, tm):
    n = x2d.shape[0]
    n_tiles = n // tm
    tps = n_tiles // nb
    n_blk = tm // LANES
    w_qvo = M_QK_W + 2 * M_V_W
    cur = lambda w: pl.BlockSpec((tm, w), lambda k: (jnp.minimum(k, n_tiles - 1), 0))
    prev = lambda w: pl.BlockSpec((tm, w), lambda k: (jnp.maximum(k - 1, 0), 0))
    per_seq = lambda r, w: pl.BlockSpec((1, r, w), lambda k: (jnp.maximum(k - 1, 0) // tps, 0, 0))

    def cast_spec(a):
        rb = _cast_rows(a.shape[0], n_tiles)
        return pl.BlockSpec((rb, a.shape[1]), lambda k: (jnp.minimum(k, a.shape[0] // rb - 1), 0))

    cast_in_specs = [cast_spec(a) for a in cast_ws]
    cast_out_specs = [cast_spec(a) for a in cast_ws]
    proj_scratch = [pltpu.VMEM((n_blk, w_qvo, LANES), BF16), pltpu.VMEM((n_blk, 2 * M_HEADS, LANES), F32),
                    pltpu.VMEM((n_blk, A_Q_W, LANES), BF16), pltpu.VMEM((tm, M_QK_W), BF16),
                    pltpu.VMEM((tm, 2 * A_KV_W), F32)]
    return pl.pallas_call(
        functools.partial(_front_kernel, tps, len(cast_ws)),
        grid=(n_tiles + 1,),
        in_specs=[pl.BlockSpec(memory_space=pltpu.SMEM), cur(D_MODEL)]
                 + [_const_spec(a.shape) for a in (nw, wmt, wqgt, wn, bd, qcol, krow, bias_col, nwm_col)]
                 + cast_in_specs,
        out_specs=[cur(2 * D_MODEL), prev(M_V_W), prev(A_Q_W), per_seq(WINDOW, A_KV_W), per_seq(WINDOW, A_KV_W),
                   per_seq(S_ROWS, M_QK_W), per_seq(SUBLANES, LANES)] + cast_out_specs,
        out_shape=[jax.ShapeDtypeStruct((n, 2 * D_MODEL), BF16),
                   jax.ShapeDtypeStruct((n, M_V_W), BF16),
                   jax.ShapeDtypeStruct((n, A_Q_W), BF16),
                   jax.ShapeDtypeStruct((nb, WINDOW, A_KV_W), F32),
                   jax.ShapeDtypeStruct((nb, WINDOW, A_KV_W), F32),
                   jax.ShapeDtypeStruct((nb, S_ROWS, M_QK_W), F32),
                   jax.ShapeDtypeStruct((nb, SUBLANES, LANES), F32)]
                  + [jax.ShapeDtypeStruct(a.shape, BF16) for a in cast_ws],
        scratch_shapes=proj_scratch + proj_scratch + [pltpu.VMEM((S_ROWS, M_QK_W), F32),
                                                       pltpu.VMEM((SUBLANES, LANES), F32),
                                                       pltpu.VMEM((WINDOW, 2 * A_KV_W), F32)],
        compiler_params=_params(("arbitrary",)),
        name="front",
    )(sinks, x2d, nw, wmt, wqgt, wn, bd, qcol, krow, bias_col, nwm_col, *cast_ws)


def _swa_sample_kernel(seq_len, sink_ref, qt_ref, kv_ref, ck_ref, cv_ref, h_ref, kwin_ref, vwin_ref):
    n_seq = GROUP // seq_len
    wb = ck_ref.shape[2]
    lane = lax.broadcasted_iota(jnp.int32, (1, LANES), 1)
    lo_half = lane < HALF
    q_rows = qt_ref[0].astype(F32).T
    kv_new = kv_ref[...]
    k_new = kv_new[:, 0:A_KV_W].reshape(n_seq, seq_len, A_KV_W)
    v_new = kv_new[:, A_KV_W:].reshape(n_seq, seq_len, A_KV_W)

    def to_kv_half(x, head):
        kvh = head // A_GROUPS
        if head % 2 != kvh:
            x = pltpu.roll(x, HALF, axis=1)
        return jnp.where(lo_half if kvh == 0 else ~lo_half, x, 0.0)

    lhs = jnp.concatenate(
        [to_kv_half(q_rows[:, (h // 2) * LANES:(h // 2 + 1) * LANES], h).reshape(n_seq, seq_len, LANES)
         for h in range(A_HEADS)], axis=1).astype(BF16)
    zpad = jnp.zeros((n_seq, BF16_ROWS - seq_len, A_KV_W), F32)
    k_nb = jnp.concatenate([k_new, zpad], axis=1).astype(BF16)
    v_nb = jnp.concatenate([v_new, zpad], axis=1).astype(BF16)
    s_c = jnp.einsum('sqf,sfk->sqk', lhs, ck_ref[...].astype(BF16), preferred_element_type=F32)
    s_n = jnp.einsum('sqf,skf->sqk', lhs, k_nb, preferred_element_type=F32)
    nrow = A_HEADS * seq_len
    ti = lax.broadcasted_iota(jnp.int32, (nrow, wb), 0) & (seq_len - 1)
    ki = lax.broadcasted_iota(jnp.int32, (nrow, wb), 1)
    mask_c = (ti + wb - ki) < WINDOW
    ti_n = lax.broadcasted_iota(jnp.int32, (nrow, BF16_ROWS), 0) & (seq_len - 1)
    ki_n = lax.broadcasted_iota(jnp.int32, (nrow, BF16_ROWS), 1)
    mask_n = ki_n <= ti_n
    row_head = lax.broadcasted_iota(jnp.int32, (nrow, 1), 0) >> (seq_len.bit_length() - 1)
    sk = jnp.zeros((nrow, 1), F32)
    for h in range(A_HEADS):
        sk = jnp.where(row_head == h, sink_ref[h], sk)
    s_c = jnp.where(mask_c, s_c, -jnp.inf)
    s_n = jnp.where(mask_n, s_n, -jnp.inf)
    mx = jnp.maximum(jnp.maximum(jnp.max(s_c, axis=-1, keepdims=True), jnp.max(s_n, axis=-1, keepdims=True)), sk)
    p_c = jnp.exp(s_c - mx)
    p_n = jnp.exp(s_n - mx)
    den = jnp.sum(p_c, axis=-1, keepdims=True) + jnp.sum(p_n, axis=-1, keepdims=True) + jnp.exp(sk - mx)
    o = (jnp.einsum('sqk,sfk->sqf', p_c.astype(BF16), cv_ref[...].astype(BF16), preferred_element_type=F32)
         + jnp.einsum('sqk,skf->sqf', p_n.astype(BF16), v_nb, preferred_element_type=F32)) / den

    def from_kv_half(head):
        x = o[:, head * seq_len:(head + 1) * seq_len, :].reshape(GROUP, LANES)
        return pltpu.roll(x, HALF, axis=1) if head % 2 != head // A_GROUPS else x

    for c in range(A_HEADS // 2):
        h_ref[:, c * LANES:(c + 1) * LANES] = jnp.where(lo_half, from_kv_half(2 * c),
                                                        from_kv_half(2 * c + 1)).astype(h_ref.dtype)

    kt_new, vt_new = kv_new[:, 0:A_KV_W].T, kv_new[:, A_KV_W:].T
    for s in range(n_seq):
        put = (wb - seq_len - s * seq_len) % LANES
        for new_t, c_ref, win_ref in ((kt_new, ck_ref, kwin_ref), (vt_new, cv_ref, vwin_ref)):
            win_ref[s] = jnp.where(lane >= wb - seq_len, pltpu.roll(new_t, put, axis=1) if put else new_t,
                                   pltpu.roll(c_ref[s], wb - seq_len, axis=1))


def _swa_sample(qat, kv_a, cache_k, cache_v, sinks, seq_len):
    ngrp = qat.shape[0]
    n_seq = GROUP // seq_len
    wb = cache_k.shape[2]
    assert wb == LANES, "window positions fill one lane-width"
    row = lambda w: pl.BlockSpec((GROUP, w), lambda i: (i, 0))
    cache = pl.BlockSpec((n_seq, A_KV_W, wb), lambda i: (i, 0, 0))
    return pl.pallas_call(
        functools.partial(_swa_sample_kernel, seq_len),
        grid=(ngrp,),
        in_specs=[pl.BlockSpec(memory_space=pltpu.SMEM), pl.BlockSpec((1, A_Q_W, LANES), lambda i: (i, 0, 0)),
                  row(2 * A_KV_W), cache, cache],
        out_specs=[row(A_Q_W), cache, cache],
        out_shape=[jax.ShapeDtypeStruct((ngrp * GROUP, A_Q_W), BF16),
                   jax.ShapeDtypeStruct(cache_k.shape, F32),
                   jax.ShapeDtypeStruct(cache_v.shape, F32)],
        compiler_params=_params(("arbitrary",)),
        name="swa_sample",
    )(sinks, qat, kv_a, cache_k, cache_v)


def _merge_ffn_kernel(x_ref, hm_ref, ha_ref, gab_ref, wa_ref, wb_ref, wo_ref, nw_ref, wg_ref, wu_ref, wd_ref,
                      y_ref):
    tm = x_ref.shape[0]
    halves = [slice(0, tm // 2), slice(tm // 2, tm)]
    mix = []
    for rs in halves:
        ga = jax.nn.sigmoid(gab_ref[rs, 0:D_MODEL].astype(F32))
        gb = jax.nn.sigmoid(gab_ref[rs, D_MODEL:].astype(F32))
        mix.append(ga * _dot(hm_ref[rs, :], wa_ref[...]) + gb * _dot(ha_ref[rs, :], wb_ref[...]))
    x1 = [x_ref[rs, :] + _dot(m.astype(BF16), wo_ref[...]) for rs, m in zip(halves, mix)]
    hf = [_rms_rows(v, nw_ref[...]).astype(BF16) for v in x1]
    act = [(jax.nn.silu(_dot(h, wg_ref[...])) * _dot(h, wu_ref[...])).astype(BF16) for h in hf]
    for rs, v, a in zip(halves, x1, act):
        y_ref[rs, :] = v + _dot(a, wd_ref[...])


def _merge_ffn(x2d, h_m, h_a, g_ab, wa, wb, wo, nw, wg, wu, wd, tm):
    n = x2d.shape[0]
    row = lambda w: pl.BlockSpec((tm, w), lambda i: (i, 0))
    return pl.pallas_call(
        _merge_ffn_kernel,
        grid=(n // tm,),
        in_specs=[row(D_MODEL), row(M_V_W), row(A_Q_W), row(2 * D_MODEL)]
                 + [_const_spec(w.shape) for w in (wa, wb, wo, nw, wg, wu, wd)],
        out_specs=row(D_MODEL),
        out_shape=jax.ShapeDtypeStruct((n, D_MODEL), F32),
        compiler_params=_params(("arbitrary",)),
        name="merge_ffn",
    )(x2d, h_m, h_a, g_ab, wa, wb, wo, nw, wg, wu, wd)


def kernel(x_prompt, x_sample, state_mlstm_C, state_mlstm_n, state_mlstm_m, cache_swa_k, cache_swa_v,
           norm_mix_w, w_in, mlstm_i_bias, mlstm_f_bias, mlstm_norm_w, q_norm_w, k_norm_w, attn_sinks,
           w_branch_a, w_branch_b, w_out, norm_ffn_w, w_gate, w_up, w_down):
    depth = w_in.shape[0]
    assert depth == 1, "single trunk layer"
    l = 0
    bp, tp = x_prompt.shape[0], x_prompt.shape[1]
    bs, ts = x_sample.shape[0], x_sample.shape[1]
    assert tp % TOKEN_TILE == 0 and (bs * ts) % GROUP == 0 and GROUP % ts == 0 and ts & (ts - 1) == 0
    assert ts <= SUBLANES, "sample chunk must fit one sublane tile"

    wt = jnp.transpose(w_in[l])
    c_km, c_vm = M_QK_W, 2 * M_QK_W
    c_g = 2 * M_QK_W + 2 * M_V_W
    c_qa = c_g + 2 * M_HEADS
    c_ka = c_qa + A_Q_W
    gate_pad = jnp.zeros((BF16_ROWS - 2 * M_HEADS, D_MODEL), F32)
    wmt = jnp.concatenate([wt[0:c_km], wt[c_vm:c_g]], axis=0).astype(BF16)
    wqgt = jnp.concatenate([wt[c_qa:c_ka], wt[c_g:c_qa], gate_pad], axis=0).astype(BF16)
    wn = jnp.concatenate([wt[c_km:c_vm], wt[c_ka:]], axis=0).astype(BF16)
    head_of = jnp.arange(A_KV_W) // A_HEAD_DIM
    bd = (head_of[:, None] == head_of[None, :]).astype(BF16)
    qcol = (jnp.tile(q_norm_w[l], A_HEADS) * (A_HEAD_DIM ** -0.5)).reshape(A_Q_W, 1)
    krow = jnp.tile(k_norm_w[l], A_KV_HEADS).reshape(1, A_KV_W)
    nw_mix = norm_mix_w[l].reshape(1, D_MODEL)
    nw_ffn = norm_ffn_w[l].reshape(1, D_MODEL)
    bias_col = jnp.concatenate([mlstm_i_bias[l], mlstm_f_bias[l]]).reshape(2 * M_HEADS, 1)
    nw_col = mlstm_norm_w[l].reshape(M_V_W, 1)
    sinks = attn_sinks[l]
    proj = lambda x2d, tm: _proj(x2d, nw_mix, wmt, wqgt, wn, bd, qcol, krow, tm)

    xp = x_prompt.reshape(bp * tp, D_MODEL)
    later_ws = (w_branch_a[l], w_branch_b[l], w_out[l], w_gate[l], w_up[l], w_down[l])
    g_ab, h_m, h_a, kwin_p, vwin_p, st_p, m_p, wa, wb, wo, wg, wu, wd = _front(
        xp, bp, sinks, nw_mix, wmt, wqgt, wn, bd, qcol, krow, bias_col, nw_col, later_ws, TOKEN_TILE)
    merge = lambda x2d, h_m, h_a, g_ab, tm: _merge_ffn(x2d, h_m, h_a, g_ab, wa, wb, wo, nw_ffn, wg, wu, wd, tm)
    yp = merge(xp, h_m, h_a, g_ab, TOKEN_TILE).reshape(bp, tp, D_MODEL)
    c_p = jnp.swapaxes(st_p[:, :M_DV, :], 1, 2).reshape(bp, M_HEADS, M_DK, M_DV)
    n_p = st_p[:, M_DV, :].reshape(bp, M_HEADS, M_DK)
    m_pr = m_p[:, :M_HEADS, 0]

    ns = bs * ts
    xs = x_sample.reshape(ns, D_MODEL)
    tms = TOKEN_TILE if ns % TOKEN_TILE == 0 else GROUP
    qvot, gt, qat, k_m, kv_a, g_ab = proj(xs, tms)
    ngrp = ns // GROUP
    m_lanes = jnp.repeat(state_mlstm_m[l], ts, axis=0).reshape(ngrp, GROUP, M_HEADS)
    mrow = jnp.pad(jnp.swapaxes(m_lanes, 1, 2), ((0, 0), (0, SUBLANES - M_HEADS), (0, 0)))
    h_m, c_s, n_s, mt_s = _mlstm_sample(qvot, gt, k_m, mrow, bias_col, nw_col,
                                        state_mlstm_C[l].reshape(bs, M_QK_W, M_DV),
                                        state_mlstm_n[l].reshape(bs, M_QK_W), ts)
    wbuf = cache_swa_k.shape[2]
    to_fm = lambda a: jnp.transpose(a, (0, 2, 3, 1)).reshape(bs, A_KV_W, wbuf)
    from_fm = lambda a: jnp.transpose(a.reshape(bs, A_KV_HEADS, A_HEAD_DIM, wbuf), (0, 3, 1, 2))[None]
    h_a, kwin_s, vwin_s = _swa_sample(qat, kv_a, to_fm(cache_swa_k[l]), to_fm(cache_swa_v[l]), sinks, ts)
    ys = merge(xs, h_m, h_a, g_ab, tms).reshape(bs, ts, D_MODEL)
    m_s = jnp.swapaxes(mt_s[:, :M_HEADS, :], 1, 2).reshape(bs, ts, M_HEADS)[:, ts - 1, :]

    kv5 = lambda a: a.reshape(a.shape[0], a.shape[1], A_KV_HEADS, A_HEAD_DIM)[None]
    return (yp, ys,
            c_p[None], n_p[None], m_pr[None], kv5(kwin_p), kv5(vwin_p),
            c_s.reshape(bs, M_HEADS, M_DK, M_DV)[None], n_s.reshape(bs, M_HEADS, M_DK)[None], m_s[None],
            from_fm(kwin_s), from_fm(vwin_s))
```

```python
import functools

import jax
import jax.numpy as jnp
from jax import lax
from jax.experimental import pallas as pl
from jax.experimental.pallas import tpu as pltpu

F32 = jnp.float32
BF16 = jnp.bfloat16

D_MODEL = 1024
M_HEADS = 4
M_DK = 64
M_DV = 128
M_QK_W = M_HEADS * M_DK
M_V_W = M_HEADS * M_DV
A_HEADS = 8
A_KV_HEADS = 2
A_HEAD_DIM = 64
A_GROUPS = A_HEADS // A_KV_HEADS
A_Q_W = A_HEADS * A_HEAD_DIM
A_KV_W = A_KV_HEADS * A_HEAD_DIM
WINDOW = 128
EPS = 1e-6

LANES = 128
SUBLANES = 8
BF16_ROWS = 16
GROUP = 128
S_ROWS = M_DV + BF16_ROWS
TOKEN_TILE = 512
VMEM_LIMIT = 56 * 1024 * 1024
DK_SHIFT = M_DK.bit_length() - 1
WINDOW_SHIFT = WINDOW.bit_length() - 1
HALF = LANES // 2
assert A_HEAD_DIM == HALF and A_KV_W == LANES, "attention head pairs share one lane-width"

NT_DIMS = (((1,), (1,)), ((), ()))


def _dot(a, b):
    return jnp.dot(a, b, preferred_element_type=F32)


def _dot_nt(a, b):
    return lax.dot_general(a, b, NT_DIMS, preferred_element_type=F32)


def _const_spec(shape):
    nd = len(shape)
    return pl.BlockSpec(shape, lambda *_: (0,) * nd, pipeline_mode=pl.Buffered(1))


def _params(sem):
    return pltpu.CompilerParams(dimension_semantics=sem, vmem_limit_bytes=VMEM_LIMIT)


def _rms_rows(x, nw):
    ms = jnp.mean(x * x, axis=-1, keepdims=True)
    return (x * lax.rsqrt(ms + EPS)) * nw


GAB_CHUNK = 512


def _proj_chunks(x_ref, nw_ref, wmt_ref, wqgt_ref, wn_ref, bd_ref, qcol_ref, krow_ref,
                 qvot_ref, gt_ref, qat_ref, km_ref, kva_ref, gab_ref):
    hn = _rms_rows(x_ref[...], nw_ref[...]).astype(BF16)
    n_blk = qat_ref.shape[0]

    def put(ref, rows, val):
        for c in range(n_blk):
            ref[c, rows, :] = val[:, c * LANES:(c + 1) * LANES].astype(ref.dtype)

    def mlstm_qvo():
        put(qvot_ref, slice(None), _dot_nt(wmt_ref[...], hn))

    def attn_q_and_gates():
        qt = _dot_nt(wqgt_ref[...], hn)
        put(gt_ref, slice(None), qt[A_Q_W:A_Q_W + 2 * M_HEADS])
        for h in range(A_HEADS):
            hs = slice(h * A_HEAD_DIM, (h + 1) * A_HEAD_DIM)
            blk = qt[hs]
            ssq_q = jnp.sum(blk * blk, axis=0, keepdims=True)
            put(qat_ref, hs, (blk * lax.rsqrt(ssq_q * (1.0 / A_HEAD_DIM) + EPS)) * qcol_ref[hs])

    def mlstm_k():
        km_ref[...] = _dot_nt(hn, wn_ref[0:M_QK_W, :]).astype(km_ref.dtype)

    def branch_gates(c0):
        def run():
            w0 = M_QK_W + 2 * A_KV_W + c0
            gab_ref[:, c0:c0 + GAB_CHUNK] = _dot_nt(hn, wn_ref[w0:w0 + GAB_CHUNK, :]).astype(gab_ref.dtype)
        return run

    def attn_kv():
        kv = _dot_nt(hn, wn_ref[M_QK_W:M_QK_W + 2 * A_KV_W, :])
        k = kv[:, 0:A_KV_W]
        ksq = k * k
        hi = ksq.astype(BF16)
        lo = (ksq - hi.astype(F32)).astype(BF16)
        ssq = _dot(hi, bd_ref[...]) + _dot(lo, bd_ref[...])
        kva_ref[:, 0:A_KV_W] = (k * lax.rsqrt(ssq * (1.0 / A_HEAD_DIM) + EPS)) * krow_ref[...]
        kva_ref[:, A_KV_W:] = kv[:, A_KV_W:]

    return ([mlstm_qvo, attn_q_and_gates, mlstm_k]
            + [branch_gates(c0) for c0 in range(0, gab_ref.shape[1], GAB_CHUNK)] + [attn_kv])


W_ROWS_T = M_QK_W + 2 * M_V_W
W_ROWS_Q = A_Q_W + BF16_ROWS


def _weight_views(w_ref):
    return w_ref.at[0:W_ROWS_T], w_ref.at[W_ROWS_T:W_ROWS_T + W_ROWS_Q], w_ref.at[W_ROWS_T + W_ROWS_Q:]


def _proj_kernel(x_ref, nw_ref, w_ref, *rest):
    for piece in _proj_chunks(x_ref, nw_ref, *_weight_views(w_ref), *rest):
        piece()


def _proj(x2d, nw, w_all, bd, qcol, krow, tm):
    n = x2d.shape[0]
    row = lambda w: pl.BlockSpec((tm, w), lambda i: (i, 0))
    n_blk = tm // LANES
    slab = lambda r: pl.BlockSpec((n_blk, r, LANES), lambda i: (i, 0, 0))
    w_qvo = M_QK_W + 2 * M_V_W
    return pl.pallas_call(
        _proj_kernel,
        grid=(n // tm,),
        in_specs=[row(D_MODEL)] + [_const_spec(a.shape) for a in (nw, w_all, bd, qcol, krow)],
        out_specs=[slab(w_qvo), slab(2 * M_HEADS), slab(A_Q_W), row(M_QK_W), row(2 * A_KV_W), row(2 * D_MODEL)],
        out_shape=[jax.ShapeDtypeStruct((n // LANES, w_qvo, LANES), BF16),
                   jax.ShapeDtypeStruct((n // LANES, 2 * M_HEADS, LANES), F32),
                   jax.ShapeDtypeStruct((n // LANES, A_Q_W, LANES), BF16),
                   jax.ShapeDtypeStruct((n, M_QK_W), BF16),
                   jax.ShapeDtypeStruct((n, 2 * A_KV_W), F32),
                   jax.ShapeDtypeStruct((n, 2 * D_MODEL), BF16)],
        compiler_params=_params(("arbitrary",)),
        name="proj",
    )(x2d, nw, w_all, bd, qcol, krow)


def _split3_rows(x):
    hi = x.astype(BF16).astype(F32)
    r1 = x - hi
    mid = r1.astype(BF16).astype(F32)
    lo = r1 - mid
    return jnp.concatenate([hi, mid, lo], axis=0).astype(BF16)


def _log_sigmoid(x):
    return jnp.minimum(x, 0.0) - jnp.log1p(jnp.exp(-jnp.abs(x)))


def _chunk_masks(chunk_shift):
    s = lax.broadcasted_iota(jnp.int32, (GROUP, GROUP), 0)
    t = lax.broadcasted_iota(jnp.int32, (GROUP, GROUP), 1)
    same = (s >> chunk_shift) == (t >> chunk_shift)
    return same, same & (s <= t)


def _groups_gates(gts, same, causal):
    n = len(gts)
    assert 2 * M_HEADS == SUBLANES and n * SUBLANES <= GROUP
    cm_bf = jnp.where(causal, 1.0, 0.0).astype(BF16)
    lf = _log_sigmoid(jnp.concatenate(gts, axis=0))
    nr = n * SUBLANES
    bt3 = _dot(_split3_rows(lf), cm_bf)
    bt = (bt3[0:nr] + bt3[nr:2 * nr]) + bt3[2 * nr:3 * nr]
    b4 = [bt[g * SUBLANES + M_HEADS:(g + 1) * SUBLANES] for g in range(n)]
    a4 = [gts[g][0:M_HEADS] - b4[g] for g in range(n)]
    a_rows = [a for g in range(n) for a in (a4[g], a4[g])]
    if nr < GROUP:
        a_rows.append(jnp.zeros((GROUP - nr, GROUP), F32))
    a_cols = jnp.concatenate(a_rows, axis=0).T
    out = []
    for g in range(n):
        at_mats, run_rows, chunk_rows = [], [], []
        for h in range(M_HEADS):
            c = g * SUBLANES + h
            at = jnp.broadcast_to(a_cols[:, c:c + 1], (GROUP, GROUP))
            at_mats.append(at)
            run_rows.append(jnp.max(jnp.where(causal, at, -jnp.inf), axis=0, keepdims=True))
            chunk_rows.append(jnp.max(jnp.where(same, at, -jnp.inf), axis=0, keepdims=True))
        out.append((b4[g], a4[g], at_mats, jnp.concatenate(run_rows, axis=0), jnp.concatenate(chunk_rows, axis=0)))
    return out


def _group_weights(m_prev, b4, a4, run4, chunk4):
    big_m = jnp.maximum(m_prev, run4)
    m_last = jnp.maximum(m_prev, chunk4)
    w_inter = jnp.exp(m_prev - big_m)
    g_vec = jnp.exp(m_prev - m_last)
    w_last = jnp.exp(a4 - m_last)
    m_t = b4 + big_m
    return big_m, w_inter, g_vec, w_last, m_t, jnp.exp(-m_t)


def _group_scores(qvot, ks):
    lane_head = lax.broadcasted_iota(jnp.int32, (1, M_QK_W), 1) >> DK_SHIFT
    row_head = lax.broadcasted_iota(jnp.int32, (M_QK_W, 1), 0) >> DK_SHIFT
    qt = qvot(0, M_QK_W)
    k_stack = jnp.concatenate([jnp.where(lane_head == h, ks, jnp.zeros_like(ks)) for h in range(M_HEADS)], axis=0)
    qw = jnp.concatenate([jnp.where(row_head == h, qt, jnp.zeros_like(qt)) for h in range(M_HEADS)], axis=1)
    zero_blk = jnp.zeros((M_DK, GROUP), BF16)
    sc_t = []
    for h in range(0, M_HEADS, 2):
        q_pair = jnp.concatenate(
            [jnp.concatenate([qvot(h * M_DK, (h + 1) * M_DK), zero_blk], axis=1),
             jnp.concatenate([zero_blk, qvot((h + 1) * M_DK, (h + 2) * M_DK)], axis=1)], axis=0)
        sc = _dot(ks[:, h * M_DK:(h + 2) * M_DK], q_pair)
        sc_t += [sc[:, 0:GROUP], sc[:, GROUP:]]
    return k_stack, qw, sc_t


def _group_values(qvot, sc_t, at_mats, big_m, causal):
    ones_rows = jnp.where(lax.broadcasted_iota(jnp.int32, (BF16_ROWS, GROUP), 0) == 0, 1.0, 0.0).astype(BF16)
    zero_blk = jnp.zeros((GROUP, GROUP), BF16)
    vta, s_t = [], []
    for h in range(M_HEADS):
        w_t = jnp.where(causal, jnp.exp(at_mats[h] - big_m[h:h + 1]), 0.0)
        s_t.append((sc_t[h] * w_t).astype(BF16))
        vta.append(jnp.concatenate([qvot(M_QK_W + h * M_DV, M_QK_W + (h + 1) * M_DV), ones_rows], axis=0))
    intra = []
    for h in range(0, M_HEADS, 2):
        pair = jnp.concatenate([jnp.concatenate([s_t[h], zero_blk], axis=1),
                                jnp.concatenate([zero_blk, s_t[h + 1]], axis=1)], axis=0)
        intra.append(_dot(jnp.concatenate(vta[h:h + 2], axis=1), pair))
    return vta, jnp.concatenate(intra, axis=1)


def _weighted_values(vta, w_rows):
    return jnp.concatenate([(vta[h].astype(F32) * w_rows[h:h + 1]).astype(BF16) for h in range(M_HEADS)], axis=1)


def _lanes_x(rows):
    return jnp.concatenate([rows[h:h + 1] for h in range(M_HEADS)], axis=1)


def _group_out(qvot, inter, intra, w_inter, e_neg_m, nw_ref):
    outs = []
    out_all = inter * _lanes_x(w_inter) + intra
    for h in range(M_HEADS):
        out_t = out_all[:, h * GROUP:(h + 1) * GROUP]
        hh = out_t[0:M_DV] / jnp.maximum(jnp.abs(out_t[M_DV:M_DV + 1]), e_neg_m[h:h + 1])
        ms = jnp.mean(hh * hh, axis=0, keepdims=True)
        hn = (hh * lax.rsqrt(ms + EPS)) * nw_ref[h * M_DV:(h + 1) * M_DV]
        o_t = qvot(M_QK_W + M_V_W + h * M_DV, M_QK_W + M_V_W + (h + 1) * M_DV)
        outs.append(hn * jax.nn.sigmoid(o_t.astype(F32)))
    return jnp.concatenate(outs, axis=0).T


def _decay_row(g_vec, lane0):
    lane_head = lax.broadcasted_iota(jnp.int32, (1, M_QK_W), 1) >> DK_SHIFT
    g_row = jnp.zeros((1, M_QK_W), F32)
    for h in range(M_HEADS):
        g_row = jnp.where(lane_head == h, g_vec[h:h + 1, lane0:lane0 + 1], g_row)
    return g_row


def _mlstm_sample_kernel(seq_len, qvot_ref, gt_ref, k_ref, mrow_ref, bias_ref, nw_ref, c_ref, n_ref,
                         h_ref, c_out_ref, n_out_ref, mt_ref):
    n_seq = GROUP // seq_len
    shift = seq_len.bit_length() - 1
    same, causal = _chunk_masks(shift)
    lane_seq = lax.broadcasted_iota(jnp.int32, (1, GROUP), 1) >> shift
    lane_seq_x = jnp.concatenate([lane_seq] * M_HEADS, axis=1)
    row0 = lax.broadcasted_iota(jnp.int32, (BF16_ROWS, M_QK_W), 0) == 0
    qvot = lambda r0, r1: qvot_ref[0, r0:r1, :]
    ks = k_ref[...] * (M_DK ** -0.5)
    (b4, a4, at_mats, run4, chunk4), = _groups_gates([gt_ref[0] + bias_ref[...]], same, causal)
    big_m, w_inter, g_vec, w_last, m_t, e_neg_m = _group_weights(mrow_ref[0, 0:M_HEADS, :], b4, a4, run4, chunk4)
    mt_ref[0] = jnp.concatenate([m_t, m_t], axis=0)
    k_stack, qw, sc_t = _group_scores(qvot, ks)

    st_old = []
    for s in range(n_seq):
        n_rows = jnp.where(row0, jnp.broadcast_to(n_ref[s:s + 1, :], (BF16_ROWS, M_QK_W)), 0.0)
        st_old.append(jnp.concatenate([c_ref[s].T, n_rows], axis=0))
    inter_all = _dot(jnp.concatenate(st_old, axis=0).astype(BF16), qw)
    vta, intra = _group_values(qvot, sc_t, at_mats, big_m, causal)
    inter = inter_all[0:S_ROWS]
    for s in range(1, n_seq):
        inter = jnp.where(lane_seq_x == s, inter_all[s * S_ROWS:(s + 1) * S_ROWS], inter)
    tall = jnp.concatenate([_weighted_values(vta, jnp.where(lane_seq == s, w_last, 0.0)) for s in range(n_seq)],
                           axis=0)
    d_st = _dot(tall, k_stack)
    for s in range(n_seq):
        st_new = _decay_row(g_vec, s * seq_len) * st_old[s] + d_st[s * S_ROWS:(s + 1) * S_ROWS]
        c_out_ref[s] = st_new[0:M_DV].T
        n_out_ref[s:s + 1, :] = st_new[M_DV:M_DV + 1]
    h_ref[...] = _group_out(qvot, inter, intra, w_inter, e_neg_m, nw_ref).astype(h_ref.dtype)


def _mlstm_sample(qvot, gt, k_m, mrow, bias_col, nw_col, c, n, seq_len):
    ngrp = qvot.shape[0]
    n_seq = GROUP // seq_len
    full = lambda a: pl.BlockSpec(a.shape, lambda i: (0,) * a.ndim)
    slab = lambda a: pl.BlockSpec((1,) + a.shape[1:], lambda i: (i, 0, 0))
    row = lambda w: pl.BlockSpec((GROUP, w), lambda i: (i, 0))
    c_spec = pl.BlockSpec((n_seq,) + c.shape[1:], lambda i: (i, 0, 0))
    n_spec = pl.BlockSpec((n_seq, n.shape[1]), lambda i: (i, 0))
    return pl.pallas_call(
        functools.partial(_mlstm_sample_kernel, seq_len),
        grid=(ngrp,),
        in_specs=[slab(qvot), slab(gt), row(M_QK_W), slab(mrow), full(bias_col), full(nw_col), c_spec, n_spec],
        out_specs=[row(M_V_W), c_spec, n_spec, slab(mrow)],
        out_shape=[jax.ShapeDtypeStruct((ngrp * GROUP, M_V_W), BF16),
                   jax.ShapeDtypeStruct(c.shape, F32),
                   jax.ShapeDtypeStruct(n.shape, F32),
                   jax.ShapeDtypeStruct(mrow.shape, F32)],
        compiler_params=_params(("arbitrary",)),
        name="mlstm_sample",
    )(qvot, gt, k_m, mrow, bias_col, nw_col, c, n)


def _swa_scores(qbs, has_prev, sink_ref, qt_ref, kv_ref, kvp_ref):
    hd = A_HEAD_DIM
    nq = A_GROUPS * WINDOW
    si = lax.broadcasted_iota(jnp.int32, (2 * WINDOW, nq), 0)
    qi = lax.broadcasted_iota(jnp.int32, (2 * WINDOW, nq), 1) & (WINDOW - 1)
    local = ((si < WINDOW) & (si > qi)) | ((si >= WINDOW) & (si - WINDOW <= qi))
    first = local & (has_prev | (si >= WINDOW))
    lane_grp = lax.broadcasted_iota(jnp.int32, (1, nq), 1) >> WINDOW_SHIFT
    kv_block = lambda i: kvp_ref[...] if i == 0 else kv_ref[(i - 1) * WINDOW:i * WINDOW, :]
    k_bf = lambda i: kv_block(i)[:, 0:A_KV_W].astype(BF16)
    vt_bf = lambda i: kv_block(i)[:, A_KV_W:].T.astype(BF16)
    zeros = jnp.zeros((hd, nq), BF16)
    sinks = []
    for kvh in range(A_KV_HEADS):
        sk = jnp.zeros((1, nq), F32)
        for g in range(A_GROUPS):
            sk = jnp.where(lane_grp == g, sink_ref[kvh * A_GROUPS + g], sk)
        sinks.append(sk)
    units = []
    for qb in qbs:
        kk = jnp.concatenate([k_bf(qb), k_bf(qb + 1)], axis=0)
        vt = jnp.concatenate([vt_bf(qb), vt_bf(qb + 1)], axis=1)
        mask = local if qb > 0 else first
        for kvh in range(A_KV_HEADS):
            q4t = jnp.concatenate(
                [qt_ref[qb, (kvh * A_GROUPS + g) * hd:(kvh * A_GROUPS + g + 1) * hd, :] for g in range(A_GROUPS)],
                axis=1)
            wq = jnp.concatenate([q4t, zeros] if kvh == 0 else [zeros, q4t], axis=0)
            units.append((jnp.where(mask, _dot(kk, wq), -jnp.inf), sinks[kvh], vt))
    return units


def _swa_finish(qbs, units, h_ref):
    hd = A_HEAD_DIM
    for i, qb in enumerate(qbs):
        pieces = []
        for kvh in range(A_KV_HEADS):
            s, sk, vt = units[i * A_KV_HEADS + kvh]
            mx = jnp.maximum(jnp.max(s, axis=0, keepdims=True), sk)
            p = jnp.exp(s - mx)
            den = jnp.sum(p, axis=0, keepdims=True) + jnp.exp(sk - mx)
            ot = _dot(vt[kvh * hd:(kvh + 1) * hd], p.astype(BF16)) / den
            pieces += [ot[:, g * WINDOW:(g + 1) * WINDOW] for g in range(A_GROUPS)]
        h_t = jnp.concatenate(pieces, axis=0)
        h_ref[qb * WINDOW:(qb + 1) * WINDOW, :] = h_t.T.astype(h_ref.dtype)


def _front_kernel(tiles_per_seq, n_cast, *refs):
    (sink_ref, x_ref, nw_ref, w_ref, bd_ref, qcol_ref, krow_ref, bias_ref, nwm_ref) = refs[:9]
    wmt_ref, wqgt_ref, wn_ref = _weight_views(w_ref)
    cast_in = refs[9:9 + n_cast]
    (gab_ref, hm_ref, ha_ref, kwin_ref, vwin_ref, st_ref, m_ref) = refs[9 + n_cast:16 + n_cast]
    cast_out = refs[16 + n_cast:16 + 2 * n_cast]
    set_a, set_b = refs[16 + 2 * n_cast:21 + 2 * n_cast], refs[21 + 2 * n_cast:26 + 2 * n_cast]
    st_s, m_s, kvp_s = refs[26 + 2 * n_cast:]
    odd = (pl.program_id(0) & 1) == 1
    for parity, q_set, p_set in ((jnp.logical_not(odd), set_a, set_b), (odd, set_b, set_a)):
        pl.when(parity)(functools.partial(
            _front_body, tiles_per_seq, sink_ref, x_ref, nw_ref, wmt_ref, wqgt_ref, wn_ref, bd_ref, qcol_ref,
            krow_ref, bias_ref, nwm_ref, gab_ref, hm_ref, ha_ref, kwin_ref, vwin_ref, st_ref, m_ref,
            *q_set, *p_set, st_s, m_s, kvp_s))
    for src, dst in zip(cast_in, cast_out):
        dst[...] = src[...].astype(dst.dtype)


def _front_body(tiles_per_seq, sink_ref, x_ref, nw_ref, wmt_ref, wqgt_ref, wn_ref, bd_ref, qcol_ref, krow_ref,
                bias_ref, nwm_ref,
                gab_ref, hm_ref, ha_ref, kwin_ref, vwin_ref, st_ref, m_ref,
                q_qvot, q_gt, q_qat, q_km, q_kva, p_qvot, p_gt, p_qat, p_km, p_kva, st_s, m_s, kvp_s):
    k = pl.program_id(0)

    @pl.when(k == 0)
    def _():
        for r in (p_qvot, p_gt, p_qat, p_km, p_kva, st_s, m_s, kvp_s):
            r[...] = jnp.zeros(r.shape, r.dtype)

    pieces = _proj_chunks(x_ref, nw_ref, wmt_ref, wqgt_ref, wn_ref, bd_ref, qcol_ref, krow_ref,
                          q_qvot, q_gt, q_qat, q_km, q_kva, gab_ref)

    seq_start = lax.rem(k - 1 + tiles_per_seq, tiles_per_seq) == 0
    n_blk = p_qat.shape[0]
    same, causal = _chunk_masks(GROUP.bit_length() - 1)
    st = jnp.where(seq_start, 0.0, st_s[...])
    m_col = jnp.where(seq_start, 0.0, m_s[0:M_HEADS, 0:1])

    qvots = [lambda r0, r1, g=g: p_qvot[g, r0:r1, :] for g in range(n_blk)]
    scores = [_group_scores(qvots[g], p_km[g * GROUP:(g + 1) * GROUP, :] * (M_DK ** -0.5))
              for g in range(n_blk)]
    inter = [_dot(st.astype(BF16), scores[0][1])]
    gates = _groups_gates([p_gt[g] + bias_ref[...] for g in range(n_blk)], same, causal)
    for piece in pieces[0:1]:
        piece()
    grp = []
    for g in range(n_blk):
        b4, a4, at_mats, run4, chunk4 = gates[g]
        weights = _group_weights(jnp.broadcast_to(m_col, (M_HEADS, GROUP)), b4, a4, run4, chunk4)
        m_col = weights[4][:, GROUP - 1:GROUP]
        grp.append((qvots[g], at_mats, weights) + scores[g])
    has_prev = jnp.logical_not(seq_start)
    qb_lo, qb_hi = list(range(0, n_blk // 2)), list(range(n_blk // 2, n_blk))
    units_lo = _swa_scores(qb_lo, has_prev, sink_ref, p_qat, p_kva, kvp_s)
    for piece in pieces[1:3]:
        piece()

    vals = []

    def group_values(g):
        qvot, at_mats, (big_m, _, _, w_last, _, _), k_stack, _, sc_t = grp[g]
        vta, intra = _group_values(qvot, sc_t, at_mats, big_m, causal)
        vals.append((intra, _dot(_weighted_values(vta, w_last), k_stack)))

    for g in range(0, n_blk // 2):
        group_values(g)
    _swa_finish(qb_lo[:1], units_lo[:A_KV_HEADS], ha_ref)
    for piece in pieces[3:4]:
        piece()
    for g in range(n_blk // 2, n_blk):
        group_values(g)
    _swa_finish(qb_lo[1:], units_lo[A_KV_HEADS:], ha_ref)
    units_hi = _swa_scores(qb_hi, has_prev, sink_ref, p_qat, p_kva, kvp_s)
    for piece in pieces[4:6]:
        piece()

    for g in range(n_blk):
        st = _decay_row(grp[g][2][2], 0) * st + vals[g][1]
        if g + 1 < n_blk:
            inter.append(_dot(st.astype(BF16), grp[g + 1][4]))
        if g == n_blk // 2 - 1:
            _swa_finish(qb_hi[:1], units_hi[:A_KV_HEADS], ha_ref)
            for piece in pieces[6:7]:
                piece()
    _swa_finish(qb_hi[1:], units_hi[A_KV_HEADS:], ha_ref)
    for piece in pieces[7:]:
        piece()

    for g in range(n_blk):
        qvot, _, (_, w_inter, _, _, _, e_neg_m) = grp[g][0:3]
        hm_ref[g * GROUP:(g + 1) * GROUP, :] = _group_out(qvot, inter[g], vals[g][0], w_inter, e_neg_m,
                                                          nwm_ref).astype(hm_ref.dtype)
    st_s[...] = st
    st_ref[0] = st
    m_rows = jnp.broadcast_to(m_col, (M_HEADS, LANES))
    m_rows = jnp.concatenate([m_rows, m_rows], axis=0)
    m_s[...] = m_rows
    m_ref[0] = m_rows
    tm = p_kva.shape[0]
    kwin_ref[0] = p_kva[tm - WINDOW:, 0:A_KV_W]
    vwin_ref[0] = p_kva[tm - WINDOW:, A_KV_W:]
    kvp_s[...] = p_kva[tm - WINDOW:, :]


def _cast_rows(rows, n_steps):
    rb = BF16_ROWS
    while rows % rb or rows // rb > n_steps:
        rb += BF16_ROWS
    return rb


def _front(x2d, nb, sinks, nw, w_all, bd, qcol, krow, bias_col, nwm_col, cast_ws, tm):
    n = x2d.shape[0]
    n_tiles = n // tm
    tps = n_tiles // nb
    n_blk = tm // LANES
    w_qvo = M_QK_W + 2 * M_V_W
    cur = lambda w: pl.BlockSpec((tm, w), lambda k: (jnp.minimum(k, n_tiles - 1), 0))
    prev = lambda w: pl.BlockSpec((tm, w), lambda k: (jnp.maximum(k - 1, 0), 0))
    per_seq = lambda r, w: pl.BlockSpec((1, r, w), lambda k: (jnp.maximum(k - 1, 0) // tps, 0, 0))

    def cast_spec(a):
        rb = _cast_rows(a.shape[0], n_tiles)
        return pl.BlockSpec((rb, a.shape[1]), lambda k: (jnp.minimum(k, a.shape[0] // rb - 1), 0))

    cast_in_specs = [cast_spec(a) for a in cast_ws]
    cast_out_specs = [cast_spec(a) for a in cast_ws]
    proj_scratch = [pltpu.VMEM((n_blk, w_qvo, LANES), BF16), pltpu.VMEM((n_blk, 2 * M_HEADS, LANES), F32),
                    pltpu.VMEM((n_blk, A_Q_W, LANES), BF16), pltpu.VMEM((tm, M_QK_W), BF16),
                    pltpu.VMEM((tm, 2 * A_KV_W), F32)]
    return pl.pallas_call(
        functools.partial(_front_kernel, tps, len(cast_ws)),
        grid=(n_tiles + 1,),
        in_specs=[pl.BlockSpec(memory_space=pltpu.SMEM), cur(D_MODEL)]
                 + [_const_spec(a.shape) for a in (nw, w_all, bd, qcol, krow, bias_col, nwm_col)]
                 + cast_in_specs,
        out_specs=[cur(2 * D_MODEL), prev(M_V_W), prev(A_Q_W), per_seq(WINDOW, A_KV_W), per_seq(WINDOW, A_KV_W),
                   per_seq(S_ROWS, M_QK_W), per_seq(SUBLANES, LANES)] + cast_out_specs,
        out_shape=[jax.ShapeDtypeStruct((n, 2 * D_MODEL), BF16),
                   jax.ShapeDtypeStruct((n, M_V_W), BF16),
                   jax.ShapeDtypeStruct((n, A_Q_W), BF16),
                   jax.ShapeDtypeStruct((nb, WINDOW, A_KV_W), F32),
                   jax.ShapeDtypeStruct((nb, WINDOW, A_KV_W), F32),
                   jax.ShapeDtypeStruct((nb, S_ROWS, M_QK_W), F32),
                   jax.ShapeDtypeStruct((nb, SUBLANES, LANES), F32)]
                  + [jax.ShapeDtypeStruct(a.shape, BF16) for a in cast_ws],
        scratch_shapes=proj_scratch + proj_scratch + [pltpu.VMEM((S_ROWS, M_QK_W), F32),
                                                       pltpu.VMEM((SUBLANES, LANES), F32),
                                                       pltpu.VMEM((WINDOW, 2 * A_KV_W), F32)],
        compiler_params=_params(("arbitrary",)),
        name="front",
    )(sinks, x2d, nw, w_all, bd, qcol, krow, bias_col, nwm_col, *cast_ws)


def _swa_sample_kernel(seq_len, sink_ref, qt_ref, kv_ref, ck_ref, cv_ref, h_ref, kwin_ref, vwin_ref):
    n_seq = GROUP // seq_len
    wb = ck_ref.shape[2]
    lane = lax.broadcasted_iota(jnp.int32, (1, LANES), 1)
    lo_half = lane < HALF
    q_rows = qt_ref[0].astype(F32).T
    kv_new = kv_ref[...]
    k_new = kv_new[:, 0:A_KV_W].reshape(n_seq, seq_len, A_KV_W)
    v_new = kv_new[:, A_KV_W:].reshape(n_seq, seq_len, A_KV_W)

    def to_kv_half(x, head):
        kvh = head // A_GROUPS
        if head % 2 != kvh:
            x = pltpu.roll(x, HALF, axis=1)
        return jnp.where(lo_half if kvh == 0 else ~lo_half, x, 0.0)

    lhs = jnp.concatenate(
        [to_kv_half(q_rows[:, (h // 2) * LANES:(h // 2 + 1) * LANES], h).reshape(n_seq, seq_len, LANES)
         for h in range(A_HEADS)], axis=1).astype(BF16)
    zpad = jnp.zeros((n_seq, BF16_ROWS - seq_len, A_KV_W), F32)
    k_nb = jnp.concatenate([k_new, zpad], axis=1).astype(BF16)
    v_nb = jnp.concatenate([v_new, zpad], axis=1).astype(BF16)
    s_c = jnp.einsum('sqf,sfk->sqk', lhs, ck_ref[...].astype(BF16), preferred_element_type=F32)
    s_n = jnp.einsum('sqf,skf->sqk', lhs, k_nb, preferred_element_type=F32)
    nrow = A_HEADS * seq_len
    ti = lax.broadcasted_iota(jnp.int32, (nrow, wb), 0) & (seq_len - 1)
    ki = lax.broadcasted_iota(jnp.int32, (nrow, wb), 1)
    mask_c = (ti + wb - ki) < WINDOW
    ti_n = lax.broadcasted_iota(jnp.int32, (nrow, BF16_ROWS), 0) & (seq_len - 1)
    ki_n = lax.broadcasted_iota(jnp.int32, (nrow, BF16_ROWS), 1)
    mask_n = ki_n <= ti_n
    row_head = lax.broadcasted_iota(jnp.int32, (nrow, 1), 0) >> (seq_len.bit_length() - 1)
    sk = jnp.zeros((nrow, 1), F32)
    for h in range(A_HEADS):
        sk = jnp.where(row_head == h, sink_ref[h], sk)
    s_c = jnp.where(mask_c, s_c, -jnp.inf)
    s_n = jnp.where(mask_n, s_n, -jnp.inf)
    mx = jnp.maximum(jnp.maximum(jnp.max(s_c, axis=-1, keepdims=True), jnp.max(s_n, axis=-1, keepdims=True)), sk)
    p_c = jnp.exp(s_c - mx)
    p_n = jnp.exp(s_n - mx)
    den = jnp.sum(p_c, axis=-1, keepdims=True) + jnp.sum(p_n, axis=-1, keepdims=True) + jnp.exp(sk - mx)
    o = (jnp.einsum('sqk,sfk->sqf', p_c.astype(BF16), cv_ref[...].astype(BF16), preferred_element_type=F32)
         + jnp.einsum('sqk,skf->sqf', p_n.astype(BF16), v_nb, preferred_element_type=F32)) / den

    def from_kv_half(head):
        x = o[:, head * seq_len:(head + 1) * seq_len, :].reshape(GROUP, LANES)
        return pltpu.roll(x, HALF, axis=1) if head % 2 != head // A_GROUPS else x

    for c in range(A_HEADS // 2):
        h_ref[:, c * LANES:(c + 1) * LANES] = jnp.where(lo_half, from_kv_half(2 * c),
                                                        from_kv_half(2 * c + 1)).astype(h_ref.dtype)

    kt_new, vt_new = kv_new[:, 0:A_KV_W].T, kv_new[:, A_KV_W:].T
    for s in range(n_seq):
        put = (wb - seq_len - s * seq_len) % LANES
        for new_t, c_ref, win_ref in ((kt_new, ck_ref, kwin_ref), (vt_new, cv_ref, vwin_ref)):
            win_ref[s] = jnp.where(lane >= wb - seq_len, pltpu.roll(new_t, put, axis=1) if put else new_t,
                                   pltpu.roll(c_ref[s], wb - seq_len, axis=1))


def _swa_sample(qat, kv_a, cache_k, cache_v, sinks, seq_len):
    ngrp = qat.shape[0]
    n_seq = GROUP // seq_len
    wb = cache_k.shape[2]
    assert wb == LANES, "window positions fill one lane-width"
    row = lambda w: pl.BlockSpec((GROUP, w), lambda i: (i, 0))
    cache = pl.BlockSpec((n_seq, A_KV_W, wb), lambda i: (i, 0, 0))
    return pl.pallas_call(
        functools.partial(_swa_sample_kernel, seq_len),
        grid=(ngrp,),
        in_specs=[pl.BlockSpec(memory_space=pltpu.SMEM), pl.BlockSpec((1, A_Q_W, LANES), lambda i: (i, 0, 0)),
                  row(2 * A_KV_W), cache, cache],
        out_specs=[row(A_Q_W), cache, cache],
        out_shape=[jax.ShapeDtypeStruct((ngrp * GROUP, A_Q_W), BF16),
                   jax.ShapeDtypeStruct(cache_k.shape, F32),
                   jax.ShapeDtypeStruct(cache_v.shape, F32)],
        compiler_params=_params(("arbitrary",)),
        name="swa_sample",
    )(sinks, qat, kv_a, cache_k, cache_v)


def _merge_ffn_kernel(x_ref, hm_ref, ha_ref, gab_ref, wa_ref, wb_ref, wo_ref, nw_ref, wg_ref, wu_ref, wd_ref,
                      y_ref):
    tm = x_ref.shape[0]
    halves = [slice(0, tm // 2), slice(tm // 2, tm)]
    mix = []
    for rs in halves:
        ga = jax.nn.sigmoid(gab_ref[rs, 0:D_MODEL].astype(F32))
        gb = jax.nn.sigmoid(gab_ref[rs, D_MODEL:].astype(F32))
        mix.append(ga * _dot(hm_ref[rs, :], wa_ref[...]) + gb * _dot(ha_ref[rs, :], wb_ref[...]))
    x1 = [x_ref[rs, :] + _dot(m.astype(BF16), wo_ref[...]) for rs, m in zip(halves, mix)]
    hf = [_rms_rows(v, nw_ref[...]).astype(BF16) for v in x1]
    act = [(jax.nn.silu(_dot(h, wg_ref[...])) * _dot(h, wu_ref[...])).astype(BF16) for h in hf]
    for rs, v, a in zip(halves, x1, act):
        y_ref[rs, :] = v + _dot(a, wd_ref[...])


def _merge_ffn(x2d, h_m, h_a, g_ab, wa, wb, wo, nw, wg, wu, wd, tm):
    n = x2d.shape[0]
    row = lambda w: pl.BlockSpec((tm, w), lambda i: (i, 0))
    return pl.pallas_call(
        _merge_ffn_kernel,
        grid=(n // tm,),
        in_specs=[row(D_MODEL), row(M_V_W), row(A_Q_W), row(2 * D_MODEL)]
                 + [_const_spec(w.shape) for w in (wa, wb, wo, nw, wg, wu, wd)],
        out_specs=row(D_MODEL),
        out_shape=jax.ShapeDtypeStruct((n, D_MODEL), F32),
        compiler_params=_params(("arbitrary",)),
        name="merge_ffn",
    )(x2d, h_m, h_a, g_ab, wa, wb, wo, nw, wg, wu, wd)


def kernel(x_prompt, x_sample, state_mlstm_C, state_mlstm_n, state_mlstm_m, cache_swa_k, cache_swa_v,
           norm_mix_w, w_in, mlstm_i_bias, mlstm_f_bias, mlstm_norm_w, q_norm_w, k_norm_w, attn_sinks,
           w_branch_a, w_branch_b, w_out, norm_ffn_w, w_gate, w_up, w_down):
    depth = w_in.shape[0]
    assert depth == 1, "single trunk layer"
    l = 0
    bp, tp = x_prompt.shape[0], x_prompt.shape[1]
    bs, ts = x_sample.shape[0], x_sample.shape[1]
    assert tp % TOKEN_TILE == 0 and (bs * ts) % GROUP == 0 and GROUP % ts == 0 and ts & (ts - 1) == 0
    assert ts <= SUBLANES, "sample chunk must fit one sublane tile"

    wt = jnp.transpose(w_in[l])
    c_km, c_vm = M_QK_W, 2 * M_QK_W
    c_g = 2 * M_QK_W + 2 * M_V_W
    c_qa = c_g + 2 * M_HEADS
    c_ka = c_qa + A_Q_W
    gate_pad = jnp.zeros((BF16_ROWS - 2 * M_HEADS, D_MODEL), F32)
    w_all = jnp.concatenate([wt[0:c_km], wt[c_vm:c_g],
                             wt[c_qa:c_ka], wt[c_g:c_qa], gate_pad,
                             wt[c_km:c_vm], wt[c_ka:]],
                            axis=0).astype(BF16)
    assert w_all.shape[0] == W_ROWS_T + W_ROWS_Q + M_QK_W + 2 * A_KV_W + 2 * D_MODEL
    head_of = jnp.arange(A_KV_W) // A_HEAD_DIM
    bd = (head_of[:, None] == head_of[None, :]).astype(BF16)
    qcol = (jnp.tile(q_norm_w[l], A_HEADS) * (A_HEAD_DIM ** -0.5)).reshape(A_Q_W, 1)
    krow = jnp.tile(k_norm_w[l], A_KV_HEADS).reshape(1, A_KV_W)
    nw_mix = norm_mix_w[l].reshape(1, D_MODEL)
    nw_ffn = norm_ffn_w[l].reshape(1, D_MODEL)
    bias_col = jnp.concatenate([mlstm_i_bias[l], mlstm_f_bias[l]]).reshape(2 * M_HEADS, 1)
    nw_col = mlstm_norm_w[l].reshape(M_V_W, 1)
    sinks = attn_sinks[l]
    proj = lambda x2d, tm: _proj(x2d, nw_mix, w_all, bd, qcol, krow, tm)

    xp = x_prompt.reshape(bp * tp, D_MODEL)
    later_ws = (w_branch_a[l], w_branch_b[l], w_out[l], w_gate[l], w_up[l], w_down[l])
    g_ab, h_m, h_a, kwin_p, vwin_p, st_p, m_p, wa, wb, wo, wg, wu, wd = _front(
        xp, bp, sinks, nw_mix, w_all, bd, qcol, krow, bias_col, nw_col, later_ws, TOKEN_TILE)
    merge = lambda x2d, h_m, h_a, g_ab, tm: _merge_ffn(x2d, h_m, h_a, g_ab, wa, wb, wo, nw_ffn, wg, wu, wd, tm)
    yp = merge(xp, h_m, h_a, g_ab, TOKEN_TILE).reshape(bp, tp, D_MODEL)
    c_p = jnp.swapaxes(st_p[:, :M_DV, :], 1, 2).reshape(bp, M_HEADS, M_DK, M_DV)
    n_p = st_p[:, M_DV, :].reshape(bp, M_HEADS, M_DK)
    m_pr = m_p[:, :M_HEADS, 0]

    ns = bs * ts
    xs = x_sample.reshape(ns, D_MODEL)
    tms = TOKEN_TILE if ns % TOKEN_TILE == 0 else GROUP
    qvot, gt, qat, k_m, kv_a, g_ab = proj(xs, tms)
    ngrp = ns // GROUP
    m_lanes = jnp.repeat(state_mlstm_m[l], ts, axis=0).reshape(ngrp, GROUP, M_HEADS)
    mrow = jnp.pad(jnp.swapaxes(m_lanes, 1, 2), ((0, 0), (0, SUBLANES - M_HEADS), (0, 0)))
    h_m, c_s, n_s, mt_s = _mlstm_sample(qvot, gt, k_m, mrow, bias_col, nw_col,
                                        state_mlstm_C[l].reshape(bs, M_QK_W, M_DV),
                                        state_mlstm_n[l].reshape(bs, M_QK_W), ts)
    wbuf = cache_swa_k.shape[2]
    to_fm = lambda a: jnp.transpose(a, (0, 2, 3, 1)).reshape(bs, A_KV_W, wbuf)
    from_fm = lambda a: jnp.transpose(a.reshape(bs, A_KV_HEADS, A_HEAD_DIM, wbuf), (0, 3, 1, 2))[None]
    h_a, kwin_s, vwin_s = _swa_sample(qat, kv_a, to_fm(cache_swa_k[l]), to_fm(cache_swa_v[l]), sinks, ts)
    ys = merge(xs, h_m, h_a, g_ab, tms).reshape(bs, ts, D_MODEL)
    m_s = jnp.swapaxes(mt_s[:, :M_HEADS, :], 1, 2).reshape(bs, ts, M_HEADS)[:, ts - 1, :]

    kv5 = lambda a: a.reshape(a.shape[0], a.shape[1], A_KV_HEADS, A_HEAD_DIM)[None]
    return (yp, ys,
            c_p[None], n_p[None], m_pr[None], kv5(kwin_p), kv5(vwin_p),
            c_s.reshape(bs, M_HEADS, M_DK, M_DV)[None], n_s.reshape(bs, M_HEADS, M_DK)[None], m_s[None],
            from_fm(kwin_s), from_fm(vwin_s))
```

```python
import functools

import jax
import jax.numpy as jnp
from jax import lax
from jax.experimental import pallas as pl
from jax.experimental.pallas import tpu as pltpu

F32 = jnp.float32
BF16 = jnp.bfloat16

D_MODEL = 1024
M_HEADS = 4
M_DK = 64
M_DV = 128
M_QK_W = M_HEADS * M_DK
M_V_W = M_HEADS * M_DV
A_HEADS = 8
A_KV_HEADS = 2
A_HEAD_DIM = 64
A_GROUPS = A_HEADS // A_KV_HEADS
A_Q_W = A_HEADS * A_HEAD_DIM
A_KV_W = A_KV_HEADS * A_HEAD_DIM
WINDOW = 128
EPS = 1e-6

LANES = 128
SUBLANES = 8
BF16_ROWS = 16
GROUP = 128
S_ROWS = M_DV + BF16_ROWS
TOKEN_TILE = 512
VMEM_LIMIT = 56 * 1024 * 1024
DK_SHIFT = M_DK.bit_length() - 1
WINDOW_SHIFT = WINDOW.bit_length() - 1
HALF = LANES // 2
assert A_HEAD_DIM == HALF and A_KV_W == LANES, "attention head pairs share one lane-width"

NT_DIMS = (((1,), (1,)), ((), ()))


def _dot(a, b):
    return jnp.dot(a, b, preferred_element_type=F32)


def _dot_nt(a, b):
    return lax.dot_general(a, b, NT_DIMS, preferred_element_type=F32)


def _const_spec(shape):
    nd = len(shape)
    return pl.BlockSpec(shape, lambda *_: (0,) * nd, pipeline_mode=pl.Buffered(1))


def _params(sem):
    return pltpu.CompilerParams(dimension_semantics=sem, vmem_limit_bytes=VMEM_LIMIT)


def _rms_rows(x, nw):
    ms = jnp.mean(x * x, axis=-1, keepdims=True)
    return (x * lax.rsqrt(ms + EPS)) * nw


GAB_CHUNK = 512


def _proj_chunks(x_ref, nw_ref, wmt_ref, wqgt_ref, wn_ref, bd_ref, qcol_ref, krow_ref,
                 qvot_ref, gt_ref, qat_ref, km_ref, kva_ref, gab_ref):
    hn = _rms_rows(x_ref[...], nw_ref[...]).astype(BF16)
    n_blk = qat_ref.shape[0]

    def put(ref, rows, val):
        for c in range(n_blk):
            ref[c, rows, :] = val[:, c * LANES:(c + 1) * LANES].astype(ref.dtype)

    def mlstm_qvo():
        put(qvot_ref, slice(None), _dot_nt(wmt_ref[...], hn))

    def attn_q_and_gates():
        qt = _dot_nt(wqgt_ref[...], hn)
        put(gt_ref, slice(None), qt[A_Q_W:A_Q_W + 2 * M_HEADS])
        for h in range(A_HEADS):
            hs = slice(h * A_HEAD_DIM, (h + 1) * A_HEAD_DIM)
            blk = qt[hs]
            ssq_q = jnp.sum(blk * blk, axis=0, keepdims=True)
            put(qat_ref, hs, (blk * lax.rsqrt(ssq_q * (1.0 / A_HEAD_DIM) + EPS)) * qcol_ref[hs])

    def mlstm_k():
        km_ref[...] = _dot_nt(hn, wn_ref[0:M_QK_W, :]).astype(km_ref.dtype)

    def branch_gates(c0):
        def run():
            w0 = M_QK_W + 2 * A_KV_W + c0
            gab_ref[:, c0:c0 + GAB_CHUNK] = _dot_nt(hn, wn_ref[w0:w0 + GAB_CHUNK, :]).astype(gab_ref.dtype)
        return run

    def attn_kv():
        kv = _dot_nt(hn, wn_ref[M_QK_W:M_QK_W + 2 * A_KV_W, :])
        k = kv[:, 0:A_KV_W]
        ksq = k * k
        hi = ksq.astype(BF16)
        lo = (ksq - hi.astype(F32)).astype(BF16)
        ssq = _dot(hi, bd_ref[...]) + _dot(lo, bd_ref[...])
        kva_ref[:, 0:A_KV_W] = (k * lax.rsqrt(ssq * (1.0 / A_HEAD_DIM) + EPS)) * krow_ref[...]
        kva_ref[:, A_KV_W:] = kv[:, A_KV_W:]

    return ([mlstm_qvo, attn_q_and_gates, mlstm_k]
            + [branch_gates(c0) for c0 in range(0, gab_ref.shape[1], GAB_CHUNK)] + [attn_kv])


W_ROWS_T = M_QK_W + 2 * M_V_W
W_ROWS_Q = A_Q_W + BF16_ROWS


def _weight_views(w_ref):
    return w_ref.at[0:W_ROWS_T], w_ref.at[W_ROWS_T:W_ROWS_T + W_ROWS_Q], w_ref.at[W_ROWS_T + W_ROWS_Q:]


def _proj_kernel(x_ref, nw_ref, w_ref, *rest):
    for piece in _proj_chunks(x_ref, nw_ref, *_weight_views(w_ref), *rest):
        piece()


def _proj(x2d, nw, w_all, bd, qcol, krow, tm):
    n = x2d.shape[0]
    row = lambda w: pl.BlockSpec((tm, w), lambda i: (i, 0))
    n_blk = tm // LANES
    slab = lambda r: pl.BlockSpec((n_blk, r, LANES), lambda i: (i, 0, 0))
    w_qvo = M_QK_W + 2 * M_V_W
    return pl.pallas_call(
        _proj_kernel,
        grid=(n // tm,),
        in_specs=[row(D_MODEL)] + [_const_spec(a.shape) for a in (nw, w_all, bd, qcol, krow)],
        out_specs=[slab(w_qvo), slab(2 * M_HEADS), slab(A_Q_W), row(M_QK_W), row(2 * A_KV_W), row(2 * D_MODEL)],
        out_shape=[jax.ShapeDtypeStruct((n // LANES, w_qvo, LANES), BF16),
                   jax.ShapeDtypeStruct((n // LANES, 2 * M_HEADS, LANES), F32),
                   jax.ShapeDtypeStruct((n // LANES, A_Q_W, LANES), BF16),
                   jax.ShapeDtypeStruct((n, M_QK_W), BF16),
                   jax.ShapeDtypeStruct((n, 2 * A_KV_W), F32),
                   jax.ShapeDtypeStruct((n, 2 * D_MODEL), BF16)],
        compiler_params=_params(("arbitrary",)),
        name="proj",
    )(x2d, nw, w_all, bd, qcol, krow)


def _split3_rows(x):
    hi = x.astype(BF16).astype(F32)
    r1 = x - hi
    mid = r1.astype(BF16).astype(F32)
    lo = r1 - mid
    return jnp.concatenate([hi, mid, lo], axis=0).astype(BF16)


def _log_sigmoid(x):
    return jnp.minimum(x, 0.0) - jnp.log1p(jnp.exp(-jnp.abs(x)))


def _chunk_masks(chunk_shift):
    s = lax.broadcasted_iota(jnp.int32, (GROUP, GROUP), 0)
    t = lax.broadcasted_iota(jnp.int32, (GROUP, GROUP), 1)
    same = (s >> chunk_shift) == (t >> chunk_shift)
    return same, same & (s <= t)


def _groups_gates(gts, same, causal):
    n = len(gts)
    assert 2 * M_HEADS == SUBLANES and n * SUBLANES <= GROUP
    cm_bf = jnp.where(causal, 1.0, 0.0).astype(BF16)
    lf = _log_sigmoid(jnp.concatenate(gts, axis=0))
    nr = n * SUBLANES
    bt3 = _dot(_split3_rows(lf), cm_bf)
    bt = (bt3[0:nr] + bt3[nr:2 * nr]) + bt3[2 * nr:3 * nr]
    b4 = [bt[g * SUBLANES + M_HEADS:(g + 1) * SUBLANES] for g in range(n)]
    a4 = [gts[g][0:M_HEADS] - b4[g] for g in range(n)]
    a_rows = [a for g in range(n) for a in (a4[g], a4[g])]
    if nr < GROUP:
        a_rows.append(jnp.zeros((GROUP - nr, GROUP), F32))
    a_cols = jnp.concatenate(a_rows, axis=0).T
    out = []
    for g in range(n):
        at_mats, run_rows, chunk_rows = [], [], []
        for h in range(M_HEADS):
            c = g * SUBLANES + h
            at = jnp.broadcast_to(a_cols[:, c:c + 1], (GROUP, GROUP))
            at_mats.append(at)
            run_rows.append(jnp.max(jnp.where(causal, at, -jnp.inf), axis=0, keepdims=True))
            chunk_rows.append(jnp.max(jnp.where(same, at, -jnp.inf), axis=0, keepdims=True))
        out.append((b4[g], a4[g], at_mats, jnp.concatenate(run_rows, axis=0), jnp.concatenate(chunk_rows, axis=0)))
    return out


def _group_weights(m_prev, b4, a4, run4, chunk4):
    big_m = jnp.maximum(m_prev, run4)
    m_last = jnp.maximum(m_prev, chunk4)
    w_inter = jnp.exp(m_prev - big_m)
    g_vec = jnp.exp(m_prev - m_last)
    w_last = jnp.exp(a4 - m_last)
    m_t = b4 + big_m
    return big_m, w_inter, g_vec, w_last, m_t, jnp.exp(-m_t)


def _group_scores(qvot, ks):
    lane_head = lax.broadcasted_iota(jnp.int32, (1, M_QK_W), 1) >> DK_SHIFT
    row_head = lax.broadcasted_iota(jnp.int32, (M_QK_W, 1), 0) >> DK_SHIFT
    qt = qvot(0, M_QK_W)
    k_stack = jnp.concatenate([jnp.where(lane_head == h, ks, jnp.zeros_like(ks)) for h in range(M_HEADS)], axis=0)
    qw = jnp.concatenate([jnp.where(row_head == h, qt, jnp.zeros_like(qt)) for h in range(M_HEADS)], axis=1)
    zero_blk = jnp.zeros((M_DK, GROUP), BF16)
    sc_t = []
    for h in range(0, M_HEADS, 2):
        q_pair = jnp.concatenate(
            [jnp.concatenate([qvot(h * M_DK, (h + 1) * M_DK), zero_blk], axis=1),
             jnp.concatenate([zero_blk, qvot((h + 1) * M_DK, (h + 2) * M_DK)], axis=1)], axis=0)
        sc = _dot(ks[:, h * M_DK:(h + 2) * M_DK], q_pair)
        sc_t += [sc[:, 0:GROUP], sc[:, GROUP:]]
    return k_stack, qw, sc_t


def _group_values(qvot, sc_t, at_mats, big_m, causal):
    ones_rows = jnp.where(lax.broadcasted_iota(jnp.int32, (BF16_ROWS, GROUP), 0) == 0, 1.0, 0.0).astype(BF16)
    zero_blk = jnp.zeros((GROUP, GROUP), BF16)
    vta, s_t = [], []
    for h in range(M_HEADS):
        w_t = jnp.where(causal, jnp.exp(at_mats[h] - big_m[h:h + 1]), 0.0)
        s_t.append((sc_t[h] * w_t).astype(BF16))
        vta.append(jnp.concatenate([qvot(M_QK_W + h * M_DV, M_QK_W + (h + 1) * M_DV), ones_rows], axis=0))
    intra = []
    for h in range(0, M_HEADS, 2):
        pair = jnp.concatenate([jnp.concatenate([s_t[h], zero_blk], axis=1),
                                jnp.concatenate([zero_blk, s_t[h + 1]], axis=1)], axis=0)
        intra.append(_dot(jnp.concatenate(vta[h:h + 2], axis=1), pair))
    return vta, jnp.concatenate(intra, axis=1)


def _weighted_values(vta, w_rows):
    return jnp.concatenate([(vta[h].astype(F32) * w_rows[h:h + 1]).astype(BF16) for h in range(M_HEADS)], axis=1)


def _lanes_x(rows):
    return jnp.concatenate([rows[h:h + 1] for h in range(M_HEADS)], axis=1)


def _group_out(qvot, inter, intra, w_inter, e_neg_m, nw_ref):
    outs = []
    out_all = inter * _lanes_x(w_inter) + intra
    for h in range(M_HEADS):
        out_t = out_all[:, h * GROUP:(h + 1) * GROUP]
        hh = out_t[0:M_DV] / jnp.maximum(jnp.abs(out_t[M_DV:M_DV + 1]), e_neg_m[h:h + 1])
        ms = jnp.mean(hh * hh, axis=0, keepdims=True)
        hn = (hh * lax.rsqrt(ms + EPS)) * nw_ref[h * M_DV:(h + 1) * M_DV]
        o_t = qvot(M_QK_W + M_V_W + h * M_DV, M_QK_W + M_V_W + (h + 1) * M_DV)
        outs.append(hn * jax.nn.sigmoid(o_t.astype(F32)))
    return jnp.concatenate(outs, axis=0).T


def _decay_row(g_vec, lane0):
    lane_head = lax.broadcasted_iota(jnp.int32, (1, M_QK_W), 1) >> DK_SHIFT
    g_row = jnp.zeros((1, M_QK_W), F32)
    for h in range(M_HEADS):
        g_row = jnp.where(lane_head == h, g_vec[h:h + 1, lane0:lane0 + 1], g_row)
    return g_row


def _mlstm_sample_kernel(seq_len, qvot_ref, gt_ref, k_ref, mrow_ref, bias_ref, nw_ref, c_ref, n_ref,
                         h_ref, c_out_ref, n_out_ref, mt_ref):
    n_seq = GROUP // seq_len
    shift = seq_len.bit_length() - 1
    same, causal = _chunk_masks(shift)
    lane_seq = lax.broadcasted_iota(jnp.int32, (1, GROUP), 1) >> shift
    lane_seq_x = jnp.concatenate([lane_seq] * M_HEADS, axis=1)
    row0 = lax.broadcasted_iota(jnp.int32, (BF16_ROWS, M_QK_W), 0) == 0
    qvot = lambda r0, r1: qvot_ref[0, r0:r1, :]
    ks = k_ref[...] * (M_DK ** -0.5)
    (b4, a4, at_mats, run4, chunk4), = _groups_gates([gt_ref[0] + bias_ref[...]], same, causal)
    big_m, w_inter, g_vec, w_last, m_t, e_neg_m = _group_weights(mrow_ref[0, 0:M_HEADS, :], b4, a4, run4, chunk4)
    mt_ref[0] = jnp.concatenate([m_t, m_t], axis=0)
    k_stack, qw, sc_t = _group_scores(qvot, ks)

    st_old = []
    for s in range(n_seq):
        n_rows = jnp.where(row0, jnp.broadcast_to(n_ref[s:s + 1, :], (BF16_ROWS, M_QK_W)), 0.0)
        st_old.append(jnp.concatenate([c_ref[s].T, n_rows], axis=0))
    inter_all = _dot(jnp.concatenate(st_old, axis=0).astype(BF16), qw)
    vta, intra = _group_values(qvot, sc_t, at_mats, big_m, causal)
    inter = inter_all[0:S_ROWS]
    for s in range(1, n_seq):
        inter = jnp.where(lane_seq_x == s, inter_all[s * S_ROWS:(s + 1) * S_ROWS], inter)
    tall = jnp.concatenate([_weighted_values(vta, jnp.where(lane_seq == s, w_last, 0.0)) for s in range(n_seq)],
                           axis=0)
    d_st = _dot(tall, k_stack)
    for s in range(n_seq):
        st_new = _decay_row(g_vec, s * seq_len) * st_old[s] + d_st[s * S_ROWS:(s + 1) * S_ROWS]
        c_out_ref[s] = st_new[0:M_DV].T
        n_out_ref[s:s + 1, :] = st_new[M_DV:M_DV + 1]
    h_ref[...] = _group_out(qvot, inter, intra, w_inter, e_neg_m, nw_ref).astype(h_ref.dtype)


def _mlstm_sample(qvot, gt, k_m, mrow, bias_col, nw_col, c, n, seq_len):
    ngrp = qvot.shape[0]
    n_seq = GROUP // seq_len
    full = lambda a: pl.BlockSpec(a.shape, lambda i: (0,) * a.ndim)
    slab = lambda a: pl.BlockSpec((1,) + a.shape[1:], lambda i: (i, 0, 0))
    row = lambda w: pl.BlockSpec((GROUP, w), lambda i: (i, 0))
    c_spec = pl.BlockSpec((n_seq,) + c.shape[1:], lambda i: (i, 0, 0))
    n_spec = pl.BlockSpec((n_seq, n.shape[1]), lambda i: (i, 0))
    return pl.pallas_call(
        functools.partial(_mlstm_sample_kernel, seq_len),
        grid=(ngrp,),
        in_specs=[slab(qvot), slab(gt), row(M_QK_W), slab(mrow), full(bias_col), full(nw_col), c_spec, n_spec],
        out_specs=[row(M_V_W), c_spec, n_spec, slab(mrow)],
        out_shape=[jax.ShapeDtypeStruct((ngrp * GROUP, M_V_W), BF16),
                   jax.ShapeDtypeStruct(c.shape, F32),
                   jax.ShapeDtypeStruct(n.shape, F32),
                   jax.ShapeDtypeStruct(mrow.shape, F32)],
        compiler_params=_params(("arbitrary",)),
        name="mlstm_sample",
    )(qvot, gt, k_m, mrow, bias_col, nw_col, c, n)


def _swa_scores(qbs, has_prev, sink_ref, qt_ref, kv_ref, kvp_ref):
    hd = A_HEAD_DIM
    nq = A_GROUPS * WINDOW
    si = lax.broadcasted_iota(jnp.int32, (2 * WINDOW, nq), 0)
    qi = lax.broadcasted_iota(jnp.int32, (2 * WINDOW, nq), 1) & (WINDOW - 1)
    local = ((si < WINDOW) & (si > qi)) | ((si >= WINDOW) & (si - WINDOW <= qi))
    first = local & (has_prev | (si >= WINDOW))
    lane_grp = lax.broadcasted_iota(jnp.int32, (1, nq), 1) >> WINDOW_SHIFT
    kv_block = lambda i: kvp_ref[...] if i == 0 else kv_ref[(i - 1) * WINDOW:i * WINDOW, :]
    k_bf = lambda i: kv_block(i)[:, 0:A_KV_W].astype(BF16)
    vt_bf = lambda i: kv_block(i)[:, A_KV_W:].T.astype(BF16)
    zeros = jnp.zeros((hd, nq), BF16)
    sinks = []
    for kvh in range(A_KV_HEADS):
        sk = jnp.zeros((1, nq), F32)
        for g in range(A_GROUPS):
            sk = jnp.where(lane_grp == g, sink_ref[kvh * A_GROUPS + g], sk)
        sinks.append(sk)
    units = []
    for qb in qbs:
        kk = jnp.concatenate([k_bf(qb), k_bf(qb + 1)], axis=0)
        vt = jnp.concatenate([vt_bf(qb), vt_bf(qb + 1)], axis=1)
        mask = local if qb > 0 else first
        for kvh in range(A_KV_HEADS):
            q4t = jnp.concatenate(
                [qt_ref[qb, (kvh * A_GROUPS + g) * hd:(kvh * A_GROUPS + g + 1) * hd, :] for g in range(A_GROUPS)],
                axis=1)
            wq = jnp.concatenate([q4t, zeros] if kvh == 0 else [zeros, q4t], axis=0)
            units.append((jnp.where(mask, _dot(kk, wq), -jnp.inf), sinks[kvh], vt))
    return units


def _swa_finish(qbs, units, h_ref):
    hd = A_HEAD_DIM
    for i, qb in enumerate(qbs):
        pieces = []
        for kvh in range(A_KV_HEADS):
            s, sk, vt = units[i * A_KV_HEADS + kvh]
            mx = jnp.maximum(jnp.max(s, axis=0, keepdims=True), sk)
            p = jnp.exp(s - mx)
            den = jnp.sum(p, axis=0, keepdims=True) + jnp.exp(sk - mx)
            ot = _dot(vt[kvh * hd:(kvh + 1) * hd], p.astype(BF16)) / den
            pieces += [ot[:, g * WINDOW:(g + 1) * WINDOW] for g in range(A_GROUPS)]
        h_t = jnp.concatenate(pieces, axis=0)
        h_ref[qb * WINDOW:(qb + 1) * WINDOW, :] = h_t.T.astype(h_ref.dtype)


def _front_kernel(tiles_per_seq, n_cast, *refs):
    (sink_ref, x_ref, nw_ref, w_ref, bd_ref, qcol_ref, krow_ref, bias_ref, nwm_ref) = refs[:9]
    wmt_ref, wqgt_ref, wn_ref = _weight_views(w_ref)
    cast_in = refs[9:9 + n_cast]
    (gab_ref, hm_ref, ha_ref, kwin_ref, vwin_ref, st_ref, m_ref) = refs[9 + n_cast:16 + n_cast]
    cast_out = refs[16 + n_cast:16 + 2 * n_cast]
    set_a, set_b = refs[16 + 2 * n_cast:21 + 2 * n_cast], refs[21 + 2 * n_cast:26 + 2 * n_cast]
    st_s, m_s, kvp_s = refs[26 + 2 * n_cast:]
    odd = (pl.program_id(0) & 1) == 1
    for parity, q_set, p_set in ((jnp.logical_not(odd), set_a, set_b), (odd, set_b, set_a)):
        pl.when(parity)(functools.partial(
            _front_body, tiles_per_seq, sink_ref, x_ref, nw_ref, wmt_ref, wqgt_ref, wn_ref, bd_ref, qcol_ref,
            krow_ref, bias_ref, nwm_ref, gab_ref, hm_ref, ha_ref, kwin_ref, vwin_ref, st_ref, m_ref,
            *q_set, *p_set, st_s, m_s, kvp_s))
    for src, dst in zip(cast_in, cast_out):
        dst[...] = src[...].astype(dst.dtype)


def _front_body(tiles_per_seq, sink_ref, x_ref, nw_ref, wmt_ref, wqgt_ref, wn_ref, bd_ref, qcol_ref, krow_ref,
                bias_ref, nwm_ref,
                gab_ref, hm_ref, ha_ref, kwin_ref, vwin_ref, st_ref, m_ref,
                q_qvot, q_gt, q_qat, q_km, q_kva, p_qvot, p_gt, p_qat, p_km, p_kva, st_s, m_s, kvp_s):
    k = pl.program_id(0)

    @pl.when(k == 0)
    def _():
        for r in (p_qvot, p_gt, p_qat, p_km, p_kva, st_s, m_s, kvp_s):
            r[...] = jnp.zeros(r.shape, r.dtype)

    pieces = _proj_chunks(x_ref, nw_ref, wmt_ref, wqgt_ref, wn_ref, bd_ref, qcol_ref, krow_ref,
                          q_qvot, q_gt, q_qat, q_km, q_kva, gab_ref)

    seq_start = lax.rem(k - 1 + tiles_per_seq, tiles_per_seq) == 0
    n_blk = p_qat.shape[0]
    same, causal = _chunk_masks(GROUP.bit_length() - 1)
    st = jnp.where(seq_start, 0.0, st_s[...])
    m_col = jnp.where(seq_start, 0.0, m_s[0:M_HEADS, 0:1])

    qvots = [lambda r0, r1, g=g: p_qvot[g, r0:r1, :] for g in range(n_blk)]
    scores = [_group_scores(qvots[g], p_km[g * GROUP:(g + 1) * GROUP, :] * (M_DK ** -0.5))
              for g in range(n_blk)]
    inter = [_dot(st.astype(BF16), scores[0][1])]
    gates = _groups_gates([p_gt[g] + bias_ref[...] for g in range(n_blk)], same, causal)
    for piece in pieces[0:1]:
        piece()
    grp = []
    for g in range(n_blk):
        b4, a4, at_mats, run4, chunk4 = gates[g]
        weights = _group_weights(jnp.broadcast_to(m_col, (M_HEADS, GROUP)), b4, a4, run4, chunk4)
        m_col = weights[4][:, GROUP - 1:GROUP]
        grp.append((qvots[g], at_mats, weights) + scores[g])
    has_prev = jnp.logical_not(seq_start)
    qb_lo, qb_hi = list(range(0, n_blk // 2)), list(range(n_blk // 2, n_blk))
    units_lo = _swa_scores(qb_lo, has_prev, sink_ref, p_qat, p_kva, kvp_s)
    for piece in pieces[1:3]:
        piece()

    vals = []

    def group_values(g):
        qvot, at_mats, (big_m, _, _, w_last, _, _), k_stack, _, sc_t = grp[g]
        vta, intra = _group_values(qvot, sc_t, at_mats, big_m, causal)
        vals.append((intra, _dot(_weighted_values(vta, w_last), k_stack)))

    for g in range(0, n_blk // 2):
        group_values(g)
    _swa_finish(qb_lo[:1], units_lo[:A_KV_HEADS], ha_ref)
    for piece in pieces[3:4]:
        piece()
    for g in range(n_blk // 2, n_blk):
        group_values(g)
    _swa_finish(qb_lo[1:], units_lo[A_KV_HEADS:], ha_ref)
    units_hi = _swa_scores(qb_hi, has_prev, sink_ref, p_qat, p_kva, kvp_s)
    for piece in pieces[4:6]:
        piece()

    for g in range(n_blk):
        st = _decay_row(grp[g][2][2], 0) * st + vals[g][1]
        if g + 1 < n_blk:
            inter.append(_dot(st.astype(BF16), grp[g + 1][4]))
        if g == n_blk // 2 - 1:
            _swa_finish(qb_hi[:1], units_hi[:A_KV_HEADS], ha_ref)
            for piece in pieces[6:7]:
                piece()
    _swa_finish(qb_hi[1:], units_hi[A_KV_HEADS:], ha_ref)
    for piece in pieces[7:]:
        piece()

    for g in range(n_blk):
        qvot, _, (_, w_inter, _, _, _, e_neg_m) = grp[g][0:3]
        hm_ref[g * GROUP:(g + 1) * GROUP, :] = _group_out(qvot, inter[g], vals[g][0], w_inter, e_neg_m,
                                                          nwm_ref).astype(hm_ref.dtype)
    st_s[...] = st
    st_ref[0] = st
    m_rows = jnp.broadcast_to(m_col, (M_HEADS, LANES))
    m_rows = jnp.concatenate([m_rows, m_rows], axis=0)
    m_s[...] = m_rows
    m_ref[0] = m_rows
    tm = p_kva.shape[0]
    kwin_ref[0] = p_kva[tm - WINDOW:, 0:A_KV_W]
    vwin_ref[0] = p_kva[tm - WINDOW:, A_KV_W:]
    kvp_s[...] = p_kva[tm - WINDOW:, :]


def _cast_rows(rows, n_steps):
    rb = BF16_ROWS
    while rows % rb or rows // rb > n_steps:
        rb += BF16_ROWS
    return rb


def _front(x2d, nb, sinks, nw, w_all, bd, qcol, krow, bias_col, nwm_col, cast_ws, tm):
    n = x2d.shape[0]
    n_tiles = n // tm
    tps = n_tiles // nb
    n_blk = tm // LANES
    w_qvo = M_QK_W + 2 * M_V_W
    cur = lambda w: pl.BlockSpec((tm, w), lambda k: (jnp.minimum(k, n_tiles - 1), 0))
    prev = lambda w: pl.BlockSpec((tm, w), lambda k: (jnp.maximum(k - 1, 0), 0))
    per_seq = lambda r, w: pl.BlockSpec((1, r, w), lambda k: (jnp.maximum(k - 1, 0) // tps, 0, 0))

    def cast_spec(a):
        rb = _cast_rows(a.shape[0], n_tiles)
        return pl.BlockSpec((rb, a.shape[1]), lambda k: (jnp.minimum(k, a.shape[0] // rb - 1), 0))

    cast_in_specs = [cast_spec(a) for a in cast_ws]
    cast_out_specs = [cast_spec(a) for a in cast_ws]
    proj_scratch = [pltpu.VMEM((n_blk, w_qvo, LANES), BF16), pltpu.VMEM((n_blk, 2 * M_HEADS, LANES), F32),
                    pltpu.VMEM((n_blk, A_Q_W, LANES), BF16), pltpu.VMEM((tm, M_QK_W), BF16),
                    pltpu.VMEM((tm, 2 * A_KV_W), F32)]
    return pl.pallas_call(
        functools.partial(_front_kernel, tps, len(cast_ws)),
        grid=(n_tiles + 1,),
        in_specs=[pl.BlockSpec(memory_space=pltpu.SMEM), cur(D_MODEL)]
                 + [_const_spec(a.shape) for a in (nw, w_all, bd, qcol, krow, bias_col, nwm_col)]
                 + cast_in_specs,
        out_specs=[cur(2 * D_MODEL), prev(M_V_W), prev(A_Q_W), per_seq(WINDOW, A_KV_W), per_seq(WINDOW, A_KV_W),
                   per_seq(S_ROWS, M_QK_W), per_seq(SUBLANES, LANES)] + cast_out_specs,
        out_shape=[jax.ShapeDtypeStruct((n, 2 * D_MODEL), BF16),
                   jax.ShapeDtypeStruct((n, M_V_W), BF16),
                   jax.ShapeDtypeStruct((n, A_Q_W), BF16),
                   jax.ShapeDtypeStruct((nb, WINDOW, A_KV_W), F32),
                   jax.ShapeDtypeStruct((nb, WINDOW, A_KV_W), F32),
                   jax.ShapeDtypeStruct((nb, S_ROWS, M_QK_W), F32),
                   jax.ShapeDtypeStruct((nb, SUBLANES, LANES), F32)]
                  + [jax.ShapeDtypeStruct(a.shape, BF16) for a in cast_ws],
        scratch_shapes=proj_scratch + proj_scratch + [pltpu.VMEM((S_ROWS, M_QK_W), F32),
                                                       pltpu.VMEM((SUBLANES, LANES), F32),
                                                       pltpu.VMEM((WINDOW, 2 * A_KV_W), F32)],
        compiler_params=_params(("arbitrary",)),
        name="front",
    )(sinks, x2d, nw, w_all, bd, qcol, krow, bias_col, nwm_col, *cast_ws)


def _swa_sample_kernel(seq_len, sink_ref, qt_ref, kv_ref, ck_ref, cv_ref, h_ref, kwin_ref, vwin_ref):
    n_seq = GROUP // seq_len
    wb = ck_ref.shape[2]
    lane = lax.broadcasted_iota(jnp.int32, (1, LANES), 1)
    lo_half = lane < HALF
    q_rows = qt_ref[0].astype(F32).T
    kv_new = kv_ref[...]
    k_new = kv_new[:, 0:A_KV_W].reshape(n_seq, seq_len, A_KV_W)
    v_new = kv_new[:, A_KV_W:].reshape(n_seq, seq_len, A_KV_W)

    def to_kv_half(x, head):
        kvh = head // A_GROUPS
        if head % 2 != kvh:
            x = pltpu.roll(x, HALF, axis=1)
        return jnp.where(lo_half if kvh == 0 else ~lo_half, x, 0.0)

    lhs = jnp.concatenate(
        [to_kv_half(q_rows[:, (h // 2) * LANES:(h // 2 + 1) * LANES], h).reshape(n_seq, seq_len, LANES)
         for h in range(A_HEADS)], axis=1).astype(BF16)
    zpad = jnp.zeros((n_seq, BF16_ROWS - seq_len, A_KV_W), F32)
    k_nb = jnp.concatenate([k_new, zpad], axis=1).astype(BF16)
    v_nb = jnp.concatenate([v_new, zpad], axis=1).astype(BF16)
    s_c = jnp.einsum('sqf,sfk->sqk', lhs, ck_ref[...].astype(BF16), preferred_element_type=F32)
    s_n = jnp.einsum('sqf,skf->sqk', lhs, k_nb, preferred_element_type=F32)
    nrow = A_HEADS * seq_len
    ti = lax.broadcasted_iota(jnp.int32, (nrow, wb), 0) & (seq_len - 1)
    ki = lax.broadcasted_iota(jnp.int32, (nrow, wb), 1)
    mask_c = (ti + wb - ki) < WINDOW
    ti_n = lax.broadcasted_iota(jnp.int32, (nrow, BF16_ROWS), 0) & (seq_len - 1)
    ki_n = lax.broadcasted_iota(jnp.int32, (nrow, BF16_ROWS), 1)
    mask_n = ki_n <= ti_n
    row_head = lax.broadcasted_iota(jnp.int32, (nrow, 1), 0) >> (seq_len.bit_length() - 1)
    sk = jnp.zeros((nrow, 1), F32)
    for h in range(A_HEADS):
        sk = jnp.where(row_head == h, sink_ref[h], sk)
    s_c = jnp.where(mask_c, s_c, -jnp.inf)
    s_n = jnp.where(mask_n, s_n, -jnp.inf)
    mx = jnp.maximum(jnp.maximum(jnp.max(s_c, axis=-1, keepdims=True), jnp.max(s_n, axis=-1, keepdims=True)), sk)
    p_c = jnp.exp(s_c - mx)
    p_n = jnp.exp(s_n - mx)
    den = jnp.sum(p_c, axis=-1, keepdims=True) + jnp.sum(p_n, axis=-1, keepdims=True) + jnp.exp(sk - mx)
    o = (jnp.einsum('sqk,sfk->sqf', p_c.astype(BF16), cv_ref[...].astype(BF16), preferred_element_type=F32)
         + jnp.einsum('sqk,skf->sqf', p_n.astype(BF16), v_nb, preferred_element_type=F32)) / den

    def from_kv_half(head):
        x = o[:, head * seq_len:(head + 1) * seq_len, :].reshape(GROUP, LANES)
        return pltpu.roll(x, HALF, axis=1) if head % 2 != head // A_GROUPS else x

    for c in range(A_HEADS // 2):
        h_ref[:, c * LANES:(c + 1) * LANES] = jnp.where(lo_half, from_kv_half(2 * c),
                                                        from_kv_half(2 * c + 1)).astype(h_ref.dtype)

    kt_new, vt_new = kv_new[:, 0:A_KV_W].T, kv_new[:, A_KV_W:].T
    for s in range(n_seq):
        put = (wb - seq_len - s * seq_len) % LANES
        for new_t, c_ref, win_ref in ((kt_new, ck_ref, kwin_ref), (vt_new, cv_ref, vwin_ref)):
            win_ref[s] = jnp.where(lane >= wb - seq_len, pltpu.roll(new_t, put, axis=1) if put else new_t,
                                   pltpu.roll(c_ref[s], wb - seq_len, axis=1))


def _swa_sample(qat, kv_a, cache_k, cache_v, sinks, seq_len):
    ngrp = qat.shape[0]
    n_seq = GROUP // seq_len
    wb = cache_k.shape[2]
    assert wb == LANES, "window positions fill one lane-width"
    row = lambda w: pl.BlockSpec((GROUP, w), lambda i: (i, 0))
    cache = pl.BlockSpec((n_seq, A_KV_W, wb), lambda i: (i, 0, 0))
    return pl.pallas_call(
        functools.partial(_swa_sample_kernel, seq_len),
        grid=(ngrp,),
        in_specs=[pl.BlockSpec(memory_space=pltpu.SMEM), pl.BlockSpec((1, A_Q_W, LANES), lambda i: (i, 0, 0)),
                  row(2 * A_KV_W), cache, cache],
        out_specs=[row(A_Q_W), cache, cache],
        out_shape=[jax.ShapeDtypeStruct((ngrp * GROUP, A_Q_W), BF16),
                   jax.ShapeDtypeStruct(cache_k.shape, F32),
                   jax.ShapeDtypeStruct(cache_v.shape, F32)],
        compiler_params=_params(("arbitrary",)),
        name="swa_sample",
    )(sinks, qat, kv_a, cache_k, cache_v)


def _merge_ffn_kernel(x_ref, hm_ref, ha_ref, gab_ref, wa_ref, wb_ref, wo_ref, nw_ref, wg_hbm, wu_hbm, wd_hbm,
                      y_ref, wg_ref, wu_ref, wd_ref, sems):
    copies = [pltpu.make_async_copy(src, dst, sems.at[j])
              for j, (src, dst) in enumerate(((wg_hbm, wg_ref), (wu_hbm, wu_ref), (wd_hbm, wd_ref)))]
    first = pl.program_id(0) == 0
    for cond, fetch in ((first, copies), (jnp.logical_not(first), None)):
        pl.when(cond)(functools.partial(_merge_ffn_body, x_ref, hm_ref, ha_ref, gab_ref, wa_ref, wb_ref, wo_ref,
                                        nw_ref, wg_ref, wu_ref, wd_ref, y_ref, fetch))


def _merge_ffn_body(x_ref, hm_ref, ha_ref, gab_ref, wa_ref, wb_ref, wo_ref, nw_ref, wg_ref, wu_ref, wd_ref, y_ref,
                    fetch):
    if fetch:
        for c in fetch:
            c.start()
    tm = x_ref.shape[0]
    halves = [slice(0, tm // 2), slice(tm // 2, tm)]
    mix = []
    for rs in halves:
        ga = jax.nn.sigmoid(gab_ref[rs, 0:D_MODEL].astype(F32))
        gb = jax.nn.sigmoid(gab_ref[rs, D_MODEL:].astype(F32))
        mix.append(ga * _dot(hm_ref[rs, :], wa_ref[...]) + gb * _dot(ha_ref[rs, :], wb_ref[...]))
    x1 = [x_ref[rs, :] + _dot(m.astype(BF16), wo_ref[...]) for rs, m in zip(halves, mix)]
    hf = [_rms_rows(v, nw_ref[...]).astype(BF16) for v in x1]
    if fetch:
        fetch[0].wait()
        fetch[1].wait()
    act = [(jax.nn.silu(_dot(h, wg_ref[...])) * _dot(h, wu_ref[...])).astype(BF16) for h in hf]
    if fetch:
        fetch[2].wait()
    for rs, v, a in zip(halves, x1, act):
        y_ref[rs, :] = v + _dot(a, wd_ref[...])


def _merge_ffn(x2d, h_m, h_a, g_ab, wa, wb, wo, nw, wg, wu, wd, tm):
    n = x2d.shape[0]
    row = lambda w: pl.BlockSpec((tm, w), lambda i: (i, 0))
    fetched = (wg, wu, wd)
    return pl.pallas_call(
        _merge_ffn_kernel,
        grid=(n // tm,),
        in_specs=[row(D_MODEL), row(M_V_W), row(A_Q_W), row(2 * D_MODEL)]
                 + [_const_spec(w.shape) for w in (wa, wb, wo, nw)]
                 + [pl.BlockSpec(memory_space=pl.ANY) for _ in fetched],
        out_specs=row(D_MODEL),
        out_shape=jax.ShapeDtypeStruct((n, D_MODEL), F32),
        scratch_shapes=[pltpu.VMEM(w.shape, w.dtype) for w in fetched]
                       + [pltpu.SemaphoreType.DMA((len(fetched),))],
        compiler_params=_params(("arbitrary",)),
        name="merge_ffn",
    )(x2d, h_m, h_a, g_ab, wa, wb, wo, nw, wg, wu, wd)


def kernel(x_prompt, x_sample, state_mlstm_C, state_mlstm_n, state_mlstm_m, cache_swa_k, cache_swa_v,
           norm_mix_w, w_in, mlstm_i_bias, mlstm_f_bias, mlstm_norm_w, q_norm_w, k_norm_w, attn_sinks,
           w_branch_a, w_branch_b, w_out, norm_ffn_w, w_gate, w_up, w_down):
    depth = w_in.shape[0]
    assert depth == 1, "single trunk layer"
    l = 0
    bp, tp = x_prompt.shape[0], x_prompt.shape[1]
    bs, ts = x_sample.shape[0], x_sample.shape[1]
    assert tp % TOKEN_TILE == 0 and (bs * ts) % GROUP == 0 and GROUP % ts == 0 and ts & (ts - 1) == 0
    assert ts <= SUBLANES, "sample chunk must fit one sublane tile"

    wt = jnp.transpose(w_in[l])
    c_km, c_vm = M_QK_W, 2 * M_QK_W
    c_g = 2 * M_QK_W + 2 * M_V_W
    c_qa = c_g + 2 * M_HEADS
    c_ka = c_qa + A_Q_W
    gate_pad = jnp.zeros((BF16_ROWS - 2 * M_HEADS, D_MODEL), F32)
    w_all = jnp.concatenate([wt[0:c_km], wt[c_vm:c_g],
                             wt[c_qa:c_ka], wt[c_g:c_qa], gate_pad,
                             wt[c_km:c_vm], wt[c_ka:]],
                            axis=0).astype(BF16)
    assert w_all.shape[0] == W_ROWS_T + W_ROWS_Q + M_QK_W + 2 * A_KV_W + 2 * D_MODEL
    head_of = jnp.arange(A_KV_W) // A_HEAD_DIM
    bd = (head_of[:, None] == head_of[None, :]).astype(BF16)
    qcol = (jnp.tile(q_norm_w[l], A_HEADS) * (A_HEAD_DIM ** -0.5)).reshape(A_Q_W, 1)
    krow = jnp.tile(k_norm_w[l], A_KV_HEADS).reshape(1, A_KV_W)
    nw_mix = norm_mix_w[l].reshape(1, D_MODEL)
    nw_ffn = norm_ffn_w[l].reshape(1, D_MODEL)
    bias_col = jnp.concatenate([mlstm_i_bias[l], mlstm_f_bias[l]]).reshape(2 * M_HEADS, 1)
    nw_col = mlstm_norm_w[l].reshape(M_V_W, 1)
    sinks = attn_sinks[l]
    proj = lambda x2d, tm: _proj(x2d, nw_mix, w_all, bd, qcol, krow, tm)

    xp = x_prompt.reshape(bp * tp, D_MODEL)
    later_ws = (w_branch_a[l], w_branch_b[l], w_out[l], w_gate[l], w_up[l], w_down[l])
    g_ab, h_m, h_a, kwin_p, vwin_p, st_p, m_p, wa, wb, wo, wg, wu, wd = _front(
        xp, bp, sinks, nw_mix, w_all, bd, qcol, krow, bias_col, nw_col, later_ws, TOKEN_TILE)
    merge = lambda x2d, h_m, h_a, g_ab, tm: _merge_ffn(x2d, h_m, h_a, g_ab, wa, wb, wo, nw_ffn, wg, wu, wd, tm)
    yp = merge(xp, h_m, h_a, g_ab, TOKEN_TILE).reshape(bp, tp, D_MODEL)
    c_p = jnp.swapaxes(st_p[:, :M_DV, :], 1, 2).reshape(bp, M_HEADS, M_DK, M_DV)
    n_p = st_p[:, M_DV, :].reshape(bp, M_HEADS, M_DK)
    m_pr = m_p[:, :M_HEADS, 0]

    ns = bs * ts
    xs = x_sample.reshape(ns, D_MODEL)
    tms = TOKEN_TILE if ns % TOKEN_TILE == 0 else GROUP
    qvot, gt, qat, k_m, kv_a, g_ab = proj(xs, tms)
    ngrp = ns // GROUP
    m_lanes = jnp.repeat(state_mlstm_m[l], ts, axis=0).reshape(ngrp, GROUP, M_HEADS)
    mrow = jnp.pad(jnp.swapaxes(m_lanes, 1, 2), ((0, 0), (0, SUBLANES - M_HEADS), (0, 0)))
    h_m, c_s, n_s, mt_s = _mlstm_sample(qvot, gt, k_m, mrow, bias_col, nw_col,
                                        state_mlstm_C[l].reshape(bs, M_QK_W, M_DV),
                                        state_mlstm_n[l].reshape(bs, M_QK_W), ts)
    wbuf = cache_swa_k.shape[2]
    to_fm = lambda a: jnp.transpose(a, (0, 2, 3, 1)).reshape(bs, A_KV_W, wbuf)
    from_fm = lambda a: jnp.transpose(a.reshape(bs, A_KV_HEADS, A_HEAD_DIM, wbuf), (0, 3, 1, 2))[None]
    h_a, kwin_s, vwin_s = _swa_sample(qat, kv_a, to_fm(cache_swa_k[l]), to_fm(cache_swa_v[l]), sinks, ts)
    ys = merge(xs, h_m, h_a, g_ab, tms).reshape(bs, ts, D_MODEL)
    m_s = jnp.swapaxes(mt_s[:, :M_HEADS, :], 1, 2).reshape(bs, ts, M_HEADS)[:, ts - 1, :]

    kv5 = lambda a: a.reshape(a.shape[0], a.shape[1], A_KV_HEADS, A_HEAD_DIM)[None]
    return (yp, ys,
            c_p[None], n_p[None], m_pr[None], kv5(kwin_p), kv5(vwin_p),
            c_s.reshape(bs, M_HEADS, M_DK, M_DV)[None], n_s.reshape(bs, M_HEADS, M_DK)[None], m_s[None],
            from_fm(kwin_s), from_fm(vwin_s))
```

```python
import functools

import jax
import jax.numpy as jnp
from jax import lax
from jax.experimental import pallas as pl
from jax.experimental.pallas import tpu as pltpu

F32 = jnp.float32
BF16 = jnp.bfloat16

D_MODEL = 1024
M_HEADS = 4
M_DK = 64
M_DV = 128
M_QK_W = M_HEADS * M_DK
M_V_W = M_HEADS * M_DV
A_HEADS = 8
A_KV_HEADS = 2
A_HEAD_DIM = 64
A_GROUPS = A_HEADS // A_KV_HEADS
A_Q_W = A_HEADS * A_HEAD_DIM
A_KV_W = A_KV_HEADS * A_HEAD_DIM
WINDOW = 128
EPS = 1e-6

LANES = 128
SUBLANES = 8
BF16_ROWS = 16
MXU_WIDTH = 256
FETCH_CHUNKS = 4
GROUP = 128
S_ROWS = M_DV + BF16_ROWS
TOKEN_TILE = 512
VMEM_LIMIT = 56 * 1024 * 1024
DK_SHIFT = M_DK.bit_length() - 1
WINDOW_SHIFT = WINDOW.bit_length() - 1
HALF = LANES // 2
assert A_HEAD_DIM == HALF and A_KV_W == LANES, "attention head pairs share one lane-width"

NT_DIMS = (((1,), (1,)), ((), ()))


def _dot(a, b):
    return jnp.dot(a, b, preferred_element_type=F32)


def _dot_nt(a, b):
    return lax.dot_general(a, b, NT_DIMS, preferred_element_type=F32)


def _const_spec(shape):
    nd = len(shape)
    return pl.BlockSpec(shape, lambda *_: (0,) * nd, pipeline_mode=pl.Buffered(1))


def _params(sem):
    return pltpu.CompilerParams(dimension_semantics=sem, vmem_limit_bytes=VMEM_LIMIT)


def _rms_rows(x, nw):
    ms = jnp.mean(x * x, axis=-1, keepdims=True)
    return (x * lax.rsqrt(ms + EPS)) * nw


GAB_CHUNK = 512


def _proj_chunks(x_ref, nw_ref, wmt_ref, wqgt_ref, wn_ref, bd_ref, qcol_ref, krow_ref,
                 qvot_ref, gt_ref, qat_ref, km_ref, kva_ref, gab_ref):
    hn = _rms_rows(x_ref[...], nw_ref[...]).astype(BF16)
    n_blk = qat_ref.shape[0]

    def put(ref, rows, val):
        for c in range(n_blk):
            ref[c, rows, :] = val[:, c * LANES:(c + 1) * LANES].astype(ref.dtype)

    def mlstm_qvo():
        put(qvot_ref, slice(None), _dot_nt(wmt_ref[...], hn))

    def attn_q_and_gates():
        qt = _dot_nt(wqgt_ref[...], hn)
        put(gt_ref, slice(None), qt[A_Q_W:A_Q_W + 2 * M_HEADS])
        for h in range(A_HEADS):
            hs = slice(h * A_HEAD_DIM, (h + 1) * A_HEAD_DIM)
            blk = qt[hs]
            ssq_q = jnp.sum(blk * blk, axis=0, keepdims=True)
            put(qat_ref, hs, (blk * lax.rsqrt(ssq_q * (1.0 / A_HEAD_DIM) + EPS)) * qcol_ref[hs])

    def mlstm_k():
        km_ref[...] = _dot_nt(hn, wn_ref[0:M_QK_W, :]).astype(km_ref.dtype)

    def branch_gates(c0):
        def run():
            w0 = M_QK_W + 2 * A_KV_W + c0
            gab_ref[:, c0:c0 + GAB_CHUNK] = _dot_nt(hn, wn_ref[w0:w0 + GAB_CHUNK, :]).astype(gab_ref.dtype)
        return run

    def attn_kv():
        kv = _dot_nt(hn, wn_ref[M_QK_W:M_QK_W + 2 * A_KV_W, :])
        k = kv[:, 0:A_KV_W]
        ksq = k * k
        hi = ksq.astype(BF16)
        lo = (ksq - hi.astype(F32)).astype(BF16)
        ssq = _dot(hi, bd_ref[...]) + _dot(lo, bd_ref[...])
        kva_ref[:, 0:A_KV_W] = (k * lax.rsqrt(ssq * (1.0 / A_HEAD_DIM) + EPS)) * krow_ref[...]
        kva_ref[:, A_KV_W:] = kv[:, A_KV_W:]

    return ([mlstm_qvo, attn_q_and_gates, mlstm_k]
            + [branch_gates(c0) for c0 in range(0, gab_ref.shape[1], GAB_CHUNK)] + [attn_kv])


W_ROWS_T = M_QK_W + 2 * M_V_W
W_ROWS_Q = A_Q_W + BF16_ROWS


def _weight_views(w_ref):
    return w_ref.at[0:W_ROWS_T], w_ref.at[W_ROWS_T:W_ROWS_T + W_ROWS_Q], w_ref.at[W_ROWS_T + W_ROWS_Q:]


def _proj_kernel(x_ref, nw_ref, w_ref, *rest):
    for piece in _proj_chunks(x_ref, nw_ref, *_weight_views(w_ref), *rest):
        piece()


def _proj(x2d, nw, w_all, bd, qcol, krow, tm):
    n = x2d.shape[0]
    row = lambda w: pl.BlockSpec((tm, w), lambda i: (i, 0))
    n_blk = tm // LANES
    slab = lambda r: pl.BlockSpec((n_blk, r, LANES), lambda i: (i, 0, 0))
    w_qvo = M_QK_W + 2 * M_V_W
    return pl.pallas_call(
        _proj_kernel,
        grid=(n // tm,),
        in_specs=[row(D_MODEL)] + [_const_spec(a.shape) for a in (nw, w_all, bd, qcol, krow)],
        out_specs=[slab(w_qvo), slab(2 * M_HEADS), slab(A_Q_W), row(M_QK_W), row(2 * A_KV_W), row(2 * D_MODEL)],
        out_shape=[jax.ShapeDtypeStruct((n // LANES, w_qvo, LANES), BF16),
                   jax.ShapeDtypeStruct((n // LANES, 2 * M_HEADS, LANES), F32),
                   jax.ShapeDtypeStruct((n // LANES, A_Q_W, LANES), BF16),
                   jax.ShapeDtypeStruct((n, M_QK_W), BF16),
                   jax.ShapeDtypeStruct((n, 2 * A_KV_W), F32),
                   jax.ShapeDtypeStruct((n, 2 * D_MODEL), BF16)],
        compiler_params=_params(("arbitrary",)),
        name="proj",
    )(x2d, nw, w_all, bd, qcol, krow)


def _split3_rows(x):
    hi = x.astype(BF16).astype(F32)
    r1 = x - hi
    mid = r1.astype(BF16).astype(F32)
    lo = r1 - mid
    return jnp.concatenate([hi, mid, lo], axis=0).astype(BF16)


def _log_sigmoid(x):
    return jnp.minimum(x, 0.0) - jnp.log1p(jnp.exp(-jnp.abs(x)))


def _chunk_masks(chunk_shift):
    s = lax.broadcasted_iota(jnp.int32, (GROUP, GROUP), 0)
    t = lax.broadcasted_iota(jnp.int32, (GROUP, GROUP), 1)
    same = (s >> chunk_shift) == (t >> chunk_shift)
    return same, same & (s <= t)


def _groups_gates(gts, same, causal):
    n = len(gts)
    assert 2 * M_HEADS == SUBLANES and n * SUBLANES <= GROUP
    cm_bf = jnp.where(causal, 1.0, 0.0).astype(BF16)
    lf = _log_sigmoid(jnp.concatenate(gts, axis=0))
    nr = n * SUBLANES
    bt3 = _dot(_split3_rows(lf), cm_bf)
    bt = (bt3[0:nr] + bt3[nr:2 * nr]) + bt3[2 * nr:3 * nr]
    b4 = [bt[g * SUBLANES + M_HEADS:(g + 1) * SUBLANES] for g in range(n)]
    a4 = [gts[g][0:M_HEADS] - b4[g] for g in range(n)]
    a_rows = [a for g in range(n) for a in (a4[g], a4[g])]
    if nr < GROUP:
        a_rows.append(jnp.zeros((GROUP - nr, GROUP), F32))
    a_cols = jnp.concatenate(a_rows, axis=0).T
    out = []
    for g in range(n):
        at_mats, run_rows, chunk_rows = [], [], []
        for h in range(M_HEADS):
            c = g * SUBLANES + h
            at = jnp.broadcast_to(a_cols[:, c:c + 1], (GROUP, GROUP))
            at_mats.append(at)
            run_rows.append(jnp.max(jnp.where(causal, at, -jnp.inf), axis=0, keepdims=True))
            chunk_rows.append(jnp.max(jnp.where(same, at, -jnp.inf), axis=0, keepdims=True))
        out.append((b4[g], a4[g], at_mats, jnp.concatenate(run_rows, axis=0), jnp.concatenate(chunk_rows, axis=0)))
    return out


def _group_weights(m_prev, b4, a4, run4, chunk4):
    big_m = jnp.maximum(m_prev, run4)
    m_last = jnp.maximum(m_prev, chunk4)
    w_inter = jnp.exp(m_prev - big_m)
    g_vec = jnp.exp(m_prev - m_last)
    w_last = jnp.exp(a4 - m_last)
    m_t = b4 + big_m
    return big_m, w_inter, g_vec, w_last, m_t, jnp.exp(-m_t)


def _group_scores(qvot, ks):
    lane_head = lax.broadcasted_iota(jnp.int32, (1, M_QK_W), 1) >> DK_SHIFT
    row_head = lax.broadcasted_iota(jnp.int32, (M_QK_W, 1), 0) >> DK_SHIFT
    qt = qvot(0, M_QK_W)
    k_stack = jnp.concatenate([jnp.where(lane_head == h, ks, jnp.zeros_like(ks)) for h in range(M_HEADS)], axis=0)
    qw = jnp.concatenate([jnp.where(row_head == h, qt, jnp.zeros_like(qt)) for h in range(M_HEADS)], axis=1)
    zero_blk = jnp.zeros((M_DK, GROUP), BF16)
    sc_t = []
    for h in range(0, M_HEADS, 2):
        q_pair = jnp.concatenate(
            [jnp.concatenate([qvot(h * M_DK, (h + 1) * M_DK), zero_blk], axis=1),
             jnp.concatenate([zero_blk, qvot((h + 1) * M_DK, (h + 2) * M_DK)], axis=1)], axis=0)
        sc = _dot(ks[:, h * M_DK:(h + 2) * M_DK], q_pair)
        sc_t += [sc[:, 0:GROUP], sc[:, GROUP:]]
    return k_stack, qw, sc_t


def _group_values(qvot, sc_t, at_mats, big_m, causal):
    ones_rows = jnp.where(lax.broadcasted_iota(jnp.int32, (BF16_ROWS, GROUP), 0) == 0, 1.0, 0.0).astype(BF16)
    zero_blk = jnp.zeros((GROUP, GROUP), BF16)
    vta, s_t = [], []
    for h in range(M_HEADS):
        w_t = jnp.where(causal, jnp.exp(at_mats[h] - big_m[h:h + 1]), 0.0)
        s_t.append((sc_t[h] * w_t).astype(BF16))
        vta.append(jnp.concatenate([qvot(M_QK_W + h * M_DV, M_QK_W + (h + 1) * M_DV), ones_rows], axis=0))
    intra = []
    for h in range(0, M_HEADS, 2):
        pair = jnp.concatenate([jnp.concatenate([s_t[h], zero_blk], axis=1),
                                jnp.concatenate([zero_blk, s_t[h + 1]], axis=1)], axis=0)
        intra.append(_dot(jnp.concatenate(vta[h:h + 2], axis=1), pair))
    return vta, jnp.concatenate(intra, axis=1)


def _weighted_values(vta, w_rows):
    return jnp.concatenate([(vta[h].astype(F32) * w_rows[h:h + 1]).astype(BF16) for h in range(M_HEADS)], axis=1)


def _lanes_x(rows):
    return jnp.concatenate([rows[h:h + 1] for h in range(M_HEADS)], axis=1)


def _group_out(qvot, inter, intra, w_inter, e_neg_m, nw_ref):
    outs = []
    out_all = inter * _lanes_x(w_inter) + intra
    for h in range(M_HEADS):
        out_t = out_all[:, h * GROUP:(h + 1) * GROUP]
        hh = out_t[0:M_DV] / jnp.maximum(jnp.abs(out_t[M_DV:M_DV + 1]), e_neg_m[h:h + 1])
        ms = jnp.mean(hh * hh, axis=0, keepdims=True)
        hn = (hh * lax.rsqrt(ms + EPS)) * nw_ref[h * M_DV:(h + 1) * M_DV]
        o_t = qvot(M_QK_W + M_V_W + h * M_DV, M_QK_W + M_V_W + (h + 1) * M_DV)
        outs.append(hn * jax.nn.sigmoid(o_t.astype(F32)))
    return jnp.concatenate(outs, axis=0).T


def _decay_row(g_vec, lane0):
    lane_head = lax.broadcasted_iota(jnp.int32, (1, M_QK_W), 1) >> DK_SHIFT
    g_row = jnp.zeros((1, M_QK_W), F32)
    for h in range(M_HEADS):
        g_row = jnp.where(lane_head == h, g_vec[h:h + 1, lane0:lane0 + 1], g_row)
    return g_row


def _mlstm_sample_kernel(seq_len, qvot_ref, gt_ref, k_ref, mrow_ref, bias_ref, nw_ref, c_ref, n_ref,
                         h_ref, c_out_ref, n_out_ref, mt_ref):
    n_seq = GROUP // seq_len
    shift = seq_len.bit_length() - 1
    same, causal = _chunk_masks(shift)
    lane_seq = lax.broadcasted_iota(jnp.int32, (1, GROUP), 1) >> shift
    lane_seq_x = jnp.concatenate([lane_seq] * M_HEADS, axis=1)
    row0 = lax.broadcasted_iota(jnp.int32, (BF16_ROWS, M_QK_W), 0) == 0
    qvot = lambda r0, r1: qvot_ref[0, r0:r1, :]
    ks = k_ref[...] * (M_DK ** -0.5)
    (b4, a4, at_mats, run4, chunk4), = _groups_gates([gt_ref[0] + bias_ref[...]], same, causal)
    big_m, w_inter, g_vec, w_last, m_t, e_neg_m = _group_weights(mrow_ref[0, 0:M_HEADS, :], b4, a4, run4, chunk4)
    mt_ref[0] = jnp.concatenate([m_t, m_t], axis=0)
    k_stack, qw, sc_t = _group_scores(qvot, ks)

    st_old = []
    for s in range(n_seq):
        n_rows = jnp.where(row0, jnp.broadcast_to(n_ref[s:s + 1, :], (BF16_ROWS, M_QK_W)), 0.0)
        st_old.append(jnp.concatenate([c_ref[s].T, n_rows], axis=0))
    inter_all = _dot(jnp.concatenate(st_old, axis=0).astype(BF16), qw)
    vta, intra = _group_values(qvot, sc_t, at_mats, big_m, causal)
    inter = inter_all[0:S_ROWS]
    for s in range(1, n_seq):
        inter = jnp.where(lane_seq_x == s, inter_all[s * S_ROWS:(s + 1) * S_ROWS], inter)
    tall = jnp.concatenate([_weighted_values(vta, jnp.where(lane_seq == s, w_last, 0.0)) for s in range(n_seq)],
                           axis=0)
    d_st = _dot(tall, k_stack)
    for s in range(n_seq):
        st_new = _decay_row(g_vec, s * seq_len) * st_old[s] + d_st[s * S_ROWS:(s + 1) * S_ROWS]
        c_out_ref[s] = st_new[0:M_DV].T
        n_out_ref[s:s + 1, :] = st_new[M_DV:M_DV + 1]
    h_ref[...] = _group_out(qvot, inter, intra, w_inter, e_neg_m, nw_ref).astype(h_ref.dtype)


def _mlstm_sample(qvot, gt, k_m, mrow, bias_col, nw_col, c, n, seq_len):
    ngrp = qvot.shape[0]
    n_seq = GROUP // seq_len
    full = lambda a: pl.BlockSpec(a.shape, lambda i: (0,) * a.ndim)
    slab = lambda a: pl.BlockSpec((1,) + a.shape[1:], lambda i: (i, 0, 0))
    row = lambda w: pl.BlockSpec((GROUP, w), lambda i: (i, 0))
    c_spec = pl.BlockSpec((n_seq,) + c.shape[1:], lambda i: (i, 0, 0))
    n_spec = pl.BlockSpec((n_seq, n.shape[1]), lambda i: (i, 0))
    return pl.pallas_call(
        functools.partial(_mlstm_sample_kernel, seq_len),
        grid=(ngrp,),
        in_specs=[slab(qvot), slab(gt), row(M_QK_W), slab(mrow), full(bias_col), full(nw_col), c_spec, n_spec],
        out_specs=[row(M_V_W), c_spec, n_spec, slab(mrow)],
        out_shape=[jax.ShapeDtypeStruct((ngrp * GROUP, M_V_W), BF16),
                   jax.ShapeDtypeStruct(c.shape, F32),
                   jax.ShapeDtypeStruct(n.shape, F32),
                   jax.ShapeDtypeStruct(mrow.shape, F32)],
        compiler_params=_params(("arbitrary",)),
        name="mlstm_sample",
    )(qvot, gt, k_m, mrow, bias_col, nw_col, c, n)


def _swa_scores(qbs, has_prev, sink_ref, qt_ref, kv_ref, kvp_ref):
    hd = A_HEAD_DIM
    nq = A_GROUPS * WINDOW
    si = lax.broadcasted_iota(jnp.int32, (2 * WINDOW, nq), 0)
    qi = lax.broadcasted_iota(jnp.int32, (2 * WINDOW, nq), 1) & (WINDOW - 1)
    local = ((si < WINDOW) & (si > qi)) | ((si >= WINDOW) & (si - WINDOW <= qi))
    first = local & (has_prev | (si >= WINDOW))
    lane_grp = lax.broadcasted_iota(jnp.int32, (1, nq), 1) >> WINDOW_SHIFT
    kv_block = lambda i: kvp_ref[...] if i == 0 else kv_ref[(i - 1) * WINDOW:i * WINDOW, :]
    k_bf = lambda i: kv_block(i)[:, 0:A_KV_W].astype(BF16)
    vt_bf = lambda i: kv_block(i)[:, A_KV_W:].T.astype(BF16)
    zeros = jnp.zeros((hd, nq), BF16)
    sinks = []
    for kvh in range(A_KV_HEADS):
        sk = jnp.zeros((1, nq), F32)
        for g in range(A_GROUPS):
            sk = jnp.where(lane_grp == g, sink_ref[kvh * A_GROUPS + g], sk)
        sinks.append(sk)
    units = []
    for qb in qbs:
        kk = jnp.concatenate([k_bf(qb), k_bf(qb + 1)], axis=0)
        vt = jnp.concatenate([vt_bf(qb), vt_bf(qb + 1)], axis=1)
        mask = local if qb > 0 else first
        for kvh in range(A_KV_HEADS):
            q4t = jnp.concatenate(
                [qt_ref[qb, (kvh * A_GROUPS + g) * hd:(kvh * A_GROUPS + g + 1) * hd, :] for g in range(A_GROUPS)],
                axis=1)
            wq = jnp.concatenate([q4t, zeros] if kvh == 0 else [zeros, q4t], axis=0)
            units.append((jnp.where(mask, _dot(kk, wq), -jnp.inf), sinks[kvh], vt))
    return units


def _swa_finish(qbs, units, h_ref):
    hd = A_HEAD_DIM
    for i, qb in enumerate(qbs):
        pieces = []
        for kvh in range(A_KV_HEADS):
            s, sk, vt = units[i * A_KV_HEADS + kvh]
            mx = jnp.maximum(jnp.max(s, axis=0, keepdims=True), sk)
            p = jnp.exp(s - mx)
            den = jnp.sum(p, axis=0, keepdims=True) + jnp.exp(sk - mx)
            ot = _dot(vt[kvh * hd:(kvh + 1) * hd], p.astype(BF16)) / den
            pieces += [ot[:, g * WINDOW:(g + 1) * WINDOW] for g in range(A_GROUPS)]
        h_t = jnp.concatenate(pieces, axis=0)
        h_ref[qb * WINDOW:(qb + 1) * WINDOW, :] = h_t.T.astype(h_ref.dtype)


def _front_kernel(tiles_per_seq, n_cast, *refs):
    (sink_ref, x_ref, nw_ref, w_ref, bd_ref, qcol_ref, krow_ref, bias_ref, nwm_ref) = refs[:9]
    wmt_ref, wqgt_ref, wn_ref = _weight_views(w_ref)
    cast_in = refs[9:9 + n_cast]
    (gab_ref, hm_ref, ha_ref, kwin_ref, vwin_ref, st_ref, m_ref) = refs[9 + n_cast:16 + n_cast]
    cast_out = refs[16 + n_cast:16 + 2 * n_cast]
    set_a, set_b = refs[16 + 2 * n_cast:21 + 2 * n_cast], refs[21 + 2 * n_cast:26 + 2 * n_cast]
    st_s, m_s, kvp_s = refs[26 + 2 * n_cast:]
    odd = (pl.program_id(0) & 1) == 1
    for parity, q_set, p_set in ((jnp.logical_not(odd), set_a, set_b), (odd, set_b, set_a)):
        pl.when(parity)(functools.partial(
            _front_body, tiles_per_seq, sink_ref, x_ref, nw_ref, wmt_ref, wqgt_ref, wn_ref, bd_ref, qcol_ref,
            krow_ref, bias_ref, nwm_ref, gab_ref, hm_ref, ha_ref, kwin_ref, vwin_ref, st_ref, m_ref,
            *q_set, *p_set, st_s, m_s, kvp_s))
    for src, dst in zip(cast_in, cast_out):
        dst[...] = src[...].astype(dst.dtype)


def _front_body(tiles_per_seq, sink_ref, x_ref, nw_ref, wmt_ref, wqgt_ref, wn_ref, bd_ref, qcol_ref, krow_ref,
                bias_ref, nwm_ref,
                gab_ref, hm_ref, ha_ref, kwin_ref, vwin_ref, st_ref, m_ref,
                q_qvot, q_gt, q_qat, q_km, q_kva, p_qvot, p_gt, p_qat, p_km, p_kva, st_s, m_s, kvp_s):
    k = pl.program_id(0)

    @pl.when(k == 0)
    def _():
        for r in (p_qvot, p_gt, p_qat, p_km, p_kva, st_s, m_s, kvp_s):
            r[...] = jnp.zeros(r.shape, r.dtype)

    pieces = _proj_chunks(x_ref, nw_ref, wmt_ref, wqgt_ref, wn_ref, bd_ref, qcol_ref, krow_ref,
                          q_qvot, q_gt, q_qat, q_km, q_kva, gab_ref)

    seq_start = lax.rem(k - 1 + tiles_per_seq, tiles_per_seq) == 0
    n_blk = p_qat.shape[0]
    same, causal = _chunk_masks(GROUP.bit_length() - 1)
    st = jnp.where(seq_start, 0.0, st_s[...])
    m_col = jnp.where(seq_start, 0.0, m_s[0:M_HEADS, 0:1])

    qvots = [lambda r0, r1, g=g: p_qvot[g, r0:r1, :] for g in range(n_blk)]
    scores = [_group_scores(qvots[g], p_km[g * GROUP:(g + 1) * GROUP, :] * (M_DK ** -0.5))
              for g in range(n_blk)]
    inter = [_dot(st.astype(BF16), scores[0][1])]
    gates = _groups_gates([p_gt[g] + bias_ref[...] for g in range(n_blk)], same, causal)
    for piece in pieces[0:1]:
        piece()
    grp = []
    for g in range(n_blk):
        b4, a4, at_mats, run4, chunk4 = gates[g]
        weights = _group_weights(jnp.broadcast_to(m_col, (M_HEADS, GROUP)), b4, a4, run4, chunk4)
        m_col = weights[4][:, GROUP - 1:GROUP]
        grp.append((qvots[g], at_mats, weights) + scores[g])
    has_prev = jnp.logical_not(seq_start)
    qb_lo, qb_hi = list(range(0, n_blk // 2)), list(range(n_blk // 2, n_blk))
    units_lo = _swa_scores(qb_lo, has_prev, sink_ref, p_qat, p_kva, kvp_s)
    for piece in pieces[1:3]:
        piece()

    vals = []

    def group_values(g):
        qvot, at_mats, (big_m, _, _, w_last, _, _), k_stack, _, sc_t = grp[g]
        vta, intra = _group_values(qvot, sc_t, at_mats, big_m, causal)
        vals.append((intra, _dot(_weighted_values(vta, w_last), k_stack)))

    for g in range(0, n_blk // 2):
        group_values(g)
    _swa_finish(qb_lo[:1], units_lo[:A_KV_HEADS], ha_ref)
    for piece in pieces[3:4]:
        piece()
    for g in range(n_blk // 2, n_blk):
        group_values(g)
    _swa_finish(qb_lo[1:], units_lo[A_KV_HEADS:], ha_ref)
    units_hi = _swa_scores(qb_hi, has_prev, sink_ref, p_qat, p_kva, kvp_s)
    for piece in pieces[4:6]:
        piece()

    for g in range(n_blk):
        st = _decay_row(grp[g][2][2], 0) * st + vals[g][1]
        if g + 1 < n_blk:
            inter.append(_dot(st.astype(BF16), grp[g + 1][4]))
        if g == n_blk // 2 - 1:
            _swa_finish(qb_hi[:1], units_hi[:A_KV_HEADS], ha_ref)
            for piece in pieces[6:7]:
                piece()
    _swa_finish(qb_hi[1:], units_hi[A_KV_HEADS:], ha_ref)
    for piece in pieces[7:]:
        piece()

    for g in range(n_blk):
        qvot, _, (_, w_inter, _, _, _, e_neg_m) = grp[g][0:3]
        hm_ref[g * GROUP:(g + 1) * GROUP, :] = _group_out(qvot, inter[g], vals[g][0], w_inter, e_neg_m,
                                                          nwm_ref).astype(hm_ref.dtype)
    st_s[...] = st
    st_ref[0] = st
    m_rows = jnp.broadcast_to(m_col, (M_HEADS, LANES))
    m_rows = jnp.concatenate([m_rows, m_rows], axis=0)
    m_s[...] = m_rows
    m_ref[0] = m_rows
    tm = p_kva.shape[0]
    kwin_ref[0] = p_kva[tm - WINDOW:, 0:A_KV_W]
    vwin_ref[0] = p_kva[tm - WINDOW:, A_KV_W:]
    kvp_s[...] = p_kva[tm - WINDOW:, :]


def _cast_rows(rows, n_steps):
    rb = BF16_ROWS
    while rows % rb or rows // rb > n_steps:
        rb += BF16_ROWS
    return rb


def _front(x2d, nb, sinks, nw, w_all, bd, qcol, krow, bias_col, nwm_col, cast_ws, tm):
    n = x2d.shape[0]
    n_tiles = n // tm
    tps = n_tiles // nb
    n_blk = tm // LANES
    w_qvo = M_QK_W + 2 * M_V_W
    cur = lambda w: pl.BlockSpec((tm, w), lambda k: (jnp.minimum(k, n_tiles - 1), 0))
    prev = lambda w: pl.BlockSpec((tm, w), lambda k: (jnp.maximum(k - 1, 0), 0))
    per_seq = lambda r, w: pl.BlockSpec((1, r, w), lambda k: (jnp.maximum(k - 1, 0) // tps, 0, 0))

    def cast_spec(a):
        rb = _cast_rows(a.shape[0], n_tiles)
        return pl.BlockSpec((rb, a.shape[1]), lambda k: (jnp.minimum(k, a.shape[0] // rb - 1), 0))

    cast_in_specs = [cast_spec(a) for a in cast_ws]
    cast_out_specs = [cast_spec(a) for a in cast_ws]
    proj_scratch = [pltpu.VMEM((n_blk, w_qvo, LANES), BF16), pltpu.VMEM((n_blk, 2 * M_HEADS, LANES), F32),
                    pltpu.VMEM((n_blk, A_Q_W, LANES), BF16), pltpu.VMEM((tm, M_QK_W), BF16),
                    pltpu.VMEM((tm, 2 * A_KV_W), F32)]
    return pl.pallas_call(
        functools.partial(_front_kernel, tps, len(cast_ws)),
        grid=(n_tiles + 1,),
        in_specs=[pl.BlockSpec(memory_space=pltpu.SMEM), cur(D_MODEL)]
                 + [_const_spec(a.shape) for a in (nw, w_all, bd, qcol, krow, bias_col, nwm_col)]
                 + cast_in_specs,
        out_specs=[cur(2 * D_MODEL), prev(M_V_W), prev(A_Q_W), per_seq(WINDOW, A_KV_W), per_seq(WINDOW, A_KV_W),
                   per_seq(S_ROWS, M_QK_W), per_seq(SUBLANES, LANES)] + cast_out_specs,
        out_shape=[jax.ShapeDtypeStruct((n, 2 * D_MODEL), BF16),
                   jax.ShapeDtypeStruct((n, M_V_W), BF16),
                   jax.ShapeDtypeStruct((n, A_Q_W), BF16),
                   jax.ShapeDtypeStruct((nb, WINDOW, A_KV_W), F32),
                   jax.ShapeDtypeStruct((nb, WINDOW, A_KV_W), F32),
                   jax.ShapeDtypeStruct((nb, S_ROWS, M_QK_W), F32),
                   jax.ShapeDtypeStruct((nb, SUBLANES, LANES), F32)]
                  + [jax.ShapeDtypeStruct(a.shape, BF16) for a in cast_ws],
        scratch_shapes=proj_scratch + proj_scratch + [pltpu.VMEM((S_ROWS, M_QK_W), F32),
                                                       pltpu.VMEM((SUBLANES, LANES), F32),
                                                       pltpu.VMEM((WINDOW, 2 * A_KV_W), F32)],
        compiler_params=_params(("arbitrary",)),
        name="front",
    )(sinks, x2d, nw, w_all, bd, qcol, krow, bias_col, nwm_col, *cast_ws)


def _swa_sample_kernel(seq_len, sink_ref, qt_ref, kv_ref, ck_ref, cv_ref, h_ref, kwin_ref, vwin_ref):
    n_seq = GROUP // seq_len
    wb = ck_ref.shape[2]
    lane = lax.broadcasted_iota(jnp.int32, (1, LANES), 1)
    lo_half = lane < HALF
    q_rows = qt_ref[0].astype(F32).T
    kv_new = kv_ref[...]
    k_new = kv_new[:, 0:A_KV_W].reshape(n_seq, seq_len, A_KV_W)
    v_new = kv_new[:, A_KV_W:].reshape(n_seq, seq_len, A_KV_W)

    def to_kv_half(x, head):
        kvh = head // A_GROUPS
        if head % 2 != kvh:
            x = pltpu.roll(x, HALF, axis=1)
        return jnp.where(lo_half if kvh == 0 else ~lo_half, x, 0.0)

    lhs = jnp.concatenate(
        [to_kv_half(q_rows[:, (h // 2) * LANES:(h // 2 + 1) * LANES], h).reshape(n_seq, seq_len, LANES)
         for h in range(A_HEADS)], axis=1).astype(BF16)
    zpad = jnp.zeros((n_seq, BF16_ROWS - seq_len, A_KV_W), F32)
    k_nb = jnp.concatenate([k_new, zpad], axis=1).astype(BF16)
    v_nb = jnp.concatenate([v_new, zpad], axis=1).astype(BF16)
    s_c = jnp.einsum('sqf,sfk->sqk', lhs, ck_ref[...].astype(BF16), preferred_element_type=F32)
    s_n = jnp.einsum('sqf,skf->sqk', lhs, k_nb, preferred_element_type=F32)
    nrow = A_HEADS * seq_len
    ti = lax.broadcasted_iota(jnp.int32, (nrow, wb), 0) & (seq_len - 1)
    ki = lax.broadcasted_iota(jnp.int32, (nrow, wb), 1)
    mask_c = (ti + wb - ki) < WINDOW
    ti_n = lax.broadcasted_iota(jnp.int32, (nrow, BF16_ROWS), 0) & (seq_len - 1)
    ki_n = lax.broadcasted_iota(jnp.int32, (nrow, BF16_ROWS), 1)
    mask_n = ki_n <= ti_n
    row_head = lax.broadcasted_iota(jnp.int32, (nrow, 1), 0) >> (seq_len.bit_length() - 1)
    sk = jnp.zeros((nrow, 1), F32)
    for h in range(A_HEADS):
        sk = jnp.where(row_head == h, sink_ref[h], sk)
    s_c = jnp.where(mask_c, s_c, -jnp.inf)
    s_n = jnp.where(mask_n, s_n, -jnp.inf)
    mx = jnp.maximum(jnp.maximum(jnp.max(s_c, axis=-1, keepdims=True), jnp.max(s_n, axis=-1, keepdims=True)), sk)
    p_c = jnp.exp(s_c - mx)
    p_n = jnp.exp(s_n - mx)
    den = jnp.sum(p_c, axis=-1, keepdims=True) + jnp.sum(p_n, axis=-1, keepdims=True) + jnp.exp(sk - mx)
    o = (jnp.einsum('sqk,sfk->sqf', p_c.astype(BF16), cv_ref[...].astype(BF16), preferred_element_type=F32)
         + jnp.einsum('sqk,skf->sqf', p_n.astype(BF16), v_nb, preferred_element_type=F32)) / den

    def from_kv_half(head):
        x = o[:, head * seq_len:(head + 1) * seq_len, :].reshape(GROUP, LANES)
        return pltpu.roll(x, HALF, axis=1) if head % 2 != head // A_GROUPS else x

    for c in range(A_HEADS // 2):
        h_ref[:, c * LANES:(c + 1) * LANES] = jnp.where(lo_half, from_kv_half(2 * c),
                                                        from_kv_half(2 * c + 1)).astype(h_ref.dtype)

    kt_new, vt_new = kv_new[:, 0:A_KV_W].T, kv_new[:, A_KV_W:].T
    for s in range(n_seq):
        put = (wb - seq_len - s * seq_len) % LANES
        for new_t, c_ref, win_ref in ((kt_new, ck_ref, kwin_ref), (vt_new, cv_ref, vwin_ref)):
            win_ref[s] = jnp.where(lane >= wb - seq_len, pltpu.roll(new_t, put, axis=1) if put else new_t,
                                   pltpu.roll(c_ref[s], wb - seq_len, axis=1))


def _swa_sample(qat, kv_a, cache_k, cache_v, sinks, seq_len):
    ngrp = qat.shape[0]
    n_seq = GROUP // seq_len
    wb = cache_k.shape[2]
    assert wb == LANES, "window positions fill one lane-width"
    row = lambda w: pl.BlockSpec((GROUP, w), lambda i: (i, 0))
    cache = pl.BlockSpec((n_seq, A_KV_W, wb), lambda i: (i, 0, 0))
    return pl.pallas_call(
        functools.partial(_swa_sample_kernel, seq_len),
        grid=(ngrp,),
        in_specs=[pl.BlockSpec(memory_space=pltpu.SMEM), pl.BlockSpec((1, A_Q_W, LANES), lambda i: (i, 0, 0)),
                  row(2 * A_KV_W), cache, cache],
        out_specs=[row(A_Q_W), cache, cache],
        out_shape=[jax.ShapeDtypeStruct((ngrp * GROUP, A_Q_W), BF16),
                   jax.ShapeDtypeStruct(cache_k.shape, F32),
                   jax.ShapeDtypeStruct(cache_v.shape, F32)],
        compiler_params=_params(("arbitrary",)),
        name="swa_sample",
    )(sinks, qat, kv_a, cache_k, cache_v)


def _merge_ffn_kernel(x_ref, hm_ref, ha_ref, gab_ref, wa_ref, wb_ref, wo_ref, nw_ref, wg_hbm, wu_hbm, wd_hbm,
                      y_ref, wg_ref, wu_ref, wd_ref, sems):
    cuts = _fetch_cuts(wg_ref.shape[1])
    pairs = [(src.at[:, c0:c1], dst.at[:, c0:c1]) for c0, c1 in zip(cuts[:-1], cuts[1:])
             for src, dst in ((wg_hbm, wg_ref), (wu_hbm, wu_ref))] + [(wd_hbm, wd_ref)]
    copies = [pltpu.make_async_copy(src, dst, sems.at[j]) for j, (src, dst) in enumerate(pairs)]
    first = pl.program_id(0) == 0
    for cond, fetch in ((first, copies), (jnp.logical_not(first), None)):
        pl.when(cond)(functools.partial(_merge_ffn_body, x_ref, hm_ref, ha_ref, gab_ref, wa_ref, wb_ref, wo_ref,
                                        nw_ref, wg_ref, wu_ref, wd_ref, y_ref, fetch))


def _fetch_cuts(cols):
    units = cols // MXU_WIDTH
    assert units * MXU_WIDTH == cols
    return [units * j // FETCH_CHUNKS * MXU_WIDTH for j in range(FETCH_CHUNKS + 1)]


def _merge_ffn_body(x_ref, hm_ref, ha_ref, gab_ref, wa_ref, wb_ref, wo_ref, nw_ref, wg_ref, wu_ref, wd_ref, y_ref,
                    fetch):
    if fetch:
        for c in fetch:
            c.start()
    tm = x_ref.shape[0]
    halves = [slice(0, tm // 2), slice(tm // 2, tm)]
    mix = []
    for rs in halves:
        ga = jax.nn.sigmoid(gab_ref[rs, 0:D_MODEL].astype(F32))
        gb = jax.nn.sigmoid(gab_ref[rs, D_MODEL:].astype(F32))
        mix.append(ga * _dot(hm_ref[rs, :], wa_ref[...]) + gb * _dot(ha_ref[rs, :], wb_ref[...]))
    x1 = [x_ref[rs, :] + _dot(m.astype(BF16), wo_ref[...]) for rs, m in zip(halves, mix)]
    hf = [_rms_rows(v, nw_ref[...]).astype(BF16) for v in x1]
    if fetch:
        cuts = _fetch_cuts(wg_ref.shape[1])
        parts = []
        for j, (c0, c1) in enumerate(zip(cuts[:-1], cuts[1:])):
            fetch[2 * j].wait()
            fetch[2 * j + 1].wait()
            parts.append([(jax.nn.silu(_dot(h, wg_ref[:, c0:c1])) * _dot(h, wu_ref[:, c0:c1])).astype(BF16)
                          for h in hf])
        act = [jnp.concatenate(half_parts, axis=1) for half_parts in zip(*parts)]
        fetch[-1].wait()
    else:
        act = [(jax.nn.silu(_dot(h, wg_ref[...])) * _dot(h, wu_ref[...])).astype(BF16) for h in hf]
    for rs, v, a in zip(halves, x1, act):
        y_ref[rs, :] = v + _dot(a, wd_ref[...])


def _merge_ffn(x2d, h_m, h_a, g_ab, wa, wb, wo, nw, wg, wu, wd, tm):
    n = x2d.shape[0]
    row = lambda w: pl.BlockSpec((tm, w), lambda i: (i, 0))
    fetched = (wg, wu, wd)
    return pl.pallas_call(
        _merge_ffn_kernel,
        grid=(n // tm,),
        in_specs=[row(D_MODEL), row(M_V_W), row(A_Q_W), row(2 * D_MODEL)]
                 + [_const_spec(w.shape) for w in (wa, wb, wo, nw)]
                 + [pl.BlockSpec(memory_space=pl.ANY) for _ in fetched],
        out_specs=row(D_MODEL),
        out_shape=jax.ShapeDtypeStruct((n, D_MODEL), F32),
        scratch_shapes=[pltpu.VMEM(w.shape, w.dtype) for w in fetched]
                       + [pltpu.SemaphoreType.DMA((2 * FETCH_CHUNKS + 1,))],
        compiler_params=_params(("arbitrary",)),
        name="merge_ffn",
    )(x2d, h_m, h_a, g_ab, wa, wb, wo, nw, wg, wu, wd)


def kernel(x_prompt, x_sample, state_mlstm_C, state_mlstm_n, state_mlstm_m, cache_swa_k, cache_swa_v,
           norm_mix_w, w_in, mlstm_i_bias, mlstm_f_bias, mlstm_norm_w, q_norm_w, k_norm_w, attn_sinks,
           w_branch_a, w_branch_b, w_out, norm_ffn_w, w_gate, w_up, w_down):
    depth = w_in.shape[0]
    assert depth == 1, "single trunk layer"
    l = 0
    bp, tp = x_prompt.shape[0], x_prompt.shape[1]
    bs, ts = x_sample.shape[0], x_sample.shape[1]
    assert tp % TOKEN_TILE == 0 and (bs * ts) % GROUP == 0 and GROUP % ts == 0 and ts & (ts - 1) == 0
    assert ts <= SUBLANES, "sample chunk must fit one sublane tile"

    wt = jnp.transpose(w_in[l])
    c_km, c_vm = M_QK_W, 2 * M_QK_W
    c_g = 2 * M_QK_W + 2 * M_V_W
    c_qa = c_g + 2 * M_HEADS
    c_ka = c_qa + A_Q_W
    gate_pad = jnp.zeros((BF16_ROWS - 2 * M_HEADS, D_MODEL), F32)
    w_all = jnp.concatenate([wt[0:c_km], wt[c_vm:c_g],
                             wt[c_qa:c_ka], wt[c_g:c_qa], gate_pad,
                             wt[c_km:c_vm], wt[c_ka:]],
                            axis=0).astype(BF16)
    assert w_all.shape[0] == W_ROWS_T + W_ROWS_Q + M_QK_W + 2 * A_KV_W + 2 * D_MODEL
    head_of = jnp.arange(A_KV_W) // A_HEAD_DIM
    bd = (head_of[:, None] == head_of[None, :]).astype(BF16)
    qcol = (jnp.tile(q_norm_w[l], A_HEADS) * (A_HEAD_DIM ** -0.5)).reshape(A_Q_W, 1)
    krow = jnp.tile(k_norm_w[l], A_KV_HEADS).reshape(1, A_KV_W)
    nw_mix = norm_mix_w[l].reshape(1, D_MODEL)
    nw_ffn = norm_ffn_w[l].reshape(1, D_MODEL)
    bias_col = jnp.concatenate([mlstm_i_bias[l], mlstm_f_bias[l]]).reshape(2 * M_HEADS, 1)
    nw_col = mlstm_norm_w[l].reshape(M_V_W, 1)
    sinks = attn_sinks[l]
    proj = lambda x2d, tm: _proj(x2d, nw_mix, w_all, bd, qcol, krow, tm)

    xp = x_prompt.reshape(bp * tp, D_MODEL)
    later_ws = (w_branch_a[l], w_branch_b[l], w_out[l], w_gate[l], w_up[l], w_down[l])
    g_ab, h_m, h_a, kwin_p, vwin_p, st_p, m_p, wa, wb, wo, wg, wu, wd = _front(
        xp, bp, sinks, nw_mix, w_all, bd, qcol, krow, bias_col, nw_col, later_ws, TOKEN_TILE)
    merge = lambda x2d, h_m, h_a, g_ab, tm: _merge_ffn(x2d, h_m, h_a, g_ab, wa, wb, wo, nw_ffn, wg, wu, wd, tm)
    yp = merge(xp, h_m, h_a, g_ab, TOKEN_TILE).reshape(bp, tp, D_MODEL)
    c_p = jnp.swapaxes(st_p[:, :M_DV, :], 1, 2).reshape(bp, M_HEADS, M_DK, M_DV)
    n_p = st_p[:, M_DV, :].reshape(bp, M_HEADS, M_DK)
    m_pr = m_p[:, :M_HEADS, 0]

    ns = bs * ts
    xs = x_sample.reshape(ns, D_MODEL)
    tms = TOKEN_TILE if ns % TOKEN_TILE == 0 else GROUP
    qvot, gt, qat, k_m, kv_a, g_ab = proj(xs, tms)
    ngrp = ns // GROUP
    m_lanes = jnp.repeat(state_mlstm_m[l], ts, axis=0).reshape(ngrp, GROUP, M_HEADS)
    mrow = jnp.pad(jnp.swapaxes(m_lanes, 1, 2), ((0, 0), (0, SUBLANES - M_HEADS), (0, 0)))
    h_m, c_s, n_s, mt_s = _mlstm_sample(qvot, gt, k_m, mrow, bias_col, nw_col,
                                        state_mlstm_C[l].reshape(bs, M_QK_W, M_DV),
                                        state_mlstm_n[l].reshape(bs, M_QK_W), ts)
    wbuf = cache_swa_k.shape[2]
    to_fm = lambda a: jnp.transpose(a, (0, 2, 3, 1)).reshape(bs, A_KV_W, wbuf)
    from_fm = lambda a: jnp.transpose(a.reshape(bs, A_KV_HEADS, A_HEAD_DIM, wbuf), (0, 3, 1, 2))[None]
    h_a, kwin_s, vwin_s = _swa_sample(qat, kv_a, to_fm(cache_swa_k[l]), to_fm(cache_swa_v[l]), sinks, ts)
    ys = merge(xs, h_m, h_a, g_ab, tms).reshape(bs, ts, D_MODEL)
    m_s = jnp.swapaxes(mt_s[:, :M_HEADS, :], 1, 2).reshape(bs, ts, M_HEADS)[:, ts - 1, :]

    kv5 = lambda a: a.reshape(a.shape[0], a.shape[1], A_KV_HEADS, A_HEAD_DIM)[None]
    return (yp, ys,
            c_p[None], n_p[None], m_pr[None], kv5(kwin_p), kv5(vwin_p),
            c_s.reshape(bs, M_HEADS, M_DK, M_DV)[None], n_s.reshape(bs, M_HEADS, M_DK)[None], m_s[None],
            from_fm(kwin_s), from_fm(vwin_s))
```

```python
import functools

import jax
import jax.numpy as jnp
from jax import lax
from jax.experimental import pallas as pl
from jax.experimental.pallas import tpu as pltpu

F32 = jnp.float32
BF16 = jnp.bfloat16

D_MODEL = 1024
M_HEADS = 4
M_DK = 64
M_DV = 128
M_QK_W = M_HEADS * M_DK
M_V_W = M_HEADS * M_DV
A_HEADS = 8
A_KV_HEADS = 2
A_HEAD_DIM = 64
A_GROUPS = A_HEADS // A_KV_HEADS
A_Q_W = A_HEADS * A_HEAD_DIM
A_KV_W = A_KV_HEADS * A_HEAD_DIM
WINDOW = 128
EPS = 1e-6

LANES = 128
SUBLANES = 8
BF16_ROWS = 16
MXU_WIDTH = 256
FETCH_CHUNKS = 4
GROUP = 128
S_ROWS = M_DV + BF16_ROWS
TOKEN_TILE = 512
VMEM_LIMIT = 56 * 1024 * 1024
DK_SHIFT = M_DK.bit_length() - 1
WINDOW_SHIFT = WINDOW.bit_length() - 1
HALF = LANES // 2
assert A_HEAD_DIM == HALF and A_KV_W == LANES, "attention head pairs share one lane-width"

NT_DIMS = (((1,), (1,)), ((), ()))


def _dot(a, b):
    return jnp.dot(a, b, preferred_element_type=F32)


def _dot_nt(a, b):
    return lax.dot_general(a, b, NT_DIMS, preferred_element_type=F32)


def _const_spec(shape):
    nd = len(shape)
    return pl.BlockSpec(shape, lambda *_: (0,) * nd, pipeline_mode=pl.Buffered(1))


def _params(sem):
    return pltpu.CompilerParams(dimension_semantics=sem, vmem_limit_bytes=VMEM_LIMIT)


def _rms_rows(x, nw):
    ms = jnp.mean(x * x, axis=-1, keepdims=True)
    return (x * lax.rsqrt(ms + EPS)) * nw


GAB_CHUNK = 512


def _proj_chunks(x_ref, nw_ref, wmt_ref, wqgt_ref, wn_ref, bd_ref, qcol_ref, krow_ref,
                 qvot_ref, gt_ref, qat_ref, km_ref, kva_ref, gab_ref):
    hn = _rms_rows(x_ref[...], nw_ref[...]).astype(BF16)
    n_blk = qat_ref.shape[0]

    def put(ref, rows, val):
        for c in range(n_blk):
            ref[c, rows, :] = val[:, c * LANES:(c + 1) * LANES].astype(ref.dtype)

    def mlstm_qvo():
        put(qvot_ref, slice(None), _dot_nt(wmt_ref[...], hn))

    def attn_q_and_gates():
        qt = _dot_nt(wqgt_ref[...], hn)
        put(gt_ref, slice(None), qt[A_Q_W:A_Q_W + 2 * M_HEADS])
        for h in range(A_HEADS):
            hs = slice(h * A_HEAD_DIM, (h + 1) * A_HEAD_DIM)
            blk = qt[hs]
            ssq_q = jnp.sum(blk * blk, axis=0, keepdims=True)
            put(qat_ref, hs, (blk * lax.rsqrt(ssq_q * (1.0 / A_HEAD_DIM) + EPS)) * qcol_ref[hs])

    def mlstm_k():
        km_ref[...] = _dot_nt(hn, wn_ref[0:M_QK_W, :]).astype(km_ref.dtype)

    def branch_gates(c0):
        def run():
            w0 = M_QK_W + 2 * A_KV_W + c0
            gab_ref[:, c0:c0 + GAB_CHUNK] = _dot_nt(hn, wn_ref[w0:w0 + GAB_CHUNK, :]).astype(gab_ref.dtype)
        return run

    def attn_kv():
        kv = _dot_nt(hn, wn_ref[M_QK_W:M_QK_W + 2 * A_KV_W, :])
        k = kv[:, 0:A_KV_W]
        ksq = k * k
        hi = ksq.astype(BF16)
        lo = (ksq - hi.astype(F32)).astype(BF16)
        ssq = _dot(hi, bd_ref[...]) + _dot(lo, bd_ref[...])
        kva_ref[:, 0:A_KV_W] = (k * lax.rsqrt(ssq * (1.0 / A_HEAD_DIM) + EPS)) * krow_ref[...]
        kva_ref[:, A_KV_W:] = kv[:, A_KV_W:]

    return ([mlstm_qvo, attn_q_and_gates, mlstm_k]
            + [branch_gates(c0) for c0 in range(0, gab_ref.shape[1], GAB_CHUNK)] + [attn_kv])


W_ROWS_T = M_QK_W + 2 * M_V_W
W_ROWS_Q = A_Q_W + BF16_ROWS


def _weight_views(w_ref):
    return w_ref.at[0:W_ROWS_T], w_ref.at[W_ROWS_T:W_ROWS_T + W_ROWS_Q], w_ref.at[W_ROWS_T + W_ROWS_Q:]


def _proj_fetch_cuts(n_rows):
    small = W_ROWS_T + W_ROWS_Q + M_QK_W + 2 * A_KV_W
    half = (n_rows - small) // 2
    assert half % GAB_CHUNK == 0 and small + 2 * half == n_rows
    return [0, W_ROWS_T, small, small + half, n_rows]


def _proj_kernel(x_ref, nw_ref, w_hbm, *rest):
    *rest, w_ref, sems = rest
    cuts = _proj_fetch_cuts(w_ref.shape[0])
    copies = [pltpu.make_async_copy(w_hbm.at[r0:r1], w_ref.at[r0:r1], sems.at[j])
              for j, (r0, r1) in enumerate(zip(cuts[:-1], cuts[1:]))]
    first = pl.program_id(0) == 0
    for cond, fetch in ((first, copies), (jnp.logical_not(first), None)):
        pl.when(cond)(functools.partial(_proj_body, x_ref, nw_ref, w_ref, rest, fetch))


def _proj_body(x_ref, nw_ref, w_ref, rest, fetch):
    if fetch:
        for c in fetch:
            c.start()
    pieces = _proj_chunks(x_ref, nw_ref, *_weight_views(w_ref), *rest)
    if not fetch:
        for piece in pieces:
            piece()
        return
    ordered = pieces[0:3] + pieces[-1:] + pieces[3:-1]
    n_gates = len(pieces) - 4
    wait_at = {0: 0, 1: 1, 4: 2, 4 + n_gates // 2: 3}
    for j, piece in enumerate(ordered):
        if j in wait_at:
            fetch[wait_at[j]].wait()
        piece()


def _proj(x2d, nw, w_all, bd, qcol, krow, tm):
    n = x2d.shape[0]
    row = lambda w: pl.BlockSpec((tm, w), lambda i: (i, 0))
    n_blk = tm // LANES
    slab = lambda r: pl.BlockSpec((n_blk, r, LANES), lambda i: (i, 0, 0))
    w_qvo = M_QK_W + 2 * M_V_W
    return pl.pallas_call(
        _proj_kernel,
        grid=(n // tm,),
        in_specs=[row(D_MODEL), _const_spec(nw.shape), pl.BlockSpec(memory_space=pl.ANY)]
                 + [_const_spec(a.shape) for a in (bd, qcol, krow)],
        out_specs=[slab(w_qvo), slab(2 * M_HEADS), slab(A_Q_W), row(M_QK_W), row(2 * A_KV_W), row(2 * D_MODEL)],
        scratch_shapes=[pltpu.VMEM(w_all.shape, w_all.dtype),
                        pltpu.SemaphoreType.DMA((len(_proj_fetch_cuts(w_all.shape[0])) - 1,))],
        out_shape=[jax.ShapeDtypeStruct((n // LANES, w_qvo, LANES), BF16),
                   jax.ShapeDtypeStruct((n // LANES, 2 * M_HEADS, LANES), F32),
                   jax.ShapeDtypeStruct((n // LANES, A_Q_W, LANES), BF16),
                   jax.ShapeDtypeStruct((n, M_QK_W), BF16),
                   jax.ShapeDtypeStruct((n, 2 * A_KV_W), F32),
                   jax.ShapeDtypeStruct((n, 2 * D_MODEL), BF16)],
        compiler_params=_params(("arbitrary",)),
        name="proj",
    )(x2d, nw, w_all, bd, qcol, krow)


def _split3_rows(x):
    hi = x.astype(BF16).astype(F32)
    r1 = x - hi
    mid = r1.astype(BF16).astype(F32)
    lo = r1 - mid
    return jnp.concatenate([hi, mid, lo], axis=0).astype(BF16)


def _log_sigmoid(x):
    return jnp.minimum(x, 0.0) - jnp.log1p(jnp.exp(-jnp.abs(x)))


def _chunk_masks(chunk_shift):
    s = lax.broadcasted_iota(jnp.int32, (GROUP, GROUP), 0)
    t = lax.broadcasted_iota(jnp.int32, (GROUP, GROUP), 1)
    same = (s >> chunk_shift) == (t >> chunk_shift)
    return same, same & (s <= t)


def _groups_gates(gts, same, causal):
    n = len(gts)
    assert 2 * M_HEADS == SUBLANES and n * SUBLANES <= GROUP
    cm_bf = jnp.where(causal, 1.0, 0.0).astype(BF16)
    lf = _log_sigmoid(jnp.concatenate(gts, axis=0))
    nr = n * SUBLANES
    bt3 = _dot(_split3_rows(lf), cm_bf)
    bt = (bt3[0:nr] + bt3[nr:2 * nr]) + bt3[2 * nr:3 * nr]
    b4 = [bt[g * SUBLANES + M_HEADS:(g + 1) * SUBLANES] for g in range(n)]
    a4 = [gts[g][0:M_HEADS] - b4[g] for g in range(n)]
    a_rows = [a for g in range(n) for a in (a4[g], a4[g])]
    if nr < GROUP:
        a_rows.append(jnp.zeros((GROUP - nr, GROUP), F32))
    a_cols = jnp.concatenate(a_rows, axis=0).T
    out = []
    for g in range(n):
        at_mats, run_rows, chunk_rows = [], [], []
        for h in range(M_HEADS):
            c = g * SUBLANES + h
            at = jnp.broadcast_to(a_cols[:, c:c + 1], (GROUP, GROUP))
            at_mats.append(at)
            run_rows.append(jnp.max(jnp.where(causal, at, -jnp.inf), axis=0, keepdims=True))
            chunk_rows.append(jnp.max(jnp.where(same, at, -jnp.inf), axis=0, keepdims=True))
        out.append((b4[g], a4[g], at_mats, jnp.concatenate(run_rows, axis=0), jnp.concatenate(chunk_rows, axis=0)))
    return out


def _group_weights(m_prev, b4, a4, run4, chunk4):
    big_m = jnp.maximum(m_prev, run4)
    m_last = jnp.maximum(m_prev, chunk4)
    w_inter = jnp.exp(m_prev - big_m)
    g_vec = jnp.exp(m_prev - m_last)
    w_last = jnp.exp(a4 - m_last)
    m_t = b4 + big_m
    return big_m, w_inter, g_vec, w_last, m_t, jnp.exp(-m_t)


def _group_scores(qvot, ks):
    lane_head = lax.broadcasted_iota(jnp.int32, (1, M_QK_W), 1) >> DK_SHIFT
    row_head = lax.broadcasted_iota(jnp.int32, (M_QK_W, 1), 0) >> DK_SHIFT
    qt = qvot(0, M_QK_W)
    k_stack = jnp.concatenate([jnp.where(lane_head == h, ks, jnp.zeros_like(ks)) for h in range(M_HEADS)], axis=0)
    qw = jnp.concatenate([jnp.where(row_head == h, qt, jnp.zeros_like(qt)) for h in range(M_HEADS)], axis=1)
    zero_blk = jnp.zeros((M_DK, GROUP), BF16)
    sc_t = []
    for h in range(0, M_HEADS, 2):
        q_pair = jnp.concatenate(
            [jnp.concatenate([qvot(h * M_DK, (h + 1) * M_DK), zero_blk], axis=1),
             jnp.concatenate([zero_blk, qvot((h + 1) * M_DK, (h + 2) * M_DK)], axis=1)], axis=0)
        sc = _dot(ks[:, h * M_DK:(h + 2) * M_DK], q_pair)
        sc_t += [sc[:, 0:GROUP], sc[:, GROUP:]]
    return k_stack, qw, sc_t


def _group_values(qvot, sc_t, at_mats, big_m, causal):
    ones_rows = jnp.where(lax.broadcasted_iota(jnp.int32, (BF16_ROWS, GROUP), 0) == 0, 1.0, 0.0).astype(BF16)
    zero_blk = jnp.zeros((GROUP, GROUP), BF16)
    vta, s_t = [], []
    for h in range(M_HEADS):
        w_t = jnp.where(causal, jnp.exp(at_mats[h] - big_m[h:h + 1]), 0.0)
        s_t.append((sc_t[h] * w_t).astype(BF16))
        vta.append(jnp.concatenate([qvot(M_QK_W + h * M_DV, M_QK_W + (h + 1) * M_DV), ones_rows], axis=0))
    intra = []
    for h in range(0, M_HEADS, 2):
        pair = jnp.concatenate([jnp.concatenate([s_t[h], zero_blk], axis=1),
                                jnp.concatenate([zero_blk, s_t[h + 1]], axis=1)], axis=0)
        intra.append(_dot(jnp.concatenate(vta[h:h + 2], axis=1), pair))
    return vta, jnp.concatenate(intra, axis=1)


def _weighted_values(vta, w_rows):
    return jnp.concatenate([(vta[h].astype(F32) * w_rows[h:h + 1]).astype(BF16) for h in range(M_HEADS)], axis=1)


def _lanes_x(rows):
    return jnp.concatenate([rows[h:h + 1] for h in range(M_HEADS)], axis=1)


def _group_out(qvot, inter, intra, w_inter, e_neg_m, nw_ref):
    outs = []
    out_all = inter * _lanes_x(w_inter) + intra
    for h in range(M_HEADS):
        out_t = out_all[:, h * GROUP:(h + 1) * GROUP]
        hh = out_t[0:M_DV] / jnp.maximum(jnp.abs(out_t[M_DV:M_DV + 1]), e_neg_m[h:h + 1])
        ms = jnp.mean(hh * hh, axis=0, keepdims=True)
        hn = (hh * lax.rsqrt(ms + EPS)) * nw_ref[h * M_DV:(h + 1) * M_DV]
        o_t = qvot(M_QK_W + M_V_W + h * M_DV, M_QK_W + M_V_W + (h + 1) * M_DV)
        outs.append(hn * jax.nn.sigmoid(o_t.astype(F32)))
    return jnp.concatenate(outs, axis=0).T


def _decay_row(g_vec, lane0):
    lane_head = lax.broadcasted_iota(jnp.int32, (1, M_QK_W), 1) >> DK_SHIFT
    g_row = jnp.zeros((1, M_QK_W), F32)
    for h in range(M_HEADS):
        g_row = jnp.where(lane_head == h, g_vec[h:h + 1, lane0:lane0 + 1], g_row)
    return g_row


def _mlstm_sample_kernel(seq_len, qvot_ref, gt_ref, k_ref, mrow_ref, bias_ref, nw_ref, c_ref, n_ref,
                         h_ref, c_out_ref, n_out_ref, mt_ref):
    n_seq = GROUP // seq_len
    shift = seq_len.bit_length() - 1
    same, causal = _chunk_masks(shift)
    lane_seq = lax.broadcasted_iota(jnp.int32, (1, GROUP), 1) >> shift
    lane_seq_x = jnp.concatenate([lane_seq] * M_HEADS, axis=1)
    row0 = lax.broadcasted_iota(jnp.int32, (BF16_ROWS, M_QK_W), 0) == 0
    qvot = lambda r0, r1: qvot_ref[0, r0:r1, :]
    ks = k_ref[...] * (M_DK ** -0.5)
    (b4, a4, at_mats, run4, chunk4), = _groups_gates([gt_ref[0] + bias_ref[...]], same, causal)
    big_m, w_inter, g_vec, w_last, m_t, e_neg_m = _group_weights(mrow_ref[0, 0:M_HEADS, :], b4, a4, run4, chunk4)
    mt_ref[0] = jnp.concatenate([m_t, m_t], axis=0)
    k_stack, qw, sc_t = _group_scores(qvot, ks)

    st_old = []
    for s in range(n_seq):
        n_rows = jnp.where(row0, jnp.broadcast_to(n_ref[s:s + 1, :], (BF16_ROWS, M_QK_W)), 0.0)
        st_old.append(jnp.concatenate([c_ref[s].T, n_rows], axis=0))
    inter_all = _dot(jnp.concatenate(st_old, axis=0).astype(BF16), qw)
    vta, intra = _group_values(qvot, sc_t, at_mats, big_m, causal)
    inter = inter_all[0:S_ROWS]
    for s in range(1, n_seq):
        inter = jnp.where(lane_seq_x == s, inter_all[s * S_ROWS:(s + 1) * S_ROWS], inter)
    tall = jnp.concatenate([_weighted_values(vta, jnp.where(lane_seq == s, w_last, 0.0)) for s in range(n_seq)],
                           axis=0)
    d_st = _dot(tall, k_stack)
    for s in range(n_seq):
        st_new = _decay_row(g_vec, s * seq_len) * st_old[s] + d_st[s * S_ROWS:(s + 1) * S_ROWS]
        c_out_ref[s] = st_new[0:M_DV].T
        n_out_ref[s:s + 1, :] = st_new[M_DV:M_DV + 1]
    h_ref[...] = _group_out(qvot, inter, intra, w_inter, e_neg_m, nw_ref).astype(h_ref.dtype)


def _mlstm_sample(qvot, gt, k_m, mrow, bias_col, nw_col, c, n, seq_len):
    ngrp = qvot.shape[0]
    n_seq = GROUP // seq_len
    full = lambda a: pl.BlockSpec(a.shape, lambda i: (0,) * a.ndim)
    slab = lambda a: pl.BlockSpec((1,) + a.shape[1:], lambda i: (i, 0, 0))
    row = lambda w: pl.BlockSpec((GROUP, w), lambda i: (i, 0))
    c_spec = pl.BlockSpec((n_seq,) + c.shape[1:], lambda i: (i, 0, 0))
    n_spec = pl.BlockSpec((n_seq, n.shape[1]), lambda i: (i, 0))
    return pl.pallas_call(
        functools.partial(_mlstm_sample_kernel, seq_len),
        grid=(ngrp,),
        in_specs=[slab(qvot), slab(gt), row(M_QK_W), slab(mrow), full(bias_col), full(nw_col), c_spec, n_spec],
        out_specs=[row(M_V_W), c_spec, n_spec, slab(mrow)],
        out_shape=[jax.ShapeDtypeStruct((ngrp * GROUP, M_V_W), BF16),
                   jax.ShapeDtypeStruct(c.shape, F32),
                   jax.ShapeDtypeStruct(n.shape, F32),
                   jax.ShapeDtypeStruct(mrow.shape, F32)],
        compiler_params=_params(("arbitrary",)),
        name="mlstm_sample",
    )(qvot, gt, k_m, mrow, bias_col, nw_col, c, n)


def _swa_scores(qbs, has_prev, sink_ref, qt_ref, kv_ref, kvp_ref):
    hd = A_HEAD_DIM
    nq = A_GROUPS * WINDOW
    si = lax.broadcasted_iota(jnp.int32, (2 * WINDOW, nq), 0)
    qi = lax.broadcasted_iota(jnp.int32, (2 * WINDOW, nq), 1) & (WINDOW - 1)
    local = ((si < WINDOW) & (si > qi)) | ((si >= WINDOW) & (si - WINDOW <= qi))
    first = local & (has_prev | (si >= WINDOW))
    lane_grp = lax.broadcasted_iota(jnp.int32, (1, nq), 1) >> WINDOW_SHIFT
    kv_block = lambda i: kvp_ref[...] if i == 0 else kv_ref[(i - 1) * WINDOW:i * WINDOW, :]
    k_bf = lambda i: kv_block(i)[:, 0:A_KV_W].astype(BF16)
    vt_bf = lambda i: kv_block(i)[:, A_KV_W:].T.astype(BF16)
    zeros = jnp.zeros((hd, nq), BF16)
    sinks = []
    for kvh in range(A_KV_HEADS):
        sk = jnp.zeros((1, nq), F32)
        for g in range(A_GROUPS):
            sk = jnp.where(lane_grp == g, sink_ref[kvh * A_GROUPS + g], sk)
        sinks.append(sk)
    units = []
    for qb in qbs:
        kk = jnp.concatenate([k_bf(qb), k_bf(qb + 1)], axis=0)
        vt = jnp.concatenate([vt_bf(qb), vt_bf(qb + 1)], axis=1)
        mask = local if qb > 0 else first
        for kvh in range(A_KV_HEADS):
            q4t = jnp.concatenate(
                [qt_ref[qb, (kvh * A_GROUPS + g) * hd:(kvh * A_GROUPS + g + 1) * hd, :] for g in range(A_GROUPS)],
                axis=1)
            wq = jnp.concatenate([q4t, zeros] if kvh == 0 else [zeros, q4t], axis=0)
            units.append((jnp.where(mask, _dot(kk, wq), -jnp.inf), sinks[kvh], vt))
    return units


def _swa_finish(qbs, units, h_ref):
    hd = A_HEAD_DIM
    for i, qb in enumerate(qbs):
        pieces = []
        for kvh in range(A_KV_HEADS):
            s, sk, vt = units[i * A_KV_HEADS + kvh]
            mx = jnp.maximum(jnp.max(s, axis=0, keepdims=True), sk)
            p = jnp.exp(s - mx)
            den = jnp.sum(p, axis=0, keepdims=True) + jnp.exp(sk - mx)
            ot = _dot(vt[kvh * hd:(kvh + 1) * hd], p.astype(BF16)) / den
            pieces += [ot[:, g * WINDOW:(g + 1) * WINDOW] for g in range(A_GROUPS)]
        h_t = jnp.concatenate(pieces, axis=0)
        h_ref[qb * WINDOW:(qb + 1) * WINDOW, :] = h_t.T.astype(h_ref.dtype)


def _front_kernel(tiles_per_seq, n_cast, *refs):
    (sink_ref, x_ref, nw_ref, w_ref, bd_ref, qcol_ref, krow_ref, bias_ref, nwm_ref) = refs[:9]
    wmt_ref, wqgt_ref, wn_ref = _weight_views(w_ref)
    cast_in = refs[9:9 + n_cast]
    (gab_ref, hm_ref, ha_ref, kwin_ref, vwin_ref, st_ref, m_ref) = refs[9 + n_cast:16 + n_cast]
    cast_out = refs[16 + n_cast:16 + 2 * n_cast]
    set_a, set_b = refs[16 + 2 * n_cast:21 + 2 * n_cast], refs[21 + 2 * n_cast:26 + 2 * n_cast]
    st_s, m_s, kvp_s = refs[26 + 2 * n_cast:]
    odd = (pl.program_id(0) & 1) == 1
    for parity, q_set, p_set in ((jnp.logical_not(odd), set_a, set_b), (odd, set_b, set_a)):
        pl.when(parity)(functools.partial(
            _front_body, tiles_per_seq, sink_ref, x_ref, nw_ref, wmt_ref, wqgt_ref, wn_ref, bd_ref, qcol_ref,
            krow_ref, bias_ref, nwm_ref, gab_ref, hm_ref, ha_ref, kwin_ref, vwin_ref, st_ref, m_ref,
            *q_set, *p_set, st_s, m_s, kvp_s))
    for src, dst in zip(cast_in, cast_out):
        dst[...] = src[...].astype(dst.dtype)


def _front_body(tiles_per_seq, sink_ref, x_ref, nw_ref, wmt_ref, wqgt_ref, wn_ref, bd_ref, qcol_ref, krow_ref,
                bias_ref, nwm_ref,
                gab_ref, hm_ref, ha_ref, kwin_ref, vwin_ref, st_ref, m_ref,
                q_qvot, q_gt, q_qat, q_km, q_kva, p_qvot, p_gt, p_qat, p_km, p_kva, st_s, m_s, kvp_s):
    k = pl.program_id(0)

    @pl.when(k == 0)
    def _():
        for r in (p_qvot, p_gt, p_qat, p_km, p_kva, st_s, m_s, kvp_s):
            r[...] = jnp.zeros(r.shape, r.dtype)

    pieces = _proj_chunks(x_ref, nw_ref, wmt_ref, wqgt_ref, wn_ref, bd_ref, qcol_ref, krow_ref,
                          q_qvot, q_gt, q_qat, q_km, q_kva, gab_ref)

    seq_start = lax.rem(k - 1 + tiles_per_seq, tiles_per_seq) == 0
    n_blk = p_qat.shape[0]
    same, causal = _chunk_masks(GROUP.bit_length() - 1)
    st = jnp.where(seq_start, 0.0, st_s[...])
    m_col = jnp.where(seq_start, 0.0, m_s[0:M_HEADS, 0:1])

    qvots = [lambda r0, r1, g=g: p_qvot[g, r0:r1, :] for g in range(n_blk)]
    scores = [_group_scores(qvots[g], p_km[g * GROUP:(g + 1) * GROUP, :] * (M_DK ** -0.5))
              for g in range(n_blk)]
    inter = [_dot(st.astype(BF16), scores[0][1])]
    gates = _groups_gates([p_gt[g] + bias_ref[...] for g in range(n_blk)], same, causal)
    for piece in pieces[0:1]:
        piece()
    grp = []
    for g in range(n_blk):
        b4, a4, at_mats, run4, chunk4 = gates[g]
        weights = _group_weights(jnp.broadcast_to(m_col, (M_HEADS, GROUP)), b4, a4, run4, chunk4)
        m_col = weights[4][:, GROUP - 1:GROUP]
        grp.append((qvots[g], at_mats, weights) + scores[g])
    has_prev = jnp.logical_not(seq_start)
    qb_lo, qb_hi = list(range(0, n_blk // 2)), list(range(n_blk // 2, n_blk))
    units_lo = _swa_scores(qb_lo, has_prev, sink_ref, p_qat, p_kva, kvp_s)
    for piece in pieces[1:3]:
        piece()

    vals = []

    def group_values(g):
        qvot, at_mats, (big_m, _, _, w_last, _, _), k_stack, _, sc_t = grp[g]
        vta, intra = _group_values(qvot, sc_t, at_mats, big_m, causal)
        vals.append((intra, _dot(_weighted_values(vta, w_last), k_stack)))

    for g in range(0, n_blk // 2):
        group_values(g)
    _swa_finish(qb_lo[:1], units_lo[:A_KV_HEADS], ha_ref)
    for piece in pieces[3:4]:
        piece()
    for g in range(n_blk // 2, n_blk):
        group_values(g)
    _swa_finish(qb_lo[1:], units_lo[A_KV_HEADS:], ha_ref)
    units_hi = _swa_scores(qb_hi, has_prev, sink_ref, p_qat, p_kva, kvp_s)
    for piece in pieces[4:6]:
        piece()

    for g in range(n_blk):
        st = _decay_row(grp[g][2][2], 0) * st + vals[g][1]
        if g + 1 < n_blk:
            inter.append(_dot(st.astype(BF16), grp[g + 1][4]))
        if g == n_blk // 2 - 1:
            _swa_finish(qb_hi[:1], units_hi[:A_KV_HEADS], ha_ref)
            for piece in pieces[6:7]:
                piece()
    _swa_finish(qb_hi[1:], units_hi[A_KV_HEADS:], ha_ref)
    for piece in pieces[7:]:
        piece()

    for g in range(n_blk):
        qvot, _, (_, w_inter, _, _, _, e_neg_m) = grp[g][0:3]
        hm_ref[g * GROUP:(g + 1) * GROUP, :] = _group_out(qvot, inter[g], vals[g][0], w_inter, e_neg_m,
                                                          nwm_ref).astype(hm_ref.dtype)
    st_s[...] = st
    st_ref[0] = st
    m_rows = jnp.broadcast_to(m_col, (M_HEADS, LANES))
    m_rows = jnp.concatenate([m_rows, m_rows], axis=0)
    m_s[...] = m_rows
    m_ref[0] = m_rows
    tm = p_kva.shape[0]
    kwin_ref[0] = p_kva[tm - WINDOW:, 0:A_KV_W]
    vwin_ref[0] = p_kva[tm - WINDOW:, A_KV_W:]
    kvp_s[...] = p_kva[tm - WINDOW:, :]


def _cast_rows(rows, n_steps):
    rb = BF16_ROWS
    while rows % rb or rows // rb > n_steps:
        rb += BF16_ROWS
    return rb


def _front(x2d, nb, sinks, nw, w_all, bd, qcol, krow, bias_col, nwm_col, cast_ws, tm):
    n = x2d.shape[0]
    n_tiles = n // tm
    tps = n_tiles // nb
    n_blk = tm // LANES
    w_qvo = M_QK_W + 2 * M_V_W
    cur = lambda w: pl.BlockSpec((tm, w), lambda k: (jnp.minimum(k, n_tiles - 1), 0))
    prev = lambda w: pl.BlockSpec((tm, w), lambda k: (jnp.maximum(k - 1, 0), 0))
    per_seq = lambda r, w: pl.BlockSpec((1, r, w), lambda k: (jnp.maximum(k - 1, 0) // tps, 0, 0))

    def cast_spec(a):
        rb = _cast_rows(a.shape[0], n_tiles)
        return pl.BlockSpec((rb, a.shape[1]), lambda k: (jnp.minimum(k, a.shape[0] // rb - 1), 0))

    cast_in_specs = [cast_spec(a) for a in cast_ws]
    cast_out_specs = [cast_spec(a) for a in cast_ws]
    proj_scratch = [pltpu.VMEM((n_blk, w_qvo, LANES), BF16), pltpu.VMEM((n_blk, 2 * M_HEADS, LANES), F32),
                    pltpu.VMEM((n_blk, A_Q_W, LANES), BF16), pltpu.VMEM((tm, M_QK_W), BF16),
                    pltpu.VMEM((tm, 2 * A_KV_W), F32)]
    return pl.pallas_call(
        functools.partial(_front_kernel, tps, len(cast_ws)),
        grid=(n_tiles + 1,),
        in_specs=[pl.BlockSpec(memory_space=pltpu.SMEM), cur(D_MODEL)]
                 + [_const_spec(a.shape) for a in (nw, w_all, bd, qcol, krow, bias_col, nwm_col)]
                 + cast_in_specs,
        out_specs=[cur(2 * D_MODEL), prev(M_V_W), prev(A_Q_W), per_seq(WINDOW, A_KV_W), per_seq(WINDOW, A_KV_W),
                   per_seq(S_ROWS, M_QK_W), per_seq(SUBLANES, LANES)] + cast_out_specs,
        out_shape=[jax.ShapeDtypeStruct((n, 2 * D_MODEL), BF16),
                   jax.ShapeDtypeStruct((n, M_V_W), BF16),
                   jax.ShapeDtypeStruct((n, A_Q_W), BF16),
                   jax.ShapeDtypeStruct((nb, WINDOW, A_KV_W), F32),
                   jax.ShapeDtypeStruct((nb, WINDOW, A_KV_W), F32),
                   jax.ShapeDtypeStruct((nb, S_ROWS, M_QK_W), F32),
                   jax.ShapeDtypeStruct((nb, SUBLANES, LANES), F32)]
                  + [jax.ShapeDtypeStruct(a.shape, BF16) for a in cast_ws],
        scratch_shapes=proj_scratch + proj_scratch + [pltpu.VMEM((S_ROWS, M_QK_W), F32),
                                                       pltpu.VMEM((SUBLANES, LANES), F32),
                                                       pltpu.VMEM((WINDOW, 2 * A_KV_W), F32)],
        compiler_params=_params(("arbitrary",)),
        name="front",
    )(sinks, x2d, nw, w_all, bd, qcol, krow, bias_col, nwm_col, *cast_ws)


def _swa_sample_kernel(seq_len, sink_ref, qt_ref, kv_ref, ck_ref, cv_ref, h_ref, kwin_ref, vwin_ref):
    n_seq = GROUP // seq_len
    wb = ck_ref.shape[2]
    lane = lax.broadcasted_iota(jnp.int32, (1, LANES), 1)
    lo_half = lane < HALF
    q_rows = qt_ref[0].astype(F32).T
    kv_new = kv_ref[...]
    k_new = kv_new[:, 0:A_KV_W].reshape(n_seq, seq_len, A_KV_W)
    v_new = kv_new[:, A_KV_W:].reshape(n_seq, seq_len, A_KV_W)

    def to_kv_half(x, head):
        kvh = head // A_GROUPS
        if head % 2 != kvh:
            x = pltpu.roll(x, HALF, axis=1)
        return jnp.where(lo_half if kvh == 0 else ~lo_half, x, 0.0)

    lhs = jnp.concatenate(
        [to_kv_half(q_rows[:, (h // 2) * LANES:(h // 2 + 1) * LANES], h).reshape(n_seq, seq_len, LANES)
         for h in range(A_HEADS)], axis=1).astype(BF16)
    zpad = jnp.zeros((n_seq, BF16_ROWS - seq_len, A_KV_W), F32)
    k_nb = jnp.concatenate([k_new, zpad], axis=1).astype(BF16)
    v_nb = jnp.concatenate([v_new, zpad], axis=1).astype(BF16)
    s_c = jnp.einsum('sqf,sfk->sqk', lhs, ck_ref[...].astype(BF16), preferred_element_type=F32)
    s_n = jnp.einsum('sqf,skf->sqk', lhs, k_nb, preferred_element_type=F32)
    nrow = A_HEADS * seq_len
    ti = lax.broadcasted_iota(jnp.int32, (nrow, wb), 0) & (seq_len - 1)
    ki = lax.broadcasted_iota(jnp.int32, (nrow, wb), 1)
    mask_c = (ti + wb - ki) < WINDOW
    ti_n = lax.broadcasted_iota(jnp.int32, (nrow, BF16_ROWS), 0) & (seq_len - 1)
    ki_n = lax.broadcasted_iota(jnp.int32, (nrow, BF16_ROWS), 1)
    mask_n = ki_n <= ti_n
    row_head = lax.broadcasted_iota(jnp.int32, (nrow, 1), 0) >> (seq_len.bit_length() - 1)
    sk = jnp.zeros((nrow, 1), F32)
    for h in range(A_HEADS):
        sk = jnp.where(row_head == h, sink_ref[h], sk)
    s_c = jnp.where(mask_c, s_c, -jnp.inf)
    s_n = jnp.where(mask_n, s_n, -jnp.inf)
    mx = jnp.maximum(jnp.maximum(jnp.max(s_c, axis=-1, keepdims=True), jnp.max(s_n, axis=-1, keepdims=True)), sk)
    p_c = jnp.exp(s_c - mx)
    p_n = jnp.exp(s_n - mx)
    den = jnp.sum(p_c, axis=-1, keepdims=True) + jnp.sum(p_n, axis=-1, keepdims=True) + jnp.exp(sk - mx)
    o = (jnp.einsum('sqk,sfk->sqf', p_c.astype(BF16), cv_ref[...].astype(BF16), preferred_element_type=F32)
         + jnp.einsum('sqk,skf->sqf', p_n.astype(BF16), v_nb, preferred_element_type=F32)) / den

    def from_kv_half(head):
        x = o[:, head * seq_len:(head + 1) * seq_len, :].reshape(GROUP, LANES)
        return pltpu.roll(x, HALF, axis=1) if head % 2 != head // A_GROUPS else x

    for c in range(A_HEADS // 2):
        h_ref[:, c * LANES:(c + 1) * LANES] = jnp.where(lo_half, from_kv_half(2 * c),
                                                        from_kv_half(2 * c + 1)).astype(h_ref.dtype)

    kt_new, vt_new = kv_new[:, 0:A_KV_W].T, kv_new[:, A_KV_W:].T
    for s in range(n_seq):
        put = (wb - seq_len - s * seq_len) % LANES
        for new_t, c_ref, win_ref in ((kt_new, ck_ref, kwin_ref), (vt_new, cv_ref, vwin_ref)):
            win_ref[s] = jnp.where(lane >= wb - seq_len, pltpu.roll(new_t, put, axis=1) if put else new_t,
                                   pltpu.roll(c_ref[s], wb - seq_len, axis=1))


def _swa_sample(qat, kv_a, cache_k, cache_v, sinks, seq_len):
    ngrp = qat.shape[0]
    n_seq = GROUP // seq_len
    wb = cache_k.shape[2]
    assert wb == LANES, "window positions fill one lane-width"
    row = lambda w: pl.BlockSpec((GROUP, w), lambda i: (i, 0))
    cache = pl.BlockSpec((n_seq, A_KV_W, wb), lambda i: (i, 0, 0))
    return pl.pallas_call(
        functools.partial(_swa_sample_kernel, seq_len),
        grid=(ngrp,),
        in_specs=[pl.BlockSpec(memory_space=pltpu.SMEM), pl.BlockSpec((1, A_Q_W, LANES), lambda i: (i, 0, 0)),
                  row(2 * A_KV_W), cache, cache],
        out_specs=[row(A_Q_W), cache, cache],
        out_shape=[jax.ShapeDtypeStruct((ngrp * GROUP, A_Q_W), BF16),
                   jax.ShapeDtypeStruct(cache_k.shape, F32),
                   jax.ShapeDtypeStruct(cache_v.shape, F32)],
        compiler_params=_params(("arbitrary",)),
        name="swa_sample",
    )(sinks, qat, kv_a, cache_k, cache_v)


def _merge_ffn_kernel(x_ref, hm_ref, ha_ref, gab_ref, wa_ref, wb_ref, wo_ref, nw_ref, wg_hbm, wu_hbm, wd_hbm,
                      y_ref, wg_ref, wu_ref, wd_ref, sems):
    cuts = _fetch_cuts(wg_ref.shape[1])
    pairs = [(src.at[:, c0:c1], dst.at[:, c0:c1]) for c0, c1 in zip(cuts[:-1], cuts[1:])
             for src, dst in ((wg_hbm, wg_ref), (wu_hbm, wu_ref))] + [(wd_hbm, wd_ref)]
    copies = [pltpu.make_async_copy(src, dst, sems.at[j]) for j, (src, dst) in enumerate(pairs)]
    first = pl.program_id(0) == 0
    for cond, fetch in ((first, copies), (jnp.logical_not(first), None)):
        pl.when(cond)(functools.partial(_merge_ffn_body, x_ref, hm_ref, ha_ref, gab_ref, wa_ref, wb_ref, wo_ref,
                                        nw_ref, wg_ref, wu_ref, wd_ref, y_ref, fetch))


def _fetch_cuts(cols):
    units = cols // MXU_WIDTH
    assert units * MXU_WIDTH == cols
    return [units * j // FETCH_CHUNKS * MXU_WIDTH for j in range(FETCH_CHUNKS + 1)]


def _merge_ffn_body(x_ref, hm_ref, ha_ref, gab_ref, wa_ref, wb_ref, wo_ref, nw_ref, wg_ref, wu_ref, wd_ref, y_ref,
                    fetch):
    if fetch:
        for c in fetch:
            c.start()
    tm = x_ref.shape[0]
    halves = [slice(0, tm // 2), slice(tm // 2, tm)]
    mix = []
    for rs in halves:
        ga = jax.nn.sigmoid(gab_ref[rs, 0:D_MODEL].astype(F32))
        gb = jax.nn.sigmoid(gab_ref[rs, D_MODEL:].astype(F32))
        mix.append(ga * _dot(hm_ref[rs, :], wa_ref[...]) + gb * _dot(ha_ref[rs, :], wb_ref[...]))
    x1 = [x_ref[rs, :] + _dot(m.astype(BF16), wo_ref[...]) for rs, m in zip(halves, mix)]
    hf = [_rms_rows(v, nw_ref[...]).astype(BF16) for v in x1]
    if fetch:
        cuts = _fetch_cuts(wg_ref.shape[1])
        parts = []
        for j, (c0, c1) in enumerate(zip(cuts[:-1], cuts[1:])):
            fetch[2 * j].wait()
            fetch[2 * j + 1].wait()
            parts.append([(jax.nn.silu(_dot(h, wg_ref[:, c0:c1])) * _dot(h, wu_ref[:, c0:c1])).astype(BF16)
                          for h in hf])
        act = [jnp.concatenate(half_parts, axis=1) for half_parts in zip(*parts)]
        fetch[-1].wait()
    else:
        act = [(jax.nn.silu(_dot(h, wg_ref[...])) * _dot(h, wu_ref[...])).astype(BF16) for h in hf]
    for rs, v, a in zip(halves, x1, act):
        y_ref[rs, :] = v + _dot(a, wd_ref[...])


def _merge_ffn(x2d, h_m, h_a, g_ab, wa, wb, wo, nw, wg, wu, wd, tm):
    n = x2d.shape[0]
    row = lambda w: pl.BlockSpec((tm, w), lambda i: (i, 0))
    fetched = (wg, wu, wd)
    return pl.pallas_call(
        _merge_ffn_kernel,
        grid=(n // tm,),
        in_specs=[row(D_MODEL), row(M_V_W), row(A_Q_W), row(2 * D_MODEL)]
                 + [_const_spec(w.shape) for w in (wa, wb, wo, nw)]
                 + [pl.BlockSpec(memory_space=pl.ANY) for _ in fetched],
        out_specs=row(D_MODEL),
        out_shape=jax.ShapeDtypeStruct((n, D_MODEL), F32),
        scratch_shapes=[pltpu.VMEM(w.shape, w.dtype) for w in fetched]
                       + [pltpu.SemaphoreType.DMA((2 * FETCH_CHUNKS + 1,))],
        compiler_params=_params(("arbitrary",)),
        name="merge_ffn",
    )(x2d, h_m, h_a, g_ab, wa, wb, wo, nw, wg, wu, wd)


def kernel(x_prompt, x_sample, state_mlstm_C, state_mlstm_n, state_mlstm_m, cache_swa_k, cache_swa_v,
           norm_mix_w, w_in, mlstm_i_bias, mlstm_f_bias, mlstm_norm_w, q_norm_w, k_norm_w, attn_sinks,
           w_branch_a, w_branch_b, w_out, norm_ffn_w, w_gate, w_up, w_down):
    depth = w_in.shape[0]
    assert depth == 1, "single trunk layer"
    l = 0
    bp, tp = x_prompt.shape[0], x_prompt.shape[1]
    bs, ts = x_sample.shape[0], x_sample.shape[1]
    assert tp % TOKEN_TILE == 0 and (bs * ts) % GROUP == 0 and GROUP % ts == 0 and ts & (ts - 1) == 0
    assert ts <= SUBLANES, "sample chunk must fit one sublane tile"

    wt = jnp.transpose(w_in[l])
    c_km, c_vm = M_QK_W, 2 * M_QK_W
    c_g = 2 * M_QK_W + 2 * M_V_W
    c_qa = c_g + 2 * M_HEADS
    c_ka = c_qa + A_Q_W
    gate_pad = jnp.zeros((BF16_ROWS - 2 * M_HEADS, D_MODEL), F32)
    w_all = jnp.concatenate([wt[0:c_km], wt[c_vm:c_g],
                             wt[c_qa:c_ka], wt[c_g:c_qa], gate_pad,
                             wt[c_km:c_vm], wt[c_ka:]],
                            axis=0).astype(BF16)
    assert w_all.shape[0] == W_ROWS_T + W_ROWS_Q + M_QK_W + 2 * A_KV_W + 2 * D_MODEL
    head_of = jnp.arange(A_KV_W) // A_HEAD_DIM
    bd = (head_of[:, None] == head_of[None, :]).astype(BF16)
    qcol = (jnp.tile(q_norm_w[l], A_HEADS) * (A_HEAD_DIM ** -0.5)).reshape(A_Q_W, 1)
    krow = jnp.tile(k_norm_w[l], A_KV_HEADS).reshape(1, A_KV_W)
    nw_mix = norm_mix_w[l].reshape(1, D_MODEL)
    nw_ffn = norm_ffn_w[l].reshape(1, D_MODEL)
    bias_col = jnp.concatenate([mlstm_i_bias[l], mlstm_f_bias[l]]).reshape(2 * M_HEADS, 1)
    nw_col = mlstm_norm_w[l].reshape(M_V_W, 1)
    sinks = attn_sinks[l]
    proj = lambda x2d, tm: _proj(x2d, nw_mix, w_all, bd, qcol, krow, tm)

    xp = x_prompt.reshape(bp * tp, D_MODEL)
    later_ws = (w_branch_a[l], w_branch_b[l], w_out[l], w_gate[l], w_up[l], w_down[l])
    g_ab, h_m, h_a, kwin_p, vwin_p, st_p, m_p, wa, wb, wo, wg, wu, wd = _front(
        xp, bp, sinks, nw_mix, w_all, bd, qcol, krow, bias_col, nw_col, later_ws, TOKEN_TILE)
    merge = lambda x2d, h_m, h_a, g_ab, tm: _merge_ffn(x2d, h_m, h_a, g_ab, wa, wb, wo, nw_ffn, wg, wu, wd, tm)
    yp = merge(xp, h_m, h_a, g_ab, TOKEN_TILE).reshape(bp, tp, D_MODEL)
    c_p = jnp.swapaxes(st_p[:, :M_DV, :], 1, 2).reshape(bp, M_HEADS, M_DK, M_DV)
    n_p = st_p[:, M_DV, :].reshape(bp, M_HEADS, M_DK)
    m_pr = m_p[:, :M_HEADS, 0]

    ns = bs * ts
    xs = x_sample.reshape(ns, D_MODEL)
    tms = TOKEN_TILE if ns % TOKEN_TILE == 0 else GROUP
    qvot, gt, qat, k_m, kv_a, g_ab = proj(xs, tms)
    ngrp = ns // GROUP
    m_lanes = jnp.repeat(state_mlstm_m[l], ts, axis=0).reshape(ngrp, GROUP, M_HEADS)
    mrow = jnp.pad(jnp.swapaxes(m_lanes, 1, 2), ((0, 0), (0, SUBLANES - M_HEADS), (0, 0)))
    h_m, c_s, n_s, mt_s = _mlstm_sample(qvot, gt, k_m, mrow, bias_col, nw_col,
                                        state_mlstm_C[l].reshape(bs, M_QK_W, M_DV),
                                        state_mlstm_n[l].reshape(bs, M_QK_W), ts)
    wbuf = cache_swa_k.shape[2]
    to_fm = lambda a: jnp.transpose(a, (0, 2, 3, 1)).reshape(bs, A_KV_W, wbuf)
    from_fm = lambda a: jnp.transpose(a.reshape(bs, A_KV_HEADS, A_HEAD_DIM, wbuf), (0, 3, 1, 2))[None]
    h_a, kwin_s, vwin_s = _swa_sample(qat, kv_a, to_fm(cache_swa_k[l]), to_fm(cache_swa_v[l]), sinks, ts)
    ys = merge(xs, h_m, h_a, g_ab, tms).reshape(bs, ts, D_MODEL)
    m_s = jnp.swapaxes(mt_s[:, :M_HEADS, :], 1, 2).reshape(bs, ts, M_HEADS)[:, ts - 1, :]

    kv5 = lambda a: a.reshape(a.shape[0], a.shape[1], A_KV_HEADS, A_HEAD_DIM)[None]
    return (yp, ys,
            c_p[None], n_p[None], m_pr[None], kv5(kwin_p), kv5(vwin_p),
            c_s.reshape(bs, M_HEADS, M_DK, M_DV)[None], n_s.reshape(bs, M_HEADS, M_DK)[None], m_s[None],
            from_fm(kwin_s), from_fm(vwin_s))
```

```python
import functools

import jax
import jax.numpy as jnp
from jax import lax
from jax.experimental import pallas as pl
from jax.experimental.pallas import tpu as pltpu

F32 = jnp.float32
BF16 = jnp.bfloat16

D_MODEL = 1024
M_HEADS = 4
M_DK = 64
M_DV = 128
M_QK_W = M_HEADS * M_DK
M_V_W = M_HEADS * M_DV
A_HEADS = 8
A_KV_HEADS = 2
A_HEAD_DIM = 64
A_GROUPS = A_HEADS // A_KV_HEADS
A_Q_W = A_HEADS * A_HEAD_DIM
A_KV_W = A_KV_HEADS * A_HEAD_DIM
WINDOW = 128
EPS = 1e-6

LANES = 128
SUBLANES = 8
BF16_ROWS = 16
MXU_WIDTH = 256
FETCH_CHUNKS = 4
GROUP = 128
S_ROWS = M_DV + BF16_ROWS
TOKEN_TILE = 512
VMEM_LIMIT = 56 * 1024 * 1024
DK_SHIFT = M_DK.bit_length() - 1
WINDOW_SHIFT = WINDOW.bit_length() - 1
HALF = LANES // 2
assert A_HEAD_DIM == HALF and A_KV_W == LANES, "attention head pairs share one lane-width"

NT_DIMS = (((1,), (1,)), ((), ()))


def _dot(a, b):
    return jnp.dot(a, b, preferred_element_type=F32)


def _dot_nt(a, b):
    return lax.dot_general(a, b, NT_DIMS, preferred_element_type=F32)


def _const_spec(shape):
    nd = len(shape)
    return pl.BlockSpec(shape, lambda *_: (0,) * nd, pipeline_mode=pl.Buffered(1))


def _params(sem):
    return pltpu.CompilerParams(dimension_semantics=sem, vmem_limit_bytes=VMEM_LIMIT)


def _rms_rows(x, nw):
    ms = jnp.mean(x * x, axis=-1, keepdims=True)
    return (x * lax.rsqrt(ms + EPS)) * nw


GAB_CHUNK = 512


def _proj_chunks(x_ref, nw_ref, wmt_ref, wqgt_ref, wn_ref, bd_ref, qcol_ref, krow_ref,
                 qvot_ref, gt_ref, qat_ref, km_ref, kva_ref, gab_ref):
    hn = _rms_rows(x_ref[...], nw_ref[...]).astype(BF16)
    n_blk = qat_ref.shape[0]

    def put(ref, rows, val):
        for c in range(n_blk):
            ref[c, rows, :] = val[:, c * LANES:(c + 1) * LANES].astype(ref.dtype)

    def mlstm_qvo():
        put(qvot_ref, slice(None), _dot_nt(wmt_ref[...], hn))

    def attn_q_and_gates():
        qt = _dot_nt(wqgt_ref[...], hn)
        put(gt_ref, slice(None), qt[A_Q_W:A_Q_W + 2 * M_HEADS])
        for h in range(A_HEADS):
            hs = slice(h * A_HEAD_DIM, (h + 1) * A_HEAD_DIM)
            blk = qt[hs]
            ssq_q = jnp.sum(blk * blk, axis=0, keepdims=True)
            put(qat_ref, hs, (blk * lax.rsqrt(ssq_q * (1.0 / A_HEAD_DIM) + EPS)) * qcol_ref[hs])

    def mlstm_k():
        km_ref[...] = _dot_nt(hn, wn_ref[0:M_QK_W, :]).astype(km_ref.dtype)

    def branch_gates(c0):
        def run():
            w0 = M_QK_W + 2 * A_KV_W + c0
            gab_ref[:, c0:c0 + GAB_CHUNK] = _dot_nt(hn, wn_ref[w0:w0 + GAB_CHUNK, :]).astype(gab_ref.dtype)
        return run

    def attn_kv():
        kv = _dot_nt(hn, wn_ref[M_QK_W:M_QK_W + 2 * A_KV_W, :])
        k = kv[:, 0:A_KV_W]
        ksq = k * k
        hi = ksq.astype(BF16)
        lo = (ksq - hi.astype(F32)).astype(BF16)
        ssq = _dot(hi, bd_ref[...]) + _dot(lo, bd_ref[...])
        kva_ref[:, 0:A_KV_W] = (k * lax.rsqrt(ssq * (1.0 / A_HEAD_DIM) + EPS)) * krow_ref[...]
        kva_ref[:, A_KV_W:] = kv[:, A_KV_W:]

    return ([mlstm_qvo, attn_q_and_gates, mlstm_k]
            + [branch_gates(c0) for c0 in range(0, gab_ref.shape[1], GAB_CHUNK)] + [attn_kv])


W_ROWS_T = M_QK_W + 2 * M_V_W
W_ROWS_Q = A_Q_W + BF16_ROWS


def _weight_views(w_ref):
    return w_ref.at[0:W_ROWS_T], w_ref.at[W_ROWS_T:W_ROWS_T + W_ROWS_Q], w_ref.at[W_ROWS_T + W_ROWS_Q:]


def _proj_kernel(x_ref, nw_ref, w_ref, *rest):
    for piece in _proj_chunks(x_ref, nw_ref, *_weight_views(w_ref), *rest):
        piece()


def _proj(x2d, nw, w_all, bd, qcol, krow, tm):
    n = x2d.shape[0]
    row = lambda w: pl.BlockSpec((tm, w), lambda i: (i, 0))
    n_blk = tm // LANES
    slab = lambda r: pl.BlockSpec((n_blk, r, LANES), lambda i: (i, 0, 0))
    w_qvo = M_QK_W + 2 * M_V_W
    return pl.pallas_call(
        _proj_kernel,
        grid=(n // tm,),
        in_specs=[row(D_MODEL)] + [_const_spec(a.shape) for a in (nw, w_all, bd, qcol, krow)],
        out_specs=[slab(w_qvo), slab(2 * M_HEADS), slab(A_Q_W), row(M_QK_W), row(2 * A_KV_W), row(2 * D_MODEL)],
        out_shape=[jax.ShapeDtypeStruct((n // LANES, w_qvo, LANES), BF16),
                   jax.ShapeDtypeStruct((n // LANES, 2 * M_HEADS, LANES), F32),
                   jax.ShapeDtypeStruct((n // LANES, A_Q_W, LANES), BF16),
                   jax.ShapeDtypeStruct((n, M_QK_W), BF16),
                   jax.ShapeDtypeStruct((n, 2 * A_KV_W), F32),
                   jax.ShapeDtypeStruct((n, 2 * D_MODEL), BF16)],
        compiler_params=_params(("arbitrary",)),
        name="proj",
    )(x2d, nw, w_all, bd, qcol, krow)


def _split3_rows(x):
    hi = x.astype(BF16).astype(F32)
    r1 = x - hi
    mid = r1.astype(BF16).astype(F32)
    lo = r1 - mid
    return jnp.concatenate([hi, mid, lo], axis=0).astype(BF16)


def _log_sigmoid(x):
    return jnp.minimum(x, 0.0) - jnp.log1p(jnp.exp(-jnp.abs(x)))


def _chunk_masks(chunk_shift):
    s = lax.broadcasted_iota(jnp.int32, (GROUP, GROUP), 0)
    t = lax.broadcasted_iota(jnp.int32, (GROUP, GROUP), 1)
    same = (s >> chunk_shift) == (t >> chunk_shift)
    return same, same & (s <= t)


def _groups_gates(gts, same, causal):
    n = len(gts)
    assert 2 * M_HEADS == SUBLANES and n * SUBLANES <= GROUP
    cm_bf = jnp.where(causal, 1.0, 0.0).astype(BF16)
    lf = _log_sigmoid(jnp.concatenate(gts, axis=0))
    nr = n * SUBLANES
    bt3 = _dot(_split3_rows(lf), cm_bf)
    bt = (bt3[0:nr] + bt3[nr:2 * nr]) + bt3[2 * nr:3 * nr]
    b4 = [bt[g * SUBLANES + M_HEADS:(g + 1) * SUBLANES] for g in range(n)]
    a4 = [gts[g][0:M_HEADS] - b4[g] for g in range(n)]
    a_rows = [a for g in range(n) for a in (a4[g], a4[g])]
    if nr < GROUP:
        a_rows.append(jnp.zeros((GROUP - nr, GROUP), F32))
    a_cols = jnp.concatenate(a_rows, axis=0).T
    out = []
    for g in range(n):
        at_mats, run_rows, chunk_rows = [], [], []
        for h in range(M_HEADS):
            c = g * SUBLANES + h
            at = jnp.broadcast_to(a_cols[:, c:c + 1], (GROUP, GROUP))
            at_mats.append(at)
            run_rows.append(jnp.max(jnp.where(causal, at, -jnp.inf), axis=0, keepdims=True))
            chunk_rows.append(jnp.max(jnp.where(same, at, -jnp.inf), axis=0, keepdims=True))
        out.append((b4[g], a4[g], at_mats, jnp.concatenate(run_rows, axis=0), jnp.concatenate(chunk_rows, axis=0)))
    return out


def _group_weights(m_prev, b4, a4, run4, chunk4):
    big_m = jnp.maximum(m_prev, run4)
    m_last = jnp.maximum(m_prev, chunk4)
    w_inter = jnp.exp(m_prev - big_m)
    g_vec = jnp.exp(m_prev - m_last)
    w_last = jnp.exp(a4 - m_last)
    m_t = b4 + big_m
    return big_m, w_inter, g_vec, w_last, m_t, jnp.exp(-m_t)


def _group_scores(qvot, ks):
    lane_head = lax.broadcasted_iota(jnp.int32, (1, M_QK_W), 1) >> DK_SHIFT
    row_head = lax.broadcasted_iota(jnp.int32, (M_QK_W, 1), 0) >> DK_SHIFT
    qt = qvot(0, M_QK_W)
    k_stack = jnp.concatenate([jnp.where(lane_head == h, ks, jnp.zeros_like(ks)) for h in range(M_HEADS)], axis=0)
    qw = jnp.concatenate([jnp.where(row_head == h, qt, jnp.zeros_like(qt)) for h in range(M_HEADS)], axis=1)
    zero_blk = jnp.zeros((M_DK, GROUP), BF16)
    sc_t = []
    for h in range(0, M_HEADS, 2):
        q_pair = jnp.concatenate(
            [jnp.concatenate([qvot(h * M_DK, (h + 1) * M_DK), zero_blk], axis=1),
             jnp.concatenate([zero_blk, qvot((h + 1) * M_DK, (h + 2) * M_DK)], axis=1)], axis=0)
        sc = _dot(ks[:, h * M_DK:(h + 2) * M_DK], q_pair)
        sc_t += [sc[:, 0:GROUP], sc[:, GROUP:]]
    return k_stack, qw, sc_t


def _group_values(qvot, sc_t, at_mats, big_m, causal):
    ones_rows = jnp.where(lax.broadcasted_iota(jnp.int32, (BF16_ROWS, GROUP), 0) == 0, 1.0, 0.0).astype(BF16)
    zero_blk = jnp.zeros((GROUP, GROUP), BF16)
    vta, s_t = [], []
    for h in range(M_HEADS):
        w_t = jnp.where(causal, jnp.exp(at_mats[h] - big_m[h:h + 1]), 0.0)
        s_t.append((sc_t[h] * w_t).astype(BF16))
        vta.append(jnp.concatenate([qvot(M_QK_W + h * M_DV, M_QK_W + (h + 1) * M_DV), ones_rows], axis=0))
    intra = []
    for h in range(0, M_HEADS, 2):
        pair = jnp.concatenate([jnp.concatenate([s_t[h], zero_blk], axis=1),
                                jnp.concatenate([zero_blk, s_t[h + 1]], axis=1)], axis=0)
        intra.append(_dot(jnp.concatenate(vta[h:h + 2], axis=1), pair))
    return vta, jnp.concatenate(intra, axis=1)


def _weighted_values(vta, w_rows):
    return jnp.concatenate([(vta[h].astype(F32) * w_rows[h:h + 1]).astype(BF16) for h in range(M_HEADS)], axis=1)


def _lanes_x(rows):
    return jnp.concatenate([rows[h:h + 1] for h in range(M_HEADS)], axis=1)


def _group_out(qvot, inter, intra, w_inter, e_neg_m, nw_ref):
    outs = []
    out_all = inter * _lanes_x(w_inter) + intra
    for h in range(M_HEADS):
        out_t = out_all[:, h * GROUP:(h + 1) * GROUP]
        hh = out_t[0:M_DV] / jnp.maximum(jnp.abs(out_t[M_DV:M_DV + 1]), e_neg_m[h:h + 1])
        ms = jnp.mean(hh * hh, axis=0, keepdims=True)
        hn = (hh * lax.rsqrt(ms + EPS)) * nw_ref[h * M_DV:(h + 1) * M_DV]
        o_t = qvot(M_QK_W + M_V_W + h * M_DV, M_QK_W + M_V_W + (h + 1) * M_DV)
        outs.append(hn * jax.nn.sigmoid(o_t.astype(F32)))
    return jnp.concatenate(outs, axis=0).T


def _decay_row(g_vec, lane0):
    lane_head = lax.broadcasted_iota(jnp.int32, (1, M_QK_W), 1) >> DK_SHIFT
    g_row = jnp.zeros((1, M_QK_W), F32)
    for h in range(M_HEADS):
        g_row = jnp.where(lane_head == h, g_vec[h:h + 1, lane0:lane0 + 1], g_row)
    return g_row


def _mlstm_sample_kernel(seq_len, qvot_ref, gt_ref, k_ref, mrow_ref, bias_ref, nw_ref, c_ref, n_ref,
                         h_ref, c_out_ref, n_out_ref, mt_ref):
    n_seq = GROUP // seq_len
    shift = seq_len.bit_length() - 1
    same, causal = _chunk_masks(shift)
    lane_seq = lax.broadcasted_iota(jnp.int32, (1, GROUP), 1) >> shift
    lane_seq_x = jnp.concatenate([lane_seq] * M_HEADS, axis=1)
    row0 = lax.broadcasted_iota(jnp.int32, (BF16_ROWS, M_QK_W), 0) == 0
    qvot = lambda r0, r1: qvot_ref[0, r0:r1, :]
    ks = k_ref[...] * (M_DK ** -0.5)
    (b4, a4, at_mats, run4, chunk4), = _groups_gates([gt_ref[0] + bias_ref[...]], same, causal)
    big_m, w_inter, g_vec, w_last, m_t, e_neg_m = _group_weights(mrow_ref[0, 0:M_HEADS, :], b4, a4, run4, chunk4)
    mt_ref[0] = jnp.concatenate([m_t, m_t], axis=0)
    k_stack, qw, sc_t = _group_scores(qvot, ks)

    st_old = []
    for s in range(n_seq):
        n_rows = jnp.where(row0, jnp.broadcast_to(n_ref[s:s + 1, :], (BF16_ROWS, M_QK_W)), 0.0)
        st_old.append(jnp.concatenate([c_ref[s].T, n_rows], axis=0))
    inter_all = _dot(jnp.concatenate(st_old, axis=0).astype(BF16), qw)
    vta, intra = _group_values(qvot, sc_t, at_mats, big_m, causal)
    inter = inter_all[0:S_ROWS]
    for s in range(1, n_seq):
        inter = jnp.where(lane_seq_x == s, inter_all[s * S_ROWS:(s + 1) * S_ROWS], inter)
    tall = jnp.concatenate([_weighted_values(vta, jnp.where(lane_seq == s, w_last, 0.0)) for s in range(n_seq)],
                           axis=0)
    d_st = _dot(tall, k_stack)
    for s in range(n_seq):
        st_new = _decay_row(g_vec, s * seq_len) * st_old[s] + d_st[s * S_ROWS:(s + 1) * S_ROWS]
        c_out_ref[s] = st_new[0:M_DV].T
        n_out_ref[s:s + 1, :] = st_new[M_DV:M_DV + 1]
    h_ref[...] = _group_out(qvot, inter, intra, w_inter, e_neg_m, nw_ref).astype(h_ref.dtype)


def _mlstm_sample(qvot, gt, k_m, mrow, bias_col, nw_col, c, n, seq_len):
    ngrp = qvot.shape[0]
    n_seq = GROUP // seq_len
    full = lambda a: pl.BlockSpec(a.shape, lambda i: (0,) * a.ndim)
    slab = lambda a: pl.BlockSpec((1,) + a.shape[1:], lambda i: (i, 0, 0))
    row = lambda w: pl.BlockSpec((GROUP, w), lambda i: (i, 0))
    c_spec = pl.BlockSpec((n_seq,) + c.shape[1:], lambda i: (i, 0, 0))
    n_spec = pl.BlockSpec((n_seq, n.shape[1]), lambda i: (i, 0))
    return pl.pallas_call(
        functools.partial(_mlstm_sample_kernel, seq_len),
        grid=(ngrp,),
        in_specs=[slab(qvot), slab(gt), row(M_QK_W), slab(mrow), full(bias_col), full(nw_col), c_spec, n_spec],
        out_specs=[row(M_V_W), c_spec, n_spec, slab(mrow)],
        out_shape=[jax.ShapeDtypeStruct((ngrp * GROUP, M_V_W), BF16),
                   jax.ShapeDtypeStruct(c.shape, F32),
                   jax.ShapeDtypeStruct(n.shape, F32),
                   jax.ShapeDtypeStruct(mrow.shape, F32)],
        compiler_params=_params(("arbitrary",)),
        name="mlstm_sample",
    )(qvot, gt, k_m, mrow, bias_col, nw_col, c, n)


def _swa_scores(qbs, has_prev, sink_ref, qt_ref, kv_ref, kvp_ref):
    hd = A_HEAD_DIM
    nq = A_GROUPS * WINDOW
    si = lax.broadcasted_iota(jnp.int32, (2 * WINDOW, nq), 0)
    qi = lax.broadcasted_iota(jnp.int32, (2 * WINDOW, nq), 1) & (WINDOW - 1)
    local = ((si < WINDOW) & (si > qi)) | ((si >= WINDOW) & (si - WINDOW <= qi))
    first = local & (has_prev | (si >= WINDOW))
    lane_grp = lax.broadcasted_iota(jnp.int32, (1, nq), 1) >> WINDOW_SHIFT
    kv_block = lambda i: kvp_ref[...] if i == 0 else kv_ref[(i - 1) * WINDOW:i * WINDOW, :]
    k_bf = lambda i: kv_block(i)[:, 0:A_KV_W].astype(BF16)
    vt_bf = lambda i: kv_block(i)[:, A_KV_W:].T.astype(BF16)
    zeros = jnp.zeros((hd, nq), BF16)
    sinks = []
    for kvh in range(A_KV_HEADS):
        sk = jnp.zeros((1, nq), F32)
        for g in range(A_GROUPS):
            sk = jnp.where(lane_grp == g, sink_ref[kvh * A_GROUPS + g], sk)
        sinks.append(sk)
    units = []
    for qb in qbs:
        kk = jnp.concatenate([k_bf(qb), k_bf(qb + 1)], axis=0)
        vt = jnp.concatenate([vt_bf(qb), vt_bf(qb + 1)], axis=1)
        mask = local if qb > 0 else first
        for kvh in range(A_KV_HEADS):
            q4t = jnp.concatenate(
                [qt_ref[qb, (kvh * A_GROUPS + g) * hd:(kvh * A_GROUPS + g + 1) * hd, :] for g in range(A_GROUPS)],
                axis=1)
            wq = jnp.concatenate([q4t, zeros] if kvh == 0 else [zeros, q4t], axis=0)
            units.append((jnp.where(mask, _dot(kk, wq), -jnp.inf), sinks[kvh], vt))
    return units


def _swa_finish(qbs, units, h_ref):
    hd = A_HEAD_DIM
    for i, qb in enumerate(qbs):
        pieces = []
        for kvh in range(A_KV_HEADS):
            s, sk, vt = units[i * A_KV_HEADS + kvh]
            mx = jnp.maximum(jnp.max(s, axis=0, keepdims=True), sk)
            p = jnp.exp(s - mx)
            den = jnp.sum(p, axis=0, keepdims=True) + jnp.exp(sk - mx)
            ot = _dot(vt[kvh * hd:(kvh + 1) * hd], p.astype(BF16)) / den
            pieces += [ot[:, g * WINDOW:(g + 1) * WINDOW] for g in range(A_GROUPS)]
        h_t = jnp.concatenate(pieces, axis=0)
        h_ref[qb * WINDOW:(qb + 1) * WINDOW, :] = h_t.T.astype(h_ref.dtype)


def _front_kernel(tiles_per_seq, n_cast, *refs):
    (sink_ref, x_ref, nw_ref, w_ref, bd_ref, qcol_ref, krow_ref, bias_ref, nwm_ref) = refs[:9]
    wmt_ref, wqgt_ref, wn_ref = _weight_views(w_ref)
    cast_in = refs[9:9 + n_cast]
    (gab_ref, hm_ref, ha_ref, kwin_ref, vwin_ref, st_ref, m_ref) = refs[9 + n_cast:16 + n_cast]
    cast_out = refs[16 + n_cast:16 + 2 * n_cast]
    set_a, set_b = refs[16 + 2 * n_cast:21 + 2 * n_cast], refs[21 + 2 * n_cast:26 + 2 * n_cast]
    st_s, m_s, kvp_s = refs[26 + 2 * n_cast:]
    odd = (pl.program_id(0) & 1) == 1
    for parity, q_set, p_set in ((jnp.logical_not(odd), set_a, set_b), (odd, set_b, set_a)):
        pl.when(parity)(functools.partial(
            _front_body, tiles_per_seq, sink_ref, x_ref, nw_ref, wmt_ref, wqgt_ref, wn_ref, bd_ref, qcol_ref,
            krow_ref, bias_ref, nwm_ref, gab_ref, hm_ref, ha_ref, kwin_ref, vwin_ref, st_ref, m_ref,
            *q_set, *p_set, st_s, m_s, kvp_s))
    for src, dst in zip(cast_in, cast_out):
        dst[...] = src[...].astype(dst.dtype)


def _front_body(tiles_per_seq, sink_ref, x_ref, nw_ref, wmt_ref, wqgt_ref, wn_ref, bd_ref, qcol_ref, krow_ref,
                bias_ref, nwm_ref,
                gab_ref, hm_ref, ha_ref, kwin_ref, vwin_ref, st_ref, m_ref,
                q_qvot, q_gt, q_qat, q_km, q_kva, p_qvot, p_gt, p_qat, p_km, p_kva, st_s, m_s, kvp_s):
    k = pl.program_id(0)

    @pl.when(k == 0)
    def _():
        for r in (p_qvot, p_gt, p_qat, p_km, p_kva, st_s, m_s, kvp_s):
            r[...] = jnp.zeros(r.shape, r.dtype)

    pieces = _proj_chunks(x_ref, nw_ref, wmt_ref, wqgt_ref, wn_ref, bd_ref, qcol_ref, krow_ref,
                          q_qvot, q_gt, q_qat, q_km, q_kva, gab_ref)

    seq_start = lax.rem(k - 1 + tiles_per_seq, tiles_per_seq) == 0
    n_blk = p_qat.shape[0]
    same, causal = _chunk_masks(GROUP.bit_length() - 1)
    st = jnp.where(seq_start, 0.0, st_s[...])
    m_col = jnp.where(seq_start, 0.0, m_s[0:M_HEADS, 0:1])

    qvots = [lambda r0, r1, g=g: p_qvot[g, r0:r1, :] for g in range(n_blk)]
    scores = [_group_scores(qvots[g], p_km[g * GROUP:(g + 1) * GROUP, :] * (M_DK ** -0.5))
              for g in range(n_blk)]
    inter = [_dot(st.astype(BF16), scores[0][1])]
    gates = _groups_gates([p_gt[g] + bias_ref[...] for g in range(n_blk)], same, causal)
    for piece in pieces[0:1]:
        piece()
    grp = []
    for g in range(n_blk):
        b4, a4, at_mats, run4, chunk4 = gates[g]
        weights = _group_weights(jnp.broadcast_to(m_col, (M_HEADS, GROUP)), b4, a4, run4, chunk4)
        m_col = weights[4][:, GROUP - 1:GROUP]
        grp.append((qvots[g], at_mats, weights) + scores[g])
    has_prev = jnp.logical_not(seq_start)
    qb_lo, qb_hi = list(range(0, n_blk // 2)), list(range(n_blk // 2, n_blk))
    units_lo = _swa_scores(qb_lo, has_prev, sink_ref, p_qat, p_kva, kvp_s)
    for piece in pieces[1:3]:
        piece()

    vals = []

    def group_values(g):
        qvot, at_mats, (big_m, _, _, w_last, _, _), k_stack, _, sc_t = grp[g]
        vta, intra = _group_values(qvot, sc_t, at_mats, big_m, causal)
        vals.append((intra, _dot(_weighted_values(vta, w_last), k_stack)))

    for g in range(0, n_blk // 2):
        group_values(g)
    _swa_finish(qb_lo[:1], units_lo[:A_KV_HEADS], ha_ref)
    for piece in pieces[3:4]:
        piece()
    for g in range(n_blk // 2, n_blk):
        group_values(g)
    _swa_finish(qb_lo[1:], units_lo[A_KV_HEADS:], ha_ref)
    units_hi = _swa_scores(qb_hi, has_prev, sink_ref, p_qat, p_kva, kvp_s)
    for piece in pieces[4:6]:
        piece()

    for g in range(n_blk):
        st = _decay_row(grp[g][2][2], 0) * st + vals[g][1]
        if g + 1 < n_blk:
            inter.append(_dot(st.astype(BF16), grp[g + 1][4]))
        if g == n_blk // 2 - 1:
            _swa_finish(qb_hi[:1], units_hi[:A_KV_HEADS], ha_ref)
            for piece in pieces[6:7]:
                piece()
    _swa_finish(qb_hi[1:], units_hi[A_KV_HEADS:], ha_ref)
    for piece in pieces[7:]:
        piece()

    for g in range(n_blk):
        qvot, _, (_, w_inter, _, _, _, e_neg_m) = grp[g][0:3]
        hm_ref[g * GROUP:(g + 1) * GROUP, :] = _group_out(qvot, inter[g], vals[g][0], w_inter, e_neg_m,
                                                          nwm_ref).astype(hm_ref.dtype)
    st_s[...] = st
    st_ref[0] = st
    m_rows = jnp.broadcast_to(m_col, (M_HEADS, LANES))
    m_rows = jnp.concatenate([m_rows, m_rows], axis=0)
    m_s[...] = m_rows
    m_ref[0] = m_rows
    tm = p_kva.shape[0]
    kwin_ref[0] = p_kva[tm - WINDOW:, 0:A_KV_W]
    vwin_ref[0] = p_kva[tm - WINDOW:, A_KV_W:]
    kvp_s[...] = p_kva[tm - WINDOW:, :]


def _cast_rows(rows, n_steps):
    rb = BF16_ROWS
    while rows % rb or rows // rb > n_steps:
        rb += BF16_ROWS
    return rb


def _front(x2d, nb, sinks, nw, w_all, bd, qcol, krow, bias_col, nwm_col, cast_ws, tm):
    n = x2d.shape[0]
    n_tiles = n // tm
    tps = n_tiles // nb
    n_blk = tm // LANES
    w_qvo = M_QK_W + 2 * M_V_W
    cur = lambda w: pl.BlockSpec((tm, w), lambda k: (jnp.minimum(k, n_tiles - 1), 0))
    prev = lambda w: pl.BlockSpec((tm, w), lambda k: (jnp.maximum(k - 1, 0), 0))
    per_seq = lambda r, w: pl.BlockSpec((1, r, w), lambda k: (jnp.maximum(k - 1, 0) // tps, 0, 0))

    def cast_spec(a):
        rb = _cast_rows(a.shape[0], n_tiles)
        return pl.BlockSpec((rb, a.shape[1]), lambda k: (jnp.minimum(k, a.shape[0] // rb - 1), 0))

    cast_in_specs = [cast_spec(a) for a in cast_ws]
    cast_out_specs = [cast_spec(a) for a in cast_ws]
    proj_scratch = [pltpu.VMEM((n_blk, w_qvo, LANES), BF16), pltpu.VMEM((n_blk, 2 * M_HEADS, LANES), F32),
                    pltpu.VMEM((n_blk, A_Q_W, LANES), BF16), pltpu.VMEM((tm, M_QK_W), BF16),
                    pltpu.VMEM((tm, 2 * A_KV_W), F32)]
    return pl.pallas_call(
        functools.partial(_front_kernel, tps, len(cast_ws)),
        grid=(n_tiles + 1,),
        in_specs=[pl.BlockSpec(memory_space=pltpu.SMEM), cur(D_MODEL)]
                 + [_const_spec(a.shape) for a in (nw, w_all, bd, qcol, krow, bias_col, nwm_col)]
                 + cast_in_specs,
        out_specs=[cur(2 * D_MODEL), prev(M_V_W), prev(A_Q_W), per_seq(WINDOW, A_KV_W), per_seq(WINDOW, A_KV_W),
                   per_seq(S_ROWS, M_QK_W), per_seq(SUBLANES, LANES)] + cast_out_specs,
        out_shape=[jax.ShapeDtypeStruct((n, 2 * D_MODEL), BF16),
                   jax.ShapeDtypeStruct((n, M_V_W), BF16),
                   jax.ShapeDtypeStruct((n, A_Q_W), BF16),
                   jax.ShapeDtypeStruct((nb, WINDOW, A_KV_W), F32),
                   jax.ShapeDtypeStruct((nb, WINDOW, A_KV_W), F32),
                   jax.ShapeDtypeStruct((nb, S_ROWS, M_QK_W), F32),
                   jax.ShapeDtypeStruct((nb, SUBLANES, LANES), F32)]
                  + [jax.ShapeDtypeStruct(a.shape, BF16) for a in cast_ws],
        scratch_shapes=proj_scratch + proj_scratch + [pltpu.VMEM((S_ROWS, M_QK_W), F32),
                                                       pltpu.VMEM((SUBLANES, LANES), F32),
                                                       pltpu.VMEM((WINDOW, 2 * A_KV_W), F32)],
        compiler_params=_params(("arbitrary",)),
        name="front",
    )(sinks, x2d, nw, w_all, bd, qcol, krow, bias_col, nwm_col, *cast_ws)


def _swa_sample_kernel(seq_len, sink_ref, qt_ref, kv_ref, ck_ref, cv_ref, h_ref, kwin_ref, vwin_ref):
    n_seq = GROUP // seq_len
    wb = ck_ref.shape[2]
    lane = lax.broadcasted_iota(jnp.int32, (1, LANES), 1)
    lo_half = lane < HALF
    q_rows = qt_ref[0].astype(F32).T
    kv_new = kv_ref[...]
    k_new = kv_new[:, 0:A_KV_W].reshape(n_seq, seq_len, A_KV_W)
    v_new = kv_new[:, A_KV_W:].reshape(n_seq, seq_len, A_KV_W)

    def to_kv_half(x, head):
        kvh = head // A_GROUPS
        if head % 2 != kvh:
            x = pltpu.roll(x, HALF, axis=1)
        return jnp.where(lo_half if kvh == 0 else ~lo_half, x, 0.0)

    lhs = jnp.concatenate(
        [to_kv_half(q_rows[:, (h // 2) * LANES:(h // 2 + 1) * LANES], h).reshape(n_seq, seq_len, LANES)
         for h in range(A_HEADS)], axis=1).astype(BF16)
    zpad = jnp.zeros((n_seq, BF16_ROWS - seq_len, A_KV_W), F32)
    k_nb = jnp.concatenate([k_new, zpad], axis=1).astype(BF16)
    v_nb = jnp.concatenate([v_new, zpad], axis=1).astype(BF16)
    s_c = jnp.einsum('sqf,sfk->sqk', lhs, ck_ref[...].astype(BF16), preferred_element_type=F32)
    s_n = jnp.einsum('sqf,skf->sqk', lhs, k_nb, preferred_element_type=F32)
    nrow = A_HEADS * seq_len
    ti = lax.broadcasted_iota(jnp.int32, (nrow, wb), 0) & (seq_len - 1)
    ki = lax.broadcasted_iota(jnp.int32, (nrow, wb), 1)
    mask_c = (ti + wb - ki) < WINDOW
    ti_n = lax.broadcasted_iota(jnp.int32, (nrow, BF16_ROWS), 0) & (seq_len - 1)
    ki_n = lax.broadcasted_iota(jnp.int32, (nrow, BF16_ROWS), 1)
    mask_n = ki_n <= ti_n
    row_head = lax.broadcasted_iota(jnp.int32, (nrow, 1), 0) >> (seq_len.bit_length() - 1)
    sk = jnp.zeros((nrow, 1), F32)
    for h in range(A_HEADS):
        sk = jnp.where(row_head == h, sink_ref[h], sk)
    s_c = jnp.where(mask_c, s_c, -jnp.inf)
    s_n = jnp.where(mask_n, s_n, -jnp.inf)
    mx = jnp.maximum(jnp.maximum(jnp.max(s_c, axis=-1, keepdims=True), jnp.max(s_n, axis=-1, keepdims=True)), sk)
    p_c = jnp.exp(s_c - mx)
    p_n = jnp.exp(s_n - mx)
    den = jnp.sum(p_c, axis=-1, keepdims=True) + jnp.sum(p_n, axis=-1, keepdims=True) + jnp.exp(sk - mx)
    o = (jnp.einsum('sqk,sfk->sqf', p_c.astype(BF16), cv_ref[...].astype(BF16), preferred_element_type=F32)
         + jnp.einsum('sqk,skf->sqf', p_n.astype(BF16), v_nb, preferred_element_type=F32)) / den

    def from_kv_half(head):
        x = o[:, head * seq_len:(head + 1) * seq_len, :].reshape(GROUP, LANES)
        return pltpu.roll(x, HALF, axis=1) if head % 2 != head // A_GROUPS else x

    for c in range(A_HEADS // 2):
        h_ref[:, c * LANES:(c + 1) * LANES] = jnp.where(lo_half, from_kv_half(2 * c),
                                                        from_kv_half(2 * c + 1)).astype(h_ref.dtype)

    kt_new, vt_new = kv_new[:, 0:A_KV_W].T, kv_new[:, A_KV_W:].T
    for s in range(n_seq):
        put = (wb - seq_len - s * seq_len) % LANES
        for new_t, c_ref, win_ref in ((kt_new, ck_ref, kwin_ref), (vt_new, cv_ref, vwin_ref)):
            win_ref[s] = jnp.where(lane >= wb - seq_len, pltpu.roll(new_t, put, axis=1) if put else new_t,
                                   pltpu.roll(c_ref[s], wb - seq_len, axis=1))


def _swa_sample(qat, kv_a, cache_k, cache_v, sinks, seq_len):
    ngrp = qat.shape[0]
    n_seq = GROUP // seq_len
    wb = cache_k.shape[2]
    assert wb == LANES, "window positions fill one lane-width"
    row = lambda w: pl.BlockSpec((GROUP, w), lambda i: (i, 0))
    cache = pl.BlockSpec((n_seq, A_KV_W, wb), lambda i: (i, 0, 0))
    return pl.pallas_call(
        functools.partial(_swa_sample_kernel, seq_len),
        grid=(ngrp,),
        in_specs=[pl.BlockSpec(memory_space=pltpu.SMEM), pl.BlockSpec((1, A_Q_W, LANES), lambda i: (i, 0, 0)),
                  row(2 * A_KV_W), cache, cache],
        out_specs=[row(A_Q_W), cache, cache],
        out_shape=[jax.ShapeDtypeStruct((ngrp * GROUP, A_Q_W), BF16),
                   jax.ShapeDtypeStruct(cache_k.shape, F32),
                   jax.ShapeDtypeStruct(cache_v.shape, F32)],
        compiler_params=_params(("arbitrary",)),
        name="swa_sample",
    )(sinks, qat, kv_a, cache_k, cache_v)


def _merge_ffn_kernel(x_ref, hm_ref, ha_ref, gab_ref, nw_ref, wa_hbm, wb_hbm, wo_hbm, wg_hbm, wu_hbm, wd_hbm,
                      y_ref, wa_ref, wb_ref, wo_ref, wg_ref, wu_ref, wd_ref, sems):
    cuts = _fetch_cuts(wg_ref.shape[1])
    pairs = [(wa_hbm, wa_ref), (wb_hbm, wb_ref), (wo_hbm, wo_ref)]
    pairs += [(src.at[:, c0:c1], dst.at[:, c0:c1]) for c0, c1 in zip(cuts[:-1], cuts[1:])
              for src, dst in ((wg_hbm, wg_ref), (wu_hbm, wu_ref))] + [(wd_hbm, wd_ref)]
    copies = [pltpu.make_async_copy(src, dst, sems.at[j]) for j, (src, dst) in enumerate(pairs)]
    first = pl.program_id(0) == 0
    for cond, fetch in ((first, copies), (jnp.logical_not(first), None)):
        pl.when(cond)(functools.partial(_merge_ffn_body, x_ref, hm_ref, ha_ref, gab_ref, wa_ref, wb_ref, wo_ref,
                                        nw_ref, wg_ref, wu_ref, wd_ref, y_ref, fetch))


def _fetch_cuts(cols):
    units = cols // MXU_WIDTH
    assert units * MXU_WIDTH == cols
    return [units * j // FETCH_CHUNKS * MXU_WIDTH for j in range(FETCH_CHUNKS + 1)]


def _merge_ffn_body(x_ref, hm_ref, ha_ref, gab_ref, wa_ref, wb_ref, wo_ref, nw_ref, wg_ref, wu_ref, wd_ref, y_ref,
                    fetch):
    if fetch:
        for c in fetch:
            c.start()
    tm = x_ref.shape[0]
    halves = [slice(0, tm // 2), slice(tm // 2, tm)]
    if fetch:
        fetch[0].wait()
        fetch[1].wait()
    mix = []
    for rs in halves:
        ga = jax.nn.sigmoid(gab_ref[rs, 0:D_MODEL].astype(F32))
        gb = jax.nn.sigmoid(gab_ref[rs, D_MODEL:].astype(F32))
        mix.append(ga * _dot(hm_ref[rs, :], wa_ref[...]) + gb * _dot(ha_ref[rs, :], wb_ref[...]))
    if fetch:
        fetch[2].wait()
    x1 = [x_ref[rs, :] + _dot(m.astype(BF16), wo_ref[...]) for rs, m in zip(halves, mix)]
    hf = [_rms_rows(v, nw_ref[...]).astype(BF16) for v in x1]
    if fetch:
        cuts = _fetch_cuts(wg_ref.shape[1])
        parts = []
        for j, (c0, c1) in enumerate(zip(cuts[:-1], cuts[1:])):
            fetch[3 + 2 * j].wait()
            fetch[4 + 2 * j].wait()
            parts.append([(jax.nn.silu(_dot(h, wg_ref[:, c0:c1])) * _dot(h, wu_ref[:, c0:c1])).astype(BF16)
                          for h in hf])
        act = [jnp.concatenate(half_parts, axis=1) for half_parts in zip(*parts)]
        fetch[-1].wait()
    else:
        act = [(jax.nn.silu(_dot(h, wg_ref[...])) * _dot(h, wu_ref[...])).astype(BF16) for h in hf]
    for rs, v, a in zip(halves, x1, act):
        y_ref[rs, :] = v + _dot(a, wd_ref[...])


def _merge_ffn(x2d, h_m, h_a, g_ab, wa, wb, wo, nw, wg, wu, wd, tm):
    n = x2d.shape[0]
    row = lambda w: pl.BlockSpec((tm, w), lambda i: (i, 0))
    fetched = (wa, wb, wo, wg, wu, wd)
    return pl.pallas_call(
        _merge_ffn_kernel,
        grid=(n // tm,),
        in_specs=[row(D_MODEL), row(M_V_W), row(A_Q_W), row(2 * D_MODEL), _const_spec(nw.shape)]
                 + [pl.BlockSpec(memory_space=pl.ANY) for _ in fetched],
        out_specs=row(D_MODEL),
        out_shape=jax.ShapeDtypeStruct((n, D_MODEL), F32),
        scratch_shapes=[pltpu.VMEM(w.shape, w.dtype) for w in fetched]
                       + [pltpu.SemaphoreType.DMA((2 * FETCH_CHUNKS + 4,))],
        compiler_params=_params(("arbitrary",)),
        name="merge_ffn",
    )(x2d, h_m, h_a, g_ab, nw, wa, wb, wo, wg, wu, wd)


def kernel(x_prompt, x_sample, state_mlstm_C, state_mlstm_n, state_mlstm_m, cache_swa_k, cache_swa_v,
           norm_mix_w, w_in, mlstm_i_bias, mlstm_f_bias, mlstm_norm_w, q_norm_w, k_norm_w, attn_sinks,
           w_branch_a, w_branch_b, w_out, norm_ffn_w, w_gate, w_up, w_down):
    depth = w_in.shape[0]
    assert depth == 1, "single trunk layer"
    l = 0
    bp, tp = x_prompt.shape[0], x_prompt.shape[1]
    bs, ts = x_sample.shape[0], x_sample.shape[1]
    assert tp % TOKEN_TILE == 0 and (bs * ts) % GROUP == 0 and GROUP % ts == 0 and ts & (ts - 1) == 0
    assert ts <= SUBLANES, "sample chunk must fit one sublane tile"

    wt = jnp.transpose(w_in[l])
    c_km, c_vm = M_QK_W, 2 * M_QK_W
    c_g = 2 * M_QK_W + 2 * M_V_W
    c_qa = c_g + 2 * M_HEADS
    c_ka = c_qa + A_Q_W
    gate_pad = jnp.zeros((BF16_ROWS - 2 * M_HEADS, D_MODEL), F32)
    w_all = jnp.concatenate([wt[0:c_km], wt[c_vm:c_g],
                             wt[c_qa:c_ka], wt[c_g:c_qa], gate_pad,
                             wt[c_km:c_vm], wt[c_ka:]],
                            axis=0).astype(BF16)
    assert w_all.shape[0] == W_ROWS_T + W_ROWS_Q + M_QK_W + 2 * A_KV_W + 2 * D_MODEL
    head_of = jnp.arange(A_KV_W) // A_HEAD_DIM
    bd = (head_of[:, None] == head_of[None, :]).astype(BF16)
    qcol = (jnp.tile(q_norm_w[l], A_HEADS) * (A_HEAD_DIM ** -0.5)).reshape(A_Q_W, 1)
    krow = jnp.tile(k_norm_w[l], A_KV_HEADS).reshape(1, A_KV_W)
    nw_mix = norm_mix_w[l].reshape(1, D_MODEL)
    nw_ffn = norm_ffn_w[l].reshape(1, D_MODEL)
    bias_col = jnp.concatenate([mlstm_i_bias[l], mlstm_f_bias[l]]).reshape(2 * M_HEADS, 1)
    nw_col = mlstm_norm_w[l].reshape(M_V_W, 1)
    sinks = attn_sinks[l]
    proj = lambda x2d, tm: _proj(x2d, nw_mix, w_all, bd, qcol, krow, tm)

    xp = x_prompt.reshape(bp * tp, D_MODEL)
    later_ws = (w_branch_a[l], w_branch_b[l], w_out[l], w_gate[l], w_up[l], w_down[l])
    g_ab, h_m, h_a, kwin_p, vwin_p, st_p, m_p, wa, wb, wo, wg, wu, wd = _front(
        xp, bp, sinks, nw_mix, w_all, bd, qcol, krow, bias_col, nw_col, later_ws, TOKEN_TILE)
    merge = lambda x2d, h_m, h_a, g_ab, tm: _merge_ffn(x2d, h_m, h_a, g_ab, wa, wb, wo, nw_ffn, wg, wu, wd, tm)
    yp = merge(xp, h_m, h_a, g_ab, TOKEN_TILE).reshape(bp, tp, D_MODEL)
    c_p = jnp.swapaxes(st_p[:, :M_DV, :], 1, 2).reshape(bp, M_HEADS, M_DK, M_DV)
    n_p = st_p[:, M_DV, :].reshape(bp, M_HEADS, M_DK)
    m_pr = m_p[:, :M_HEADS, 0]

    ns = bs * ts
    xs = x_sample.reshape(ns, D_MODEL)
    tms = TOKEN_TILE if ns % TOKEN_TILE == 0 else GROUP
    qvot, gt, qat, k_m, kv_a, g_ab = proj(xs, tms)
    ngrp = ns // GROUP
    m_lanes = jnp.repeat(state_mlstm_m[l], ts, axis=0).reshape(ngrp, GROUP, M_HEADS)
    mrow = jnp.pad(jnp.swapaxes(m_lanes, 1, 2), ((0, 0), (0, SUBLANES - M_HEADS), (0, 0)))
    h_m, c_s, n_s, mt_s = _mlstm_sample(qvot, gt, k_m, mrow, bias_col, nw_col,
                                        state_mlstm_C[l].reshape(bs, M_QK_W, M_DV),
                                        state_mlstm_n[l].reshape(bs, M_QK_W), ts)
    wbuf = cache_swa_k.shape[2]
    to_fm = lambda a: jnp.transpose(a, (0, 2, 3, 1)).reshape(bs, A_KV_W, wbuf)
    from_fm = lambda a: jnp.transpose(a.reshape(bs, A_KV_HEADS, A_HEAD_DIM, wbuf), (0, 3, 1, 2))[None]
    h_a, kwin_s, vwin_s = _swa_sample(qat, kv_a, to_fm(cache_swa_k[l]), to_fm(cache_swa_v[l]), sinks, ts)
    ys = merge(xs, h_m, h_a, g_ab, tms).reshape(bs, ts, D_MODEL)
    m_s = jnp.swapaxes(mt_s[:, :M_HEADS, :], 1, 2).reshape(bs, ts, M_HEADS)[:, ts - 1, :]

    kv5 = lambda a: a.reshape(a.shape[0], a.shape[1], A_KV_HEADS, A_HEAD_DIM)[None]
    return (yp, ys,
            c_p[None], n_p[None], m_pr[None], kv5(kwin_p), kv5(vwin_p),
            c_s.reshape(bs, M_HEADS, M_DK, M_DV)[None], n_s.reshape(bs, M_HEADS, M_DK)[None], m_s[None],
            from_fm(kwin_s), from_fm(vwin_s))
```

```python
import functools

import jax
import jax.numpy as jnp
from jax import lax
from jax.experimental import pallas as pl
from jax.experimental.pallas import tpu as pltpu

F32 = jnp.float32
BF16 = jnp.bfloat16

D_MODEL = 1024
M_HEADS = 4
M_DK = 64
M_DV = 128
M_QK_W = M_HEADS * M_DK
M_V_W = M_HEADS * M_DV
A_HEADS = 8
A_KV_HEADS = 2
A_HEAD_DIM = 64
A_GROUPS = A_HEADS // A_KV_HEADS
A_Q_W = A_HEADS * A_HEAD_DIM
A_KV_W = A_KV_HEADS * A_HEAD_DIM
WINDOW = 128
EPS = 1e-6

LANES = 128
SUBLANES = 8
BF16_ROWS = 16
MXU_WIDTH = 256
FETCH_CHUNKS = 4
GROUP = 128
S_ROWS = M_DV + BF16_ROWS
TOKEN_TILE = 512
VMEM_LIMIT = 56 * 1024 * 1024
DK_SHIFT = M_DK.bit_length() - 1
WINDOW_SHIFT = WINDOW.bit_length() - 1
HALF = LANES // 2
assert A_HEAD_DIM == HALF and A_KV_W == LANES, "attention head pairs share one lane-width"

NT_DIMS = (((1,), (1,)), ((), ()))


def _dot(a, b):
    return jnp.dot(a, b, preferred_element_type=F32)


def _dot_nt(a, b):
    return lax.dot_general(a, b, NT_DIMS, preferred_element_type=F32)


def _const_spec(shape):
    nd = len(shape)
    return pl.BlockSpec(shape, lambda *_: (0,) * nd, pipeline_mode=pl.Buffered(1))


def _params(sem):
    return pltpu.CompilerParams(dimension_semantics=sem, vmem_limit_bytes=VMEM_LIMIT)


def _rms_rows(x, nw):
    ms = jnp.mean(x * x, axis=-1, keepdims=True)
    return (x * lax.rsqrt(ms + EPS)) * nw


GAB_CHUNK = 512


def _proj_chunks(x_ref, nw_ref, wmt_ref, wqgt_ref, wn_ref, bd_ref, qcol_ref, krow_ref,
                 qvot_ref, gt_ref, qat_ref, km_ref, kva_ref, gab_ref):
    hn = _rms_rows(x_ref[...], nw_ref[...]).astype(BF16)
    n_blk = qat_ref.shape[0]

    def put(ref, rows, val):
        for c in range(n_blk):
            ref[c, rows, :] = val[:, c * LANES:(c + 1) * LANES].astype(ref.dtype)

    def mlstm_qvo():
        put(qvot_ref, slice(None), _dot_nt(wmt_ref[...], hn))

    def attn_q_and_gates():
        qt = _dot_nt(wqgt_ref[...], hn)
        put(gt_ref, slice(None), qt[A_Q_W:A_Q_W + 2 * M_HEADS])
        for h in range(A_HEADS):
            hs = slice(h * A_HEAD_DIM, (h + 1) * A_HEAD_DIM)
            blk = qt[hs]
            ssq_q = jnp.sum(blk * blk, axis=0, keepdims=True)
            put(qat_ref, hs, (blk * lax.rsqrt(ssq_q * (1.0 / A_HEAD_DIM) + EPS)) * qcol_ref[hs])

    def mlstm_k():
        km_ref[...] = _dot_nt(hn, wn_ref[0:M_QK_W, :]).astype(km_ref.dtype)

    def branch_gates(c0):
        def run():
            w0 = M_QK_W + 2 * A_KV_W + c0
            gab_ref[:, c0:c0 + GAB_CHUNK] = _dot_nt(hn, wn_ref[w0:w0 + GAB_CHUNK, :]).astype(gab_ref.dtype)
        return run

    def attn_kv():
        kv = _dot_nt(hn, wn_ref[M_QK_W:M_QK_W + 2 * A_KV_W, :])
        k = kv[:, 0:A_KV_W]
        ksq = k * k
        hi = ksq.astype(BF16)
        lo = (ksq - hi.astype(F32)).astype(BF16)
        ssq = _dot(hi, bd_ref[...]) + _dot(lo, bd_ref[...])
        kva_ref[:, 0:A_KV_W] = (k * lax.rsqrt(ssq * (1.0 / A_HEAD_DIM) + EPS)) * krow_ref[...]
        kva_ref[:, A_KV_W:] = kv[:, A_KV_W:]

    return ([mlstm_qvo, attn_q_and_gates, mlstm_k]
            + [branch_gates(c0) for c0 in range(0, gab_ref.shape[1], GAB_CHUNK)] + [attn_kv])


W_ROWS_T = M_QK_W + 2 * M_V_W
W_ROWS_Q = A_Q_W + BF16_ROWS


def _weight_views(w_ref):
    return w_ref.at[0:W_ROWS_T], w_ref.at[W_ROWS_T:W_ROWS_T + W_ROWS_Q], w_ref.at[W_ROWS_T + W_ROWS_Q:]


def _proj_kernel(x_ref, nw_ref, w_ref, *rest):
    for piece in _proj_chunks(x_ref, nw_ref, *_weight_views(w_ref), *rest):
        piece()


def _proj(x2d, nw, w_all, bd, qcol, krow, tm):
    n = x2d.shape[0]
    row = lambda w: pl.BlockSpec((tm, w), lambda i: (i, 0))
    n_blk = tm // LANES
    slab = lambda r: pl.BlockSpec((n_blk, r, LANES), lambda i: (i, 0, 0))
    w_qvo = M_QK_W + 2 * M_V_W
    return pl.pallas_call(
        _proj_kernel,
        grid=(n // tm,),
        in_specs=[row(D_MODEL)] + [_const_spec(a.shape) for a in (nw, w_all, bd, qcol, krow)],
        out_specs=[slab(w_qvo), slab(2 * M_HEADS), slab(A_Q_W), row(M_QK_W), row(2 * A_KV_W), row(2 * D_MODEL)],
        out_shape=[jax.ShapeDtypeStruct((n // LANES, w_qvo, LANES), BF16),
                   jax.ShapeDtypeStruct((n // LANES, 2 * M_HEADS, LANES), F32),
                   jax.ShapeDtypeStruct((n // LANES, A_Q_W, LANES), BF16),
                   jax.ShapeDtypeStruct((n, M_QK_W), BF16),
                   jax.ShapeDtypeStruct((n, 2 * A_KV_W), F32),
                   jax.ShapeDtypeStruct((n, 2 * D_MODEL), BF16)],
        compiler_params=_params(("arbitrary",)),
        name="proj",
    )(x2d, nw, w_all, bd, qcol, krow)


def _split3_rows(x):
    hi = x.astype(BF16).astype(F32)
    r1 = x - hi
    mid = r1.astype(BF16).astype(F32)
    lo = r1 - mid
    return jnp.concatenate([hi, mid, lo], axis=0).astype(BF16)


def _log_sigmoid(x):
    return jnp.minimum(x, 0.0) - jnp.log1p(jnp.exp(-jnp.abs(x)))


def _chunk_masks(chunk_shift):
    s = lax.broadcasted_iota(jnp.int32, (GROUP, GROUP), 0)
    t = lax.broadcasted_iota(jnp.int32, (GROUP, GROUP), 1)
    same = (s >> chunk_shift) == (t >> chunk_shift)
    return same, same & (s <= t)


def _groups_gates(gts, same, causal):
    n = len(gts)
    assert 2 * M_HEADS == SUBLANES and n * SUBLANES <= GROUP
    cm_bf = jnp.where(causal, 1.0, 0.0).astype(BF16)
    lf = _log_sigmoid(jnp.concatenate(gts, axis=0))
    nr = n * SUBLANES
    bt3 = _dot(_split3_rows(lf), cm_bf)
    bt = (bt3[0:nr] + bt3[nr:2 * nr]) + bt3[2 * nr:3 * nr]
    b4 = [bt[g * SUBLANES + M_HEADS:(g + 1) * SUBLANES] for g in range(n)]
    a4 = [gts[g][0:M_HEADS] - b4[g] for g in range(n)]
    a_rows = [a for g in range(n) for a in (a4[g], a4[g])]
    if nr < GROUP:
        a_rows.append(jnp.zeros((GROUP - nr, GROUP), F32))
    a_cols = jnp.concatenate(a_rows, axis=0).T
    out = []
    for g in range(n):
        at_mats, run_rows, chunk_rows = [], [], []
        for h in range(M_HEADS):
            c = g * SUBLANES + h
            at = jnp.broadcast_to(a_cols[:, c:c + 1], (GROUP, GROUP))
            at_mats.append(at)
            run_rows.append(jnp.max(jnp.where(causal, at, -jnp.inf), axis=0, keepdims=True))
            chunk_rows.append(jnp.max(jnp.where(same, at, -jnp.inf), axis=0, keepdims=True))
        out.append((b4[g], a4[g], at_mats, jnp.concatenate(run_rows, axis=0), jnp.concatenate(chunk_rows, axis=0)))
    return out


def _group_weights(m_prev, b4, a4, run4, chunk4):
    big_m = jnp.maximum(m_prev, run4)
    m_last = jnp.maximum(m_prev, chunk4)
    w_inter = jnp.exp(m_prev - big_m)
    g_vec = jnp.exp(m_prev - m_last)
    w_last = jnp.exp(a4 - m_last)
    m_t = b4 + big_m
    return big_m, w_inter, g_vec, w_last, m_t, jnp.exp(-m_t)


def _group_scores(qvot, ks):
    lane_head = lax.broadcasted_iota(jnp.int32, (1, M_QK_W), 1) >> DK_SHIFT
    row_head = lax.broadcasted_iota(jnp.int32, (M_QK_W, 1), 0) >> DK_SHIFT
    qt = qvot(0, M_QK_W)
    k_stack = jnp.concatenate([jnp.where(lane_head == h, ks, jnp.zeros_like(ks)) for h in range(M_HEADS)], axis=0)
    qw = jnp.concatenate([jnp.where(row_head == h, qt, jnp.zeros_like(qt)) for h in range(M_HEADS)], axis=1)
    zero_blk = jnp.zeros((M_DK, GROUP), BF16)
    sc_t = []
    for h in range(0, M_HEADS, 2):
        q_pair = jnp.concatenate(
            [jnp.concatenate([qvot(h * M_DK, (h + 1) * M_DK), zero_blk], axis=1),
             jnp.concatenate([zero_blk, qvot((h + 1) * M_DK, (h + 2) * M_DK)], axis=1)], axis=0)
        sc = _dot(ks[:, h * M_DK:(h + 2) * M_DK], q_pair)
        sc_t += [sc[:, 0:GROUP], sc[:, GROUP:]]
    return k_stack, qw, sc_t


def _group_values(qvot, sc_t, at_mats, big_m, causal):
    ones_rows = jnp.where(lax.broadcasted_iota(jnp.int32, (BF16_ROWS, GROUP), 0) == 0, 1.0, 0.0).astype(BF16)
    zero_blk = jnp.zeros((GROUP, GROUP), BF16)
    vta, s_t = [], []
    for h in range(M_HEADS):
        w_t = jnp.where(causal, jnp.exp(at_mats[h] - big_m[h:h + 1]), 0.0)
        s_t.append((sc_t[h] * w_t).astype(BF16))
        vta.append(jnp.concatenate([qvot(M_QK_W + h * M_DV, M_QK_W + (h + 1) * M_DV), ones_rows], axis=0))
    intra = []
    for h in range(0, M_HEADS, 2):
        pair = jnp.concatenate([jnp.concatenate([s_t[h], zero_blk], axis=1),
                                jnp.concatenate([zero_blk, s_t[h + 1]], axis=1)], axis=0)
        intra.append(_dot(jnp.concatenate(vta[h:h + 2], axis=1), pair))
    return vta, jnp.concatenate(intra, axis=1)


def _weighted_values(vta, w_rows):
    return jnp.concatenate([(vta[h].astype(F32) * w_rows[h:h + 1]).astype(BF16) for h in range(M_HEADS)], axis=1)


def _lanes_x(rows):
    return jnp.concatenate([rows[h:h + 1] for h in range(M_HEADS)], axis=1)


def _group_out(qvot, inter, intra, w_inter, e_neg_m, nw_ref):
    outs = []
    out_all = inter * _lanes_x(w_inter) + intra
    for h in range(M_HEADS):
        out_t = out_all[:, h * GROUP:(h + 1) * GROUP]
        hh = out_t[0:M_DV] / jnp.maximum(jnp.abs(out_t[M_DV:M_DV + 1]), e_neg_m[h:h + 1])
        ms = jnp.mean(hh * hh, axis=0, keepdims=True)
        hn = (hh * lax.rsqrt(ms + EPS)) * nw_ref[h * M_DV:(h + 1) * M_DV]
        o_t = qvot(M_QK_W + M_V_W + h * M_DV, M_QK_W + M_V_W + (h + 1) * M_DV)
        outs.append(hn * jax.nn.sigmoid(o_t.astype(F32)))
    return jnp.concatenate(outs, axis=0).T


def _decay_row(g_vec, lane0):
    lane_head = lax.broadcasted_iota(jnp.int32, (1, M_QK_W), 1) >> DK_SHIFT
    g_row = jnp.zeros((1, M_QK_W), F32)
    for h in range(M_HEADS):
        g_row = jnp.where(lane_head == h, g_vec[h:h + 1, lane0:lane0 + 1], g_row)
    return g_row


def _mlstm_sample_kernel(seq_len, qvot_ref, gt_ref, k_ref, mrow_ref, bias_ref, nw_ref, c_ref, n_ref,
                         h_ref, c_out_ref, n_out_ref, mt_ref):
    n_seq = GROUP // seq_len
    shift = seq_len.bit_length() - 1
    same, causal = _chunk_masks(shift)
    lane_seq = lax.broadcasted_iota(jnp.int32, (1, GROUP), 1) >> shift
    lane_seq_x = jnp.concatenate([lane_seq] * M_HEADS, axis=1)
    row0 = lax.broadcasted_iota(jnp.int32, (BF16_ROWS, M_QK_W), 0) == 0
    qvot = lambda r0, r1: qvot_ref[0, r0:r1, :]
    ks = k_ref[...] * (M_DK ** -0.5)
    (b4, a4, at_mats, run4, chunk4), = _groups_gates([gt_ref[0] + bias_ref[...]], same, causal)
    big_m, w_inter, g_vec, w_last, m_t, e_neg_m = _group_weights(mrow_ref[0, 0:M_HEADS, :], b4, a4, run4, chunk4)
    mt_ref[0] = jnp.concatenate([m_t, m_t], axis=0)
    k_stack, qw, sc_t = _group_scores(qvot, ks)

    st_old = []
    for s in range(n_seq):
        n_rows = jnp.where(row0, jnp.broadcast_to(n_ref[s:s + 1, :], (BF16_ROWS, M_QK_W)), 0.0)
        st_old.append(jnp.concatenate([c_ref[s].T, n_rows], axis=0))
    inter_all = _dot(jnp.concatenate(st_old, axis=0).astype(BF16), qw)
    vta, intra = _group_values(qvot, sc_t, at_mats, big_m, causal)
    inter = inter_all[0:S_ROWS]
    for s in range(1, n_seq):
        inter = jnp.where(lane_seq_x == s, inter_all[s * S_ROWS:(s + 1) * S_ROWS], inter)
    tall = jnp.concatenate([_weighted_values(vta, jnp.where(lane_seq == s, w_last, 0.0)) for s in range(n_seq)],
                           axis=0)
    d_st = _dot(tall, k_stack)
    for s in range(n_seq):
        st_new = _decay_row(g_vec, s * seq_len) * st_old[s] + d_st[s * S_ROWS:(s + 1) * S_ROWS]
        c_out_ref[s] = st_new[0:M_DV].T
        n_out_ref[s:s + 1, :] = st_new[M_DV:M_DV + 1]
    h_ref[...] = _group_out(qvot, inter, intra, w_inter, e_neg_m, nw_ref).astype(h_ref.dtype)


def _mlstm_sample(qvot, gt, k_m, mrow, bias_col, nw_col, c, n, seq_len):
    ngrp = qvot.shape[0]
    n_seq = GROUP // seq_len
    full = lambda a: pl.BlockSpec(a.shape, lambda i: (0,) * a.ndim)
    slab = lambda a: pl.BlockSpec((1,) + a.shape[1:], lambda i: (i, 0, 0))
    row = lambda w: pl.BlockSpec((GROUP, w), lambda i: (i, 0))
    c_spec = pl.BlockSpec((n_seq,) + c.shape[1:], lambda i: (i, 0, 0))
    n_spec = pl.BlockSpec((n_seq, n.shape[1]), lambda i: (i, 0))
    return pl.pallas_call(
        functools.partial(_mlstm_sample_kernel, seq_len),
        grid=(ngrp,),
        in_specs=[slab(qvot), slab(gt), row(M_QK_W), slab(mrow), full(bias_col), full(nw_col), c_spec, n_spec],
        out_specs=[row(M_V_W), c_spec, n_spec, slab(mrow)],
        out_shape=[jax.ShapeDtypeStruct((ngrp * GROUP, M_V_W), BF16),
                   jax.ShapeDtypeStruct(c.shape, F32),
                   jax.ShapeDtypeStruct(n.shape, F32),
                   jax.ShapeDtypeStruct(mrow.shape, F32)],
        compiler_params=_params(("arbitrary",)),
        name="mlstm_sample",
    )(qvot, gt, k_m, mrow, bias_col, nw_col, c, n)


def _swa_scores(qbs, has_prev, sink_ref, qt_ref, kv_ref, kvp_ref):
    hd = A_HEAD_DIM
    nq = A_GROUPS * WINDOW
    si = lax.broadcasted_iota(jnp.int32, (2 * WINDOW, nq), 0)
    qi = lax.broadcasted_iota(jnp.int32, (2 * WINDOW, nq), 1) & (WINDOW - 1)
    local = ((si < WINDOW) & (si > qi)) | ((si >= WINDOW) & (si - WINDOW <= qi))
    first = local & (has_prev | (si >= WINDOW))
    lane_grp = lax.broadcasted_iota(jnp.int32, (1, nq), 1) >> WINDOW_SHIFT
    kv_block = lambda i: kvp_ref[...] if i == 0 else kv_ref[(i - 1) * WINDOW:i * WINDOW, :]
    k_bf = lambda i: kv_block(i)[:, 0:A_KV_W].astype(BF16)
    vt_bf = lambda i: kv_block(i)[:, A_KV_W:].T.astype(BF16)
    zeros = jnp.zeros((hd, nq), BF16)
    sinks = []
    for kvh in range(A_KV_HEADS):
        sk = jnp.zeros((1, nq), F32)
        for g in range(A_GROUPS):
            sk = jnp.where(lane_grp == g, sink_ref[kvh * A_GROUPS + g], sk)
        sinks.append(sk)
    units = []
    for qb in qbs:
        kk = jnp.concatenate([k_bf(qb), k_bf(qb + 1)], axis=0)
        vt = jnp.concatenate([vt_bf(qb), vt_bf(qb + 1)], axis=1)
        mask = local if qb > 0 else first
        for kvh in range(A_KV_HEADS):
            q4t = jnp.concatenate(
                [qt_ref[qb, (kvh * A_GROUPS + g) * hd:(kvh * A_GROUPS + g + 1) * hd, :] for g in range(A_GROUPS)],
                axis=1)
            wq = jnp.concatenate([q4t, zeros] if kvh == 0 else [zeros, q4t], axis=0)
            units.append((jnp.where(mask, _dot(kk, wq), -jnp.inf), sinks[kvh], vt))
    return units


def _swa_finish(qbs, units, h_ref):
    hd = A_HEAD_DIM
    for i, qb in enumerate(qbs):
        pieces = []
        for kvh in range(A_KV_HEADS):
            s, sk, vt = units[i * A_KV_HEADS + kvh]
            mx = jnp.maximum(jnp.max(s, axis=0, keepdims=True), sk)
            p = jnp.exp(s - mx)
            den = jnp.sum(p, axis=0, keepdims=True) + jnp.exp(sk - mx)
            ot = _dot(vt[kvh * hd:(kvh + 1) * hd], p.astype(BF16)) / den
            pieces += [ot[:, g * WINDOW:(g + 1) * WINDOW] for g in range(A_GROUPS)]
        h_t = jnp.concatenate(pieces, axis=0)
        h_ref[qb * WINDOW:(qb + 1) * WINDOW, :] = h_t.T.astype(h_ref.dtype)


def _front_kernel(tiles_per_seq, n_cast, *refs):
    (sink_ref, x_ref, nw_ref, w_ref, bd_ref, qcol_ref, krow_ref, bias_ref, nwm_ref) = refs[:9]
    wmt_ref, wqgt_ref, wn_ref = _weight_views(w_ref)
    cast_in = refs[9:9 + n_cast]
    (gab_ref, hm_ref, ha_ref, kwin_ref, vwin_ref, st_ref, m_ref) = refs[9 + n_cast:16 + n_cast]
    cast_out = refs[16 + n_cast:16 + 2 * n_cast]
    set_a, set_b = refs[16 + 2 * n_cast:21 + 2 * n_cast], refs[21 + 2 * n_cast:26 + 2 * n_cast]
    st_s, m_s, kvp_s = refs[26 + 2 * n_cast:]
    odd = (pl.program_id(0) & 1) == 1
    for parity, q_set, p_set in ((jnp.logical_not(odd), set_a, set_b), (odd, set_b, set_a)):
        pl.when(parity)(functools.partial(
            _front_body, tiles_per_seq, sink_ref, x_ref, nw_ref, wmt_ref, wqgt_ref, wn_ref, bd_ref, qcol_ref,
            krow_ref, bias_ref, nwm_ref, gab_ref, hm_ref, ha_ref, kwin_ref, vwin_ref, st_ref, m_ref,
            *q_set, *p_set, st_s, m_s, kvp_s))
    for src, dst in zip(cast_in, cast_out):
        dst[...] = src[...].astype(dst.dtype)


def _front_body(tiles_per_seq, sink_ref, x_ref, nw_ref, wmt_ref, wqgt_ref, wn_ref, bd_ref, qcol_ref, krow_ref,
                bias_ref, nwm_ref,
                gab_ref, hm_ref, ha_ref, kwin_ref, vwin_ref, st_ref, m_ref,
                q_qvot, q_gt, q_qat, q_km, q_kva, p_qvot, p_gt, p_qat, p_km, p_kva, st_s, m_s, kvp_s):
    k = pl.program_id(0)

    @pl.when(k == 0)
    def _():
        for r in (p_qvot, p_gt, p_qat, p_km, p_kva, st_s, m_s, kvp_s):
            r[...] = jnp.zeros(r.shape, r.dtype)

    pieces = _proj_chunks(x_ref, nw_ref, wmt_ref, wqgt_ref, wn_ref, bd_ref, qcol_ref, krow_ref,
                          q_qvot, q_gt, q_qat, q_km, q_kva, gab_ref)

    seq_start = lax.rem(k - 1 + tiles_per_seq, tiles_per_seq) == 0
    n_blk = p_qat.shape[0]
    same, causal = _chunk_masks(GROUP.bit_length() - 1)
    st = jnp.where(seq_start, 0.0, st_s[...])
    m_col = jnp.where(seq_start, 0.0, m_s[0:M_HEADS, 0:1])

    qvots = [lambda r0, r1, g=g: p_qvot[g, r0:r1, :] for g in range(n_blk)]
    scores = [_group_scores(qvots[g], p_km[g * GROUP:(g + 1) * GROUP, :] * (M_DK ** -0.5))
              for g in range(n_blk)]
    inter = [_dot(st.astype(BF16), scores[0][1])]
    gates = _groups_gates([p_gt[g] + bias_ref[...] for g in range(n_blk)], same, causal)
    for piece in pieces[0:1]:
        piece()
    grp = []
    for g in range(n_blk):
        b4, a4, at_mats, run4, chunk4 = gates[g]
        weights = _group_weights(jnp.broadcast_to(m_col, (M_HEADS, GROUP)), b4, a4, run4, chunk4)
        m_col = weights[4][:, GROUP - 1:GROUP]
        grp.append((qvots[g], at_mats, weights) + scores[g])
    has_prev = jnp.logical_not(seq_start)
    qb_lo, qb_hi = list(range(0, n_blk // 2)), list(range(n_blk // 2, n_blk))
    units_lo = _swa_scores(qb_lo, has_prev, sink_ref, p_qat, p_kva, kvp_s)
    for piece in pieces[1:3]:
        piece()

    vals = []

    def group_values(g):
        qvot, at_mats, (big_m, _, _, w_last, _, _), k_stack, _, sc_t = grp[g]
        vta, intra = _group_values(qvot, sc_t, at_mats, big_m, causal)
        vals.append((intra, _dot(_weighted_values(vta, w_last), k_stack)))

    for g in range(0, n_blk // 2):
        group_values(g)
    _swa_finish(qb_lo[:1], units_lo[:A_KV_HEADS], ha_ref)
    for piece in pieces[3:4]:
        piece()
    for g in range(n_blk // 2, n_blk):
        group_values(g)
    _swa_finish(qb_lo[1:], units_lo[A_KV_HEADS:], ha_ref)
    units_hi = _swa_scores(qb_hi, has_prev, sink_ref, p_qat, p_kva, kvp_s)
    for piece in pieces[4:6]:
        piece()

    for g in range(n_blk):
        st = _decay_row(grp[g][2][2], 0) * st + vals[g][1]
        if g + 1 < n_blk:
            inter.append(_dot(st.astype(BF16), grp[g + 1][4]))
        if g == n_blk // 2 - 1:
            _swa_finish(qb_hi[:1], units_hi[:A_KV_HEADS], ha_ref)
            for piece in pieces[6:7]:
                piece()
    _swa_finish(qb_hi[1:], units_hi[A_KV_HEADS:], ha_ref)
    for piece in pieces[7:]:
        piece()

    for g in range(n_blk):
        qvot, _, (_, w_inter, _, _, _, e_neg_m) = grp[g][0:3]
        hm_ref[g * GROUP:(g + 1) * GROUP, :] = _group_out(qvot, inter[g], vals[g][0], w_inter, e_neg_m,
                                                          nwm_ref).astype(hm_ref.dtype)
    st_s[...] = st
    st_ref[0] = st
    m_rows = jnp.broadcast_to(m_col, (M_HEADS, LANES))
    m_rows = jnp.concatenate([m_rows, m_rows], axis=0)
    m_s[...] = m_rows
    m_ref[0] = m_rows
    tm = p_kva.shape[0]
    kwin_ref[0] = p_kva[tm - WINDOW:, 0:A_KV_W]
    vwin_ref[0] = p_kva[tm - WINDOW:, A_KV_W:]
    kvp_s[...] = p_kva[tm - WINDOW:, :]


def _cast_rows(rows, n_steps):
    rb = BF16_ROWS
    while rows % rb or rows // rb > n_steps:
        rb += BF16_ROWS
    return rb


def _front(x2d, nb, sinks, nw, w_all, bd, qcol, krow, bias_col, nwm_col, cast_ws, tm):
    n = x2d.shape[0]
    n_tiles = n // tm
    tps = n_tiles // nb
    n_blk = tm // LANES
    w_qvo = M_QK_W + 2 * M_V_W
    cur = lambda w: pl.BlockSpec((tm, w), lambda k: (jnp.minimum(k, n_tiles - 1), 0))
    prev = lambda w: pl.BlockSpec((tm, w), lambda k: (jnp.maximum(k - 1, 0), 0))
    per_seq = lambda r, w: pl.BlockSpec((1, r, w), lambda k: (jnp.maximum(k - 1, 0) // tps, 0, 0))

    def cast_spec(a):
        rb = _cast_rows(a.shape[0], n_tiles)
        return pl.BlockSpec((rb, a.shape[1]), lambda k: (jnp.minimum(k, a.shape[0] // rb - 1), 0))

    cast_in_specs = [cast_spec(a) for a in cast_ws]
    cast_out_specs = [cast_spec(a) for a in cast_ws]
    proj_scratch = [pltpu.VMEM((n_blk, w_qvo, LANES), BF16), pltpu.VMEM((n_blk, 2 * M_HEADS, LANES), F32),
                    pltpu.VMEM((n_blk, A_Q_W, LANES), BF16), pltpu.VMEM((tm, M_QK_W), BF16),
                    pltpu.VMEM((tm, 2 * A_KV_W), F32)]
    return pl.pallas_call(
        functools.partial(_front_kernel, tps, len(cast_ws)),
        grid=(n_tiles + 1,),
        in_specs=[pl.BlockSpec(memory_space=pltpu.SMEM), cur(D_MODEL)]
                 + [_const_spec(a.shape) for a in (nw, w_all, bd, qcol, krow, bias_col, nwm_col)]
                 + cast_in_specs,
        out_specs=[cur(2 * D_MODEL), prev(M_V_W), prev(A_Q_W), per_seq(WINDOW, A_KV_W), per_seq(WINDOW, A_KV_W),
                   per_seq(S_ROWS, M_QK_W), per_seq(SUBLANES, LANES)] + cast_out_specs,
        out_shape=[jax.ShapeDtypeStruct((n, 2 * D_MODEL), BF16),
                   jax.ShapeDtypeStruct((n, M_V_W), BF16),
                   jax.ShapeDtypeStruct((n, A_Q_W), BF16),
                   jax.ShapeDtypeStruct((nb, WINDOW, A_KV_W), F32),
                   jax.ShapeDtypeStruct((nb, WINDOW, A_KV_W), F32),
                   jax.ShapeDtypeStruct((nb, S_ROWS, M_QK_W), F32),
                   jax.ShapeDtypeStruct((nb, SUBLANES, LANES), F32)]
                  + [jax.ShapeDtypeStruct(a.shape, BF16) for a in cast_ws],
        scratch_shapes=proj_scratch + proj_scratch + [pltpu.VMEM((S_ROWS, M_QK_W), F32),
                                                       pltpu.VMEM((SUBLANES, LANES), F32),
                                                       pltpu.VMEM((WINDOW, 2 * A_KV_W), F32)],
        compiler_params=_params(("arbitrary",)),
        name="front",
    )(sinks, x2d, nw, w_all, bd, qcol, krow, bias_col, nwm_col, *cast_ws)


def _swa_sample_kernel(seq_len, sink_ref, qt_ref, kv_ref, ck_ref, cv_ref, h_ref, kwin_ref, vwin_ref):
    n_seq = GROUP // seq_len
    wb = ck_ref.shape[2]
    lane = lax.broadcasted_iota(jnp.int32, (1, LANES), 1)
    lo_half = lane < HALF
    q_rows = qt_ref[0].astype(F32).T
    kv_new = kv_ref[...]
    k_new = kv_new[:, 0:A_KV_W].reshape(n_seq, seq_len, A_KV_W)
    v_new = kv_new[:, A_KV_W:].reshape(n_seq, seq_len, A_KV_W)

    def to_kv_half(x, head):
        kvh = head // A_GROUPS
        if head % 2 != kvh:
            x = pltpu.roll(x, HALF, axis=1)
        return jnp.where(lo_half if kvh == 0 else ~lo_half, x, 0.0)

    lhs = jnp.concatenate(
        [to_kv_half(q_rows[:, (h // 2) * LANES:(h // 2 + 1) * LANES], h).reshape(n_seq, seq_len, LANES)
         for h in range(A_HEADS)], axis=1).astype(BF16)
    zpad = jnp.zeros((n_seq, BF16_ROWS - seq_len, A_KV_W), F32)
    k_nb = jnp.concatenate([k_new, zpad], axis=1).astype(BF16)
    v_nb = jnp.concatenate([v_new, zpad], axis=1).astype(BF16)
    s_c = jnp.einsum('sqf,sfk->sqk', lhs, ck_ref[...].astype(BF16), preferred_element_type=F32)
    s_n = jnp.einsum('sqf,skf->sqk', lhs, k_nb, preferred_element_type=F32)
    nrow = A_HEADS * seq_len
    ti = lax.broadcasted_iota(jnp.int32, (nrow, wb), 0) & (seq_len - 1)
    ki = lax.broadcasted_iota(jnp.int32, (nrow, wb), 1)
    mask_c = (ti + wb - ki) < WINDOW
    ti_n = lax.broadcasted_iota(jnp.int32, (nrow, BF16_ROWS), 0) & (seq_len - 1)
    ki_n = lax.broadcasted_iota(jnp.int32, (nrow, BF16_ROWS), 1)
    mask_n = ki_n <= ti_n
    row_head = lax.broadcasted_iota(jnp.int32, (nrow, 1), 0) >> (seq_len.bit_length() - 1)
    sk = jnp.zeros((nrow, 1), F32)
    for h in range(A_HEADS):
        sk = jnp.where(row_head == h, sink_ref[h], sk)
    s_c = jnp.where(mask_c, s_c, -jnp.inf)
    s_n = jnp.where(mask_n, s_n, -jnp.inf)
    mx = jnp.maximum(jnp.maximum(jnp.max(s_c, axis=-1, keepdims=True), jnp.max(s_n, axis=-1, keepdims=True)), sk)
    p_c = jnp.exp(s_c - mx)
    p_n = jnp.exp(s_n - mx)
    den = jnp.sum(p_c, axis=-1, keepdims=True) + jnp.sum(p_n, axis=-1, keepdims=True) + jnp.exp(sk - mx)
    o = (jnp.einsum('sqk,sfk->sqf', p_c.astype(BF16), cv_ref[...].astype(BF16), preferred_element_type=F32)
         + jnp.einsum('sqk,skf->sqf', p_n.astype(BF16), v_nb, preferred_element_type=F32)) / den

    def from_kv_half(head):
        x = o[:, head * seq_len:(head + 1) * seq_len, :].reshape(GROUP, LANES)
        return pltpu.roll(x, HALF, axis=1) if head % 2 != head // A_GROUPS else x

    for c in range(A_HEADS // 2):
        h_ref[:, c * LANES:(c + 1) * LANES] = jnp.where(lo_half, from_kv_half(2 * c),
                                                        from_kv_half(2 * c + 1)).astype(h_ref.dtype)

    kt_new, vt_new = kv_new[:, 0:A_KV_W].T, kv_new[:, A_KV_W:].T
    for s in range(n_seq):
        put = (wb - seq_len - s * seq_len) % LANES
        for new_t, c_ref, win_ref in ((kt_new, ck_ref, kwin_ref), (vt_new, cv_ref, vwin_ref)):
            win_ref[s] = jnp.where(lane >= wb - seq_len, pltpu.roll(new_t, put, axis=1) if put else new_t,
                                   pltpu.roll(c_ref[s], wb - seq_len, axis=1))


def _swa_sample(qat, kv_a, cache_k, cache_v, sinks, seq_len):
    ngrp = qat.shape[0]
    n_seq = GROUP // seq_len
    wb = cache_k.shape[2]
    assert wb == LANES, "window positions fill one lane-width"
    row = lambda w: pl.BlockSpec((GROUP, w), lambda i: (i, 0))
    cache = pl.BlockSpec((n_seq, A_KV_W, wb), lambda i: (i, 0, 0))
    return pl.pallas_call(
        functools.partial(_swa_sample_kernel, seq_len),
        grid=(ngrp,),
        in_specs=[pl.BlockSpec(memory_space=pltpu.SMEM), pl.BlockSpec((1, A_Q_W, LANES), lambda i: (i, 0, 0)),
                  row(2 * A_KV_W), cache, cache],
        out_specs=[row(A_Q_W), cache, cache],
        out_shape=[jax.ShapeDtypeStruct((ngrp * GROUP, A_Q_W), BF16),
                   jax.ShapeDtypeStruct(cache_k.shape, F32),
                   jax.ShapeDtypeStruct(cache_v.shape, F32)],
        compiler_params=_params(("arbitrary",)),
        name="swa_sample",
    )(sinks, qat, kv_a, cache_k, cache_v)


def _merge_ffn_kernel(x_ref, hm_ref, ha_ref, gab_ref, wa_ref, wb_ref, wo_ref, nw_ref, wg_hbm, wu_hbm, wd_hbm,
                      y_ref, wg_ref, wu_ref, wd_ref, sems):
    cuts = _fetch_cuts(wg_ref.shape[1])
    pairs = [(src.at[:, c0:c1], dst.at[:, c0:c1]) for c0, c1 in zip(cuts[:-1], cuts[1:])
             for src, dst in ((wg_hbm, wg_ref), (wu_hbm, wu_ref))] + [(wd_hbm, wd_ref)]
    copies = [pltpu.make_async_copy(src, dst, sems.at[j]) for j, (src, dst) in enumerate(pairs)]
    first = pl.program_id(0) == 0
    for cond, fetch in ((first, copies), (jnp.logical_not(first), None)):
        pl.when(cond)(functools.partial(_merge_ffn_body, x_ref, hm_ref, ha_ref, gab_ref, wa_ref, wb_ref, wo_ref,
                                        nw_ref, wg_ref, wu_ref, wd_ref, y_ref, fetch))


def _fetch_cuts(cols):
    units = cols // MXU_WIDTH
    assert units * MXU_WIDTH == cols
    return [units * j // FETCH_CHUNKS * MXU_WIDTH for j in range(FETCH_CHUNKS + 1)]


def _merge_ffn_body(x_ref, hm_ref, ha_ref, gab_ref, wa_ref, wb_ref, wo_ref, nw_ref, wg_ref, wu_ref, wd_ref, y_ref,
                    fetch):
    if fetch:
        for j, c in enumerate(fetch):
            c.start(priority=j % 2)
    tm = x_ref.shape[0]
    halves = [slice(0, tm // 2), slice(tm // 2, tm)]
    mix = []
    for rs in halves:
        ga = jax.nn.sigmoid(gab_ref[rs, 0:D_MODEL].astype(F32))
        gb = jax.nn.sigmoid(gab_ref[rs, D_MODEL:].astype(F32))
        mix.append(ga * _dot(hm_ref[rs, :], wa_ref[...]) + gb * _dot(ha_ref[rs, :], wb_ref[...]))
    x1 = [x_ref[rs, :] + _dot(m.astype(BF16), wo_ref[...]) for rs, m in zip(halves, mix)]
    hf = [_rms_rows(v, nw_ref[...]).astype(BF16) for v in x1]
    if fetch:
        cuts = _fetch_cuts(wg_ref.shape[1])
        parts = []
        for j, (c0, c1) in enumerate(zip(cuts[:-1], cuts[1:])):
            fetch[2 * j].wait()
            fetch[2 * j + 1].wait()
            parts.append([(jax.nn.silu(_dot(h, wg_ref[:, c0:c1])) * _dot(h, wu_ref[:, c0:c1])).astype(BF16)
                          for h in hf])
        act = [jnp.concatenate(half_parts, axis=1) for half_parts in zip(*parts)]
        fetch[-1].wait()
    else:
        act = [(jax.nn.silu(_dot(h, wg_ref[...])) * _dot(h, wu_ref[...])).astype(BF16) for h in hf]
    for rs, v, a in zip(halves, x1, act):
        y_ref[rs, :] = v + _dot(a, wd_ref[...])


def _merge_ffn(x2d, h_m, h_a, g_ab, wa, wb, wo, nw, wg, wu, wd, tm):
    n = x2d.shape[0]
    row = lambda w: pl.BlockSpec((tm, w), lambda i: (i, 0))
    fetched = (wg, wu, wd)
    return pl.pallas_call(
        _merge_ffn_kernel,
        grid=(n // tm,),
        in_specs=[row(D_MODEL), row(M_V_W), row(A_Q_W), row(2 * D_MODEL)]
                 + [_const_spec(w.shape) for w in (wa, wb, wo, nw)]
                 + [pl.BlockSpec(memory_space=pl.ANY) for _ in fetched],
        out_specs=row(D_MODEL),
        out_shape=jax.ShapeDtypeStruct((n, D_MODEL), F32),
        scratch_shapes=[pltpu.VMEM(w.shape, w.dtype) for w in fetched]
                       + [pltpu.SemaphoreType.DMA((2 * FETCH_CHUNKS + 1,))],
        compiler_params=_params(("arbitrary",)),
        name="merge_ffn",
    )(x2d, h_m, h_a, g_ab, wa, wb, wo, nw, wg, wu, wd)


def kernel(x_prompt, x_sample, state_mlstm_C, state_mlstm_n, state_mlstm_m, cache_swa_k, cache_swa_v,
           norm_mix_w, w_in, mlstm_i_bias, mlstm_f_bias, mlstm_norm_w, q_norm_w, k_norm_w, attn_sinks,
           w_branch_a, w_branch_b, w_out, norm_ffn_w, w_gate, w_up, w_down):
    depth = w_in.shape[0]
    assert depth == 1, "single trunk layer"
    l = 0
    bp, tp = x_prompt.shape[0], x_prompt.shape[1]
    bs, ts = x_sample.shape[0], x_sample.shape[1]
    assert tp % TOKEN_TILE == 0 and (bs * ts) % GROUP == 0 and GROUP % ts == 0 and ts & (ts - 1) == 0
    assert ts <= SUBLANES, "sample chunk must fit one sublane tile"

    wt = jnp.transpose(w_in[l])
    c_km, c_vm = M_QK_W, 2 * M_QK_W
    c_g = 2 * M_QK_W + 2 * M_V_W
    c_qa = c_g + 2 * M_HEADS
    c_ka = c_qa + A_Q_W
    gate_pad = jnp.zeros((BF16_ROWS - 2 * M_HEADS, D_MODEL), F32)
    w_all = jnp.concatenate([wt[0:c_km], wt[c_vm:c_g],
                             wt[c_qa:c_ka], wt[c_g:c_qa], gate_pad,
                             wt[c_km:c_vm], wt[c_ka:]],
                            axis=0).astype(BF16)
    assert w_all.shape[0] == W_ROWS_T + W_ROWS_Q + M_QK_W + 2 * A_KV_W + 2 * D_MODEL
    head_of = jnp.arange(A_KV_W) // A_HEAD_DIM
    bd = (head_of[:, None] == head_of[None, :]).astype(BF16)
    qcol = (jnp.tile(q_norm_w[l], A_HEADS) * (A_HEAD_DIM ** -0.5)).reshape(A_Q_W, 1)
    krow = jnp.tile(k_norm_w[l], A_KV_HEADS).reshape(1, A_KV_W)
    nw_mix = norm_mix_w[l].reshape(1, D_MODEL)
    nw_ffn = norm_ffn_w[l].reshape(1, D_MODEL)
    bias_col = jnp.concatenate([mlstm_i_bias[l], mlstm_f_bias[l]]).reshape(2 * M_HEADS, 1)
    nw_col = mlstm_norm_w[l].reshape(M_V_W, 1)
    sinks = attn_sinks[l]
    proj = lambda x2d, tm: _proj(x2d, nw_mix, w_all, bd, qcol, krow, tm)

    xp = x_prompt.reshape(bp * tp, D_MODEL)
    later_ws = (w_branch_a[l], w_branch_b[l], w_out[l], w_gate[l], w_up[l], w_down[l])
    g_ab, h_m, h_a, kwin_p, vwin_p, st_p, m_p, wa, wb, wo, wg, wu, wd = _front(
        xp, bp, sinks, nw_mix, w_all, bd, qcol, krow, bias_col, nw_col, later_ws, TOKEN_TILE)
    merge = lambda x2d, h_m, h_a, g_ab, tm: _merge_ffn(x2d, h_m, h_a, g_ab, wa, wb, wo, nw_ffn, wg, wu, wd, tm)
    yp = merge(xp, h_m, h_a, g_ab, TOKEN_TILE).reshape(bp, tp, D_MODEL)
    c_p = jnp.swapaxes(st_p[:, :M_DV, :], 1, 2).reshape(bp, M_HEADS, M_DK, M_DV)
    n_p = st_p[:, M_DV, :].reshape(bp, M_HEADS, M_DK)
    m_pr = m_p[:, :M_HEADS, 0]

    ns = bs * ts
    xs = x_sample.reshape(ns, D_MODEL)
    tms = TOKEN_TILE if ns % TOKEN_TILE == 0 else GROUP
    qvot, gt, qat, k_m, kv_a, g_ab = proj(xs, tms)
    ngrp = ns // GROUP
    m_lanes = jnp.repeat(state_mlstm_m[l], ts, axis=0).reshape(ngrp, GROUP, M_HEADS)
    mrow = jnp.pad(jnp.swapaxes(m_lanes, 1, 2), ((0, 0), (0, SUBLANES - M_HEADS), (0, 0)))
    h_m, c_s, n_s, mt_s = _mlstm_sample(qvot, gt, k_m, mrow, bias_col, nw_col,
                                        state_mlstm_C[l].reshape(bs, M_QK_W, M_DV),
                                        state_mlstm_n[l].reshape(bs, M_QK_W), ts)
    wbuf = cache_swa_k.shape[2]
    to_fm = lambda a: jnp.transpose(a, (0, 2, 3, 1)).reshape(bs, A_KV_W, wbuf)
    from_fm = lambda a: jnp.transpose(a.reshape(bs, A_KV_HEADS, A_HEAD_DIM, wbuf), (0, 3, 1, 2))[None]
    h_a, kwin_s, vwin_s = _swa_sample(qat, kv_a, to_fm(cache_swa_k[l]), to_fm(cache_swa_v[l]), sinks, ts)
    ys = merge(xs, h_m, h_a, g_ab, tms).reshape(bs, ts, D_MODEL)
    m_s = jnp.swapaxes(mt_s[:, :M_HEADS, :], 1, 2).reshape(bs, ts, M_HEADS)[:, ts - 1, :]

    kv5 = lambda a: a.reshape(a.shape[0], a.shape[1], A_KV_HEADS, A_HEAD_DIM)[None]
    return (yp, ys,
            c_p[None], n_p[None], m_pr[None], kv5(kwin_p), kv5(vwin_p),
            c_s.reshape(bs, M_HEADS, M_DK, M_DV)[None], n_s.reshape(bs, M_HEADS, M_DK)[None], m_s[None],
            from_fm(kwin_s), from_fm(vwin_s))
```
